```python
import jax, jax.numpy as jnp
from jax import lax
import numpy as np

D_MODEL = 2048
BATCH = 8
SEQ = 8192
DEPTH = 1

CHUNK = 64
D_MIX = D_MODEL
D_CONV = D_MIX // 2
CONV_GROUPS = 8
CONV_WIDTH = 3
D_GLA_V = D_MIX - D_CONV
GLA_HEADS = 4
GLA_DV = D_GLA_V // GLA_HEADS
GLA_DK = GLA_DV // 2
D_GLA_K = GLA_HEADS * GLA_DK
GATE_RANK = 16
GATE_NORMALIZER = 16.0
D_FF = 4 * D_MODEL
EPS = 1e-6
IN_SIZES = (D_CONV, D_CONV, D_CONV, D_GLA_K, D_GLA_K, D_GLA_V, D_GLA_V, GATE_RANK)
D_IN = sum(IN_SIZES)

kernel_name = "hymba_conv_gla_sqrelu_block"


def rmsnorm(x, g):
    xf = x.astype(jnp.float32)
    y = xf * lax.rsqrt(jnp.mean(xf * xf, axis=-1, keepdims=True) + EPS)
    return (y * g.astype(jnp.float32)).astype(x.dtype)


def group_rms(x, groups):
    xf = x.astype(jnp.float32).reshape(x.shape[:-1] + (groups, x.shape[-1] // groups))
    xf = xf * lax.rsqrt(jnp.mean(xf * xf, axis=-1, keepdims=True) + EPS)
    return xf.reshape(x.shape)


def short_conv_mixer(b_gate, c_gate, h, conv_w, conv_g):
    seq = h.shape[1]
    u = c_gate * h
    up = jnp.pad(u, ((0, 0), (CONV_WIDTH - 1, 0), (0, 0)))
    conv = sum(up[:, k:k + seq, :] * conv_w[:, k] for k in range(CONV_WIDTH))
    y = b_gate * conv
    return (group_rms(y, CONV_GROUPS) * conv_g.astype(jnp.float32)).astype(h.dtype)


def gla_chunk_causal(q, k, v, log_a):
    bsz, seq, nh, dk = q.shape
    dv = v.shape[-1]
    nc = seq // CHUNK
    f32 = jnp.float32
    q = q.astype(f32).reshape(bsz, nc, CHUNK, nh, dk) * (dk ** -0.5)
    k = k.astype(f32).reshape(bsz, nc, CHUNK, nh, dk)
    v = v.astype(f32).reshape(bsz, nc, CHUNK, nh, dv)
    la = log_a.astype(f32).reshape(bsz, nc, CHUNK, nh, dk)
    b_cum = jnp.cumsum(la, axis=2)
    b_end = b_cum[:, :, -1:]
    k_dec = k * jnp.exp(b_end - b_cum)
    kv = jnp.einsum('bnshk,bnshv->bnhkv', k_dec, v)
    decay = jnp.exp(b_end[:, :, 0])

    def step(state, xs):
        q_c, dec_c, kv_c = xs
        state = dec_c[..., None] * state + kv_c
        o_c = jnp.einsum('bthk,bhkv->bthv', q_c, state)
        return state, o_c

    s0 = jnp.zeros((bsz, nh, dk, dv), f32)
    xs = (jnp.moveaxis(q, 1, 0), jnp.moveaxis(decay, 1, 0), jnp.moveaxis(kv, 1, 0))
    _, o = lax.scan(step, s0, xs)
    return jnp.moveaxis(o, 0, 1).reshape(bsz, seq, nh, dv)


def _fwd_setup_inputs(seed: int = 0) -> dict:
    key = jax.random.key(seed)
    ks = jax.random.split(key, 14)
    f32 = jnp.float32
    nrm = lambda k, shape, s: jax.random.normal(k, shape, f32) * s
    gain = lambda k, shape: 1.0 + 0.02 * jax.random.normal(k, shape, f32)
    return {
        "x": jax.random.normal(ks[0], (BATCH, SEQ, D_MODEL), f32),
        "norm1_g": gain(ks[1], (DEPTH, D_MODEL)),
        "w_in": nrm(ks[2], (DEPTH, D_MODEL, D_IN), D_MODEL ** -0.5),
        "w_gate_up": nrm(ks[3], (DEPTH, GATE_RANK, D_GLA_K), GATE_RANK ** -0.5),
        "b_gate": nrm(ks[4], (DEPTH, D_GLA_K), 0.1),
        "conv_w": nrm(ks[5], (DEPTH, D_CONV, CONV_WIDTH), CONV_WIDTH ** -0.5),
        "conv_norm_g": gain(ks[6], (DEPTH, D_CONV)),
        "gla_norm_g": gain(ks[7], (DEPTH, GLA_DV)),
        "w_out": nrm(ks[8], (DEPTH, D_MIX, D_MODEL), D_MIX ** -0.5),
        "norm2_g": gain(ks[9], (DEPTH, D_MODEL)),
        "w_ff1": nrm(ks[10], (DEPTH, D_MODEL, D_FF), D_MODEL ** -0.5),
        "w_ff2": nrm(ks[11], (DEPTH, D_FF, D_MODEL), D_FF ** -0.5),
        "norm_f_g": gain(ks[12], (D_MODEL,)),
    }


def _fwd_reference(x, norm1_g, w_in, w_gate_up, b_gate, conv_w, conv_norm_g, gla_norm_g,
              w_out, norm2_g, w_ff1, w_ff2, norm_f_g):
    bsz, seq, _ = x.shape
    split_at = [int(i) for i in np.cumsum(IN_SIZES)[:-1]]
    for l in range(DEPTH):
        u = rmsnorm(x, norm1_g[l])
        z = u @ w_in[l]
        cb, cc, ch, q, k, v, og, a_low = jnp.split(z, split_at, axis=-1)
        y_conv = short_conv_mixer(cb, cc, ch, conv_w[l], conv_norm_g[l])
        log_a = jax.nn.log_sigmoid(a_low @ w_gate_up[l] + b_gate[l]) / GATE_NORMALIZER
        o = gla_chunk_causal(q.reshape(bsz, seq, GLA_HEADS, GLA_DK),
                             k.reshape(bsz, seq, GLA_HEADS, GLA_DK),
                             v.reshape(bsz, seq, GLA_HEADS, GLA_DV),
                             log_a.reshape(bsz, seq, GLA_HEADS, GLA_DK))
        o = o * lax.rsqrt(jnp.mean(o * o, axis=-1, keepdims=True) + EPS)
        o = o * gla_norm_g[l].astype(jnp.float32) * jax.nn.silu(
            og.astype(jnp.float32).reshape(bsz, seq, GLA_HEADS, GLA_DV))
        y_gla = o.reshape(bsz, seq, D_GLA_V).astype(x.dtype)
        y = jnp.concatenate([y_conv, y_gla], axis=-1)
        x = x + y @ w_out[l]
        h = rmsnorm(x, norm2_g[l])
        x = x + jnp.square(jax.nn.relu(h @ w_ff1[l])) @ w_ff2[l]
    return rmsnorm(x, norm_f_g)


import jax as _jax
import jax.numpy as _jnp

TWIN_FORMAT = 'train_step'
FWD_PARAMS = ['x', 'norm1_g', 'w_in', 'w_gate_up', 'b_gate', 'conv_w', 'conv_norm_g', 'gla_norm_g', 'w_out', 'norm2_g', 'w_ff1', 'w_ff2', 'norm_f_g']
TWIN_WEIGHTS = ['norm1_g', 'w_in', 'w_gate_up', 'b_gate', 'conv_w', 'conv_norm_g', 'gla_norm_g', 'w_out', 'norm2_g', 'w_ff1', 'w_ff2', 'norm_f_g']
TWIN_DIFF_INPUT = 'x'
TWIN_INPUTS = ['x', 'norm1_g', 'w_in', 'w_gate_up', 'b_gate', 'conv_w', 'conv_norm_g', 'gla_norm_g', 'w_out', 'norm2_g', 'w_ff1', 'w_ff2', 'norm_f_g', 'loss_target', 'm_norm1_g', 'm_w_in', 'm_w_gate_up', 'm_b_gate', 'm_conv_w', 'm_conv_norm_g', 'm_gla_norm_g', 'm_w_out', 'm_norm2_g', 'm_w_ff1', 'm_w_ff2', 'm_norm_f_g', 'v_norm1_g', 'v_w_in', 'v_w_gate_up', 'v_b_gate', 'v_conv_w', 'v_conv_norm_g', 'v_gla_norm_g', 'v_w_out', 'v_norm2_g', 'v_w_ff1', 'v_w_ff2', 'v_norm_f_g']
TWIN_OUTPUTS = ['loss', 'grad_x', 'grad_norm1_g', 'grad_w_in', 'grad_w_gate_up', 'grad_b_gate', 'grad_conv_w', 'grad_conv_norm_g', 'grad_gla_norm_g', 'grad_w_out', 'grad_norm2_g', 'grad_w_ff1', 'grad_w_ff2', 'grad_norm_f_g', 'delta_norm1_g', 'delta_w_in', 'delta_w_gate_up', 'delta_b_gate', 'delta_conv_w', 'delta_conv_norm_g', 'delta_gla_norm_g', 'delta_w_out', 'delta_norm2_g', 'delta_w_ff1', 'delta_w_ff2', 'delta_norm_f_g', 'new_m_norm1_g', 'new_m_w_in', 'new_m_w_gate_up', 'new_m_b_gate', 'new_m_conv_w', 'new_m_conv_norm_g', 'new_m_gla_norm_g', 'new_m_w_out', 'new_m_norm2_g', 'new_m_w_ff1', 'new_m_w_ff2', 'new_m_norm_f_g', 'new_v_norm1_g', 'new_v_w_in', 'new_v_w_gate_up', 'new_v_b_gate', 'new_v_conv_w', 'new_v_conv_norm_g', 'new_v_gla_norm_g', 'new_v_w_out', 'new_v_norm2_g', 'new_v_w_ff1', 'new_v_w_ff2', 'new_v_norm_f_g']
TWIN_LEAF_KINDS = {'loss': 'loss', 'grad_x': 'grad_x', 'grad_norm1_g': 'grad_w', 'grad_w_in': 'grad_w', 'grad_w_gate_up': 'grad_w', 'grad_b_gate': 'grad_w', 'grad_conv_w': 'grad_w', 'grad_conv_norm_g': 'grad_w', 'grad_gla_norm_g': 'grad_w', 'grad_w_out': 'grad_w', 'grad_norm2_g': 'grad_w', 'grad_w_ff1': 'grad_w', 'grad_w_ff2': 'grad_w', 'grad_norm_f_g': 'grad_w', 'delta_norm1_g': 'delta_w', 'delta_w_in': 'delta_w', 'delta_w_gate_up': 'delta_w', 'delta_b_gate': 'delta_w', 'delta_conv_w': 'delta_w', 'delta_conv_norm_g': 'delta_w', 'delta_gla_norm_g': 'delta_w', 'delta_w_out': 'delta_w', 'delta_norm2_g': 'delta_w', 'delta_w_ff1': 'delta_w', 'delta_w_ff2': 'delta_w', 'delta_norm_f_g': 'delta_w', 'new_m_norm1_g': 'new_m', 'new_m_w_in': 'new_m', 'new_m_w_gate_up': 'new_m', 'new_m_b_gate': 'new_m', 'new_m_conv_w': 'new_m', 'new_m_conv_norm_g': 'new_m', 'new_m_gla_norm_g': 'new_m', 'new_m_w_out': 'new_m', 'new_m_norm2_g': 'new_m', 'new_m_w_ff1': 'new_m', 'new_m_w_ff2': 'new_m', 'new_m_norm_f_g': 'new_m', 'new_v_norm1_g': 'new_v', 'new_v_w_in': 'new_v', 'new_v_w_gate_up': 'new_v', 'new_v_b_gate': 'new_v', 'new_v_conv_w': 'new_v', 'new_v_conv_norm_g': 'new_v', 'new_v_gla_norm_g': 'new_v', 'new_v_w_out': 'new_v', 'new_v_norm2_g': 'new_v', 'new_v_w_ff1': 'new_v', 'new_v_w_ff2': 'new_v', 'new_v_norm_f_g': 'new_v'}


def _forward(args):
    return _fwd_reference(*[args[k] for k in FWD_PARAMS])


def _output_shape():
    def fwd():
        inp = _fwd_setup_inputs(0)
        return _fwd_reference(*[inp[k] for k in FWD_PARAMS])
    out = _jax.eval_shape(fwd)
    return out.shape, out.dtype

N_MICROBATCH = 1
ADAM_LR = 0.001
ADAM_B1 = 0.9
ADAM_B2 = 0.999
ADAM_EPS = 1e-08
ADAM_WD = 0.01
ADAM_STEP = 10
PER_EXAMPLE_BATCH_AXIS = {'x': 0, 'loss_target': 0}
SHARED_INPUTS = []
_WEIGHT_DTYPES = {'norm1_g': _jnp.float32, 'w_in': _jnp.float32, 'w_gate_up': _jnp.float32, 'b_gate': _jnp.float32, 'conv_w': _jnp.float32, 'conv_norm_g': _jnp.float32, 'gla_norm_g': _jnp.float32, 'w_out': _jnp.float32, 'norm2_g': _jnp.float32, 'w_ff1': _jnp.float32, 'w_ff2': _jnp.float32, 'norm_f_g': _jnp.float32}
MOMENT_SCALE = {'norm1_g': 1.619059e-01, 'w_in': 9.165564e-02, 'w_gate_up': 1.060179e-02, 'b_gate': 5.196746e-02, 'conv_w': 1.096081e-01, 'conv_norm_g': 1.151533e-01, 'gla_norm_g': 1.236191e-01, 'w_out': 8.593757e-02, 'norm2_g': 1.002890e-01, 'w_ff1': 4.975955e-02, 'w_ff2': 1.023914e-01, 'norm_f_g': 3.217163e+01}


def _to_microbatches(a, axis):
    t = _jnp.moveaxis(a, axis, 0)
    t = t.reshape((N_MICROBATCH, t.shape[0] // N_MICROBATCH) + t.shape[1:])
    return _jnp.moveaxis(t, 1, axis + 1)


def setup_inputs(seed: int = 0) -> dict:
    inp = _fwd_setup_inputs(seed)
    key = _jax.random.fold_in(_jax.random.key(seed), 7919)
    shape, _ = _output_shape()
    out = dict(inp)
    out["loss_target"] = _jax.random.normal(_jax.random.fold_in(key, 0), shape, _jnp.float32)
    for i, name in enumerate(TWIN_WEIGHTS):
        w = inp[name].astype(_jnp.float32)
        if MOMENT_SCALE is None:
            s = _jnp.sqrt(_jnp.mean(_jnp.square(w)) + 1e-30)
        else:
            s = MOMENT_SCALE[name]
        km, kv = _jax.random.split(_jax.random.fold_in(key, i + 1))
        out[name] = w
        out["m_" + name] = s * _jax.random.normal(km, w.shape, _jnp.float32)
        out["v_" + name] = (s * s) * _jax.random.uniform(kv, w.shape, _jnp.float32, 0.5, 1.5)
    if N_MICROBATCH > 1:
        for name, axis in PER_EXAMPLE_BATCH_AXIS.items():
            out[name] = _to_microbatches(out[name], axis)
    return {'x': out['x'], 'norm1_g': out['norm1_g'], 'w_in': out['w_in'], 'w_gate_up': out['w_gate_up'], 'b_gate': out['b_gate'], 'conv_w': out['conv_w'], 'conv_norm_g': out['conv_norm_g'], 'gla_norm_g': out['gla_norm_g'], 'w_out': out['w_out'], 'norm2_g': out['norm2_g'], 'w_ff1': out['w_ff1'], 'w_ff2': out['w_ff2'], 'norm_f_g': out['norm_f_g'], 'loss_target': out['loss_target'], 'm_norm1_g': out['m_norm1_g'], 'm_w_in': out['m_w_in'], 'm_w_gate_up': out['m_w_gate_up'], 'm_b_gate': out['m_b_gate'], 'm_conv_w': out['m_conv_w'], 'm_conv_norm_g': out['m_conv_norm_g'], 'm_gla_norm_g': out['m_gla_norm_g'], 'm_w_out': out['m_w_out'], 'm_norm2_g': out['m_norm2_g'], 'm_w_ff1': out['m_w_ff1'], 'm_w_ff2': out['m_w_ff2'], 'm_norm_f_g': out['m_norm_f_g'], 'v_norm1_g': out['v_norm1_g'], 'v_w_in': out['v_w_in'], 'v_w_gate_up': out['v_w_gate_up'], 'v_b_gate': out['v_b_gate'], 'v_conv_w': out['v_conv_w'], 'v_conv_norm_g': out['v_conv_norm_g'], 'v_gla_norm_g': out['v_gla_norm_g'], 'v_w_out': out['v_w_out'], 'v_norm2_g': out['v_norm2_g'], 'v_w_ff1': out['v_w_ff1'], 'v_w_ff2': out['v_w_ff2'], 'v_norm_f_g': out['v_norm_f_g']}


def _loss(weights, diff, rest, loss_target):
    with _jax.named_scope("forward"):
        args = {**rest, TWIN_DIFF_INPUT: diff, **{k: w.astype(_WEIGHT_DTYPES[k]) for k, w in weights.items()}}
        y = _forward(args)
    with _jax.named_scope("loss_head"):
        err = _jnp.square(y.astype(_jnp.float32) - loss_target)
        return 0.5 * _jnp.sum(_jnp.mean(err, axis=-1)) if err.ndim else 0.5 * err


def _adamw(w, g, m, v):
    m = ADAM_B1 * m + (1.0 - ADAM_B1) * g
    v = ADAM_B2 * v + (1.0 - ADAM_B2) * _jnp.square(g)
    m_hat = m / (1.0 - ADAM_B1 ** ADAM_STEP)
    v_hat = v / (1.0 - ADAM_B2 ** ADAM_STEP)
    delta = -ADAM_LR * (m_hat / (_jnp.sqrt(v_hat) + ADAM_EPS) + ADAM_WD * w)
    return delta, m, v


def reference(x, norm1_g, w_in, w_gate_up, b_gate, conv_w, conv_norm_g, gla_norm_g, w_out, norm2_g, w_ff1, w_ff2, norm_f_g, loss_target, m_norm1_g, m_w_in, m_w_gate_up, m_b_gate, m_conv_w, m_conv_norm_g, m_gla_norm_g, m_w_out, m_norm2_g, m_w_ff1, m_w_ff2, m_norm_f_g, v_norm1_g, v_w_in, v_w_gate_up, v_b_gate, v_conv_w, v_conv_norm_g, v_gla_norm_g, v_w_out, v_norm2_g, v_w_ff1, v_w_ff2, v_norm_f_g):
    given = dict(x=x, norm1_g=norm1_g, w_in=w_in, w_gate_up=w_gate_up, b_gate=b_gate, conv_w=conv_w, conv_norm_g=conv_norm_g, gla_norm_g=gla_norm_g, w_out=w_out, norm2_g=norm2_g, w_ff1=w_ff1, w_ff2=w_ff2, norm_f_g=norm_f_g, loss_target=loss_target, m_norm1_g=m_norm1_g, m_w_in=m_w_in, m_w_gate_up=m_w_gate_up, m_b_gate=m_b_gate, m_conv_w=m_conv_w, m_conv_norm_g=m_conv_norm_g, m_gla_norm_g=m_gla_norm_g, m_w_out=m_w_out, m_norm2_g=m_norm2_g, m_w_ff1=m_w_ff1, m_w_ff2=m_w_ff2, m_norm_f_g=m_norm_f_g, v_norm1_g=v_norm1_g, v_w_in=v_w_in, v_w_gate_up=v_w_gate_up, v_b_gate=v_b_gate, v_conv_w=v_conv_w, v_conv_norm_g=v_conv_norm_g, v_gla_norm_g=v_gla_norm_g, v_w_out=v_w_out, v_norm2_g=v_norm2_g, v_w_ff1=v_w_ff1, v_w_ff2=v_w_ff2, v_norm_f_g=v_norm_f_g)
    weights = {n: given[n] for n in TWIN_WEIGHTS}
    shared = {n: given[n] for n in SHARED_INPUTS}
    per_example = {n: given[n] for n in ['x']}
    grad_fn = _jax.value_and_grad(_loss, argnums=(0, 1))

    def one_microbatch(ex, loss_target):
        ex = dict(ex)
        diff = ex.pop(TWIN_DIFF_INPUT)
        return grad_fn(weights, diff, {**shared, **ex}, loss_target)

    if N_MICROBATCH == 1:
        loss, (grad_w, grad_x) = one_microbatch(per_example, given["loss_target"])
    else:
        def body(carry, xs):
            loss_sum, grad_sum = carry
            l_k, (gw_k, gx_k) = one_microbatch(xs[0], xs[1])
            with _jax.named_scope("update"):
                return (loss_sum + l_k, _jax.tree.map(_jnp.add, grad_sum, gw_k)), gx_k

        init = (_jnp.zeros((), _jnp.float32), _jax.tree.map(_jnp.zeros_like, weights))
        (loss, grad_w), grad_x = _jax.lax.scan(body, init, (per_example, given["loss_target"]))
    with _jax.named_scope("update"):
        delta_w, new_m, new_v = {}, {}, {}
        for n in TWIN_WEIGHTS:
            delta_w[n], new_m[n], new_v[n] = _adamw(weights[n], grad_w[n], given["m_" + n], given["v_" + n])
    return (loss, grad_x, *[grad_w[n] for n in TWIN_WEIGHTS], *[delta_w[n] for n in TWIN_WEIGHTS],
            *[new_m[n] for n in TWIN_WEIGHTS], *[new_v[n] for n in TWIN_WEIGHTS])
```

```python
import functools

import jax
import jax.numpy as jnp
from jax import lax
from jax.experimental import pallas as pl
from jax.experimental.pallas import tpu as pltpu

F32 = jnp.float32
BF16 = jnp.bfloat16

N_DEV = 8
CHUNK = 64
GLA_HEADS = 4
CONV_GROUPS = 8
CONV_WIDTH = 3
GATE_RANK = 16
GATE_NORMALIZER = 16.0
EPS = 1e-6
ADAM_LR = 0.001
ADAM_B1 = 0.9
ADAM_B2 = 0.999
ADAM_EPS = 1e-08
ADAM_WD = 0.01
ADAM_STEP = 10

LANES = 128
SUBLANES = 8
VMEM_LIMIT = 56 << 20

_NN = (((1,), (0,)), ((), ()))
_NT = (((1,), (1,)), ((), ()))
_TN = (((0,), (0,)), ((), ()))


def _dot(a, b, dims=_NN):
    return lax.dot_general(a, b, dims, preferred_element_type=F32)


def _params(n_grid):
    return pltpu.CompilerParams(dimension_semantics=("arbitrary",) * n_grid, vmem_limit_bytes=VMEM_LIMIT)


def _relu_sq(a):
    r = jnp.maximum(a.astype(F32), 0.0)
    return (r * r).astype(BF16)


def _device_index():
    return 4 * lax.axis_index("x") + 2 * lax.axis_index("y") + lax.axis_index("c")


def _peer(mask):
    x, y, c = lax.axis_index("x"), lax.axis_index("y"), lax.axis_index("c")
    return (x ^ ((mask >> 2) & 1), y ^ ((mask >> 1) & 1), c ^ (mask & 1))


def _all_gather(shards):
    n = len(shards)

    def body(*refs):
        src, dst = refs[:n], refs[n:2 * n]
        send_sems, recv_sems, local_sems = refs[2 * n:]
        me = _device_index()
        local = [pltpu.make_async_copy(src[a], dst[a].at[me], local_sems.at[a]) for a in range(n)]
        for cp in local:
            cp.start()
        sends = []
        for a in range(n):
            for mask in range(1, N_DEV):
                sends.append(pltpu.make_async_remote_copy(
                    src_ref=src[a], dst_ref=dst[a].at[me],
                    send_sem=send_sems.at[a, mask - 1], recv_sem=recv_sems.at[a, mask - 1],
                    device_id=_peer(mask), device_id_type=pl.DeviceIdType.MESH))
        for cp in sends:
            cp.start()
        for a in range(n):
            for mask in range(1, N_DEV):
                pltpu.make_async_remote_copy(
                    src_ref=src[a], dst_ref=dst[a].at[me ^ mask],
                    send_sem=send_sems.at[a, mask - 1], recv_sem=recv_sems.at[a, mask - 1],
                    device_id=_peer(mask), device_id_type=pl.DeviceIdType.MESH).wait_recv()
        for cp in sends:
            cp.wait_send()
        for cp in local:
            cp.wait()

    any_spec = pl.BlockSpec(memory_space=pl.ANY)
    return pl.pallas_call(
        body, name="all_gather_weights",
        out_shape=[jax.ShapeDtypeStruct((N_DEV,) + s.shape, s.dtype) for s in shards],
        in_specs=[any_spec] * n, out_specs=[any_spec] * n,
        scratch_shapes=[pltpu.SemaphoreType.DMA((n, N_DEV - 1)), pltpu.SemaphoreType.DMA((n, N_DEV - 1)),
                        pltpu.SemaphoreType.DMA((n,))],
    )(*shards)


def _scatter_partials(partials):
    n = len(partials)

    def body(*refs):
        src, dst = refs[:n], refs[n:2 * n]
        send_sems, recv_sems, local_sems = refs[2 * n:]
        me = _device_index()
        local = [pltpu.make_async_copy(src[a].at[me], dst[a].at[me], local_sems.at[a]) for a in range(n)]
        for cp in local:
            cp.start()
        sends = []
        for a in range(n):
            for mask in range(1, N_DEV):
                sends.append(pltpu.make_async_remote_copy(
                    src_ref=src[a].at[me ^ mask], dst_ref=dst[a].at[me],
                    send_sem=send_sems.at[a, mask - 1], recv_sem=recv_sems.at[a, mask - 1],
                    device_id=_peer(mask), device_id_type=pl.DeviceIdType.MESH))
        for cp in sends:
            cp.start()
        for a in range(n):
            for mask in range(1, N_DEV):
                pltpu.make_async_remote_copy(
                    src_ref=src[a].at[me], dst_ref=dst[a].at[me ^ mask],
                    send_sem=send_sems.at[a, mask - 1], recv_sem=recv_sems.at[a, mask - 1],
                    device_id=_peer(mask), device_id_type=pl.DeviceIdType.MESH).wait_recv()
        for cp in sends:
            cp.wait_send()
        for cp in local:
            cp.wait()

    any_spec = pl.BlockSpec(memory_space=pl.ANY)
    return pl.pallas_call(
        body, name="scatter_grad_partials",
        out_shape=[jax.ShapeDtypeStruct(p.shape, p.dtype) for p in partials],
        in_specs=[any_spec] * n, out_specs=[any_spec] * n,
        scratch_shapes=[pltpu.SemaphoreType.DMA((n, N_DEV - 1)), pltpu.SemaphoreType.DMA((n, N_DEV - 1)),
                        pltpu.SemaphoreType.DMA((n,))],
    )(*partials)


def _inproj(x, g1, w_main, w_alow, tm=1024, tn=1024):
    t, d = x.shape
    tm = min(tm, t)
    n = w_main.shape[1]

    def body(x_ref, g_ref, w_ref, wa_ref, z_ref, u_ref, al_ref):
        @pl.when(pl.program_id(1) == 0)
        def _():
            xf = x_ref[...]
            r = lax.rsqrt(jnp.mean(xf * xf, axis=-1, keepdims=True) + EPS)
            u = (xf * r * g_ref[...]).astype(BF16)
            u_ref[...] = u
            al_ref[...] = _dot(u, wa_ref[...])

        z_ref[...] = _dot(u_ref[...], w_ref[...])

    return pl.pallas_call(
        body, name="rmsnorm_inproj", grid=(t // tm, n // tn),
        in_specs=[pl.BlockSpec((tm, d), lambda m, j: (m, 0)), pl.BlockSpec((1, d), lambda m, j: (0, 0)),
                  pl.BlockSpec((d, tn), lambda m, j: (0, j)), pl.BlockSpec((d, LANES), lambda m, j: (0, 0))],
        out_specs=[pl.BlockSpec((tm, tn), lambda m, j: (m, j)), pl.BlockSpec((tm, d), lambda m, j: (m, 0)),
                   pl.BlockSpec((tm, LANES), lambda m, j: (m, 0))],
        out_shape=[jax.ShapeDtypeStruct((t, n), F32), jax.ShapeDtypeStruct((t, d), BF16),
                   jax.ShapeDtypeStruct((t, LANES), F32)],
        compiler_params=_params(2),
    )(x, g1, w_main, w_alow)


def _outproj(y, w_out, x, g2, tm=512):
    t, d = x.shape
    tm = min(tm, t)
    k = y.shape[1]

    def body(y_ref, w_ref, x_ref, g_ref, x1_ref, h_ref):
        x1 = x_ref[...] + _dot(y_ref[...], w_ref[...])
        x1_ref[...] = x1
        r = lax.rsqrt(jnp.mean(x1 * x1, axis=-1, keepdims=True) + EPS)
        h_ref[...] = (x1 * r * g_ref[...]).astype(BF16)

    return pl.pallas_call(
        body, name="outproj_rmsnorm", grid=(t // tm,),
        in_specs=[pl.BlockSpec((tm, k), lambda m: (m, 0)), pl.BlockSpec((k, d), lambda m: (0, 0)),
                  pl.BlockSpec((tm, d), lambda m: (m, 0)), pl.BlockSpec((1, d), lambda m: (0, 0))],
        out_specs=[pl.BlockSpec((tm, d), lambda m: (m, 0)), pl.BlockSpec((tm, d), lambda m: (m, 0))],
        out_shape=[jax.ShapeDtypeStruct((t, d), F32), jax.ShapeDtypeStruct((t, d), BF16)],
        compiler_params=_params(1),
    )(y, w_out, x, g2)


def _ff1(h, w1g, tm=1024):
    t, d = h.shape
    tm = min(tm, t)
    g, _, f = w1g.shape

    def body(h_ref, w_ref, a_ref):
        a_ref[...] = _dot(h_ref[...], w_ref[...]).astype(BF16)

    return pl.pallas_call(
        body, name="ff1", grid=(t // tm, g),
        in_specs=[pl.BlockSpec((tm, d), lambda m, j: (m, 0)), pl.BlockSpec((None, d, f), lambda m, j: (j, 0, 0))],
        out_specs=pl.BlockSpec((tm, f), lambda m, j: (m, j)),
        out_shape=jax.ShapeDtypeStruct((t, g * f), BF16),
        compiler_params=_params(2),
    )(h, w1g)


def _ff2(a, w2, x1, tm=1024, tn=1024, tk=2048):
    t, f = a.shape
    tm = min(tm, t)
    d = w2.shape[1]
    nk = f // tk

    def body(a_ref, w_ref, x1_ref, o_ref, acc_ref):
        kk = pl.program_id(2)

        @pl.when(kk == 0)
        def _():
            acc_ref[...] = x1_ref[...]

        acc_ref[...] += _dot(_relu_sq(a_ref[...]), w_ref[...])

        @pl.when(kk == nk - 1)
        def _():
            o_ref[...] = acc_ref[...]

    return pl.pallas_call(
        body, name="ff2_residual", grid=(t // tm, d // tn, nk),
        in_specs=[pl.BlockSpec((tm, tk), lambda m, j, kk: (m, kk)), pl.BlockSpec((tk, tn), lambda m, j, kk: (kk, j)),
                  pl.BlockSpec((tm, tn), lambda m, j, kk: (m, j))],
        out_specs=pl.BlockSpec((tm, tn), lambda m, j, kk: (m, j)),
        out_shape=jax.ShapeDtypeStruct((t, d), F32),
        scratch_shapes=[pltpu.VMEM((tm, tn), F32)],
        compiler_params=_params(3),
    )(a, w2, x1)


def _dff2(dx2b, w2, a, tm=1024, tn=1024):
    t, d = dx2b.shape
    tm = min(tm, t)
    f = w2.shape[0]

    def body(g_ref, w_ref, a_ref, o_ref):
        dp = _dot(g_ref[...], w_ref[...], _NT)
        o_ref[...] = (dp * (2.0 * jnp.maximum(a_ref[...].astype(F32), 0.0))).astype(BF16)

    return pl.pallas_call(
        body, name="dff2", grid=(t // tm, f // tn),
        in_specs=[pl.BlockSpec((tm, d), lambda m, j: (m, 0)), pl.BlockSpec((tn, d), lambda m, j: (j, 0)),
                  pl.BlockSpec((tm, tn), lambda m, j: (m, j))],
        out_specs=pl.BlockSpec((tm, tn), lambda m, j: (m, j)),
        out_shape=jax.ShapeDtypeStruct((t, f), BF16),
        compiler_params=_params(2),
    )(dx2b, w2, a)


def _tn_matmul(name, a, b, grid, a_spec, b_spec, out_shape, out_spec, acc_shape, a_fn=None):
    nk = grid[-1]

    def body(a_ref, b_ref, o_ref, acc_ref):
        kk = pl.program_id(len(grid) - 1)
        av = a_ref[...]
        if a_fn is not None:
            av = a_fn(av)
        part = _dot(av, b_ref[...], _TN)

        @pl.when(kk == 0)
        def _():
            acc_ref[...] = part

        @pl.when(kk > 0)
        def _():
            acc_ref[...] += part

        @pl.when(kk == nk - 1)
        def _():
            o_ref[...] = acc_ref[...].astype(o_ref.dtype)

    return pl.pallas_call(
        body, name=name, grid=grid, in_specs=[a_spec, b_spec], out_specs=out_spec, out_shape=out_shape,
        scratch_shapes=[pltpu.VMEM(acc_shape, F32)], compiler_params=_params(len(grid)),
    )(a, b)


def _dh(da, w1g, tm=1024, tn=1024):
    t = da.shape[0]
    tm = min(tm, t)
    g, d, f = w1g.shape

    def body(a_ref, w_ref, o_ref, acc_ref):
        kk = pl.program_id(2)
        part = _dot(a_ref[...], w_ref[...], _NT)

        @pl.when(kk == 0)
        def _():
            acc_ref[...] = part

        @pl.when(kk > 0)
        def _():
            acc_ref[...] += part

        @pl.when(kk == g - 1)
        def _():
            o_ref[...] = acc_ref[...]

    return pl.pallas_call(
        body, name="dh", grid=(t // tm, d // tn, g),
        in_specs=[pl.BlockSpec((tm, f), lambda m, j, kk: (m, kk)),
                  pl.BlockSpec((None, tn, f), lambda m, j, kk: (kk, j, 0))],
        out_specs=pl.BlockSpec((tm, tn), lambda m, j, kk: (m, j)),
        out_shape=jax.ShapeDtypeStruct((t, d), F32),
        scratch_shapes=[pltpu.VMEM((tm, tn), F32)],
        compiler_params=_params(3),
    )(da, w1g)


def _nt_matmul(name, a, b, tm=1024, tn=1024):
    t, k = a.shape
    tm = min(tm, t)
    n = b.shape[0]

    def body(a_ref, b_ref, o_ref):
        o_ref[...] = _dot(a_ref[...], b_ref[...], _NT)

    return pl.pallas_call(
        body, name=name, grid=(t // tm, n // tn),
        in_specs=[pl.BlockSpec((tm, k), lambda m, j: (m, 0)), pl.BlockSpec((tn, k), lambda m, j: (j, 0))],
        out_specs=pl.BlockSpec((tm, tn), lambda m, j: (m, j)),
        out_shape=jax.ShapeDtypeStruct((t, n), F32),
        compiler_params=_params(2),
    )(a, b)


def _du(dz, w_main, dzal, w_alow, tm=1024, tn=1024, tk=2048):
    t, n = dz.shape
    tm = min(tm, t)
    d = w_main.shape[0]
    nk = n // tk

    def body(a_ref, w_ref, al_ref, wa_ref, o_ref, acc_ref):
        kk = pl.program_id(2)

        @pl.when(kk == 0)
        def _():
            acc_ref[...] = _dot(al_ref[...], wa_ref[...], _NT)

        acc_ref[...] += _dot(a_ref[...], w_ref[...], _NT)

        @pl.when(kk == nk - 1)
        def _():
            o_ref[...] = acc_ref[...]

    return pl.pallas_call(
        body, name="du", grid=(t // tm, d // tn, nk),
        in_specs=[pl.BlockSpec((tm, tk), lambda m, j, kk: (m, kk)), pl.BlockSpec((tn, tk), lambda m, j, kk: (j, kk)),
                  pl.BlockSpec((tm, LANES), lambda m, j, kk: (m, 0)), pl.BlockSpec((tn, LANES), lambda m, j, kk: (j, 0))],
        out_specs=pl.BlockSpec((tm, tn), lambda m, j, kk: (m, j)),
        out_shape=jax.ShapeDtypeStruct((t, d), F32),
        scratch_shapes=[pltpu.VMEM((tm, tn), F32)],
        compiler_params=_params(3),
    )(dz, w_main, dzal, w_alow)


def _loss_head(x2, gf, tgt, tr=256):
    t, d = x2.shape

    def body(x_ref, g_ref, t_ref, dx_ref, dxb_ref, loss_ref, dg_ref):
        @pl.when(pl.program_id(0) == 0)
        def _():
            loss_ref[...] = jnp.zeros_like(loss_ref)
            dg_ref[...] = jnp.zeros_like(dg_ref)

        xf = x_ref[...]
        g = g_ref[...]
        r = lax.rsqrt(jnp.mean(xf * xf, axis=-1, keepdims=True) + EPS)
        xh = xf * r
        e = xh * g - t_ref[...]
        loss_ref[...] += 0.5 * jnp.sum(jnp.mean(e * e, axis=-1, keepdims=True))
        dy = e * (1.0 / d)
        dg_ref[...] += jnp.sum(dy * xh, axis=0, keepdims=True)
        dyg = dy * g
        dx = r * (dyg - xh * jnp.mean(dyg * xh, axis=-1, keepdims=True))
        dx_ref[...] = dx
        dxb_ref[...] = dx.astype(BF16)

    return pl.pallas_call(
        body, name="loss_head", grid=(t // tr,),
        in_specs=[pl.BlockSpec((tr, d), lambda i: (i, 0)), pl.BlockSpec((1, d), lambda i: (0, 0)),
                  pl.BlockSpec((tr, d), lambda i: (i, 0))],
        out_specs=[pl.BlockSpec((tr, d), lambda i: (i, 0)), pl.BlockSpec((tr, d), lambda i: (i, 0)),
                   pl.BlockSpec((SUBLANES, LANES), lambda i: (0, 0)), pl.BlockSpec((1, d), lambda i: (0, 0))],
        out_shape=[jax.ShapeDtypeStruct((t, d), F32), jax.ShapeDtypeStruct((t, d), BF16),
                   jax.ShapeDtypeStruct((SUBLANES, LANES), F32), jax.ShapeDtypeStruct((1, d), F32)],
        compiler_params=_params(1),
    )(x2, gf, tgt)


def _norm_bwd(name, dh, xin, g, dres, tr=256):
    t, d = xin.shape

    def body(dh_ref, x_ref, g_ref, dr_ref, dx_ref, dxb_ref, dg_ref):
        @pl.when(pl.program_id(0) == 0)
        def _():
            dg_ref[...] = jnp.zeros_like(dg_ref)

        xf = x_ref[...]
        dhv = dh_ref[...]
        r = lax.rsqrt(jnp.mean(xf * xf, axis=-1, keepdims=True) + EPS)
        xh = xf * r
        dg_ref[...] += jnp.sum(dhv * xh, axis=0, keepdims=True)
        dyg = dhv * g_ref[...]
        dx = dr_ref[...] + r * (dyg - xh * jnp.mean(dyg * xh, axis=-1, keepdims=True))
        dx_ref[...] = dx
        dxb_ref[...] = dx.astype(BF16)

    return pl.pallas_call(
        body, name=name, grid=(t // tr,),
        in_specs=[pl.BlockSpec((tr, d), lambda i: (i, 0)), pl.BlockSpec((tr, d), lambda i: (i, 0)),
                  pl.BlockSpec((1, d), lambda i: (0, 0)), pl.BlockSpec((tr, d), lambda i: (i, 0))],
        out_specs=[pl.BlockSpec((tr, d), lambda i: (i, 0)), pl.BlockSpec((tr, d), lambda i: (i, 0)),
                   pl.BlockSpec((1, d), lambda i: (0, 0))],
        out_shape=[jax.ShapeDtypeStruct((t, d), F32), jax.ShapeDtypeStruct((t, d), BF16),
                   jax.ShapeDtypeStruct((1, d), F32)],
        compiler_params=_params(1),
    )(dh, xin, g, dres)


MIX_TILE = 256
CHUNKS_PER_TILE = MIX_TILE // CHUNK
CHUNK_SHIFT = CHUNK.bit_length() - 1
assert 1 << CHUNK_SHIFT == CHUNK


def _chunk_masks(n):
    row = lax.broadcasted_iota(jnp.int32, (n, n), 0)
    col = lax.broadcasted_iota(jnp.int32, (n, n), 1)
    same = lax.shift_right_logical(row, CHUNK_SHIFT) == lax.shift_right_logical(col, CHUNK_SHIFT)
    one = lambda m: jnp.where(m, 1.0, 0.0).astype(BF16)
    return one(same & (col > row)), one(same), one(same & (col < row))


def _mask_dot(mask, x):
    hi = x.astype(BF16)
    r1 = x - hi.astype(F32)
    mid = r1.astype(BF16)
    lo = (r1 - mid.astype(F32)).astype(BF16)
    return _dot(mask, hi) + _dot(mask, mid) + _dot(mask, lo)


def _log_sigmoid(x):
    return jnp.minimum(x, 0.0) - jnp.log1p(jnp.exp(-jnp.abs(x)))


def _conv_taps(prev8, uc, w):
    ext = jnp.concatenate([prev8, uc], axis=0)
    s1 = pltpu.roll(ext, 1, 0)[SUBLANES:]
    s2 = pltpu.roll(ext, 2, 0)[SUBLANES:]
    return s2 * w[0:1] + s1 * w[1:2] + uc * w[2:3], s1, s2


def _z_specs(tile, idx):
    d_conv = 1024
    wide = lambda c: pl.BlockSpec((tile, d_conv), lambda i, c=c: (idx(i), c))
    half = lambda c: pl.BlockSpec((tile, d_conv // 2), lambda i, c=c: (idx(i), c))
    return [wide(0), wide(1), wide(2), half(6), half(7), wide(4), wide(5)]


def _mixer_fwd(z, alow, wgu, b_gate, convw, conv_g, gla_g):
    t = z.shape[0]
    tb, cpt = MIX_TILE, CHUNKS_PER_TILE
    d_conv = conv_g.shape[1]
    dv = gla_g.shape[1]
    dk = dv // 2
    d_k = GLA_HEADS * dk
    gw = d_conv // CONV_GROUPS
    scale = dk ** -0.5

    def body(cb_ref, cc_ref, ch_ref, q_ref, k_ref, v_ref, og_ref, al_ref, wgu_ref, bg_ref, cw_ref, cg_ref, gg_ref,
             y_ref, sall_ref, carry_ref, s_ref):
        @pl.when(pl.program_id(0) == 0)
        def _():
            carry_ref[...] = jnp.zeros_like(carry_ref)
            s_ref[...] = jnp.zeros_like(s_ref)

        uc = cc_ref[...] * ch_ref[...]
        conv, _, _ = _conv_taps(carry_ref[...], uc, cw_ref[...])
        carry_ref[...] = uc[tb - SUBLANES:]
        ypre = cb_ref[...] * conv
        cg = cg_ref[...]
        for g in range(CONV_GROUPS):
            sl = slice(g * gw, (g + 1) * gw)
            seg = ypre[:, sl]
            r = lax.rsqrt(jnp.mean(seg * seg, axis=-1, keepdims=True) + EPS)
            y_ref[:, sl] = (seg * r * cg[:, sl]).astype(BF16)

        later, same, _ = _chunk_masks(tb)
        pre = _dot(al_ref[...].astype(BF16), wgu_ref[...]) + bg_ref[...]
        la = _log_sigmoid(pre) * (1.0 / GATE_NORMALIZER)
        e_dec = _mask_dot(later, la)
        dec_all = jnp.exp(_mask_dot(same, la))
        kdec = (k_ref[...] * jnp.exp(e_dec)).astype(BF16)
        qs = (q_ref[...] * scale).astype(BF16)
        vb = v_ref[...].astype(BF16)
        og = og_ref[...]
        gg = gg_ref[...]
        for c in range(cpt):
            rows = slice(c * CHUNK, (c + 1) * CHUNK)
            for h in range(GLA_HEADS):
                ks = slice(h * dk, (h + 1) * dk)
                vs = slice(h * dv, (h + 1) * dv)
                kvt = _dot(vb[rows, vs], kdec[rows, ks], _TN)
                st = s_ref[h] * dec_all[c * CHUNK:c * CHUNK + 1, ks] + kvt
                s_ref[h] = st
                sall_ref[c, h] = st
                o = _dot(qs[rows, ks], st.astype(BF16), _NT)
                ro = lax.rsqrt(jnp.mean(o * o, axis=-1, keepdims=True) + EPS)
                ogs = og[rows, vs]
                yg = o * ro * gg * (ogs * jax.nn.sigmoid(ogs))
                y_ref[rows, d_conv + h * dv:d_conv + (h + 1) * dv] = yg.astype(BF16)

    full = lambda shape: pl.BlockSpec(shape, lambda i: (0,) * len(shape))
    return pl.pallas_call(
        body, name="mixer_fwd", grid=(t // tb,),
        in_specs=_z_specs(tb, lambda i: i) + [
            pl.BlockSpec((tb, LANES), lambda i: (i, 0)), full(wgu.shape), full(b_gate.shape), full(convw.shape),
            full(conv_g.shape), full(gla_g.shape)],
        out_specs=[pl.BlockSpec((tb, d_conv + GLA_HEADS * dv), lambda i: (i, 0)),
                   pl.BlockSpec((cpt, GLA_HEADS, dv, dk), lambda i: (i, 0, 0, 0))],
        out_shape=[jax.ShapeDtypeStruct((t, d_conv + GLA_HEADS * dv), BF16),
                   jax.ShapeDtypeStruct((t // CHUNK, GLA_HEADS, dv, dk), F32)],
        scratch_shapes=[pltpu.VMEM((SUBLANES, d_conv), F32), pltpu.VMEM((GLA_HEADS, dv, dk), F32)],
        compiler_params=_params(1),
    )(z, z, z, z, z, z, z, alow, wgu, b_gate, convw, conv_g, gla_g)


def _mixer_bwd(z, alow, dy, sall, wgu, b_gate, convw, conv_g, gla_g):
    t = z.shape[0]
    tb, cpt = MIX_TILE, CHUNKS_PER_TILE
    nt = t // tb
    d_conv = conv_g.shape[1]
    dv = gla_g.shape[1]
    dk = dv // 2
    d_k = GLA_HEADS * dk
    gw = d_conv // CONV_GROUPS
    scale = dk ** -0.5
    rev = lambda i: nt - 1 - i

    def body(cb_ref, cc_ref, ch_ref, q_ref, k_ref, v_ref, og_ref, ccp_ref, chp_ref, al_ref, dy_ref, sall_ref, sprev_ref,
             wgu_ref, bg_ref, cw_ref, cg_ref, gg_ref,
             dz_ref, dzal_ref, dcw_ref, dcg_ref, dgg_ref, dbg_ref, dwgu_ref,
             dcarry_ref, gd_ref, de_ref, ddd_ref):
        i = pl.program_id(0)

        @pl.when(i == 0)
        def _():
            dcarry_ref[...] = jnp.zeros_like(dcarry_ref)
            gd_ref[...] = jnp.zeros_like(gd_ref)
            dcw_ref[...] = jnp.zeros_like(dcw_ref)
            dcg_ref[...] = jnp.zeros_like(dcg_ref)
            dgg_ref[...] = jnp.zeros_like(dgg_ref)
            dbg_ref[...] = jnp.zeros_like(dbg_ref)
            dwgu_ref[...] = jnp.zeros_like(dwgu_ref)

        first = rev(i) == 0

        cb, cc, ch = cb_ref[...], cc_ref[...], ch_ref[...]
        w = cw_ref[...]
        uc = cc * ch
        prev8 = jnp.where(first, 0.0, ccp_ref[...] * chp_ref[...])
        conv, s1, s2 = _conv_taps(prev8, uc, w)
        ypre = cb * conv
        cg = cg_ref[...]
        dypre_parts = []
        for g in range(CONV_GROUPS):
            sl = slice(g * gw, (g + 1) * gw)
            seg = ypre[:, sl]
            r = lax.rsqrt(jnp.mean(seg * seg, axis=-1, keepdims=True) + EPS)
            yn = seg * r
            dyc = dy_ref[:, sl]
            dcg_ref[:, sl] += jnp.sum(dyc * yn, axis=0, keepdims=True)
            dyn = dyc * cg[:, sl]
            dypre_parts.append(r * (dyn - yn * jnp.mean(dyn * yn, axis=-1, keepdims=True)))
        dypre = jnp.concatenate(dypre_parts, axis=1)
        dconv = dypre * cb
        dz_ref[:, 0:d_conv] = (dypre * conv).astype(BF16)
        dcw_ref[0:1] += jnp.sum(dconv * s2, axis=0, keepdims=True)
        dcw_ref[1:2] += jnp.sum(dconv * s1, axis=0, keepdims=True)
        dcw_ref[2:3] += jnp.sum(dconv * uc, axis=0, keepdims=True)
        ext = jnp.concatenate([dconv, dcarry_ref[...]], axis=0)
        f1 = pltpu.roll(ext, tb + SUBLANES - 1, 0)[:tb]
        f2 = pltpu.roll(ext, tb + SUBLANES - 2, 0)[:tb]
        dcarry_ref[...] = dconv[:SUBLANES]
        duc = dconv * w[2:3] + f1 * w[1:2] + f2 * w[0:1]
        dz_ref[:, d_conv:2 * d_conv] = (duc * ch).astype(BF16)
        dz_ref[:, 2 * d_conv:3 * d_conv] = (duc * cc).astype(BF16)

        q_off = 3 * d_conv
        k_off = q_off + d_k
        v_off = k_off + d_k
        og_off = v_off + GLA_HEADS * dv
        later, same, earlier = _chunk_masks(tb)
        alb = al_ref[...].astype(BF16)
        pre = _dot(alb, wgu_ref[...]) + bg_ref[...]
        la = _log_sigmoid(pre) * (1.0 / GATE_NORMALIZER)
        exp_e = jnp.exp(_mask_dot(later, la))
        dec_all = jnp.exp(_mask_dot(same, la))
        kdec = k_ref[...] * exp_e
        kdec_b = kdec.astype(BF16)
        qs = (q_ref[...] * scale).astype(BF16)
        vb = v_ref[...].astype(BF16)
        og = og_ref[...]
        gg = gg_ref[...]
        for c in reversed(range(cpt)):
            rows = slice(c * CHUNK, (c + 1) * CHUNK)
            for h in range(GLA_HEADS):
                ks = slice(h * dk, (h + 1) * dk)
                vs = slice(h * dv, (h + 1) * dv)
                st = sall_ref[c, h]
                if c > 0:
                    st_prev = sall_ref[c - 1, h]
                else:
                    st_prev = jnp.where(first, 0.0, sprev_ref[0, h])
                st_b = st.astype(BF16)
                o = _dot(qs[rows, ks], st_b, _NT)
                ro = lax.rsqrt(jnp.mean(o * o, axis=-1, keepdims=True) + EPS)
                on = o * ro
                ogs = og[rows, vs]
                sg = jax.nn.sigmoid(ogs)
                gate = ogs * sg
                dyg = dy_ref[rows, d_conv + h * dv:d_conv + (h + 1) * dv]
                dgg_ref[...] += jnp.sum(dyg * on * gate, axis=0, keepdims=True)
                dz_ref[rows, og_off + h * dv:og_off + (h + 1) * dv] = (
                    dyg * on * gg * (sg * (1.0 + ogs * (1.0 - sg)))).astype(BF16)
                don = dyg * gg * gate
                do = (ro * (don - on * jnp.mean(don * on, axis=-1, keepdims=True))).astype(BF16)
                dz_ref[rows, q_off + h * dk:q_off + (h + 1) * dk] = (_dot(do, st_b) * scale).astype(BF16)
                gt = _dot(do, qs[rows, ks], _TN) + gd_ref[h]
                dec = dec_all[c * CHUNK:c * CHUNK + 1, ks]
                ddec = jnp.sum(gt * st_prev, axis=0, keepdims=True)
                gd_ref[h] = gt * dec
                gt_b = gt.astype(BF16)
                dkdec = _dot(vb[rows, vs], gt_b)
                dz_ref[rows, v_off + h * dv:v_off + (h + 1) * dv] = _dot(kdec_b[rows, ks], gt_b, _NT).astype(BF16)
                dz_ref[rows, k_off + h * dk:k_off + (h + 1) * dk] = (dkdec * exp_e[rows, ks]).astype(BF16)
                de_ref[rows, ks] = dkdec * kdec[rows, ks]
                ddd_ref[rows, ks] = jnp.broadcast_to(ddec * dec, (CHUNK, dk))
        dla = _mask_dot(earlier, de_ref[...]) + ddd_ref[...]
        dpre = dla * (1.0 / GATE_NORMALIZER) * jax.nn.sigmoid(-pre)
        dbg_ref[...] += jnp.sum(dpre, axis=0, keepdims=True)
        dpre_b = dpre.astype(BF16)
        dwgu_ref[...] += _dot(alb, dpre_b, _TN)
        dzal_ref[...] = _dot(dpre_b, wgu_ref[...], _NT).astype(BF16)

    full = lambda shape: pl.BlockSpec(shape, lambda i: (0,) * len(shape))
    prev_rows = lambda c: pl.BlockSpec(
        (SUBLANES, d_conv), lambda i, c=c: (jnp.maximum(rev(i) * (tb // SUBLANES) - 1, 0), c))
    n_z = 3 * d_conv + 2 * d_k + 2 * GLA_HEADS * dv
    return pl.pallas_call(
        body, name="mixer_bwd", grid=(nt,),
        in_specs=_z_specs(tb, rev) + [
            prev_rows(1), prev_rows(2),
            pl.BlockSpec((tb, LANES), lambda i: (rev(i), 0)),
            pl.BlockSpec((tb, d_conv + GLA_HEADS * dv), lambda i: (rev(i), 0)),
            pl.BlockSpec((cpt, GLA_HEADS, dv, dk), lambda i: (rev(i), 0, 0, 0)),
            pl.BlockSpec((1, GLA_HEADS, dv, dk), lambda i: (jnp.maximum(rev(i) * cpt - 1, 0), 0, 0, 0)),
            full(wgu.shape), full(b_gate.shape), full(convw.shape), full(conv_g.shape), full(gla_g.shape)],
        out_specs=[pl.BlockSpec((tb, n_z), lambda i: (rev(i), 0)), pl.BlockSpec((tb, LANES), lambda i: (rev(i), 0)),
                   full(convw.shape), full(conv_g.shape), full(gla_g.shape), full(b_gate.shape), full(wgu.shape)],
        out_shape=[jax.ShapeDtypeStruct((t, n_z), BF16), jax.ShapeDtypeStruct((t, LANES), BF16),
                   jax.ShapeDtypeStruct(convw.shape, F32), jax.ShapeDtypeStruct(conv_g.shape, F32),
                   jax.ShapeDtypeStruct(gla_g.shape, F32), jax.ShapeDtypeStruct(b_gate.shape, F32),
                   jax.ShapeDtypeStruct(wgu.shape, F32)],
        scratch_shapes=[pltpu.VMEM((SUBLANES, d_conv), F32), pltpu.VMEM((GLA_HEADS, dv, dk), F32),
                        pltpu.VMEM((tb, d_k), F32), pltpu.VMEM((tb, d_k), F32)],
        compiler_params=_params(1),
    )(z, z, z, z, z, z, z, z, z, alow, dy, sall, sall, wgu, b_gate, convw, conv_g, gla_g)


def _adamw_math(g, w, m, v):
    m = ADAM_B1 * m + (1.0 - ADAM_B1) * g
    v = ADAM_B2 * v + (1.0 - ADAM_B2) * (g * g)
    m_hat = m / (1.0 - ADAM_B1 ** ADAM_STEP)
    v_hat = v / (1.0 - ADAM_B2 ** ADAM_STEP)
    delta = -ADAM_LR * (m_hat / (jnp.sqrt(v_hat) + ADAM_EPS) + ADAM_WD * w)
    return delta, m, v


def _adamw(name, parts, w, m, v, tr):
    r, c = w.shape

    def body(p_ref, w_ref, m_ref, v_ref, g_ref, d_ref, nm_ref, nv_ref):
        g = p_ref[0].astype(F32)
        for j in range(1, N_DEV):
            g = g + p_ref[j].astype(F32)
        g_ref[...] = g
        d_ref[...], nm_ref[...], nv_ref[...] = _adamw_math(g, w_ref[...], m_ref[...], v_ref[...])

    blk = pl.BlockSpec((tr, c), lambda i: (i, 0))
    return pl.pallas_call(
        body, name=name, grid=(r // tr,),
        in_specs=[pl.BlockSpec((N_DEV, tr, c), lambda i: (0, i, 0)), blk, blk, blk],
        out_specs=[blk] * 4, out_shape=[jax.ShapeDtypeStruct((r, c), F32)] * 4,
        compiler_params=_params(1),
    )(parts, w, m, v)


def _pack_rows(vectors, rows):
    flat = jnp.concatenate([a.reshape(-1).astype(F32) for a in vectors])
    return jnp.pad(flat, (0, rows * LANES - flat.shape[0])).reshape(rows, LANES)


def _unpack_rows(block, shapes):
    flat = block.reshape(-1)
    out, off = [], 0
    for s in shapes:
        n = 1
        for dim in s:
            n *= dim
        out.append(flat[off:off + n].reshape(s))
        off += n
    return out


def kernel(x, norm1_g, w_in, w_gate_up, b_gate, conv_w, conv_norm_g, gla_norm_g, w_out, norm2_g, w_ff1, w_ff2, norm_f_g, loss_target, m_norm1_g, m_w_in, m_w_gate_up, m_b_gate, m_conv_w, m_conv_norm_g, m_gla_norm_g, m_w_out, m_norm2_g, m_w_ff1, m_w_ff2, m_norm_f_g, v_norm1_g, v_w_in, v_w_gate_up, v_b_gate, v_conv_w, v_conv_norm_g, v_gla_norm_g, v_w_out, v_norm2_g, v_w_ff1, v_w_ff2, v_norm_f_g):
    me = _device_index()
    x2d, tgt = x[0], loss_target[0]
    t, d = x2d.shape
    d_in_shard = w_in.shape[2]
    d_in = N_DEV * d_in_shard
    n_main = d_in - GATE_RANK
    d_conv = conv_norm_g.shape[1]
    d_k = b_gate.shape[1]
    d_ff = N_DEV * w_ff1.shape[2]

    small_rows = 16
    small_shard = _pack_rows([w_gate_up[0], conv_w[0]], small_rows)
    win_g, wout_g, w1g, w2_g, small_g = _all_gather(
        [w_in[0].astype(BF16), w_out[0].astype(BF16), w_ff1[0].astype(BF16), w_ff2[0].astype(BF16), small_shard])
    w_in_full = win_g.transpose(1, 0, 2).reshape(d, d_in)
    w_main = w_in_full[:, :n_main]
    w_alow = jnp.pad(w_in_full[:, n_main:], ((0, 0), (0, LANES - GATE_RANK)))
    w_out_full = wout_g.reshape(-1, d)
    w2_full = w2_g.reshape(d_ff, d)
    small_flat = small_g.reshape(N_DEV, -1)
    n_wgu = GATE_RANK * (d_k // N_DEV)
    wgu_full = small_flat[:, :n_wgu].reshape(N_DEV, GATE_RANK, d_k // N_DEV).transpose(1, 0, 2).reshape(GATE_RANK, d_k)
    conv_w_full = small_flat[:, n_wgu:n_wgu + (d_conv // N_DEV) * CONV_WIDTH].reshape(d_conv, CONV_WIDTH)
    wgu_pad = jnp.pad(wgu_full, ((0, LANES - GATE_RANK), (0, 0))).astype(BF16)
    convw_taps = jnp.pad(conv_w_full.T, ((0, SUBLANES - CONV_WIDTH), (0, 0)))

    grads = _local_step(x2d, tgt, norm1_g, w_main, w_alow, wgu_pad, b_gate, convw_taps, conv_norm_g, gla_norm_g,
                        w_out_full, norm2_g, w1g, w2_full, norm_f_g)
    grad_x = grads["x"]

    dwin_parts = jnp.concatenate([grads["w_in_main"], grads["w_in_alow"][:, :GATE_RANK]], axis=1).reshape(
        d, N_DEV, d_in_shard).transpose(1, 0, 2)
    small_shapes = [(1, d), (1, d_k), (1, d_conv), (1, gla_norm_g.shape[1]), (1, d), (d,),
                    (GATE_RANK, d_k), (d_conv, CONV_WIDTH), (1,)]
    small_grad_rows = 152
    small_part = _pack_rows(
        [grads["norm1_g"], grads["b_gate"], grads["conv_norm_g"], grads["gla_norm_g"], grads["norm2_g"],
         grads["norm_f_g"], grads["w_gate_up"][:GATE_RANK], grads["conv_w"][:CONV_WIDTH].T, grads["loss"][0, 0]],
        small_grad_rows)
    small_bcast = jnp.broadcast_to(small_part[None], (N_DEV, small_grad_rows, LANES))
    gin_r, gout_r, g1_r, g2_r, small_r = _scatter_partials(
        [dwin_parts, grads["w_out"].reshape(N_DEV, -1, d), grads["w_ff1"], grads["w_ff2"].reshape(N_DEV, -1, d),
         small_bcast])
    return _update(me, gin_r, gout_r, g1_r, g2_r, small_r, small_shapes, grad_x, dict(
        norm1_g=(norm1_g, m_norm1_g, v_norm1_g), w_in=(w_in, m_w_in, v_w_in),
        w_gate_up=(w_gate_up, m_w_gate_up, v_w_gate_up), b_gate=(b_gate, m_b_gate, v_b_gate),
        conv_w=(conv_w, m_conv_w, v_conv_w), conv_norm_g=(conv_norm_g, m_conv_norm_g, v_conv_norm_g),
        gla_norm_g=(gla_norm_g, m_gla_norm_g, v_gla_norm_g), w_out=(w_out, m_w_out, v_w_out),
        norm2_g=(norm2_g, m_norm2_g, v_norm2_g), w_ff1=(w_ff1, m_w_ff1, v_w_ff1), w_ff2=(w_ff2, m_w_ff2, v_w_ff2),
        norm_f_g=(norm_f_g, m_norm_f_g, v_norm_f_g)))


def _local_step(x2d, tgt, norm1_g, w_main, w_alow, wgu_pad, b_gate, convw_taps, conv_norm_g, gla_norm_g,
                w_out_full, norm2_g, w1g, w2_full, norm_f_g):
    t, d = x2d.shape
    n_main = w_main.shape[1]
    d_ff = w2_full.shape[0]

    z, u, alow = _inproj(x2d, norm1_g, w_main, w_alow)
    y, sall = _mixer_fwd(z, alow, wgu_pad, b_gate, convw_taps, conv_norm_g, gla_norm_g)
    x1, h = _outproj(y, w_out_full, x2d, norm2_g)
    a = _ff1(h, w1g)
    x2 = _ff2(a, w2_full, x1)
    dx2, dx2b, loss_part, d_normf = _loss_head(x2, norm_f_g.reshape(1, d), tgt)

    tk = min(2048, t)
    nk = t // tk
    da = _dff2(dx2b, w2_full, a)
    dw2 = _tn_matmul(
        "dw_ff2", a, dx2b, (d_ff // 1024, d // 1024, nk),
        pl.BlockSpec((tk, 1024), lambda m, j, kk: (kk, m)), pl.BlockSpec((tk, 1024), lambda m, j, kk: (kk, j)),
        jax.ShapeDtypeStruct((d_ff, d), BF16), pl.BlockSpec((1024, 1024), lambda m, j, kk: (m, j)), (1024, 1024),
        a_fn=_relu_sq)
    f_shard = d_ff // N_DEV
    dw1 = _tn_matmul(
        "dw_ff1", h, da, (N_DEV, d // 1024, nk),
        pl.BlockSpec((tk, 1024), lambda g, m, kk: (kk, m)), pl.BlockSpec((tk, f_shard), lambda g, m, kk: (kk, g)),
        jax.ShapeDtypeStruct((N_DEV, d, f_shard), BF16), pl.BlockSpec((None, 1024, f_shard), lambda g, m, kk: (g, m, 0)),
        (1024, f_shard))
    dh = _dh(da, w1g)
    dx1, dx1b, d_norm2 = _norm_bwd("norm2_bwd", dh, x1, norm2_g, dx2)
    dy = _nt_matmul("dy", dx1b, w_out_full)
    dwout = _tn_matmul(
        "dw_out", y, dx1b, (d // 1024, d // 1024, nk),
        pl.BlockSpec((tk, 1024), lambda m, j, kk: (kk, m)), pl.BlockSpec((tk, 1024), lambda m, j, kk: (kk, j)),
        jax.ShapeDtypeStruct((d, d), BF16), pl.BlockSpec((1024, 1024), lambda m, j, kk: (m, j)), (1024, 1024))
    dz, dzal, d_convw, d_convg, d_glag, d_bgate, d_wgu = _mixer_bwd(
        z, alow, dy, sall, wgu_pad, b_gate, convw_taps, conv_norm_g, gla_norm_g)
    dwin_main = _tn_matmul(
        "dw_in", u, dz, (d // 1024, n_main // 1024, nk),
        pl.BlockSpec((tk, 1024), lambda m, j, kk: (kk, m)), pl.BlockSpec((tk, 1024), lambda m, j, kk: (kk, j)),
        jax.ShapeDtypeStruct((d, n_main), BF16), pl.BlockSpec((1024, 1024), lambda m, j, kk: (m, j)), (1024, 1024))
    dwin_alow = _tn_matmul(
        "dw_in_alow", u, dzal, (d // 1024, 1, nk),
        pl.BlockSpec((tk, 1024), lambda m, j, kk: (kk, m)), pl.BlockSpec((tk, LANES), lambda m, j, kk: (kk, 0)),
        jax.ShapeDtypeStruct((d, LANES), BF16), pl.BlockSpec((1024, LANES), lambda m, j, kk: (m, 0)), (1024, LANES))
    du = _du(dz, w_main, dzal, w_alow)
    grad_x, _, d_norm1 = _norm_bwd("norm1_bwd", du, x2d, norm1_g, dx1)
    return dict(x=grad_x, loss=loss_part, norm1_g=d_norm1, w_in_main=dwin_main, w_in_alow=dwin_alow, w_gate_up=d_wgu,
                b_gate=d_bgate, conv_w=d_convw, conv_norm_g=d_convg, gla_norm_g=d_glag, w_out=dwout, norm2_g=d_norm2,
                w_ff1=dw1, w_ff2=dw2, norm_f_g=d_normf)


_WEIGHT_ORDER = ("norm1_g", "w_in", "w_gate_up", "b_gate", "conv_w", "conv_norm_g", "gla_norm_g", "w_out", "norm2_g",
                 "w_ff1", "w_ff2", "norm_f_g")
_SMALL_ORDER = ("norm1_g", "b_gate", "conv_norm_g", "gla_norm_g", "norm2_g", "norm_f_g", "w_gate_up", "conv_w")


def _update(me, gin_r, gout_r, g1_r, g2_r, small_r, small_shapes, grad_x, wmv):
    big = {
        "w_in": _adamw("adamw_w_in", gin_r, *(a[0] for a in wmv["w_in"]), 256),
        "w_out": _adamw("adamw_w_out", gout_r, *(a[0] for a in wmv["w_out"]), 128),
        "w_ff1": _adamw("adamw_w_ff1", g1_r, *(a[0] for a in wmv["w_ff1"]), 256),
        "w_ff2": _adamw("adamw_w_ff2", g2_r, *(a[0] for a in wmv["w_ff2"]), 128),
    }

    wgu_cols = wmv["w_gate_up"][0].shape[2]
    cw_rows = wmv["conv_w"][0].shape[1]

    def local_block(flat_block):
        parts = _unpack_rows(flat_block, small_shapes)
        parts[6] = lax.dynamic_slice_in_dim(parts[6], me * wgu_cols, wgu_cols, axis=1)
        parts[7] = lax.dynamic_slice_in_dim(parts[7], me * cw_rows, cw_rows, axis=0)
        return parts

    local_shapes = small_shapes[:6] + [(GATE_RANK, wgu_cols), (cw_rows, CONV_WIDTH), (1,)]
    local_rows = 80
    parts_local = jnp.stack([_pack_rows(local_block(small_r[j]), local_rows) for j in range(N_DEV)])
    extra = (jnp.zeros((1,), F32), jnp.zeros((1,), F32), jnp.ones((1,), F32))
    packed = [_pack_rows([wmv[nm][k] for nm in _SMALL_ORDER] + [extra[k]], local_rows) for k in range(3)]
    out_small = _adamw("adamw_small", parts_local, *packed, local_rows)
    unpacked = [_unpack_rows(o, local_shapes) for o in out_small]

    outs = []
    for k in range(4):
        for nm in _WEIGHT_ORDER:
            if nm in big:
                outs.append(big[nm][k][None])
            else:
                val = unpacked[k][_SMALL_ORDER.index(nm)]
                outs.append(val.reshape(wmv[nm][0].shape))
    loss = unpacked[0][8][0]
    return (loss, grad_x[None], *outs)
```

```python
import functools

import jax
import jax.numpy as jnp
from jax import lax
from jax.experimental import pallas as pl
from jax.experimental.pallas import tpu as pltpu

F32 = jnp.float32
BF16 = jnp.bfloat16

N_DEV = 8
CHUNK = 64
GLA_HEADS = 4
CONV_GROUPS = 8
CONV_WIDTH = 3
GATE_RANK = 16
GATE_NORMALIZER = 16.0
EPS = 1e-6
ADAM_LR = 0.001
ADAM_B1 = 0.9
ADAM_B2 = 0.999
ADAM_EPS = 1e-08
ADAM_WD = 0.01
ADAM_STEP = 10

LANES = 128
SUBLANES = 8
VMEM_LIMIT = 56 << 20

_NN = (((1,), (0,)), ((), ()))
_NT = (((1,), (1,)), ((), ()))
_TN = (((0,), (0,)), ((), ()))


def _dot(a, b, dims=_NN):
    return lax.dot_general(a, b, dims, preferred_element_type=F32)


def _params(n_grid):
    return pltpu.CompilerParams(dimension_semantics=("arbitrary",) * n_grid, vmem_limit_bytes=VMEM_LIMIT)


def _relu_sq(a):
    r = jnp.maximum(a.astype(F32), 0.0)
    return (r * r).astype(BF16)


def _device_index():
    return 4 * lax.axis_index("x") + 2 * lax.axis_index("y") + lax.axis_index("c")


def _peer(mask):
    x, y, c = lax.axis_index("x"), lax.axis_index("y"), lax.axis_index("c")
    return (x ^ ((mask >> 2) & 1), y ^ ((mask >> 1) & 1), c ^ (mask & 1))


_HBM_SPEC = pl.BlockSpec(memory_space=pltpu.HBM)
_SEM_SPEC = pl.BlockSpec(memory_space=pltpu.SEMAPHORE)
_SIDE_EFFECT = pltpu.SideEffectType.DATAFLOW_SIDE_EFFECTING
N_PEERS = N_DEV - 1


def _exchange_copy(src_ref, land_ref, send_sems, recv_sems, mask, scatter, arriving):
    me = _device_index()
    src = src_ref.at[me ^ mask] if scatter else src_ref
    dst = land_ref.at[(me ^ mask) if arriving else me]
    return pltpu.make_async_remote_copy(
        src_ref=src, dst_ref=dst, send_sem=send_sems.at[mask - 1], recv_sem=recv_sems.at[mask - 1],
        device_id=_peer(mask), device_id_type=pl.DeviceIdType.MESH)


def _land_zone(own):
    zone = lax.empty((N_DEV,) + own.shape, own.dtype)
    return lax.dynamic_update_slice(zone, own[None], (_device_index(),) + (0,) * own.ndim)


def _exchange_start(name, srcs, lands, scatter):
    n = len(srcs)

    def body(*refs):
        src, land = refs[:n], refs[n:2 * n]
        send_sems, recv_sems = refs[2 * n:3 * n], refs[3 * n:4 * n]
        token = refs[-1]
        for a in range(n):
            for mask in range(1, N_DEV):
                _exchange_copy(src[a], land[a], send_sems[a], recv_sems[a], mask, scatter, False).start()
        token[...] = jnp.zeros_like(token)

    hbm = lambda a: pltpu.HBM(a.shape, a.dtype)
    outs = pl.pallas_call(
        body, name=name,
        out_shape=([pltpu.SemaphoreType.DMA((N_PEERS,))] * (2 * n) + [hbm(a) for a in srcs] + [hbm(a) for a in lands]
                   + [jax.ShapeDtypeStruct((SUBLANES, LANES), F32)]),
        in_specs=[_HBM_SPEC] * (2 * n),
        out_specs=[_SEM_SPEC] * (2 * n) + [_HBM_SPEC] * (2 * n) + [pl.BlockSpec(memory_space=pltpu.VMEM)],
        input_output_aliases={a: 2 * n + a for a in range(2 * n)},
        compiler_params=pltpu.CompilerParams(has_side_effects=_SIDE_EFFECT),
    )(*[pltpu.with_memory_space_constraint(a, pltpu.HBM) for a in list(srcs) + list(lands)])
    send_sems, recv_sems = outs[:n], outs[n:2 * n]
    src_thru, land_thru = outs[2 * n:3 * n], outs[3 * n:4 * n]
    return send_sems, recv_sems, src_thru, land_thru, outs[-1]


def _exchange_wait(name, send_sems, recv_sems, src_thru, land_thru, after, scatter):
    def body(src_ref, land_ref, send_ref, recv_ref, after_ref, src_dead, got_ref):
        for mask in range(1, N_DEV):
            cp = _exchange_copy(src_ref, land_ref, send_ref, recv_ref, mask, scatter, True)
            cp.wait_send()
            cp.wait_recv()

    return pl.pallas_call(
        body, name=name,
        out_shape=(pltpu.HBM(src_thru.shape, src_thru.dtype), pltpu.HBM(land_thru.shape, land_thru.dtype)),
        in_specs=[_HBM_SPEC, _HBM_SPEC, _SEM_SPEC, _SEM_SPEC, pl.BlockSpec(memory_space=pl.ANY)],
        out_specs=(_HBM_SPEC, _HBM_SPEC), input_output_aliases={0: 0, 1: 1},
        compiler_params=pltpu.CompilerParams(has_side_effects=_SIDE_EFFECT),
    )(src_thru, land_thru, send_sems, recv_sems, after)[1]


def _inproj(x, g1, w_main, w_alow, tm=1024, tn=1024):
    t, d = x.shape
    tm = min(tm, t)
    n = w_main.shape[1]

    def body(x_ref, g_ref, w_ref, wa_ref, z_ref, u_ref, al_ref):
        @pl.when(pl.program_id(1) == 0)
        def _():
            xf = x_ref[...]
            r = lax.rsqrt(jnp.mean(xf * xf, axis=-1, keepdims=True) + EPS)
            u = (xf * r * g_ref[...]).astype(BF16)
            u_ref[...] = u
            al_ref[...] = _dot(u, wa_ref[...])

        z_ref[...] = _dot(u_ref[...], w_ref[...])

    return pl.pallas_call(
        body, name="rmsnorm_inproj", grid=(t // tm, n // tn),
        in_specs=[pl.BlockSpec((tm, d), lambda m, j: (m, 0)), pl.BlockSpec((1, d), lambda m, j: (0, 0)),
                  pl.BlockSpec((d, tn), lambda m, j: (0, j)), pl.BlockSpec((d, LANES), lambda m, j: (0, 0))],
        out_specs=[pl.BlockSpec((tm, tn), lambda m, j: (m, j)), pl.BlockSpec((tm, d), lambda m, j: (m, 0)),
                   pl.BlockSpec((tm, LANES), lambda m, j: (m, 0))],
        out_shape=[jax.ShapeDtypeStruct((t, n), F32), jax.ShapeDtypeStruct((t, d), BF16),
                   jax.ShapeDtypeStruct((t, LANES), F32)],
        compiler_params=_params(2),
    )(x, g1, w_main, w_alow)


def _outproj(y, w_out, x, g2, tm=512):
    t, d = x.shape
    tm = min(tm, t)
    k = y.shape[1]

    def body(y_ref, w_ref, x_ref, g_ref, x1_ref, h_ref):
        x1 = x_ref[...] + _dot(y_ref[...], w_ref[...])
        x1_ref[...] = x1
        r = lax.rsqrt(jnp.mean(x1 * x1, axis=-1, keepdims=True) + EPS)
        h_ref[...] = (x1 * r * g_ref[...]).astype(BF16)

    return pl.pallas_call(
        body, name="outproj_rmsnorm", grid=(t // tm,),
        in_specs=[pl.BlockSpec((tm, k), lambda m: (m, 0)), pl.BlockSpec((k, d), lambda m: (0, 0)),
                  pl.BlockSpec((tm, d), lambda m: (m, 0)), pl.BlockSpec((1, d), lambda m: (0, 0))],
        out_specs=[pl.BlockSpec((tm, d), lambda m: (m, 0)), pl.BlockSpec((tm, d), lambda m: (m, 0))],
        out_shape=[jax.ShapeDtypeStruct((t, d), F32), jax.ShapeDtypeStruct((t, d), BF16)],
        compiler_params=_params(1),
    )(y, w_out, x, g2)


def _ff1(h, w1g, tm=1024):
    t, d = h.shape
    tm = min(tm, t)
    g, _, f = w1g.shape

    def body(h_ref, w_ref, a_ref):
        a_ref[...] = _dot(h_ref[...], w_ref[...]).astype(BF16)

    return pl.pallas_call(
        body, name="ff1", grid=(t // tm, g),
        in_specs=[pl.BlockSpec((tm, d), lambda m, j: (m, 0)), pl.BlockSpec((None, d, f), lambda m, j: (j, 0, 0))],
        out_specs=pl.BlockSpec((tm, f), lambda m, j: (m, j)),
        out_shape=jax.ShapeDtypeStruct((t, g * f), BF16),
        compiler_params=_params(2),
    )(h, w1g)


def _ff2(a, w2, x1, tm=1024, tn=1024, tk=2048):
    t, f = a.shape
    tm = min(tm, t)
    d = w2.shape[1]
    nk = f // tk

    def body(a_ref, w_ref, x1_ref, o_ref, acc_ref):
        kk = pl.program_id(2)

        @pl.when(kk == 0)
        def _():
            acc_ref[...] = x1_ref[...]

        acc_ref[...] += _dot(_relu_sq(a_ref[...]), w_ref[...])

        @pl.when(kk == nk - 1)
        def _():
            o_ref[...] = acc_ref[...]

    return pl.pallas_call(
        body, name="ff2_residual", grid=(t // tm, d // tn, nk),
        in_specs=[pl.BlockSpec((tm, tk), lambda m, j, kk: (m, kk)), pl.BlockSpec((tk, tn), lambda m, j, kk: (kk, j)),
                  pl.BlockSpec((tm, tn), lambda m, j, kk: (m, j))],
        out_specs=pl.BlockSpec((tm, tn), lambda m, j, kk: (m, j)),
        out_shape=jax.ShapeDtypeStruct((t, d), F32),
        scratch_shapes=[pltpu.VMEM((tm, tn), F32)],
        compiler_params=_params(3),
    )(a, w2, x1)


def _dff2(dx2b, w2, a, tm=1024, tn=1024):
    t, d = dx2b.shape
    tm = min(tm, t)
    f = w2.shape[0]

    def body(g_ref, w_ref, a_ref, o_ref):
        dp = _dot(g_ref[...], w_ref[...], _NT)
        o_ref[...] = (dp * (2.0 * jnp.maximum(a_ref[...].astype(F32), 0.0))).astype(BF16)

    return pl.pallas_call(
        body, name="dff2", grid=(t // tm, f // tn),
        in_specs=[pl.BlockSpec((tm, d), lambda m, j: (m, 0)), pl.BlockSpec((tn, d), lambda m, j: (j, 0)),
                  pl.BlockSpec((tm, tn), lambda m, j: (m, j))],
        out_specs=pl.BlockSpec((tm, tn), lambda m, j: (m, j)),
        out_shape=jax.ShapeDtypeStruct((t, f), BF16),
        compiler_params=_params(2),
    )(dx2b, w2, a)


def _behind(token):
    if token is None:
        return [], []
    return [token], [pl.BlockSpec(token.shape, lambda *_: (0,) * token.ndim)]


def _tn_matmul(name, a, b, grid, a_spec, b_spec, out_shape, out_spec, acc_shape, a_fn=None, behind=None):
    nk = grid[-1]
    dep_args, dep_specs = _behind(behind)

    def body(a_ref, b_ref, *rest):
        o_ref, acc_ref = rest[-2:]
        kk = pl.program_id(len(grid) - 1)
        av = a_ref[...]
        if a_fn is not None:
            av = a_fn(av)
        part = _dot(av, b_ref[...], _TN)

        @pl.when(kk == 0)
        def _():
            acc_ref[...] = part

        @pl.when(kk > 0)
        def _():
            acc_ref[...] += part

        @pl.when(kk == nk - 1)
        def _():
            o_ref[...] = acc_ref[...].astype(o_ref.dtype)

    return pl.pallas_call(
        body, name=name, grid=grid, in_specs=[a_spec, b_spec] + dep_specs, out_specs=out_spec, out_shape=out_shape,
        scratch_shapes=[pltpu.VMEM(acc_shape, F32)], compiler_params=_params(len(grid)),
    )(a, b, *dep_args)


def _dh(da, w1g, tm=1024, tn=1024, behind=None):
    t = da.shape[0]
    tm = min(tm, t)
    g, d, f = w1g.shape
    dep_args, dep_specs = _behind(behind)

    def body(a_ref, w_ref, *rest):
        o_ref, acc_ref = rest[-2:]
        kk = pl.program_id(2)
        part = _dot(a_ref[...], w_ref[...], _NT)

        @pl.when(kk == 0)
        def _():
            acc_ref[...] = part

        @pl.when(kk > 0)
        def _():
            acc_ref[...] += part

        @pl.when(kk == g - 1)
        def _():
            o_ref[...] = acc_ref[...]

    return pl.pallas_call(
        body, name="dh", grid=(t // tm, d // tn, g),
        in_specs=[pl.BlockSpec((tm, f), lambda m, j, kk: (m, kk)),
                  pl.BlockSpec((None, tn, f), lambda m, j, kk: (kk, j, 0))] + dep_specs,
        out_specs=pl.BlockSpec((tm, tn), lambda m, j, kk: (m, j)),
        out_shape=jax.ShapeDtypeStruct((t, d), F32),
        scratch_shapes=[pltpu.VMEM((tm, tn), F32)],
        compiler_params=_params(3),
    )(da, w1g, *dep_args)


def _nt_matmul(name, a, b, tm=1024, tn=1024):
    t, k = a.shape
    tm = min(tm, t)
    n = b.shape[0]

    def body(a_ref, b_ref, o_ref):
        o_ref[...] = _dot(a_ref[...], b_ref[...], _NT)

    return pl.pallas_call(
        body, name=name, grid=(t // tm, n // tn),
        in_specs=[pl.BlockSpec((tm, k), lambda m, j: (m, 0)), pl.BlockSpec((tn, k), lambda m, j: (j, 0))],
        out_specs=pl.BlockSpec((tm, tn), lambda m, j: (m, j)),
        out_shape=jax.ShapeDtypeStruct((t, n), F32),
        compiler_params=_params(2),
    )(a, b)


def _du(dz, w_main, dzal, w_alow, tm=1024, tn=1024, tk=2048, behind=None):
    t, n = dz.shape
    tm = min(tm, t)
    d = w_main.shape[0]
    nk = n // tk
    dep_args, dep_specs = _behind(behind)

    def body(a_ref, w_ref, al_ref, wa_ref, *rest):
        o_ref, acc_ref = rest[-2:]
        kk = pl.program_id(2)

        @pl.when(kk == 0)
        def _():
            acc_ref[...] = _dot(al_ref[...], wa_ref[...], _NT)

        acc_ref[...] += _dot(a_ref[...], w_ref[...], _NT)

        @pl.when(kk == nk - 1)
        def _():
            o_ref[...] = acc_ref[...]

    return pl.pallas_call(
        body, name="du", grid=(t // tm, d // tn, nk),
        in_specs=[pl.BlockSpec((tm, tk), lambda m, j, kk: (m, kk)), pl.BlockSpec((tn, tk), lambda m, j, kk: (j, kk)),
                  pl.BlockSpec((tm, LANES), lambda m, j, kk: (m, 0)), pl.BlockSpec((tn, LANES), lambda m, j, kk: (j, 0))]
        + dep_specs,
        out_specs=pl.BlockSpec((tm, tn), lambda m, j, kk: (m, j)),
        out_shape=jax.ShapeDtypeStruct((t, d), F32),
        scratch_shapes=[pltpu.VMEM((tm, tn), F32)],
        compiler_params=_params(3),
    )(dz, w_main, dzal, w_alow, *dep_args)


def _loss_head(x2, gf, tgt, tr=256):
    t, d = x2.shape

    def body(x_ref, g_ref, t_ref, dx_ref, dxb_ref, loss_ref, dg_ref):
        @pl.when(pl.program_id(0) == 0)
        def _():
            loss_ref[...] = jnp.zeros_like(loss_ref)
            dg_ref[...] = jnp.zeros_like(dg_ref)

        xf = x_ref[...]
        g = g_ref[...]
        r = lax.rsqrt(jnp.mean(xf * xf, axis=-1, keepdims=True) + EPS)
        xh = xf * r
        e = xh * g - t_ref[...]
        loss_ref[...] += 0.5 * jnp.sum(jnp.mean(e * e, axis=-1, keepdims=True))
        dy = e * (1.0 / d)
        dg_ref[...] += jnp.sum(dy * xh, axis=0, keepdims=True)
        dyg = dy * g
        dx = r * (dyg - xh * jnp.mean(dyg * xh, axis=-1, keepdims=True))
        dx_ref[...] = dx
        dxb_ref[...] = dx.astype(BF16)

    return pl.pallas_call(
        body, name="loss_head", grid=(t // tr,),
        in_specs=[pl.BlockSpec((tr, d), lambda i: (i, 0)), pl.BlockSpec((1, d), lambda i: (0, 0)),
                  pl.BlockSpec((tr, d), lambda i: (i, 0))],
        out_specs=[pl.BlockSpec((tr, d), lambda i: (i, 0)), pl.BlockSpec((tr, d), lambda i: (i, 0)),
                   pl.BlockSpec((SUBLANES, LANES), lambda i: (0, 0)), pl.BlockSpec((1, d), lambda i: (0, 0))],
        out_shape=[jax.ShapeDtypeStruct((t, d), F32), jax.ShapeDtypeStruct((t, d), BF16),
                   jax.ShapeDtypeStruct((SUBLANES, LANES), F32), jax.ShapeDtypeStruct((1, d), F32)],
        compiler_params=_params(1),
    )(x2, gf, tgt)


def _norm_bwd(name, dh, xin, g, dres, tr=256):
    t, d = xin.shape

    def body(dh_ref, x_ref, g_ref, dr_ref, dx_ref, dxb_ref, dg_ref):
        @pl.when(pl.program_id(0) == 0)
        def _():
            dg_ref[...] = jnp.zeros_like(dg_ref)

        xf = x_ref[...]
        dhv = dh_ref[...]
        r = lax.rsqrt(jnp.mean(xf * xf, axis=-1, keepdims=True) + EPS)
        xh = xf * r
        dg_ref[...] += jnp.sum(dhv * xh, axis=0, keepdims=True)
        dyg = dhv * g_ref[...]
        dx = dr_ref[...] + r * (dyg - xh * jnp.mean(dyg * xh, axis=-1, keepdims=True))
        dx_ref[...] = dx
        dxb_ref[...] = dx.astype(BF16)

    return pl.pallas_call(
        body, name=name, grid=(t // tr,),
        in_specs=[pl.BlockSpec((tr, d), lambda i: (i, 0)), pl.BlockSpec((tr, d), lambda i: (i, 0)),
                  pl.BlockSpec((1, d), lambda i: (0, 0)), pl.BlockSpec((tr, d), lambda i: (i, 0))],
        out_specs=[pl.BlockSpec((tr, d), lambda i: (i, 0)), pl.BlockSpec((tr, d), lambda i: (i, 0)),
                   pl.BlockSpec((1, d), lambda i: (0, 0))],
        out_shape=[jax.ShapeDtypeStruct((t, d), F32), jax.ShapeDtypeStruct((t, d), BF16),
                   jax.ShapeDtypeStruct((1, d), F32)],
        compiler_params=_params(1),
    )(dh, xin, g, dres)


MIX_TILE = 256
CHUNKS_PER_TILE = MIX_TILE // CHUNK
CHUNK_SHIFT = CHUNK.bit_length() - 1
assert 1 << CHUNK_SHIFT == CHUNK


def _chunk_masks(n):
    row = lax.broadcasted_iota(jnp.int32, (n, n), 0)
    col = lax.broadcasted_iota(jnp.int32, (n, n), 1)
    same = lax.shift_right_logical(row, CHUNK_SHIFT) == lax.shift_right_logical(col, CHUNK_SHIFT)
    one = lambda m: jnp.where(m, 1.0, 0.0).astype(BF16)
    return one(same & (col > row)), one(same), one(same & (col < row))


def _mask_dot(mask, x):
    hi = x.astype(BF16)
    r1 = x - hi.astype(F32)
    mid = r1.astype(BF16)
    lo = (r1 - mid.astype(F32)).astype(BF16)
    return _dot(mask, hi) + _dot(mask, mid) + _dot(mask, lo)


def _log_sigmoid(x):
    return jnp.minimum(x, 0.0) - jnp.log1p(jnp.exp(-jnp.abs(x)))


def _conv_taps(prev8, uc, w):
    ext = jnp.concatenate([prev8, uc], axis=0)
    s1 = pltpu.roll(ext, 1, 0)[SUBLANES:]
    s2 = pltpu.roll(ext, 2, 0)[SUBLANES:]
    return s2 * w[0:1] + s1 * w[1:2] + uc * w[2:3], s1, s2


def _z_specs(tile, idx):
    d_conv = 1024
    wide = lambda c: pl.BlockSpec((tile, d_conv), lambda i, c=c: (idx(i), c))
    half = lambda c: pl.BlockSpec((tile, d_conv // 2), lambda i, c=c: (idx(i), c))
    return [wide(0), wide(1), wide(2), half(6), half(7), wide(4), wide(5)]


def _mixer_fwd(z, alow, wgu, b_gate, convw, conv_g, gla_g):
    t = z.shape[0]
    tb, cpt = MIX_TILE, CHUNKS_PER_TILE
    d_conv = conv_g.shape[1]
    dv = gla_g.shape[1]
    dk = dv // 2
    d_k = GLA_HEADS * dk
    gw = d_conv // CONV_GROUPS
    scale = dk ** -0.5

    def body(cb_ref, cc_ref, ch_ref, q_ref, k_ref, v_ref, og_ref, al_ref, wgu_ref, bg_ref, cw_ref, cg_ref, gg_ref,
             y_ref, sall_ref, carry_ref, s_ref):
        @pl.when(pl.program_id(0) == 0)
        def _():
            carry_ref[...] = jnp.zeros_like(carry_ref)
            s_ref[...] = jnp.zeros_like(s_ref)

        uc = cc_ref[...] * ch_ref[...]
        conv, _, _ = _conv_taps(carry_ref[...], uc, cw_ref[...])
        carry_ref[...] = uc[tb - SUBLANES:]
        ypre = cb_ref[...] * conv
        cg = cg_ref[...]
        for g in range(CONV_GROUPS):
            sl = slice(g * gw, (g + 1) * gw)
            seg = ypre[:, sl]
            r = lax.rsqrt(jnp.mean(seg * seg, axis=-1, keepdims=True) + EPS)
            y_ref[:, sl] = (seg * r * cg[:, sl]).astype(BF16)

        later, same, _ = _chunk_masks(tb)
        pre = _dot(al_ref[...].astype(BF16), wgu_ref[...]) + bg_ref[...]
        la = _log_sigmoid(pre) * (1.0 / GATE_NORMALIZER)
        e_dec = _mask_dot(later, la)
        dec_all = jnp.exp(_mask_dot(same, la))
        kdec = (k_ref[...] * jnp.exp(e_dec)).astype(BF16)
        qs = (q_ref[...] * scale).astype(BF16)
        vb = v_ref[...].astype(BF16)
        og = og_ref[...]
        gg = gg_ref[...]
        for c in range(cpt):
            rows = slice(c * CHUNK, (c + 1) * CHUNK)
            for h in range(GLA_HEADS):
                ks = slice(h * dk, (h + 1) * dk)
                vs = slice(h * dv, (h + 1) * dv)
                kvt = _dot(vb[rows, vs], kdec[rows, ks], _TN)
                st = s_ref[h] * dec_all[c * CHUNK:c * CHUNK + 1, ks] + kvt
                s_ref[h] = st
                sall_ref[c, h] = st
                o = _dot(qs[rows, ks], st.astype(BF16), _NT)
                ro = lax.rsqrt(jnp.mean(o * o, axis=-1, keepdims=True) + EPS)
                ogs = og[rows, vs]
                yg = o * ro * gg * (ogs * jax.nn.sigmoid(ogs))
                y_ref[rows, d_conv + h * dv:d_conv + (h + 1) * dv] = yg.astype(BF16)

    full = lambda shape: pl.BlockSpec(shape, lambda i: (0,) * len(shape))
    return pl.pallas_call(
        body, name="mixer_fwd", grid=(t // tb,),
        in_specs=_z_specs(tb, lambda i: i) + [
            pl.BlockSpec((tb, LANES), lambda i: (i, 0)), full(wgu.shape), full(b_gate.shape), full(convw.shape),
            full(conv_g.shape), full(gla_g.shape)],
        out_specs=[pl.BlockSpec((tb, d_conv + GLA_HEADS * dv), lambda i: (i, 0)),
                   pl.BlockSpec((cpt, GLA_HEADS, dv, dk), lambda i: (i, 0, 0, 0))],
        out_shape=[jax.ShapeDtypeStruct((t, d_conv + GLA_HEADS * dv), BF16),
                   jax.ShapeDtypeStruct((t // CHUNK, GLA_HEADS, dv, dk), F32)],
        scratch_shapes=[pltpu.VMEM((SUBLANES, d_conv), F32), pltpu.VMEM((GLA_HEADS, dv, dk), F32)],
        compiler_params=_params(1),
    )(z, z, z, z, z, z, z, alow, wgu, b_gate, convw, conv_g, gla_g)


def _mixer_bwd(z, alow, dy, sall, wgu, b_gate, convw, conv_g, gla_g, behind=None):
    t = z.shape[0]
    tb, cpt = MIX_TILE, CHUNKS_PER_TILE
    nt = t // tb
    d_conv = conv_g.shape[1]
    dv = gla_g.shape[1]
    dk = dv // 2
    d_k = GLA_HEADS * dk
    gw = d_conv // CONV_GROUPS
    scale = dk ** -0.5
    rev = lambda i: nt - 1 - i
    dep_args, dep_specs = _behind(behind)

    def body(cb_ref, cc_ref, ch_ref, q_ref, k_ref, v_ref, og_ref, ccp_ref, chp_ref, al_ref, dy_ref, sall_ref, sprev_ref,
             wgu_ref, bg_ref, cw_ref, cg_ref, gg_ref, *rest):
        (dz_ref, dzal_ref, dcw_ref, dcg_ref, dgg_ref, dbg_ref, dwgu_ref,
         dcarry_ref, gd_ref, de_ref, ddd_ref) = rest[-11:]
        i = pl.program_id(0)

        @pl.when(i == 0)
        def _():
            dcarry_ref[...] = jnp.zeros_like(dcarry_ref)
            gd_ref[...] = jnp.zeros_like(gd_ref)
            dcw_ref[...] = jnp.zeros_like(dcw_ref)
            dcg_ref[...] = jnp.zeros_like(dcg_ref)
            dgg_ref[...] = jnp.zeros_like(dgg_ref)
            dbg_ref[...] = jnp.zeros_like(dbg_ref)
            dwgu_ref[...] = jnp.zeros_like(dwgu_ref)

        first = rev(i) == 0

        cb, cc, ch = cb_ref[...], cc_ref[...], ch_ref[...]
        w = cw_ref[...]
        uc = cc * ch
        prev8 = jnp.where(first, 0.0, ccp_ref[...] * chp_ref[...])
        conv, s1, s2 = _conv_taps(prev8, uc, w)
        ypre = cb * conv
        cg = cg_ref[...]
        dypre_parts = []
        for g in range(CONV_GROUPS):
            sl = slice(g * gw, (g + 1) * gw)
            seg = ypre[:, sl]
            r = lax.rsqrt(jnp.mean(seg * seg, axis=-1, keepdims=True) + EPS)
            yn = seg * r
            dyc = dy_ref[:, sl]
            dcg_ref[:, sl] += jnp.sum(dyc * yn, axis=0, keepdims=True)
            dyn = dyc * cg[:, sl]
            dypre_parts.append(r * (dyn - yn * jnp.mean(dyn * yn, axis=-1, keepdims=True)))
        dypre = jnp.concatenate(dypre_parts, axis=1)
        dconv = dypre * cb
        dz_ref[:, 0:d_conv] = (dypre * conv).astype(BF16)
        dcw_ref[0:1] += jnp.sum(dconv * s2, axis=0, keepdims=True)
        dcw_ref[1:2] += jnp.sum(dconv * s1, axis=0, keepdims=True)
        dcw_ref[2:3] += jnp.sum(dconv * uc, axis=0, keepdims=True)
        ext = jnp.concatenate([dconv, dcarry_ref[...]], axis=0)
        f1 = pltpu.roll(ext, tb + SUBLANES - 1, 0)[:tb]
        f2 = pltpu.roll(ext, tb + SUBLANES - 2, 0)[:tb]
        dcarry_ref[...] = dconv[:SUBLANES]
        duc = dconv * w[2:3] + f1 * w[1:2] + f2 * w[0:1]
        dz_ref[:, d_conv:2 * d_conv] = (duc * ch).astype(BF16)
        dz_ref[:, 2 * d_conv:3 * d_conv] = (duc * cc).astype(BF16)

        q_off = 3 * d_conv
        k_off = q_off + d_k
        v_off = k_off + d_k
        og_off = v_off + GLA_HEADS * dv
        later, same, earlier = _chunk_masks(tb)
        alb = al_ref[...].astype(BF16)
        pre = _dot(alb, wgu_ref[...]) + bg_ref[...]
        la = _log_sigmoid(pre) * (1.0 / GATE_NORMALIZER)
        exp_e = jnp.exp(_mask_dot(later, la))
        dec_all = jnp.exp(_mask_dot(same, la))
        kdec = k_ref[...] * exp_e
        kdec_b = kdec.astype(BF16)
        qs = (q_ref[...] * scale).astype(BF16)
        vb = v_ref[...].astype(BF16)
        og = og_ref[...]
        gg = gg_ref[...]
        for c in reversed(range(cpt)):
            rows = slice(c * CHUNK, (c + 1) * CHUNK)
            for h in range(GLA_HEADS):
                ks = slice(h * dk, (h + 1) * dk)
                vs = slice(h * dv, (h + 1) * dv)
                st = sall_ref[c, h]
                if c > 0:
                    st_prev = sall_ref[c - 1, h]
                else:
                    st_prev = jnp.where(first, 0.0, sprev_ref[0, h])
                st_b = st.astype(BF16)
                o = _dot(qs[rows, ks], st_b, _NT)
                ro = lax.rsqrt(jnp.mean(o * o, axis=-1, keepdims=True) + EPS)
                on = o * ro
                ogs = og[rows, vs]
                sg = jax.nn.sigmoid(ogs)
                gate = ogs * sg
                dyg = dy_ref[rows, d_conv + h * dv:d_conv + (h + 1) * dv]
                dgg_ref[...] += jnp.sum(dyg * on * gate, axis=0, keepdims=True)
                dz_ref[rows, og_off + h * dv:og_off + (h + 1) * dv] = (
                    dyg * on * gg * (sg * (1.0 + ogs * (1.0 - sg)))).astype(BF16)
                don = dyg * gg * gate
                do = (ro * (don - on * jnp.mean(don * on, axis=-1, keepdims=True))).astype(BF16)
                dz_ref[rows, q_off + h * dk:q_off + (h + 1) * dk] = (_dot(do, st_b) * scale).astype(BF16)
                gt = _dot(do, qs[rows, ks], _TN) + gd_ref[h]
                dec = dec_all[c * CHUNK:c * CHUNK + 1, ks]
                ddec = jnp.sum(gt * st_prev, axis=0, keepdims=True)
                gd_ref[h] = gt * dec
                gt_b = gt.astype(BF16)
                dkdec = _dot(vb[rows, vs], gt_b)
                dz_ref[rows, v_off + h * dv:v_off + (h + 1) * dv] = _dot(kdec_b[rows, ks], gt_b, _NT).astype(BF16)
                dz_ref[rows, k_off + h * dk:k_off + (h + 1) * dk] = (dkdec * exp_e[rows, ks]).astype(BF16)
                de_ref[rows, ks] = dkdec * kdec[rows, ks]
                ddd_ref[rows, ks] = jnp.broadcast_to(ddec * dec, (CHUNK, dk))
        dla = _mask_dot(earlier, de_ref[...]) + ddd_ref[...]
        dpre = dla * (1.0 / GATE_NORMALIZER) * jax.nn.sigmoid(-pre)
        dbg_ref[...] += jnp.sum(dpre, axis=0, keepdims=True)
        dpre_b = dpre.astype(BF16)
        dwgu_ref[...] += _dot(alb, dpre_b, _TN)
        dzal_ref[...] = _dot(dpre_b, wgu_ref[...], _NT).astype(BF16)

    full = lambda shape: pl.BlockSpec(shape, lambda i: (0,) * len(shape))
    prev_rows = lambda c: pl.BlockSpec(
        (SUBLANES, d_conv), lambda i, c=c: (jnp.maximum(rev(i) * (tb // SUBLANES) - 1, 0), c))
    n_z = 3 * d_conv + 2 * d_k + 2 * GLA_HEADS * dv
    return pl.pallas_call(
        body, name="mixer_bwd", grid=(nt,),
        in_specs=_z_specs(tb, rev) + [
            prev_rows(1), prev_rows(2),
            pl.BlockSpec((tb, LANES), lambda i: (rev(i), 0)),
            pl.BlockSpec((tb, d_conv + GLA_HEADS * dv), lambda i: (rev(i), 0)),
            pl.BlockSpec((cpt, GLA_HEADS, dv, dk), lambda i: (rev(i), 0, 0, 0)),
            pl.BlockSpec((1, GLA_HEADS, dv, dk), lambda i: (jnp.maximum(rev(i) * cpt - 1, 0), 0, 0, 0)),
            full(wgu.shape), full(b_gate.shape), full(convw.shape), full(conv_g.shape), full(gla_g.shape)]
        + dep_specs,
        out_specs=[pl.BlockSpec((tb, n_z), lambda i: (rev(i), 0)), pl.BlockSpec((tb, LANES), lambda i: (rev(i), 0)),
                   full(convw.shape), full(conv_g.shape), full(gla_g.shape), full(b_gate.shape), full(wgu.shape)],
        out_shape=[jax.ShapeDtypeStruct((t, n_z), BF16), jax.ShapeDtypeStruct((t, LANES), BF16),
                   jax.ShapeDtypeStruct(convw.shape, F32), jax.ShapeDtypeStruct(conv_g.shape, F32),
                   jax.ShapeDtypeStruct(gla_g.shape, F32), jax.ShapeDtypeStruct(b_gate.shape, F32),
                   jax.ShapeDtypeStruct(wgu.shape, F32)],
        scratch_shapes=[pltpu.VMEM((SUBLANES, d_conv), F32), pltpu.VMEM((GLA_HEADS, dv, dk), F32),
                        pltpu.VMEM((tb, d_k), F32), pltpu.VMEM((tb, d_k), F32)],
        compiler_params=_params(1),
    )(z, z, z, z, z, z, z, z, z, alow, dy, sall, sall, wgu, b_gate, convw, conv_g, gla_g, *dep_args)


def _adamw_math(g, w, m, v):
    m = ADAM_B1 * m + (1.0 - ADAM_B1) * g
    v = ADAM_B2 * v + (1.0 - ADAM_B2) * (g * g)
    m_hat = m / (1.0 - ADAM_B1 ** ADAM_STEP)
    v_hat = v / (1.0 - ADAM_B2 ** ADAM_STEP)
    delta = -ADAM_LR * (m_hat / (jnp.sqrt(v_hat) + ADAM_EPS) + ADAM_WD * w)
    return delta, m, v


def _adamw(name, parts, w, m, v, tr):
    r, c = w.shape

    def body(p_ref, w_ref, m_ref, v_ref, g_ref, d_ref, nm_ref, nv_ref):
        g = p_ref[0].astype(F32)
        for j in range(1, N_DEV):
            g = g + p_ref[j].astype(F32)
        g_ref[...] = g
        d_ref[...], nm_ref[...], nv_ref[...] = _adamw_math(g, w_ref[...], m_ref[...], v_ref[...])

    blk = pl.BlockSpec((tr, c), lambda i: (i, 0))
    return pl.pallas_call(
        body, name=name, grid=(r // tr,),
        in_specs=[pl.BlockSpec((N_DEV, tr, c), lambda i: (0, i, 0)), blk, blk, blk],
        out_specs=[blk] * 4, out_shape=[jax.ShapeDtypeStruct((r, c), F32)] * 4,
        compiler_params=_params(1),
    )(parts, w, m, v)


def _pack_rows(vectors, rows):
    flat = jnp.concatenate([a.reshape(-1).astype(F32) for a in vectors])
    return jnp.pad(flat, (0, rows * LANES - flat.shape[0])).reshape(rows, LANES)


def _unpack_rows(block, shapes):
    flat = block.reshape(-1)
    out, off = [], 0
    for s in shapes:
        n = 1
        for dim in s:
            n *= dim
        out.append(flat[off:off + n].reshape(s))
        off += n
    return out


def kernel(x, norm1_g, w_in, w_gate_up, b_gate, conv_w, conv_norm_g, gla_norm_g, w_out, norm2_g, w_ff1, w_ff2, norm_f_g, loss_target, m_norm1_g, m_w_in, m_w_gate_up, m_b_gate, m_conv_w, m_conv_norm_g, m_gla_norm_g, m_w_out, m_norm2_g, m_w_ff1, m_w_ff2, m_norm_f_g, v_norm1_g, v_w_in, v_w_gate_up, v_b_gate, v_conv_w, v_conv_norm_g, v_gla_norm_g, v_w_out, v_norm2_g, v_w_ff1, v_w_ff2, v_norm_f_g):
    me = _device_index()
    x2d, tgt = x[0], loss_target[0]
    t, d = x2d.shape
    d_in_shard = w_in.shape[2]
    d_in = N_DEV * d_in_shard
    n_main = d_in - GATE_RANK
    d_conv = conv_norm_g.shape[1]
    d_k = b_gate.shape[1]
    d_ff = N_DEV * w_ff1.shape[2]

    small_rows = 16
    small_shard = _pack_rows([w_gate_up[0], conv_w[0]], small_rows)
    shards = [small_shard, w_in[0].astype(BF16), w_out[0].astype(BF16), w_ff1[0].astype(BF16), w_ff2[0].astype(BF16)]
    ag_send, ag_recv, ag_src, ag_land, _ = _exchange_start(
        "all_gather_start", shards, [_land_zone(s) for s in shards], scatter=False)

    def gathered(k, name, after):
        return _exchange_wait(name, ag_send[k], ag_recv[k], ag_src[k], ag_land[k], after, scatter=False)

    small_g = gathered(0, "all_gather_wait_small", x2d)
    win_g = gathered(1, "all_gather_wait_w_in", small_g)
    w_in_full = win_g.transpose(1, 0, 2).reshape(d, d_in)
    w_main = w_in_full[:, :n_main]
    w_alow = jnp.pad(w_in_full[:, n_main:], ((0, 0), (0, LANES - GATE_RANK)))
    small_flat = small_g.reshape(N_DEV, -1)
    n_wgu = GATE_RANK * (d_k // N_DEV)
    wgu_full = small_flat[:, :n_wgu].reshape(N_DEV, GATE_RANK, d_k // N_DEV).transpose(1, 0, 2).reshape(GATE_RANK, d_k)
    conv_w_full = small_flat[:, n_wgu:n_wgu + (d_conv // N_DEV) * CONV_WIDTH].reshape(d_conv, CONV_WIDTH)
    wgu_pad = jnp.pad(wgu_full, ((0, LANES - GATE_RANK), (0, 0))).astype(BF16)
    convw_taps = jnp.pad(conv_w_full.T, ((0, SUBLANES - CONV_WIDTH), (0, 0)))

    get_w_out = lambda after: gathered(2, "all_gather_wait_w_out", after).reshape(-1, d)
    get_w1 = lambda after: gathered(3, "all_gather_wait_w_ff1", after)
    get_w2 = lambda after: gathered(4, "all_gather_wait_w_ff2", after).reshape(d_ff, d)

    in_flight = {}

    def send_partials(name, parts):
        own = lax.dynamic_index_in_dim(parts, me, axis=0, keepdims=False)
        send, recv, src, land, token = _exchange_start("scatter_start_" + name, [parts], [_land_zone(own)], scatter=True)
        in_flight[name] = (send[0], recv[0], src[0], land[0])
        return token

    def on_grad(name, value):
        if name == "w_in":
            main, alow_part = value
            value = jnp.concatenate([main, alow_part[:, :GATE_RANK]], axis=1).reshape(
                d, N_DEV, d_in_shard).transpose(1, 0, 2)
        elif name in ("w_out", "w_ff2"):
            value = value.reshape(N_DEV, -1, d)
        return send_partials(name, value)

    grads = _local_step(x2d, tgt, norm1_g, w_main, w_alow, wgu_pad, b_gate, convw_taps, conv_norm_g, gla_norm_g,
                        norm2_g, norm_f_g, get_w_out, get_w1, get_w2, on_grad)
    grad_x = grads["x"]

    small_shapes = [(1, d), (1, d_k), (1, d_conv), (1, gla_norm_g.shape[1]), (1, d), (d,),
                    (GATE_RANK, d_k), (d_conv, CONV_WIDTH), (1,)]
    small_grad_rows = 152
    small_part = _pack_rows(
        [grads["norm1_g"], grads["b_gate"], grads["conv_norm_g"], grads["gla_norm_g"], grads["norm2_g"],
         grads["norm_f_g"], grads["w_gate_up"][:GATE_RANK], grads["conv_w"][:CONV_WIDTH].T, grads["loss"][0, 0]],
        small_grad_rows)
    send_partials("small", jnp.broadcast_to(small_part[None], (N_DEV, small_grad_rows, LANES)))

    def received(name):
        send, recv, src, land = in_flight[name]
        return _exchange_wait("scatter_wait_" + name, send, recv, src, land, grad_x, scatter=True)

    small_r = received("small")
    gin_r, gout_r, g1_r, g2_r = received("w_in"), received("w_out"), received("w_ff1"), received("w_ff2")
    return _update(me, gin_r, gout_r, g1_r, g2_r, small_r, small_shapes, grad_x, dict(
        norm1_g=(norm1_g, m_norm1_g, v_norm1_g), w_in=(w_in, m_w_in, v_w_in),
        w_gate_up=(w_gate_up, m_w_gate_up, v_w_gate_up), b_gate=(b_gate, m_b_gate, v_b_gate),
        conv_w=(conv_w, m_conv_w, v_conv_w), conv_norm_g=(conv_norm_g, m_conv_norm_g, v_conv_norm_g),
        gla_norm_g=(gla_norm_g, m_gla_norm_g, v_gla_norm_g), w_out=(w_out, m_w_out, v_w_out),
        norm2_g=(norm2_g, m_norm2_g, v_norm2_g), w_ff1=(w_ff1, m_w_ff1, v_w_ff1), w_ff2=(w_ff2, m_w_ff2, v_w_ff2),
        norm_f_g=(norm_f_g, m_norm_f_g, v_norm_f_g)))


def _local_step(x2d, tgt, norm1_g, w_main, w_alow, wgu_pad, b_gate, convw_taps, conv_norm_g, gla_norm_g,
                norm2_g, norm_f_g, get_w_out, get_w1, get_w2, on_grad):
    t, d = x2d.shape
    n_main = w_main.shape[1]

    z, u, alow = _inproj(x2d, norm1_g, w_main, w_alow)
    y, sall = _mixer_fwd(z, alow, wgu_pad, b_gate, convw_taps, conv_norm_g, gla_norm_g)
    w_out_full = get_w_out(y)
    x1, h = _outproj(y, w_out_full, x2d, norm2_g)
    w1g = get_w1(h)
    a = _ff1(h, w1g)
    w2_full = get_w2(a)
    d_ff = w2_full.shape[0]
    x2 = _ff2(a, w2_full, x1)
    dx2, dx2b, loss_part, d_normf = _loss_head(x2, norm_f_g.reshape(1, d), tgt)

    tk = min(2048, t)
    nk = t // tk
    da = _dff2(dx2b, w2_full, a)
    dw2 = _tn_matmul(
        "dw_ff2", a, dx2b, (d_ff // 1024, d // 1024, nk),
        pl.BlockSpec((tk, 1024), lambda m, j, kk: (kk, m)), pl.BlockSpec((tk, 1024), lambda m, j, kk: (kk, j)),
        jax.ShapeDtypeStruct((d_ff, d), BF16), pl.BlockSpec((1024, 1024), lambda m, j, kk: (m, j)), (1024, 1024),
        a_fn=_relu_sq)
    token = on_grad("w_ff2", dw2)
    f_shard = d_ff // N_DEV
    dw1 = _tn_matmul(
        "dw_ff1", h, da, (N_DEV, d // 1024, nk),
        pl.BlockSpec((tk, 1024), lambda g, m, kk: (kk, m)), pl.BlockSpec((tk, f_shard), lambda g, m, kk: (kk, g)),
        jax.ShapeDtypeStruct((N_DEV, d, f_shard), BF16), pl.BlockSpec((None, 1024, f_shard), lambda g, m, kk: (g, m, 0)),
        (1024, f_shard), behind=token)
    token = on_grad("w_ff1", dw1)
    dh = _dh(da, w1g, behind=token)
    dx1, dx1b, d_norm2 = _norm_bwd("norm2_bwd", dh, x1, norm2_g, dx2)
    dy = _nt_matmul("dy", dx1b, w_out_full)
    dwout = _tn_matmul(
        "dw_out", y, dx1b, (d // 1024, d // 1024, nk),
        pl.BlockSpec((tk, 1024), lambda m, j, kk: (kk, m)), pl.BlockSpec((tk, 1024), lambda m, j, kk: (kk, j)),
        jax.ShapeDtypeStruct((d, d), BF16), pl.BlockSpec((1024, 1024), lambda m, j, kk: (m, j)), (1024, 1024))
    token = on_grad("w_out", dwout)
    dz, dzal, d_convw, d_convg, d_glag, d_bgate, d_wgu = _mixer_bwd(
        z, alow, dy, sall, wgu_pad, b_gate, convw_taps, conv_norm_g, gla_norm_g, behind=token)
    dwin_main = _tn_matmul(
        "dw_in", u, dz, (d // 1024, n_main // 1024, nk),
        pl.BlockSpec((tk, 1024), lambda m, j, kk: (kk, m)), pl.BlockSpec((tk, 1024), lambda m, j, kk: (kk, j)),
        jax.ShapeDtypeStruct((d, n_main), BF16), pl.BlockSpec((1024, 1024), lambda m, j, kk: (m, j)), (1024, 1024))
    dwin_alow = _tn_matmul(
        "dw_in_alow", u, dzal, (d // 1024, 1, nk),
        pl.BlockSpec((tk, 1024), lambda m, j, kk: (kk, m)), pl.BlockSpec((tk, LANES), lambda m, j, kk: (kk, 0)),
        jax.ShapeDtypeStruct((d, LANES), BF16), pl.BlockSpec((1024, LANES), lambda m, j, kk: (m, 0)), (1024, LANES))
    token = on_grad("w_in", (dwin_main, dwin_alow))
    du = _du(dz, w_main, dzal, w_alow, behind=token)
    grad_x, _, d_norm1 = _norm_bwd("norm1_bwd", du, x2d, norm1_g, dx1)
    return dict(x=grad_x, loss=loss_part, norm1_g=d_norm1, w_gate_up=d_wgu, b_gate=d_bgate, conv_w=d_convw,
                conv_norm_g=d_convg, gla_norm_g=d_glag, norm2_g=d_norm2, norm_f_g=d_normf)


_WEIGHT_ORDER = ("norm1_g", "w_in", "w_gate_up", "b_gate", "conv_w", "conv_norm_g", "gla_norm_g", "w_out", "norm2_g",
                 "w_ff1", "w_ff2", "norm_f_g")
_SMALL_ORDER = ("norm1_g", "b_gate", "conv_norm_g", "gla_norm_g", "norm2_g", "norm_f_g", "w_gate_up", "conv_w")


def _update(me, gin_r, gout_r, g1_r, g2_r, small_r, small_shapes, grad_x, wmv):
    big = {
        "w_in": _adamw("adamw_w_in", gin_r, *(a[0] for a in wmv["w_in"]), 256),
        "w_out": _adamw("adamw_w_out", gout_r, *(a[0] for a in wmv["w_out"]), 128),
        "w_ff1": _adamw("adamw_w_ff1", g1_r, *(a[0] for a in wmv["w_ff1"]), 256),
        "w_ff2": _adamw("adamw_w_ff2", g2_r, *(a[0] for a in wmv["w_ff2"]), 128),
    }

    wgu_cols = wmv["w_gate_up"][0].shape[2]
    cw_rows = wmv["conv_w"][0].shape[1]

    def local_block(flat_block):
        parts = _unpack_rows(flat_block, small_shapes)
        parts[6] = lax.dynamic_slice_in_dim(parts[6], me * wgu_cols, wgu_cols, axis=1)
        parts[7] = lax.dynamic_slice_in_dim(parts[7], me * cw_rows, cw_rows, axis=0)
        return parts

    local_shapes = small_shapes[:6] + [(GATE_RANK, wgu_cols), (cw_rows, CONV_WIDTH), (1,)]
    local_rows = 80
    parts_local = jnp.stack([_pack_rows(local_block(small_r[j]), local_rows) for j in range(N_DEV)])
    extra = (jnp.zeros((1,), F32), jnp.zeros((1,), F32), jnp.ones((1,), F32))
    packed = [_pack_rows([wmv[nm][k] for nm in _SMALL_ORDER] + [extra[k]], local_rows) for k in range(3)]
    out_small = _adamw("adamw_small", parts_local, *packed, local_rows)
    unpacked = [_unpack_rows(o, local_shapes) for o in out_small]

    outs = []
    for k in range(4):
        for nm in _WEIGHT_ORDER:
            if nm in big:
                outs.append(big[nm][k][None])
            else:
                val = unpacked[k][_SMALL_ORDER.index(nm)]
                outs.append(val.reshape(wmv[nm][0].shape))
    loss = unpacked[0][8][0]
    return (loss, grad_x[None], *outs)
```

```python
import functools

import jax
import jax.numpy as jnp
from jax import lax
from jax.experimental import pallas as pl
from jax.experimental.pallas import tpu as pltpu

F32 = jnp.float32
BF16 = jnp.bfloat16

N_DEV = 8
CHUNK = 64
GLA_HEADS = 4
CONV_GROUPS = 8
CONV_WIDTH = 3
GATE_RANK = 16
GATE_NORMALIZER = 16.0
EPS = 1e-6
ADAM_LR = 0.001
ADAM_B1 = 0.9
ADAM_B2 = 0.999
ADAM_EPS = 1e-08
ADAM_WD = 0.01
ADAM_STEP = 10

LANES = 128
SUBLANES = 8
VMEM_LIMIT = 56 << 20

_NN = (((1,), (0,)), ((), ()))
_NT = (((1,), (1,)), ((), ()))
_TN = (((0,), (0,)), ((), ()))


def _dot(a, b, dims=_NN):
    return lax.dot_general(a, b, dims, preferred_element_type=F32)


def _params(n_grid):
    return pltpu.CompilerParams(dimension_semantics=("arbitrary",) * n_grid, vmem_limit_bytes=VMEM_LIMIT)


def _relu_sq(a):
    r = jnp.maximum(a.astype(F32), 0.0)
    return (r * r).astype(BF16)


def _device_index():
    return 4 * lax.axis_index("x") + 2 * lax.axis_index("y") + lax.axis_index("c")


def _peer(mask):
    x, y, c = lax.axis_index("x"), lax.axis_index("y"), lax.axis_index("c")
    return (x ^ ((mask >> 2) & 1), y ^ ((mask >> 1) & 1), c ^ (mask & 1))


_HBM_SPEC = pl.BlockSpec(memory_space=pltpu.HBM)
_SEM_SPEC = pl.BlockSpec(memory_space=pltpu.SEMAPHORE)
_SIDE_EFFECT = pltpu.SideEffectType.DATAFLOW_SIDE_EFFECTING
N_PEERS = N_DEV - 1


def _exchange_copy(src_ref, land_ref, send_sems, recv_sems, mask, scatter, arriving):
    me = _device_index()
    src = src_ref.at[me ^ mask] if scatter else src_ref
    dst = land_ref.at[(me ^ mask) if arriving else me]
    return pltpu.make_async_remote_copy(
        src_ref=src, dst_ref=dst, send_sem=send_sems.at[mask - 1], recv_sem=recv_sems.at[mask - 1],
        device_id=_peer(mask), device_id_type=pl.DeviceIdType.MESH)


def _land_zone(own):
    zone = lax.empty((N_DEV,) + own.shape, own.dtype)
    return lax.dynamic_update_slice(zone, own[None], (_device_index(),) + (0,) * own.ndim)


def _exchange_start(name, srcs, lands, scatter):
    n = len(srcs)

    def body(*refs):
        src, land = refs[:n], refs[n:2 * n]
        send_sems, recv_sems = refs[2 * n:3 * n], refs[3 * n:4 * n]
        token = refs[-1]
        for a in range(n):
            for mask in range(1, N_DEV):
                _exchange_copy(src[a], land[a], send_sems[a], recv_sems[a], mask, scatter, False).start()
        token[...] = jnp.zeros_like(token)

    hbm = lambda a: pltpu.HBM(a.shape, a.dtype)
    outs = pl.pallas_call(
        body, name=name,
        out_shape=([pltpu.SemaphoreType.DMA((N_PEERS,))] * (2 * n) + [hbm(a) for a in srcs] + [hbm(a) for a in lands]
                   + [jax.ShapeDtypeStruct((SUBLANES, LANES), F32)]),
        in_specs=[_HBM_SPEC] * (2 * n),
        out_specs=[_SEM_SPEC] * (2 * n) + [_HBM_SPEC] * (2 * n) + [pl.BlockSpec(memory_space=pltpu.VMEM)],
        input_output_aliases={a: 2 * n + a for a in range(2 * n)},
        compiler_params=pltpu.CompilerParams(has_side_effects=_SIDE_EFFECT),
    )(*[pltpu.with_memory_space_constraint(a, pltpu.HBM) for a in list(srcs) + list(lands)])
    send_sems, recv_sems = outs[:n], outs[n:2 * n]
    src_thru, land_thru = outs[2 * n:3 * n], outs[3 * n:4 * n]
    return send_sems, recv_sems, src_thru, land_thru, outs[-1]


def _exchange_wait(name, send_sems, recv_sems, src_thru, land_thru, after, scatter):
    def body(src_ref, land_ref, send_ref, recv_ref, after_ref, src_dead, got_ref):
        for mask in range(1, N_DEV):
            cp = _exchange_copy(src_ref, land_ref, send_ref, recv_ref, mask, scatter, True)
            cp.wait_send()
            cp.wait_recv()

    return pl.pallas_call(
        body, name=name,
        out_shape=(pltpu.HBM(src_thru.shape, src_thru.dtype), pltpu.HBM(land_thru.shape, land_thru.dtype)),
        in_specs=[_HBM_SPEC, _HBM_SPEC, _SEM_SPEC, _SEM_SPEC, pl.BlockSpec(memory_space=pl.ANY)],
        out_specs=(_HBM_SPEC, _HBM_SPEC), input_output_aliases={0: 0, 1: 1},
        compiler_params=pltpu.CompilerParams(has_side_effects=_SIDE_EFFECT),
    )(src_thru, land_thru, send_sems, recv_sems, after)[1]


def _shards_to_columns(g, n_main, tr=256):
    n_dev, d, s = g.shape

    def body(g_ref, main_ref, rest_ref):
        for j in range(n_dev):
            lo, hi = j * s, (j + 1) * s
            if hi <= n_main:
                main_ref[:, lo:hi] = g_ref[j]
            else:
                main_ref[:, lo:n_main] = g_ref[j, :, 0:n_main - lo]
                rest_ref[...] = jnp.zeros_like(rest_ref)
                rest_ref[:, 0:hi - n_main] = g_ref[j, :, n_main - lo:s]

    return pl.pallas_call(
        body, grid=(d // tr,), name="shards_to_columns",
        in_specs=[pl.BlockSpec((n_dev, tr, s), lambda i: (0, i, 0))],
        out_specs=[pl.BlockSpec((tr, n_main), lambda i: (i, 0)), pl.BlockSpec((tr, LANES), lambda i: (i, 0))],
        out_shape=[jax.ShapeDtypeStruct((d, n_main), g.dtype), jax.ShapeDtypeStruct((d, LANES), g.dtype)],
        compiler_params=_params(1),
    )(g)


def _columns_to_shards(main, rest, n_dev, s, tr=256):
    d, n_main = main.shape
    assert (n_dev - 1) * s <= n_main < n_dev * s

    def body(main_ref, rest_ref, o_ref):
        for j in range(n_dev):
            lo, hi = j * s, (j + 1) * s
            if hi <= n_main:
                o_ref[j] = main_ref[:, lo:hi]
            else:
                o_ref[j, :, 0:n_main - lo] = main_ref[:, lo:n_main]
                o_ref[j, :, n_main - lo:s] = rest_ref[:, 0:hi - n_main]

    return pl.pallas_call(
        body, grid=(d // tr,), name="columns_to_shards",
        in_specs=[pl.BlockSpec((tr, n_main), lambda i: (i, 0)), pl.BlockSpec((tr, LANES), lambda i: (i, 0))],
        out_specs=pl.BlockSpec((n_dev, tr, s), lambda i: (0, i, 0)),
        out_shape=jax.ShapeDtypeStruct((n_dev, d, s), main.dtype),
        compiler_params=_params(1),
    )(main, rest)


def _inproj(x, g1, w_main, w_alow, tm=1024, tn=1024):
    t, d = x.shape
    tm = min(tm, t)
    n = w_main.shape[1]

    def body(x_ref, g_ref, w_ref, wa_ref, z_ref, u_ref, al_ref):
        @pl.when(pl.program_id(1) == 0)
        def _():
            xf = x_ref[...]
            r = lax.rsqrt(jnp.mean(xf * xf, axis=-1, keepdims=True) + EPS)
            u = (xf * r * g_ref[...]).astype(BF16)
            u_ref[...] = u
            al_ref[...] = _dot(u, wa_ref[...])

        z_ref[...] = _dot(u_ref[...], w_ref[...])

    return pl.pallas_call(
        body, name="rmsnorm_inproj", grid=(t // tm, n // tn),
        in_specs=[pl.BlockSpec((tm, d), lambda m, j: (m, 0)), pl.BlockSpec((1, d), lambda m, j: (0, 0)),
                  pl.BlockSpec((d, tn), lambda m, j: (0, j)), pl.BlockSpec((d, LANES), lambda m, j: (0, 0))],
        out_specs=[pl.BlockSpec((tm, tn), lambda m, j: (m, j)), pl.BlockSpec((tm, d), lambda m, j: (m, 0)),
                   pl.BlockSpec((tm, LANES), lambda m, j: (m, 0))],
        out_shape=[jax.ShapeDtypeStruct((t, n), F32), jax.ShapeDtypeStruct((t, d), BF16),
                   jax.ShapeDtypeStruct((t, LANES), F32)],
        compiler_params=_params(2),
    )(x, g1, w_main, w_alow)


def _outproj(y, w_out, x, g2, tm=512):
    t, d = x.shape
    tm = min(tm, t)
    k = y.shape[1]

    def body(y_ref, w_ref, x_ref, g_ref, x1_ref, h_ref):
        x1 = x_ref[...] + _dot(y_ref[...], w_ref[...])
        x1_ref[...] = x1
        r = lax.rsqrt(jnp.mean(x1 * x1, axis=-1, keepdims=True) + EPS)
        h_ref[...] = (x1 * r * g_ref[...]).astype(BF16)

    return pl.pallas_call(
        body, name="outproj_rmsnorm", grid=(t // tm,),
        in_specs=[pl.BlockSpec((tm, k), lambda m: (m, 0)), pl.BlockSpec((k, d), lambda m: (0, 0)),
                  pl.BlockSpec((tm, d), lambda m: (m, 0)), pl.BlockSpec((1, d), lambda m: (0, 0))],
        out_specs=[pl.BlockSpec((tm, d), lambda m: (m, 0)), pl.BlockSpec((tm, d), lambda m: (m, 0))],
        out_shape=[jax.ShapeDtypeStruct((t, d), F32), jax.ShapeDtypeStruct((t, d), BF16)],
        compiler_params=_params(1),
    )(y, w_out, x, g2)


def _ff1(h, w1g, tm=1024):
    t, d = h.shape
    tm = min(tm, t)
    g, _, f = w1g.shape

    def body(h_ref, w_ref, a_ref):
        a_ref[...] = _dot(h_ref[...], w_ref[...]).astype(BF16)

    return pl.pallas_call(
        body, name="ff1", grid=(t // tm, g),
        in_specs=[pl.BlockSpec((tm, d), lambda m, j: (m, 0)), pl.BlockSpec((None, d, f), lambda m, j: (j, 0, 0))],
        out_specs=pl.BlockSpec((tm, f), lambda m, j: (m, j)),
        out_shape=jax.ShapeDtypeStruct((t, g * f), BF16),
        compiler_params=_params(2),
    )(h, w1g)


def _ff2(a, w2, x1, tm=512, tn=512):
    t, f = a.shape
    tm = min(tm, t)
    d = w2.shape[1]

    def body(a_ref, w_ref, x1_ref, o_ref, p_ref):
        @pl.when(pl.program_id(1) == 0)
        def _():
            p_ref[...] = _relu_sq(a_ref[...])

        o_ref[...] = x1_ref[...] + _dot(p_ref[...], w_ref[...])

    return pl.pallas_call(
        body, name="ff2_residual", grid=(t // tm, d // tn),
        in_specs=[pl.BlockSpec((tm, f), lambda m, j: (m, 0)), pl.BlockSpec((f, tn), lambda m, j: (0, j)),
                  pl.BlockSpec((tm, tn), lambda m, j: (m, j))],
        out_specs=pl.BlockSpec((tm, tn), lambda m, j: (m, j)),
        out_shape=jax.ShapeDtypeStruct((t, d), F32),
        scratch_shapes=[pltpu.VMEM((tm, f), BF16)],
        compiler_params=_params(2),
    )(a, w2, x1)


def _dff2(dx2b, w2, a, tm=1024, tn=1024):
    t, d = dx2b.shape
    tm = min(tm, t)
    f = w2.shape[0]

    def body(g_ref, w_ref, a_ref, o_ref):
        dp = _dot(g_ref[...], w_ref[...], _NT)
        o_ref[...] = (dp * (2.0 * jnp.maximum(a_ref[...].astype(F32), 0.0))).astype(BF16)

    return pl.pallas_call(
        body, name="dff2", grid=(t // tm, f // tn),
        in_specs=[pl.BlockSpec((tm, d), lambda m, j: (m, 0)), pl.BlockSpec((tn, d), lambda m, j: (j, 0)),
                  pl.BlockSpec((tm, tn), lambda m, j: (m, j))],
        out_specs=pl.BlockSpec((tm, tn), lambda m, j: (m, j)),
        out_shape=jax.ShapeDtypeStruct((t, f), BF16),
        compiler_params=_params(2),
    )(dx2b, w2, a)


def _behind(token):
    if token is None:
        return [], []
    return [token], [pl.BlockSpec(token.shape, lambda *_: (0,) * token.ndim)]


def _tn_matmul(name, a, b, grid, a_spec, b_spec, out_shape, out_spec, acc_shape, a_fn=None, behind=None):
    nk = grid[-1]
    dep_args, dep_specs = _behind(behind)

    def body(a_ref, b_ref, *rest):
        o_ref, acc_ref = rest[-2:]
        kk = pl.program_id(len(grid) - 1)
        av = a_ref[...]
        if a_fn is not None:
            av = a_fn(av)
        part = _dot(av, b_ref[...], _TN)

        @pl.when(kk == 0)
        def _():
            acc_ref[...] = part

        @pl.when(kk > 0)
        def _():
            acc_ref[...] += part

        @pl.when(kk == nk - 1)
        def _():
            o_ref[...] = acc_ref[...].astype(o_ref.dtype)

    return pl.pallas_call(
        body, name=name, grid=grid, in_specs=[a_spec, b_spec] + dep_specs, out_specs=out_spec, out_shape=out_shape,
        scratch_shapes=[pltpu.VMEM(acc_shape, F32)], compiler_params=_params(len(grid)),
    )(a, b, *dep_args)


def _dh(da, w1g, tm=512, tn=512, behind=None):
    t = da.shape[0]
    tm = min(tm, t)
    g, d, f = w1g.shape
    dep_args, dep_specs = _behind(behind)

    def body(a_ref, w_ref, *rest):
        o_ref = rest[-1]
        acc = _dot(a_ref[:, 0:f], w_ref[0], _NT)
        for s in range(1, g):
            acc = acc + _dot(a_ref[:, s * f:(s + 1) * f], w_ref[s], _NT)
        o_ref[...] = acc

    return pl.pallas_call(
        body, name="dh", grid=(t // tm, d // tn),
        in_specs=[pl.BlockSpec((tm, g * f), lambda m, j: (m, 0)),
                  pl.BlockSpec((g, tn, f), lambda m, j: (0, j, 0))] + dep_specs,
        out_specs=pl.BlockSpec((tm, tn), lambda m, j: (m, j)),
        out_shape=jax.ShapeDtypeStruct((t, d), F32),
        compiler_params=_params(2),
    )(da, w1g, *dep_args)


def _nt_matmul(name, a, b, tm=1024, tn=1024):
    t, k = a.shape
    tm = min(tm, t)
    n = b.shape[0]

    def body(a_ref, b_ref, o_ref):
        o_ref[...] = _dot(a_ref[...], b_ref[...], _NT)

    return pl.pallas_call(
        body, name=name, grid=(t // tm, n // tn),
        in_specs=[pl.BlockSpec((tm, k), lambda m, j: (m, 0)), pl.BlockSpec((tn, k), lambda m, j: (j, 0))],
        out_specs=pl.BlockSpec((tm, tn), lambda m, j: (m, j)),
        out_shape=jax.ShapeDtypeStruct((t, n), F32),
        compiler_params=_params(2),
    )(a, b)


def _du(dz, w_main, dzal, w_alow, tm=512, tn=512, behind=None):
    t, n = dz.shape
    tm = min(tm, t)
    d = w_main.shape[0]
    dep_args, dep_specs = _behind(behind)

    def body(a_ref, w_ref, al_ref, wa_ref, *rest):
        o_ref = rest[-1]
        o_ref[...] = _dot(a_ref[...], w_ref[...], _NT) + _dot(al_ref[...], wa_ref[...], _NT)

    return pl.pallas_call(
        body, name="du", grid=(t // tm, d // tn),
        in_specs=[pl.BlockSpec((tm, n), lambda m, j: (m, 0)), pl.BlockSpec((tn, n), lambda m, j: (j, 0)),
                  pl.BlockSpec((tm, LANES), lambda m, j: (m, 0)), pl.BlockSpec((tn, LANES), lambda m, j: (j, 0))]
        + dep_specs,
        out_specs=pl.BlockSpec((tm, tn), lambda m, j: (m, j)),
        out_shape=jax.ShapeDtypeStruct((t, d), F32),
        compiler_params=_params(2),
    )(dz, w_main, dzal, w_alow, *dep_args)


def _loss_head(x2, gf, tgt, tr=256):
    t, d = x2.shape

    def body(x_ref, g_ref, t_ref, dx_ref, dxb_ref, loss_ref, dg_ref):
        @pl.when(pl.program_id(0) == 0)
        def _():
            loss_ref[...] = jnp.zeros_like(loss_ref)
            dg_ref[...] = jnp.zeros_like(dg_ref)

        xf = x_ref[...]
        g = g_ref[...]
        r = lax.rsqrt(jnp.mean(xf * xf, axis=-1, keepdims=True) + EPS)
        xh = xf * r
        e = xh * g - t_ref[...]
        loss_ref[...] += 0.5 * jnp.sum(jnp.mean(e * e, axis=-1, keepdims=True))
        dy = e * (1.0 / d)
        dg_ref[...] += jnp.sum(dy * xh, axis=0, keepdims=True)
        dyg = dy * g
        dx = r * (dyg - xh * jnp.mean(dyg * xh, axis=-1, keepdims=True))
        dx_ref[...] = dx
        dxb_ref[...] = dx.astype(BF16)

    return pl.pallas_call(
        body, name="loss_head", grid=(t // tr,),
        in_specs=[pl.BlockSpec((tr, d), lambda i: (i, 0)), pl.BlockSpec((1, d), lambda i: (0, 0)),
                  pl.BlockSpec((tr, d), lambda i: (i, 0))],
        out_specs=[pl.BlockSpec((tr, d), lambda i: (i, 0)), pl.BlockSpec((tr, d), lambda i: (i, 0)),
                   pl.BlockSpec((SUBLANES, LANES), lambda i: (0, 0)), pl.BlockSpec((1, d), lambda i: (0, 0))],
        out_shape=[jax.ShapeDtypeStruct((t, d), F32), jax.ShapeDtypeStruct((t, d), BF16),
                   jax.ShapeDtypeStruct((SUBLANES, LANES), F32), jax.ShapeDtypeStruct((1, d), F32)],
        compiler_params=_params(1),
    )(x2, gf, tgt)


def _norm_bwd(name, dh, xin, g, dres, tr=256):
    t, d = xin.shape

    def body(dh_ref, x_ref, g_ref, dr_ref, dx_ref, dxb_ref, dg_ref):
        @pl.when(pl.program_id(0) == 0)
        def _():
            dg_ref[...] = jnp.zeros_like(dg_ref)

        xf = x_ref[...]
        dhv = dh_ref[...]
        r = lax.rsqrt(jnp.mean(xf * xf, axis=-1, keepdims=True) + EPS)
        xh = xf * r
        dg_ref[...] += jnp.sum(dhv * xh, axis=0, keepdims=True)
        dyg = dhv * g_ref[...]
        dx = dr_ref[...] + r * (dyg - xh * jnp.mean(dyg * xh, axis=-1, keepdims=True))
        dx_ref[...] = dx
        dxb_ref[...] = dx.astype(BF16)

    return pl.pallas_call(
        body, name=name, grid=(t // tr,),
        in_specs=[pl.BlockSpec((tr, d), lambda i: (i, 0)), pl.BlockSpec((tr, d), lambda i: (i, 0)),
                  pl.BlockSpec((1, d), lambda i: (0, 0)), pl.BlockSpec((tr, d), lambda i: (i, 0))],
        out_specs=[pl.BlockSpec((tr, d), lambda i: (i, 0)), pl.BlockSpec((tr, d), lambda i: (i, 0)),
                   pl.BlockSpec((1, d), lambda i: (0, 0))],
        out_shape=[jax.ShapeDtypeStruct((t, d), F32), jax.ShapeDtypeStruct((t, d), BF16),
                   jax.ShapeDtypeStruct((1, d), F32)],
        compiler_params=_params(1),
    )(dh, xin, g, dres)


MIX_TILE = 256
CHUNKS_PER_TILE = MIX_TILE // CHUNK
CHUNK_SHIFT = CHUNK.bit_length() - 1
assert 1 << CHUNK_SHIFT == CHUNK


def _chunk_masks(n):
    row = lax.broadcasted_iota(jnp.int32, (n, n), 0)
    col = lax.broadcasted_iota(jnp.int32, (n, n), 1)
    same = lax.shift_right_logical(row, CHUNK_SHIFT) == lax.shift_right_logical(col, CHUNK_SHIFT)
    one = lambda m: jnp.where(m, 1.0, 0.0).astype(BF16)
    return one(same & (col > row)), one(same), one(same & (col < row))


def _mask_dot(mask, x):
    hi = x.astype(BF16)
    r1 = x - hi.astype(F32)
    mid = r1.astype(BF16)
    lo = (r1 - mid.astype(F32)).astype(BF16)
    return _dot(mask, hi) + _dot(mask, mid) + _dot(mask, lo)


def _log_sigmoid(x):
    return jnp.minimum(x, 0.0) - jnp.log1p(jnp.exp(-jnp.abs(x)))


def _conv_taps(prev8, uc, w):
    ext = jnp.concatenate([prev8, uc], axis=0)
    s1 = pltpu.roll(ext, 1, 0)[SUBLANES:]
    s2 = pltpu.roll(ext, 2, 0)[SUBLANES:]
    return s2 * w[0:1] + s1 * w[1:2] + uc * w[2:3], s1, s2


def _z_specs(tile, idx):
    d_conv = 1024
    wide = lambda c: pl.BlockSpec((tile, d_conv), lambda i, c=c: (idx(i), c))
    half = lambda c: pl.BlockSpec((tile, d_conv // 2), lambda i, c=c: (idx(i), c))
    return [wide(0), wide(1), wide(2), half(6), half(7), wide(4), wide(5)]


def _mixer_fwd(z, alow, wgu, b_gate, convw, conv_g, gla_g):
    t = z.shape[0]
    tb, cpt = MIX_TILE, CHUNKS_PER_TILE
    d_conv = conv_g.shape[1]
    dv = gla_g.shape[1]
    dk = dv // 2
    d_k = GLA_HEADS * dk
    gw = d_conv // CONV_GROUPS
    scale = dk ** -0.5

    def body(cb_ref, cc_ref, ch_ref, q_ref, k_ref, v_ref, og_ref, al_ref, wgu_ref, bg_ref, cw_ref, cg_ref, gg_ref,
             y_ref, sall_ref, carry_ref, s_ref):
        @pl.when(pl.program_id(0) == 0)
        def _():
            carry_ref[...] = jnp.zeros_like(carry_ref)
            s_ref[...] = jnp.zeros_like(s_ref)

        uc = cc_ref[...] * ch_ref[...]
        conv, _, _ = _conv_taps(carry_ref[...], uc, cw_ref[...])
        carry_ref[...] = uc[tb - SUBLANES:]
        ypre = cb_ref[...] * conv
        cg = cg_ref[...]
        for g in range(CONV_GROUPS):
            sl = slice(g * gw, (g + 1) * gw)
            seg = ypre[:, sl]
            r = lax.rsqrt(jnp.mean(seg * seg, axis=-1, keepdims=True) + EPS)
            y_ref[:, sl] = (seg * r * cg[:, sl]).astype(BF16)

        later, same, _ = _chunk_masks(tb)
        pre = _dot(al_ref[...].astype(BF16), wgu_ref[...]) + bg_ref[...]
        la = _log_sigmoid(pre) * (1.0 / GATE_NORMALIZER)
        e_dec = _mask_dot(later, la)
        dec_all = jnp.exp(_mask_dot(same, la))
        kdec = (k_ref[...] * jnp.exp(e_dec)).astype(BF16)
        qs = (q_ref[...] * scale).astype(BF16)
        vb = v_ref[...].astype(BF16)
        gg = gg_ref[...]
        rows = [slice(c * CHUNK, (c + 1) * CHUNK) for c in range(cpt)]
        ks = [slice(h * dk, (h + 1) * dk) for h in range(GLA_HEADS)]
        vs = [slice(h * dv, (h + 1) * dv) for h in range(GLA_HEADS)]
        kvt = [[_dot(vb[rows[c], vs[h]], kdec[rows[c], ks[h]], _TN) for h in range(GLA_HEADS)] for c in range(cpt)]
        state = [s_ref[h] for h in range(GLA_HEADS)]
        states = []
        for c in range(cpt):
            state = [state[h] * dec_all[c * CHUNK:c * CHUNK + 1, ks[h]] + kvt[c][h] for h in range(GLA_HEADS)]
            states.append(state)
            for h in range(GLA_HEADS):
                sall_ref[c, h] = state[h]
        for h in range(GLA_HEADS):
            s_ref[h] = state[h]
        for h in range(GLA_HEADS):
            o = jnp.concatenate(
                [_dot(qs[rows[c], ks[h]], states[c][h].astype(BF16), _NT) for c in range(cpt)], axis=0)
            ro = lax.rsqrt(jnp.mean(o * o, axis=-1, keepdims=True) + EPS)
            ogs = og_ref[:, vs[h]]
            yg = o * ro * gg * (ogs * jax.nn.sigmoid(ogs))
            y_ref[:, d_conv + h * dv:d_conv + (h + 1) * dv] = yg.astype(BF16)

    full = lambda shape: pl.BlockSpec(shape, lambda i: (0,) * len(shape))
    return pl.pallas_call(
        body, name="mixer_fwd", grid=(t // tb,),
        in_specs=_z_specs(tb, lambda i: i) + [
            pl.BlockSpec((tb, LANES), lambda i: (i, 0)), full(wgu.shape), full(b_gate.shape), full(convw.shape),
            full(conv_g.shape), full(gla_g.shape)],
        out_specs=[pl.BlockSpec((tb, d_conv + GLA_HEADS * dv), lambda i: (i, 0)),
                   pl.BlockSpec((cpt, GLA_HEADS, dv, dk), lambda i: (i, 0, 0, 0))],
        out_shape=[jax.ShapeDtypeStruct((t, d_conv + GLA_HEADS * dv), BF16),
                   jax.ShapeDtypeStruct((t // CHUNK, GLA_HEADS, dv, dk), F32)],
        scratch_shapes=[pltpu.VMEM((SUBLANES, d_conv), F32), pltpu.VMEM((GLA_HEADS, dv, dk), F32)],
        compiler_params=_params(1),
    )(z, z, z, z, z, z, z, alow, wgu, b_gate, convw, conv_g, gla_g)


def _mixer_bwd(z, alow, dy, sall, wgu, b_gate, convw, conv_g, gla_g, behind=None):
    t = z.shape[0]
    tb, cpt = MIX_TILE, CHUNKS_PER_TILE
    nt = t // tb
    d_conv = conv_g.shape[1]
    dv = gla_g.shape[1]
    dk = dv // 2
    d_k = GLA_HEADS * dk
    gw = d_conv // CONV_GROUPS
    scale = dk ** -0.5
    rev = lambda i: nt - 1 - i
    dep_args, dep_specs = _behind(behind)

    def body(cb_ref, cc_ref, ch_ref, q_ref, k_ref, v_ref, og_ref, ccp_ref, chp_ref, al_ref, dy_ref, sall_ref, sprev_ref,
             wgu_ref, bg_ref, cw_ref, cg_ref, gg_ref, *rest):
        dz_ref, dzal_ref, dcw_ref, dcg_ref, dgg_ref, dbg_ref, dwgu_ref, dcarry_ref, gd_ref = rest[-9:]
        i = pl.program_id(0)

        @pl.when(i == 0)
        def _():
            dcarry_ref[...] = jnp.zeros_like(dcarry_ref)
            gd_ref[...] = jnp.zeros_like(gd_ref)
            dcw_ref[...] = jnp.zeros_like(dcw_ref)
            dcg_ref[...] = jnp.zeros_like(dcg_ref)
            dgg_ref[...] = jnp.zeros_like(dgg_ref)
            dbg_ref[...] = jnp.zeros_like(dbg_ref)
            dwgu_ref[...] = jnp.zeros_like(dwgu_ref)

        first = rev(i) == 0

        cb, cc, ch = cb_ref[...], cc_ref[...], ch_ref[...]
        w = cw_ref[...]
        uc = cc * ch
        prev8 = jnp.where(first, 0.0, ccp_ref[...] * chp_ref[...])
        conv, s1, s2 = _conv_taps(prev8, uc, w)
        ypre = cb * conv
        cg = cg_ref[...]
        dypre_parts = []
        for g in range(CONV_GROUPS):
            sl = slice(g * gw, (g + 1) * gw)
            seg = ypre[:, sl]
            r = lax.rsqrt(jnp.mean(seg * seg, axis=-1, keepdims=True) + EPS)
            yn = seg * r
            dyc = dy_ref[:, sl]
            dcg_ref[:, sl] += jnp.sum(dyc * yn, axis=0, keepdims=True)
            dyn = dyc * cg[:, sl]
            dypre_parts.append(r * (dyn - yn * jnp.mean(dyn * yn, axis=-1, keepdims=True)))
        dypre = jnp.concatenate(dypre_parts, axis=1)
        dconv = dypre * cb
        dz_ref[:, 0:d_conv] = (dypre * conv).astype(BF16)
        dcw_ref[0:1] += jnp.sum(dconv * s2, axis=0, keepdims=True)
        dcw_ref[1:2] += jnp.sum(dconv * s1, axis=0, keepdims=True)
        dcw_ref[2:3] += jnp.sum(dconv * uc, axis=0, keepdims=True)
        ext = jnp.concatenate([dconv, dcarry_ref[...]], axis=0)
        f1 = pltpu.roll(ext, tb + SUBLANES - 1, 0)[:tb]
        f2 = pltpu.roll(ext, tb + SUBLANES - 2, 0)[:tb]
        dcarry_ref[...] = dconv[:SUBLANES]
        duc = dconv * w[2:3] + f1 * w[1:2] + f2 * w[0:1]
        dz_ref[:, d_conv:2 * d_conv] = (duc * ch).astype(BF16)
        dz_ref[:, 2 * d_conv:3 * d_conv] = (duc * cc).astype(BF16)

        q_off = 3 * d_conv
        k_off = q_off + d_k
        v_off = k_off + d_k
        og_off = v_off + GLA_HEADS * dv
        later, same, earlier = _chunk_masks(tb)
        alb = al_ref[...].astype(BF16)
        pre = _dot(alb, wgu_ref[...]) + bg_ref[...]
        la = _log_sigmoid(pre) * (1.0 / GATE_NORMALIZER)
        exp_e = jnp.exp(_mask_dot(later, la))
        dec_all = jnp.exp(_mask_dot(same, la))
        kdec = k_ref[...] * exp_e
        kdec_b = kdec.astype(BF16)
        qs = (q_ref[...] * scale).astype(BF16)
        vb = v_ref[...].astype(BF16)
        gg = gg_ref[...]
        rows = [slice(c * CHUNK, (c + 1) * CHUNK) for c in range(cpt)]
        ks = [slice(h * dk, (h + 1) * dk) for h in range(GLA_HEADS)]
        vs = [slice(h * dv, (h + 1) * dv) for h in range(GLA_HEADS)]
        st_b = [[sall_ref[c, h].astype(BF16) for h in range(GLA_HEADS)] for c in range(cpt)]
        do_b = []
        dgg = jnp.zeros_like(gg)
        for h in range(GLA_HEADS):
            o = jnp.concatenate([_dot(qs[rows[c], ks[h]], st_b[c][h], _NT) for c in range(cpt)], axis=0)
            ro = lax.rsqrt(jnp.mean(o * o, axis=-1, keepdims=True) + EPS)
            on = o * ro
            ogs = og_ref[:, vs[h]]
            sg = jax.nn.sigmoid(ogs)
            gate = ogs * sg
            dyg = dy_ref[:, d_conv + h * dv:d_conv + (h + 1) * dv]
            dgg = dgg + jnp.sum(dyg * on * gate, axis=0, keepdims=True)
            dz_ref[:, og_off + h * dv:og_off + (h + 1) * dv] = (
                dyg * on * gg * (sg * (1.0 + ogs * (1.0 - sg)))).astype(BF16)
            don = dyg * gg * gate
            do_b.append((ro * (don - on * jnp.mean(don * on, axis=-1, keepdims=True))).astype(BF16))
        dgg_ref[...] += dgg
        for h in range(GLA_HEADS):
            dq = jnp.concatenate([_dot(do_b[h][rows[c]], st_b[c][h]) for c in range(cpt)], axis=0)
            dz_ref[:, q_off + h * dk:q_off + (h + 1) * dk] = (dq * scale).astype(BF16)
        own = [[_dot(do_b[h][rows[c]], qs[rows[c], ks[h]], _TN) for h in range(GLA_HEADS)] for c in range(cpt)]
        carried = [gd_ref[h] for h in range(GLA_HEADS)]
        gt_b = [None] * cpt
        ddd = [None] * cpt
        for c in reversed(range(cpt)):
            gt = [own[c][h] + carried[h] for h in range(GLA_HEADS)]
            dec = [dec_all[c * CHUNK:c * CHUNK + 1, ks[h]] for h in range(GLA_HEADS)]
            carried = [gt[h] * dec[h] for h in range(GLA_HEADS)]
            if c > 0:
                st_prev = [sall_ref[c - 1, h] for h in range(GLA_HEADS)]
            else:
                st_prev = [jnp.where(first, 0.0, sprev_ref[0, h]) for h in range(GLA_HEADS)]
            ddec = [jnp.sum(gt[h] * st_prev[h], axis=0, keepdims=True) * dec[h] for h in range(GLA_HEADS)]
            ddd[c] = jnp.broadcast_to(jnp.concatenate(ddec, axis=1), (CHUNK, d_k))
            gt_b[c] = [gt[h].astype(BF16) for h in range(GLA_HEADS)]
        for h in range(GLA_HEADS):
            gd_ref[h] = carried[h]
        dkdec_cols = []
        for h in range(GLA_HEADS):
            dvh = jnp.concatenate([_dot(kdec_b[rows[c], ks[h]], gt_b[c][h], _NT) for c in range(cpt)], axis=0)
            dz_ref[:, v_off + h * dv:v_off + (h + 1) * dv] = dvh.astype(BF16)
            dkdec_cols.append(jnp.concatenate([_dot(vb[rows[c], vs[h]], gt_b[c][h]) for c in range(cpt)], axis=0))
        dkdec = jnp.concatenate(dkdec_cols, axis=1)
        dz_ref[:, k_off:k_off + d_k] = (dkdec * exp_e).astype(BF16)
        dla = _mask_dot(earlier, dkdec * kdec) + jnp.concatenate(ddd, axis=0)
        dpre = dla * (1.0 / GATE_NORMALIZER) * jax.nn.sigmoid(-pre)
        dbg_ref[...] += jnp.sum(dpre, axis=0, keepdims=True)
        dpre_b = dpre.astype(BF16)
        dwgu_ref[...] += _dot(alb, dpre_b, _TN)
        dzal_ref[...] = _dot(dpre_b, wgu_ref[...], _NT).astype(BF16)

    full = lambda shape: pl.BlockSpec(shape, lambda i: (0,) * len(shape))
    prev_rows = lambda c: pl.BlockSpec(
        (SUBLANES, d_conv), lambda i, c=c: (jnp.maximum(rev(i) * (tb // SUBLANES) - 1, 0), c))
    n_z = 3 * d_conv + 2 * d_k + 2 * GLA_HEADS * dv
    return pl.pallas_call(
        body, name="mixer_bwd", grid=(nt,),
        in_specs=_z_specs(tb, rev) + [
            prev_rows(1), prev_rows(2),
            pl.BlockSpec((tb, LANES), lambda i: (rev(i), 0)),
            pl.BlockSpec((tb, d_conv + GLA_HEADS * dv), lambda i: (rev(i), 0)),
            pl.BlockSpec((cpt, GLA_HEADS, dv, dk), lambda i: (rev(i), 0, 0, 0)),
            pl.BlockSpec((1, GLA_HEADS, dv, dk), lambda i: (jnp.maximum(rev(i) * cpt - 1, 0), 0, 0, 0)),
            full(wgu.shape), full(b_gate.shape), full(convw.shape), full(conv_g.shape), full(gla_g.shape)]
        + dep_specs,
        out_specs=[pl.BlockSpec((tb, n_z), lambda i: (rev(i), 0)), pl.BlockSpec((tb, LANES), lambda i: (rev(i), 0)),
                   full(convw.shape), full(conv_g.shape), full(gla_g.shape), full(b_gate.shape), full(wgu.shape)],
        out_shape=[jax.ShapeDtypeStruct((t, n_z), BF16), jax.ShapeDtypeStruct((t, LANES), BF16),
                   jax.ShapeDtypeStruct(convw.shape, F32), jax.ShapeDtypeStruct(conv_g.shape, F32),
                   jax.ShapeDtypeStruct(gla_g.shape, F32), jax.ShapeDtypeStruct(b_gate.shape, F32),
                   jax.ShapeDtypeStruct(wgu.shape, F32)],
        scratch_shapes=[pltpu.VMEM((SUBLANES, d_conv), F32), pltpu.VMEM((GLA_HEADS, dv, dk), F32)],
        compiler_params=_params(1),
    )(z, z, z, z, z, z, z, z, z, alow, dy, sall, sall, wgu, b_gate, convw, conv_g, gla_g, *dep_args)


def _adamw_math(g, w, m, v):
    m = ADAM_B1 * m + (1.0 - ADAM_B1) * g
    v = ADAM_B2 * v + (1.0 - ADAM_B2) * (g * g)
    m_hat = m / (1.0 - ADAM_B1 ** ADAM_STEP)
    v_hat = v / (1.0 - ADAM_B2 ** ADAM_STEP)
    delta = -ADAM_LR * (m_hat / (jnp.sqrt(v_hat) + ADAM_EPS) + ADAM_WD * w)
    return delta, m, v


def _adamw(name, parts, w, m, v, tr):
    r, c = w.shape

    def body(p_ref, w_ref, m_ref, v_ref, g_ref, d_ref, nm_ref, nv_ref):
        g = p_ref[0].astype(F32)
        for j in range(1, N_DEV):
            g = g + p_ref[j].astype(F32)
        g_ref[...] = g
        d_ref[...], nm_ref[...], nv_ref[...] = _adamw_math(g, w_ref[...], m_ref[...], v_ref[...])

    blk = pl.BlockSpec((tr, c), lambda i: (i, 0))
    return pl.pallas_call(
        body, name=name, grid=(r // tr,),
        in_specs=[pl.BlockSpec((N_DEV, tr, c), lambda i: (0, i, 0)), blk, blk, blk],
        out_specs=[blk] * 4, out_shape=[jax.ShapeDtypeStruct((r, c), F32)] * 4,
        compiler_params=_params(1),
    )(parts, w, m, v)


def _pack_rows(vectors, rows):
    flat = jnp.concatenate([a.reshape(-1).astype(F32) for a in vectors])
    return jnp.pad(flat, (0, rows * LANES - flat.shape[0])).reshape(rows, LANES)


def _unpack_rows(block, shapes):
    flat = block.reshape(-1)
    out, off = [], 0
    for s in shapes:
        n = 1
        for dim in s:
            n *= dim
        out.append(flat[off:off + n].reshape(s))
        off += n
    return out


def kernel(x, norm1_g, w_in, w_gate_up, b_gate, conv_w, conv_norm_g, gla_norm_g, w_out, norm2_g, w_ff1, w_ff2, norm_f_g, loss_target, m_norm1_g, m_w_in, m_w_gate_up, m_b_gate, m_conv_w, m_conv_norm_g, m_gla_norm_g, m_w_out, m_norm2_g, m_w_ff1, m_w_ff2, m_norm_f_g, v_norm1_g, v_w_in, v_w_gate_up, v_b_gate, v_conv_w, v_conv_norm_g, v_gla_norm_g, v_w_out, v_norm2_g, v_w_ff1, v_w_ff2, v_norm_f_g):
    me = _device_index()
    x2d, tgt = x[0], loss_target[0]
    t, d = x2d.shape
    d_in_shard = w_in.shape[2]
    d_in = N_DEV * d_in_shard
    n_main = d_in - GATE_RANK
    d_conv = conv_norm_g.shape[1]
    d_k = b_gate.shape[1]
    d_ff = N_DEV * w_ff1.shape[2]

    small_rows = 16
    small_shard = _pack_rows([w_gate_up[0], conv_w[0]], small_rows)
    shards = [small_shard, w_in[0].astype(BF16), w_out[0].astype(BF16), w_ff1[0].astype(BF16), w_ff2[0].astype(BF16)]
    ag_send, ag_recv, ag_src, ag_land, _ = _exchange_start(
        "all_gather_start", shards, [_land_zone(s) for s in shards], scatter=False)

    def gathered(k, name, after):
        return _exchange_wait(name, ag_send[k], ag_recv[k], ag_src[k], ag_land[k], after, scatter=False)

    small_g = gathered(0, "all_gather_wait_small", x2d)
    win_g = gathered(1, "all_gather_wait_w_in", small_g)
    w_main, w_alow = _shards_to_columns(win_g, n_main)
    small_flat = small_g.reshape(N_DEV, -1)
    n_wgu = GATE_RANK * (d_k // N_DEV)
    wgu_full = small_flat[:, :n_wgu].reshape(N_DEV, GATE_RANK, d_k // N_DEV).transpose(1, 0, 2).reshape(GATE_RANK, d_k)
    conv_w_full = small_flat[:, n_wgu:n_wgu + (d_conv // N_DEV) * CONV_WIDTH].reshape(d_conv, CONV_WIDTH)
    wgu_pad = jnp.pad(wgu_full, ((0, LANES - GATE_RANK), (0, 0))).astype(BF16)
    convw_taps = jnp.pad(conv_w_full.T, ((0, SUBLANES - CONV_WIDTH), (0, 0)))

    get_w_out = lambda after: gathered(2, "all_gather_wait_w_out", after).reshape(-1, d)
    get_w1 = lambda after: gathered(3, "all_gather_wait_w_ff1", after)
    get_w2 = lambda after: gathered(4, "all_gather_wait_w_ff2", after).reshape(d_ff, d)

    in_flight = {}

    def send_partials(name, parts):
        own = lax.dynamic_index_in_dim(parts, me, axis=0, keepdims=False)
        send, recv, src, land, token = _exchange_start("scatter_start_" + name, [parts], [_land_zone(own)], scatter=True)
        in_flight[name] = (send[0], recv[0], src[0], land[0])
        return token

    def on_grad(name, value):
        if name == "w_in":
            main, alow_part = value
            value = _columns_to_shards(main, alow_part, N_DEV, d_in_shard)
        elif name in ("w_out", "w_ff2"):
            value = value.reshape(N_DEV, -1, d)
        return send_partials(name, value)

    grads = _local_step(x2d, tgt, norm1_g, w_main, w_alow, wgu_pad, b_gate, convw_taps, conv_norm_g, gla_norm_g,
                        norm2_g, norm_f_g, get_w_out, get_w1, get_w2, on_grad)
    grad_x = grads["x"]

    small_shapes = [(1, d), (1, d_k), (1, d_conv), (1, gla_norm_g.shape[1]), (1, d), (d,),
                    (GATE_RANK, d_k), (d_conv, CONV_WIDTH), (1,)]
    small_grad_rows = 152
    small_part = _pack_rows(
        [grads["norm1_g"], grads["b_gate"], grads["conv_norm_g"], grads["gla_norm_g"], grads["norm2_g"],
         grads["norm_f_g"], grads["w_gate_up"][:GATE_RANK], grads["conv_w"][:CONV_WIDTH].T, grads["loss"][0, 0]],
        small_grad_rows)
    send_partials("small", jnp.broadcast_to(small_part[None], (N_DEV, small_grad_rows, LANES)))

    def received(name):
        send, recv, src, land = in_flight[name]
        return _exchange_wait("scatter_wait_" + name, send, recv, src, land, grad_x, scatter=True)

    small_r = received("small")
    gin_r, gout_r, g1_r, g2_r = received("w_in"), received("w_out"), received("w_ff1"), received("w_ff2")
    return _update(me, gin_r, gout_r, g1_r, g2_r, small_r, small_shapes, grad_x, dict(
        norm1_g=(norm1_g, m_norm1_g, v_norm1_g), w_in=(w_in, m_w_in, v_w_in),
        w_gate_up=(w_gate_up, m_w_gate_up, v_w_gate_up), b_gate=(b_gate, m_b_gate, v_b_gate),
        conv_w=(conv_w, m_conv_w, v_conv_w), conv_norm_g=(conv_norm_g, m_conv_norm_g, v_conv_norm_g),
        gla_norm_g=(gla_norm_g, m_gla_norm_g, v_gla_norm_g), w_out=(w_out, m_w_out, v_w_out),
        norm2_g=(norm2_g, m_norm2_g, v_norm2_g), w_ff1=(w_ff1, m_w_ff1, v_w_ff1), w_ff2=(w_ff2, m_w_ff2, v_w_ff2),
        norm_f_g=(norm_f_g, m_norm_f_g, v_norm_f_g)))


def _local_step(x2d, tgt, norm1_g, w_main, w_alow, wgu_pad, b_gate, convw_taps, conv_norm_g, gla_norm_g,
                norm2_g, norm_f_g, get_w_out, get_w1, get_w2, on_grad):
    t, d = x2d.shape
    n_main = w_main.shape[1]

    z, u, alow = _inproj(x2d, norm1_g, w_main, w_alow)
    y, sall = _mixer_fwd(z, alow, wgu_pad, b_gate, convw_taps, conv_norm_g, gla_norm_g)
    w_out_full = get_w_out(y)
    x1, h = _outproj(y, w_out_full, x2d, norm2_g)
    w1g = get_w1(h)
    a = _ff1(h, w1g)
    w2_full = get_w2(a)
    d_ff = w2_full.shape[0]
    x2 = _ff2(a, w2_full, x1)
    dx2, dx2b, loss_part, d_normf = _loss_head(x2, norm_f_g.reshape(1, d), tgt)

    tk = min(4096, t)
    nk = t // tk
    da = _dff2(dx2b, w2_full, a)
    dw2 = _tn_matmul(
        "dw_ff2", a, dx2b, (d_ff // 1024, d // 1024, nk),
        pl.BlockSpec((tk, 1024), lambda m, j, kk: (kk, m)), pl.BlockSpec((tk, 1024), lambda m, j, kk: (kk, j)),
        jax.ShapeDtypeStruct((d_ff, d), BF16), pl.BlockSpec((1024, 1024), lambda m, j, kk: (m, j)), (1024, 1024),
        a_fn=_relu_sq)
    token = on_grad("w_ff2", dw2)
    f_shard = d_ff // N_DEV
    dw1 = _tn_matmul(
        "dw_ff1", h, da, (N_DEV, d // 1024, nk),
        pl.BlockSpec((tk, 1024), lambda g, m, kk: (kk, m)), pl.BlockSpec((tk, f_shard), lambda g, m, kk: (kk, g)),
        jax.ShapeDtypeStruct((N_DEV, d, f_shard), BF16), pl.BlockSpec((None, 1024, f_shard), lambda g, m, kk: (g, m, 0)),
        (1024, f_shard), behind=token)
    token = on_grad("w_ff1", dw1)
    dh = _dh(da, w1g, behind=token)
    dx1, dx1b, d_norm2 = _norm_bwd("norm2_bwd", dh, x1, norm2_g, dx2)
    dy = _nt_matmul("dy", dx1b, w_out_full)
    dwout = _tn_matmul(
        "dw_out", y, dx1b, (d // 1024, d // 1024, nk),
        pl.BlockSpec((tk, 1024), lambda m, j, kk: (kk, m)), pl.BlockSpec((tk, 1024), lambda m, j, kk: (kk, j)),
        jax.ShapeDtypeStruct((d, d), BF16), pl.BlockSpec((1024, 1024), lambda m, j, kk: (m, j)), (1024, 1024))
    token = on_grad("w_out", dwout)
    dz, dzal, d_convw, d_convg, d_glag, d_bgate, d_wgu = _mixer_bwd(
        z, alow, dy, sall, wgu_pad, b_gate, convw_taps, conv_norm_g, gla_norm_g, behind=token)
    dwin_main = _tn_matmul(
        "dw_in", u, dz, (d // 1024, n_main // 1024, nk),
        pl.BlockSpec((tk, 1024), lambda m, j, kk: (kk, m)), pl.BlockSpec((tk, 1024), lambda m, j, kk: (kk, j)),
        jax.ShapeDtypeStruct((d, n_main), BF16), pl.BlockSpec((1024, 1024), lambda m, j, kk: (m, j)), (1024, 1024))
    dwin_alow = _tn_matmul(
        "dw_in_alow", u, dzal, (d // 1024, 1, nk),
        pl.BlockSpec((tk, 1024), lambda m, j, kk: (kk, m)), pl.BlockSpec((tk, LANES), lambda m, j, kk: (kk, 0)),
        jax.ShapeDtypeStruct((d, LANES), BF16), pl.BlockSpec((1024, LANES), lambda m, j, kk: (m, 0)), (1024, LANES))
    token = on_grad("w_in", (dwin_main, dwin_alow))
    du = _du(dz, w_main, dzal, w_alow, behind=token)
    grad_x, _, d_norm1 = _norm_bwd("norm1_bwd", du, x2d, norm1_g, dx1)
    return dict(x=grad_x, loss=loss_part, norm1_g=d_norm1, w_gate_up=d_wgu, b_gate=d_bgate, conv_w=d_convw,
                conv_norm_g=d_convg, gla_norm_g=d_glag, norm2_g=d_norm2, norm_f_g=d_normf)


_WEIGHT_ORDER = ("norm1_g", "w_in", "w_gate_up", "b_gate", "conv_w", "conv_norm_g", "gla_norm_g", "w_out", "norm2_g",
                 "w_ff1", "w_ff2", "norm_f_g")
_SMALL_ORDER = ("norm1_g", "b_gate", "conv_norm_g", "gla_norm_g", "norm2_g", "norm_f_g", "w_gate_up", "conv_w")


def _update(me, gin_r, gout_r, g1_r, g2_r, small_r, small_shapes, grad_x, wmv):
    big = {
        "w_in": _adamw("adamw_w_in", gin_r, *(a[0] for a in wmv["w_in"]), 256),
        "w_out": _adamw("adamw_w_out", gout_r, *(a[0] for a in wmv["w_out"]), 128),
        "w_ff1": _adamw("adamw_w_ff1", g1_r, *(a[0] for a in wmv["w_ff1"]), 256),
        "w_ff2": _adamw("adamw_w_ff2", g2_r, *(a[0] for a in wmv["w_ff2"]), 128),
    }

    wgu_cols = wmv["w_gate_up"][0].shape[2]
    cw_rows = wmv["conv_w"][0].shape[1]

    def local_block(flat_block):
        parts = _unpack_rows(flat_block, small_shapes)
        parts[6] = lax.dynamic_slice_in_dim(parts[6], me * wgu_cols, wgu_cols, axis=1)
        parts[7] = lax.dynamic_slice_in_dim(parts[7], me * cw_rows, cw_rows, axis=0)
        return parts

    local_shapes = small_shapes[:6] + [(GATE_RANK, wgu_cols), (cw_rows, CONV_WIDTH), (1,)]
    local_rows = 80
    parts_local = jnp.stack([_pack_rows(local_block(small_r[j]), local_rows) for j in range(N_DEV)])
    extra = (jnp.zeros((1,), F32), jnp.zeros((1,), F32), jnp.ones((1,), F32))
    packed = [_pack_rows([wmv[nm][k] for nm in _SMALL_ORDER] + [extra[k]], local_rows) for k in range(3)]
    out_small = _adamw("adamw_small", parts_local, *packed, local_rows)
    unpacked = [_unpack_rows(o, local_shapes) for o in out_small]

    outs = []
    for k in range(4):
        for nm in _WEIGHT_ORDER:
            if nm in big:
                outs.append(big[nm][k][None])
            else:
                val = unpacked[k][_SMALL_ORDER.index(nm)]
                outs.append(val.reshape(wmv[nm][0].shape))
    loss = unpacked[0][8][0]
    return (loss, grad_x[None], *outs)
```

```python
import functools

import jax
import jax.numpy as jnp
from jax import lax
from jax.experimental import pallas as pl
from jax.experimental.pallas import tpu as pltpu

F32 = jnp.float32
BF16 = jnp.bfloat16

N_DEV = 8
CHUNK = 64
GLA_HEADS = 4
CONV_GROUPS = 8
CONV_WIDTH = 3
GATE_RANK = 16
GATE_NORMALIZER = 16.0
EPS = 1e-6
ADAM_LR = 0.001
ADAM_B1 = 0.9
ADAM_B2 = 0.999
ADAM_EPS = 1e-08
ADAM_WD = 0.01
ADAM_STEP = 10

LANES = 128
SUBLANES = 8
VMEM_LIMIT = 56 << 20

_NN = (((1,), (0,)), ((), ()))
_NT = (((1,), (1,)), ((), ()))
_TN = (((0,), (0,)), ((), ()))


def _dot(a, b, dims=_NN):
    return lax.dot_general(a, b, dims, preferred_element_type=F32)


def _params(n_grid):
    return pltpu.CompilerParams(dimension_semantics=("arbitrary",) * n_grid, vmem_limit_bytes=VMEM_LIMIT)


def _relu_sq(a):
    r = jnp.maximum(a, 0.0)
    return r * r


def _device_index():
    return 4 * lax.axis_index("x") + 2 * lax.axis_index("y") + lax.axis_index("c")


def _peer(mask):
    x, y, c = lax.axis_index("x"), lax.axis_index("y"), lax.axis_index("c")
    return (x ^ ((mask >> 2) & 1), y ^ ((mask >> 1) & 1), c ^ (mask & 1))


_HBM_SPEC = pl.BlockSpec(memory_space=pltpu.HBM)
_SEM_SPEC = pl.BlockSpec(memory_space=pltpu.SEMAPHORE)
_SIDE_EFFECT = pltpu.SideEffectType.DATAFLOW_SIDE_EFFECTING
N_PEERS = N_DEV - 1


def _exchange_copy(src_ref, land_ref, send_sems, recv_sems, mask, scatter, arriving):
    me = _device_index()
    src = src_ref.at[me ^ mask] if scatter else src_ref
    dst = land_ref.at[(me ^ mask) if arriving else me]
    return pltpu.make_async_remote_copy(
        src_ref=src, dst_ref=dst, send_sem=send_sems.at[mask - 1], recv_sem=recv_sems.at[mask - 1],
        device_id=_peer(mask), device_id_type=pl.DeviceIdType.MESH)


def _land_zone(own):
    zone = lax.empty((N_DEV,) + own.shape, own.dtype)
    return lax.dynamic_update_slice(zone, own[None], (_device_index(),) + (0,) * own.ndim)


ALL_PEERS = tuple(range(1, N_DEV))
SIBLING = 1
SAME_CORE_PEERS = (2, 4, 6)


def _exchange_start(name, srcs, lands, scatter, masks=None):
    n = len(srcs)
    masks = masks or [ALL_PEERS] * n

    def body(*refs):
        src, land = refs[:n], refs[n:2 * n]
        send_sems, recv_sems = refs[2 * n:3 * n], refs[3 * n:4 * n]
        token = refs[-1]
        for a in range(n):
            for mask in masks[a]:
                _exchange_copy(src[a], land[a], send_sems[a], recv_sems[a], mask, scatter, False).start()
        token[...] = jnp.zeros_like(token)

    hbm = lambda a: pltpu.HBM(a.shape, a.dtype)
    outs = pl.pallas_call(
        body, name=name,
        out_shape=([pltpu.SemaphoreType.DMA((N_PEERS,))] * (2 * n) + [hbm(a) for a in srcs] + [hbm(a) for a in lands]
                   + [jax.ShapeDtypeStruct((SUBLANES, LANES), F32)]),
        in_specs=[_HBM_SPEC] * (2 * n),
        out_specs=[_SEM_SPEC] * (2 * n) + [_HBM_SPEC] * (2 * n) + [pl.BlockSpec(memory_space=pltpu.VMEM)],
        input_output_aliases={a: 2 * n + a for a in range(2 * n)},
        compiler_params=pltpu.CompilerParams(has_side_effects=_SIDE_EFFECT),
    )(*[pltpu.with_memory_space_constraint(a, pltpu.HBM) for a in list(srcs) + list(lands)])
    send_sems, recv_sems = outs[:n], outs[n:2 * n]
    src_thru, land_thru = outs[2 * n:3 * n], outs[3 * n:4 * n]
    return send_sems, recv_sems, src_thru, land_thru, outs[-1]


def _exchange_wait(name, send_sems, recv_sems, src_thru, land_thru, after, scatter, masks=ALL_PEERS):
    def body(src_ref, land_ref, send_ref, recv_ref, after_ref, src_dead, got_ref):
        for mask in masks:
            cp = _exchange_copy(src_ref, land_ref, send_ref, recv_ref, mask, scatter, True)
            cp.wait_send()
            cp.wait_recv()

    return pl.pallas_call(
        body, name=name,
        out_shape=(pltpu.HBM(src_thru.shape, src_thru.dtype), pltpu.HBM(land_thru.shape, land_thru.dtype)),
        in_specs=[_HBM_SPEC, _HBM_SPEC, _SEM_SPEC, _SEM_SPEC, pl.BlockSpec(memory_space=pl.ANY)],
        out_specs=(_HBM_SPEC, _HBM_SPEC), input_output_aliases={0: 0, 1: 1},
        compiler_params=pltpu.CompilerParams(has_side_effects=_SIDE_EFFECT),
    )(src_thru, land_thru, send_sems, recv_sems, after)[1]


def _forward_copy(land_ref, send_sems, recv_sems, k, arriving):
    me = _device_index()
    slot = me ^ SAME_CORE_PEERS[k]
    return pltpu.make_async_remote_copy(
        src_ref=land_ref.at[slot], dst_ref=land_ref.at[(slot ^ SIBLING) if arriving else slot],
        send_sem=send_sems.at[k], recv_sem=recv_sems.at[k],
        device_id=_peer(SIBLING), device_id_type=pl.DeviceIdType.MESH)


def _forward_start(name, land):
    n_fwd = len(SAME_CORE_PEERS)

    def body(land_ref, send_sems, recv_sems, land_thru):
        for k in range(n_fwd):
            _forward_copy(land_ref, send_sems, recv_sems, k, False).start()

    send, recv, thru = pl.pallas_call(
        body, name=name,
        out_shape=[pltpu.SemaphoreType.DMA((n_fwd,)), pltpu.SemaphoreType.DMA((n_fwd,)), pltpu.HBM(land.shape, land.dtype)],
        in_specs=[_HBM_SPEC], out_specs=[_SEM_SPEC, _SEM_SPEC, _HBM_SPEC], input_output_aliases={0: 2},
        compiler_params=pltpu.CompilerParams(has_side_effects=_SIDE_EFFECT),
    )(pltpu.with_memory_space_constraint(land, pltpu.HBM))
    return send, recv, thru


def _forward_wait(name, send_sems, recv_sems, land_thru):
    def body(land_ref, send_ref, recv_ref, got_ref):
        for k in range(len(SAME_CORE_PEERS)):
            cp = _forward_copy(land_ref, send_ref, recv_ref, k, True)
            cp.wait_send()
            cp.wait_recv()

    return pl.pallas_call(
        body, name=name, out_shape=pltpu.HBM(land_thru.shape, land_thru.dtype),
        in_specs=[_HBM_SPEC, _SEM_SPEC, _SEM_SPEC], out_specs=_HBM_SPEC, input_output_aliases={0: 0},
        compiler_params=pltpu.CompilerParams(has_side_effects=_SIDE_EFFECT),
    )(land_thru, send_sems, recv_sems)


def _shards_to_columns(g, n_main, tr=256):
    n_dev, d, s = g.shape

    def body(g_ref, main_ref, rest_ref):
        for j in range(n_dev):
            lo, hi = j * s, (j + 1) * s
            if hi <= n_main:
                main_ref[:, lo:hi] = g_ref[j]
            else:
                main_ref[:, lo:n_main] = g_ref[j, :, 0:n_main - lo]
                rest_ref[...] = jnp.zeros_like(rest_ref)
                rest_ref[:, 0:hi - n_main] = g_ref[j, :, n_main - lo:s]

    return pl.pallas_call(
        body, grid=(d // tr,), name="shards_to_columns",
        in_specs=[pl.BlockSpec((n_dev, tr, s), lambda i: (0, i, 0))],
        out_specs=[pl.BlockSpec((tr, n_main), lambda i: (i, 0)), pl.BlockSpec((tr, LANES), lambda i: (i, 0))],
        out_shape=[jax.ShapeDtypeStruct((d, n_main), g.dtype), jax.ShapeDtypeStruct((d, LANES), g.dtype)],
        compiler_params=_params(1),
    )(g)


def _columns_to_shards(main, rest, n_dev, s, tr=256):
    d, n_main = main.shape
    assert (n_dev - 1) * s <= n_main < n_dev * s

    def body(main_ref, rest_ref, o_ref):
        for j in range(n_dev):
            lo, hi = j * s, (j + 1) * s
            if hi <= n_main:
                o_ref[j] = main_ref[:, lo:hi]
            else:
                o_ref[j, :, 0:n_main - lo] = main_ref[:, lo:n_main]
                o_ref[j, :, n_main - lo:s] = rest_ref[:, 0:hi - n_main]

    return pl.pallas_call(
        body, grid=(d // tr,), name="columns_to_shards",
        in_specs=[pl.BlockSpec((tr, n_main), lambda i: (i, 0)), pl.BlockSpec((tr, LANES), lambda i: (i, 0))],
        out_specs=pl.BlockSpec((n_dev, tr, s), lambda i: (0, i, 0)),
        out_shape=jax.ShapeDtypeStruct((n_dev, d, s), main.dtype),
        compiler_params=_params(1),
    )(main, rest)


def _inproj(x, g1, w_main, w_alow, tm=1024, tn=1024):
    t, d = x.shape
    tm = min(tm, t)
    n = w_main.shape[1]

    def body(x_ref, g_ref, w_ref, wa_ref, z_ref, u_ref, al_ref):
        @pl.when(pl.program_id(1) == 0)
        def _():
            xf = x_ref[...]
            r = lax.rsqrt(jnp.mean(xf * xf, axis=-1, keepdims=True) + EPS)
            u = (xf * r * g_ref[...]).astype(BF16)
            u_ref[...] = u
            al_ref[...] = _dot(u, wa_ref[...])

        z_ref[...] = _dot(u_ref[...], w_ref[...])

    return pl.pallas_call(
        body, name="rmsnorm_inproj", grid=(t // tm, n // tn),
        in_specs=[pl.BlockSpec((tm, d), lambda m, j: (m, 0)), pl.BlockSpec((1, d), lambda m, j: (0, 0)),
                  pl.BlockSpec((d, tn), lambda m, j: (0, j)), pl.BlockSpec((d, LANES), lambda m, j: (0, 0))],
        out_specs=[pl.BlockSpec((tm, tn), lambda m, j: (m, j)), pl.BlockSpec((tm, d), lambda m, j: (m, 0)),
                   pl.BlockSpec((tm, LANES), lambda m, j: (m, 0))],
        out_shape=[jax.ShapeDtypeStruct((t, n), F32), jax.ShapeDtypeStruct((t, d), BF16),
                   jax.ShapeDtypeStruct((t, LANES), F32)],
        compiler_params=_params(2),
    )(x, g1, w_main, w_alow)


def _outproj(y, w_out, x, g2, tm=512):
    t, d = x.shape
    tm = min(tm, t)
    k = y.shape[1]

    def body(y_ref, w_ref, x_ref, g_ref, x1_ref, h_ref):
        x1 = x_ref[...] + _dot(y_ref[...], w_ref[...])
        x1_ref[...] = x1
        r = lax.rsqrt(jnp.mean(x1 * x1, axis=-1, keepdims=True) + EPS)
        h_ref[...] = (x1 * r * g_ref[...]).astype(BF16)

    return pl.pallas_call(
        body, name="outproj_rmsnorm", grid=(t // tm,),
        in_specs=[pl.BlockSpec((tm, k), lambda m: (m, 0)), pl.BlockSpec((k, d), lambda m: (0, 0)),
                  pl.BlockSpec((tm, d), lambda m: (m, 0)), pl.BlockSpec((1, d), lambda m: (0, 0))],
        out_specs=[pl.BlockSpec((tm, d), lambda m: (m, 0)), pl.BlockSpec((tm, d), lambda m: (m, 0))],
        out_shape=[jax.ShapeDtypeStruct((t, d), F32), jax.ShapeDtypeStruct((t, d), BF16)],
        compiler_params=_params(1),
    )(y, w_out, x, g2)


def _ff1(h, w1g, tm=1024):
    t, d = h.shape
    tm = min(tm, t)
    g, _, f = w1g.shape

    def body(h_ref, w_ref, a_ref):
        a_ref[...] = _dot(h_ref[...], w_ref[...]).astype(BF16)

    return pl.pallas_call(
        body, name="ff1", grid=(t // tm, g),
        in_specs=[pl.BlockSpec((tm, d), lambda m, j: (m, 0)), pl.BlockSpec((None, d, f), lambda m, j: (j, 0, 0))],
        out_specs=pl.BlockSpec((tm, f), lambda m, j: (m, j)),
        out_shape=jax.ShapeDtypeStruct((t, g * f), BF16),
        compiler_params=_params(2),
    )(h, w1g)


def _ff2(a, w2, x1, tm=1024, tn=1024, tk=2048):
    t, f = a.shape
    tm = min(tm, t)
    d = w2.shape[1]

    def body(a_ref, w_ref, x1_ref, o_ref):
        @pl.when(pl.program_id(2) == 0)
        def _():
            o_ref[...] = x1_ref[...]

        o_ref[...] += _dot(_relu_sq(a_ref[...]), w_ref[...])

    return pl.pallas_call(
        body, name="ff2_residual", grid=(t // tm, d // tn, f // tk),
        in_specs=[pl.BlockSpec((tm, tk), lambda m, j, kk: (m, kk)), pl.BlockSpec((tk, tn), lambda m, j, kk: (kk, j)),
                  pl.BlockSpec((tm, tn), lambda m, j, kk: (m, j))],
        out_specs=pl.BlockSpec((tm, tn), lambda m, j, kk: (m, j)),
        out_shape=jax.ShapeDtypeStruct((t, d), F32),
        compiler_params=_params(3),
    )(a, w2, x1)


def _dff2(dx2b, w2, a, tm=1024, tn=1024):
    t, d = dx2b.shape
    tm = min(tm, t)
    f = w2.shape[0]

    def body(g_ref, w_ref, a_ref, o_ref):
        dp = _dot(g_ref[...], w_ref[...], _NT)
        o_ref[...] = (dp * (2.0 * jnp.maximum(a_ref[...].astype(F32), 0.0))).astype(BF16)

    return pl.pallas_call(
        body, name="dff2", grid=(t // tm, f // tn),
        in_specs=[pl.BlockSpec((tm, d), lambda m, j: (m, 0)), pl.BlockSpec((tn, d), lambda m, j: (j, 0)),
                  pl.BlockSpec((tm, tn), lambda m, j: (m, j))],
        out_specs=pl.BlockSpec((tm, tn), lambda m, j: (m, j)),
        out_shape=jax.ShapeDtypeStruct((t, f), BF16),
        compiler_params=_params(2),
    )(dx2b, w2, a)


def _behind(token):
    if token is None:
        return [], []
    return [token], [pl.BlockSpec(token.shape, lambda *_: (0,) * token.ndim)]


def _tn_matmul(name, a, b, grid, a_spec, b_spec, out_shape, out_spec, acc_shape, a_fn=None, behind=None):
    nk = grid[-1]
    dep_args, dep_specs = _behind(behind)

    def body(a_ref, b_ref, *rest):
        o_ref, acc_ref = rest[-2:]
        kk = pl.program_id(len(grid) - 1)
        av = a_ref[...]
        if a_fn is not None:
            av = a_fn(av)
        part = _dot(av, b_ref[...], _TN)

        @pl.when(kk == 0)
        def _():
            acc_ref[...] = part

        @pl.when(kk > 0)
        def _():
            acc_ref[...] += part

        @pl.when(kk == nk - 1)
        def _():
            o_ref[...] = acc_ref[...].astype(o_ref.dtype)

    return pl.pallas_call(
        body, name=name, grid=grid, in_specs=[a_spec, b_spec] + dep_specs, out_specs=out_spec, out_shape=out_shape,
        scratch_shapes=[pltpu.VMEM(acc_shape, F32)], compiler_params=_params(len(grid)),
    )(a, b, *dep_args)


def _dh(da, w1g, tm=512, tn=512, behind=None):
    t = da.shape[0]
    tm = min(tm, t)
    g, d, f = w1g.shape
    dep_args, dep_specs = _behind(behind)

    def body(a_ref, w_ref, *rest):
        o_ref = rest[-1]
        acc = _dot(a_ref[:, 0:f], w_ref[0], _NT)
        for s in range(1, g):
            acc = acc + _dot(a_ref[:, s * f:(s + 1) * f], w_ref[s], _NT)
        o_ref[...] = acc

    return pl.pallas_call(
        body, name="dh", grid=(t // tm, d // tn),
        in_specs=[pl.BlockSpec((tm, g * f), lambda m, j: (m, 0)),
                  pl.BlockSpec((g, tn, f), lambda m, j: (0, j, 0))] + dep_specs,
        out_specs=pl.BlockSpec((tm, tn), lambda m, j: (m, j)),
        out_shape=jax.ShapeDtypeStruct((t, d), F32),
        compiler_params=_params(2),
    )(da, w1g, *dep_args)


def _nt_matmul(name, a, b, tm=1024, tn=1024):
    t, k = a.shape
    tm = min(tm, t)
    n = b.shape[0]

    def body(a_ref, b_ref, o_ref):
        o_ref[...] = _dot(a_ref[...], b_ref[...], _NT)

    return pl.pallas_call(
        body, name=name, grid=(t // tm, n // tn),
        in_specs=[pl.BlockSpec((tm, k), lambda m, j: (m, 0)), pl.BlockSpec((tn, k), lambda m, j: (j, 0))],
        out_specs=pl.BlockSpec((tm, tn), lambda m, j: (m, j)),
        out_shape=jax.ShapeDtypeStruct((t, n), F32),
        compiler_params=_params(2),
    )(a, b)


def _du(dz, w_main, dzal, w_alow, tm=1024, tn=1024, tk=3072, behind=None):
    t, n = dz.shape
    tm = min(tm, t)
    d = w_main.shape[0]
    dep_args, dep_specs = _behind(behind)

    def body(a_ref, w_ref, al_ref, wa_ref, *rest):
        o_ref = rest[-1]

        @pl.when(pl.program_id(2) == 0)
        def _():
            o_ref[...] = _dot(al_ref[...], wa_ref[...], _NT)

        o_ref[...] += _dot(a_ref[...], w_ref[...], _NT)

    return pl.pallas_call(
        body, name="du", grid=(t // tm, d // tn, n // tk),
        in_specs=[pl.BlockSpec((tm, tk), lambda m, j, kk: (m, kk)), pl.BlockSpec((tn, tk), lambda m, j, kk: (j, kk)),
                  pl.BlockSpec((tm, LANES), lambda m, j, kk: (m, 0)), pl.BlockSpec((tn, LANES), lambda m, j, kk: (j, 0))]
        + dep_specs,
        out_specs=pl.BlockSpec((tm, tn), lambda m, j, kk: (m, j)),
        out_shape=jax.ShapeDtypeStruct((t, d), F32),
        compiler_params=_params(3),
    )(dz, w_main, dzal, w_alow, *dep_args)


def _loss_head(x2, gf, tgt, tr=256):
    t, d = x2.shape

    def body(x_ref, g_ref, t_ref, dx_ref, dxb_ref, loss_ref, dg_ref):
        @pl.when(pl.program_id(0) == 0)
        def _():
            loss_ref[...] = jnp.zeros_like(loss_ref)
            dg_ref[...] = jnp.zeros_like(dg_ref)

        xf = x_ref[...]
        g = g_ref[...]
        r = lax.rsqrt(jnp.mean(xf * xf, axis=-1, keepdims=True) + EPS)
        xh = xf * r
        e = xh * g - t_ref[...]
        loss_ref[...] += 0.5 * jnp.sum(jnp.mean(e * e, axis=-1, keepdims=True))
        dy = e * (1.0 / d)
        dg_ref[...] += jnp.sum(dy * xh, axis=0, keepdims=True)
        dyg = dy * g
        dx = r * (dyg - xh * jnp.mean(dyg * xh, axis=-1, keepdims=True))
        dx_ref[...] = dx
        dxb_ref[...] = dx.astype(BF16)

    return pl.pallas_call(
        body, name="loss_head", grid=(t // tr,),
        in_specs=[pl.BlockSpec((tr, d), lambda i: (i, 0)), pl.BlockSpec((1, d), lambda i: (0, 0)),
                  pl.BlockSpec((tr, d), lambda i: (i, 0))],
        out_specs=[pl.BlockSpec((tr, d), lambda i: (i, 0)), pl.BlockSpec((tr, d), lambda i: (i, 0)),
                   pl.BlockSpec((SUBLANES, LANES), lambda i: (0, 0)), pl.BlockSpec((1, d), lambda i: (0, 0))],
        out_shape=[jax.ShapeDtypeStruct((t, d), F32), jax.ShapeDtypeStruct((t, d), BF16),
                   jax.ShapeDtypeStruct((SUBLANES, LANES), F32), jax.ShapeDtypeStruct((1, d), F32)],
        compiler_params=_params(1),
    )(x2, gf, tgt)


def _norm_bwd(name, dh, xin, g, dres, tr=256):
    t, d = xin.shape

    def body(dh_ref, x_ref, g_ref, dr_ref, dx_ref, dxb_ref, dg_ref):
        @pl.when(pl.program_id(0) == 0)
        def _():
            dg_ref[...] = jnp.zeros_like(dg_ref)

        xf = x_ref[...]
        dhv = dh_ref[...]
        r = lax.rsqrt(jnp.mean(xf * xf, axis=-1, keepdims=True) + EPS)
        xh = xf * r
        dg_ref[...] += jnp.sum(dhv * xh, axis=0, keepdims=True)
        dyg = dhv * g_ref[...]
        dx = dr_ref[...] + r * (dyg - xh * jnp.mean(dyg * xh, axis=-1, keepdims=True))
        dx_ref[...] = dx
        dxb_ref[...] = dx.astype(BF16)

    return pl.pallas_call(
        body, name=name, grid=(t // tr,),
        in_specs=[pl.BlockSpec((tr, d), lambda i: (i, 0)), pl.BlockSpec((tr, d), lambda i: (i, 0)),
                  pl.BlockSpec((1, d), lambda i: (0, 0)), pl.BlockSpec((tr, d), lambda i: (i, 0))],
        out_specs=[pl.BlockSpec((tr, d), lambda i: (i, 0)), pl.BlockSpec((tr, d), lambda i: (i, 0)),
                   pl.BlockSpec((1, d), lambda i: (0, 0))],
        out_shape=[jax.ShapeDtypeStruct((t, d), F32), jax.ShapeDtypeStruct((t, d), BF16),
                   jax.ShapeDtypeStruct((1, d), F32)],
        compiler_params=_params(1),
    )(dh, xin, g, dres)


MIX_TILE = 256
CHUNKS_PER_TILE = MIX_TILE // CHUNK
CHUNK_SHIFT = CHUNK.bit_length() - 1
assert 1 << CHUNK_SHIFT == CHUNK


def _chunk_masks(n):
    row = lax.broadcasted_iota(jnp.int32, (n, n), 0)
    col = lax.broadcasted_iota(jnp.int32, (n, n), 1)
    same = lax.shift_right_logical(row, CHUNK_SHIFT) == lax.shift_right_logical(col, CHUNK_SHIFT)
    one = lambda m: jnp.where(m, 1.0, 0.0).astype(BF16)
    return one(same & (col > row)), one(same), one(same & (col < row))


def _mask_dot(mask, x):
    hi = x.astype(BF16)
    r1 = x - hi.astype(F32)
    mid = r1.astype(BF16)
    lo = (r1 - mid.astype(F32)).astype(BF16)
    return _dot(mask, hi) + _dot(mask, mid) + _dot(mask, lo)


def _log_sigmoid(x):
    return jnp.minimum(x, 0.0) - jnp.log1p(jnp.exp(-jnp.abs(x)))


def _conv_taps(prev8, uc, w):
    ext = jnp.concatenate([prev8, uc], axis=0)
    s1 = pltpu.roll(ext, 1, 0)[SUBLANES:]
    s2 = pltpu.roll(ext, 2, 0)[SUBLANES:]
    return s2 * w[0:1] + s1 * w[1:2] + uc * w[2:3], s1, s2


def _z_specs(tile, idx):
    d_conv = 1024
    wide = lambda c: pl.BlockSpec((tile, d_conv), lambda i, c=c: (idx(i), c))
    half = lambda c: pl.BlockSpec((tile, d_conv // 2), lambda i, c=c: (idx(i), c))
    return [wide(0), wide(1), wide(2), half(6), half(7), wide(4), wide(5)]


def _mixer_fwd(z, alow, wgu, b_gate, convw, conv_g, gla_g):
    t = z.shape[0]
    tb, cpt = MIX_TILE, CHUNKS_PER_TILE
    d_conv = conv_g.shape[1]
    dv = gla_g.shape[1]
    dk = dv // 2
    d_k = GLA_HEADS * dk
    gw = d_conv // CONV_GROUPS
    scale = dk ** -0.5

    def body(cb_ref, cc_ref, ch_ref, q_ref, k_ref, v_ref, og_ref, al_ref, wgu_ref, bg_ref, cw_ref, cg_ref, gg_ref,
             y_ref, sall_ref, carry_ref, s_ref):
        @pl.when(pl.program_id(0) == 0)
        def _():
            carry_ref[...] = jnp.zeros_like(carry_ref)
            s_ref[...] = jnp.zeros_like(s_ref)

        uc = cc_ref[...] * ch_ref[...]
        conv, _, _ = _conv_taps(carry_ref[...], uc, cw_ref[...])
        carry_ref[...] = uc[tb - SUBLANES:]
        ypre = cb_ref[...] * conv
        cg = cg_ref[...]
        for g in range(CONV_GROUPS):
            sl = slice(g * gw, (g + 1) * gw)
            seg = ypre[:, sl]
            r = lax.rsqrt(jnp.mean(seg * seg, axis=-1, keepdims=True) + EPS)
            y_ref[:, sl] = (seg * r * cg[:, sl]).astype(BF16)

        later, same, _ = _chunk_masks(tb)
        pre = _dot(al_ref[...].astype(BF16), wgu_ref[...]) + bg_ref[...]
        la = _log_sigmoid(pre) * (1.0 / GATE_NORMALIZER)
        e_dec = _mask_dot(later, la)
        dec_all = jnp.exp(_mask_dot(same, la))
        kdec = (k_ref[...] * jnp.exp(e_dec)).astype(BF16)
        qs = (q_ref[...] * scale).astype(BF16)
        vb = v_ref[...].astype(BF16)
        gg = gg_ref[...]
        rows = [slice(c * CHUNK, (c + 1) * CHUNK) for c in range(cpt)]
        ks = [slice(h * dk, (h + 1) * dk) for h in range(GLA_HEADS)]
        vs = [slice(h * dv, (h + 1) * dv) for h in range(GLA_HEADS)]
        kvt = [[_dot(vb[rows[c], vs[h]], kdec[rows[c], ks[h]], _TN) for h in range(GLA_HEADS)] for c in range(cpt)]
        state = [s_ref[h] for h in range(GLA_HEADS)]
        states = []
        for c in range(cpt):
            state = [state[h] * dec_all[c * CHUNK:c * CHUNK + 1, ks[h]] + kvt[c][h] for h in range(GLA_HEADS)]
            states.append(state)
            for h in range(GLA_HEADS):
                sall_ref[c, h] = state[h]
        for h in range(GLA_HEADS):
            s_ref[h] = state[h]
        for h in range(GLA_HEADS):
            o = jnp.concatenate(
                [_dot(qs[rows[c], ks[h]], states[c][h].astype(BF16), _NT) for c in range(cpt)], axis=0)
            ro = lax.rsqrt(jnp.mean(o * o, axis=-1, keepdims=True) + EPS)
            ogs = og_ref[:, vs[h]]
            yg = o * ro * gg * (ogs * jax.nn.sigmoid(ogs))
            y_ref[:, d_conv + h * dv:d_conv + (h + 1) * dv] = yg.astype(BF16)

    full = lambda shape: pl.BlockSpec(shape, lambda i: (0,) * len(shape))
    return pl.pallas_call(
        body, name="mixer_fwd", grid=(t // tb,),
        in_specs=_z_specs(tb, lambda i: i) + [
            pl.BlockSpec((tb, LANES), lambda i: (i, 0)), full(wgu.shape), full(b_gate.shape), full(convw.shape),
            full(conv_g.shape), full(gla_g.shape)],
        out_specs=[pl.BlockSpec((tb, d_conv + GLA_HEADS * dv), lambda i: (i, 0)),
                   pl.BlockSpec((cpt, GLA_HEADS, dv, dk), lambda i: (i, 0, 0, 0))],
        out_shape=[jax.ShapeDtypeStruct((t, d_conv + GLA_HEADS * dv), BF16),
                   jax.ShapeDtypeStruct((t // CHUNK, GLA_HEADS, dv, dk), F32)],
        scratch_shapes=[pltpu.VMEM((SUBLANES, d_conv), F32), pltpu.VMEM((GLA_HEADS, dv, dk), F32)],
        compiler_params=_params(1),
    )(z, z, z, z, z, z, z, alow, wgu, b_gate, convw, conv_g, gla_g)


def _mixer_bwd(z, alow, dy, sall, wgu, b_gate, convw, conv_g, gla_g, behind=None):
    t = z.shape[0]
    tb, cpt = MIX_TILE, CHUNKS_PER_TILE
    nt = t // tb
    d_conv = conv_g.shape[1]
    dv = gla_g.shape[1]
    dk = dv // 2
    d_k = GLA_HEADS * dk
    gw = d_conv // CONV_GROUPS
    scale = dk ** -0.5
    rev = lambda i: nt - 1 - i
    dep_args, dep_specs = _behind(behind)

    def body(cb_ref, cc_ref, ch_ref, q_ref, k_ref, v_ref, og_ref, ccp_ref, chp_ref, al_ref, dy_ref, sall_ref, sprev_ref,
             wgu_ref, bg_ref, cw_ref, cg_ref, gg_ref, *rest):
        dz_ref, dzal_ref, dcw_ref, dcg_ref, dgg_ref, dbg_ref, dwgu_ref, dcarry_ref, gd_ref = rest[-9:]
        i = pl.program_id(0)

        @pl.when(i == 0)
        def _():
            dcarry_ref[...] = jnp.zeros_like(dcarry_ref)
            gd_ref[...] = jnp.zeros_like(gd_ref)
            dcw_ref[...] = jnp.zeros_like(dcw_ref)
            dcg_ref[...] = jnp.zeros_like(dcg_ref)
            dgg_ref[...] = jnp.zeros_like(dgg_ref)
            dbg_ref[...] = jnp.zeros_like(dbg_ref)
            dwgu_ref[...] = jnp.zeros_like(dwgu_ref)

        first = rev(i) == 0

        cb, cc, ch = cb_ref[...], cc_ref[...], ch_ref[...]
        w = cw_ref[...]
        uc = cc * ch
        prev8 = jnp.where(first, 0.0, ccp_ref[...] * chp_ref[...])
        conv, s1, s2 = _conv_taps(prev8, uc, w)
        ypre = cb * conv
        cg = cg_ref[...]
        dypre_parts = []
        for g in range(CONV_GROUPS):
            sl = slice(g * gw, (g + 1) * gw)
            seg = ypre[:, sl]
            r = lax.rsqrt(jnp.mean(seg * seg, axis=-1, keepdims=True) + EPS)
            yn = seg * r
            dyc = dy_ref[:, sl]
            dcg_ref[:, sl] += jnp.sum(dyc * yn, axis=0, keepdims=True)
            dyn = dyc * cg[:, sl]
            dypre_parts.append(r * (dyn - yn * jnp.mean(dyn * yn, axis=-1, keepdims=True)))
        dypre = jnp.concatenate(dypre_parts, axis=1)
        dconv = dypre * cb
        dz_ref[:, 0:d_conv] = (dypre * conv).astype(BF16)
        dcw_ref[0:1] += jnp.sum(dconv * s2, axis=0, keepdims=True)
        dcw_ref[1:2] += jnp.sum(dconv * s1, axis=0, keepdims=True)
        dcw_ref[2:3] += jnp.sum(dconv * uc, axis=0, keepdims=True)
        ext = jnp.concatenate([dconv, dcarry_ref[...]], axis=0)
        f1 = pltpu.roll(ext, tb + SUBLANES - 1, 0)[:tb]
        f2 = pltpu.roll(ext, tb + SUBLANES - 2, 0)[:tb]
        dcarry_ref[...] = dconv[:SUBLANES]
        duc = dconv * w[2:3] + f1 * w[1:2] + f2 * w[0:1]
        dz_ref[:, d_conv:2 * d_conv] = (duc * ch).astype(BF16)
        dz_ref[:, 2 * d_conv:3 * d_conv] = (duc * cc).astype(BF16)

        q_off = 3 * d_conv
        k_off = q_off + d_k
        v_off = k_off + d_k
        og_off = v_off + GLA_HEADS * dv
        later, same, earlier = _chunk_masks(tb)
        alb = al_ref[...].astype(BF16)
        pre = _dot(alb, wgu_ref[...]) + bg_ref[...]
        la = _log_sigmoid(pre) * (1.0 / GATE_NORMALIZER)
        exp_e = jnp.exp(_mask_dot(later, la))
        dec_all = jnp.exp(_mask_dot(same, la))
        kdec = k_ref[...] * exp_e
        kdec_b = kdec.astype(BF16)
        qs = (q_ref[...] * scale).astype(BF16)
        vb = v_ref[...].astype(BF16)
        gg = gg_ref[...]
        rows = [slice(c * CHUNK, (c + 1) * CHUNK) for c in range(cpt)]
        ks = [slice(h * dk, (h + 1) * dk) for h in range(GLA_HEADS)]
        vs = [slice(h * dv, (h + 1) * dv) for h in range(GLA_HEADS)]
        st_b = [[sall_ref[c, h].astype(BF16) for h in range(GLA_HEADS)] for c in range(cpt)]
        do_b = []
        dgg = jnp.zeros_like(gg)
        for h in range(GLA_HEADS):
            o = jnp.concatenate([_dot(qs[rows[c], ks[h]], st_b[c][h], _NT) for c in range(cpt)], axis=0)
            ro = lax.rsqrt(jnp.mean(o * o, axis=-1, keepdims=True) + EPS)
            on = o * ro
            ogs = og_ref[:, vs[h]]
            sg = jax.nn.sigmoid(ogs)
            gate = ogs * sg
            dyg = dy_ref[:, d_conv + h * dv:d_conv + (h + 1) * dv]
            dgg = dgg + jnp.sum(dyg * on * gate, axis=0, keepdims=True)
            dz_ref[:, og_off + h * dv:og_off + (h + 1) * dv] = (
                dyg * on * gg * (sg * (1.0 + ogs * (1.0 - sg)))).astype(BF16)
            don = dyg * gg * gate
            do_b.append((ro * (don - on * jnp.mean(don * on, axis=-1, keepdims=True))).astype(BF16))
        dgg_ref[...] += dgg
        for h in range(GLA_HEADS):
            dq = jnp.concatenate([_dot(do_b[h][rows[c]], st_b[c][h]) for c in range(cpt)], axis=0)
            dz_ref[:, q_off + h * dk:q_off + (h + 1) * dk] = (dq * scale).astype(BF16)
        own = [[_dot(do_b[h][rows[c]], qs[rows[c], ks[h]], _TN) for h in range(GLA_HEADS)] for c in range(cpt)]
        carried = [gd_ref[h] for h in range(GLA_HEADS)]
        gt_b = [None] * cpt
        ddd = [None] * cpt
        for c in reversed(range(cpt)):
            gt = [own[c][h] + carried[h] for h in range(GLA_HEADS)]
            dec = [dec_all[c * CHUNK:c * CHUNK + 1, ks[h]] for h in range(GLA_HEADS)]
            carried = [gt[h] * dec[h] for h in range(GLA_HEADS)]
            if c > 0:
                st_prev = [sall_ref[c - 1, h] for h in range(GLA_HEADS)]
            else:
                st_prev = [jnp.where(first, 0.0, sprev_ref[0, h]) for h in range(GLA_HEADS)]
            ddec = [jnp.sum(gt[h] * st_prev[h], axis=0, keepdims=True) * dec[h] for h in range(GLA_HEADS)]
            ddd[c] = jnp.broadcast_to(jnp.concatenate(ddec, axis=1), (CHUNK, d_k))
            gt_b[c] = [gt[h].astype(BF16) for h in range(GLA_HEADS)]
        for h in range(GLA_HEADS):
            gd_ref[h] = carried[h]
        dkdec_cols = []
        for h in range(GLA_HEADS):
            dvh = jnp.concatenate([_dot(kdec_b[rows[c], ks[h]], gt_b[c][h], _NT) for c in range(cpt)], axis=0)
            dz_ref[:, v_off + h * dv:v_off + (h + 1) * dv] = dvh.astype(BF16)
            dkdec_cols.append(jnp.concatenate([_dot(vb[rows[c], vs[h]], gt_b[c][h]) for c in range(cpt)], axis=0))
        dkdec = jnp.concatenate(dkdec_cols, axis=1)
        dz_ref[:, k_off:k_off + d_k] = (dkdec * exp_e).astype(BF16)
        dla = _mask_dot(earlier, dkdec * kdec) + jnp.concatenate(ddd, axis=0)
        dpre = dla * (1.0 / GATE_NORMALIZER) * jax.nn.sigmoid(-pre)
        dbg_ref[...] += jnp.sum(dpre, axis=0, keepdims=True)
        dpre_b = dpre.astype(BF16)
        dwgu_ref[...] += _dot(alb, dpre_b, _TN)
        dzal_ref[...] = _dot(dpre_b, wgu_ref[...], _NT).astype(BF16)

    full = lambda shape: pl.BlockSpec(shape, lambda i: (0,) * len(shape))
    prev_rows = lambda c: pl.BlockSpec(
        (SUBLANES, d_conv), lambda i, c=c: (jnp.maximum(rev(i) * (tb // SUBLANES) - 1, 0), c))
    n_z = 3 * d_conv + 2 * d_k + 2 * GLA_HEADS * dv
    return pl.pallas_call(
        body, name="mixer_bwd", grid=(nt,),
        in_specs=_z_specs(tb, rev) + [
            prev_rows(1), prev_rows(2),
            pl.BlockSpec((tb, LANES), lambda i: (rev(i), 0)),
            pl.BlockSpec((tb, d_conv + GLA_HEADS * dv), lambda i: (rev(i), 0)),
            pl.BlockSpec((cpt, GLA_HEADS, dv, dk), lambda i: (rev(i), 0, 0, 0)),
            pl.BlockSpec((1, GLA_HEADS, dv, dk), lambda i: (jnp.maximum(rev(i) * cpt - 1, 0), 0, 0, 0)),
            full(wgu.shape), full(b_gate.shape), full(convw.shape), full(conv_g.shape), full(gla_g.shape)]
        + dep_specs,
        out_specs=[pl.BlockSpec((tb, n_z), lambda i: (rev(i), 0)), pl.BlockSpec((tb, LANES), lambda i: (rev(i), 0)),
                   full(convw.shape), full(conv_g.shape), full(gla_g.shape), full(b_gate.shape), full(wgu.shape)],
        out_shape=[jax.ShapeDtypeStruct((t, n_z), BF16), jax.ShapeDtypeStruct((t, LANES), BF16),
                   jax.ShapeDtypeStruct(convw.shape, F32), jax.ShapeDtypeStruct(conv_g.shape, F32),
                   jax.ShapeDtypeStruct(gla_g.shape, F32), jax.ShapeDtypeStruct(b_gate.shape, F32),
                   jax.ShapeDtypeStruct(wgu.shape, F32)],
        scratch_shapes=[pltpu.VMEM((SUBLANES, d_conv), F32), pltpu.VMEM((GLA_HEADS, dv, dk), F32)],
        compiler_params=_params(1),
    )(z, z, z, z, z, z, z, z, z, alow, dy, sall, sall, wgu, b_gate, convw, conv_g, gla_g, *dep_args)


def _adamw_math(g, w, m, v):
    m = ADAM_B1 * m + (1.0 - ADAM_B1) * g
    v = ADAM_B2 * v + (1.0 - ADAM_B2) * (g * g)
    m_hat = m / (1.0 - ADAM_B1 ** ADAM_STEP)
    v_hat = v / (1.0 - ADAM_B2 ** ADAM_STEP)
    delta = -ADAM_LR * (m_hat / (jnp.sqrt(v_hat) + ADAM_EPS) + ADAM_WD * w)
    return delta, m, v


def _adamw(name, parts, w, m, v, tr):
    r, c = w.shape

    def body(p_ref, w_ref, m_ref, v_ref, g_ref, d_ref, nm_ref, nv_ref):
        g = p_ref[0].astype(F32)
        for j in range(1, N_DEV):
            g = g + p_ref[j].astype(F32)
        g_ref[...] = g
        d_ref[...], nm_ref[...], nv_ref[...] = _adamw_math(g, w_ref[...], m_ref[...], v_ref[...])

    blk = pl.BlockSpec((tr, c), lambda i: (i, 0))
    return pl.pallas_call(
        body, name=name, grid=(r // tr,),
        in_specs=[pl.BlockSpec((N_DEV, tr, c), lambda i: (0, i, 0)), blk, blk, blk],
        out_specs=[blk] * 4, out_shape=[jax.ShapeDtypeStruct((r, c), F32)] * 4,
        compiler_params=_params(1),
    )(parts, w, m, v)


def _pack_rows(vectors, rows):
    flat = jnp.concatenate([a.reshape(-1).astype(F32) for a in vectors])
    return jnp.pad(flat, (0, rows * LANES - flat.shape[0])).reshape(rows, LANES)


def _unpack_rows(block, shapes):
    flat = block.reshape(-1)
    out, off = [], 0
    for s in shapes:
        n = 1
        for dim in s:
            n *= dim
        out.append(flat[off:off + n].reshape(s))
        off += n
    return out


def kernel(x, norm1_g, w_in, w_gate_up, b_gate, conv_w, conv_norm_g, gla_norm_g, w_out, norm2_g, w_ff1, w_ff2, norm_f_g, loss_target, m_norm1_g, m_w_in, m_w_gate_up, m_b_gate, m_conv_w, m_conv_norm_g, m_gla_norm_g, m_w_out, m_norm2_g, m_w_ff1, m_w_ff2, m_norm_f_g, v_norm1_g, v_w_in, v_w_gate_up, v_b_gate, v_conv_w, v_conv_norm_g, v_gla_norm_g, v_w_out, v_norm2_g, v_w_ff1, v_w_ff2, v_norm_f_g):
    me = _device_index()
    x2d, tgt = x[0], loss_target[0]
    t, d = x2d.shape
    d_in_shard = w_in.shape[2]
    d_in = N_DEV * d_in_shard
    n_main = d_in - GATE_RANK
    d_conv = conv_norm_g.shape[1]
    d_k = b_gate.shape[1]
    d_ff = N_DEV * w_ff1.shape[2]

    small_rows = 16
    small_shard = _pack_rows([w_gate_up[0], conv_w[0]], small_rows)
    shards = [small_shard, w_in[0].astype(BF16), w_out[0].astype(BF16), w_ff1[0].astype(BF16), w_ff2[0].astype(BF16)]
    first_level = (SIBLING,) + SAME_CORE_PEERS
    ag_send, ag_recv, ag_src, ag_land, _ = _exchange_start(
        "all_gather_start", shards, [_land_zone(s) for s in shards], scatter=False,
        masks=[ALL_PEERS, first_level, ALL_PEERS, ALL_PEERS, ALL_PEERS])

    def gathered(k, name, after, masks=ALL_PEERS):
        return _exchange_wait(name, ag_send[k], ag_recv[k], ag_src[k], ag_land[k], after, scatter=False, masks=masks)

    small_g = gathered(0, "all_gather_wait_small", x2d)
    win_level1 = gathered(1, "all_gather_wait_w_in", small_g, masks=first_level)
    win_g = _forward_wait("all_gather_wait_w_in_forwarded", *_forward_start("all_gather_forward_w_in", win_level1))
    w_main, w_alow = _shards_to_columns(win_g, n_main)
    small_flat = small_g.reshape(N_DEV, -1)
    n_wgu = GATE_RANK * (d_k // N_DEV)
    wgu_full = small_flat[:, :n_wgu].reshape(N_DEV, GATE_RANK, d_k // N_DEV).transpose(1, 0, 2).reshape(GATE_RANK, d_k)
    conv_w_full = small_flat[:, n_wgu:n_wgu + (d_conv // N_DEV) * CONV_WIDTH].reshape(d_conv, CONV_WIDTH)
    wgu_pad = jnp.pad(wgu_full, ((0, LANES - GATE_RANK), (0, 0))).astype(BF16)
    convw_taps = jnp.pad(conv_w_full.T, ((0, SUBLANES - CONV_WIDTH), (0, 0)))

    get_w_out = lambda after: gathered(2, "all_gather_wait_w_out", after).reshape(-1, d)
    get_w1 = lambda after: gathered(3, "all_gather_wait_w_ff1", after)
    get_w2 = lambda after: gathered(4, "all_gather_wait_w_ff2", after).reshape(d_ff, d)

    in_flight = {}

    def send_partials(name, parts):
        own = lax.dynamic_index_in_dim(parts, me, axis=0, keepdims=False)
        send, recv, src, land, token = _exchange_start("scatter_start_" + name, [parts], [_land_zone(own)], scatter=True)
        in_flight[name] = (send[0], recv[0], src[0], land[0])
        return token

    def on_grad(name, value):
        if name == "w_in":
            main, alow_part = value
            value = _columns_to_shards(main, alow_part, N_DEV, d_in_shard)
        elif name in ("w_out", "w_ff2"):
            value = value.reshape(N_DEV, -1, d)
        return send_partials(name, value)

    grads = _local_step(x2d, tgt, norm1_g, w_main, w_alow, wgu_pad, b_gate, convw_taps, conv_norm_g, gla_norm_g,
                        norm2_g, norm_f_g, get_w_out, get_w1, get_w2, on_grad)
    grad_x = grads["x"]

    small_shapes = [(1, d), (1, d_k), (1, d_conv), (1, gla_norm_g.shape[1]), (1, d), (d,),
                    (GATE_RANK, d_k), (d_conv, CONV_WIDTH), (1,)]
    small_grad_rows = 152
    small_part = _pack_rows(
        [grads["norm1_g"], grads["b_gate"], grads["conv_norm_g"], grads["gla_norm_g"], grads["norm2_g"],
         grads["norm_f_g"], grads["w_gate_up"][:GATE_RANK], grads["conv_w"][:CONV_WIDTH].T, grads["loss"][0, 0]],
        small_grad_rows)
    send_partials("small", jnp.broadcast_to(small_part[None], (N_DEV, small_grad_rows, LANES)))

    def received(name):
        send, recv, src, land = in_flight[name]
        return _exchange_wait("scatter_wait_" + name, send, recv, src, land, grad_x, scatter=True)

    small_r = received("small")
    gin_r, gout_r, g1_r, g2_r = received("w_in"), received("w_out"), received("w_ff1"), received("w_ff2")
    return _update(me, gin_r, gout_r, g1_r, g2_r, small_r, small_shapes, grad_x, dict(
        norm1_g=(norm1_g, m_norm1_g, v_norm1_g), w_in=(w_in, m_w_in, v_w_in),
        w_gate_up=(w_gate_up, m_w_gate_up, v_w_gate_up), b_gate=(b_gate, m_b_gate, v_b_gate),
        conv_w=(conv_w, m_conv_w, v_conv_w), conv_norm_g=(conv_norm_g, m_conv_norm_g, v_conv_norm_g),
        gla_norm_g=(gla_norm_g, m_gla_norm_g, v_gla_norm_g), w_out=(w_out, m_w_out, v_w_out),
        norm2_g=(norm2_g, m_norm2_g, v_norm2_g), w_ff1=(w_ff1, m_w_ff1, v_w_ff1), w_ff2=(w_ff2, m_w_ff2, v_w_ff2),
        norm_f_g=(norm_f_g, m_norm_f_g, v_norm_f_g)))


def _local_step(x2d, tgt, norm1_g, w_main, w_alow, wgu_pad, b_gate, convw_taps, conv_norm_g, gla_norm_g,
                norm2_g, norm_f_g, get_w_out, get_w1, get_w2, on_grad):
    t, d = x2d.shape
    n_main = w_main.shape[1]

    z, u, alow = _inproj(x2d, norm1_g, w_main, w_alow)
    y, sall = _mixer_fwd(z, alow, wgu_pad, b_gate, convw_taps, conv_norm_g, gla_norm_g)
    w_out_full = get_w_out(y)
    x1, h = _outproj(y, w_out_full, x2d, norm2_g)
    w1g = get_w1(h)
    a = _ff1(h, w1g)
    w2_full = get_w2(a)
    d_ff = w2_full.shape[0]
    x2 = _ff2(a, w2_full, x1)
    dx2, dx2b, loss_part, d_normf = _loss_head(x2, norm_f_g.reshape(1, d), tgt)

    tk = min(4096, t)
    nk = t // tk
    da = _dff2(dx2b, w2_full, a)
    dw2 = _tn_matmul(
        "dw_ff2", a, dx2b, (d_ff // 1024, d // 1024, nk),
        pl.BlockSpec((tk, 1024), lambda m, j, kk: (kk, m)), pl.BlockSpec((tk, 1024), lambda m, j, kk: (kk, j)),
        jax.ShapeDtypeStruct((d_ff, d), BF16), pl.BlockSpec((1024, 1024), lambda m, j, kk: (m, j)), (1024, 1024),
        a_fn=_relu_sq)
    token = on_grad("w_ff2", dw2)
    f_shard = d_ff // N_DEV
    dw1 = _tn_matmul(
        "dw_ff1", h, da, (N_DEV, d // 1024, nk),
        pl.BlockSpec((tk, 1024), lambda g, m, kk: (kk, m)), pl.BlockSpec((tk, f_shard), lambda g, m, kk: (kk, g)),
        jax.ShapeDtypeStruct((N_DEV, d, f_shard), BF16), pl.BlockSpec((None, 1024, f_shard), lambda g, m, kk: (g, m, 0)),
        (1024, f_shard), behind=token)
    token = on_grad("w_ff1", dw1)
    dh = _dh(da, w1g, behind=token)
    dx1, dx1b, d_norm2 = _norm_bwd("norm2_bwd", dh, x1, norm2_g, dx2)
    dy = _nt_matmul("dy", dx1b, w_out_full)
    dwout = _tn_matmul(
        "dw_out", y, dx1b, (d // 1024, d // 1024, nk),
        pl.BlockSpec((tk, 1024), lambda m, j, kk: (kk, m)), pl.BlockSpec((tk, 1024), lambda m, j, kk: (kk, j)),
        jax.ShapeDtypeStruct((d, d), BF16), pl.BlockSpec((1024, 1024), lambda m, j, kk: (m, j)), (1024, 1024))
    token = on_grad("w_out", dwout)
    dz, dzal, d_convw, d_convg, d_glag, d_bgate, d_wgu = _mixer_bwd(
        z, alow, dy, sall, wgu_pad, b_gate, convw_taps, conv_norm_g, gla_norm_g, behind=token)
    dwin_main = _tn_matmul(
        "dw_in", u, dz, (d // 1024, n_main // 1024, nk),
        pl.BlockSpec((tk, 1024), lambda m, j, kk: (kk, m)), pl.BlockSpec((tk, 1024), lambda m, j, kk: (kk, j)),
        jax.ShapeDtypeStruct((d, n_main), BF16), pl.BlockSpec((1024, 1024), lambda m, j, kk: (m, j)), (1024, 1024))
    dwin_alow = _tn_matmul(
        "dw_in_alow", u, dzal, (d // 1024, 1, nk),
        pl.BlockSpec((tk, 1024), lambda m, j, kk: (kk, m)), pl.BlockSpec((tk, LANES), lambda m, j, kk: (kk, 0)),
        jax.ShapeDtypeStruct((d, LANES), BF16), pl.BlockSpec((1024, LANES), lambda m, j, kk: (m, 0)), (1024, LANES))
    token = on_grad("w_in", (dwin_main, dwin_alow))
    du = _du(dz, w_main, dzal, w_alow, behind=token)
    grad_x, _, d_norm1 = _norm_bwd("norm1_bwd", du, x2d, norm1_g, dx1)
    return dict(x=grad_x, loss=loss_part, norm1_g=d_norm1, w_gate_up=d_wgu, b_gate=d_bgate, conv_w=d_convw,
                conv_norm_g=d_convg, gla_norm_g=d_glag, norm2_g=d_norm2, norm_f_g=d_normf)


_WEIGHT_ORDER = ("norm1_g", "w_in", "w_gate_up", "b_gate", "conv_w", "conv_norm_g", "gla_norm_g", "w_out", "norm2_g",
                 "w_ff1", "w_ff2", "norm_f_g")
_SMALL_ORDER = ("norm1_g", "b_gate", "conv_norm_g", "gla_norm_g", "norm2_g", "norm_f_g", "w_gate_up", "conv_w")


def _update(me, gin_r, gout_r, g1_r, g2_r, small_r, small_shapes, grad_x, wmv):
    big = {
        "w_in": _adamw("adamw_w_in", gin_r, *(a[0] for a in wmv["w_in"]), 256),
        "w_out": _adamw("adamw_w_out", gout_r, *(a[0] for a in wmv["w_out"]), 128),
        "w_ff1": _adamw("adamw_w_ff1", g1_r, *(a[0] for a in wmv["w_ff1"]), 256),
        "w_ff2": _adamw("adamw_w_ff2", g2_r, *(a[0] for a in wmv["w_ff2"]), 128),
    }

    wgu_cols = wmv["w_gate_up"][0].shape[2]
    cw_rows = wmv["conv_w"][0].shape[1]

    def local_block(flat_block):
        parts = _unpack_rows(flat_block, small_shapes)
        parts[6] = lax.dynamic_slice_in_dim(parts[6], me * wgu_cols, wgu_cols, axis=1)
        parts[7] = lax.dynamic_slice_in_dim(parts[7], me * cw_rows, cw_rows, axis=0)
        return parts

    local_shapes = small_shapes[:6] + [(GATE_RANK, wgu_cols), (cw_rows, CONV_WIDTH), (1,)]
    local_rows = 80
    parts_local = jnp.stack([_pack_rows(local_block(small_r[j]), local_rows) for j in range(N_DEV)])
    extra = (jnp.zeros((1,), F32), jnp.zeros((1,), F32), jnp.ones((1,), F32))
    packed = [_pack_rows([wmv[nm][k] for nm in _SMALL_ORDER] + [extra[k]], local_rows) for k in range(3)]
    out_small = _adamw("adamw_small", parts_local, *packed, local_rows)
    unpacked = [_unpack_rows(o, local_shapes) for o in out_small]

    outs = []
    for k in range(4):
        for nm in _WEIGHT_ORDER:
            if nm in big:
                outs.append(big[nm][k][None])
            else:
                val = unpacked[k][_SMALL_ORDER.index(nm)]
                outs.append(val.reshape(wmv[nm][0].shape))
    loss = unpacked[0][8][0]
    return (loss, grad_x[None], *outs)
```

```python
import functools

import jax
import jax.numpy as jnp
from jax import lax
from jax.experimental import pallas as pl
from jax.experimental.pallas import tpu as pltpu

F32 = jnp.float32
BF16 = jnp.bfloat16

N_DEV = 8
CHUNK = 64
GLA_HEADS = 4
CONV_GROUPS = 8
CONV_WIDTH = 3
GATE_RANK = 16
GATE_NORMALIZER = 16.0
EPS = 1e-6
ADAM_LR = 0.001
ADAM_B1 = 0.9
ADAM_B2 = 0.999
ADAM_EPS = 1e-08
ADAM_WD = 0.01
ADAM_STEP = 10

LANES = 128
SUBLANES = 8
VMEM_LIMIT = 56 << 20

_NN = (((1,), (0,)), ((), ()))
_NT = (((1,), (1,)), ((), ()))
_TN = (((0,), (0,)), ((), ()))


def _dot(a, b, dims=_NN):
    return lax.dot_general(a, b, dims, preferred_element_type=F32)


def _params(n_grid):
    return pltpu.CompilerParams(dimension_semantics=("arbitrary",) * n_grid, vmem_limit_bytes=VMEM_LIMIT)


def _relu_sq(a):
    r = jnp.maximum(a, 0.0)
    return r * r


def _device_index():
    return 4 * lax.axis_index("x") + 2 * lax.axis_index("y") + lax.axis_index("c")


def _peer(mask):
    x, y, c = lax.axis_index("x"), lax.axis_index("y"), lax.axis_index("c")
    return (x ^ ((mask >> 2) & 1), y ^ ((mask >> 1) & 1), c ^ (mask & 1))


_HBM_SPEC = pl.BlockSpec(memory_space=pltpu.HBM)
_SEM_SPEC = pl.BlockSpec(memory_space=pltpu.SEMAPHORE)
_SIDE_EFFECT = pltpu.SideEffectType.DATAFLOW_SIDE_EFFECTING
N_PEERS = N_DEV - 1


def _exchange_copy(src_ref, land_ref, send_sems, recv_sems, mask, scatter, arriving):
    me = _device_index()
    src = src_ref.at[me ^ mask] if scatter else src_ref
    dst = land_ref.at[(me ^ mask) if arriving else me]
    return pltpu.make_async_remote_copy(
        src_ref=src, dst_ref=dst, send_sem=send_sems.at[mask - 1], recv_sem=recv_sems.at[mask - 1],
        device_id=_peer(mask), device_id_type=pl.DeviceIdType.MESH)


def _land_zone(own):
    zone = lax.empty((N_DEV,) + own.shape, own.dtype)
    return lax.dynamic_update_slice(zone, own[None], (_device_index(),) + (0,) * own.ndim)


ALL_PEERS = tuple(range(1, N_DEV))
SIBLING = 1
SAME_CORE_PEERS = (2, 4, 6)


def _exchange_start(name, srcs, lands, scatter, masks=None):
    n = len(srcs)
    masks = masks or [ALL_PEERS] * n

    def body(*refs):
        src, land = refs[:n], refs[n:2 * n]
        send_sems, recv_sems = refs[2 * n:3 * n], refs[3 * n:4 * n]
        token = refs[-1]
        for a in range(n):
            for mask in masks[a]:
                _exchange_copy(src[a], land[a], send_sems[a], recv_sems[a], mask, scatter, False).start()
        token[...] = jnp.zeros_like(token)

    hbm = lambda a: pltpu.HBM(a.shape, a.dtype)
    outs = pl.pallas_call(
        body, name=name,
        out_shape=([pltpu.SemaphoreType.DMA((N_PEERS,))] * (2 * n) + [hbm(a) for a in srcs] + [hbm(a) for a in lands]
                   + [jax.ShapeDtypeStruct((SUBLANES, LANES), F32)]),
        in_specs=[_HBM_SPEC] * (2 * n),
        out_specs=[_SEM_SPEC] * (2 * n) + [_HBM_SPEC] * (2 * n) + [pl.BlockSpec(memory_space=pltpu.VMEM)],
        input_output_aliases={a: 2 * n + a for a in range(2 * n)},
        compiler_params=pltpu.CompilerParams(has_side_effects=_SIDE_EFFECT),
    )(*[pltpu.with_memory_space_constraint(a, pltpu.HBM) for a in list(srcs) + list(lands)])
    send_sems, recv_sems = outs[:n], outs[n:2 * n]
    src_thru, land_thru = outs[2 * n:3 * n], outs[3 * n:4 * n]
    return send_sems, recv_sems, src_thru, land_thru, outs[-1]


def _exchange_wait(name, send_sems, recv_sems, src_thru, land_thru, after, scatter, masks=ALL_PEERS):
    after = list(after) if isinstance(after, (list, tuple)) else [after]

    def body(src_ref, land_ref, send_ref, recv_ref, *rest):
        for mask in masks:
            cp = _exchange_copy(src_ref, land_ref, send_ref, recv_ref, mask, scatter, True)
            cp.wait_send()
            cp.wait_recv()

    return pl.pallas_call(
        body, name=name,
        out_shape=(pltpu.HBM(src_thru.shape, src_thru.dtype), pltpu.HBM(land_thru.shape, land_thru.dtype)),
        in_specs=[_HBM_SPEC, _HBM_SPEC, _SEM_SPEC, _SEM_SPEC] + [pl.BlockSpec(memory_space=pl.ANY)] * len(after),
        out_specs=(_HBM_SPEC, _HBM_SPEC), input_output_aliases={0: 0, 1: 1},
        compiler_params=pltpu.CompilerParams(has_side_effects=_SIDE_EFFECT),
    )(src_thru, land_thru, send_sems, recv_sems, *after)[1]


def _forward_copy(land_ref, send_sems, recv_sems, k, arriving):
    me = _device_index()
    slot = me ^ SAME_CORE_PEERS[k]
    return pltpu.make_async_remote_copy(
        src_ref=land_ref.at[slot], dst_ref=land_ref.at[(slot ^ SIBLING) if arriving else slot],
        send_sem=send_sems.at[k], recv_sem=recv_sems.at[k],
        device_id=_peer(SIBLING), device_id_type=pl.DeviceIdType.MESH)


def _forward_start(name, land):
    n_fwd = len(SAME_CORE_PEERS)

    def body(land_ref, send_sems, recv_sems, land_thru):
        for k in range(n_fwd):
            _forward_copy(land_ref, send_sems, recv_sems, k, False).start()

    send, recv, thru = pl.pallas_call(
        body, name=name,
        out_shape=[pltpu.SemaphoreType.DMA((n_fwd,)), pltpu.SemaphoreType.DMA((n_fwd,)), pltpu.HBM(land.shape, land.dtype)],
        in_specs=[_HBM_SPEC], out_specs=[_SEM_SPEC, _SEM_SPEC, _HBM_SPEC], input_output_aliases={0: 2},
        compiler_params=pltpu.CompilerParams(has_side_effects=_SIDE_EFFECT),
    )(pltpu.with_memory_space_constraint(land, pltpu.HBM))
    return send, recv, thru


def _forward_wait(name, send_sems, recv_sems, land_thru):
    def body(land_ref, send_ref, recv_ref, got_ref):
        for k in range(len(SAME_CORE_PEERS)):
            cp = _forward_copy(land_ref, send_ref, recv_ref, k, True)
            cp.wait_send()
            cp.wait_recv()

    return pl.pallas_call(
        body, name=name, out_shape=pltpu.HBM(land_thru.shape, land_thru.dtype),
        in_specs=[_HBM_SPEC, _SEM_SPEC, _SEM_SPEC], out_specs=_HBM_SPEC, input_output_aliases={0: 0},
        compiler_params=pltpu.CompilerParams(has_side_effects=_SIDE_EFFECT),
    )(land_thru, send_sems, recv_sems)


def _shards_to_columns(g, n_main, tr=256):
    n_dev, d, s = g.shape

    def body(g_ref, main_ref, rest_ref):
        for j in range(n_dev):
            lo, hi = j * s, (j + 1) * s
            if hi <= n_main:
                main_ref[:, lo:hi] = g_ref[j]
            else:
                main_ref[:, lo:n_main] = g_ref[j, :, 0:n_main - lo]
                rest_ref[...] = jnp.zeros_like(rest_ref)
                rest_ref[:, 0:hi - n_main] = g_ref[j, :, n_main - lo:s]

    return pl.pallas_call(
        body, grid=(d // tr,), name="shards_to_columns",
        in_specs=[pl.BlockSpec((n_dev, tr, s), lambda i: (0, i, 0))],
        out_specs=[pl.BlockSpec((tr, n_main), lambda i: (i, 0)), pl.BlockSpec((tr, LANES), lambda i: (i, 0))],
        out_shape=[jax.ShapeDtypeStruct((d, n_main), g.dtype), jax.ShapeDtypeStruct((d, LANES), g.dtype)],
        compiler_params=_params(1),
    )(g)


def _columns_to_shards(main, rest, n_dev, s, tr=256):
    d, n_main = main.shape
    assert (n_dev - 1) * s <= n_main < n_dev * s

    def body(main_ref, rest_ref, o_ref):
        for j in range(n_dev):
            lo, hi = j * s, (j + 1) * s
            if hi <= n_main:
                o_ref[j] = main_ref[:, lo:hi]
            else:
                o_ref[j, :, 0:n_main - lo] = main_ref[:, lo:n_main]
                o_ref[j, :, n_main - lo:s] = rest_ref[:, 0:hi - n_main]

    return pl.pallas_call(
        body, grid=(d // tr,), name="columns_to_shards",
        in_specs=[pl.BlockSpec((tr, n_main), lambda i: (i, 0)), pl.BlockSpec((tr, LANES), lambda i: (i, 0))],
        out_specs=pl.BlockSpec((n_dev, tr, s), lambda i: (0, i, 0)),
        out_shape=jax.ShapeDtypeStruct((n_dev, d, s), main.dtype),
        compiler_params=_params(1),
    )(main, rest)


def _rmsnorm(x, g, tr=512):
    t, d = x.shape
    tr = min(tr, t)

    def body(x_ref, g_ref, u_ref):
        xf = x_ref[...]
        r = lax.rsqrt(jnp.mean(xf * xf, axis=-1, keepdims=True) + EPS)
        u_ref[...] = (xf * r * g_ref[...]).astype(BF16)

    return pl.pallas_call(
        body, name="rmsnorm1", grid=(t // tr,),
        in_specs=[pl.BlockSpec((tr, d), lambda i: (i, 0)), pl.BlockSpec((1, d), lambda i: (0, 0))],
        out_specs=pl.BlockSpec((tr, d), lambda i: (i, 0)),
        out_shape=jax.ShapeDtypeStruct((t, d), BF16),
        compiler_params=_params(1),
    )(x, g)


def _inproj(u, w_main, w_alow, tm=1024, tn=1024):
    t, d = u.shape
    tm = min(tm, t)
    n = w_main.shape[1]

    def body(u_ref, w_ref, wa_ref, z_ref, al_ref):
        @pl.when(pl.program_id(1) == 0)
        def _():
            al_ref[...] = _dot(u_ref[...], wa_ref[...])

        z_ref[...] = _dot(u_ref[...], w_ref[...])

    return pl.pallas_call(
        body, name="inproj", grid=(t // tm, n // tn),
        in_specs=[pl.BlockSpec((tm, d), lambda m, j: (m, 0)), pl.BlockSpec((d, tn), lambda m, j: (0, j)),
                  pl.BlockSpec((d, LANES), lambda m, j: (0, 0))],
        out_specs=[pl.BlockSpec((tm, tn), lambda m, j: (m, j)), pl.BlockSpec((tm, LANES), lambda m, j: (m, 0))],
        out_shape=[jax.ShapeDtypeStruct((t, n), F32), jax.ShapeDtypeStruct((t, LANES), F32)],
        compiler_params=_params(2),
    )(u, w_main, w_alow)


def _outproj(y, w_out, x, g2, tm=512):
    t, d = x.shape
    tm = min(tm, t)
    k = y.shape[1]

    def body(y_ref, w_ref, x_ref, g_ref, x1_ref, h_ref):
        x1 = x_ref[...] + _dot(y_ref[...], w_ref[...])
        x1_ref[...] = x1
        r = lax.rsqrt(jnp.mean(x1 * x1, axis=-1, keepdims=True) + EPS)
        h_ref[...] = (x1 * r * g_ref[...]).astype(BF16)

    return pl.pallas_call(
        body, name="outproj_rmsnorm", grid=(t // tm,),
        in_specs=[pl.BlockSpec((tm, k), lambda m: (m, 0)), pl.BlockSpec((k, d), lambda m: (0, 0)),
                  pl.BlockSpec((tm, d), lambda m: (m, 0)), pl.BlockSpec((1, d), lambda m: (0, 0))],
        out_specs=[pl.BlockSpec((tm, d), lambda m: (m, 0)), pl.BlockSpec((tm, d), lambda m: (m, 0))],
        out_shape=[jax.ShapeDtypeStruct((t, d), F32), jax.ShapeDtypeStruct((t, d), BF16)],
        compiler_params=_params(1),
    )(y, w_out, x, g2)


def _ff1(h, w1g, tm=1024):
    t, d = h.shape
    tm = min(tm, t)
    g, _, f = w1g.shape

    def body(h_ref, w_ref, a_ref):
        a_ref[...] = _dot(h_ref[...], w_ref[...]).astype(BF16)

    return pl.pallas_call(
        body, name="ff1", grid=(t // tm, g),
        in_specs=[pl.BlockSpec((tm, d), lambda m, j: (m, 0)), pl.BlockSpec((None, d, f), lambda m, j: (j, 0, 0))],
        out_specs=pl.BlockSpec((tm, f), lambda m, j: (m, j)),
        out_shape=jax.ShapeDtypeStruct((t, g * f), BF16),
        compiler_params=_params(2),
    )(h, w1g)


def _ff2(a, w2, x1, tm=1024, tn=1024, tk=2048):
    t, f = a.shape
    tm = min(tm, t)
    d = w2.shape[1]

    def body(a_ref, w_ref, x1_ref, o_ref):
        @pl.when(pl.program_id(2) == 0)
        def _():
            o_ref[...] = x1_ref[...]

        o_ref[...] += _dot(_relu_sq(a_ref[...]), w_ref[...])

    return pl.pallas_call(
        body, name="ff2_residual", grid=(t // tm, d // tn, f // tk),
        in_specs=[pl.BlockSpec((tm, tk), lambda m, j, kk: (m, kk)), pl.BlockSpec((tk, tn), lambda m, j, kk: (kk, j)),
                  pl.BlockSpec((tm, tn), lambda m, j, kk: (m, j))],
        out_specs=pl.BlockSpec((tm, tn), lambda m, j, kk: (m, j)),
        out_shape=jax.ShapeDtypeStruct((t, d), F32),
        compiler_params=_params(3),
    )(a, w2, x1)


def _dff2(dx2b, w2, a, tm=1024, tn=1024):
    t, d = dx2b.shape
    tm = min(tm, t)
    f = w2.shape[0]

    def body(g_ref, w_ref, a_ref, o_ref):
        dp = _dot(g_ref[...], w_ref[...], _NT)
        o_ref[...] = (dp * (2.0 * jnp.maximum(a_ref[...].astype(F32), 0.0))).astype(BF16)

    return pl.pallas_call(
        body, name="dff2", grid=(t // tm, f // tn),
        in_specs=[pl.BlockSpec((tm, d), lambda m, j: (m, 0)), pl.BlockSpec((tn, d), lambda m, j: (j, 0)),
                  pl.BlockSpec((tm, tn), lambda m, j: (m, j))],
        out_specs=pl.BlockSpec((tm, tn), lambda m, j: (m, j)),
        out_shape=jax.ShapeDtypeStruct((t, f), BF16),
        compiler_params=_params(2),
    )(dx2b, w2, a)


def _behind(token):
    if token is None:
        return [], []
    return [token], [pl.BlockSpec(token.shape, lambda *_: (0,) * token.ndim)]


def _tn_matmul(name, a, b, grid, a_spec, b_spec, out_shape, out_spec, acc_shape, a_fn=None, behind=None):
    nk = grid[-1]
    dep_args, dep_specs = _behind(behind)

    def body(a_ref, b_ref, *rest):
        o_ref, acc_ref = rest[-2:]
        kk = pl.program_id(len(grid) - 1)
        av = a_ref[...]
        if a_fn is not None:
            av = a_fn(av)
        part = _dot(av, b_ref[...], _TN)

        @pl.when(kk == 0)
        def _():
            acc_ref[...] = part

        @pl.when(kk > 0)
        def _():
            acc_ref[...] += part

        @pl.when(kk == nk - 1)
        def _():
            o_ref[...] = acc_ref[...].astype(o_ref.dtype)

    return pl.pallas_call(
        body, name=name, grid=grid, in_specs=[a_spec, b_spec] + dep_specs, out_specs=out_spec, out_shape=out_shape,
        scratch_shapes=[pltpu.VMEM(acc_shape, F32)], compiler_params=_params(len(grid)),
    )(a, b, *dep_args)


def _dh(da, w1g, tm=512, tn=512, behind=None):
    t = da.shape[0]
    tm = min(tm, t)
    g, d, f = w1g.shape
    dep_args, dep_specs = _behind(behind)

    def body(a_ref, w_ref, *rest):
        o_ref = rest[-1]
        acc = _dot(a_ref[:, 0:f], w_ref[0], _NT)
        for s in range(1, g):
            acc = acc + _dot(a_ref[:, s * f:(s + 1) * f], w_ref[s], _NT)
        o_ref[...] = acc

    return pl.pallas_call(
        body, name="dh", grid=(t // tm, d // tn),
        in_specs=[pl.BlockSpec((tm, g * f), lambda m, j: (m, 0)),
                  pl.BlockSpec((g, tn, f), lambda m, j: (0, j, 0))] + dep_specs,
        out_specs=pl.BlockSpec((tm, tn), lambda m, j: (m, j)),
        out_shape=jax.ShapeDtypeStruct((t, d), F32),
        compiler_params=_params(2),
    )(da, w1g, *dep_args)


def _nt_matmul(name, a, b, tm=1024, tn=1024):
    t, k = a.shape
    tm = min(tm, t)
    n = b.shape[0]

    def body(a_ref, b_ref, o_ref):
        o_ref[...] = _dot(a_ref[...], b_ref[...], _NT)

    return pl.pallas_call(
        body, name=name, grid=(t // tm, n // tn),
        in_specs=[pl.BlockSpec((tm, k), lambda m, j: (m, 0)), pl.BlockSpec((tn, k), lambda m, j: (j, 0))],
        out_specs=pl.BlockSpec((tm, tn), lambda m, j: (m, j)),
        out_shape=jax.ShapeDtypeStruct((t, n), F32),
        compiler_params=_params(2),
    )(a, b)


def _du(dz, w_main, dzal, w_alow, tm=1024, tn=1024, tk=3072, behind=None):
    t, n = dz.shape
    tm = min(tm, t)
    d = w_main.shape[0]
    dep_args, dep_specs = _behind(behind)

    def body(a_ref, w_ref, al_ref, wa_ref, *rest):
        o_ref = rest[-1]

        @pl.when(pl.program_id(2) == 0)
        def _():
            o_ref[...] = _dot(al_ref[...], wa_ref[...], _NT)

        o_ref[...] += _dot(a_ref[...], w_ref[...], _NT)

    return pl.pallas_call(
        body, name="du", grid=(t // tm, d // tn, n // tk),
        in_specs=[pl.BlockSpec((tm, tk), lambda m, j, kk: (m, kk)), pl.BlockSpec((tn, tk), lambda m, j, kk: (j, kk)),
                  pl.BlockSpec((tm, LANES), lambda m, j, kk: (m, 0)), pl.BlockSpec((tn, LANES), lambda m, j, kk: (j, 0))]
        + dep_specs,
        out_specs=pl.BlockSpec((tm, tn), lambda m, j, kk: (m, j)),
        out_shape=jax.ShapeDtypeStruct((t, d), F32),
        compiler_params=_params(3),
    )(dz, w_main, dzal, w_alow, *dep_args)


def _loss_head(x2, gf, tgt, tr=256):
    t, d = x2.shape

    def body(x_ref, g_ref, t_ref, dx_ref, dxb_ref, loss_ref, dg_ref):
        @pl.when(pl.program_id(0) == 0)
        def _():
            loss_ref[...] = jnp.zeros_like(loss_ref)
            dg_ref[...] = jnp.zeros_like(dg_ref)

        xf = x_ref[...]
        g = g_ref[...]
        r = lax.rsqrt(jnp.mean(xf * xf, axis=-1, keepdims=True) + EPS)
        xh = xf * r
        e = xh * g - t_ref[...]
        loss_ref[...] += 0.5 * jnp.sum(jnp.mean(e * e, axis=-1, keepdims=True))
        dy = e * (1.0 / d)
        dg_ref[...] += jnp.sum(dy * xh, axis=0, keepdims=True)
        dyg = dy * g
        dx = r * (dyg - xh * jnp.mean(dyg * xh, axis=-1, keepdims=True))
        dx_ref[...] = dx
        dxb_ref[...] = dx.astype(BF16)

    return pl.pallas_call(
        body, name="loss_head", grid=(t // tr,),
        in_specs=[pl.BlockSpec((tr, d), lambda i: (i, 0)), pl.BlockSpec((1, d), lambda i: (0, 0)),
                  pl.BlockSpec((tr, d), lambda i: (i, 0))],
        out_specs=[pl.BlockSpec((tr, d), lambda i: (i, 0)), pl.BlockSpec((tr, d), lambda i: (i, 0)),
                   pl.BlockSpec((SUBLANES, LANES), lambda i: (0, 0)), pl.BlockSpec((1, d), lambda i: (0, 0))],
        out_shape=[jax.ShapeDtypeStruct((t, d), F32), jax.ShapeDtypeStruct((t, d), BF16),
                   jax.ShapeDtypeStruct((SUBLANES, LANES), F32), jax.ShapeDtypeStruct((1, d), F32)],
        compiler_params=_params(1),
    )(x2, gf, tgt)


def _norm_bwd(name, dh, xin, g, dres, with_bf16, tr=256):
    t, d = xin.shape

    def body(dh_ref, x_ref, g_ref, dr_ref, dx_ref, *rest):
        dg_ref = rest[-1]

        @pl.when(pl.program_id(0) == 0)
        def _():
            dg_ref[...] = jnp.zeros_like(dg_ref)

        xf = x_ref[...]
        dhv = dh_ref[...]
        r = lax.rsqrt(jnp.mean(xf * xf, axis=-1, keepdims=True) + EPS)
        xh = xf * r
        dg_ref[...] += jnp.sum(dhv * xh, axis=0, keepdims=True)
        dyg = dhv * g_ref[...]
        dx = dr_ref[...] + r * (dyg - xh * jnp.mean(dyg * xh, axis=-1, keepdims=True))
        dx_ref[...] = dx
        if with_bf16:
            rest[0][...] = dx.astype(BF16)

    rows = pl.BlockSpec((tr, d), lambda i: (i, 0))
    vec = pl.BlockSpec((1, d), lambda i: (0, 0))
    return pl.pallas_call(
        body, name=name, grid=(t // tr,),
        in_specs=[rows, rows, vec, rows],
        out_specs=[rows] + [rows] * with_bf16 + [vec],
        out_shape=[jax.ShapeDtypeStruct((t, d), F32)] + [jax.ShapeDtypeStruct((t, d), BF16)] * with_bf16
        + [jax.ShapeDtypeStruct((1, d), F32)],
        compiler_params=_params(1),
    )(dh, xin, g, dres)


MIX_TILE = 256
CHUNKS_PER_TILE = MIX_TILE // CHUNK
CHUNK_SHIFT = CHUNK.bit_length() - 1
assert 1 << CHUNK_SHIFT == CHUNK


def _chunk_masks(n):
    row = lax.broadcasted_iota(jnp.int32, (n, n), 0)
    col = lax.broadcasted_iota(jnp.int32, (n, n), 1)
    same = lax.shift_right_logical(row, CHUNK_SHIFT) == lax.shift_right_logical(col, CHUNK_SHIFT)
    one = lambda m: jnp.where(m, 1.0, 0.0).astype(BF16)
    return one(same & (col > row)), one(same), one(same & (col < row))


def _mask_dot(mask, x):
    hi = x.astype(BF16)
    r1 = x - hi.astype(F32)
    mid = r1.astype(BF16)
    lo = (r1 - mid.astype(F32)).astype(BF16)
    return _dot(mask, hi) + _dot(mask, mid) + _dot(mask, lo)


def _log_sigmoid(x):
    return jnp.minimum(x, 0.0) - jnp.log1p(jnp.exp(-jnp.abs(x)))


def _conv_taps(prev8, uc, w):
    ext = jnp.concatenate([prev8, uc], axis=0)
    s1 = pltpu.roll(ext, 1, 0)[SUBLANES:]
    s2 = pltpu.roll(ext, 2, 0)[SUBLANES:]
    return s2 * w[0:1] + s1 * w[1:2] + uc * w[2:3], s1, s2


def _z_specs(tile, idx):
    d_conv = 1024
    wide = lambda c: pl.BlockSpec((tile, d_conv), lambda i, c=c: (idx(i), c))
    half = lambda c: pl.BlockSpec((tile, d_conv // 2), lambda i, c=c: (idx(i), c))
    return [wide(0), wide(1), wide(2), half(6), half(7), wide(4), wide(5)]


def _mixer_fwd(z, alow, wgu, b_gate, convw, conv_g, gla_g):
    t = z.shape[0]
    tb, cpt = MIX_TILE, CHUNKS_PER_TILE
    d_conv = conv_g.shape[1]
    dv = gla_g.shape[1]
    dk = dv // 2
    d_k = GLA_HEADS * dk
    gw = d_conv // CONV_GROUPS
    scale = dk ** -0.5

    def body(cb_ref, cc_ref, ch_ref, q_ref, k_ref, v_ref, og_ref, al_ref, wgu_ref, bg_ref, cw_ref, cg_ref, gg_ref,
             y_ref, sall_ref, carry_ref, s_ref):
        @pl.when(pl.program_id(0) == 0)
        def _():
            carry_ref[...] = jnp.zeros_like(carry_ref)
            s_ref[...] = jnp.zeros_like(s_ref)

        uc = cc_ref[...] * ch_ref[...]
        conv, _, _ = _conv_taps(carry_ref[...], uc, cw_ref[...])
        carry_ref[...] = uc[tb - SUBLANES:]
        ypre = cb_ref[...] * conv
        cg = cg_ref[...]
        for g in range(CONV_GROUPS):
            sl = slice(g * gw, (g + 1) * gw)
            seg = ypre[:, sl]
            r = lax.rsqrt(jnp.mean(seg * seg, axis=-1, keepdims=True) + EPS)
            y_ref[:, sl] = (seg * r * cg[:, sl]).astype(BF16)

        later, same, _ = _chunk_masks(tb)
        pre = _dot(al_ref[...].astype(BF16), wgu_ref[...]) + bg_ref[...]
        la = _log_sigmoid(pre) * (1.0 / GATE_NORMALIZER)
        e_dec = _mask_dot(later, la)
        dec_all = jnp.exp(_mask_dot(same, la))
        kdec = (k_ref[...] * jnp.exp(e_dec)).astype(BF16)
        qs = (q_ref[...] * scale).astype(BF16)
        vb = v_ref[...].astype(BF16)
        gg = gg_ref[...]
        rows = [slice(c * CHUNK, (c + 1) * CHUNK) for c in range(cpt)]
        ks = [slice(h * dk, (h + 1) * dk) for h in range(GLA_HEADS)]
        vs = [slice(h * dv, (h + 1) * dv) for h in range(GLA_HEADS)]
        kvt = [[_dot(vb[rows[c], vs[h]], kdec[rows[c], ks[h]], _TN) for h in range(GLA_HEADS)] for c in range(cpt)]
        state = [s_ref[h] for h in range(GLA_HEADS)]
        states = []
        for c in range(cpt):
            state = [state[h] * dec_all[c * CHUNK:c * CHUNK + 1, ks[h]] + kvt[c][h] for h in range(GLA_HEADS)]
            states.append(state)
            for h in range(GLA_HEADS):
                sall_ref[c, h] = state[h]
        for h in range(GLA_HEADS):
            s_ref[h] = state[h]
        for h in range(GLA_HEADS):
            o = jnp.concatenate(
                [_dot(qs[rows[c], ks[h]], states[c][h].astype(BF16), _NT) for c in range(cpt)], axis=0)
            ro = lax.rsqrt(jnp.mean(o * o, axis=-1, keepdims=True) + EPS)
            ogs = og_ref[:, vs[h]]
            yg = o * ro * gg * (ogs * jax.nn.sigmoid(ogs))
            y_ref[:, d_conv + h * dv:d_conv + (h + 1) * dv] = yg.astype(BF16)

    full = lambda shape: pl.BlockSpec(shape, lambda i: (0,) * len(shape))
    return pl.pallas_call(
        body, name="mixer_fwd", grid=(t // tb,),
        in_specs=_z_specs(tb, lambda i: i) + [
            pl.BlockSpec((tb, LANES), lambda i: (i, 0)), full(wgu.shape), full(b_gate.shape), full(convw.shape),
            full(conv_g.shape), full(gla_g.shape)],
        out_specs=[pl.BlockSpec((tb, d_conv + GLA_HEADS * dv), lambda i: (i, 0)),
                   pl.BlockSpec((cpt, GLA_HEADS, dv, dk), lambda i: (i, 0, 0, 0))],
        out_shape=[jax.ShapeDtypeStruct((t, d_conv + GLA_HEADS * dv), BF16),
                   jax.ShapeDtypeStruct((t // CHUNK, GLA_HEADS, dv, dk), F32)],
        scratch_shapes=[pltpu.VMEM((SUBLANES, d_conv), F32), pltpu.VMEM((GLA_HEADS, dv, dk), F32)],
        compiler_params=_params(1),
    )(z, z, z, z, z, z, z, alow, wgu, b_gate, convw, conv_g, gla_g)


def _mixer_bwd(z, alow, dy, sall, wgu, b_gate, convw, conv_g, gla_g, behind=None):
    t = z.shape[0]
    tb, cpt = MIX_TILE, CHUNKS_PER_TILE
    nt = t // tb
    d_conv = conv_g.shape[1]
    dv = gla_g.shape[1]
    dk = dv // 2
    d_k = GLA_HEADS * dk
    gw = d_conv // CONV_GROUPS
    scale = dk ** -0.5
    rev = lambda i: nt - 1 - i
    dep_args, dep_specs = _behind(behind)

    def body(cb_ref, cc_ref, ch_ref, q_ref, k_ref, v_ref, og_ref, ccp_ref, chp_ref, al_ref, dy_ref, sall_ref, sprev_ref,
             wgu_ref, bg_ref, cw_ref, cg_ref, gg_ref, *rest):
        dz_ref, dzal_ref, dcw_ref, dcg_ref, dgg_ref, dbg_ref, dwgu_ref, dcarry_ref, gd_ref = rest[-9:]
        i = pl.program_id(0)

        @pl.when(i == 0)
        def _():
            dcarry_ref[...] = jnp.zeros_like(dcarry_ref)
            gd_ref[...] = jnp.zeros_like(gd_ref)
            dcw_ref[...] = jnp.zeros_like(dcw_ref)
            dcg_ref[...] = jnp.zeros_like(dcg_ref)
            dgg_ref[...] = jnp.zeros_like(dgg_ref)
            dbg_ref[...] = jnp.zeros_like(dbg_ref)
            dwgu_ref[...] = jnp.zeros_like(dwgu_ref)

        first = rev(i) == 0

        cb, cc, ch = cb_ref[...], cc_ref[...], ch_ref[...]
        w = cw_ref[...]
        uc = cc * ch
        prev8 = jnp.where(first, 0.0, ccp_ref[...] * chp_ref[...])
        conv, s1, s2 = _conv_taps(prev8, uc, w)
        ypre = cb * conv
        cg = cg_ref[...]
        dypre_parts = []
        for g in range(CONV_GROUPS):
            sl = slice(g * gw, (g + 1) * gw)
            seg = ypre[:, sl]
            r = lax.rsqrt(jnp.mean(seg * seg, axis=-1, keepdims=True) + EPS)
            yn = seg * r
            dyc = dy_ref[:, sl]
            dcg_ref[:, sl] += jnp.sum(dyc * yn, axis=0, keepdims=True)
            dyn = dyc * cg[:, sl]
            dypre_parts.append(r * (dyn - yn * jnp.mean(dyn * yn, axis=-1, keepdims=True)))
        dypre = jnp.concatenate(dypre_parts, axis=1)
        dconv = dypre * cb
        dz_ref[:, 0:d_conv] = (dypre * conv).astype(BF16)
        dcw_ref[0:1] += jnp.sum(dconv * s2, axis=0, keepdims=True)
        dcw_ref[1:2] += jnp.sum(dconv * s1, axis=0, keepdims=True)
        dcw_ref[2:3] += jnp.sum(dconv * uc, axis=0, keepdims=True)
        ext = jnp.concatenate([dconv, dcarry_ref[...]], axis=0)
        f1 = pltpu.roll(ext, tb + SUBLANES - 1, 0)[:tb]
        f2 = pltpu.roll(ext, tb + SUBLANES - 2, 0)[:tb]
        dcarry_ref[...] = dconv[:SUBLANES]
        duc = dconv * w[2:3] + f1 * w[1:2] + f2 * w[0:1]
        dz_ref[:, d_conv:2 * d_conv] = (duc * ch).astype(BF16)
        dz_ref[:, 2 * d_conv:3 * d_conv] = (duc * cc).astype(BF16)

        q_off = 3 * d_conv
        k_off = q_off + d_k
        v_off = k_off + d_k
        og_off = v_off + GLA_HEADS * dv
        later, same, earlier = _chunk_masks(tb)
        alb = al_ref[...].astype(BF16)
        pre = _dot(alb, wgu_ref[...]) + bg_ref[...]
        la = _log_sigmoid(pre) * (1.0 / GATE_NORMALIZER)
        exp_e = jnp.exp(_mask_dot(later, la))
        dec_all = jnp.exp(_mask_dot(same, la))
        kdec = k_ref[...] * exp_e
        kdec_b = kdec.astype(BF16)
        qs = (q_ref[...] * scale).astype(BF16)
        vb = v_ref[...].astype(BF16)
        gg = gg_ref[...]
        rows = [slice(c * CHUNK, (c + 1) * CHUNK) for c in range(cpt)]
        ks = [slice(h * dk, (h + 1) * dk) for h in range(GLA_HEADS)]
        vs = [slice(h * dv, (h + 1) * dv) for h in range(GLA_HEADS)]
        st_b = [[sall_ref[c, h].astype(BF16) for h in range(GLA_HEADS)] for c in range(cpt)]
        do_b = []
        dgg = jnp.zeros_like(gg)
        for h in range(GLA_HEADS):
            o = jnp.concatenate([_dot(qs[rows[c], ks[h]], st_b[c][h], _NT) for c in range(cpt)], axis=0)
            ro = lax.rsqrt(jnp.mean(o * o, axis=-1, keepdims=True) + EPS)
            on = o * ro
            ogs = og_ref[:, vs[h]]
            sg = jax.nn.sigmoid(ogs)
            gate = ogs * sg
            dyg = dy_ref[:, d_conv + h * dv:d_conv + (h + 1) * dv]
            dgg = dgg + jnp.sum(dyg * on * gate, axis=0, keepdims=True)
            dz_ref[:, og_off + h * dv:og_off + (h + 1) * dv] = (
                dyg * on * gg * (sg * (1.0 + ogs * (1.0 - sg)))).astype(BF16)
            don = dyg * gg * gate
            do_b.append((ro * (don - on * jnp.mean(don * on, axis=-1, keepdims=True))).astype(BF16))
        dgg_ref[...] += dgg
        for h in range(GLA_HEADS):
            dq = jnp.concatenate([_dot(do_b[h][rows[c]], st_b[c][h]) for c in range(cpt)], axis=0)
            dz_ref[:, q_off + h * dk:q_off + (h + 1) * dk] = (dq * scale).astype(BF16)
        own = [[_dot(do_b[h][rows[c]], qs[rows[c], ks[h]], _TN) for h in range(GLA_HEADS)] for c in range(cpt)]
        carried = [gd_ref[h] for h in range(GLA_HEADS)]
        gt_b = [None] * cpt
        ddd = [None] * cpt
        for c in reversed(range(cpt)):
            gt = [own[c][h] + carried[h] for h in range(GLA_HEADS)]
            dec = [dec_all[c * CHUNK:c * CHUNK + 1, ks[h]] for h in range(GLA_HEADS)]
            carried = [gt[h] * dec[h] for h in range(GLA_HEADS)]
            if c > 0:
                st_prev = [sall_ref[c - 1, h] for h in range(GLA_HEADS)]
            else:
                st_prev = [jnp.where(first, 0.0, sprev_ref[0, h]) for h in range(GLA_HEADS)]
            ddec = [jnp.sum(gt[h] * st_prev[h], axis=0, keepdims=True) * dec[h] for h in range(GLA_HEADS)]
            ddd[c] = jnp.broadcast_to(jnp.concatenate(ddec, axis=1), (CHUNK, d_k))
            gt_b[c] = [gt[h].astype(BF16) for h in range(GLA_HEADS)]
        for h in range(GLA_HEADS):
            gd_ref[h] = carried[h]
        dkdec_cols = []
        for h in range(GLA_HEADS):
            dvh = jnp.concatenate([_dot(kdec_b[rows[c], ks[h]], gt_b[c][h], _NT) for c in range(cpt)], axis=0)
            dz_ref[:, v_off + h * dv:v_off + (h + 1) * dv] = dvh.astype(BF16)
            dkdec_cols.append(jnp.concatenate([_dot(vb[rows[c], vs[h]], gt_b[c][h]) for c in range(cpt)], axis=0))
        dkdec = jnp.concatenate(dkdec_cols, axis=1)
        dz_ref[:, k_off:k_off + d_k] = (dkdec * exp_e).astype(BF16)
        dla = _mask_dot(earlier, dkdec * kdec) + jnp.concatenate(ddd, axis=0)
        dpre = dla * (1.0 / GATE_NORMALIZER) * jax.nn.sigmoid(-pre)
        dbg_ref[...] += jnp.sum(dpre, axis=0, keepdims=True)
        dpre_b = dpre.astype(BF16)
        dwgu_ref[...] += _dot(alb, dpre_b, _TN)
        dzal_ref[...] = _dot(dpre_b, wgu_ref[...], _NT).astype(BF16)

    full = lambda shape: pl.BlockSpec(shape, lambda i: (0,) * len(shape))
    prev_rows = lambda c: pl.BlockSpec(
        (SUBLANES, d_conv), lambda i, c=c: (jnp.maximum(rev(i) * (tb // SUBLANES) - 1, 0), c))
    n_z = 3 * d_conv + 2 * d_k + 2 * GLA_HEADS * dv
    return pl.pallas_call(
        body, name="mixer_bwd", grid=(nt,),
        in_specs=_z_specs(tb, rev) + [
            prev_rows(1), prev_rows(2),
            pl.BlockSpec((tb, LANES), lambda i: (rev(i), 0)),
            pl.BlockSpec((tb, d_conv + GLA_HEADS * dv), lambda i: (rev(i), 0)),
            pl.BlockSpec((cpt, GLA_HEADS, dv, dk), lambda i: (rev(i), 0, 0, 0)),
            pl.BlockSpec((1, GLA_HEADS, dv, dk), lambda i: (jnp.maximum(rev(i) * cpt - 1, 0), 0, 0, 0)),
            full(wgu.shape), full(b_gate.shape), full(convw.shape), full(conv_g.shape), full(gla_g.shape)]
        + dep_specs,
        out_specs=[pl.BlockSpec((tb, n_z), lambda i: (rev(i), 0)), pl.BlockSpec((tb, LANES), lambda i: (rev(i), 0)),
                   full(convw.shape), full(conv_g.shape), full(gla_g.shape), full(b_gate.shape), full(wgu.shape)],
        out_shape=[jax.ShapeDtypeStruct((t, n_z), BF16), jax.ShapeDtypeStruct((t, LANES), BF16),
                   jax.ShapeDtypeStruct(convw.shape, F32), jax.ShapeDtypeStruct(conv_g.shape, F32),
                   jax.ShapeDtypeStruct(gla_g.shape, F32), jax.ShapeDtypeStruct(b_gate.shape, F32),
                   jax.ShapeDtypeStruct(wgu.shape, F32)],
        scratch_shapes=[pltpu.VMEM((SUBLANES, d_conv), F32), pltpu.VMEM((GLA_HEADS, dv, dk), F32)],
        compiler_params=_params(1),
    )(z, z, z, z, z, z, z, z, z, alow, dy, sall, sall, wgu, b_gate, convw, conv_g, gla_g, *dep_args)


def _adamw_math(g, w, m, v):
    m = ADAM_B1 * m + (1.0 - ADAM_B1) * g
    v = ADAM_B2 * v + (1.0 - ADAM_B2) * (g * g)
    m_hat = m / (1.0 - ADAM_B1 ** ADAM_STEP)
    v_hat = v / (1.0 - ADAM_B2 ** ADAM_STEP)
    delta = -ADAM_LR * (m_hat / (jnp.sqrt(v_hat) + ADAM_EPS) + ADAM_WD * w)
    return delta, m, v


def _adamw(name, parts, w, m, v, tr):
    r, c = w.shape
    n_parts = parts.shape[0]

    def body(p_ref, w_ref, m_ref, v_ref, g_ref, d_ref, nm_ref, nv_ref):
        g = p_ref[0].astype(F32)
        for j in range(1, n_parts):
            g = g + p_ref[j].astype(F32)
        g_ref[...] = g
        d_ref[...], nm_ref[...], nv_ref[...] = _adamw_math(g, w_ref[...], m_ref[...], v_ref[...])

    blk = pl.BlockSpec((tr, c), lambda i: (i, 0))
    return pl.pallas_call(
        body, name=name, grid=(r // tr,),
        in_specs=[pl.BlockSpec((n_parts, tr, c), lambda i: (0, i, 0)), blk, blk, blk],
        out_specs=[blk] * 4, out_shape=[jax.ShapeDtypeStruct((r, c), F32)] * 4,
        compiler_params=_params(1),
    )(parts, w, m, v)


def _sum_partials(parts):
    n_parts, rows, lanes = parts.shape

    def body(p_ref, o_ref):
        g = p_ref[0]
        for j in range(1, n_parts):
            g = g + p_ref[j]
        o_ref[...] = g

    return pl.pallas_call(
        body, name="sum_small_partials", out_shape=jax.ShapeDtypeStruct((rows, lanes), F32),
        in_specs=[pl.BlockSpec(memory_space=pltpu.VMEM)], out_specs=pl.BlockSpec(memory_space=pltpu.VMEM),
    )(parts)


def _pack_rows(vectors, rows):
    flat = jnp.concatenate([a.reshape(-1).astype(F32) for a in vectors])
    return jnp.pad(flat, (0, rows * LANES - flat.shape[0])).reshape(rows, LANES)


def _unpack_rows(block, shapes):
    flat = block.reshape(-1)
    out, off = [], 0
    for s in shapes:
        n = 1
        for dim in s:
            n *= dim
        out.append(flat[off:off + n].reshape(s))
        off += n
    return out


def kernel(x, norm1_g, w_in, w_gate_up, b_gate, conv_w, conv_norm_g, gla_norm_g, w_out, norm2_g, w_ff1, w_ff2, norm_f_g, loss_target, m_norm1_g, m_w_in, m_w_gate_up, m_b_gate, m_conv_w, m_conv_norm_g, m_gla_norm_g, m_w_out, m_norm2_g, m_w_ff1, m_w_ff2, m_norm_f_g, v_norm1_g, v_w_in, v_w_gate_up, v_b_gate, v_conv_w, v_conv_norm_g, v_gla_norm_g, v_w_out, v_norm2_g, v_w_ff1, v_w_ff2, v_norm_f_g):
    me = _device_index()
    x2d, tgt = x[0], loss_target[0]
    t, d = x2d.shape
    d_in_shard = w_in.shape[2]
    d_in = N_DEV * d_in_shard
    n_main = d_in - GATE_RANK
    d_conv = conv_norm_g.shape[1]
    d_k = b_gate.shape[1]
    d_ff = N_DEV * w_ff1.shape[2]
    wmv = dict(
        norm1_g=(norm1_g, m_norm1_g, v_norm1_g), w_in=(w_in, m_w_in, v_w_in),
        w_gate_up=(w_gate_up, m_w_gate_up, v_w_gate_up), b_gate=(b_gate, m_b_gate, v_b_gate),
        conv_w=(conv_w, m_conv_w, v_conv_w), conv_norm_g=(conv_norm_g, m_conv_norm_g, v_conv_norm_g),
        gla_norm_g=(gla_norm_g, m_gla_norm_g, v_gla_norm_g), w_out=(w_out, m_w_out, v_w_out),
        norm2_g=(norm2_g, m_norm2_g, v_norm2_g), w_ff1=(w_ff1, m_w_ff1, v_w_ff1), w_ff2=(w_ff2, m_w_ff2, v_w_ff2),
        norm_f_g=(norm_f_g, m_norm_f_g, v_norm_f_g))

    small_rows = 16
    small_shard = _pack_rows([w_gate_up[0], conv_w[0]], small_rows)
    first_level = (SIBLING,) + SAME_CORE_PEERS
    win_shard = w_in[0].astype(BF16)
    in_send, in_recv, in_src, in_land, _ = _exchange_start(
        "all_gather_start_w_in", [win_shard], [_land_zone(win_shard)], scatter=False, masks=[first_level])
    shards = [small_shard, w_out[0].astype(BF16), w_ff1[0].astype(BF16), w_ff2[0].astype(BF16)]
    ag_send, ag_recv, ag_src, ag_land, _ = _exchange_start(
        "all_gather_start", shards, [_land_zone(s) for s in shards], scatter=False)

    def gathered(k, name, after):
        return _exchange_wait(name, ag_send[k], ag_recv[k], ag_src[k], ag_land[k], after, scatter=False)

    u = _rmsnorm(x2d, norm1_g)
    packed_small = [_pack_rows([wmv[nm][k] for nm in _SMALL_ORDER] + [jnp.full((1,), _LOSS_SLOT[k], F32)],
                               SMALL_LOCAL_ROWS) for k in range(3)]
    small_g = gathered(0, "all_gather_wait_small", u)
    win_level1 = _exchange_wait(
        "all_gather_wait_w_in", in_send[0], in_recv[0], in_src[0], in_land[0],
        [small_g, m_w_in[0], v_w_in[0]] + packed_small, scatter=False, masks=first_level)
    win_g = _forward_wait("all_gather_wait_w_in_forwarded", *_forward_start("all_gather_forward_w_in", win_level1))
    w_main, w_alow = _shards_to_columns(win_g, n_main)
    small_flat = small_g.reshape(N_DEV, -1)
    n_wgu = GATE_RANK * (d_k // N_DEV)
    wgu_full = small_flat[:, :n_wgu].reshape(N_DEV, GATE_RANK, d_k // N_DEV).transpose(1, 0, 2).reshape(GATE_RANK, d_k)
    conv_w_full = small_flat[:, n_wgu:n_wgu + (d_conv // N_DEV) * CONV_WIDTH].reshape(d_conv, CONV_WIDTH)
    wgu_pad = jnp.pad(wgu_full, ((0, LANES - GATE_RANK), (0, 0))).astype(BF16)
    convw_taps = jnp.pad(conv_w_full.T, ((0, SUBLANES - CONV_WIDTH), (0, 0)))

    get_w_out = lambda after: gathered(1, "all_gather_wait_w_out", after).reshape(-1, d)
    get_w1 = lambda after: gathered(2, "all_gather_wait_w_ff1", after)
    get_w2 = lambda after: gathered(3, "all_gather_wait_w_ff2", after).reshape(d_ff, d)

    in_flight = {}

    def send_partials(name, parts):
        own = lax.dynamic_index_in_dim(parts, me, axis=0, keepdims=False)
        send, recv, src, land, token = _exchange_start("scatter_start_" + name, [parts], [_land_zone(own)], scatter=True)
        in_flight[name] = (send[0], recv[0], src[0], land[0])
        return token

    def on_grad(name, value):
        if name == "w_in":
            main, alow_part = value
            value = _columns_to_shards(main, alow_part, N_DEV, d_in_shard)
        elif name in ("w_out", "w_ff2"):
            value = value.reshape(N_DEV, -1, d)
        return send_partials(name, value)

    grads = _local_step(x2d, u, tgt, norm1_g, w_main, w_alow, wgu_pad, b_gate, convw_taps, conv_norm_g, gla_norm_g,
                        norm2_g, norm_f_g, get_w_out, get_w1, get_w2, on_grad)
    grad_x = grads["x"]

    small_shapes = [(1, d), (1, d_k), (1, d_conv), (1, gla_norm_g.shape[1]), (1, d), (d,),
                    (GATE_RANK, d_k), (d_conv, CONV_WIDTH), (1,)]
    small_grad_rows = 152
    small_part = _pack_rows(
        [grads["norm1_g"], grads["b_gate"], grads["conv_norm_g"], grads["gla_norm_g"], grads["norm2_g"],
         grads["norm_f_g"], grads["w_gate_up"][:GATE_RANK], grads["conv_w"][:CONV_WIDTH].T, grads["loss"][0, 0]],
        small_grad_rows)
    send_partials("small", jnp.broadcast_to(small_part[None], (N_DEV, small_grad_rows, LANES)))

    def received(name):
        send, recv, src, land = in_flight[name]
        return _exchange_wait("scatter_wait_" + name, send, recv, src, land, grad_x, scatter=True)

    small_r = received("small")
    gin_r, gout_r, g1_r, g2_r = received("w_in"), received("w_out"), received("w_ff1"), received("w_ff2")
    return _update(me, gin_r, gout_r, g1_r, g2_r, small_r, small_shapes, grad_x, wmv, packed_small)


def _local_step(x2d, u, tgt, norm1_g, w_main, w_alow, wgu_pad, b_gate, convw_taps, conv_norm_g, gla_norm_g,
                norm2_g, norm_f_g, get_w_out, get_w1, get_w2, on_grad):
    t, d = x2d.shape
    n_main = w_main.shape[1]

    z, alow = _inproj(u, w_main, w_alow)
    y, sall = _mixer_fwd(z, alow, wgu_pad, b_gate, convw_taps, conv_norm_g, gla_norm_g)
    w_out_full = get_w_out(y)
    x1, h = _outproj(y, w_out_full, x2d, norm2_g)
    w1g = get_w1(h)
    a = _ff1(h, w1g)
    w2_full = get_w2(a)
    d_ff = w2_full.shape[0]
    x2 = _ff2(a, w2_full, x1)
    dx2, dx2b, loss_part, d_normf = _loss_head(x2, norm_f_g.reshape(1, d), tgt)

    tk = min(4096, t)
    nk = t // tk
    da = _dff2(dx2b, w2_full, a)
    dw2 = _tn_matmul(
        "dw_ff2", a, dx2b, (d_ff // 1024, d // 1024, nk),
        pl.BlockSpec((tk, 1024), lambda m, j, kk: (kk, m)), pl.BlockSpec((tk, 1024), lambda m, j, kk: (kk, j)),
        jax.ShapeDtypeStruct((d_ff, d), BF16), pl.BlockSpec((1024, 1024), lambda m, j, kk: (m, j)), (1024, 1024),
        a_fn=_relu_sq)
    token = on_grad("w_ff2", dw2)
    f_shard = d_ff // N_DEV
    dw1 = _tn_matmul(
        "dw_ff1", h, da, (N_DEV, d // 1024, nk),
        pl.BlockSpec((tk, 1024), lambda g, m, kk: (kk, m)), pl.BlockSpec((tk, f_shard), lambda g, m, kk: (kk, g)),
        jax.ShapeDtypeStruct((N_DEV, d, f_shard), BF16), pl.BlockSpec((None, 1024, f_shard), lambda g, m, kk: (g, m, 0)),
        (1024, f_shard), behind=token)
    token = on_grad("w_ff1", dw1)
    dh = _dh(da, w1g, behind=token)
    dx1, dx1b, d_norm2 = _norm_bwd("norm2_bwd", dh, x1, norm2_g, dx2, with_bf16=True)
    dy = _nt_matmul("dy", dx1b, w_out_full)
    dwout = _tn_matmul(
        "dw_out", y, dx1b, (d // 1024, d // 1024, nk),
        pl.BlockSpec((tk, 1024), lambda m, j, kk: (kk, m)), pl.BlockSpec((tk, 1024), lambda m, j, kk: (kk, j)),
        jax.ShapeDtypeStruct((d, d), BF16), pl.BlockSpec((1024, 1024), lambda m, j, kk: (m, j)), (1024, 1024))
    token = on_grad("w_out", dwout)
    dz, dzal, d_convw, d_convg, d_glag, d_bgate, d_wgu = _mixer_bwd(
        z, alow, dy, sall, wgu_pad, b_gate, convw_taps, conv_norm_g, gla_norm_g, behind=token)
    dwin_main = _tn_matmul(
        "dw_in", u, dz, (d // 1024, n_main // 1024, nk),
        pl.BlockSpec((tk, 1024), lambda m, j, kk: (kk, m)), pl.BlockSpec((tk, 1024), lambda m, j, kk: (kk, j)),
        jax.ShapeDtypeStruct((d, n_main), BF16), pl.BlockSpec((1024, 1024), lambda m, j, kk: (m, j)), (1024, 1024))
    dwin_alow = _tn_matmul(
        "dw_in_alow", u, dzal, (d // 1024, 1, nk),
        pl.BlockSpec((tk, 1024), lambda m, j, kk: (kk, m)), pl.BlockSpec((tk, LANES), lambda m, j, kk: (kk, 0)),
        jax.ShapeDtypeStruct((d, LANES), BF16), pl.BlockSpec((1024, LANES), lambda m, j, kk: (m, 0)), (1024, LANES))
    token = on_grad("w_in", (dwin_main, dwin_alow))
    du = _du(dz, w_main, dzal, w_alow, behind=token)
    grad_x, d_norm1 = _norm_bwd("norm1_bwd", du, x2d, norm1_g, dx1, with_bf16=False)
    return dict(x=grad_x, loss=loss_part, norm1_g=d_norm1, w_gate_up=d_wgu, b_gate=d_bgate, conv_w=d_convw,
                conv_norm_g=d_convg, gla_norm_g=d_glag, norm2_g=d_norm2, norm_f_g=d_normf)


_WEIGHT_ORDER = ("norm1_g", "w_in", "w_gate_up", "b_gate", "conv_w", "conv_norm_g", "gla_norm_g", "w_out", "norm2_g",
                 "w_ff1", "w_ff2", "norm_f_g")
_SMALL_ORDER = ("norm1_g", "b_gate", "conv_norm_g", "gla_norm_g", "norm2_g", "norm_f_g", "w_gate_up", "conv_w")
SMALL_LOCAL_ROWS = 80
_LOSS_SLOT = (0.0, 0.0, 1.0)


def _update(me, gin_r, gout_r, g1_r, g2_r, small_r, small_shapes, grad_x, wmv, packed_small):
    big = {
        "w_in": _adamw("adamw_w_in", gin_r, *(a[0] for a in wmv["w_in"]), 256),
        "w_out": _adamw("adamw_w_out", gout_r, *(a[0] for a in wmv["w_out"]), 128),
        "w_ff1": _adamw("adamw_w_ff1", g1_r, *(a[0] for a in wmv["w_ff1"]), 256),
        "w_ff2": _adamw("adamw_w_ff2", g2_r, *(a[0] for a in wmv["w_ff2"]), 128),
    }

    wgu_cols = wmv["w_gate_up"][0].shape[2]
    cw_rows = wmv["conv_w"][0].shape[1]

    summed = _unpack_rows(_sum_partials(small_r), small_shapes)
    summed[6] = lax.dynamic_slice_in_dim(summed[6], me * wgu_cols, wgu_cols, axis=1)
    summed[7] = lax.dynamic_slice_in_dim(summed[7], me * cw_rows, cw_rows, axis=0)
    local_shapes = small_shapes[:6] + [(GATE_RANK, wgu_cols), (cw_rows, CONV_WIDTH), (1,)]
    out_small = _adamw("adamw_small", _pack_rows(summed, SMALL_LOCAL_ROWS)[None], *packed_small, SMALL_LOCAL_ROWS)
    unpacked = [_unpack_rows(o, local_shapes) for o in out_small]

    outs = []
    for k in range(4):
        for nm in _WEIGHT_ORDER:
            if nm in big:
                outs.append(big[nm][k][None])
            else:
                val = unpacked[k][_SMALL_ORDER.index(nm)]
                outs.append(val.reshape(wmv[nm][0].shape))
    loss = unpacked[0][8][0]
    return (loss, grad_x[None], *outs)
```

```python
import functools

import jax
import jax.numpy as jnp
from jax import lax
from jax.experimental import pallas as pl
from jax.experimental.pallas import tpu as pltpu

F32 = jnp.float32
BF16 = jnp.bfloat16

N_DEV = 8
CHUNK = 64
GLA_HEADS = 4
CONV_GROUPS = 8
CONV_WIDTH = 3
GATE_RANK = 16
GATE_NORMALIZER = 16.0
EPS = 1e-6
ADAM_LR = 0.001
ADAM_B1 = 0.9
ADAM_B2 = 0.999
ADAM_EPS = 1e-08
ADAM_WD = 0.01
ADAM_STEP = 10

LANES = 128
SUBLANES = 8
VMEM_LIMIT = 56 << 20

_NN = (((1,), (0,)), ((), ()))
_NT = (((1,), (1,)), ((), ()))
_TN = (((0,), (0,)), ((), ()))


def _dot(a, b, dims=_NN):
    return lax.dot_general(a, b, dims, preferred_element_type=F32)


def _params(n_grid):
    return pltpu.CompilerParams(dimension_semantics=("arbitrary",) * n_grid, vmem_limit_bytes=VMEM_LIMIT)


def _relu_sq(a):
    r = jnp.maximum(a, 0.0)
    return r * r


def _device_index():
    return 4 * lax.axis_index("x") + 2 * lax.axis_index("y") + lax.axis_index("c")


def _peer(mask):
    x, y, c = lax.axis_index("x"), lax.axis_index("y"), lax.axis_index("c")
    return (x ^ ((mask >> 2) & 1), y ^ ((mask >> 1) & 1), c ^ (mask & 1))


_HBM_SPEC = pl.BlockSpec(memory_space=pltpu.HBM)
_SEM_SPEC = pl.BlockSpec(memory_space=pltpu.SEMAPHORE)
_SIDE_EFFECT = pltpu.SideEffectType.DATAFLOW_SIDE_EFFECTING
N_PEERS = N_DEV - 1


def _exchange_copy(src_ref, land_ref, send_sems, recv_sems, mask, scatter, arriving):
    me = _device_index()
    src = src_ref.at[me ^ mask] if scatter else src_ref
    dst = land_ref.at[(me ^ mask) if arriving else me]
    return pltpu.make_async_remote_copy(
        src_ref=src, dst_ref=dst, send_sem=send_sems.at[mask - 1], recv_sem=recv_sems.at[mask - 1],
        device_id=_peer(mask), device_id_type=pl.DeviceIdType.MESH)


def _land_zone(own):
    zone = lax.empty((N_DEV,) + own.shape, own.dtype)
    return lax.dynamic_update_slice(zone, own[None], (_device_index(),) + (0,) * own.ndim)


ALL_PEERS = tuple(range(1, N_DEV))
SIBLING = 1
SAME_CORE_PEERS = (2, 4, 6)


def _exchange_start(name, srcs, lands, scatter, masks=None, behind=None):
    n = len(srcs)
    masks = masks or [ALL_PEERS] * n
    dep_args = [] if behind is None else [behind]

    def body(*refs):
        src, land = refs[:n], refs[n:2 * n]
        outs = refs[2 * n + len(dep_args):]
        send_sems, recv_sems = outs[:n], outs[n:2 * n]
        token = refs[-1]
        for a in range(n):
            for mask in masks[a]:
                _exchange_copy(src[a], land[a], send_sems[a], recv_sems[a], mask, scatter, False).start()
        token[...] = jnp.zeros_like(token)

    hbm = lambda a: pltpu.HBM(a.shape, a.dtype)
    outs = pl.pallas_call(
        body, name=name,
        out_shape=([pltpu.SemaphoreType.DMA((N_PEERS,))] * (2 * n) + [hbm(a) for a in srcs] + [hbm(a) for a in lands]
                   + [jax.ShapeDtypeStruct((SUBLANES, LANES), F32)]),
        in_specs=[_HBM_SPEC] * (2 * n) + [pl.BlockSpec(memory_space=pl.ANY)] * len(dep_args),
        out_specs=[_SEM_SPEC] * (2 * n) + [_HBM_SPEC] * (2 * n) + [pl.BlockSpec(memory_space=pltpu.VMEM)],
        input_output_aliases={a: 2 * n + a for a in range(2 * n)},
        compiler_params=pltpu.CompilerParams(has_side_effects=_SIDE_EFFECT),
    )(*[pltpu.with_memory_space_constraint(a, pltpu.HBM) for a in list(srcs) + list(lands)], *dep_args)
    send_sems, recv_sems = outs[:n], outs[n:2 * n]
    src_thru, land_thru = outs[2 * n:3 * n], outs[3 * n:4 * n]
    return send_sems, recv_sems, src_thru, land_thru, outs[-1]


def _exchange_wait(name, send_sems, recv_sems, src_thru, land_thru, after, scatter, masks=ALL_PEERS):
    after = list(after) if isinstance(after, (list, tuple)) else [after]

    def body(src_ref, land_ref, send_ref, recv_ref, *rest):
        for mask in masks:
            cp = _exchange_copy(src_ref, land_ref, send_ref, recv_ref, mask, scatter, True)
            cp.wait_send()
            cp.wait_recv()

    return pl.pallas_call(
        body, name=name,
        out_shape=(pltpu.HBM(src_thru.shape, src_thru.dtype), pltpu.HBM(land_thru.shape, land_thru.dtype)),
        in_specs=[_HBM_SPEC, _HBM_SPEC, _SEM_SPEC, _SEM_SPEC] + [pl.BlockSpec(memory_space=pl.ANY)] * len(after),
        out_specs=(_HBM_SPEC, _HBM_SPEC), input_output_aliases={0: 0, 1: 1},
        compiler_params=pltpu.CompilerParams(has_side_effects=_SIDE_EFFECT),
    )(src_thru, land_thru, send_sems, recv_sems, *after)[1]


def _forward_copy(land_ref, send_sems, recv_sems, k, arriving):
    me = _device_index()
    slot = me ^ SAME_CORE_PEERS[k]
    return pltpu.make_async_remote_copy(
        src_ref=land_ref.at[slot], dst_ref=land_ref.at[(slot ^ SIBLING) if arriving else slot],
        send_sem=send_sems.at[k], recv_sem=recv_sems.at[k],
        device_id=_peer(SIBLING), device_id_type=pl.DeviceIdType.MESH)


def _forward_start(name, land):
    n_fwd = len(SAME_CORE_PEERS)

    def body(land_ref, send_sems, recv_sems, land_thru):
        for k in range(n_fwd):
            _forward_copy(land_ref, send_sems, recv_sems, k, False).start()

    send, recv, thru = pl.pallas_call(
        body, name=name,
        out_shape=[pltpu.SemaphoreType.DMA((n_fwd,)), pltpu.SemaphoreType.DMA((n_fwd,)), pltpu.HBM(land.shape, land.dtype)],
        in_specs=[_HBM_SPEC], out_specs=[_SEM_SPEC, _SEM_SPEC, _HBM_SPEC], input_output_aliases={0: 2},
        compiler_params=pltpu.CompilerParams(has_side_effects=_SIDE_EFFECT),
    )(pltpu.with_memory_space_constraint(land, pltpu.HBM))
    return send, recv, thru


def _forward_wait(name, send_sems, recv_sems, land_thru):
    def body(land_ref, send_ref, recv_ref, got_ref):
        for k in range(len(SAME_CORE_PEERS)):
            cp = _forward_copy(land_ref, send_ref, recv_ref, k, True)
            cp.wait_send()
            cp.wait_recv()

    return pl.pallas_call(
        body, name=name, out_shape=pltpu.HBM(land_thru.shape, land_thru.dtype),
        in_specs=[_HBM_SPEC, _SEM_SPEC, _SEM_SPEC], out_specs=_HBM_SPEC, input_output_aliases={0: 0},
        compiler_params=pltpu.CompilerParams(has_side_effects=_SIDE_EFFECT),
    )(land_thru, send_sems, recv_sems)


def _shards_to_columns(g, n_main, tr=256):
    n_dev, d, s = g.shape

    def body(g_ref, main_ref, rest_ref):
        for j in range(n_dev):
            lo, hi = j * s, (j + 1) * s
            if hi <= n_main:
                main_ref[:, lo:hi] = g_ref[j]
            else:
                main_ref[:, lo:n_main] = g_ref[j, :, 0:n_main - lo]
                rest_ref[...] = jnp.zeros_like(rest_ref)
                rest_ref[:, 0:hi - n_main] = g_ref[j, :, n_main - lo:s]

    return pl.pallas_call(
        body, grid=(d // tr,), name="shards_to_columns",
        in_specs=[pl.BlockSpec((n_dev, tr, s), lambda i: (0, i, 0))],
        out_specs=[pl.BlockSpec((tr, n_main), lambda i: (i, 0)), pl.BlockSpec((tr, LANES), lambda i: (i, 0))],
        out_shape=[jax.ShapeDtypeStruct((d, n_main), g.dtype), jax.ShapeDtypeStruct((d, LANES), g.dtype)],
        compiler_params=_params(1),
    )(g)


def _columns_to_shards(main, rest, n_dev, s, tr=256):
    d, n_main = main.shape
    assert (n_dev - 1) * s <= n_main < n_dev * s

    def body(main_ref, rest_ref, o_ref):
        for j in range(n_dev):
            lo, hi = j * s, (j + 1) * s
            if hi <= n_main:
                o_ref[j] = main_ref[:, lo:hi]
            else:
                o_ref[j, :, 0:n_main - lo] = main_ref[:, lo:n_main]
                o_ref[j, :, n_main - lo:s] = rest_ref[:, 0:hi - n_main]

    return pl.pallas_call(
        body, grid=(d // tr,), name="columns_to_shards",
        in_specs=[pl.BlockSpec((tr, n_main), lambda i: (i, 0)), pl.BlockSpec((tr, LANES), lambda i: (i, 0))],
        out_specs=pl.BlockSpec((n_dev, tr, s), lambda i: (0, i, 0)),
        out_shape=jax.ShapeDtypeStruct((n_dev, d, s), main.dtype),
        compiler_params=_params(1),
    )(main, rest)


def _rmsnorm(x, g, tr=512, behind=None):
    t, d = x.shape
    tr = min(tr, t)
    dep_args, dep_specs = _behind(behind)

    def body(x_ref, g_ref, *rest):
        u_ref = rest[-1]
        xf = x_ref[...]
        r = lax.rsqrt(jnp.mean(xf * xf, axis=-1, keepdims=True) + EPS)
        u_ref[...] = (xf * r * g_ref[...]).astype(BF16)

    return pl.pallas_call(
        body, name="rmsnorm1", grid=(t // tr,),
        in_specs=[pl.BlockSpec((tr, d), lambda i: (i, 0)), pl.BlockSpec((1, d), lambda i: (0, 0))] + dep_specs,
        out_specs=pl.BlockSpec((tr, d), lambda i: (i, 0)),
        out_shape=jax.ShapeDtypeStruct((t, d), BF16),
        compiler_params=_params(1),
    )(x, g, *dep_args)


def _inproj(u, w_main, w_alow, tm=1024, tn=1024):
    t, d = u.shape
    tm = min(tm, t)
    n = w_main.shape[1]

    def body(u_ref, w_ref, wa_ref, z_ref, al_ref):
        @pl.when(pl.program_id(1) == 0)
        def _():
            al_ref[...] = _dot(u_ref[...], wa_ref[...])

        z_ref[...] = _dot(u_ref[...], w_ref[...])

    return pl.pallas_call(
        body, name="inproj", grid=(t // tm, n // tn),
        in_specs=[pl.BlockSpec((tm, d), lambda m, j: (m, 0)), pl.BlockSpec((d, tn), lambda m, j: (0, j)),
                  pl.BlockSpec((d, LANES), lambda m, j: (0, 0))],
        out_specs=[pl.BlockSpec((tm, tn), lambda m, j: (m, j)), pl.BlockSpec((tm, LANES), lambda m, j: (m, 0))],
        out_shape=[jax.ShapeDtypeStruct((t, n), F32), jax.ShapeDtypeStruct((t, LANES), F32)],
        compiler_params=_params(2),
    )(u, w_main, w_alow)


def _outproj(y, w_out, x, g2, tm=512):
    t, d = x.shape
    tm = min(tm, t)
    k = y.shape[1]

    def body(y_ref, w_ref, x_ref, g_ref, x1_ref, h_ref):
        x1 = x_ref[...] + _dot(y_ref[...], w_ref[...])
        x1_ref[...] = x1
        r = lax.rsqrt(jnp.mean(x1 * x1, axis=-1, keepdims=True) + EPS)
        h_ref[...] = (x1 * r * g_ref[...]).astype(BF16)

    return pl.pallas_call(
        body, name="outproj_rmsnorm", grid=(t // tm,),
        in_specs=[pl.BlockSpec((tm, k), lambda m: (m, 0)), pl.BlockSpec((k, d), lambda m: (0, 0)),
                  pl.BlockSpec((tm, d), lambda m: (m, 0)), pl.BlockSpec((1, d), lambda m: (0, 0))],
        out_specs=[pl.BlockSpec((tm, d), lambda m: (m, 0)), pl.BlockSpec((tm, d), lambda m: (m, 0))],
        out_shape=[jax.ShapeDtypeStruct((t, d), F32), jax.ShapeDtypeStruct((t, d), BF16)],
        compiler_params=_params(1),
    )(y, w_out, x, g2)


def _ff1(h, w1g, tm=1024):
    t, d = h.shape
    tm = min(tm, t)
    g, _, f = w1g.shape

    def body(h_ref, w_ref, a_ref):
        a_ref[...] = _dot(h_ref[...], w_ref[...]).astype(BF16)

    return pl.pallas_call(
        body, name="ff1", grid=(t // tm, g),
        in_specs=[pl.BlockSpec((tm, d), lambda m, j: (m, 0)), pl.BlockSpec((None, d, f), lambda m, j: (j, 0, 0))],
        out_specs=pl.BlockSpec((tm, f), lambda m, j: (m, j)),
        out_shape=jax.ShapeDtypeStruct((t, g * f), BF16),
        compiler_params=_params(2),
    )(h, w1g)


def _ff2(a, w2, x1, tm=1024, tn=1024, tk=2048):
    t, f = a.shape
    tm = min(tm, t)
    d = w2.shape[1]

    def body(a_ref, w_ref, x1_ref, o_ref):
        @pl.when(pl.program_id(2) == 0)
        def _():
            o_ref[...] = x1_ref[...]

        o_ref[...] += _dot(_relu_sq(a_ref[...]), w_ref[...])

    return pl.pallas_call(
        body, name="ff2_residual", grid=(t // tm, d // tn, f // tk),
        in_specs=[pl.BlockSpec((tm, tk), lambda m, j, kk: (m, kk)), pl.BlockSpec((tk, tn), lambda m, j, kk: (kk, j)),
                  pl.BlockSpec((tm, tn), lambda m, j, kk: (m, j))],
        out_specs=pl.BlockSpec((tm, tn), lambda m, j, kk: (m, j)),
        out_shape=jax.ShapeDtypeStruct((t, d), F32),
        compiler_params=_params(3),
    )(a, w2, x1)


def _dff2(dx2b, w2, a, tm=1024, tn=1024):
    t, d = dx2b.shape
    tm = min(tm, t)
    f = w2.shape[0]

    def body(g_ref, w_ref, a_ref, o_ref):
        dp = _dot(g_ref[...], w_ref[...], _NT)
        o_ref[...] = (dp * (2.0 * jnp.maximum(a_ref[...].astype(F32), 0.0))).astype(BF16)

    return pl.pallas_call(
        body, name="dff2", grid=(t // tm, f // tn),
        in_specs=[pl.BlockSpec((tm, d), lambda m, j: (m, 0)), pl.BlockSpec((tn, d), lambda m, j: (j, 0)),
                  pl.BlockSpec((tm, tn), lambda m, j: (m, j))],
        out_specs=pl.BlockSpec((tm, tn), lambda m, j: (m, j)),
        out_shape=jax.ShapeDtypeStruct((t, f), BF16),
        compiler_params=_params(2),
    )(dx2b, w2, a)


def _behind(token):
    if token is None:
        return [], []
    return [token], [pl.BlockSpec(token.shape, lambda *_: (0,) * token.ndim)]


def _tn_matmul(name, a, b, grid, a_spec, b_spec, out_shape, out_spec, acc_shape, a_fn=None, behind=None):
    nk = grid[-1]
    dep_args, dep_specs = _behind(behind)

    def body(a_ref, b_ref, *rest):
        o_ref, acc_ref = rest[-2:]
        kk = pl.program_id(len(grid) - 1)
        av = a_ref[...]
        if a_fn is not None:
            av = a_fn(av)
        part = _dot(av, b_ref[...], _TN)

        @pl.when(kk == 0)
        def _():
            acc_ref[...] = part

        @pl.when(kk > 0)
        def _():
            acc_ref[...] += part

        @pl.when(kk == nk - 1)
        def _():
            o_ref[...] = acc_ref[...].astype(o_ref.dtype)

    return pl.pallas_call(
        body, name=name, grid=grid, in_specs=[a_spec, b_spec] + dep_specs, out_specs=out_spec, out_shape=out_shape,
        scratch_shapes=[pltpu.VMEM(acc_shape, F32)], compiler_params=_params(len(grid)),
    )(a, b, *dep_args)


def _dh(da, w1g, tm=512, tn=512, behind=None):
    t = da.shape[0]
    tm = min(tm, t)
    g, d, f = w1g.shape
    dep_args, dep_specs = _behind(behind)

    def body(a_ref, w_ref, *rest):
        o_ref = rest[-1]
        acc = _dot(a_ref[:, 0:f], w_ref[0], _NT)
        for s in range(1, g):
            acc = acc + _dot(a_ref[:, s * f:(s + 1) * f], w_ref[s], _NT)
        o_ref[...] = acc

    return pl.pallas_call(
        body, name="dh", grid=(t // tm, d // tn),
        in_specs=[pl.BlockSpec((tm, g * f), lambda m, j: (m, 0)),
                  pl.BlockSpec((g, tn, f), lambda m, j: (0, j, 0))] + dep_specs,
        out_specs=pl.BlockSpec((tm, tn), lambda m, j: (m, j)),
        out_shape=jax.ShapeDtypeStruct((t, d), F32),
        compiler_params=_params(2),
    )(da, w1g, *dep_args)


def _nt_matmul(name, a, b, tm=1024, tn=1024):
    t, k = a.shape
    tm = min(tm, t)
    n = b.shape[0]

    def body(a_ref, b_ref, o_ref):
        o_ref[...] = _dot(a_ref[...], b_ref[...], _NT)

    return pl.pallas_call(
        body, name=name, grid=(t // tm, n // tn),
        in_specs=[pl.BlockSpec((tm, k), lambda m, j: (m, 0)), pl.BlockSpec((tn, k), lambda m, j: (j, 0))],
        out_specs=pl.BlockSpec((tm, tn), lambda m, j: (m, j)),
        out_shape=jax.ShapeDtypeStruct((t, n), F32),
        compiler_params=_params(2),
    )(a, b)


def _du(dz, w_main, dzal, w_alow, tm=1024, tn=1024, tk=3072, behind=None):
    t, n = dz.shape
    tm = min(tm, t)
    d = w_main.shape[0]
    dep_args, dep_specs = _behind(behind)

    def body(a_ref, w_ref, al_ref, wa_ref, *rest):
        o_ref = rest[-1]

        @pl.when(pl.program_id(2) == 0)
        def _():
            o_ref[...] = _dot(al_ref[...], wa_ref[...], _NT)

        o_ref[...] += _dot(a_ref[...], w_ref[...], _NT)

    return pl.pallas_call(
        body, name="du", grid=(t // tm, d // tn, n // tk),
        in_specs=[pl.BlockSpec((tm, tk), lambda m, j, kk: (m, kk)), pl.BlockSpec((tn, tk), lambda m, j, kk: (j, kk)),
                  pl.BlockSpec((tm, LANES), lambda m, j, kk: (m, 0)), pl.BlockSpec((tn, LANES), lambda m, j, kk: (j, 0))]
        + dep_specs,
        out_specs=pl.BlockSpec((tm, tn), lambda m, j, kk: (m, j)),
        out_shape=jax.ShapeDtypeStruct((t, d), F32),
        compiler_params=_params(3),
    )(dz, w_main, dzal, w_alow, *dep_args)


def _loss_head(x2, gf, tgt, tr=256):
    t, d = x2.shape

    def body(x_ref, g_ref, t_ref, dx_ref, dxb_ref, loss_ref, dg_ref):
        @pl.when(pl.program_id(0) == 0)
        def _():
            loss_ref[...] = jnp.zeros_like(loss_ref)
            dg_ref[...] = jnp.zeros_like(dg_ref)

        xf = x_ref[...]
        g = g_ref[...]
        r = lax.rsqrt(jnp.mean(xf * xf, axis=-1, keepdims=True) + EPS)
        xh = xf * r
        e = xh * g - t_ref[...]
        loss_ref[...] += 0.5 * jnp.sum(jnp.mean(e * e, axis=-1, keepdims=True))
        dy = e * (1.0 / d)
        dg_ref[...] += jnp.sum(dy * xh, axis=0, keepdims=True)
        dyg = dy * g
        dx = r * (dyg - xh * jnp.mean(dyg * xh, axis=-1, keepdims=True))
        dx_ref[...] = dx
        dxb_ref[...] = dx.astype(BF16)

    return pl.pallas_call(
        body, name="loss_head", grid=(t // tr,),
        in_specs=[pl.BlockSpec((tr, d), lambda i: (i, 0)), pl.BlockSpec((1, d), lambda i: (0, 0)),
                  pl.BlockSpec((tr, d), lambda i: (i, 0))],
        out_specs=[pl.BlockSpec((tr, d), lambda i: (i, 0)), pl.BlockSpec((tr, d), lambda i: (i, 0)),
                   pl.BlockSpec((SUBLANES, LANES), lambda i: (0, 0)), pl.BlockSpec((1, d), lambda i: (0, 0))],
        out_shape=[jax.ShapeDtypeStruct((t, d), F32), jax.ShapeDtypeStruct((t, d), BF16),
                   jax.ShapeDtypeStruct((SUBLANES, LANES), F32), jax.ShapeDtypeStruct((1, d), F32)],
        compiler_params=_params(1),
    )(x2, gf, tgt)


def _norm_bwd(name, dh, xin, g, dres, with_bf16, tr=256):
    t, d = xin.shape

    def body(dh_ref, x_ref, g_ref, dr_ref, dx_ref, *rest):
        dg_ref = rest[-1]

        @pl.when(pl.program_id(0) == 0)
        def _():
            dg_ref[...] = jnp.zeros_like(dg_ref)

        xf = x_ref[...]
        dhv = dh_ref[...]
        r = lax.rsqrt(jnp.mean(xf * xf, axis=-1, keepdims=True) + EPS)
        xh = xf * r
        dg_ref[...] += jnp.sum(dhv * xh, axis=0, keepdims=True)
        dyg = dhv * g_ref[...]
        dx = dr_ref[...] + r * (dyg - xh * jnp.mean(dyg * xh, axis=-1, keepdims=True))
        dx_ref[...] = dx
        if with_bf16:
            rest[0][...] = dx.astype(BF16)

    rows = pl.BlockSpec((tr, d), lambda i: (i, 0))
    vec = pl.BlockSpec((1, d), lambda i: (0, 0))
    return pl.pallas_call(
        body, name=name, grid=(t // tr,),
        in_specs=[rows, rows, vec, rows],
        out_specs=[rows] + [rows] * with_bf16 + [vec],
        out_shape=[jax.ShapeDtypeStruct((t, d), F32)] + [jax.ShapeDtypeStruct((t, d), BF16)] * with_bf16
        + [jax.ShapeDtypeStruct((1, d), F32)],
        compiler_params=_params(1),
    )(dh, xin, g, dres)


MIX_TILE = 256
CHUNKS_PER_TILE = MIX_TILE // CHUNK
CHUNK_SHIFT = CHUNK.bit_length() - 1
assert 1 << CHUNK_SHIFT == CHUNK


def _chunk_masks(n):
    row = lax.broadcasted_iota(jnp.int32, (n, n), 0)
    col = lax.broadcasted_iota(jnp.int32, (n, n), 1)
    same = lax.shift_right_logical(row, CHUNK_SHIFT) == lax.shift_right_logical(col, CHUNK_SHIFT)
    one = lambda m: jnp.where(m, 1.0, 0.0).astype(BF16)
    return one(same & (col > row)), one(same), one(same & (col < row))


def _mask_dot(mask, x):
    hi = x.astype(BF16)
    r1 = x - hi.astype(F32)
    mid = r1.astype(BF16)
    lo = (r1 - mid.astype(F32)).astype(BF16)
    return _dot(mask, hi) + _dot(mask, mid) + _dot(mask, lo)


def _log_sigmoid(x):
    return jnp.minimum(x, 0.0) - jnp.log1p(jnp.exp(-jnp.abs(x)))


def _conv_taps(prev8, uc, w):
    ext = jnp.concatenate([prev8, uc], axis=0)
    s1 = pltpu.roll(ext, 1, 0)[SUBLANES:]
    s2 = pltpu.roll(ext, 2, 0)[SUBLANES:]
    return s2 * w[0:1] + s1 * w[1:2] + uc * w[2:3], s1, s2


def _z_specs(tile, idx):
    d_conv = 1024
    wide = lambda c: pl.BlockSpec((tile, d_conv), lambda i, c=c: (idx(i), c))
    half = lambda c: pl.BlockSpec((tile, d_conv // 2), lambda i, c=c: (idx(i), c))
    return [wide(0), wide(1), wide(2), half(6), half(7), wide(4), wide(5)]


def _mixer_fwd(z, alow, wgu, b_gate, convw, conv_g, gla_g):
    t = z.shape[0]
    tb, cpt = MIX_TILE, CHUNKS_PER_TILE
    d_conv = conv_g.shape[1]
    dv = gla_g.shape[1]
    dk = dv // 2
    d_k = GLA_HEADS * dk
    gw = d_conv // CONV_GROUPS
    scale = dk ** -0.5

    def body(cb_ref, cc_ref, ch_ref, q_ref, k_ref, v_ref, og_ref, al_ref, wgu_ref, bg_ref, cw_ref, cg_ref, gg_ref,
             y_ref, sall_ref, carry_ref, s_ref):
        @pl.when(pl.program_id(0) == 0)
        def _():
            carry_ref[...] = jnp.zeros_like(carry_ref)
            s_ref[...] = jnp.zeros_like(s_ref)

        uc = cc_ref[...] * ch_ref[...]
        conv, _, _ = _conv_taps(carry_ref[...], uc, cw_ref[...])
        carry_ref[...] = uc[tb - SUBLANES:]
        ypre = cb_ref[...] * conv
        cg = cg_ref[...]
        for g in range(CONV_GROUPS):
            sl = slice(g * gw, (g + 1) * gw)
            seg = ypre[:, sl]
            r = lax.rsqrt(jnp.mean(seg * seg, axis=-1, keepdims=True) + EPS)
            y_ref[:, sl] = (seg * r * cg[:, sl]).astype(BF16)

        later, same, _ = _chunk_masks(tb)
        pre = _dot(al_ref[...].astype(BF16), wgu_ref[...]) + bg_ref[...]
        la = _log_sigmoid(pre) * (1.0 / GATE_NORMALIZER)
        e_dec = _mask_dot(later, la)
        dec_all = jnp.exp(_mask_dot(same, la))
        kdec = (k_ref[...] * jnp.exp(e_dec)).astype(BF16)
        qs = (q_ref[...] * scale).astype(BF16)
        vb = v_ref[...].astype(BF16)
        gg = gg_ref[...]
        rows = [slice(c * CHUNK, (c + 1) * CHUNK) for c in range(cpt)]
        ks = [slice(h * dk, (h + 1) * dk) for h in range(GLA_HEADS)]
        vs = [slice(h * dv, (h + 1) * dv) for h in range(GLA_HEADS)]
        kvt = [[_dot(vb[rows[c], vs[h]], kdec[rows[c], ks[h]], _TN) for h in range(GLA_HEADS)] for c in range(cpt)]
        state = [s_ref[h] for h in range(GLA_HEADS)]
        states = []
        for c in range(cpt):
            state = [state[h] * dec_all[c * CHUNK:c * CHUNK + 1, ks[h]] + kvt[c][h] for h in range(GLA_HEADS)]
            states.append(state)
            for h in range(GLA_HEADS):
                sall_ref[c, h] = state[h]
        for h in range(GLA_HEADS):
            s_ref[h] = state[h]
        for h in range(GLA_HEADS):
            o = jnp.concatenate(
                [_dot(qs[rows[c], ks[h]], states[c][h].astype(BF16), _NT) for c in range(cpt)], axis=0)
            ro = lax.rsqrt(jnp.mean(o * o, axis=-1, keepdims=True) + EPS)
            ogs = og_ref[:, vs[h]]
            yg = o * ro * gg * (ogs * jax.nn.sigmoid(ogs))
            y_ref[:, d_conv + h * dv:d_conv + (h + 1) * dv] = yg.astype(BF16)

    full = lambda shape: pl.BlockSpec(shape, lambda i: (0,) * len(shape))
    return pl.pallas_call(
        body, name="mixer_fwd", grid=(t // tb,),
        in_specs=_z_specs(tb, lambda i: i) + [
            pl.BlockSpec((tb, LANES), lambda i: (i, 0)), full(wgu.shape), full(b_gate.shape), full(convw.shape),
            full(conv_g.shape), full(gla_g.shape)],
        out_specs=[pl.BlockSpec((tb, d_conv + GLA_HEADS * dv), lambda i: (i, 0)),
                   pl.BlockSpec((cpt, GLA_HEADS, dv, dk), lambda i: (i, 0, 0, 0))],
        out_shape=[jax.ShapeDtypeStruct((t, d_conv + GLA_HEADS * dv), BF16),
                   jax.ShapeDtypeStruct((t // CHUNK, GLA_HEADS, dv, dk), F32)],
        scratch_shapes=[pltpu.VMEM((SUBLANES, d_conv), F32), pltpu.VMEM((GLA_HEADS, dv, dk), F32)],
        compiler_params=_params(1),
    )(z, z, z, z, z, z, z, alow, wgu, b_gate, convw, conv_g, gla_g)


def _mixer_bwd(z, alow, dy, sall, wgu, b_gate, convw, conv_g, gla_g, behind=None):
    t = z.shape[0]
    tb, cpt = MIX_TILE, CHUNKS_PER_TILE
    nt = t // tb
    d_conv = conv_g.shape[1]
    dv = gla_g.shape[1]
    dk = dv // 2
    d_k = GLA_HEADS * dk
    gw = d_conv // CONV_GROUPS
    scale = dk ** -0.5
    rev = lambda i: nt - 1 - i
    dep_args, dep_specs = _behind(behind)

    def body(cb_ref, cc_ref, ch_ref, q_ref, k_ref, v_ref, og_ref, ccp_ref, chp_ref, al_ref, dy_ref, sall_ref, sprev_ref,
             wgu_ref, bg_ref, cw_ref, cg_ref, gg_ref, *rest):
        dz_ref, dzal_ref, dcw_ref, dcg_ref, dgg_ref, dbg_ref, dwgu_ref, dcarry_ref, gd_ref = rest[-9:]
        i = pl.program_id(0)

        @pl.when(i == 0)
        def _():
            dcarry_ref[...] = jnp.zeros_like(dcarry_ref)
            gd_ref[...] = jnp.zeros_like(gd_ref)
            dcw_ref[...] = jnp.zeros_like(dcw_ref)
            dcg_ref[...] = jnp.zeros_like(dcg_ref)
            dgg_ref[...] = jnp.zeros_like(dgg_ref)
            dbg_ref[...] = jnp.zeros_like(dbg_ref)
            dwgu_ref[...] = jnp.zeros_like(dwgu_ref)

        first = rev(i) == 0

        cb, cc, ch = cb_ref[...], cc_ref[...], ch_ref[...]
        w = cw_ref[...]
        uc = cc * ch
        prev8 = jnp.where(first, 0.0, ccp_ref[...] * chp_ref[...])
        conv, s1, s2 = _conv_taps(prev8, uc, w)
        ypre = cb * conv
        cg = cg_ref[...]
        dypre_parts = []
        for g in range(CONV_GROUPS):
            sl = slice(g * gw, (g + 1) * gw)
            seg = ypre[:, sl]
            r = lax.rsqrt(jnp.mean(seg * seg, axis=-1, keepdims=True) + EPS)
            yn = seg * r
            dyc = dy_ref[:, sl]
            dcg_ref[:, sl] += jnp.sum(dyc * yn, axis=0, keepdims=True)
            dyn = dyc * cg[:, sl]
            dypre_parts.append(r * (dyn - yn * jnp.mean(dyn * yn, axis=-1, keepdims=True)))
        dypre = jnp.concatenate(dypre_parts, axis=1)
        dconv = dypre * cb
        dz_ref[:, 0:d_conv] = (dypre * conv).astype(BF16)
        dcw_ref[0:1] += jnp.sum(dconv * s2, axis=0, keepdims=True)
        dcw_ref[1:2] += jnp.sum(dconv * s1, axis=0, keepdims=True)
        dcw_ref[2:3] += jnp.sum(dconv * uc, axis=0, keepdims=True)
        ext = jnp.concatenate([dconv, dcarry_ref[...]], axis=0)
        f1 = pltpu.roll(ext, tb + SUBLANES - 1, 0)[:tb]
        f2 = pltpu.roll(ext, tb + SUBLANES - 2, 0)[:tb]
        dcarry_ref[...] = dconv[:SUBLANES]
        duc = dconv * w[2:3] + f1 * w[1:2] + f2 * w[0:1]
        dz_ref[:, d_conv:2 * d_conv] = (duc * ch).astype(BF16)
        dz_ref[:, 2 * d_conv:3 * d_conv] = (duc * cc).astype(BF16)

        q_off = 3 * d_conv
        k_off = q_off + d_k
        v_off = k_off + d_k
        og_off = v_off + GLA_HEADS * dv
        later, same, earlier = _chunk_masks(tb)
        alb = al_ref[...].astype(BF16)
        pre = _dot(alb, wgu_ref[...]) + bg_ref[...]
        la = _log_sigmoid(pre) * (1.0 / GATE_NORMALIZER)
        exp_e = jnp.exp(_mask_dot(later, la))
        dec_all = jnp.exp(_mask_dot(same, la))
        kdec = k_ref[...] * exp_e
        kdec_b = kdec.astype(BF16)
        qs = (q_ref[...] * scale).astype(BF16)
        vb = v_ref[...].astype(BF16)
        gg = gg_ref[...]
        rows = [slice(c * CHUNK, (c + 1) * CHUNK) for c in range(cpt)]
        ks = [slice(h * dk, (h + 1) * dk) for h in range(GLA_HEADS)]
        vs = [slice(h * dv, (h + 1) * dv) for h in range(GLA_HEADS)]
        st_b = [[sall_ref[c, h].astype(BF16) for h in range(GLA_HEADS)] for c in range(cpt)]
        do_b = []
        dgg = jnp.zeros_like(gg)
        for h in range(GLA_HEADS):
            o = jnp.concatenate([_dot(qs[rows[c], ks[h]], st_b[c][h], _NT) for c in range(cpt)], axis=0)
            ro = lax.rsqrt(jnp.mean(o * o, axis=-1, keepdims=True) + EPS)
            on = o * ro
            ogs = og_ref[:, vs[h]]
            sg = jax.nn.sigmoid(ogs)
            gate = ogs * sg
            dyg = dy_ref[:, d_conv + h * dv:d_conv + (h + 1) * dv]
            dgg = dgg + jnp.sum(dyg * on * gate, axis=0, keepdims=True)
            dz_ref[:, og_off + h * dv:og_off + (h + 1) * dv] = (
                dyg * on * gg * (sg * (1.0 + ogs * (1.0 - sg)))).astype(BF16)
            don = dyg * gg * gate
            do_b.append((ro * (don - on * jnp.mean(don * on, axis=-1, keepdims=True))).astype(BF16))
        dgg_ref[...] += dgg
        for h in range(GLA_HEADS):
            dq = jnp.concatenate([_dot(do_b[h][rows[c]], st_b[c][h]) for c in range(cpt)], axis=0)
            dz_ref[:, q_off + h * dk:q_off + (h + 1) * dk] = (dq * scale).astype(BF16)
        own = [[_dot(do_b[h][rows[c]], qs[rows[c], ks[h]], _TN) for h in range(GLA_HEADS)] for c in range(cpt)]
        carried = [gd_ref[h] for h in range(GLA_HEADS)]
        gt_b = [None] * cpt
        ddd = [None] * cpt
        for c in reversed(range(cpt)):
            gt = [own[c][h] + carried[h] for h in range(GLA_HEADS)]
            dec = [dec_all[c * CHUNK:c * CHUNK + 1, ks[h]] for h in range(GLA_HEADS)]
            carried = [gt[h] * dec[h] for h in range(GLA_HEADS)]
            if c > 0:
                st_prev = [sall_ref[c - 1, h] for h in range(GLA_HEADS)]
            else:
                st_prev = [jnp.where(first, 0.0, sprev_ref[0, h]) for h in range(GLA_HEADS)]
            ddec = [jnp.sum(gt[h] * st_prev[h], axis=0, keepdims=True) * dec[h] for h in range(GLA_HEADS)]
            ddd[c] = jnp.broadcast_to(jnp.concatenate(ddec, axis=1), (CHUNK, d_k))
            gt_b[c] = [gt[h].astype(BF16) for h in range(GLA_HEADS)]
        for h in range(GLA_HEADS):
            gd_ref[h] = carried[h]
        dkdec_cols = []
        for h in range(GLA_HEADS):
            dvh = jnp.concatenate([_dot(kdec_b[rows[c], ks[h]], gt_b[c][h], _NT) for c in range(cpt)], axis=0)
            dz_ref[:, v_off + h * dv:v_off + (h + 1) * dv] = dvh.astype(BF16)
            dkdec_cols.append(jnp.concatenate([_dot(vb[rows[c], vs[h]], gt_b[c][h]) for c in range(cpt)], axis=0))
        dkdec = jnp.concatenate(dkdec_cols, axis=1)
        dz_ref[:, k_off:k_off + d_k] = (dkdec * exp_e).astype(BF16)
        dla = _mask_dot(earlier, dkdec * kdec) + jnp.concatenate(ddd, axis=0)
        dpre = dla * (1.0 / GATE_NORMALIZER) * jax.nn.sigmoid(-pre)
        dbg_ref[...] += jnp.sum(dpre, axis=0, keepdims=True)
        dpre_b = dpre.astype(BF16)
        dwgu_ref[...] += _dot(alb, dpre_b, _TN)
        dzal_ref[...] = _dot(dpre_b, wgu_ref[...], _NT).astype(BF16)

    full = lambda shape: pl.BlockSpec(shape, lambda i: (0,) * len(shape))
    prev_rows = lambda c: pl.BlockSpec(
        (SUBLANES, d_conv), lambda i, c=c: (jnp.maximum(rev(i) * (tb // SUBLANES) - 1, 0), c))
    n_z = 3 * d_conv + 2 * d_k + 2 * GLA_HEADS * dv
    return pl.pallas_call(
        body, name="mixer_bwd", grid=(nt,),
        in_specs=_z_specs(tb, rev) + [
            prev_rows(1), prev_rows(2),
            pl.BlockSpec((tb, LANES), lambda i: (rev(i), 0)),
            pl.BlockSpec((tb, d_conv + GLA_HEADS * dv), lambda i: (rev(i), 0)),
            pl.BlockSpec((cpt, GLA_HEADS, dv, dk), lambda i: (rev(i), 0, 0, 0)),
            pl.BlockSpec((1, GLA_HEADS, dv, dk), lambda i: (jnp.maximum(rev(i) * cpt - 1, 0), 0, 0, 0)),
            full(wgu.shape), full(b_gate.shape), full(convw.shape), full(conv_g.shape), full(gla_g.shape)]
        + dep_specs,
        out_specs=[pl.BlockSpec((tb, n_z), lambda i: (rev(i), 0)), pl.BlockSpec((tb, LANES), lambda i: (rev(i), 0)),
                   full(convw.shape), full(conv_g.shape), full(gla_g.shape), full(b_gate.shape), full(wgu.shape)],
        out_shape=[jax.ShapeDtypeStruct((t, n_z), BF16), jax.ShapeDtypeStruct((t, LANES), BF16),
                   jax.ShapeDtypeStruct(convw.shape, F32), jax.ShapeDtypeStruct(conv_g.shape, F32),
                   jax.ShapeDtypeStruct(gla_g.shape, F32), jax.ShapeDtypeStruct(b_gate.shape, F32),
                   jax.ShapeDtypeStruct(wgu.shape, F32)],
        scratch_shapes=[pltpu.VMEM((SUBLANES, d_conv), F32), pltpu.VMEM((GLA_HEADS, dv, dk), F32)],
        compiler_params=_params(1),
    )(z, z, z, z, z, z, z, z, z, alow, dy, sall, sall, wgu, b_gate, convw, conv_g, gla_g, *dep_args)


def _adamw_math(g, w, m, v):
    m = ADAM_B1 * m + (1.0 - ADAM_B1) * g
    v = ADAM_B2 * v + (1.0 - ADAM_B2) * (g * g)
    m_hat = m / (1.0 - ADAM_B1 ** ADAM_STEP)
    v_hat = v / (1.0 - ADAM_B2 ** ADAM_STEP)
    delta = -ADAM_LR * (m_hat / (jnp.sqrt(v_hat) + ADAM_EPS) + ADAM_WD * w)
    return delta, m, v


def _adamw(name, parts, w, m, v, tr):
    r, c = w.shape
    n_parts = parts.shape[0]

    def body(p_ref, w_ref, m_ref, v_ref, g_ref, d_ref, nm_ref, nv_ref):
        g = p_ref[0].astype(F32)
        for j in range(1, n_parts):
            g = g + p_ref[j].astype(F32)
        g_ref[...] = g
        d_ref[...], nm_ref[...], nv_ref[...] = _adamw_math(g, w_ref[...], m_ref[...], v_ref[...])

    blk = pl.BlockSpec((tr, c), lambda i: (i, 0))
    return pl.pallas_call(
        body, name=name, grid=(r // tr,),
        in_specs=[pl.BlockSpec((n_parts, tr, c), lambda i: (0, i, 0)), blk, blk, blk],
        out_specs=[blk] * 4, out_shape=[jax.ShapeDtypeStruct((r, c), F32)] * 4,
        compiler_params=_params(1),
    )(parts, w, m, v)


def _sum_partials(parts):
    n_parts, rows, lanes = parts.shape

    def body(p_ref, o_ref):
        g = p_ref[0]
        for j in range(1, n_parts):
            g = g + p_ref[j]
        o_ref[...] = g

    return pl.pallas_call(
        body, name="sum_small_partials", out_shape=jax.ShapeDtypeStruct((rows, lanes), F32),
        in_specs=[pl.BlockSpec(memory_space=pltpu.VMEM)], out_specs=pl.BlockSpec(memory_space=pltpu.VMEM),
    )(parts)


def _pack_rows(vectors, rows):
    flat = jnp.concatenate([a.reshape(-1).astype(F32) for a in vectors])
    return jnp.pad(flat, (0, rows * LANES - flat.shape[0])).reshape(rows, LANES)


def _unpack_rows(block, shapes):
    flat = block.reshape(-1)
    out, off = [], 0
    for s in shapes:
        n = 1
        for dim in s:
            n *= dim
        out.append(flat[off:off + n].reshape(s))
        off += n
    return out


def kernel(x, norm1_g, w_in, w_gate_up, b_gate, conv_w, conv_norm_g, gla_norm_g, w_out, norm2_g, w_ff1, w_ff2, norm_f_g, loss_target, m_norm1_g, m_w_in, m_w_gate_up, m_b_gate, m_conv_w, m_conv_norm_g, m_gla_norm_g, m_w_out, m_norm2_g, m_w_ff1, m_w_ff2, m_norm_f_g, v_norm1_g, v_w_in, v_w_gate_up, v_b_gate, v_conv_w, v_conv_norm_g, v_gla_norm_g, v_w_out, v_norm2_g, v_w_ff1, v_w_ff2, v_norm_f_g):
    me = _device_index()
    x2d, tgt = x[0], loss_target[0]
    t, d = x2d.shape
    d_in_shard = w_in.shape[2]
    d_in = N_DEV * d_in_shard
    n_main = d_in - GATE_RANK
    d_conv = conv_norm_g.shape[1]
    d_k = b_gate.shape[1]
    d_ff = N_DEV * w_ff1.shape[2]
    wmv = dict(
        norm1_g=(norm1_g, m_norm1_g, v_norm1_g), w_in=(w_in, m_w_in, v_w_in),
        w_gate_up=(w_gate_up, m_w_gate_up, v_w_gate_up), b_gate=(b_gate, m_b_gate, v_b_gate),
        conv_w=(conv_w, m_conv_w, v_conv_w), conv_norm_g=(conv_norm_g, m_conv_norm_g, v_conv_norm_g),
        gla_norm_g=(gla_norm_g, m_gla_norm_g, v_gla_norm_g), w_out=(w_out, m_w_out, v_w_out),
        norm2_g=(norm2_g, m_norm2_g, v_norm2_g), w_ff1=(w_ff1, m_w_ff1, v_w_ff1), w_ff2=(w_ff2, m_w_ff2, v_w_ff2),
        norm_f_g=(norm_f_g, m_norm_f_g, v_norm_f_g))

    small_rows = 16
    first_level = (SIBLING,) + SAME_CORE_PEERS
    win_shard = w_in[0].astype(BF16)
    in_send, in_recv, in_src, in_land, token = _exchange_start(
        "all_gather_start_w_in", [win_shard], [_land_zone(win_shard)], scatter=False, masks=[first_level])
    _, wgu_t, cw_t, wout_t, w1_t, w2_t = lax.optimization_barrier((token, w_gate_up, conv_w, w_out, w_ff1, w_ff2))
    small_shard = _pack_rows([wgu_t[0], cw_t[0]], small_rows)
    shards = [small_shard, wout_t[0].astype(BF16), w1_t[0].astype(BF16), w2_t[0].astype(BF16)]
    ag_send, ag_recv, ag_src, ag_land, token = _exchange_start(
        "all_gather_start", shards, [_land_zone(s) for s in shards], scatter=False, behind=token)

    def gathered(k, name, after):
        return _exchange_wait(name, ag_send[k], ag_recv[k], ag_src[k], ag_land[k], after, scatter=False)

    u = _rmsnorm(x2d, norm1_g, behind=token)
    small_wmv = [wmv[nm][k] for k in range(3) for nm in _SMALL_ORDER]
    tied = lax.optimization_barrier((token, m_w_in, v_w_in, *small_wmv))
    wmv["w_in"] = (w_in, tied[1], tied[2])
    n_small = len(_SMALL_ORDER)
    packed_small = [_pack_rows(list(tied[3 + k * n_small:3 + (k + 1) * n_small]) + [jnp.full((1,), _LOSS_SLOT[k], F32)],
                               SMALL_LOCAL_ROWS) for k in range(3)]
    small_g = gathered(0, "all_gather_wait_small", u)
    win_level1 = _exchange_wait(
        "all_gather_wait_w_in", in_send[0], in_recv[0], in_src[0], in_land[0],
        [small_g, wmv["w_in"][1][0], wmv["w_in"][2][0]] + packed_small, scatter=False, masks=first_level)
    win_g = _forward_wait("all_gather_wait_w_in_forwarded", *_forward_start("all_gather_forward_w_in", win_level1))
    w_main, w_alow = _shards_to_columns(win_g, n_main)
    small_flat = small_g.reshape(N_DEV, -1)
    n_wgu = GATE_RANK * (d_k // N_DEV)
    wgu_full = small_flat[:, :n_wgu].reshape(N_DEV, GATE_RANK, d_k // N_DEV).transpose(1, 0, 2).reshape(GATE_RANK, d_k)
    conv_w_full = small_flat[:, n_wgu:n_wgu + (d_conv // N_DEV) * CONV_WIDTH].reshape(d_conv, CONV_WIDTH)
    wgu_pad = jnp.pad(wgu_full, ((0, LANES - GATE_RANK), (0, 0))).astype(BF16)
    convw_taps = jnp.pad(conv_w_full.T, ((0, SUBLANES - CONV_WIDTH), (0, 0)))

    get_w_out = lambda after: gathered(1, "all_gather_wait_w_out", after).reshape(-1, d)
    get_w1 = lambda after: gathered(2, "all_gather_wait_w_ff1", after)
    get_w2 = lambda after: gathered(3, "all_gather_wait_w_ff2", after).reshape(d_ff, d)

    in_flight = {}

    def send_partials(name, parts):
        own = lax.dynamic_index_in_dim(parts, me, axis=0, keepdims=False)
        send, recv, src, land, token = _exchange_start("scatter_start_" + name, [parts], [_land_zone(own)], scatter=True)
        in_flight[name] = (send[0], recv[0], src[0], land[0])
        return token

    def on_grad(name, value):
        if name == "w_in":
            main, alow_part = value
            value = _columns_to_shards(main, alow_part, N_DEV, d_in_shard)
        elif name in ("w_out", "w_ff2"):
            value = value.reshape(N_DEV, -1, d)
        return send_partials(name, value)

    grads = _local_step(x2d, u, tgt, norm1_g, w_main, w_alow, wgu_pad, b_gate, convw_taps, conv_norm_g, gla_norm_g,
                        norm2_g, norm_f_g, get_w_out, get_w1, get_w2, on_grad)
    grad_x = grads["x"]

    small_shapes = [(1, d), (1, d_k), (1, d_conv), (1, gla_norm_g.shape[1]), (1, d), (d,),
                    (GATE_RANK, d_k), (d_conv, CONV_WIDTH), (1,)]
    small_grad_rows = 152
    small_part = _pack_rows(
        [grads["norm1_g"], grads["b_gate"], grads["conv_norm_g"], grads["gla_norm_g"], grads["norm2_g"],
         grads["norm_f_g"], grads["w_gate_up"][:GATE_RANK], grads["conv_w"][:CONV_WIDTH].T, grads["loss"][0, 0]],
        small_grad_rows)
    send_partials("small", jnp.broadcast_to(small_part[None], (N_DEV, small_grad_rows, LANES)))

    def received(name):
        send, recv, src, land = in_flight[name]
        return _exchange_wait("scatter_wait_" + name, send, recv, src, land, grad_x, scatter=True)

    small_r = received("small")
    gin_r, gout_r, g1_r, g2_r = received("w_in"), received("w_out"), received("w_ff1"), received("w_ff2")
    return _update(me, gin_r, gout_r, g1_r, g2_r, small_r, small_shapes, grad_x, wmv, packed_small)


def _local_step(x2d, u, tgt, norm1_g, w_main, w_alow, wgu_pad, b_gate, convw_taps, conv_norm_g, gla_norm_g,
                norm2_g, norm_f_g, get_w_out, get_w1, get_w2, on_grad):
    t, d = x2d.shape
    n_main = w_main.shape[1]

    z, alow = _inproj(u, w_main, w_alow)
    y, sall = _mixer_fwd(z, alow, wgu_pad, b_gate, convw_taps, conv_norm_g, gla_norm_g)
    w_out_full = get_w_out(y)
    x1, h = _outproj(y, w_out_full, x2d, norm2_g)
    w1g = get_w1(h)
    a = _ff1(h, w1g)
    w2_full = get_w2(a)
    d_ff = w2_full.shape[0]
    x2 = _ff2(a, w2_full, x1)
    dx2, dx2b, loss_part, d_normf = _loss_head(x2, norm_f_g.reshape(1, d), tgt)

    tk = min(4096, t)
    nk = t // tk
    da = _dff2(dx2b, w2_full, a)
    dw2 = _tn_matmul(
        "dw_ff2", a, dx2b, (d_ff // 1024, d // 1024, nk),
        pl.BlockSpec((tk, 1024), lambda m, j, kk: (kk, m)), pl.BlockSpec((tk, 1024), lambda m, j, kk: (kk, j)),
        jax.ShapeDtypeStruct((d_ff, d), BF16), pl.BlockSpec((1024, 1024), lambda m, j, kk: (m, j)), (1024, 1024),
        a_fn=_relu_sq)
    token = on_grad("w_ff2", dw2)
    f_shard = d_ff // N_DEV
    dw1 = _tn_matmul(
        "dw_ff1", h, da, (N_DEV, d // 1024, nk),
        pl.BlockSpec((tk, 1024), lambda g, m, kk: (kk, m)), pl.BlockSpec((tk, f_shard), lambda g, m, kk: (kk, g)),
        jax.ShapeDtypeStruct((N_DEV, d, f_shard), BF16), pl.BlockSpec((None, 1024, f_shard), lambda g, m, kk: (g, m, 0)),
        (1024, f_shard), behind=token)
    token = on_grad("w_ff1", dw1)
    dh = _dh(da, w1g, behind=token)
    dx1, dx1b, d_norm2 = _norm_bwd("norm2_bwd", dh, x1, norm2_g, dx2, with_bf16=True)
    dy = _nt_matmul("dy", dx1b, w_out_full)
    dwout = _tn_matmul(
        "dw_out", y, dx1b, (d // 1024, d // 1024, nk),
        pl.BlockSpec((tk, 1024), lambda m, j, kk: (kk, m)), pl.BlockSpec((tk, 1024), lambda m, j, kk: (kk, j)),
        jax.ShapeDtypeStruct((d, d), BF16), pl.BlockSpec((1024, 1024), lambda m, j, kk: (m, j)), (1024, 1024))
    token = on_grad("w_out", dwout)
    dz, dzal, d_convw, d_convg, d_glag, d_bgate, d_wgu = _mixer_bwd(
        z, alow, dy, sall, wgu_pad, b_gate, convw_taps, conv_norm_g, gla_norm_g, behind=token)
    dwin_main = _tn_matmul(
        "dw_in", u, dz, (d // 1024, n_main // 1024, nk),
        pl.BlockSpec((tk, 1024), lambda m, j, kk: (kk, m)), pl.BlockSpec((tk, 1024), lambda m, j, kk: (kk, j)),
        jax.ShapeDtypeStruct((d, n_main), BF16), pl.BlockSpec((1024, 1024), lambda m, j, kk: (m, j)), (1024, 1024))
    dwin_alow = _tn_matmul(
        "dw_in_alow", u, dzal, (d // 1024, 1, nk),
        pl.BlockSpec((tk, 1024), lambda m, j, kk: (kk, m)), pl.BlockSpec((tk, LANES), lambda m, j, kk: (kk, 0)),
        jax.ShapeDtypeStruct((d, LANES), BF16), pl.BlockSpec((1024, LANES), lambda m, j, kk: (m, 0)), (1024, LANES))
    token = on_grad("w_in", (dwin_main, dwin_alow))
    du = _du(dz, w_main, dzal, w_alow, behind=token)
    grad_x, d_norm1 = _norm_bwd("norm1_bwd", du, x2d, norm1_g, dx1, with_bf16=False)
    return dict(x=grad_x, loss=loss_part, norm1_g=d_norm1, w_gate_up=d_wgu, b_gate=d_bgate, conv_w=d_convw,
                conv_norm_g=d_convg, gla_norm_g=d_glag, norm2_g=d_norm2, norm_f_g=d_normf)


_WEIGHT_ORDER = ("norm1_g", "w_in", "w_gate_up", "b_gate", "conv_w", "conv_norm_g", "gla_norm_g", "w_out", "norm2_g",
                 "w_ff1", "w_ff2", "norm_f_g")
_SMALL_ORDER = ("norm1_g", "b_gate", "conv_norm_g", "gla_norm_g", "norm2_g", "norm_f_g", "w_gate_up", "conv_w")
SMALL_LOCAL_ROWS = 80
_LOSS_SLOT = (0.0, 0.0, 1.0)


def _update(me, gin_r, gout_r, g1_r, g2_r, small_r, small_shapes, grad_x, wmv, packed_small):
    big = {
        "w_in": _adamw("adamw_w_in", gin_r, *(a[0] for a in wmv["w_in"]), 256),
        "w_out": _adamw("adamw_w_out", gout_r, *(a[0] for a in wmv["w_out"]), 128),
        "w_ff1": _adamw("adamw_w_ff1", g1_r, *(a[0] for a in wmv["w_ff1"]), 256),
        "w_ff2": _adamw("adamw_w_ff2", g2_r, *(a[0] for a in wmv["w_ff2"]), 128),
    }

    wgu_cols = wmv["w_gate_up"][0].shape[2]
    cw_rows = wmv["conv_w"][0].shape[1]

    summed = _unpack_rows(_sum_partials(small_r), small_shapes)
    summed[6] = lax.dynamic_slice_in_dim(summed[6], me * wgu_cols, wgu_cols, axis=1)
    summed[7] = lax.dynamic_slice_in_dim(summed[7], me * cw_rows, cw_rows, axis=0)
    local_shapes = small_shapes[:6] + [(GATE_RANK, wgu_cols), (cw_rows, CONV_WIDTH), (1,)]
    out_small = _adamw("adamw_small", _pack_rows(summed, SMALL_LOCAL_ROWS)[None], *packed_small, SMALL_LOCAL_ROWS)
    unpacked = [_unpack_rows(o, local_shapes) for o in out_small]

    outs = []
    for k in range(4):
        for nm in _WEIGHT_ORDER:
            if nm in big:
                outs.append(big[nm][k][None])
            else:
                val = unpacked[k][_SMALL_ORDER.index(nm)]
                outs.append(val.reshape(wmv[nm][0].shape))
    loss = unpacked[0][8][0]
    return (loss, grad_x[None], *outs)
```

```python
import functools

import jax
import jax.numpy as jnp
from jax import lax
from jax.experimental import pallas as pl
from jax.experimental.pallas import tpu as pltpu

F32 = jnp.float32
BF16 = jnp.bfloat16

N_DEV = 8
CHUNK = 64
GLA_HEADS = 4
CONV_GROUPS = 8
CONV_WIDTH = 3
GATE_RANK = 16
GATE_NORMALIZER = 16.0
EPS = 1e-6
ADAM_LR = 0.001
ADAM_B1 = 0.9
ADAM_B2 = 0.999
ADAM_EPS = 1e-08
ADAM_WD = 0.01
ADAM_STEP = 10

LANES = 128
SUBLANES = 8
VMEM_LIMIT = 56 << 20

_NN = (((1,), (0,)), ((), ()))
_NT = (((1,), (1,)), ((), ()))
_TN = (((0,), (0,)), ((), ()))


def _dot(a, b, dims=_NN):
    return lax.dot_general(a, b, dims, preferred_element_type=F32)


def _params(n_grid):
    return pltpu.CompilerParams(dimension_semantics=("arbitrary",) * n_grid, vmem_limit_bytes=VMEM_LIMIT)


def _relu_sq(a):
    r = jnp.maximum(a, 0.0)
    return r * r


def _device_index():
    return 4 * lax.axis_index("x") + 2 * lax.axis_index("y") + lax.axis_index("c")


def _peer(mask):
    x, y, c = lax.axis_index("x"), lax.axis_index("y"), lax.axis_index("c")
    return (x ^ ((mask >> 2) & 1), y ^ ((mask >> 1) & 1), c ^ (mask & 1))


_HBM_SPEC = pl.BlockSpec(memory_space=pltpu.HBM)
_SEM_SPEC = pl.BlockSpec(memory_space=pltpu.SEMAPHORE)
_SIDE_EFFECT = pltpu.SideEffectType.DATAFLOW_SIDE_EFFECTING
N_PEERS = N_DEV - 1


def _exchange_copy(src_ref, land_ref, send_sems, recv_sems, mask, scatter, arriving):
    me = _device_index()
    src = src_ref.at[me ^ mask] if scatter else src_ref
    dst = land_ref.at[(me ^ mask) if arriving else me]
    return pltpu.make_async_remote_copy(
        src_ref=src, dst_ref=dst, send_sem=send_sems.at[mask - 1], recv_sem=recv_sems.at[mask - 1],
        device_id=_peer(mask), device_id_type=pl.DeviceIdType.MESH)


def _land_zone(own):
    zone = lax.empty((N_DEV,) + own.shape, own.dtype)
    return lax.dynamic_update_slice(zone, own[None], (_device_index(),) + (0,) * own.ndim)


ALL_PEERS = tuple(range(1, N_DEV))
SIBLING = 1
SAME_CORE_PEERS = (2, 4, 6)


def _exchange_start(name, srcs, lands, scatter, masks=None, behind=None):
    n = len(srcs)
    masks = masks or [ALL_PEERS] * n
    dep_args = [] if behind is None else [behind]

    def body(*refs):
        src, land = refs[:n], refs[n:2 * n]
        outs = refs[2 * n + len(dep_args):]
        send_sems, recv_sems = outs[:n], outs[n:2 * n]
        token = refs[-1]
        for a in range(n):
            for mask in masks[a]:
                _exchange_copy(src[a], land[a], send_sems[a], recv_sems[a], mask, scatter, False).start()
        token[...] = jnp.zeros_like(token)

    hbm = lambda a: pltpu.HBM(a.shape, a.dtype)
    outs = pl.pallas_call(
        body, name=name,
        out_shape=([pltpu.SemaphoreType.DMA((N_PEERS,))] * (2 * n) + [hbm(a) for a in srcs] + [hbm(a) for a in lands]
                   + [jax.ShapeDtypeStruct((SUBLANES, LANES), F32)]),
        in_specs=[_HBM_SPEC] * (2 * n) + [pl.BlockSpec(memory_space=pl.ANY)] * len(dep_args),
        out_specs=[_SEM_SPEC] * (2 * n) + [_HBM_SPEC] * (2 * n) + [pl.BlockSpec(memory_space=pltpu.VMEM)],
        input_output_aliases={a: 2 * n + a for a in range(2 * n)},
        compiler_params=pltpu.CompilerParams(has_side_effects=_SIDE_EFFECT),
    )(*[pltpu.with_memory_space_constraint(a, pltpu.HBM) for a in list(srcs) + list(lands)], *dep_args)
    send_sems, recv_sems = outs[:n], outs[n:2 * n]
    src_thru, land_thru = outs[2 * n:3 * n], outs[3 * n:4 * n]
    return send_sems, recv_sems, src_thru, land_thru, outs[-1]


def _exchange_wait(name, send_sems, recv_sems, src_thru, land_thru, after, scatter, masks=ALL_PEERS):
    after = list(after) if isinstance(after, (list, tuple)) else [after]

    def body(src_ref, land_ref, send_ref, recv_ref, *rest):
        for mask in masks:
            cp = _exchange_copy(src_ref, land_ref, send_ref, recv_ref, mask, scatter, True)
            cp.wait_send()
            cp.wait_recv()

    return pl.pallas_call(
        body, name=name,
        out_shape=(pltpu.HBM(src_thru.shape, src_thru.dtype), pltpu.HBM(land_thru.shape, land_thru.dtype)),
        in_specs=[_HBM_SPEC, _HBM_SPEC, _SEM_SPEC, _SEM_SPEC] + [pl.BlockSpec(memory_space=pl.ANY)] * len(after),
        out_specs=(_HBM_SPEC, _HBM_SPEC), input_output_aliases={0: 0, 1: 1},
        compiler_params=pltpu.CompilerParams(has_side_effects=_SIDE_EFFECT),
    )(src_thru, land_thru, send_sems, recv_sems, *after)[1]


def _forward_copy(land_ref, send_sems, recv_sems, k, arriving):
    me = _device_index()
    slot = me ^ SAME_CORE_PEERS[k]
    return pltpu.make_async_remote_copy(
        src_ref=land_ref.at[slot], dst_ref=land_ref.at[(slot ^ SIBLING) if arriving else slot],
        send_sem=send_sems.at[k], recv_sem=recv_sems.at[k],
        device_id=_peer(SIBLING), device_id_type=pl.DeviceIdType.MESH)


def _forward_start(name, land):
    n_fwd = len(SAME_CORE_PEERS)

    def body(land_ref, send_sems, recv_sems, land_thru):
        for k in range(n_fwd):
            _forward_copy(land_ref, send_sems, recv_sems, k, False).start()

    send, recv, thru = pl.pallas_call(
        body, name=name,
        out_shape=[pltpu.SemaphoreType.DMA((n_fwd,)), pltpu.SemaphoreType.DMA((n_fwd,)), pltpu.HBM(land.shape, land.dtype)],
        in_specs=[_HBM_SPEC], out_specs=[_SEM_SPEC, _SEM_SPEC, _HBM_SPEC], input_output_aliases={0: 2},
        compiler_params=pltpu.CompilerParams(has_side_effects=_SIDE_EFFECT),
    )(pltpu.with_memory_space_constraint(land, pltpu.HBM))
    return send, recv, thru


def _forward_wait(name, send_sems, recv_sems, land_thru):
    def body(land_ref, send_ref, recv_ref, got_ref):
        for k in range(len(SAME_CORE_PEERS)):
            cp = _forward_copy(land_ref, send_ref, recv_ref, k, True)
            cp.wait_send()
            cp.wait_recv()

    return pl.pallas_call(
        body, name=name, out_shape=pltpu.HBM(land_thru.shape, land_thru.dtype),
        in_specs=[_HBM_SPEC, _SEM_SPEC, _SEM_SPEC], out_specs=_HBM_SPEC, input_output_aliases={0: 0},
        compiler_params=pltpu.CompilerParams(has_side_effects=_SIDE_EFFECT),
    )(land_thru, send_sems, recv_sems)


def _shards_to_columns(g, n_main, tr=256):
    n_dev, d, s = g.shape

    def body(g_ref, main_ref, rest_ref):
        for j in range(n_dev):
            lo, hi = j * s, (j + 1) * s
            if hi <= n_main:
                main_ref[:, lo:hi] = g_ref[j]
            else:
                main_ref[:, lo:n_main] = g_ref[j, :, 0:n_main - lo]
                rest_ref[...] = jnp.zeros_like(rest_ref)
                rest_ref[:, 0:hi - n_main] = g_ref[j, :, n_main - lo:s]

    return pl.pallas_call(
        body, grid=(d // tr,), name="shards_to_columns",
        in_specs=[pl.BlockSpec((n_dev, tr, s), lambda i: (0, i, 0))],
        out_specs=[pl.BlockSpec((tr, n_main), lambda i: (i, 0)), pl.BlockSpec((tr, LANES), lambda i: (i, 0))],
        out_shape=[jax.ShapeDtypeStruct((d, n_main), g.dtype), jax.ShapeDtypeStruct((d, LANES), g.dtype)],
        compiler_params=_params(1),
    )(g)


def _columns_to_shards(main, rest, n_dev, s, tr=256):
    d, n_main = main.shape
    assert (n_dev - 1) * s <= n_main < n_dev * s

    def body(main_ref, rest_ref, o_ref):
        for j in range(n_dev):
            lo, hi = j * s, (j + 1) * s
            if hi <= n_main:
                o_ref[j] = main_ref[:, lo:hi]
            else:
                o_ref[j, :, 0:n_main - lo] = main_ref[:, lo:n_main]
                o_ref[j, :, n_main - lo:s] = rest_ref[:, 0:hi - n_main]

    return pl.pallas_call(
        body, grid=(d // tr,), name="columns_to_shards",
        in_specs=[pl.BlockSpec((tr, n_main), lambda i: (i, 0)), pl.BlockSpec((tr, LANES), lambda i: (i, 0))],
        out_specs=pl.BlockSpec((n_dev, tr, s), lambda i: (0, i, 0)),
        out_shape=jax.ShapeDtypeStruct((n_dev, d, s), main.dtype),
        compiler_params=_params(1),
    )(main, rest)


def _rmsnorm(x, g, tr=512, behind=None):
    t, d = x.shape
    tr = min(tr, t)
    dep_args, dep_specs = _behind(behind)

    def body(x_ref, g_ref, *rest):
        u_ref = rest[-1]
        xf = x_ref[...]
        r = lax.rsqrt(jnp.mean(xf * xf, axis=-1, keepdims=True) + EPS)
        u_ref[...] = (xf * r * g_ref[...]).astype(BF16)

    return pl.pallas_call(
        body, name="rmsnorm1", grid=(t // tr,),
        in_specs=[pl.BlockSpec((tr, d), lambda i: (i, 0)), pl.BlockSpec((1, d), lambda i: (0, 0))] + dep_specs,
        out_specs=pl.BlockSpec((tr, d), lambda i: (i, 0)),
        out_shape=jax.ShapeDtypeStruct((t, d), BF16),
        compiler_params=_params(1),
    )(x, g, *dep_args)


def _inproj(u, w_main, w_alow, tm=1024, tn=1024):
    t, d = u.shape
    tm = min(tm, t)
    n = w_main.shape[1]

    def body(u_ref, w_ref, wa_ref, z_ref, al_ref):
        @pl.when(pl.program_id(1) == 0)
        def _():
            al_ref[...] = _dot(u_ref[...], wa_ref[...])

        z_ref[...] = _dot(u_ref[...], w_ref[...])

    return pl.pallas_call(
        body, name="inproj", grid=(t // tm, n // tn),
        in_specs=[pl.BlockSpec((tm, d), lambda m, j: (m, 0)), pl.BlockSpec((d, tn), lambda m, j: (0, j)),
                  pl.BlockSpec((d, LANES), lambda m, j: (0, 0))],
        out_specs=[pl.BlockSpec((tm, tn), lambda m, j: (m, j)), pl.BlockSpec((tm, LANES), lambda m, j: (m, 0))],
        out_shape=[jax.ShapeDtypeStruct((t, n), F32), jax.ShapeDtypeStruct((t, LANES), F32)],
        compiler_params=_params(2),
    )(u, w_main, w_alow)


def _outproj(y, w_out, x, g2, tm=512):
    t, d = x.shape
    tm = min(tm, t)
    k = y.shape[1]

    def body(y_ref, w_ref, x_ref, g_ref, x1_ref, h_ref):
        x1 = x_ref[...] + _dot(y_ref[...], w_ref[...])
        x1_ref[...] = x1
        r = lax.rsqrt(jnp.mean(x1 * x1, axis=-1, keepdims=True) + EPS)
        h_ref[...] = (x1 * r * g_ref[...]).astype(BF16)

    return pl.pallas_call(
        body, name="outproj_rmsnorm", grid=(t // tm,),
        in_specs=[pl.BlockSpec((tm, k), lambda m: (m, 0)), pl.BlockSpec((k, d), lambda m: (0, 0)),
                  pl.BlockSpec((tm, d), lambda m: (m, 0)), pl.BlockSpec((1, d), lambda m: (0, 0))],
        out_specs=[pl.BlockSpec((tm, d), lambda m: (m, 0)), pl.BlockSpec((tm, d), lambda m: (m, 0))],
        out_shape=[jax.ShapeDtypeStruct((t, d), F32), jax.ShapeDtypeStruct((t, d), BF16)],
        compiler_params=_params(1),
    )(y, w_out, x, g2)


def _ff1(h, w1g, tm=1024):
    t, d = h.shape
    tm = min(tm, t)
    g, _, f = w1g.shape

    def body(h_ref, w_ref, a_ref):
        a_ref[...] = _dot(h_ref[...], w_ref[...]).astype(BF16)

    return pl.pallas_call(
        body, name="ff1", grid=(t // tm, g),
        in_specs=[pl.BlockSpec((tm, d), lambda m, j: (m, 0)), pl.BlockSpec((None, d, f), lambda m, j: (j, 0, 0))],
        out_specs=pl.BlockSpec((tm, f), lambda m, j: (m, j)),
        out_shape=jax.ShapeDtypeStruct((t, g * f), BF16),
        compiler_params=_params(2),
    )(h, w1g)


def _ff2(a, w2, x1, tm=1024, tn=1024, tk=2048):
    t, f = a.shape
    tm = min(tm, t)
    d = w2.shape[1]

    def body(a_ref, w_ref, x1_ref, o_ref):
        @pl.when(pl.program_id(2) == 0)
        def _():
            o_ref[...] = x1_ref[...]

        o_ref[...] += _dot(_relu_sq(a_ref[...]), w_ref[...])

    return pl.pallas_call(
        body, name="ff2_residual", grid=(t // tm, d // tn, f // tk),
        in_specs=[pl.BlockSpec((tm, tk), lambda m, j, kk: (m, kk)), pl.BlockSpec((tk, tn), lambda m, j, kk: (kk, j)),
                  pl.BlockSpec((tm, tn), lambda m, j, kk: (m, j))],
        out_specs=pl.BlockSpec((tm, tn), lambda m, j, kk: (m, j)),
        out_shape=jax.ShapeDtypeStruct((t, d), F32),
        compiler_params=_params(3),
    )(a, w2, x1)


def _dff2(dx2b, w2, a, tm=1024, tn=1024):
    t, d = dx2b.shape
    tm = min(tm, t)
    f = w2.shape[0]

    def body(g_ref, w_ref, a_ref, o_ref):
        dp = _dot(g_ref[...], w_ref[...], _NT)
        o_ref[...] = (dp * (2.0 * jnp.maximum(a_ref[...].astype(F32), 0.0))).astype(BF16)

    return pl.pallas_call(
        body, name="dff2", grid=(t // tm, f // tn),
        in_specs=[pl.BlockSpec((tm, d), lambda m, j: (m, 0)), pl.BlockSpec((tn, d), lambda m, j: (j, 0)),
                  pl.BlockSpec((tm, tn), lambda m, j: (m, j))],
        out_specs=pl.BlockSpec((tm, tn), lambda m, j: (m, j)),
        out_shape=jax.ShapeDtypeStruct((t, f), BF16),
        compiler_params=_params(2),
    )(dx2b, w2, a)


def _behind(token):
    if token is None:
        return [], []
    return [token], [pl.BlockSpec(token.shape, lambda *_: (0,) * token.ndim)]


def _tn_matmul(name, a, b, grid, a_spec, b_spec, out_shape, out_spec, acc_shape, a_fn=None, behind=None):
    nk = grid[-1]
    dep_args, dep_specs = _behind(behind)

    def body(a_ref, b_ref, *rest):
        o_ref, acc_ref = rest[-2:]
        kk = pl.program_id(len(grid) - 1)
        av = a_ref[...]
        if a_fn is not None:
            av = a_fn(av)
        part = _dot(av, b_ref[...], _TN)

        @pl.when(kk == 0)
        def _():
            acc_ref[...] = part

        @pl.when(kk > 0)
        def _():
            acc_ref[...] += part

        @pl.when(kk == nk - 1)
        def _():
            o_ref[...] = acc_ref[...].astype(o_ref.dtype)

    return pl.pallas_call(
        body, name=name, grid=grid, in_specs=[a_spec, b_spec] + dep_specs, out_specs=out_spec, out_shape=out_shape,
        scratch_shapes=[pltpu.VMEM(acc_shape, F32)], compiler_params=_params(len(grid)),
    )(a, b, *dep_args)


def _dh(da, w1g, tm=1024, tn=1024, shards_per_step=4, behind=None):
    t = da.shape[0]
    tm = min(tm, t)
    g, d, f = w1g.shape
    sps = shards_per_step
    dep_args, dep_specs = _behind(behind)

    def body(a_ref, w_ref, *rest):
        o_ref = rest[-1]
        acc = _dot(a_ref[:, 0:f], w_ref[0], _NT)
        for s in range(1, sps):
            acc = acc + _dot(a_ref[:, s * f:(s + 1) * f], w_ref[s], _NT)

        @pl.when(pl.program_id(2) == 0)
        def _():
            o_ref[...] = acc

        @pl.when(pl.program_id(2) > 0)
        def _():
            o_ref[...] += acc

    return pl.pallas_call(
        body, name="dh", grid=(t // tm, d // tn, g // sps),
        in_specs=[pl.BlockSpec((tm, sps * f), lambda m, j, kk: (m, kk)),
                  pl.BlockSpec((sps, tn, f), lambda m, j, kk: (kk, j, 0))] + dep_specs,
        out_specs=pl.BlockSpec((tm, tn), lambda m, j, kk: (m, j)),
        out_shape=jax.ShapeDtypeStruct((t, d), F32),
        compiler_params=_params(3),
    )(da, w1g, *dep_args)


def _nt_matmul(name, a, b, tm=1024, tn=1024):
    t, k = a.shape
    tm = min(tm, t)
    n = b.shape[0]

    def body(a_ref, b_ref, o_ref):
        o_ref[...] = _dot(a_ref[...], b_ref[...], _NT)

    return pl.pallas_call(
        body, name=name, grid=(t // tm, n // tn),
        in_specs=[pl.BlockSpec((tm, k), lambda m, j: (m, 0)), pl.BlockSpec((tn, k), lambda m, j: (j, 0))],
        out_specs=pl.BlockSpec((tm, tn), lambda m, j: (m, j)),
        out_shape=jax.ShapeDtypeStruct((t, n), F32),
        compiler_params=_params(2),
    )(a, b)


def _du(dz, w_main, dzal, w_alow, tm=1024, tn=1024, tk=3072, behind=None):
    t, n = dz.shape
    tm = min(tm, t)
    d = w_main.shape[0]
    dep_args, dep_specs = _behind(behind)

    def body(a_ref, w_ref, al_ref, wa_ref, *rest):
        o_ref = rest[-1]

        @pl.when(pl.program_id(2) == 0)
        def _():
            o_ref[...] = _dot(al_ref[...], wa_ref[...], _NT)

        o_ref[...] += _dot(a_ref[...], w_ref[...], _NT)

    return pl.pallas_call(
        body, name="du", grid=(t // tm, d // tn, n // tk),
        in_specs=[pl.BlockSpec((tm, tk), lambda m, j, kk: (m, kk)), pl.BlockSpec((tn, tk), lambda m, j, kk: (j, kk)),
                  pl.BlockSpec((tm, LANES), lambda m, j, kk: (m, 0)), pl.BlockSpec((tn, LANES), lambda m, j, kk: (j, 0))]
        + dep_specs,
        out_specs=pl.BlockSpec((tm, tn), lambda m, j, kk: (m, j)),
        out_shape=jax.ShapeDtypeStruct((t, d), F32),
        compiler_params=_params(3),
    )(dz, w_main, dzal, w_alow, *dep_args)


def _loss_head(x2, gf, tgt, tr=256):
    t, d = x2.shape

    def body(x_ref, g_ref, t_ref, dx_ref, dxb_ref, loss_ref, dg_ref):
        @pl.when(pl.program_id(0) == 0)
        def _():
            loss_ref[...] = jnp.zeros_like(loss_ref)
            dg_ref[...] = jnp.zeros_like(dg_ref)

        xf = x_ref[...]
        g = g_ref[...]
        r = lax.rsqrt(jnp.mean(xf * xf, axis=-1, keepdims=True) + EPS)
        xh = xf * r
        e = xh * g - t_ref[...]
        loss_ref[...] += 0.5 * jnp.sum(jnp.mean(e * e, axis=-1, keepdims=True))
        dy = e * (1.0 / d)
        dg_ref[...] += jnp.sum(dy * xh, axis=0, keepdims=True)
        dyg = dy * g
        dx = r * (dyg - xh * jnp.mean(dyg * xh, axis=-1, keepdims=True))
        dx_ref[...] = dx
        dxb_ref[...] = dx.astype(BF16)

    return pl.pallas_call(
        body, name="loss_head", grid=(t // tr,),
        in_specs=[pl.BlockSpec((tr, d), lambda i: (i, 0)), pl.BlockSpec((1, d), lambda i: (0, 0)),
                  pl.BlockSpec((tr, d), lambda i: (i, 0))],
        out_specs=[pl.BlockSpec((tr, d), lambda i: (i, 0)), pl.BlockSpec((tr, d), lambda i: (i, 0)),
                   pl.BlockSpec((SUBLANES, LANES), lambda i: (0, 0)), pl.BlockSpec((1, d), lambda i: (0, 0))],
        out_shape=[jax.ShapeDtypeStruct((t, d), F32), jax.ShapeDtypeStruct((t, d), BF16),
                   jax.ShapeDtypeStruct((SUBLANES, LANES), F32), jax.ShapeDtypeStruct((1, d), F32)],
        compiler_params=_params(1),
    )(x2, gf, tgt)


def _norm_bwd(name, dh, xin, g, dres, with_bf16, tr=256):
    t, d = xin.shape

    def body(dh_ref, x_ref, g_ref, dr_ref, dx_ref, *rest):
        dg_ref = rest[-1]

        @pl.when(pl.program_id(0) == 0)
        def _():
            dg_ref[...] = jnp.zeros_like(dg_ref)

        xf = x_ref[...]
        dhv = dh_ref[...]
        r = lax.rsqrt(jnp.mean(xf * xf, axis=-1, keepdims=True) + EPS)
        xh = xf * r
        dg_ref[...] += jnp.sum(dhv * xh, axis=0, keepdims=True)
        dyg = dhv * g_ref[...]
        dx = dr_ref[...] + r * (dyg - xh * jnp.mean(dyg * xh, axis=-1, keepdims=True))
        dx_ref[...] = dx
        if with_bf16:
            rest[0][...] = dx.astype(BF16)

    rows = pl.BlockSpec((tr, d), lambda i: (i, 0))
    vec = pl.BlockSpec((1, d), lambda i: (0, 0))
    return pl.pallas_call(
        body, name=name, grid=(t // tr,),
        in_specs=[rows, rows, vec, rows],
        out_specs=[rows] + [rows] * with_bf16 + [vec],
        out_shape=[jax.ShapeDtypeStruct((t, d), F32)] + [jax.ShapeDtypeStruct((t, d), BF16)] * with_bf16
        + [jax.ShapeDtypeStruct((1, d), F32)],
        compiler_params=_params(1),
    )(dh, xin, g, dres)


MIX_TILE = 256
CHUNKS_PER_TILE = MIX_TILE // CHUNK
CHUNK_SHIFT = CHUNK.bit_length() - 1
assert 1 << CHUNK_SHIFT == CHUNK


def _chunk_masks(n):
    row = lax.broadcasted_iota(jnp.int32, (n, n), 0)
    col = lax.broadcasted_iota(jnp.int32, (n, n), 1)
    same = lax.shift_right_logical(row, CHUNK_SHIFT) == lax.shift_right_logical(col, CHUNK_SHIFT)
    one = lambda m: jnp.where(m, 1.0, 0.0).astype(BF16)
    return one(same & (col > row)), one(same), one(same & (col < row))


def _mask_dot(mask, x):
    hi = x.astype(BF16)
    r1 = x - hi.astype(F32)
    mid = r1.astype(BF16)
    lo = (r1 - mid.astype(F32)).astype(BF16)
    return _dot(mask, hi) + _dot(mask, mid) + _dot(mask, lo)


def _log_sigmoid(x):
    return jnp.minimum(x, 0.0) - jnp.log1p(jnp.exp(-jnp.abs(x)))


def _conv_taps(prev8, uc, w):
    ext = jnp.concatenate([prev8, uc], axis=0)
    s1 = pltpu.roll(ext, 1, 0)[SUBLANES:]
    s2 = pltpu.roll(ext, 2, 0)[SUBLANES:]
    return s2 * w[0:1] + s1 * w[1:2] + uc * w[2:3], s1, s2


def _z_specs(tile, idx):
    d_conv = 1024
    wide = lambda c: pl.BlockSpec((tile, d_conv), lambda i, c=c: (idx(i), c))
    half = lambda c: pl.BlockSpec((tile, d_conv // 2), lambda i, c=c: (idx(i), c))
    return [wide(0), wide(1), wide(2), half(6), half(7), wide(4), wide(5)]


def _mixer_fwd(z, alow, wgu, b_gate, convw, conv_g, gla_g):
    t = z.shape[0]
    tb, cpt = MIX_TILE, CHUNKS_PER_TILE
    d_conv = conv_g.shape[1]
    dv = gla_g.shape[1]
    dk = dv // 2
    d_k = GLA_HEADS * dk
    gw = d_conv // CONV_GROUPS
    scale = dk ** -0.5

    def body(cb_ref, cc_ref, ch_ref, q_ref, k_ref, v_ref, og_ref, al_ref, wgu_ref, bg_ref, cw_ref, cg_ref, gg_ref,
             y_ref, sall_ref, carry_ref, s_ref):
        @pl.when(pl.program_id(0) == 0)
        def _():
            carry_ref[...] = jnp.zeros_like(carry_ref)
            s_ref[...] = jnp.zeros_like(s_ref)

        uc = cc_ref[...] * ch_ref[...]
        conv, _, _ = _conv_taps(carry_ref[...], uc, cw_ref[...])
        carry_ref[...] = uc[tb - SUBLANES:]
        ypre = cb_ref[...] * conv
        cg = cg_ref[...]
        for g in range(CONV_GROUPS):
            sl = slice(g * gw, (g + 1) * gw)
            seg = ypre[:, sl]
            r = lax.rsqrt(jnp.mean(seg * seg, axis=-1, keepdims=True) + EPS)
            y_ref[:, sl] = (seg * r * cg[:, sl]).astype(BF16)

        later, same, _ = _chunk_masks(tb)
        pre = _dot(al_ref[...].astype(BF16), wgu_ref[...]) + bg_ref[...]
        la = _log_sigmoid(pre) * (1.0 / GATE_NORMALIZER)
        e_dec = _mask_dot(later, la)
        dec_all = jnp.exp(_mask_dot(same, la))
        kdec = (k_ref[...] * jnp.exp(e_dec)).astype(BF16)
        qs = (q_ref[...] * scale).astype(BF16)
        vb = v_ref[...].astype(BF16)
        gg = gg_ref[...]
        rows = [slice(c * CHUNK, (c + 1) * CHUNK) for c in range(cpt)]
        ks = [slice(h * dk, (h + 1) * dk) for h in range(GLA_HEADS)]
        vs = [slice(h * dv, (h + 1) * dv) for h in range(GLA_HEADS)]
        kvt = [[_dot(vb[rows[c], vs[h]], kdec[rows[c], ks[h]], _TN) for h in range(GLA_HEADS)] for c in range(cpt)]
        state = [s_ref[h] for h in range(GLA_HEADS)]
        states = []
        for c in range(cpt):
            state = [state[h] * dec_all[c * CHUNK:c * CHUNK + 1, ks[h]] + kvt[c][h] for h in range(GLA_HEADS)]
            states.append(state)
            for h in range(GLA_HEADS):
                sall_ref[c, h] = state[h]
        for h in range(GLA_HEADS):
            s_ref[h] = state[h]
        for h in range(GLA_HEADS):
            o = jnp.concatenate(
                [_dot(qs[rows[c], ks[h]], states[c][h].astype(BF16), _NT) for c in range(cpt)], axis=0)
            ro = lax.rsqrt(jnp.mean(o * o, axis=-1, keepdims=True) + EPS)
            ogs = og_ref[:, vs[h]]
            yg = o * ro * gg * (ogs * jax.nn.sigmoid(ogs))
            y_ref[:, d_conv + h * dv:d_conv + (h + 1) * dv] = yg.astype(BF16)

    full = lambda shape: pl.BlockSpec(shape, lambda i: (0,) * len(shape))
    return pl.pallas_call(
        body, name="mixer_fwd", grid=(t // tb,),
        in_specs=_z_specs(tb, lambda i: i) + [
            pl.BlockSpec((tb, LANES), lambda i: (i, 0)), full(wgu.shape), full(b_gate.shape), full(convw.shape),
            full(conv_g.shape), full(gla_g.shape)],
        out_specs=[pl.BlockSpec((tb, d_conv + GLA_HEADS * dv), lambda i: (i, 0)),
                   pl.BlockSpec((cpt, GLA_HEADS, dv, dk), lambda i: (i, 0, 0, 0))],
        out_shape=[jax.ShapeDtypeStruct((t, d_conv + GLA_HEADS * dv), BF16),
                   jax.ShapeDtypeStruct((t // CHUNK, GLA_HEADS, dv, dk), F32)],
        scratch_shapes=[pltpu.VMEM((SUBLANES, d_conv), F32), pltpu.VMEM((GLA_HEADS, dv, dk), F32)],
        compiler_params=_params(1),
    )(z, z, z, z, z, z, z, alow, wgu, b_gate, convw, conv_g, gla_g)


def _mixer_bwd(z, alow, dy, sall, wgu, b_gate, convw, conv_g, gla_g, behind=None):
    t = z.shape[0]
    tb, cpt = MIX_TILE, CHUNKS_PER_TILE
    nt = t // tb
    d_conv = conv_g.shape[1]
    dv = gla_g.shape[1]
    dk = dv // 2
    d_k = GLA_HEADS * dk
    gw = d_conv // CONV_GROUPS
    scale = dk ** -0.5
    rev = lambda i: nt - 1 - i
    dep_args, dep_specs = _behind(behind)

    def body(cb_ref, cc_ref, ch_ref, q_ref, k_ref, v_ref, og_ref, ccp_ref, chp_ref, al_ref, dy_ref, sall_ref, sprev_ref,
             wgu_ref, bg_ref, cw_ref, cg_ref, gg_ref, *rest):
        dz_ref, dzal_ref, dcw_ref, dcg_ref, dgg_ref, dbg_ref, dwgu_ref, dcarry_ref, gd_ref = rest[-9:]
        i = pl.program_id(0)

        @pl.when(i == 0)
        def _():
            dcarry_ref[...] = jnp.zeros_like(dcarry_ref)
            gd_ref[...] = jnp.zeros_like(gd_ref)
            dcw_ref[...] = jnp.zeros_like(dcw_ref)
            dcg_ref[...] = jnp.zeros_like(dcg_ref)
            dgg_ref[...] = jnp.zeros_like(dgg_ref)
            dbg_ref[...] = jnp.zeros_like(dbg_ref)
            dwgu_ref[...] = jnp.zeros_like(dwgu_ref)

        first = rev(i) == 0

        cb, cc, ch = cb_ref[...], cc_ref[...], ch_ref[...]
        w = cw_ref[...]
        uc = cc * ch
        prev8 = jnp.where(first, 0.0, ccp_ref[...] * chp_ref[...])
        conv, s1, s2 = _conv_taps(prev8, uc, w)
        ypre = cb * conv
        cg = cg_ref[...]
        dypre_parts = []
        for g in range(CONV_GROUPS):
            sl = slice(g * gw, (g + 1) * gw)
            seg = ypre[:, sl]
            r = lax.rsqrt(jnp.mean(seg * seg, axis=-1, keepdims=True) + EPS)
            yn = seg * r
            dyc = dy_ref[:, sl]
            dcg_ref[:, sl] += jnp.sum(dyc * yn, axis=0, keepdims=True)
            dyn = dyc * cg[:, sl]
            dypre_parts.append(r * (dyn - yn * jnp.mean(dyn * yn, axis=-1, keepdims=True)))
        dypre = jnp.concatenate(dypre_parts, axis=1)
        dconv = dypre * cb
        dz_ref[:, 0:d_conv] = (dypre * conv).astype(BF16)
        dcw_ref[0:1] += jnp.sum(dconv * s2, axis=0, keepdims=True)
        dcw_ref[1:2] += jnp.sum(dconv * s1, axis=0, keepdims=True)
        dcw_ref[2:3] += jnp.sum(dconv * uc, axis=0, keepdims=True)
        ext = jnp.concatenate([dconv, dcarry_ref[...]], axis=0)
        f1 = pltpu.roll(ext, tb + SUBLANES - 1, 0)[:tb]
        f2 = pltpu.roll(ext, tb + SUBLANES - 2, 0)[:tb]
        dcarry_ref[...] = dconv[:SUBLANES]
        duc = dconv * w[2:3] + f1 * w[1:2] + f2 * w[0:1]
        dz_ref[:, d_conv:2 * d_conv] = (duc * ch).astype(BF16)
        dz_ref[:, 2 * d_conv:3 * d_conv] = (duc * cc).astype(BF16)

        q_off = 3 * d_conv
        k_off = q_off + d_k
        v_off = k_off + d_k
        og_off = v_off + GLA_HEADS * dv
        later, same, earlier = _chunk_masks(tb)
        alb = al_ref[...].astype(BF16)
        pre = _dot(alb, wgu_ref[...]) + bg_ref[...]
        la = _log_sigmoid(pre) * (1.0 / GATE_NORMALIZER)
        exp_e = jnp.exp(_mask_dot(later, la))
        dec_all = jnp.exp(_mask_dot(same, la))
        kdec = k_ref[...] * exp_e
        kdec_b = kdec.astype(BF16)
        qs = (q_ref[...] * scale).astype(BF16)
        vb = v_ref[...].astype(BF16)
        gg = gg_ref[...]
        rows = [slice(c * CHUNK, (c + 1) * CHUNK) for c in range(cpt)]
        ks = [slice(h * dk, (h + 1) * dk) for h in range(GLA_HEADS)]
        vs = [slice(h * dv, (h + 1) * dv) for h in range(GLA_HEADS)]
        st_b = [[sall_ref[c, h].astype(BF16) for h in range(GLA_HEADS)] for c in range(cpt)]
        do_b = []
        dgg = jnp.zeros_like(gg)
        for h in range(GLA_HEADS):
            o = jnp.concatenate([_dot(qs[rows[c], ks[h]], st_b[c][h], _NT) for c in range(cpt)], axis=0)
            ro = lax.rsqrt(jnp.mean(o * o, axis=-1, keepdims=True) + EPS)
            on = o * ro
            ogs = og_ref[:, vs[h]]
            sg = jax.nn.sigmoid(ogs)
            gate = ogs * sg
            dyg = dy_ref[:, d_conv + h * dv:d_conv + (h + 1) * dv]
            dgg = dgg + jnp.sum(dyg * on * gate, axis=0, keepdims=True)
            dz_ref[:, og_off + h * dv:og_off + (h + 1) * dv] = (
                dyg * on * gg * (sg * (1.0 + ogs * (1.0 - sg)))).astype(BF16)
            don = dyg * gg * gate
            do_b.append((ro * (don - on * jnp.mean(don * on, axis=-1, keepdims=True))).astype(BF16))
        dgg_ref[...] += dgg
        for h in range(GLA_HEADS):
            dq = jnp.concatenate([_dot(do_b[h][rows[c]], st_b[c][h]) for c in range(cpt)], axis=0)
            dz_ref[:, q_off + h * dk:q_off + (h + 1) * dk] = (dq * scale).astype(BF16)
        own = [[_dot(do_b[h][rows[c]], qs[rows[c], ks[h]], _TN) for h in range(GLA_HEADS)] for c in range(cpt)]
        carried = [gd_ref[h] for h in range(GLA_HEADS)]
        gt_b = [None] * cpt
        ddd = [None] * cpt
        for c in reversed(range(cpt)):
            gt = [own[c][h] + carried[h] for h in range(GLA_HEADS)]
            dec = [dec_all[c * CHUNK:c * CHUNK + 1, ks[h]] for h in range(GLA_HEADS)]
            carried = [gt[h] * dec[h] for h in range(GLA_HEADS)]
            if c > 0:
                st_prev = [sall_ref[c - 1, h] for h in range(GLA_HEADS)]
            else:
                st_prev = [jnp.where(first, 0.0, sprev_ref[0, h]) for h in range(GLA_HEADS)]
            ddec = [jnp.sum(gt[h] * st_prev[h], axis=0, keepdims=True) * dec[h] for h in range(GLA_HEADS)]
            ddd[c] = jnp.broadcast_to(jnp.concatenate(ddec, axis=1), (CHUNK, d_k))
            gt_b[c] = [gt[h].astype(BF16) for h in range(GLA_HEADS)]
        for h in range(GLA_HEADS):
            gd_ref[h] = carried[h]
        dkdec_cols = []
        for h in range(GLA_HEADS):
            dvh = jnp.concatenate([_dot(kdec_b[rows[c], ks[h]], gt_b[c][h], _NT) for c in range(cpt)], axis=0)
            dz_ref[:, v_off + h * dv:v_off + (h + 1) * dv] = dvh.astype(BF16)
            dkdec_cols.append(jnp.concatenate([_dot(vb[rows[c], vs[h]], gt_b[c][h]) for c in range(cpt)], axis=0))
        dkdec = jnp.concatenate(dkdec_cols, axis=1)
        dz_ref[:, k_off:k_off + d_k] = (dkdec * exp_e).astype(BF16)
        dla = _mask_dot(earlier, dkdec * kdec) + jnp.concatenate(ddd, axis=0)
        dpre = dla * (1.0 / GATE_NORMALIZER) * jax.nn.sigmoid(-pre)
        dbg_ref[...] += jnp.sum(dpre, axis=0, keepdims=True)
        dpre_b = dpre.astype(BF16)
        dwgu_ref[...] += _dot(alb, dpre_b, _TN)
        dzal_ref[...] = _dot(dpre_b, wgu_ref[...], _NT).astype(BF16)

    full = lambda shape: pl.BlockSpec(shape, lambda i: (0,) * len(shape))
    prev_rows = lambda c: pl.BlockSpec(
        (SUBLANES, d_conv), lambda i, c=c: (jnp.maximum(rev(i) * (tb // SUBLANES) - 1, 0), c))
    n_z = 3 * d_conv + 2 * d_k + 2 * GLA_HEADS * dv
    return pl.pallas_call(
        body, name="mixer_bwd", grid=(nt,),
        in_specs=_z_specs(tb, rev) + [
            prev_rows(1), prev_rows(2),
            pl.BlockSpec((tb, LANES), lambda i: (rev(i), 0)),
            pl.BlockSpec((tb, d_conv + GLA_HEADS * dv), lambda i: (rev(i), 0)),
            pl.BlockSpec((cpt, GLA_HEADS, dv, dk), lambda i: (rev(i), 0, 0, 0)),
            pl.BlockSpec((1, GLA_HEADS, dv, dk), lambda i: (jnp.maximum(rev(i) * cpt - 1, 0), 0, 0, 0)),
            full(wgu.shape), full(b_gate.shape), full(convw.shape), full(conv_g.shape), full(gla_g.shape)]
        + dep_specs,
        out_specs=[pl.BlockSpec((tb, n_z), lambda i: (rev(i), 0)), pl.BlockSpec((tb, LANES), lambda i: (rev(i), 0)),
                   full(convw.shape), full(conv_g.shape), full(gla_g.shape), full(b_gate.shape), full(wgu.shape)],
        out_shape=[jax.ShapeDtypeStruct((t, n_z), BF16), jax.ShapeDtypeStruct((t, LANES), BF16),
                   jax.ShapeDtypeStruct(convw.shape, F32), jax.ShapeDtypeStruct(conv_g.shape, F32),
                   jax.ShapeDtypeStruct(gla_g.shape, F32), jax.ShapeDtypeStruct(b_gate.shape, F32),
                   jax.ShapeDtypeStruct(wgu.shape, F32)],
        scratch_shapes=[pltpu.VMEM((SUBLANES, d_conv), F32), pltpu.VMEM((GLA_HEADS, dv, dk), F32)],
        compiler_params=_params(1),
    )(z, z, z, z, z, z, z, z, z, alow, dy, sall, sall, wgu, b_gate, convw, conv_g, gla_g, *dep_args)


def _adamw_math(g, w, m, v):
    m = ADAM_B1 * m + (1.0 - ADAM_B1) * g
    v = ADAM_B2 * v + (1.0 - ADAM_B2) * (g * g)
    m_hat = m / (1.0 - ADAM_B1 ** ADAM_STEP)
    v_hat = v / (1.0 - ADAM_B2 ** ADAM_STEP)
    delta = -ADAM_LR * (m_hat / (jnp.sqrt(v_hat) + ADAM_EPS) + ADAM_WD * w)
    return delta, m, v


def _adamw(name, parts, w, m, v, tr):
    r, c = w.shape
    n_parts = parts.shape[0]

    def body(p_ref, w_ref, m_ref, v_ref, g_ref, d_ref, nm_ref, nv_ref):
        g = p_ref[0].astype(F32)
        for j in range(1, n_parts):
            g = g + p_ref[j].astype(F32)
        g_ref[...] = g
        d_ref[...], nm_ref[...], nv_ref[...] = _adamw_math(g, w_ref[...], m_ref[...], v_ref[...])

    blk = pl.BlockSpec((tr, c), lambda i: (i, 0))
    return pl.pallas_call(
        body, name=name, grid=(r // tr,),
        in_specs=[pl.BlockSpec((n_parts, tr, c), lambda i: (0, i, 0)), blk, blk, blk],
        out_specs=[blk] * 4, out_shape=[jax.ShapeDtypeStruct((r, c), F32)] * 4,
        compiler_params=_params(1),
    )(parts, w, m, v)


def _sum_partials(parts):
    n_parts, rows, lanes = parts.shape

    def body(p_ref, o_ref):
        g = p_ref[0]
        for j in range(1, n_parts):
            g = g + p_ref[j]
        o_ref[...] = g

    return pl.pallas_call(
        body, name="sum_small_partials", out_shape=jax.ShapeDtypeStruct((rows, lanes), F32),
        in_specs=[pl.BlockSpec(memory_space=pltpu.VMEM)], out_specs=pl.BlockSpec(memory_space=pltpu.VMEM),
    )(parts)


def _pack_rows(vectors, rows):
    flat = jnp.concatenate([a.reshape(-1).astype(F32) for a in vectors])
    return jnp.pad(flat, (0, rows * LANES - flat.shape[0])).reshape(rows, LANES)


def _unpack_rows(block, shapes):
    flat = block.reshape(-1)
    out, off = [], 0
    for s in shapes:
        n = 1
        for dim in s:
            n *= dim
        out.append(flat[off:off + n].reshape(s))
        off += n
    return out


def kernel(x, norm1_g, w_in, w_gate_up, b_gate, conv_w, conv_norm_g, gla_norm_g, w_out, norm2_g, w_ff1, w_ff2, norm_f_g, loss_target, m_norm1_g, m_w_in, m_w_gate_up, m_b_gate, m_conv_w, m_conv_norm_g, m_gla_norm_g, m_w_out, m_norm2_g, m_w_ff1, m_w_ff2, m_norm_f_g, v_norm1_g, v_w_in, v_w_gate_up, v_b_gate, v_conv_w, v_conv_norm_g, v_gla_norm_g, v_w_out, v_norm2_g, v_w_ff1, v_w_ff2, v_norm_f_g):
    me = _device_index()
    x2d, tgt = x[0], loss_target[0]
    t, d = x2d.shape
    d_in_shard = w_in.shape[2]
    d_in = N_DEV * d_in_shard
    n_main = d_in - GATE_RANK
    d_conv = conv_norm_g.shape[1]
    d_k = b_gate.shape[1]
    d_ff = N_DEV * w_ff1.shape[2]
    wmv = dict(
        norm1_g=(norm1_g, m_norm1_g, v_norm1_g), w_in=(w_in, m_w_in, v_w_in),
        w_gate_up=(w_gate_up, m_w_gate_up, v_w_gate_up), b_gate=(b_gate, m_b_gate, v_b_gate),
        conv_w=(conv_w, m_conv_w, v_conv_w), conv_norm_g=(conv_norm_g, m_conv_norm_g, v_conv_norm_g),
        gla_norm_g=(gla_norm_g, m_gla_norm_g, v_gla_norm_g), w_out=(w_out, m_w_out, v_w_out),
        norm2_g=(norm2_g, m_norm2_g, v_norm2_g), w_ff1=(w_ff1, m_w_ff1, v_w_ff1), w_ff2=(w_ff2, m_w_ff2, v_w_ff2),
        norm_f_g=(norm_f_g, m_norm_f_g, v_norm_f_g))

    small_rows = 16
    first_level = (SIBLING,) + SAME_CORE_PEERS
    win_shard = w_in[0].astype(BF16)
    in_send, in_recv, in_src, in_land, token = _exchange_start(
        "all_gather_start_w_in", [win_shard], [_land_zone(win_shard)], scatter=False, masks=[first_level])
    _, wgu_t, cw_t, wout_t, w1_t, w2_t = lax.optimization_barrier((token, w_gate_up, conv_w, w_out, w_ff1, w_ff2))
    small_shard = _pack_rows([wgu_t[0], cw_t[0]], small_rows)
    shards = [small_shard, wout_t[0].astype(BF16), w1_t[0].astype(BF16), w2_t[0].astype(BF16)]
    ag_send, ag_recv, ag_src, ag_land, token = _exchange_start(
        "all_gather_start", shards, [_land_zone(s) for s in shards], scatter=False, behind=token)

    def gathered(k, name, after):
        return _exchange_wait(name, ag_send[k], ag_recv[k], ag_src[k], ag_land[k], after, scatter=False)

    u = _rmsnorm(x2d, norm1_g, behind=token)
    small_wmv = [wmv[nm][k] for k in range(3) for nm in _SMALL_ORDER]
    tied = lax.optimization_barrier((token, m_w_in, v_w_in, *small_wmv))
    wmv["w_in"] = (w_in, tied[1], tied[2])
    n_small = len(_SMALL_ORDER)
    packed_small = [_pack_rows(list(tied[3 + k * n_small:3 + (k + 1) * n_small]) + [jnp.full((1,), _LOSS_SLOT[k], F32)],
                               SMALL_LOCAL_ROWS) for k in range(3)]
    small_g = gathered(0, "all_gather_wait_small", [u, wmv["w_in"][1][0], wmv["w_in"][2][0]] + packed_small)
    win_level1 = _exchange_wait(
        "all_gather_wait_w_in", in_send[0], in_recv[0], in_src[0], in_land[0], small_g, scatter=False,
        masks=first_level)
    win_g = _forward_wait("all_gather_wait_w_in_forwarded", *_forward_start("all_gather_forward_w_in", win_level1))
    w_main, w_alow = _shards_to_columns(win_g, n_main)
    small_flat = small_g.reshape(N_DEV, -1)
    n_wgu = GATE_RANK * (d_k // N_DEV)
    wgu_full = small_flat[:, :n_wgu].reshape(N_DEV, GATE_RANK, d_k // N_DEV).transpose(1, 0, 2).reshape(GATE_RANK, d_k)
    conv_w_full = small_flat[:, n_wgu:n_wgu + (d_conv // N_DEV) * CONV_WIDTH].reshape(d_conv, CONV_WIDTH)
    wgu_pad = jnp.pad(wgu_full, ((0, LANES - GATE_RANK), (0, 0))).astype(BF16)
    convw_taps = jnp.pad(conv_w_full.T, ((0, SUBLANES - CONV_WIDTH), (0, 0)))

    get_w_out = lambda after: gathered(1, "all_gather_wait_w_out", after).reshape(-1, d)
    get_w1 = lambda after: gathered(2, "all_gather_wait_w_ff1", after)
    get_w2 = lambda after: gathered(3, "all_gather_wait_w_ff2", after).reshape(d_ff, d)

    in_flight = {}

    def send_partials(name, parts):
        own = lax.dynamic_index_in_dim(parts, me, axis=0, keepdims=False)
        send, recv, src, land, token = _exchange_start("scatter_start_" + name, [parts], [_land_zone(own)], scatter=True)
        in_flight[name] = (send[0], recv[0], src[0], land[0])
        return token

    def on_grad(name, value):
        if name == "w_in":
            main, alow_part = value
            value = _columns_to_shards(main, alow_part, N_DEV, d_in_shard)
        elif name in ("w_out", "w_ff2"):
            value = value.reshape(N_DEV, -1, d)
        return send_partials(name, value)

    grads = _local_step(x2d, u, tgt, norm1_g, w_main, w_alow, wgu_pad, b_gate, convw_taps, conv_norm_g, gla_norm_g,
                        norm2_g, norm_f_g, get_w_out, get_w1, get_w2, on_grad)
    grad_x = grads["x"]

    small_shapes = [(1, d), (1, d_k), (1, d_conv), (1, gla_norm_g.shape[1]), (1, d), (d,),
                    (GATE_RANK, d_k), (d_conv, CONV_WIDTH), (1,)]
    small_grad_rows = 152
    small_part = _pack_rows(
        [grads["norm1_g"], grads["b_gate"], grads["conv_norm_g"], grads["gla_norm_g"], grads["norm2_g"],
         grads["norm_f_g"], grads["w_gate_up"][:GATE_RANK], grads["conv_w"][:CONV_WIDTH].T, grads["loss"][0, 0]],
        small_grad_rows)
    send_partials("small", jnp.broadcast_to(small_part[None], (N_DEV, small_grad_rows, LANES)))

    def received(name):
        send, recv, src, land = in_flight[name]
        return _exchange_wait("scatter_wait_" + name, send, recv, src, land, grad_x, scatter=True)

    small_r = received("small")
    gin_r, gout_r, g1_r, g2_r = received("w_in"), received("w_out"), received("w_ff1"), received("w_ff2")
    return _update(me, gin_r, gout_r, g1_r, g2_r, small_r, small_shapes, grad_x, wmv, packed_small)


def _local_step(x2d, u, tgt, norm1_g, w_main, w_alow, wgu_pad, b_gate, convw_taps, conv_norm_g, gla_norm_g,
                norm2_g, norm_f_g, get_w_out, get_w1, get_w2, on_grad):
    t, d = x2d.shape
    n_main = w_main.shape[1]

    z, alow = _inproj(u, w_main, w_alow)
    y, sall = _mixer_fwd(z, alow, wgu_pad, b_gate, convw_taps, conv_norm_g, gla_norm_g)
    w_out_full = get_w_out(y)
    x1, h = _outproj(y, w_out_full, x2d, norm2_g)
    w1g = get_w1(h)
    a = _ff1(h, w1g)
    w2_full = get_w2(a)
    d_ff = w2_full.shape[0]
    x2 = _ff2(a, w2_full, x1)
    dx2, dx2b, loss_part, d_normf = _loss_head(x2, norm_f_g.reshape(1, d), tgt)

    tk = min(4096, t)
    nk = t // tk
    da = _dff2(dx2b, w2_full, a)
    dw2 = _tn_matmul(
        "dw_ff2", a, dx2b, (d_ff // 1024, d // 1024, nk),
        pl.BlockSpec((tk, 1024), lambda m, j, kk: (kk, m)), pl.BlockSpec((tk, 1024), lambda m, j, kk: (kk, j)),
        jax.ShapeDtypeStruct((d_ff, d), BF16), pl.BlockSpec((1024, 1024), lambda m, j, kk: (m, j)), (1024, 1024),
        a_fn=_relu_sq)
    token = on_grad("w_ff2", dw2)
    f_shard = d_ff // N_DEV
    dw1 = _tn_matmul(
        "dw_ff1", h, da, (N_DEV, d // 1024, nk),
        pl.BlockSpec((tk, 1024), lambda g, m, kk: (kk, m)), pl.BlockSpec((tk, f_shard), lambda g, m, kk: (kk, g)),
        jax.ShapeDtypeStruct((N_DEV, d, f_shard), BF16), pl.BlockSpec((None, 1024, f_shard), lambda g, m, kk: (g, m, 0)),
        (1024, f_shard), behind=token)
    token = on_grad("w_ff1", dw1)
    dh = _dh(da, w1g, behind=token)
    dx1, dx1b, d_norm2 = _norm_bwd("norm2_bwd", dh, x1, norm2_g, dx2, with_bf16=True)
    dy = _nt_matmul("dy", dx1b, w_out_full)
    dwout = _tn_matmul(
        "dw_out", y, dx1b, (d // 1024, d // 1024, nk),
        pl.BlockSpec((tk, 1024), lambda m, j, kk: (kk, m)), pl.BlockSpec((tk, 1024), lambda m, j, kk: (kk, j)),
        jax.ShapeDtypeStruct((d, d), BF16), pl.BlockSpec((1024, 1024), lambda m, j, kk: (m, j)), (1024, 1024))
    token = on_grad("w_out", dwout)
    dz, dzal, d_convw, d_convg, d_glag, d_bgate, d_wgu = _mixer_bwd(
        z, alow, dy, sall, wgu_pad, b_gate, convw_taps, conv_norm_g, gla_norm_g, behind=token)
    dwin_main = _tn_matmul(
        "dw_in", u, dz, (d // 1024, n_main // 1024, nk),
        pl.BlockSpec((tk, 1024), lambda m, j, kk: (kk, m)), pl.BlockSpec((tk, 1024), lambda m, j, kk: (kk, j)),
        jax.ShapeDtypeStruct((d, n_main), BF16), pl.BlockSpec((1024, 1024), lambda m, j, kk: (m, j)), (1024, 1024))
    dwin_alow = _tn_matmul(
        "dw_in_alow", u, dzal, (d // 1024, 1, nk),
        pl.BlockSpec((tk, 1024), lambda m, j, kk: (kk, m)), pl.BlockSpec((tk, LANES), lambda m, j, kk: (kk, 0)),
        jax.ShapeDtypeStruct((d, LANES), BF16), pl.BlockSpec((1024, LANES), lambda m, j, kk: (m, 0)), (1024, LANES))
    token = on_grad("w_in", (dwin_main, dwin_alow))
    du = _du(dz, w_main, dzal, w_alow, behind=token)
    grad_x, d_norm1 = _norm_bwd("norm1_bwd", du, x2d, norm1_g, dx1, with_bf16=False)
    return dict(x=grad_x, loss=loss_part, norm1_g=d_norm1, w_gate_up=d_wgu, b_gate=d_bgate, conv_w=d_convw,
                conv_norm_g=d_convg, gla_norm_g=d_glag, norm2_g=d_norm2, norm_f_g=d_normf)


_WEIGHT_ORDER = ("norm1_g", "w_in", "w_gate_up", "b_gate", "conv_w", "conv_norm_g", "gla_norm_g", "w_out", "norm2_g",
                 "w_ff1", "w_ff2", "norm_f_g")
_SMALL_ORDER = ("norm1_g", "b_gate", "conv_norm_g", "gla_norm_g", "norm2_g", "norm_f_g", "w_gate_up", "conv_w")
SMALL_LOCAL_ROWS = 80
_LOSS_SLOT = (0.0, 0.0, 1.0)


def _update(me, gin_r, gout_r, g1_r, g2_r, small_r, small_shapes, grad_x, wmv, packed_small):
    big = {
        "w_in": _adamw("adamw_w_in", gin_r, *(a[0] for a in wmv["w_in"]), 256),
        "w_out": _adamw("adamw_w_out", gout_r, *(a[0] for a in wmv["w_out"]), 128),
        "w_ff1": _adamw("adamw_w_ff1", g1_r, *(a[0] for a in wmv["w_ff1"]), 256),
        "w_ff2": _adamw("adamw_w_ff2", g2_r, *(a[0] for a in wmv["w_ff2"]), 128),
    }

    wgu_cols = wmv["w_gate_up"][0].shape[2]
    cw_rows = wmv["conv_w"][0].shape[1]

    summed = _unpack_rows(_sum_partials(small_r), small_shapes)
    summed[6] = lax.dynamic_slice_in_dim(summed[6], me * wgu_cols, wgu_cols, axis=1)
    summed[7] = lax.dynamic_slice_in_dim(summed[7], me * cw_rows, cw_rows, axis=0)
    local_shapes = small_shapes[:6] + [(GATE_RANK, wgu_cols), (cw_rows, CONV_WIDTH), (1,)]
    out_small = _adamw("adamw_small", _pack_rows(summed, SMALL_LOCAL_ROWS)[None], *packed_small, SMALL_LOCAL_ROWS)
    unpacked = [_unpack_rows(o, local_shapes) for o in out_small]

    outs = []
    for k in range(4):
        for nm in _WEIGHT_ORDER:
            if nm in big:
                outs.append(big[nm][k][None])
            else:
                val = unpacked[k][_SMALL_ORDER.index(nm)]
                outs.append(val.reshape(wmv[nm][0].shape))
    loss = unpacked[0][8][0]
    return (loss, grad_x[None], *outs)
```

```python
import functools

import jax
import jax.numpy as jnp
from jax import lax
from jax.experimental import pallas as pl
from jax.experimental.pallas import tpu as pltpu

F32 = jnp.float32
BF16 = jnp.bfloat16

N_DEV = 8
CHUNK = 64
GLA_HEADS = 4
CONV_GROUPS = 8
CONV_WIDTH = 3
GATE_RANK = 16
GATE_NORMALIZER = 16.0
EPS = 1e-6
ADAM_LR = 0.001
ADAM_B1 = 0.9
ADAM_B2 = 0.999
ADAM_EPS = 1e-08
ADAM_WD = 0.01
ADAM_STEP = 10

LANES = 128
SUBLANES = 8
VMEM_LIMIT = 56 << 20

_NN = (((1,), (0,)), ((), ()))
_NT = (((1,), (1,)), ((), ()))
_TN = (((0,), (0,)), ((), ()))


def _dot(a, b, dims=_NN):
    return lax.dot_general(a, b, dims, preferred_element_type=F32)


def _params(n_grid):
    return pltpu.CompilerParams(dimension_semantics=("arbitrary",) * n_grid, vmem_limit_bytes=VMEM_LIMIT)


def _relu_sq(a):
    r = jnp.maximum(a, 0.0)
    return r * r


def _device_index():
    return 4 * lax.axis_index("x") + 2 * lax.axis_index("y") + lax.axis_index("c")


def _peer(mask):
    x, y, c = lax.axis_index("x"), lax.axis_index("y"), lax.axis_index("c")
    return (x ^ ((mask >> 2) & 1), y ^ ((mask >> 1) & 1), c ^ (mask & 1))


_HBM_SPEC = pl.BlockSpec(memory_space=pltpu.HBM)
_SEM_SPEC = pl.BlockSpec(memory_space=pltpu.SEMAPHORE)
_SIDE_EFFECT = pltpu.SideEffectType.DATAFLOW_SIDE_EFFECTING
N_PEERS = N_DEV - 1


def _exchange_copy(src_ref, land_ref, send_sems, recv_sems, mask, scatter, arriving):
    me = _device_index()
    src = src_ref.at[me ^ mask] if scatter else src_ref
    dst = land_ref.at[(me ^ mask) if arriving else me]
    return pltpu.make_async_remote_copy(
        src_ref=src, dst_ref=dst, send_sem=send_sems.at[mask - 1], recv_sem=recv_sems.at[mask - 1],
        device_id=_peer(mask), device_id_type=pl.DeviceIdType.MESH)


def _land_zone(own):
    zone = lax.empty((N_DEV,) + own.shape, own.dtype)
    return lax.dynamic_update_slice(zone, own[None], (_device_index(),) + (0,) * own.ndim)


ALL_PEERS = tuple(range(1, N_DEV))
SIBLING = 1
SAME_CORE_PEERS = (2, 4, 6)


def _exchange_start(name, srcs, lands, scatter, masks=None, behind=None):
    n = len(srcs)
    masks = masks or [ALL_PEERS] * n
    dep_args = [] if behind is None else [behind]

    def body(*refs):
        src, land = refs[:n], refs[n:2 * n]
        outs = refs[2 * n + len(dep_args):]
        send_sems, recv_sems = outs[:n], outs[n:2 * n]
        token = refs[-1]
        for a in range(n):
            for mask in masks[a]:
                _exchange_copy(src[a], land[a], send_sems[a], recv_sems[a], mask, scatter, False).start()
        token[...] = jnp.zeros_like(token)

    hbm = lambda a: pltpu.HBM(a.shape, a.dtype)
    outs = pl.pallas_call(
        body, name=name,
        out_shape=([pltpu.SemaphoreType.DMA((N_PEERS,))] * (2 * n) + [hbm(a) for a in srcs] + [hbm(a) for a in lands]
                   + [jax.ShapeDtypeStruct((SUBLANES, LANES), F32)]),
        in_specs=[_HBM_SPEC] * (2 * n) + [pl.BlockSpec(memory_space=pl.ANY)] * len(dep_args),
        out_specs=[_SEM_SPEC] * (2 * n) + [_HBM_SPEC] * (2 * n) + [pl.BlockSpec(memory_space=pltpu.VMEM)],
        input_output_aliases={a: 2 * n + a for a in range(2 * n)},
        compiler_params=pltpu.CompilerParams(has_side_effects=_SIDE_EFFECT),
    )(*[pltpu.with_memory_space_constraint(a, pltpu.HBM) for a in list(srcs) + list(lands)], *dep_args)
    send_sems, recv_sems = outs[:n], outs[n:2 * n]
    src_thru, land_thru = outs[2 * n:3 * n], outs[3 * n:4 * n]
    return send_sems, recv_sems, src_thru, land_thru, outs[-1]


def _exchange_wait(name, send_sems, recv_sems, src_thru, land_thru, after, scatter, masks=ALL_PEERS):
    after = list(after) if isinstance(after, (list, tuple)) else [after]

    def body(src_ref, land_ref, send_ref, recv_ref, *rest):
        for mask in masks:
            cp = _exchange_copy(src_ref, land_ref, send_ref, recv_ref, mask, scatter, True)
            cp.wait_send()
            cp.wait_recv()

    return pl.pallas_call(
        body, name=name,
        out_shape=(pltpu.HBM(src_thru.shape, src_thru.dtype), pltpu.HBM(land_thru.shape, land_thru.dtype)),
        in_specs=[_HBM_SPEC, _HBM_SPEC, _SEM_SPEC, _SEM_SPEC] + [pl.BlockSpec(memory_space=pl.ANY)] * len(after),
        out_specs=(_HBM_SPEC, _HBM_SPEC), input_output_aliases={0: 0, 1: 1},
        compiler_params=pltpu.CompilerParams(has_side_effects=_SIDE_EFFECT),
    )(src_thru, land_thru, send_sems, recv_sems, *after)[1]


def _forward_copy(land_ref, send_sems, recv_sems, k, arriving):
    me = _device_index()
    slot = me ^ SAME_CORE_PEERS[k]
    return pltpu.make_async_remote_copy(
        src_ref=land_ref.at[slot], dst_ref=land_ref.at[(slot ^ SIBLING) if arriving else slot],
        send_sem=send_sems.at[k], recv_sem=recv_sems.at[k],
        device_id=_peer(SIBLING), device_id_type=pl.DeviceIdType.MESH)


def _forward_start(name, land):
    n_fwd = len(SAME_CORE_PEERS)

    def body(land_ref, send_sems, recv_sems, land_thru):
        for k in range(n_fwd):
            _forward_copy(land_ref, send_sems, recv_sems, k, False).start()

    send, recv, thru = pl.pallas_call(
        body, name=name,
        out_shape=[pltpu.SemaphoreType.DMA((n_fwd,)), pltpu.SemaphoreType.DMA((n_fwd,)), pltpu.HBM(land.shape, land.dtype)],
        in_specs=[_HBM_SPEC], out_specs=[_SEM_SPEC, _SEM_SPEC, _HBM_SPEC], input_output_aliases={0: 2},
        compiler_params=pltpu.CompilerParams(has_side_effects=_SIDE_EFFECT),
    )(pltpu.with_memory_space_constraint(land, pltpu.HBM))
    return send, recv, thru


def _forward_wait(name, send_sems, recv_sems, land_thru):
    def body(land_ref, send_ref, recv_ref, got_ref):
        for k in range(len(SAME_CORE_PEERS)):
            cp = _forward_copy(land_ref, send_ref, recv_ref, k, True)
            cp.wait_send()
            cp.wait_recv()

    return pl.pallas_call(
        body, name=name, out_shape=pltpu.HBM(land_thru.shape, land_thru.dtype),
        in_specs=[_HBM_SPEC, _SEM_SPEC, _SEM_SPEC], out_specs=_HBM_SPEC, input_output_aliases={0: 0},
        compiler_params=pltpu.CompilerParams(has_side_effects=_SIDE_EFFECT),
    )(land_thru, send_sems, recv_sems)


def _shards_to_columns(g, n_main, tr=256):
    n_dev, d, s = g.shape

    def body(g_ref, main_ref, rest_ref):
        for j in range(n_dev):
            lo, hi = j * s, (j + 1) * s
            if hi <= n_main:
                main_ref[:, lo:hi] = g_ref[j]
            else:
                main_ref[:, lo:n_main] = g_ref[j, :, 0:n_main - lo]
                rest_ref[...] = jnp.zeros_like(rest_ref)
                rest_ref[:, 0:hi - n_main] = g_ref[j, :, n_main - lo:s]

    return pl.pallas_call(
        body, grid=(d // tr,), name="shards_to_columns",
        in_specs=[pl.BlockSpec((n_dev, tr, s), lambda i: (0, i, 0))],
        out_specs=[pl.BlockSpec((tr, n_main), lambda i: (i, 0)), pl.BlockSpec((tr, LANES), lambda i: (i, 0))],
        out_shape=[jax.ShapeDtypeStruct((d, n_main), g.dtype), jax.ShapeDtypeStruct((d, LANES), g.dtype)],
        compiler_params=_params(1),
    )(g)


def _columns_to_shards(main, rest, n_dev, s, tr=256):
    d, n_main = main.shape
    assert (n_dev - 1) * s <= n_main < n_dev * s

    def body(main_ref, rest_ref, o_ref):
        for j in range(n_dev):
            lo, hi = j * s, (j + 1) * s
            if hi <= n_main:
                o_ref[j] = main_ref[:, lo:hi]
            else:
                o_ref[j, :, 0:n_main - lo] = main_ref[:, lo:n_main]
                o_ref[j, :, n_main - lo:s] = rest_ref[:, 0:hi - n_main]

    return pl.pallas_call(
        body, grid=(d // tr,), name="columns_to_shards",
        in_specs=[pl.BlockSpec((tr, n_main), lambda i: (i, 0)), pl.BlockSpec((tr, LANES), lambda i: (i, 0))],
        out_specs=pl.BlockSpec((n_dev, tr, s), lambda i: (0, i, 0)),
        out_shape=jax.ShapeDtypeStruct((n_dev, d, s), main.dtype),
        compiler_params=_params(1),
    )(main, rest)


def _rmsnorm(x, g, tr=512, behind=None):
    t, d = x.shape
    tr = min(tr, t)
    dep_args, dep_specs = _behind(behind)

    def body(x_ref, g_ref, *rest):
        u_ref = rest[-1]
        xf = x_ref[...]
        r = lax.rsqrt(jnp.mean(xf * xf, axis=-1, keepdims=True) + EPS)
        u_ref[...] = (xf * r * g_ref[...]).astype(BF16)

    return pl.pallas_call(
        body, name="rmsnorm1", grid=(t // tr,),
        in_specs=[pl.BlockSpec((tr, d), lambda i: (i, 0)), pl.BlockSpec((1, d), lambda i: (0, 0))] + dep_specs,
        out_specs=pl.BlockSpec((tr, d), lambda i: (i, 0)),
        out_shape=jax.ShapeDtypeStruct((t, d), BF16),
        compiler_params=_params(1),
    )(x, g, *dep_args)


def _inproj(u, w_main, w_alow, tm=1024, tn=1024):
    t, d = u.shape
    tm = min(tm, t)
    n = w_main.shape[1]

    def body(u_ref, w_ref, wa_ref, z_ref, al_ref):
        @pl.when(pl.program_id(1) == 0)
        def _():
            al_ref[...] = _dot(u_ref[...], wa_ref[...])

        z_ref[...] = _dot(u_ref[...], w_ref[...])

    return pl.pallas_call(
        body, name="inproj", grid=(t // tm, n // tn),
        in_specs=[pl.BlockSpec((tm, d), lambda m, j: (m, 0)), pl.BlockSpec((d, tn), lambda m, j: (0, j)),
                  pl.BlockSpec((d, LANES), lambda m, j: (0, 0))],
        out_specs=[pl.BlockSpec((tm, tn), lambda m, j: (m, j)), pl.BlockSpec((tm, LANES), lambda m, j: (m, 0))],
        out_shape=[jax.ShapeDtypeStruct((t, n), F32), jax.ShapeDtypeStruct((t, LANES), F32)],
        compiler_params=_params(2),
    )(u, w_main, w_alow)


def _outproj(y, w_out, x, g2, tm=512):
    t, d = x.shape
    tm = min(tm, t)
    k = y.shape[1]

    def body(y_ref, w_ref, x_ref, g_ref, x1_ref, h_ref):
        x1 = x_ref[...] + _dot(y_ref[...], w_ref[...])
        x1_ref[...] = x1
        r = lax.rsqrt(jnp.mean(x1 * x1, axis=-1, keepdims=True) + EPS)
        h_ref[...] = (x1 * r * g_ref[...]).astype(BF16)

    return pl.pallas_call(
        body, name="outproj_rmsnorm", grid=(t // tm,),
        in_specs=[pl.BlockSpec((tm, k), lambda m: (m, 0)), pl.BlockSpec((k, d), lambda m: (0, 0)),
                  pl.BlockSpec((tm, d), lambda m: (m, 0)), pl.BlockSpec((1, d), lambda m: (0, 0))],
        out_specs=[pl.BlockSpec((tm, d), lambda m: (m, 0)), pl.BlockSpec((tm, d), lambda m: (m, 0))],
        out_shape=[jax.ShapeDtypeStruct((t, d), F32), jax.ShapeDtypeStruct((t, d), BF16)],
        compiler_params=_params(1),
    )(y, w_out, x, g2)


def _ff1(h, w1g, tm=1024):
    t, d = h.shape
    tm = min(tm, t)
    g, _, f = w1g.shape

    def body(h_ref, w_ref, a_ref):
        a_ref[...] = _dot(h_ref[...], w_ref[...]).astype(BF16)

    return pl.pallas_call(
        body, name="ff1", grid=(t // tm, g),
        in_specs=[pl.BlockSpec((tm, d), lambda m, j: (m, 0)), pl.BlockSpec((None, d, f), lambda m, j: (j, 0, 0))],
        out_specs=pl.BlockSpec((tm, f), lambda m, j: (m, j)),
        out_shape=jax.ShapeDtypeStruct((t, g * f), BF16),
        compiler_params=_params(2),
    )(h, w1g)


def _ff2(a, w2, x1, tm=1024, tn=1024, tk=2048):
    t, f = a.shape
    tm = min(tm, t)
    d = w2.shape[1]

    def body(a_ref, w_ref, x1_ref, o_ref):
        @pl.when(pl.program_id(2) == 0)
        def _():
            o_ref[...] = x1_ref[...]

        o_ref[...] += _dot(_relu_sq(a_ref[...]), w_ref[...])

    return pl.pallas_call(
        body, name="ff2_residual", grid=(t // tm, d // tn, f // tk),
        in_specs=[pl.BlockSpec((tm, tk), lambda m, j, kk: (m, kk)), pl.BlockSpec((tk, tn), lambda m, j, kk: (kk, j)),
                  pl.BlockSpec((tm, tn), lambda m, j, kk: (m, j))],
        out_specs=pl.BlockSpec((tm, tn), lambda m, j, kk: (m, j)),
        out_shape=jax.ShapeDtypeStruct((t, d), F32),
        compiler_params=_params(3),
    )(a, w2, x1)


def _dff2(dx2b, w2, a, tm=1024, tn=1024):
    t, d = dx2b.shape
    tm = min(tm, t)
    f = w2.shape[0]

    def body(g_ref, w_ref, a_ref, o_ref):
        dp = _dot(g_ref[...], w_ref[...], _NT)
        o_ref[...] = (dp * (2.0 * jnp.maximum(a_ref[...].astype(F32), 0.0))).astype(BF16)

    return pl.pallas_call(
        body, name="dff2", grid=(t // tm, f // tn),
        in_specs=[pl.BlockSpec((tm, d), lambda m, j: (m, 0)), pl.BlockSpec((tn, d), lambda m, j: (j, 0)),
                  pl.BlockSpec((tm, tn), lambda m, j: (m, j))],
        out_specs=pl.BlockSpec((tm, tn), lambda m, j: (m, j)),
        out_shape=jax.ShapeDtypeStruct((t, f), BF16),
        compiler_params=_params(2),
    )(dx2b, w2, a)


def _behind(token):
    if token is None:
        return [], []
    return [token], [pl.BlockSpec(token.shape, lambda *_: (0,) * token.ndim)]


def _tn_matmul(name, a, b, grid, a_spec, b_spec, out_shape, out_spec, acc_shape, a_fn=None, behind=None):
    nk = grid[-1]
    dep_args, dep_specs = _behind(behind)

    def body(a_ref, b_ref, *rest):
        o_ref, acc_ref = rest[-2:]
        kk = pl.program_id(len(grid) - 1)
        av = a_ref[...]
        if a_fn is not None:
            av = a_fn(av)
        part = _dot(av, b_ref[...], _TN)

        @pl.when(kk == 0)
        def _():
            acc_ref[...] = part

        @pl.when(kk > 0)
        def _():
            acc_ref[...] += part

        @pl.when(kk == nk - 1)
        def _():
            o_ref[...] = acc_ref[...].astype(o_ref.dtype)

    return pl.pallas_call(
        body, name=name, grid=grid, in_specs=[a_spec, b_spec] + dep_specs, out_specs=out_spec, out_shape=out_shape,
        scratch_shapes=[pltpu.VMEM(acc_shape, F32)], compiler_params=_params(len(grid)),
    )(a, b, *dep_args)


def _dh(da, w1g, tm=1024, tn=1024, shards_per_step=4, behind=None):
    t = da.shape[0]
    tm = min(tm, t)
    g, d, f = w1g.shape
    sps = shards_per_step
    dep_args, dep_specs = _behind(behind)

    def body(a_ref, w_ref, *rest):
        o_ref = rest[-1]
        acc = _dot(a_ref[:, 0:f], w_ref[0], _NT)
        for s in range(1, sps):
            acc = acc + _dot(a_ref[:, s * f:(s + 1) * f], w_ref[s], _NT)

        @pl.when(pl.program_id(2) == 0)
        def _():
            o_ref[...] = acc

        @pl.when(pl.program_id(2) > 0)
        def _():
            o_ref[...] += acc

    return pl.pallas_call(
        body, name="dh", grid=(t // tm, d // tn, g // sps),
        in_specs=[pl.BlockSpec((tm, sps * f), lambda m, j, kk: (m, kk)),
                  pl.BlockSpec((sps, tn, f), lambda m, j, kk: (kk, j, 0))] + dep_specs,
        out_specs=pl.BlockSpec((tm, tn), lambda m, j, kk: (m, j)),
        out_shape=jax.ShapeDtypeStruct((t, d), F32),
        compiler_params=_params(3),
    )(da, w1g, *dep_args)


def _nt_matmul(name, a, b, tm=1024, tn=1024):
    t, k = a.shape
    tm = min(tm, t)
    n = b.shape[0]

    def body(a_ref, b_ref, o_ref):
        o_ref[...] = _dot(a_ref[...], b_ref[...], _NT)

    return pl.pallas_call(
        body, name=name, grid=(t // tm, n // tn),
        in_specs=[pl.BlockSpec((tm, k), lambda m, j: (m, 0)), pl.BlockSpec((tn, k), lambda m, j: (j, 0))],
        out_specs=pl.BlockSpec((tm, tn), lambda m, j: (m, j)),
        out_shape=jax.ShapeDtypeStruct((t, n), F32),
        compiler_params=_params(2),
    )(a, b)


def _du(dz, w_main, dzal, w_alow, tm=1024, tn=1024, tk=3072, behind=None):
    t, n = dz.shape
    tm = min(tm, t)
    d = w_main.shape[0]
    dep_args, dep_specs = _behind(behind)

    def body(a_ref, w_ref, al_ref, wa_ref, *rest):
        o_ref = rest[-1]

        @pl.when(pl.program_id(2) == 0)
        def _():
            o_ref[...] = _dot(al_ref[...], wa_ref[...], _NT)

        o_ref[...] += _dot(a_ref[...], w_ref[...], _NT)

    return pl.pallas_call(
        body, name="du", grid=(t // tm, d // tn, n // tk),
        in_specs=[pl.BlockSpec((tm, tk), lambda m, j, kk: (m, kk)), pl.BlockSpec((tn, tk), lambda m, j, kk: (j, kk)),
                  pl.BlockSpec((tm, LANES), lambda m, j, kk: (m, 0)), pl.BlockSpec((tn, LANES), lambda m, j, kk: (j, 0))]
        + dep_specs,
        out_specs=pl.BlockSpec((tm, tn), lambda m, j, kk: (m, j)),
        out_shape=jax.ShapeDtypeStruct((t, d), F32),
        compiler_params=_params(3),
    )(dz, w_main, dzal, w_alow, *dep_args)


def _loss_head(x2, gf, tgt, tr=256):
    t, d = x2.shape

    def body(x_ref, g_ref, t_ref, dx_ref, dxb_ref, loss_ref, dg_ref):
        @pl.when(pl.program_id(0) == 0)
        def _():
            loss_ref[...] = jnp.zeros_like(loss_ref)
            dg_ref[...] = jnp.zeros_like(dg_ref)

        xf = x_ref[...]
        g = g_ref[...]
        r = lax.rsqrt(jnp.mean(xf * xf, axis=-1, keepdims=True) + EPS)
        xh = xf * r
        e = xh * g - t_ref[...]
        loss_ref[...] += 0.5 * jnp.sum(jnp.mean(e * e, axis=-1, keepdims=True))
        dy = e * (1.0 / d)
        dg_ref[...] += jnp.sum(dy * xh, axis=0, keepdims=True)
        dyg = dy * g
        dx = r * (dyg - xh * jnp.mean(dyg * xh, axis=-1, keepdims=True))
        dx_ref[...] = dx
        dxb_ref[...] = dx.astype(BF16)

    return pl.pallas_call(
        body, name="loss_head", grid=(t // tr,),
        in_specs=[pl.BlockSpec((tr, d), lambda i: (i, 0)), pl.BlockSpec((1, d), lambda i: (0, 0)),
                  pl.BlockSpec((tr, d), lambda i: (i, 0))],
        out_specs=[pl.BlockSpec((tr, d), lambda i: (i, 0)), pl.BlockSpec((tr, d), lambda i: (i, 0)),
                   pl.BlockSpec((SUBLANES, LANES), lambda i: (0, 0)), pl.BlockSpec((1, d), lambda i: (0, 0))],
        out_shape=[jax.ShapeDtypeStruct((t, d), F32), jax.ShapeDtypeStruct((t, d), BF16),
                   jax.ShapeDtypeStruct((SUBLANES, LANES), F32), jax.ShapeDtypeStruct((1, d), F32)],
        compiler_params=_params(1),
    )(x2, gf, tgt)


def _norm_bwd(name, dh, xin, g, dres, with_bf16, tr=256):
    t, d = xin.shape

    def body(dh_ref, x_ref, g_ref, dr_ref, dx_ref, *rest):
        dg_ref = rest[-1]

        @pl.when(pl.program_id(0) == 0)
        def _():
            dg_ref[...] = jnp.zeros_like(dg_ref)

        xf = x_ref[...]
        dhv = dh_ref[...]
        r = lax.rsqrt(jnp.mean(xf * xf, axis=-1, keepdims=True) + EPS)
        xh = xf * r
        dg_ref[...] += jnp.sum(dhv * xh, axis=0, keepdims=True)
        dyg = dhv * g_ref[...]
        dx = dr_ref[...] + r * (dyg - xh * jnp.mean(dyg * xh, axis=-1, keepdims=True))
        dx_ref[...] = dx
        if with_bf16:
            rest[0][...] = dx.astype(BF16)

    rows = pl.BlockSpec((tr, d), lambda i: (i, 0))
    vec = pl.BlockSpec((1, d), lambda i: (0, 0))
    return pl.pallas_call(
        body, name=name, grid=(t // tr,),
        in_specs=[rows, rows, vec, rows],
        out_specs=[rows] + [rows] * with_bf16 + [vec],
        out_shape=[jax.ShapeDtypeStruct((t, d), F32)] + [jax.ShapeDtypeStruct((t, d), BF16)] * with_bf16
        + [jax.ShapeDtypeStruct((1, d), F32)],
        compiler_params=_params(1),
    )(dh, xin, g, dres)


MIX_TILE = 256
CHUNKS_PER_TILE = MIX_TILE // CHUNK
CHUNK_SHIFT = CHUNK.bit_length() - 1
assert 1 << CHUNK_SHIFT == CHUNK


def _chunk_masks(n):
    row = lax.broadcasted_iota(jnp.int32, (n, n), 0)
    col = lax.broadcasted_iota(jnp.int32, (n, n), 1)
    same = lax.shift_right_logical(row, CHUNK_SHIFT) == lax.shift_right_logical(col, CHUNK_SHIFT)
    one = lambda m: jnp.where(m, 1.0, 0.0).astype(BF16)
    return one(same & (col > row)), one(same), one(same & (col < row))


def _mask_dot(mask, x):
    hi = x.astype(BF16)
    r1 = x - hi.astype(F32)
    mid = r1.astype(BF16)
    lo = (r1 - mid.astype(F32)).astype(BF16)
    return _dot(mask, hi) + _dot(mask, mid) + _dot(mask, lo)


def _log_sigmoid(x):
    return jnp.minimum(x, 0.0) - jnp.log1p(jnp.exp(-jnp.abs(x)))


def _conv_taps(prev8, uc, w):
    ext = jnp.concatenate([prev8, uc], axis=0)
    s1 = pltpu.roll(ext, 1, 0)[SUBLANES:]
    s2 = pltpu.roll(ext, 2, 0)[SUBLANES:]
    return s2 * w[0:1] + s1 * w[1:2] + uc * w[2:3], s1, s2


def _z_specs(tile, idx):
    d_conv = 1024
    wide = lambda c: pl.BlockSpec((tile, d_conv), lambda i, c=c: (idx(i), c))
    half = lambda c: pl.BlockSpec((tile, d_conv // 2), lambda i, c=c: (idx(i), c))
    return [wide(0), wide(1), wide(2), half(6), half(7), wide(4), wide(5)]


def _mixer_fwd(z, alow, wgu, b_gate, convw, conv_g, gla_g):
    t = z.shape[0]
    tb, cpt = MIX_TILE, CHUNKS_PER_TILE
    d_conv = conv_g.shape[1]
    dv = gla_g.shape[1]
    dk = dv // 2
    d_k = GLA_HEADS * dk
    gw = d_conv // CONV_GROUPS
    scale = dk ** -0.5

    def body(cb_ref, cc_ref, ch_ref, q_ref, k_ref, v_ref, og_ref, al_ref, wgu_ref, bg_ref, cw_ref, cg_ref, gg_ref,
             y_ref, sall_ref, carry_ref, s_ref):
        @pl.when(pl.program_id(0) == 0)
        def _():
            carry_ref[...] = jnp.zeros_like(carry_ref)
            s_ref[...] = jnp.zeros_like(s_ref)

        uc = cc_ref[...] * ch_ref[...]
        conv, _, _ = _conv_taps(carry_ref[...], uc, cw_ref[...])
        carry_ref[...] = uc[tb - SUBLANES:]
        ypre = cb_ref[...] * conv
        cg = cg_ref[...]
        for g in range(CONV_GROUPS):
            sl = slice(g * gw, (g + 1) * gw)
            seg = ypre[:, sl]
            r = lax.rsqrt(jnp.mean(seg * seg, axis=-1, keepdims=True) + EPS)
            y_ref[:, sl] = (seg * r * cg[:, sl]).astype(BF16)

        later, same, _ = _chunk_masks(tb)
        pre = _dot(al_ref[...].astype(BF16), wgu_ref[...]) + bg_ref[...]
        la = _log_sigmoid(pre) * (1.0 / GATE_NORMALIZER)
        e_dec = _mask_dot(later, la)
        dec_all = jnp.exp(_mask_dot(same, la))
        kdec = (k_ref[...] * jnp.exp(e_dec)).astype(BF16)
        qs = (q_ref[...] * scale).astype(BF16)
        vb = v_ref[...].astype(BF16)
        gg = gg_ref[...]
        rows = [slice(c * CHUNK, (c + 1) * CHUNK) for c in range(cpt)]
        ks = [slice(h * dk, (h + 1) * dk) for h in range(GLA_HEADS)]
        vs = [slice(h * dv, (h + 1) * dv) for h in range(GLA_HEADS)]
        kvt = [[_dot(vb[rows[c], vs[h]], kdec[rows[c], ks[h]], _TN) for h in range(GLA_HEADS)] for c in range(cpt)]
        state = [s_ref[h] for h in range(GLA_HEADS)]
        states = []
        for c in range(cpt):
            state = [state[h] * dec_all[c * CHUNK:c * CHUNK + 1, ks[h]] + kvt[c][h] for h in range(GLA_HEADS)]
            states.append(state)
            for h in range(GLA_HEADS):
                sall_ref[c, h] = state[h]
        for h in range(GLA_HEADS):
            s_ref[h] = state[h]
        for h in range(GLA_HEADS):
            o = jnp.concatenate(
                [_dot(qs[rows[c], ks[h]], states[c][h].astype(BF16), _NT) for c in range(cpt)], axis=0)
            ro = lax.rsqrt(jnp.mean(o * o, axis=-1, keepdims=True) + EPS)
            ogs = og_ref[:, vs[h]]
            yg = o * ro * gg * (ogs * jax.nn.sigmoid(ogs))
            y_ref[:, d_conv + h * dv:d_conv + (h + 1) * dv] = yg.astype(BF16)

    full = lambda shape: pl.BlockSpec(shape, lambda i: (0,) * len(shape))
    return pl.pallas_call(
        body, name="mixer_fwd", grid=(t // tb,),
        in_specs=_z_specs(tb, lambda i: i) + [
            pl.BlockSpec((tb, LANES), lambda i: (i, 0)), full(wgu.shape), full(b_gate.shape), full(convw.shape),
            full(conv_g.shape), full(gla_g.shape)],
        out_specs=[pl.BlockSpec((tb, d_conv + GLA_HEADS * dv), lambda i: (i, 0)),
                   pl.BlockSpec((cpt, GLA_HEADS, dv, dk), lambda i: (i, 0, 0, 0))],
        out_shape=[jax.ShapeDtypeStruct((t, d_conv + GLA_HEADS * dv), BF16),
                   jax.ShapeDtypeStruct((t // CHUNK, GLA_HEADS, dv, dk), F32)],
        scratch_shapes=[pltpu.VMEM((SUBLANES, d_conv), F32), pltpu.VMEM((GLA_HEADS, dv, dk), F32)],
        compiler_params=_params(1),
    )(z, z, z, z, z, z, z, alow, wgu, b_gate, convw, conv_g, gla_g)


def _mixer_bwd(z, alow, dy, sall, wgu, b_gate, convw, conv_g, gla_g, behind=None):
    t = z.shape[0]
    tb, cpt = MIX_TILE, CHUNKS_PER_TILE
    nt = t // tb
    d_conv = conv_g.shape[1]
    dv = gla_g.shape[1]
    dk = dv // 2
    d_k = GLA_HEADS * dk
    gw = d_conv // CONV_GROUPS
    scale = dk ** -0.5
    rev = lambda i: nt - 1 - i
    dep_args, dep_specs = _behind(behind)

    def body(cb_ref, cc_ref, ch_ref, q_ref, k_ref, v_ref, og_ref, ccp_ref, chp_ref, al_ref, dy_ref, sall_ref, sprev_ref,
             wgu_ref, bg_ref, cw_ref, cg_ref, gg_ref, *rest):
        dz_ref, dzal_ref, dcw_ref, dcg_ref, dgg_ref, dbg_ref, dwgu_ref, dcarry_ref, gd_ref = rest[-9:]
        i = pl.program_id(0)

        @pl.when(i == 0)
        def _():
            dcarry_ref[...] = jnp.zeros_like(dcarry_ref)
            gd_ref[...] = jnp.zeros_like(gd_ref)
            dcw_ref[...] = jnp.zeros_like(dcw_ref)
            dcg_ref[...] = jnp.zeros_like(dcg_ref)
            dgg_ref[...] = jnp.zeros_like(dgg_ref)
            dbg_ref[...] = jnp.zeros_like(dbg_ref)
            dwgu_ref[...] = jnp.zeros_like(dwgu_ref)

        first = rev(i) == 0

        cb, cc, ch = cb_ref[...], cc_ref[...], ch_ref[...]
        w = cw_ref[...]
        uc = cc * ch
        prev8 = jnp.where(first, 0.0, ccp_ref[...] * chp_ref[...])
        conv, s1, s2 = _conv_taps(prev8, uc, w)
        ypre = cb * conv
        cg = cg_ref[...]
        dypre_parts = []
        for g in range(CONV_GROUPS):
            sl = slice(g * gw, (g + 1) * gw)
            seg = ypre[:, sl]
            r = lax.rsqrt(jnp.mean(seg * seg, axis=-1, keepdims=True) + EPS)
            yn = seg * r
            dyc = dy_ref[:, sl]
            dcg_ref[:, sl] += jnp.sum(dyc * yn, axis=0, keepdims=True)
            dyn = dyc * cg[:, sl]
            dypre_parts.append(r * (dyn - yn * jnp.mean(dyn * yn, axis=-1, keepdims=True)))
        dypre = jnp.concatenate(dypre_parts, axis=1)
        dconv = dypre * cb
        dz_ref[:, 0:d_conv] = (dypre * conv).astype(BF16)
        dcw_ref[0:1] += jnp.sum(dconv * s2, axis=0, keepdims=True)
        dcw_ref[1:2] += jnp.sum(dconv * s1, axis=0, keepdims=True)
        dcw_ref[2:3] += jnp.sum(dconv * uc, axis=0, keepdims=True)
        ext = jnp.concatenate([dconv, dcarry_ref[...]], axis=0)
        f1 = pltpu.roll(ext, tb + SUBLANES - 1, 0)[:tb]
        f2 = pltpu.roll(ext, tb + SUBLANES - 2, 0)[:tb]
        dcarry_ref[...] = dconv[:SUBLANES]
        duc = dconv * w[2:3] + f1 * w[1:2] + f2 * w[0:1]
        dz_ref[:, d_conv:2 * d_conv] = (duc * ch).astype(BF16)
        dz_ref[:, 2 * d_conv:3 * d_conv] = (duc * cc).astype(BF16)

        q_off = 3 * d_conv
        k_off = q_off + d_k
        v_off = k_off + d_k
        og_off = v_off + GLA_HEADS * dv
        later, same, earlier = _chunk_masks(tb)
        alb = al_ref[...].astype(BF16)
        pre = _dot(alb, wgu_ref[...]) + bg_ref[...]
        la = _log_sigmoid(pre) * (1.0 / GATE_NORMALIZER)
        exp_e = jnp.exp(_mask_dot(later, la))
        dec_all = jnp.exp(_mask_dot(same, la))
        kdec = k_ref[...] * exp_e
        kdec_b = kdec.astype(BF16)
        qs = (q_ref[...] * scale).astype(BF16)
        vb = v_ref[...].astype(BF16)
        gg = gg_ref[...]
        rows = [slice(c * CHUNK, (c + 1) * CHUNK) for c in range(cpt)]
        ks = [slice(h * dk, (h + 1) * dk) for h in range(GLA_HEADS)]
        vs = [slice(h * dv, (h + 1) * dv) for h in range(GLA_HEADS)]
        st_b = [[sall_ref[c, h].astype(BF16) for h in range(GLA_HEADS)] for c in range(cpt)]
        do_b = []
        dgg = jnp.zeros_like(gg)
        for h in range(GLA_HEADS):
            o = jnp.concatenate([_dot(qs[rows[c], ks[h]], st_b[c][h], _NT) for c in range(cpt)], axis=0)
            ro = lax.rsqrt(jnp.mean(o * o, axis=-1, keepdims=True) + EPS)
            on = o * ro
            ogs = og_ref[:, vs[h]]
            sg = jax.nn.sigmoid(ogs)
            gate = ogs * sg
            dyg = dy_ref[:, d_conv + h * dv:d_conv + (h + 1) * dv]
            dgg = dgg + jnp.sum(dyg * on * gate, axis=0, keepdims=True)
            dz_ref[:, og_off + h * dv:og_off + (h + 1) * dv] = (
                dyg * on * gg * (sg * (1.0 + ogs * (1.0 - sg)))).astype(BF16)
            don = dyg * gg * gate
            do_b.append((ro * (don - on * jnp.mean(don * on, axis=-1, keepdims=True))).astype(BF16))
        dgg_ref[...] += dgg
        for h in range(GLA_HEADS):
            dq = jnp.concatenate([_dot(do_b[h][rows[c]], st_b[c][h]) for c in range(cpt)], axis=0)
            dz_ref[:, q_off + h * dk:q_off + (h + 1) * dk] = (dq * scale).astype(BF16)
        own = [[_dot(do_b[h][rows[c]], qs[rows[c], ks[h]], _TN) for h in range(GLA_HEADS)] for c in range(cpt)]
        carried = [gd_ref[h] for h in range(GLA_HEADS)]
        gt_b = [None] * cpt
        ddd = [None] * cpt
        for c in reversed(range(cpt)):
            gt = [own[c][h] + carried[h] for h in range(GLA_HEADS)]
            dec = [dec_all[c * CHUNK:c * CHUNK + 1, ks[h]] for h in range(GLA_HEADS)]
            carried = [gt[h] * dec[h] for h in range(GLA_HEADS)]
            if c > 0:
                st_prev = [sall_ref[c - 1, h] for h in range(GLA_HEADS)]
            else:
                st_prev = [jnp.where(first, 0.0, sprev_ref[0, h]) for h in range(GLA_HEADS)]
            ddec = [jnp.sum(gt[h] * st_prev[h], axis=0, keepdims=True) * dec[h] for h in range(GLA_HEADS)]
            ddd[c] = jnp.broadcast_to(jnp.concatenate(ddec, axis=1), (CHUNK, d_k))
            gt_b[c] = [gt[h].astype(BF16) for h in range(GLA_HEADS)]
        for h in range(GLA_HEADS):
            gd_ref[h] = carried[h]
        dkdec_cols = []
        for h in range(GLA_HEADS):
            dvh = jnp.concatenate([_dot(kdec_b[rows[c], ks[h]], gt_b[c][h], _NT) for c in range(cpt)], axis=0)
            dz_ref[:, v_off + h * dv:v_off + (h + 1) * dv] = dvh.astype(BF16)
            dkdec_cols.append(jnp.concatenate([_dot(vb[rows[c], vs[h]], gt_b[c][h]) for c in range(cpt)], axis=0))
        dkdec = jnp.concatenate(dkdec_cols, axis=1)
        dz_ref[:, k_off:k_off + d_k] = (dkdec * exp_e).astype(BF16)
        dla = _mask_dot(earlier, dkdec * kdec) + jnp.concatenate(ddd, axis=0)
        dpre = dla * (1.0 / GATE_NORMALIZER) * jax.nn.sigmoid(-pre)
        dbg_ref[...] += jnp.sum(dpre, axis=0, keepdims=True)
        dpre_b = dpre.astype(BF16)
        dwgu_ref[...] += _dot(alb, dpre_b, _TN)
        dzal_ref[...] = _dot(dpre_b, wgu_ref[...], _NT).astype(BF16)

    full = lambda shape: pl.BlockSpec(shape, lambda i: (0,) * len(shape))
    prev_rows = lambda c: pl.BlockSpec(
        (SUBLANES, d_conv), lambda i, c=c: (jnp.maximum(rev(i) * (tb // SUBLANES) - 1, 0), c))
    n_z = 3 * d_conv + 2 * d_k + 2 * GLA_HEADS * dv
    return pl.pallas_call(
        body, name="mixer_bwd", grid=(nt,),
        in_specs=_z_specs(tb, rev) + [
            prev_rows(1), prev_rows(2),
            pl.BlockSpec((tb, LANES), lambda i: (rev(i), 0)),
            pl.BlockSpec((tb, d_conv + GLA_HEADS * dv), lambda i: (rev(i), 0)),
            pl.BlockSpec((cpt, GLA_HEADS, dv, dk), lambda i: (rev(i), 0, 0, 0)),
            pl.BlockSpec((1, GLA_HEADS, dv, dk), lambda i: (jnp.maximum(rev(i) * cpt - 1, 0), 0, 0, 0)),
            full(wgu.shape), full(b_gate.shape), full(convw.shape), full(conv_g.shape), full(gla_g.shape)]
        + dep_specs,
        out_specs=[pl.BlockSpec((tb, n_z), lambda i: (rev(i), 0)), pl.BlockSpec((tb, LANES), lambda i: (rev(i), 0)),
                   full(convw.shape), full(conv_g.shape), full(gla_g.shape), full(b_gate.shape), full(wgu.shape)],
        out_shape=[jax.ShapeDtypeStruct((t, n_z), BF16), jax.ShapeDtypeStruct((t, LANES), BF16),
                   jax.ShapeDtypeStruct(convw.shape, F32), jax.ShapeDtypeStruct(conv_g.shape, F32),
                   jax.ShapeDtypeStruct(gla_g.shape, F32), jax.ShapeDtypeStruct(b_gate.shape, F32),
                   jax.ShapeDtypeStruct(wgu.shape, F32)],
        scratch_shapes=[pltpu.VMEM((SUBLANES, d_conv), F32), pltpu.VMEM((GLA_HEADS, dv, dk), F32)],
        compiler_params=_params(1),
    )(z, z, z, z, z, z, z, z, z, alow, dy, sall, sall, wgu, b_gate, convw, conv_g, gla_g, *dep_args)


def _adamw_math(g, w, m, v):
    m = ADAM_B1 * m + (1.0 - ADAM_B1) * g
    v = ADAM_B2 * v + (1.0 - ADAM_B2) * (g * g)
    m_hat = m / (1.0 - ADAM_B1 ** ADAM_STEP)
    v_hat = v / (1.0 - ADAM_B2 ** ADAM_STEP)
    delta = -ADAM_LR * (m_hat / (jnp.sqrt(v_hat) + ADAM_EPS) + ADAM_WD * w)
    return delta, m, v


def _adamw(name, parts, w, m, v, tr):
    r, c = w.shape
    n_parts = parts.shape[0]

    def body(p_ref, w_ref, m_ref, v_ref, g_ref, d_ref, nm_ref, nv_ref):
        g = p_ref[0].astype(F32)
        for j in range(1, n_parts):
            g = g + p_ref[j].astype(F32)
        g_ref[...] = g
        d_ref[...], nm_ref[...], nv_ref[...] = _adamw_math(g, w_ref[...], m_ref[...], v_ref[...])

    blk = pl.BlockSpec((tr, c), lambda i: (i, 0))
    return pl.pallas_call(
        body, name=name, grid=(r // tr,),
        in_specs=[pl.BlockSpec((n_parts, tr, c), lambda i: (0, i, 0)), blk, blk, blk],
        out_specs=[blk] * 4, out_shape=[jax.ShapeDtypeStruct((r, c), F32)] * 4,
        compiler_params=_params(1),
    )(parts, w, m, v)


def _adamw_small(grads, ws, ms, vs):
    n = len(grads)

    def body(*refs):
        g, w, m, v = (refs[k * n:(k + 1) * n] for k in range(4))
        d_out, m_out, v_out = (refs[(4 + k) * n:(5 + k) * n] for k in range(3))
        for i in range(n):
            d_out[i][...], m_out[i][...], v_out[i][...] = _adamw_math(g[i][...], w[i][...], m[i][...], v[i][...])

    vmem = pl.BlockSpec(memory_space=pltpu.VMEM)
    outs = pl.pallas_call(
        body, name="adamw_small", out_shape=[jax.ShapeDtypeStruct(w.shape, F32) for w in ws] * 3,
        in_specs=[vmem] * (4 * n), out_specs=[vmem] * (3 * n),
    )(*grads, *ws, *ms, *vs)
    return [outs[:n], outs[n:2 * n], outs[2 * n:]]


def _sum_partials(parts):
    n_parts, rows, lanes = parts.shape

    def body(p_ref, o_ref):
        g = p_ref[0]
        for j in range(1, n_parts):
            g = g + p_ref[j]
        o_ref[...] = g

    return pl.pallas_call(
        body, name="sum_small_partials", out_shape=jax.ShapeDtypeStruct((rows, lanes), F32),
        in_specs=[pl.BlockSpec(memory_space=pltpu.VMEM)], out_specs=pl.BlockSpec(memory_space=pltpu.VMEM),
    )(parts)


def _pack_rows(vectors, rows):
    flat = jnp.concatenate([a.reshape(-1).astype(F32) for a in vectors])
    return jnp.pad(flat, (0, rows * LANES - flat.shape[0])).reshape(rows, LANES)


def _unpack_rows(block, shapes):
    flat = block.reshape(-1)
    out, off = [], 0
    for s in shapes:
        n = 1
        for dim in s:
            n *= dim
        out.append(flat[off:off + n].reshape(s))
        off += n
    return out


def kernel(x, norm1_g, w_in, w_gate_up, b_gate, conv_w, conv_norm_g, gla_norm_g, w_out, norm2_g, w_ff1, w_ff2, norm_f_g, loss_target, m_norm1_g, m_w_in, m_w_gate_up, m_b_gate, m_conv_w, m_conv_norm_g, m_gla_norm_g, m_w_out, m_norm2_g, m_w_ff1, m_w_ff2, m_norm_f_g, v_norm1_g, v_w_in, v_w_gate_up, v_b_gate, v_conv_w, v_conv_norm_g, v_gla_norm_g, v_w_out, v_norm2_g, v_w_ff1, v_w_ff2, v_norm_f_g):
    me = _device_index()
    x2d, tgt = x[0], loss_target[0]
    t, d = x2d.shape
    d_in_shard = w_in.shape[2]
    d_in = N_DEV * d_in_shard
    n_main = d_in - GATE_RANK
    d_conv = conv_norm_g.shape[1]
    d_k = b_gate.shape[1]
    d_ff = N_DEV * w_ff1.shape[2]
    wmv = dict(
        norm1_g=(norm1_g, m_norm1_g, v_norm1_g), w_in=(w_in, m_w_in, v_w_in),
        w_gate_up=(w_gate_up, m_w_gate_up, v_w_gate_up), b_gate=(b_gate, m_b_gate, v_b_gate),
        conv_w=(conv_w, m_conv_w, v_conv_w), conv_norm_g=(conv_norm_g, m_conv_norm_g, v_conv_norm_g),
        gla_norm_g=(gla_norm_g, m_gla_norm_g, v_gla_norm_g), w_out=(w_out, m_w_out, v_w_out),
        norm2_g=(norm2_g, m_norm2_g, v_norm2_g), w_ff1=(w_ff1, m_w_ff1, v_w_ff1), w_ff2=(w_ff2, m_w_ff2, v_w_ff2),
        norm_f_g=(norm_f_g, m_norm_f_g, v_norm_f_g))

    small_rows = 16
    first_level = (SIBLING,) + SAME_CORE_PEERS
    win_shard = w_in[0].astype(BF16)
    in_send, in_recv, in_src, in_land, token = _exchange_start(
        "all_gather_start_w_in", [win_shard], [_land_zone(win_shard)], scatter=False, masks=[first_level])
    _, wgu_t, cw_t, wout_t, w1_t, w2_t = lax.optimization_barrier((token, w_gate_up, conv_w, w_out, w_ff1, w_ff2))
    small_shard = _pack_rows([wgu_t[0], cw_t[0]], small_rows)
    shards = [small_shard, wout_t[0].astype(BF16), w1_t[0].astype(BF16), w2_t[0].astype(BF16)]
    ag_send, ag_recv, ag_src, ag_land, token = _exchange_start(
        "all_gather_start", shards, [_land_zone(s) for s in shards], scatter=False, behind=token)

    def gathered(k, name, after):
        return _exchange_wait(name, ag_send[k], ag_recv[k], ag_src[k], ag_land[k], after, scatter=False)

    u = _rmsnorm(x2d, norm1_g, behind=token)
    tied = lax.optimization_barrier((token, w_in, m_w_in, v_w_in))
    wmv["w_in"] = tuple(tied[1:])
    small_g = gathered(0, "all_gather_wait_small", [u] + [a[0] for a in wmv["w_in"]])
    win_level1 = _exchange_wait(
        "all_gather_wait_w_in", in_send[0], in_recv[0], in_src[0], in_land[0], small_g, scatter=False,
        masks=first_level)
    win_g = _forward_wait("all_gather_wait_w_in_forwarded", *_forward_start("all_gather_forward_w_in", win_level1))
    w_main, w_alow = _shards_to_columns(win_g, n_main)
    small_flat = small_g.reshape(N_DEV, -1)
    n_wgu = GATE_RANK * (d_k // N_DEV)
    wgu_full = small_flat[:, :n_wgu].reshape(N_DEV, GATE_RANK, d_k // N_DEV).transpose(1, 0, 2).reshape(GATE_RANK, d_k)
    conv_w_full = small_flat[:, n_wgu:n_wgu + (d_conv // N_DEV) * CONV_WIDTH].reshape(d_conv, CONV_WIDTH)
    wgu_pad = jnp.pad(wgu_full, ((0, LANES - GATE_RANK), (0, 0))).astype(BF16)
    convw_taps = jnp.pad(conv_w_full.T, ((0, SUBLANES - CONV_WIDTH), (0, 0)))

    get_w_out = lambda after: gathered(1, "all_gather_wait_w_out", after).reshape(-1, d)
    get_w1 = lambda after: gathered(2, "all_gather_wait_w_ff1", after)
    get_w2 = lambda after: gathered(3, "all_gather_wait_w_ff2", after).reshape(d_ff, d)

    in_flight = {}

    def send_partials(name, parts):
        own = lax.dynamic_index_in_dim(parts, me, axis=0, keepdims=False)
        send, recv, src, land, token = _exchange_start("scatter_start_" + name, [parts], [_land_zone(own)], scatter=True)
        in_flight[name] = (send[0], recv[0], src[0], land[0])
        return token

    def on_grad(name, value):
        if name == "w_in":
            main, alow_part = value
            value = _columns_to_shards(main, alow_part, N_DEV, d_in_shard)
        elif name in ("w_out", "w_ff2"):
            value = value.reshape(N_DEV, -1, d)
        return send_partials(name, value)

    grads = _local_step(x2d, u, tgt, norm1_g, w_main, w_alow, wgu_pad, b_gate, convw_taps, conv_norm_g, gla_norm_g,
                        norm2_g, norm_f_g, get_w_out, get_w1, get_w2, on_grad)
    grad_x = grads["x"]

    small_shapes = [(1, d), (1, d_k), (1, d_conv), (1, gla_norm_g.shape[1]), (1, d), (d,),
                    (GATE_RANK, d_k), (d_conv, CONV_WIDTH), (1,)]
    small_grad_rows = 152
    small_part = _pack_rows(
        [grads["norm1_g"], grads["b_gate"], grads["conv_norm_g"], grads["gla_norm_g"], grads["norm2_g"],
         grads["norm_f_g"], grads["w_gate_up"][:GATE_RANK], grads["conv_w"][:CONV_WIDTH].T, grads["loss"][0, 0]],
        small_grad_rows)
    send_partials("small", jnp.broadcast_to(small_part[None], (N_DEV, small_grad_rows, LANES)))

    def received(name):
        send, recv, src, land = in_flight[name]
        return _exchange_wait("scatter_wait_" + name, send, recv, src, land, grad_x, scatter=True)

    small_r = received("small")
    gin_r, gout_r, g1_r, g2_r = received("w_in"), received("w_out"), received("w_ff1"), received("w_ff2")
    return _update(me, gin_r, gout_r, g1_r, g2_r, small_r, small_shapes, grad_x, wmv)


def _local_step(x2d, u, tgt, norm1_g, w_main, w_alow, wgu_pad, b_gate, convw_taps, conv_norm_g, gla_norm_g,
                norm2_g, norm_f_g, get_w_out, get_w1, get_w2, on_grad):
    t, d = x2d.shape
    n_main = w_main.shape[1]

    z, alow = _inproj(u, w_main, w_alow)
    y, sall = _mixer_fwd(z, alow, wgu_pad, b_gate, convw_taps, conv_norm_g, gla_norm_g)
    w_out_full = get_w_out(y)
    x1, h = _outproj(y, w_out_full, x2d, norm2_g)
    w1g = get_w1(h)
    a = _ff1(h, w1g)
    w2_full = get_w2(a)
    d_ff = w2_full.shape[0]
    x2 = _ff2(a, w2_full, x1)
    dx2, dx2b, loss_part, d_normf = _loss_head(x2, norm_f_g.reshape(1, d), tgt)

    tk = min(4096, t)
    nk = t // tk
    da = _dff2(dx2b, w2_full, a)
    dw2 = _tn_matmul(
        "dw_ff2", a, dx2b, (d_ff // 1024, d // 1024, nk),
        pl.BlockSpec((tk, 1024), lambda m, j, kk: (kk, m)), pl.BlockSpec((tk, 1024), lambda m, j, kk: (kk, j)),
        jax.ShapeDtypeStruct((d_ff, d), BF16), pl.BlockSpec((1024, 1024), lambda m, j, kk: (m, j)), (1024, 1024),
        a_fn=_relu_sq)
    token = on_grad("w_ff2", dw2)
    f_shard = d_ff // N_DEV
    dw1 = _tn_matmul(
        "dw_ff1", h, da, (N_DEV, d // 1024, nk),
        pl.BlockSpec((tk, 1024), lambda g, m, kk: (kk, m)), pl.BlockSpec((tk, f_shard), lambda g, m, kk: (kk, g)),
        jax.ShapeDtypeStruct((N_DEV, d, f_shard), BF16), pl.BlockSpec((None, 1024, f_shard), lambda g, m, kk: (g, m, 0)),
        (1024, f_shard), behind=token)
    token = on_grad("w_ff1", dw1)
    dh = _dh(da, w1g, behind=token)
    dx1, dx1b, d_norm2 = _norm_bwd("norm2_bwd", dh, x1, norm2_g, dx2, with_bf16=True)
    dy = _nt_matmul("dy", dx1b, w_out_full)
    dwout = _tn_matmul(
        "dw_out", y, dx1b, (d // 1024, d // 1024, nk),
        pl.BlockSpec((tk, 1024), lambda m, j, kk: (kk, m)), pl.BlockSpec((tk, 1024), lambda m, j, kk: (kk, j)),
        jax.ShapeDtypeStruct((d, d), BF16), pl.BlockSpec((1024, 1024), lambda m, j, kk: (m, j)), (1024, 1024))
    token = on_grad("w_out", dwout)
    dz, dzal, d_convw, d_convg, d_glag, d_bgate, d_wgu = _mixer_bwd(
        z, alow, dy, sall, wgu_pad, b_gate, convw_taps, conv_norm_g, gla_norm_g, behind=token)
    dwin_main = _tn_matmul(
        "dw_in", u, dz, (d // 1024, n_main // 1024, nk),
        pl.BlockSpec((tk, 1024), lambda m, j, kk: (kk, m)), pl.BlockSpec((tk, 1024), lambda m, j, kk: (kk, j)),
        jax.ShapeDtypeStruct((d, n_main), BF16), pl.BlockSpec((1024, 1024), lambda m, j, kk: (m, j)), (1024, 1024))
    dwin_alow = _tn_matmul(
        "dw_in_alow", u, dzal, (d // 1024, 1, nk),
        pl.BlockSpec((tk, 1024), lambda m, j, kk: (kk, m)), pl.BlockSpec((tk, LANES), lambda m, j, kk: (kk, 0)),
        jax.ShapeDtypeStruct((d, LANES), BF16), pl.BlockSpec((1024, LANES), lambda m, j, kk: (m, 0)), (1024, LANES))
    token = on_grad("w_in", (dwin_main, dwin_alow))
    du = _du(dz, w_main, dzal, w_alow, behind=token)
    grad_x, d_norm1 = _norm_bwd("norm1_bwd", du, x2d, norm1_g, dx1, with_bf16=False)
    return dict(x=grad_x, loss=loss_part, norm1_g=d_norm1, w_gate_up=d_wgu, b_gate=d_bgate, conv_w=d_convw,
                conv_norm_g=d_convg, gla_norm_g=d_glag, norm2_g=d_norm2, norm_f_g=d_normf)


_WEIGHT_ORDER = ("norm1_g", "w_in", "w_gate_up", "b_gate", "conv_w", "conv_norm_g", "gla_norm_g", "w_out", "norm2_g",
                 "w_ff1", "w_ff2", "norm_f_g")
_SMALL_ORDER = ("norm1_g", "b_gate", "conv_norm_g", "gla_norm_g", "norm2_g", "norm_f_g", "w_gate_up", "conv_w")
def _update(me, gin_r, gout_r, g1_r, g2_r, small_r, small_shapes, grad_x, wmv):
    big = {
        "w_in": _adamw("adamw_w_in", gin_r, *(a[0] for a in wmv["w_in"]), 256),
        "w_out": _adamw("adamw_w_out", gout_r, *(a[0] for a in wmv["w_out"]), 128),
        "w_ff1": _adamw("adamw_w_ff1", g1_r, *(a[0] for a in wmv["w_ff1"]), 256),
        "w_ff2": _adamw("adamw_w_ff2", g2_r, *(a[0] for a in wmv["w_ff2"]), 128),
    }

    wgu_cols = wmv["w_gate_up"][0].shape[2]
    cw_rows = wmv["conv_w"][0].shape[1]

    summed = _unpack_rows(_sum_partials(small_r), small_shapes)
    summed[6] = lax.dynamic_slice_in_dim(summed[6], me * wgu_cols, wgu_cols, axis=1)
    summed[7] = lax.dynamic_slice_in_dim(summed[7], me * cw_rows, cw_rows, axis=0)
    as_2d = lambda a: a.reshape((1, -1) if a.ndim == 1 else a.shape[-2:])
    grads_2d = [as_2d(g) for g in summed[:len(_SMALL_ORDER)]]
    small = _adamw_small(grads_2d, *[[as_2d(wmv[nm][k]) for nm in _SMALL_ORDER] for k in range(3)])
    small = [grads_2d] + small

    outs = []
    for k in range(4):
        for nm in _WEIGHT_ORDER:
            if nm in big:
                outs.append(big[nm][k][None])
            else:
                outs.append(small[k][_SMALL_ORDER.index(nm)].reshape(wmv[nm][0].shape))
    loss = summed[8][0]
    return (loss, grad_x[None], *outs)
```

```python
import functools

import jax
import jax.numpy as jnp
from jax import lax
from jax.experimental import pallas as pl
from jax.experimental.pallas import tpu as pltpu

F32 = jnp.float32
BF16 = jnp.bfloat16

N_DEV = 8
CHUNK = 64
GLA_HEADS = 4
CONV_GROUPS = 8
CONV_WIDTH = 3
GATE_RANK = 16
GATE_NORMALIZER = 16.0
EPS = 1e-6
ADAM_LR = 0.001
ADAM_B1 = 0.9
ADAM_B2 = 0.999
ADAM_EPS = 1e-08
ADAM_WD = 0.01
ADAM_STEP = 10

LANES = 128
SUBLANES = 8
VMEM_LIMIT = 56 << 20

_NN = (((1,), (0,)), ((), ()))
_NT = (((1,), (1,)), ((), ()))
_TN = (((0,), (0,)), ((), ()))


def _dot(a, b, dims=_NN):
    return lax.dot_general(a, b, dims, preferred_element_type=F32)


def _params(n_grid):
    return pltpu.CompilerParams(dimension_semantics=("arbitrary",) * n_grid, vmem_limit_bytes=VMEM_LIMIT)


def _relu_sq(a):
    r = jnp.maximum(a, 0.0)
    return r * r


def _device_index():
    return 4 * lax.axis_index("x") + 2 * lax.axis_index("y") + lax.axis_index("c")


def _peer(mask):
    x, y, c = lax.axis_index("x"), lax.axis_index("y"), lax.axis_index("c")
    return (x ^ ((mask >> 2) & 1), y ^ ((mask >> 1) & 1), c ^ (mask & 1))


_HBM_SPEC = pl.BlockSpec(memory_space=pltpu.HBM)
_SEM_SPEC = pl.BlockSpec(memory_space=pltpu.SEMAPHORE)
_SIDE_EFFECT = pltpu.SideEffectType.DATAFLOW_SIDE_EFFECTING
N_PEERS = N_DEV - 1


def _exchange_copy(src_ref, land_ref, send_sems, recv_sems, mask, scatter, arriving):
    me = _device_index()
    src = src_ref.at[me ^ mask] if scatter else src_ref
    dst = land_ref.at[(me ^ mask) if arriving else me]
    return pltpu.make_async_remote_copy(
        src_ref=src, dst_ref=dst, send_sem=send_sems.at[mask - 1], recv_sem=recv_sems.at[mask - 1],
        device_id=_peer(mask), device_id_type=pl.DeviceIdType.MESH)


def _land_zone(own):
    zone = lax.empty((N_DEV,) + own.shape, own.dtype)
    return lax.dynamic_update_slice(zone, own[None], (_device_index(),) + (0,) * own.ndim)


ALL_PEERS = tuple(range(1, N_DEV))
SIBLING = 1
SAME_CORE_PEERS = (2, 4, 6)


def _exchange_start(name, srcs, lands, scatter, masks=None, behind=None):
    n = len(srcs)
    masks = masks or [ALL_PEERS] * n
    dep_args = [] if behind is None else [behind]

    def body(*refs):
        src, land = refs[:n], refs[n:2 * n]
        outs = refs[2 * n + len(dep_args):]
        send_sems, recv_sems = outs[:n], outs[n:2 * n]
        token = refs[-1]
        for a in range(n):
            for mask in masks[a]:
                _exchange_copy(src[a], land[a], send_sems[a], recv_sems[a], mask, scatter, False).start()
        token[...] = jnp.zeros_like(token)

    hbm = lambda a: pltpu.HBM(a.shape, a.dtype)
    outs = pl.pallas_call(
        body, name=name,
        out_shape=([pltpu.SemaphoreType.DMA((N_PEERS,))] * (2 * n) + [hbm(a) for a in srcs] + [hbm(a) for a in lands]
                   + [jax.ShapeDtypeStruct((SUBLANES, LANES), F32)]),
        in_specs=[_HBM_SPEC] * (2 * n) + [pl.BlockSpec(memory_space=pl.ANY)] * len(dep_args),
        out_specs=[_SEM_SPEC] * (2 * n) + [_HBM_SPEC] * (2 * n) + [pl.BlockSpec(memory_space=pltpu.VMEM)],
        input_output_aliases={a: 2 * n + a for a in range(2 * n)},
        compiler_params=pltpu.CompilerParams(has_side_effects=_SIDE_EFFECT),
    )(*[pltpu.with_memory_space_constraint(a, pltpu.HBM) for a in list(srcs) + list(lands)], *dep_args)
    send_sems, recv_sems = outs[:n], outs[n:2 * n]
    src_thru, land_thru = outs[2 * n:3 * n], outs[3 * n:4 * n]
    return send_sems, recv_sems, src_thru, land_thru, outs[-1]


def _exchange_wait(name, send_sems, recv_sems, src_thru, land_thru, after, scatter, masks=ALL_PEERS):
    after = list(after) if isinstance(after, (list, tuple)) else [after]

    def body(src_ref, land_ref, send_ref, recv_ref, *rest):
        for mask in masks:
            cp = _exchange_copy(src_ref, land_ref, send_ref, recv_ref, mask, scatter, True)
            cp.wait_send()
            cp.wait_recv()

    return pl.pallas_call(
        body, name=name,
        out_shape=(pltpu.HBM(src_thru.shape, src_thru.dtype), pltpu.HBM(land_thru.shape, land_thru.dtype)),
        in_specs=[_HBM_SPEC, _HBM_SPEC, _SEM_SPEC, _SEM_SPEC] + [pl.BlockSpec(memory_space=pl.ANY)] * len(after),
        out_specs=(_HBM_SPEC, _HBM_SPEC), input_output_aliases={0: 0, 1: 1},
        compiler_params=pltpu.CompilerParams(has_side_effects=_SIDE_EFFECT),
    )(src_thru, land_thru, send_sems, recv_sems, *after)[1]


def _forward_copy(land_ref, send_sems, recv_sems, k, arriving):
    me = _device_index()
    slot = me ^ SAME_CORE_PEERS[k]
    return pltpu.make_async_remote_copy(
        src_ref=land_ref.at[slot], dst_ref=land_ref.at[(slot ^ SIBLING) if arriving else slot],
        send_sem=send_sems.at[k], recv_sem=recv_sems.at[k],
        device_id=_peer(SIBLING), device_id_type=pl.DeviceIdType.MESH)


def _forward_start(name, land):
    n_fwd = len(SAME_CORE_PEERS)

    def body(land_ref, send_sems, recv_sems, land_thru):
        for k in range(n_fwd):
            _forward_copy(land_ref, send_sems, recv_sems, k, False).start()

    send, recv, thru = pl.pallas_call(
        body, name=name,
        out_shape=[pltpu.SemaphoreType.DMA((n_fwd,)), pltpu.SemaphoreType.DMA((n_fwd,)), pltpu.HBM(land.shape, land.dtype)],
        in_specs=[_HBM_SPEC], out_specs=[_SEM_SPEC, _SEM_SPEC, _HBM_SPEC], input_output_aliases={0: 2},
        compiler_params=pltpu.CompilerParams(has_side_effects=_SIDE_EFFECT),
    )(pltpu.with_memory_space_constraint(land, pltpu.HBM))
    return send, recv, thru


def _forward_wait(name, send_sems, recv_sems, land_thru):
    def body(land_ref, send_ref, recv_ref, got_ref):
        for k in range(len(SAME_CORE_PEERS)):
            cp = _forward_copy(land_ref, send_ref, recv_ref, k, True)
            cp.wait_send()
            cp.wait_recv()

    return pl.pallas_call(
        body, name=name, out_shape=pltpu.HBM(land_thru.shape, land_thru.dtype),
        in_specs=[_HBM_SPEC, _SEM_SPEC, _SEM_SPEC], out_specs=_HBM_SPEC, input_output_aliases={0: 0},
        compiler_params=pltpu.CompilerParams(has_side_effects=_SIDE_EFFECT),
    )(land_thru, send_sems, recv_sems)


def _shards_to_columns(g, n_main, tr=256):
    n_dev, d, s = g.shape

    def body(g_ref, main_ref, rest_ref):
        for j in range(n_dev):
            lo, hi = j * s, (j + 1) * s
            if hi <= n_main:
                main_ref[:, lo:hi] = g_ref[j]
            else:
                main_ref[:, lo:n_main] = g_ref[j, :, 0:n_main - lo]
                rest_ref[...] = jnp.zeros_like(rest_ref)
                rest_ref[:, 0:hi - n_main] = g_ref[j, :, n_main - lo:s]

    return pl.pallas_call(
        body, grid=(d // tr,), name="shards_to_columns",
        in_specs=[pl.BlockSpec((n_dev, tr, s), lambda i: (0, i, 0))],
        out_specs=[pl.BlockSpec((tr, n_main), lambda i: (i, 0)), pl.BlockSpec((tr, LANES), lambda i: (i, 0))],
        out_shape=[jax.ShapeDtypeStruct((d, n_main), g.dtype), jax.ShapeDtypeStruct((d, LANES), g.dtype)],
        compiler_params=_params(1),
    )(g)


def _columns_to_shards(main, rest, n_dev, s, tr=256):
    d, n_main = main.shape
    assert (n_dev - 1) * s <= n_main < n_dev * s

    def body(main_ref, rest_ref, o_ref):
        for j in range(n_dev):
            lo, hi = j * s, (j + 1) * s
            if hi <= n_main:
                o_ref[j] = main_ref[:, lo:hi]
            else:
                o_ref[j, :, 0:n_main - lo] = main_ref[:, lo:n_main]
                o_ref[j, :, n_main - lo:s] = rest_ref[:, 0:hi - n_main]

    return pl.pallas_call(
        body, grid=(d // tr,), name="columns_to_shards",
        in_specs=[pl.BlockSpec((tr, n_main), lambda i: (i, 0)), pl.BlockSpec((tr, LANES), lambda i: (i, 0))],
        out_specs=pl.BlockSpec((n_dev, tr, s), lambda i: (0, i, 0)),
        out_shape=jax.ShapeDtypeStruct((n_dev, d, s), main.dtype),
        compiler_params=_params(1),
    )(main, rest)


def _rmsnorm(x, g, tr=512, behind=None):
    t, d = x.shape
    tr = min(tr, t)
    dep_args, dep_specs = _behind(behind)

    def body(x_ref, g_ref, *rest):
        u_ref = rest[-1]
        xf = x_ref[...]
        r = lax.rsqrt(jnp.mean(xf * xf, axis=-1, keepdims=True) + EPS)
        u_ref[...] = (xf * r * g_ref[...]).astype(BF16)

    return pl.pallas_call(
        body, name="rmsnorm1", grid=(t // tr,),
        in_specs=[pl.BlockSpec((tr, d), lambda i: (i, 0)), pl.BlockSpec((1, d), lambda i: (0, 0))] + dep_specs,
        out_specs=pl.BlockSpec((tr, d), lambda i: (i, 0)),
        out_shape=jax.ShapeDtypeStruct((t, d), BF16),
        compiler_params=_params(1),
    )(x, g, *dep_args)


def _inproj(u, w_main, w_alow, tm=1024, tn=1024):
    t, d = u.shape
    tm = min(tm, t)
    n = w_main.shape[1]

    def body(u_ref, w_ref, wa_ref, z_ref, al_ref):
        @pl.when(pl.program_id(1) == 0)
        def _():
            al_ref[...] = _dot(u_ref[...], wa_ref[...])

        z_ref[...] = _dot(u_ref[...], w_ref[...])

    return pl.pallas_call(
        body, name="inproj", grid=(t // tm, n // tn),
        in_specs=[pl.BlockSpec((tm, d), lambda m, j: (m, 0)), pl.BlockSpec((d, tn), lambda m, j: (0, j)),
                  pl.BlockSpec((d, LANES), lambda m, j: (0, 0))],
        out_specs=[pl.BlockSpec((tm, tn), lambda m, j: (m, j)), pl.BlockSpec((tm, LANES), lambda m, j: (m, 0))],
        out_shape=[jax.ShapeDtypeStruct((t, n), F32), jax.ShapeDtypeStruct((t, LANES), F32)],
        compiler_params=_params(2),
    )(u, w_main, w_alow)


def _outproj(y, w_out, x, g2, tm=512):
    t, d = x.shape
    tm = min(tm, t)
    k = y.shape[1]

    def body(y_ref, w_ref, x_ref, g_ref, x1_ref, h_ref):
        x1 = x_ref[...] + _dot(y_ref[...], w_ref[...])
        x1_ref[...] = x1
        r = lax.rsqrt(jnp.mean(x1 * x1, axis=-1, keepdims=True) + EPS)
        h_ref[...] = (x1 * r * g_ref[...]).astype(BF16)

    return pl.pallas_call(
        body, name="outproj_rmsnorm", grid=(t // tm,),
        in_specs=[pl.BlockSpec((tm, k), lambda m: (m, 0)), pl.BlockSpec((k, d), lambda m: (0, 0)),
                  pl.BlockSpec((tm, d), lambda m: (m, 0)), pl.BlockSpec((1, d), lambda m: (0, 0))],
        out_specs=[pl.BlockSpec((tm, d), lambda m: (m, 0)), pl.BlockSpec((tm, d), lambda m: (m, 0))],
        out_shape=[jax.ShapeDtypeStruct((t, d), F32), jax.ShapeDtypeStruct((t, d), BF16)],
        compiler_params=_params(1),
    )(y, w_out, x, g2)


def _ff1(h, w1g, tm=1024):
    t, d = h.shape
    tm = min(tm, t)
    g, _, f = w1g.shape

    def body(h_ref, w_ref, a_ref):
        a_ref[...] = _dot(h_ref[...], w_ref[...]).astype(BF16)

    return pl.pallas_call(
        body, name="ff1", grid=(t // tm, g),
        in_specs=[pl.BlockSpec((tm, d), lambda m, j: (m, 0)), pl.BlockSpec((None, d, f), lambda m, j: (j, 0, 0))],
        out_specs=pl.BlockSpec((tm, f), lambda m, j: (m, j)),
        out_shape=jax.ShapeDtypeStruct((t, g * f), BF16),
        compiler_params=_params(2),
    )(h, w1g)


def _ff2(a, w2, x1, tm=1024, tn=1024, tk=2048):
    t, f = a.shape
    tm = min(tm, t)
    d = w2.shape[1]

    def body(a_ref, w_ref, x1_ref, o_ref):
        @pl.when(pl.program_id(2) == 0)
        def _():
            o_ref[...] = x1_ref[...]

        o_ref[...] += _dot(_relu_sq(a_ref[...]), w_ref[...])

    return pl.pallas_call(
        body, name="ff2_residual", grid=(t // tm, d // tn, f // tk),
        in_specs=[pl.BlockSpec((tm, tk), lambda m, j, kk: (m, kk)), pl.BlockSpec((tk, tn), lambda m, j, kk: (kk, j)),
                  pl.BlockSpec((tm, tn), lambda m, j, kk: (m, j))],
        out_specs=pl.BlockSpec((tm, tn), lambda m, j, kk: (m, j)),
        out_shape=jax.ShapeDtypeStruct((t, d), F32),
        compiler_params=_params(3),
    )(a, w2, x1)


def _dff2(dx2b, w2, a, tm=1024, tn=1024):
    t, d = dx2b.shape
    tm = min(tm, t)
    f = w2.shape[0]

    def body(g_ref, w_ref, a_ref, o_ref):
        dp = _dot(g_ref[...], w_ref[...], _NT)
        o_ref[...] = (dp * (2.0 * jnp.maximum(a_ref[...].astype(F32), 0.0))).astype(BF16)

    return pl.pallas_call(
        body, name="dff2", grid=(t // tm, f // tn),
        in_specs=[pl.BlockSpec((tm, d), lambda m, j: (m, 0)), pl.BlockSpec((tn, d), lambda m, j: (j, 0)),
                  pl.BlockSpec((tm, tn), lambda m, j: (m, j))],
        out_specs=pl.BlockSpec((tm, tn), lambda m, j: (m, j)),
        out_shape=jax.ShapeDtypeStruct((t, f), BF16),
        compiler_params=_params(2),
    )(dx2b, w2, a)


def _behind(token):
    if token is None:
        return [], []
    return [token], [pl.BlockSpec(token.shape, lambda *_: (0,) * token.ndim)]


def _tn_matmul(name, a, b, grid, a_spec, b_spec, out_shape, out_spec, acc_shape, a_fn=None, behind=None):
    nk = grid[-1]
    dep_args, dep_specs = _behind(behind)

    def body(a_ref, b_ref, *rest):
        o_ref, acc_ref = rest[-2:]
        kk = pl.program_id(len(grid) - 1)
        av = a_ref[...]
        if a_fn is not None:
            av = a_fn(av)
        part = _dot(av, b_ref[...], _TN)

        @pl.when(kk == 0)
        def _():
            acc_ref[...] = part

        @pl.when(kk > 0)
        def _():
            acc_ref[...] += part

        @pl.when(kk == nk - 1)
        def _():
            o_ref[...] = acc_ref[...].astype(o_ref.dtype)

    return pl.pallas_call(
        body, name=name, grid=grid, in_specs=[a_spec, b_spec] + dep_specs, out_specs=out_spec, out_shape=out_shape,
        scratch_shapes=[pltpu.VMEM(acc_shape, F32)], compiler_params=_params(len(grid)),
    )(a, b, *dep_args)


def _dw_in(u, dz, dzal, tk, tm=1024, tn=1024):
    t, d = u.shape
    n_main = dz.shape[1]
    nk = t // tk

    def body(a_ref, b_ref, al_ref, o_ref, oal_ref, acc_ref, accal_ref):
        j, kk = pl.program_id(1), pl.program_id(2)
        av = a_ref[...]

        def accumulate(acc, part, out):
            @pl.when(kk == 0)
            def _():
                acc[...] = part

            @pl.when(kk > 0)
            def _():
                acc[...] += part

            @pl.when(kk == nk - 1)
            def _():
                out[...] = acc[...].astype(out.dtype)

        accumulate(acc_ref, _dot(av, b_ref[...], _TN), o_ref)

        @pl.when(j == 0)
        def _():
            accumulate(accal_ref, _dot(av, al_ref[...], _TN), oal_ref)

    return pl.pallas_call(
        body, name="dw_in", grid=(d // tm, n_main // tn, nk),
        in_specs=[pl.BlockSpec((tk, tm), lambda m, j, kk: (kk, m)), pl.BlockSpec((tk, tn), lambda m, j, kk: (kk, j)),
                  pl.BlockSpec((tk, LANES), lambda m, j, kk: (kk, 0))],
        out_specs=[pl.BlockSpec((tm, tn), lambda m, j, kk: (m, j)), pl.BlockSpec((tm, LANES), lambda m, j, kk: (m, 0))],
        out_shape=[jax.ShapeDtypeStruct((d, n_main), BF16), jax.ShapeDtypeStruct((d, LANES), BF16)],
        scratch_shapes=[pltpu.VMEM((tm, tn), F32), pltpu.VMEM((tm, LANES), F32)],
        compiler_params=_params(3),
    )(u, dz, dzal)


def _dh(da, w1g, tm=1024, tn=1024, shards_per_step=4, behind=None):
    t = da.shape[0]
    tm = min(tm, t)
    g, d, f = w1g.shape
    sps = shards_per_step
    dep_args, dep_specs = _behind(behind)

    def body(a_ref, w_ref, *rest):
        o_ref = rest[-1]
        acc = _dot(a_ref[:, 0:f], w_ref[0], _NT)
        for s in range(1, sps):
            acc = acc + _dot(a_ref[:, s * f:(s + 1) * f], w_ref[s], _NT)

        @pl.when(pl.program_id(2) == 0)
        def _():
            o_ref[...] = acc

        @pl.when(pl.program_id(2) > 0)
        def _():
            o_ref[...] += acc

    return pl.pallas_call(
        body, name="dh", grid=(t // tm, d // tn, g // sps),
        in_specs=[pl.BlockSpec((tm, sps * f), lambda m, j, kk: (m, kk)),
                  pl.BlockSpec((sps, tn, f), lambda m, j, kk: (kk, j, 0))] + dep_specs,
        out_specs=pl.BlockSpec((tm, tn), lambda m, j, kk: (m, j)),
        out_shape=jax.ShapeDtypeStruct((t, d), F32),
        compiler_params=_params(3),
    )(da, w1g, *dep_args)


def _nt_matmul(name, a, b, tm=1024, tn=1024):
    t, k = a.shape
    tm = min(tm, t)
    n = b.shape[0]

    def body(a_ref, b_ref, o_ref):
        o_ref[...] = _dot(a_ref[...], b_ref[...], _NT)

    return pl.pallas_call(
        body, name=name, grid=(t // tm, n // tn),
        in_specs=[pl.BlockSpec((tm, k), lambda m, j: (m, 0)), pl.BlockSpec((tn, k), lambda m, j: (j, 0))],
        out_specs=pl.BlockSpec((tm, tn), lambda m, j: (m, j)),
        out_shape=jax.ShapeDtypeStruct((t, n), F32),
        compiler_params=_params(2),
    )(a, b)


def _du(dz, w_main, dzal, w_alow, tm=1024, tn=1024, tk=3072, behind=None):
    t, n = dz.shape
    tm = min(tm, t)
    d = w_main.shape[0]
    dep_args, dep_specs = _behind(behind)

    def body(a_ref, w_ref, al_ref, wa_ref, *rest):
        o_ref = rest[-1]

        @pl.when(pl.program_id(2) == 0)
        def _():
            o_ref[...] = _dot(al_ref[...], wa_ref[...], _NT)

        o_ref[...] += _dot(a_ref[...], w_ref[...], _NT)

    return pl.pallas_call(
        body, name="du", grid=(t // tm, d // tn, n // tk),
        in_specs=[pl.BlockSpec((tm, tk), lambda m, j, kk: (m, kk)), pl.BlockSpec((tn, tk), lambda m, j, kk: (j, kk)),
                  pl.BlockSpec((tm, LANES), lambda m, j, kk: (m, 0)), pl.BlockSpec((tn, LANES), lambda m, j, kk: (j, 0))]
        + dep_specs,
        out_specs=pl.BlockSpec((tm, tn), lambda m, j, kk: (m, j)),
        out_shape=jax.ShapeDtypeStruct((t, d), F32),
        compiler_params=_params(3),
    )(dz, w_main, dzal, w_alow, *dep_args)


def _loss_head(x2, gf, tgt, tr=256):
    t, d = x2.shape

    def body(x_ref, g_ref, t_ref, dx_ref, dxb_ref, loss_ref, dg_ref):
        @pl.when(pl.program_id(0) == 0)
        def _():
            loss_ref[...] = jnp.zeros_like(loss_ref)
            dg_ref[...] = jnp.zeros_like(dg_ref)

        xf = x_ref[...]
        g = g_ref[...]
        r = lax.rsqrt(jnp.mean(xf * xf, axis=-1, keepdims=True) + EPS)
        xh = xf * r
        e = xh * g - t_ref[...]
        loss_ref[...] += 0.5 * jnp.sum(jnp.mean(e * e, axis=-1, keepdims=True))
        dy = e * (1.0 / d)
        dg_ref[...] += jnp.sum(dy * xh, axis=0, keepdims=True)
        dyg = dy * g
        dx = r * (dyg - xh * jnp.mean(dyg * xh, axis=-1, keepdims=True))
        dx_ref[...] = dx
        dxb_ref[...] = dx.astype(BF16)

    return pl.pallas_call(
        body, name="loss_head", grid=(t // tr,),
        in_specs=[pl.BlockSpec((tr, d), lambda i: (i, 0)), pl.BlockSpec((1, d), lambda i: (0, 0)),
                  pl.BlockSpec((tr, d), lambda i: (i, 0))],
        out_specs=[pl.BlockSpec((tr, d), lambda i: (i, 0)), pl.BlockSpec((tr, d), lambda i: (i, 0)),
                   pl.BlockSpec((SUBLANES, LANES), lambda i: (0, 0)), pl.BlockSpec((1, d), lambda i: (0, 0))],
        out_shape=[jax.ShapeDtypeStruct((t, d), F32), jax.ShapeDtypeStruct((t, d), BF16),
                   jax.ShapeDtypeStruct((SUBLANES, LANES), F32), jax.ShapeDtypeStruct((1, d), F32)],
        compiler_params=_params(1),
    )(x2, gf, tgt)


def _norm_bwd(name, dh, xin, g, dres, with_bf16, tr=256):
    t, d = xin.shape

    def body(dh_ref, x_ref, g_ref, dr_ref, dx_ref, *rest):
        dg_ref = rest[-1]

        @pl.when(pl.program_id(0) == 0)
        def _():
            dg_ref[...] = jnp.zeros_like(dg_ref)

        xf = x_ref[...]
        dhv = dh_ref[...]
        r = lax.rsqrt(jnp.mean(xf * xf, axis=-1, keepdims=True) + EPS)
        xh = xf * r
        dg_ref[...] += jnp.sum(dhv * xh, axis=0, keepdims=True)
        dyg = dhv * g_ref[...]
        dx = dr_ref[...] + r * (dyg - xh * jnp.mean(dyg * xh, axis=-1, keepdims=True))
        dx_ref[...] = dx
        if with_bf16:
            rest[0][...] = dx.astype(BF16)

    rows = pl.BlockSpec((tr, d), lambda i: (i, 0))
    vec = pl.BlockSpec((1, d), lambda i: (0, 0))
    return pl.pallas_call(
        body, name=name, grid=(t // tr,),
        in_specs=[rows, rows, vec, rows],
        out_specs=[rows] + [rows] * with_bf16 + [vec],
        out_shape=[jax.ShapeDtypeStruct((t, d), F32)] + [jax.ShapeDtypeStruct((t, d), BF16)] * with_bf16
        + [jax.ShapeDtypeStruct((1, d), F32)],
        compiler_params=_params(1),
    )(dh, xin, g, dres)


MIX_TILE = 256
CHUNKS_PER_TILE = MIX_TILE // CHUNK
CHUNK_SHIFT = CHUNK.bit_length() - 1
assert 1 << CHUNK_SHIFT == CHUNK


def _chunk_masks(n):
    row = lax.broadcasted_iota(jnp.int32, (n, n), 0)
    col = lax.broadcasted_iota(jnp.int32, (n, n), 1)
    same = lax.shift_right_logical(row, CHUNK_SHIFT) == lax.shift_right_logical(col, CHUNK_SHIFT)
    one = lambda m: jnp.where(m, 1.0, 0.0).astype(BF16)
    return one(same & (col > row)), one(same), one(same & (col < row))


def _mask_dot(mask, x):
    hi = x.astype(BF16)
    r1 = x - hi.astype(F32)
    mid = r1.astype(BF16)
    lo = (r1 - mid.astype(F32)).astype(BF16)
    return _dot(mask, hi) + _dot(mask, mid) + _dot(mask, lo)


def _log_sigmoid(x):
    return jnp.minimum(x, 0.0) - jnp.log1p(jnp.exp(-jnp.abs(x)))


def _conv_taps(prev8, uc, w):
    ext = jnp.concatenate([prev8, uc], axis=0)
    s1 = pltpu.roll(ext, 1, 0)[SUBLANES:]
    s2 = pltpu.roll(ext, 2, 0)[SUBLANES:]
    return s2 * w[0:1] + s1 * w[1:2] + uc * w[2:3], s1, s2


def _z_specs(tile, idx):
    d_conv = 1024
    wide = lambda c: pl.BlockSpec((tile, d_conv), lambda i, c=c: (idx(i), c))
    half = lambda c: pl.BlockSpec((tile, d_conv // 2), lambda i, c=c: (idx(i), c))
    return [wide(0), wide(1), wide(2), half(6), half(7), wide(4), wide(5)]


def _mixer_fwd(z, alow, wgu, b_gate, convw, conv_g, gla_g):
    t = z.shape[0]
    tb, cpt = MIX_TILE, CHUNKS_PER_TILE
    d_conv = conv_g.shape[1]
    dv = gla_g.shape[1]
    dk = dv // 2
    d_k = GLA_HEADS * dk
    gw = d_conv // CONV_GROUPS
    scale = dk ** -0.5

    def body(cb_ref, cc_ref, ch_ref, q_ref, k_ref, v_ref, og_ref, al_ref, wgu_ref, bg_ref, cw_ref, cg_ref, gg_ref,
             y_ref, sall_ref, carry_ref, s_ref):
        @pl.when(pl.program_id(0) == 0)
        def _():
            carry_ref[...] = jnp.zeros_like(carry_ref)
            s_ref[...] = jnp.zeros_like(s_ref)

        uc = cc_ref[...] * ch_ref[...]
        conv, _, _ = _conv_taps(carry_ref[...], uc, cw_ref[...])
        carry_ref[...] = uc[tb - SUBLANES:]
        ypre = cb_ref[...] * conv
        cg = cg_ref[...]
        for g in range(CONV_GROUPS):
            sl = slice(g * gw, (g + 1) * gw)
            seg = ypre[:, sl]
            r = lax.rsqrt(jnp.mean(seg * seg, axis=-1, keepdims=True) + EPS)
            y_ref[:, sl] = (seg * r * cg[:, sl]).astype(BF16)

        later, same, _ = _chunk_masks(tb)
        pre = _dot(al_ref[...].astype(BF16), wgu_ref[...]) + bg_ref[...]
        la = _log_sigmoid(pre) * (1.0 / GATE_NORMALIZER)
        e_dec = _mask_dot(later, la)
        dec_all = jnp.exp(_mask_dot(same, la))
        kdec = (k_ref[...] * jnp.exp(e_dec)).astype(BF16)
        qs = (q_ref[...] * scale).astype(BF16)
        vb = v_ref[...].astype(BF16)
        gg = gg_ref[...]
        rows = [slice(c * CHUNK, (c + 1) * CHUNK) for c in range(cpt)]
        ks = [slice(h * dk, (h + 1) * dk) for h in range(GLA_HEADS)]
        vs = [slice(h * dv, (h + 1) * dv) for h in range(GLA_HEADS)]
        kvt = [[_dot(vb[rows[c], vs[h]], kdec[rows[c], ks[h]], _TN) for h in range(GLA_HEADS)] for c in range(cpt)]
        state = [s_ref[h] for h in range(GLA_HEADS)]
        states = []
        for c in range(cpt):
            state = [state[h] * dec_all[c * CHUNK:c * CHUNK + 1, ks[h]] + kvt[c][h] for h in range(GLA_HEADS)]
            states.append(state)
            for h in range(GLA_HEADS):
                sall_ref[c, h] = state[h]
        for h in range(GLA_HEADS):
            s_ref[h] = state[h]
        for h in range(GLA_HEADS):
            o = jnp.concatenate(
                [_dot(qs[rows[c], ks[h]], states[c][h].astype(BF16), _NT) for c in range(cpt)], axis=0)
            ro = lax.rsqrt(jnp.mean(o * o, axis=-1, keepdims=True) + EPS)
            ogs = og_ref[:, vs[h]]
            yg = o * ro * gg * (ogs * jax.nn.sigmoid(ogs))
            y_ref[:, d_conv + h * dv:d_conv + (h + 1) * dv] = yg.astype(BF16)

    full = lambda shape: pl.BlockSpec(shape, lambda i: (0,) * len(shape))
    return pl.pallas_call(
        body, name="mixer_fwd", grid=(t // tb,),
        in_specs=_z_specs(tb, lambda i: i) + [
            pl.BlockSpec((tb, LANES), lambda i: (i, 0)), full(wgu.shape), full(b_gate.shape), full(convw.shape),
            full(conv_g.shape), full(gla_g.shape)],
        out_specs=[pl.BlockSpec((tb, d_conv + GLA_HEADS * dv), lambda i: (i, 0)),
                   pl.BlockSpec((cpt, GLA_HEADS, dv, dk), lambda i: (i, 0, 0, 0))],
        out_shape=[jax.ShapeDtypeStruct((t, d_conv + GLA_HEADS * dv), BF16),
                   jax.ShapeDtypeStruct((t // CHUNK, GLA_HEADS, dv, dk), F32)],
        scratch_shapes=[pltpu.VMEM((SUBLANES, d_conv), F32), pltpu.VMEM((GLA_HEADS, dv, dk), F32)],
        compiler_params=_params(1),
    )(z, z, z, z, z, z, z, alow, wgu, b_gate, convw, conv_g, gla_g)


def _mixer_bwd(z, alow, dy, sall, wgu, b_gate, convw, conv_g, gla_g, behind=None):
    t = z.shape[0]
    tb, cpt = MIX_TILE, CHUNKS_PER_TILE
    nt = t // tb
    d_conv = conv_g.shape[1]
    dv = gla_g.shape[1]
    dk = dv // 2
    d_k = GLA_HEADS * dk
    gw = d_conv // CONV_GROUPS
    scale = dk ** -0.5
    rev = lambda i: nt - 1 - i
    dep_args, dep_specs = _behind(behind)

    def body(cb_ref, cc_ref, ch_ref, q_ref, k_ref, v_ref, og_ref, ccp_ref, chp_ref, al_ref, dy_ref, sall_ref, sprev_ref,
             wgu_ref, bg_ref, cw_ref, cg_ref, gg_ref, *rest):
        dz_ref, dzal_ref, dcw_ref, dcg_ref, dgg_ref, dbg_ref, dwgu_ref, dcarry_ref, gd_ref = rest[-9:]
        i = pl.program_id(0)

        @pl.when(i == 0)
        def _():
            dcarry_ref[...] = jnp.zeros_like(dcarry_ref)
            gd_ref[...] = jnp.zeros_like(gd_ref)
            dcw_ref[...] = jnp.zeros_like(dcw_ref)
            dcg_ref[...] = jnp.zeros_like(dcg_ref)
            dgg_ref[...] = jnp.zeros_like(dgg_ref)
            dbg_ref[...] = jnp.zeros_like(dbg_ref)
            dwgu_ref[...] = jnp.zeros_like(dwgu_ref)

        first = rev(i) == 0

        cb, cc, ch = cb_ref[...], cc_ref[...], ch_ref[...]
        w = cw_ref[...]
        uc = cc * ch
        prev8 = jnp.where(first, 0.0, ccp_ref[...] * chp_ref[...])
        conv, s1, s2 = _conv_taps(prev8, uc, w)
        ypre = cb * conv
        cg = cg_ref[...]
        dypre_parts = []
        for g in range(CONV_GROUPS):
            sl = slice(g * gw, (g + 1) * gw)
            seg = ypre[:, sl]
            r = lax.rsqrt(jnp.mean(seg * seg, axis=-1, keepdims=True) + EPS)
            yn = seg * r
            dyc = dy_ref[:, sl]
            dcg_ref[:, sl] += jnp.sum(dyc * yn, axis=0, keepdims=True)
            dyn = dyc * cg[:, sl]
            dypre_parts.append(r * (dyn - yn * jnp.mean(dyn * yn, axis=-1, keepdims=True)))
        dypre = jnp.concatenate(dypre_parts, axis=1)
        dconv = dypre * cb
        dz_ref[:, 0:d_conv] = (dypre * conv).astype(BF16)
        dcw_ref[0:1] += jnp.sum(dconv * s2, axis=0, keepdims=True)
        dcw_ref[1:2] += jnp.sum(dconv * s1, axis=0, keepdims=True)
        dcw_ref[2:3] += jnp.sum(dconv * uc, axis=0, keepdims=True)
        ext = jnp.concatenate([dconv, dcarry_ref[...]], axis=0)
        f1 = pltpu.roll(ext, tb + SUBLANES - 1, 0)[:tb]
        f2 = pltpu.roll(ext, tb + SUBLANES - 2, 0)[:tb]
        dcarry_ref[...] = dconv[:SUBLANES]
        duc = dconv * w[2:3] + f1 * w[1:2] + f2 * w[0:1]
        dz_ref[:, d_conv:2 * d_conv] = (duc * ch).astype(BF16)
        dz_ref[:, 2 * d_conv:3 * d_conv] = (duc * cc).astype(BF16)

        q_off = 3 * d_conv
        k_off = q_off + d_k
        v_off = k_off + d_k
        og_off = v_off + GLA_HEADS * dv
        later, same, earlier = _chunk_masks(tb)
        alb = al_ref[...].astype(BF16)
        pre = _dot(alb, wgu_ref[...]) + bg_ref[...]
        la = _log_sigmoid(pre) * (1.0 / GATE_NORMALIZER)
        exp_e = jnp.exp(_mask_dot(later, la))
        dec_all = jnp.exp(_mask_dot(same, la))
        kdec = k_ref[...] * exp_e
        kdec_b = kdec.astype(BF16)
        qs = (q_ref[...] * scale).astype(BF16)
        vb = v_ref[...].astype(BF16)
        gg = gg_ref[...]
        rows = [slice(c * CHUNK, (c + 1) * CHUNK) for c in range(cpt)]
        ks = [slice(h * dk, (h + 1) * dk) for h in range(GLA_HEADS)]
        vs = [slice(h * dv, (h + 1) * dv) for h in range(GLA_HEADS)]
        st_b = [[sall_ref[c, h].astype(BF16) for h in range(GLA_HEADS)] for c in range(cpt)]
        do_b = []
        dgg = jnp.zeros_like(gg)
        for h in range(GLA_HEADS):
            o = jnp.concatenate([_dot(qs[rows[c], ks[h]], st_b[c][h], _NT) for c in range(cpt)], axis=0)
            ro = lax.rsqrt(jnp.mean(o * o, axis=-1, keepdims=True) + EPS)
            on = o * ro
            ogs = og_ref[:, vs[h]]
            sg = jax.nn.sigmoid(ogs)
            gate = ogs * sg
            dyg = dy_ref[:, d_conv + h * dv:d_conv + (h + 1) * dv]
            dgg = dgg + jnp.sum(dyg * on * gate, axis=0, keepdims=True)
            dz_ref[:, og_off + h * dv:og_off + (h + 1) * dv] = (
                dyg * on * gg * (sg * (1.0 + ogs * (1.0 - sg)))).astype(BF16)
            don = dyg * gg * gate
            do_b.append((ro * (don - on * jnp.mean(don * on, axis=-1, keepdims=True))).astype(BF16))
        dgg_ref[...] += dgg
        for h in range(GLA_HEADS):
            dq = jnp.concatenate([_dot(do_b[h][rows[c]], st_b[c][h]) for c in range(cpt)], axis=0)
            dz_ref[:, q_off + h * dk:q_off + (h + 1) * dk] = (dq * scale).astype(BF16)
        own = [[_dot(do_b[h][rows[c]], qs[rows[c], ks[h]], _TN) for h in range(GLA_HEADS)] for c in range(cpt)]
        carried = [gd_ref[h] for h in range(GLA_HEADS)]
        gt_b = [None] * cpt
        ddd = [None] * cpt
        for c in reversed(range(cpt)):
            gt = [own[c][h] + carried[h] for h in range(GLA_HEADS)]
            dec = [dec_all[c * CHUNK:c * CHUNK + 1, ks[h]] for h in range(GLA_HEADS)]
            carried = [gt[h] * dec[h] for h in range(GLA_HEADS)]
            if c > 0:
                st_prev = [sall_ref[c - 1, h] for h in range(GLA_HEADS)]
            else:
                st_prev = [jnp.where(first, 0.0, sprev_ref[0, h]) for h in range(GLA_HEADS)]
            ddec = [jnp.sum(gt[h] * st_prev[h], axis=0, keepdims=True) * dec[h] for h in range(GLA_HEADS)]
            ddd[c] = jnp.broadcast_to(jnp.concatenate(ddec, axis=1), (CHUNK, d_k))
            gt_b[c] = [gt[h].astype(BF16) for h in range(GLA_HEADS)]
        for h in range(GLA_HEADS):
            gd_ref[h] = carried[h]
        dkdec_cols = []
        for h in range(GLA_HEADS):
            dvh = jnp.concatenate([_dot(kdec_b[rows[c], ks[h]], gt_b[c][h], _NT) for c in range(cpt)], axis=0)
            dz_ref[:, v_off + h * dv:v_off + (h + 1) * dv] = dvh.astype(BF16)
            dkdec_cols.append(jnp.concatenate([_dot(vb[rows[c], vs[h]], gt_b[c][h]) for c in range(cpt)], axis=0))
        dkdec = jnp.concatenate(dkdec_cols, axis=1)
        dz_ref[:, k_off:k_off + d_k] = (dkdec * exp_e).astype(BF16)
        dla = _mask_dot(earlier, dkdec * kdec) + jnp.concatenate(ddd, axis=0)
        dpre = dla * (1.0 / GATE_NORMALIZER) * jax.nn.sigmoid(-pre)
        dbg_ref[...] += jnp.sum(dpre, axis=0, keepdims=True)
        dpre_b = dpre.astype(BF16)
        dwgu_ref[...] += _dot(alb, dpre_b, _TN)
        dzal_ref[...] = _dot(dpre_b, wgu_ref[...], _NT).astype(BF16)

    full = lambda shape: pl.BlockSpec(shape, lambda i: (0,) * len(shape))
    prev_rows = lambda c: pl.BlockSpec(
        (SUBLANES, d_conv), lambda i, c=c: (jnp.maximum(rev(i) * (tb // SUBLANES) - 1, 0), c))
    n_z = 3 * d_conv + 2 * d_k + 2 * GLA_HEADS * dv
    return pl.pallas_call(
        body, name="mixer_bwd", grid=(nt,),
        in_specs=_z_specs(tb, rev) + [
            prev_rows(1), prev_rows(2),
            pl.BlockSpec((tb, LANES), lambda i: (rev(i), 0)),
            pl.BlockSpec((tb, d_conv + GLA_HEADS * dv), lambda i: (rev(i), 0)),
            pl.BlockSpec((cpt, GLA_HEADS, dv, dk), lambda i: (rev(i), 0, 0, 0)),
            pl.BlockSpec((1, GLA_HEADS, dv, dk), lambda i: (jnp.maximum(rev(i) * cpt - 1, 0), 0, 0, 0)),
            full(wgu.shape), full(b_gate.shape), full(convw.shape), full(conv_g.shape), full(gla_g.shape)]
        + dep_specs,
        out_specs=[pl.BlockSpec((tb, n_z), lambda i: (rev(i), 0)), pl.BlockSpec((tb, LANES), lambda i: (rev(i), 0)),
                   full(convw.shape), full(conv_g.shape), full(gla_g.shape), full(b_gate.shape), full(wgu.shape)],
        out_shape=[jax.ShapeDtypeStruct((t, n_z), BF16), jax.ShapeDtypeStruct((t, LANES), BF16),
                   jax.ShapeDtypeStruct(convw.shape, F32), jax.ShapeDtypeStruct(conv_g.shape, F32),
                   jax.ShapeDtypeStruct(gla_g.shape, F32), jax.ShapeDtypeStruct(b_gate.shape, F32),
                   jax.ShapeDtypeStruct(wgu.shape, F32)],
        scratch_shapes=[pltpu.VMEM((SUBLANES, d_conv), F32), pltpu.VMEM((GLA_HEADS, dv, dk), F32)],
        compiler_params=_params(1),
    )(z, z, z, z, z, z, z, z, z, alow, dy, sall, sall, wgu, b_gate, convw, conv_g, gla_g, *dep_args)


def _adamw_math(g, w, m, v):
    m = ADAM_B1 * m + (1.0 - ADAM_B1) * g
    v = ADAM_B2 * v + (1.0 - ADAM_B2) * (g * g)
    m_hat = m / (1.0 - ADAM_B1 ** ADAM_STEP)
    v_hat = v / (1.0 - ADAM_B2 ** ADAM_STEP)
    delta = -ADAM_LR * (m_hat / (jnp.sqrt(v_hat) + ADAM_EPS) + ADAM_WD * w)
    return delta, m, v


def _adamw(name, parts, w, m, v, tr):
    r, c = w.shape
    n_parts = parts.shape[0]

    def body(p_ref, w_ref, m_ref, v_ref, g_ref, d_ref, nm_ref, nv_ref):
        g = p_ref[0].astype(F32)
        for j in range(1, n_parts):
            g = g + p_ref[j].astype(F32)
        g_ref[...] = g
        d_ref[...], nm_ref[...], nv_ref[...] = _adamw_math(g, w_ref[...], m_ref[...], v_ref[...])

    blk = pl.BlockSpec((tr, c), lambda i: (i, 0))
    return pl.pallas_call(
        body, name=name, grid=(r // tr,),
        in_specs=[pl.BlockSpec((n_parts, tr, c), lambda i: (0, i, 0)), blk, blk, blk],
        out_specs=[blk] * 4, out_shape=[jax.ShapeDtypeStruct((r, c), F32)] * 4,
        compiler_params=_params(1),
    )(parts, w, m, v)


def _adamw_small(grads, ws, ms, vs):
    n = len(grads)

    def body(*refs):
        g, w, m, v = (refs[k * n:(k + 1) * n] for k in range(4))
        d_out, m_out, v_out = (refs[(4 + k) * n:(5 + k) * n] for k in range(3))
        for i in range(n):
            d_out[i][...], m_out[i][...], v_out[i][...] = _adamw_math(g[i][...], w[i][...], m[i][...], v[i][...])

    vmem = pl.BlockSpec(memory_space=pltpu.VMEM)
    outs = pl.pallas_call(
        body, name="adamw_small", out_shape=[jax.ShapeDtypeStruct(w.shape, F32) for w in ws] * 3,
        in_specs=[vmem] * (4 * n), out_specs=[vmem] * (3 * n),
    )(*grads, *ws, *ms, *vs)
    return [outs[:n], outs[n:2 * n], outs[2 * n:]]


def _sum_partials(parts):
    n_parts, rows, lanes = parts.shape

    def body(p_ref, o_ref):
        g = p_ref[0]
        for j in range(1, n_parts):
            g = g + p_ref[j]
        o_ref[...] = g

    return pl.pallas_call(
        body, name="sum_small_partials", out_shape=jax.ShapeDtypeStruct((rows, lanes), F32),
        in_specs=[pl.BlockSpec(memory_space=pltpu.VMEM)], out_specs=pl.BlockSpec(memory_space=pltpu.VMEM),
    )(parts)


def _pack_rows(vectors, rows):
    flat = jnp.concatenate([a.reshape(-1).astype(F32) for a in vectors])
    return jnp.pad(flat, (0, rows * LANES - flat.shape[0])).reshape(rows, LANES)


def _unpack_rows(block, shapes):
    flat = block.reshape(-1)
    out, off = [], 0
    for s in shapes:
        n = 1
        for dim in s:
            n *= dim
        out.append(flat[off:off + n].reshape(s))
        off += n
    return out


def kernel(x, norm1_g, w_in, w_gate_up, b_gate, conv_w, conv_norm_g, gla_norm_g, w_out, norm2_g, w_ff1, w_ff2, norm_f_g, loss_target, m_norm1_g, m_w_in, m_w_gate_up, m_b_gate, m_conv_w, m_conv_norm_g, m_gla_norm_g, m_w_out, m_norm2_g, m_w_ff1, m_w_ff2, m_norm_f_g, v_norm1_g, v_w_in, v_w_gate_up, v_b_gate, v_conv_w, v_conv_norm_g, v_gla_norm_g, v_w_out, v_norm2_g, v_w_ff1, v_w_ff2, v_norm_f_g):
    me = _device_index()
    x2d, tgt = x[0], loss_target[0]
    t, d = x2d.shape
    d_in_shard = w_in.shape[2]
    d_in = N_DEV * d_in_shard
    n_main = d_in - GATE_RANK
    d_conv = conv_norm_g.shape[1]
    d_k = b_gate.shape[1]
    d_ff = N_DEV * w_ff1.shape[2]
    wmv = dict(
        norm1_g=(norm1_g, m_norm1_g, v_norm1_g), w_in=(w_in, m_w_in, v_w_in),
        w_gate_up=(w_gate_up, m_w_gate_up, v_w_gate_up), b_gate=(b_gate, m_b_gate, v_b_gate),
        conv_w=(conv_w, m_conv_w, v_conv_w), conv_norm_g=(conv_norm_g, m_conv_norm_g, v_conv_norm_g),
        gla_norm_g=(gla_norm_g, m_gla_norm_g, v_gla_norm_g), w_out=(w_out, m_w_out, v_w_out),
        norm2_g=(norm2_g, m_norm2_g, v_norm2_g), w_ff1=(w_ff1, m_w_ff1, v_w_ff1), w_ff2=(w_ff2, m_w_ff2, v_w_ff2),
        norm_f_g=(norm_f_g, m_norm_f_g, v_norm_f_g))

    small_rows = 16
    first_level = (SIBLING,) + SAME_CORE_PEERS
    win_shard = w_in[0].astype(BF16)
    in_send, in_recv, in_src, in_land, token = _exchange_start(
        "all_gather_start_w_in", [win_shard], [_land_zone(win_shard)], scatter=False, masks=[first_level])
    _, wgu_t, cw_t, wout_t, w1_t, w2_t = lax.optimization_barrier((token, w_gate_up, conv_w, w_out, w_ff1, w_ff2))
    small_shard = _pack_rows([wgu_t[0], cw_t[0]], small_rows)
    shards = [small_shard, wout_t[0].astype(BF16), w1_t[0].astype(BF16), w2_t[0].astype(BF16)]
    ag_send, ag_recv, ag_src, ag_land, token = _exchange_start(
        "all_gather_start", shards, [_land_zone(s) for s in shards], scatter=False, behind=token)

    def gathered(k, name, after):
        return _exchange_wait(name, ag_send[k], ag_recv[k], ag_src[k], ag_land[k], after, scatter=False)

    u = _rmsnorm(x2d, norm1_g, behind=token)
    tied = lax.optimization_barrier((token, w_in, m_w_in, v_w_in))
    wmv["w_in"] = tuple(tied[1:])
    small_g = gathered(0, "all_gather_wait_small", [u] + [a[0] for a in wmv["w_in"]])
    win_level1 = _exchange_wait(
        "all_gather_wait_w_in", in_send[0], in_recv[0], in_src[0], in_land[0], small_g, scatter=False,
        masks=first_level)
    win_g = _forward_wait("all_gather_wait_w_in_forwarded", *_forward_start("all_gather_forward_w_in", win_level1))
    w_main, w_alow = _shards_to_columns(win_g, n_main)
    small_flat = small_g.reshape(N_DEV, -1)
    n_wgu = GATE_RANK * (d_k // N_DEV)
    wgu_full = small_flat[:, :n_wgu].reshape(N_DEV, GATE_RANK, d_k // N_DEV).transpose(1, 0, 2).reshape(GATE_RANK, d_k)
    conv_w_full = small_flat[:, n_wgu:n_wgu + (d_conv // N_DEV) * CONV_WIDTH].reshape(d_conv, CONV_WIDTH)
    wgu_pad = jnp.pad(wgu_full, ((0, LANES - GATE_RANK), (0, 0))).astype(BF16)
    convw_taps = jnp.pad(conv_w_full.T, ((0, SUBLANES - CONV_WIDTH), (0, 0)))

    get_w_out = lambda after: gathered(1, "all_gather_wait_w_out", after).reshape(-1, d)
    get_w1 = lambda after: gathered(2, "all_gather_wait_w_ff1", after)
    get_w2 = lambda after: gathered(3, "all_gather_wait_w_ff2", after).reshape(d_ff, d)

    in_flight = {}

    def send_partials(name, parts):
        own = lax.dynamic_index_in_dim(parts, me, axis=0, keepdims=False)
        send, recv, src, land, token = _exchange_start("scatter_start_" + name, [parts], [_land_zone(own)], scatter=True)
        in_flight[name] = (send[0], recv[0], src[0], land[0])
        return token

    def on_grad(name, value):
        if name == "w_in":
            main, alow_part = value
            value = _columns_to_shards(main, alow_part, N_DEV, d_in_shard)
        elif name in ("w_out", "w_ff2"):
            value = value.reshape(N_DEV, -1, d)
        return send_partials(name, value)

    grads = _local_step(x2d, u, tgt, norm1_g, w_main, w_alow, wgu_pad, b_gate, convw_taps, conv_norm_g, gla_norm_g,
                        norm2_g, norm_f_g, get_w_out, get_w1, get_w2, on_grad)
    grad_x = grads["x"]

    small_shapes = [(1, d), (1, d_k), (1, d_conv), (1, gla_norm_g.shape[1]), (1, d), (d,),
                    (GATE_RANK, d_k), (d_conv, CONV_WIDTH), (1,)]
    small_grad_rows = 152
    small_part = _pack_rows(
        [grads["norm1_g"], grads["b_gate"], grads["conv_norm_g"], grads["gla_norm_g"], grads["norm2_g"],
         grads["norm_f_g"], grads["w_gate_up"][:GATE_RANK], grads["conv_w"][:CONV_WIDTH].T, grads["loss"][0, 0]],
        small_grad_rows)
    small_token = send_partials("small", jnp.broadcast_to(small_part[None], (N_DEV, small_grad_rows, LANES)))

    def received(name, after):
        send, recv, src, land = in_flight[name]
        return _exchange_wait("scatter_wait_" + name, send, recv, src, land, after, scatter=True)

    gin_r, gout_r, g1_r, g2_r = (received(nm, [grad_x, small_token]) for nm in ("w_in", "w_out", "w_ff1", "w_ff2"))
    get_small = lambda after: received("small", after)
    return _update(me, gin_r, gout_r, g1_r, g2_r, get_small, small_shapes, grad_x, wmv)


def _local_step(x2d, u, tgt, norm1_g, w_main, w_alow, wgu_pad, b_gate, convw_taps, conv_norm_g, gla_norm_g,
                norm2_g, norm_f_g, get_w_out, get_w1, get_w2, on_grad):
    t, d = x2d.shape
    n_main = w_main.shape[1]

    z, alow = _inproj(u, w_main, w_alow)
    y, sall = _mixer_fwd(z, alow, wgu_pad, b_gate, convw_taps, conv_norm_g, gla_norm_g)
    w_out_full = get_w_out(y)
    x1, h = _outproj(y, w_out_full, x2d, norm2_g)
    w1g = get_w1(h)
    a = _ff1(h, w1g)
    w2_full = get_w2(a)
    d_ff = w2_full.shape[0]
    x2 = _ff2(a, w2_full, x1)
    dx2, dx2b, loss_part, d_normf = _loss_head(x2, norm_f_g.reshape(1, d), tgt)

    tk = min(4096, t)
    nk = t // tk
    da = _dff2(dx2b, w2_full, a)
    dw2 = _tn_matmul(
        "dw_ff2", a, dx2b, (d_ff // 1024, d // 1024, nk),
        pl.BlockSpec((tk, 1024), lambda m, j, kk: (kk, m)), pl.BlockSpec((tk, 1024), lambda m, j, kk: (kk, j)),
        jax.ShapeDtypeStruct((d_ff, d), BF16), pl.BlockSpec((1024, 1024), lambda m, j, kk: (m, j)), (1024, 1024),
        a_fn=_relu_sq)
    token = on_grad("w_ff2", dw2)
    f_shard = d_ff // N_DEV
    dw1 = _tn_matmul(
        "dw_ff1", h, da, (N_DEV, d // 1024, nk),
        pl.BlockSpec((tk, 1024), lambda g, m, kk: (kk, m)), pl.BlockSpec((tk, f_shard), lambda g, m, kk: (kk, g)),
        jax.ShapeDtypeStruct((N_DEV, d, f_shard), BF16), pl.BlockSpec((None, 1024, f_shard), lambda g, m, kk: (g, m, 0)),
        (1024, f_shard), behind=token)
    token = on_grad("w_ff1", dw1)
    dh = _dh(da, w1g, behind=token)
    dx1, dx1b, d_norm2 = _norm_bwd("norm2_bwd", dh, x1, norm2_g, dx2, with_bf16=True)
    dy = _nt_matmul("dy", dx1b, w_out_full)
    dwout = _tn_matmul(
        "dw_out", y, dx1b, (d // 1024, d // 1024, nk),
        pl.BlockSpec((tk, 1024), lambda m, j, kk: (kk, m)), pl.BlockSpec((tk, 1024), lambda m, j, kk: (kk, j)),
        jax.ShapeDtypeStruct((d, d), BF16), pl.BlockSpec((1024, 1024), lambda m, j, kk: (m, j)), (1024, 1024))
    token = on_grad("w_out", dwout)
    dz, dzal, d_convw, d_convg, d_glag, d_bgate, d_wgu = _mixer_bwd(
        z, alow, dy, sall, wgu_pad, b_gate, convw_taps, conv_norm_g, gla_norm_g, behind=token)
    token = on_grad("w_in", _dw_in(u, dz, dzal, tk))
    du = _du(dz, w_main, dzal, w_alow, behind=token)
    grad_x, d_norm1 = _norm_bwd("norm1_bwd", du, x2d, norm1_g, dx1, with_bf16=False)
    return dict(x=grad_x, loss=loss_part, norm1_g=d_norm1, w_gate_up=d_wgu, b_gate=d_bgate, conv_w=d_convw,
                conv_norm_g=d_convg, gla_norm_g=d_glag, norm2_g=d_norm2, norm_f_g=d_normf)


_WEIGHT_ORDER = ("norm1_g", "w_in", "w_gate_up", "b_gate", "conv_w", "conv_norm_g", "gla_norm_g", "w_out", "norm2_g",
                 "w_ff1", "w_ff2", "norm_f_g")
_SMALL_ORDER = ("norm1_g", "b_gate", "conv_norm_g", "gla_norm_g", "norm2_g", "norm_f_g", "w_gate_up", "conv_w")
def _update(me, gin_r, gout_r, g1_r, g2_r, get_small, small_shapes, grad_x, wmv):
    big = {
        "w_in": _adamw("adamw_w_in", gin_r, *(a[0] for a in wmv["w_in"]), 256),
        "w_out": _adamw("adamw_w_out", gout_r, *(a[0] for a in wmv["w_out"]), 128),
        "w_ff1": _adamw("adamw_w_ff1", g1_r, *(a[0] for a in wmv["w_ff1"]), 256),
        "w_ff2": _adamw("adamw_w_ff2", g2_r, *(a[0] for a in wmv["w_ff2"]), 128),
    }

    wgu_cols = wmv["w_gate_up"][0].shape[2]
    cw_rows = wmv["conv_w"][0].shape[1]

    small_r = get_small([big[nm][3] for nm in ("w_in", "w_out", "w_ff1", "w_ff2")])
    summed = _unpack_rows(_sum_partials(small_r), small_shapes)
    summed[6] = lax.dynamic_slice_in_dim(summed[6], me * wgu_cols, wgu_cols, axis=1)
    summed[7] = lax.dynamic_slice_in_dim(summed[7], me * cw_rows, cw_rows, axis=0)
    as_2d = lambda a: a.reshape((1, -1) if a.ndim == 1 else a.shape[-2:])
    grads_2d = [as_2d(g) for g in summed[:len(_SMALL_ORDER)]]
    small = _adamw_small(grads_2d, *[[as_2d(wmv[nm][k]) for nm in _SMALL_ORDER] for k in range(3)])
    small = [grads_2d] + small

    outs = []
    for k in range(4):
        for nm in _WEIGHT_ORDER:
            if nm in big:
                outs.append(big[nm][k][None])
            else:
                outs.append(small[k][_SMALL_ORDER.index(nm)].reshape(wmv[nm][0].shape))
    loss = summed[8][0]
    return (loss, grad_x[None], *outs)
```

```python
import functools

import jax
import jax.numpy as jnp
from jax import lax
from jax.experimental import pallas as pl
from jax.experimental.pallas import tpu as pltpu

F32 = jnp.float32
BF16 = jnp.bfloat16

N_DEV = 8
CHUNK = 64
GLA_HEADS = 4
CONV_GROUPS = 8
CONV_WIDTH = 3
GATE_RANK = 16
GATE_NORMALIZER = 16.0
EPS = 1e-6
ADAM_LR = 0.001
ADAM_B1 = 0.9
ADAM_B2 = 0.999
ADAM_EPS = 1e-08
ADAM_WD = 0.01
ADAM_STEP = 10

LANES = 128
SUBLANES = 8
VMEM_LIMIT = 56 << 20

_NN = (((1,), (0,)), ((), ()))
_NT = (((1,), (1,)), ((), ()))
_TN = (((0,), (0,)), ((), ()))


def _dot(a, b, dims=_NN):
    return lax.dot_general(a, b, dims, preferred_element_type=F32)


def _params(n_grid):
    return pltpu.CompilerParams(dimension_semantics=("arbitrary",) * n_grid, vmem_limit_bytes=VMEM_LIMIT)


def _relu_sq(a):
    r = jnp.maximum(a, 0.0)
    return r * r


def _device_index():
    return 4 * lax.axis_index("x") + 2 * lax.axis_index("y") + lax.axis_index("c")


def _peer(mask):
    x, y, c = lax.axis_index("x"), lax.axis_index("y"), lax.axis_index("c")
    return (x ^ ((mask >> 2) & 1), y ^ ((mask >> 1) & 1), c ^ (mask & 1))


_HBM_SPEC = pl.BlockSpec(memory_space=pltpu.HBM)
_SEM_SPEC = pl.BlockSpec(memory_space=pltpu.SEMAPHORE)
_SIDE_EFFECT = pltpu.SideEffectType.DATAFLOW_SIDE_EFFECTING
N_PEERS = N_DEV - 1


def _exchange_copy(src_ref, land_ref, send_sems, recv_sems, mask, scatter, arriving):
    me = _device_index()
    src = src_ref.at[me ^ mask] if scatter else src_ref
    dst = land_ref.at[(me ^ mask) if arriving else me]
    return pltpu.make_async_remote_copy(
        src_ref=src, dst_ref=dst, send_sem=send_sems.at[mask - 1], recv_sem=recv_sems.at[mask - 1],
        device_id=_peer(mask), device_id_type=pl.DeviceIdType.MESH)


def _land_zone(own):
    zone = lax.empty((N_DEV,) + own.shape, own.dtype)
    return lax.dynamic_update_slice(zone, own[None], (_device_index(),) + (0,) * own.ndim)


ALL_PEERS = tuple(range(1, N_DEV))
SIBLING = 1
SAME_CORE_PEERS = (2, 4, 6)


def _exchange_start(name, srcs, lands, scatter, masks=None, behind=None):
    n = len(srcs)
    masks = masks or [ALL_PEERS] * n
    dep_args = [] if behind is None else [behind]

    def body(*refs):
        src, land = refs[:n], refs[n:2 * n]
        outs = refs[2 * n + len(dep_args):]
        send_sems, recv_sems = outs[:n], outs[n:2 * n]
        token = refs[-1]
        for a in range(n):
            for mask in masks[a]:
                _exchange_copy(src[a], land[a], send_sems[a], recv_sems[a], mask, scatter, False).start()
        token[...] = jnp.zeros_like(token)

    hbm = lambda a: pltpu.HBM(a.shape, a.dtype)
    outs = pl.pallas_call(
        body, name=name,
        out_shape=([pltpu.SemaphoreType.DMA((N_PEERS,))] * (2 * n) + [hbm(a) for a in srcs] + [hbm(a) for a in lands]
                   + [jax.ShapeDtypeStruct((SUBLANES, LANES), F32)]),
        in_specs=[_HBM_SPEC] * (2 * n) + [pl.BlockSpec(memory_space=pl.ANY)] * len(dep_args),
        out_specs=[_SEM_SPEC] * (2 * n) + [_HBM_SPEC] * (2 * n) + [pl.BlockSpec(memory_space=pltpu.VMEM)],
        input_output_aliases={a: 2 * n + a for a in range(2 * n)},
        compiler_params=pltpu.CompilerParams(has_side_effects=_SIDE_EFFECT),
    )(*[pltpu.with_memory_space_constraint(a, pltpu.HBM) for a in list(srcs) + list(lands)], *dep_args)
    send_sems, recv_sems = outs[:n], outs[n:2 * n]
    src_thru, land_thru = outs[2 * n:3 * n], outs[3 * n:4 * n]
    return send_sems, recv_sems, src_thru, land_thru, outs[-1]


def _exchange_wait(name, send_sems, recv_sems, src_thru, land_thru, after, scatter, masks=ALL_PEERS):
    after = list(after) if isinstance(after, (list, tuple)) else [after]

    def body(src_ref, land_ref, send_ref, recv_ref, *rest):
        for mask in masks:
            cp = _exchange_copy(src_ref, land_ref, send_ref, recv_ref, mask, scatter, True)
            cp.wait_send()
            cp.wait_recv()

    return pl.pallas_call(
        body, name=name,
        out_shape=(pltpu.HBM(src_thru.shape, src_thru.dtype), pltpu.HBM(land_thru.shape, land_thru.dtype)),
        in_specs=[_HBM_SPEC, _HBM_SPEC, _SEM_SPEC, _SEM_SPEC] + [pl.BlockSpec(memory_space=pl.ANY)] * len(after),
        out_specs=(_HBM_SPEC, _HBM_SPEC), input_output_aliases={0: 0, 1: 1},
        compiler_params=pltpu.CompilerParams(has_side_effects=_SIDE_EFFECT),
    )(src_thru, land_thru, send_sems, recv_sems, *after)[1]


def _forward_copy(land_ref, send_sems, recv_sems, k, arriving):
    me = _device_index()
    slot = me ^ SAME_CORE_PEERS[k]
    return pltpu.make_async_remote_copy(
        src_ref=land_ref.at[slot], dst_ref=land_ref.at[(slot ^ SIBLING) if arriving else slot],
        send_sem=send_sems.at[k], recv_sem=recv_sems.at[k],
        device_id=_peer(SIBLING), device_id_type=pl.DeviceIdType.MESH)


def _forward_start(name, land):
    n_fwd = len(SAME_CORE_PEERS)

    def body(land_ref, send_sems, recv_sems, land_thru):
        for k in range(n_fwd):
            _forward_copy(land_ref, send_sems, recv_sems, k, False).start()

    send, recv, thru = pl.pallas_call(
        body, name=name,
        out_shape=[pltpu.SemaphoreType.DMA((n_fwd,)), pltpu.SemaphoreType.DMA((n_fwd,)), pltpu.HBM(land.shape, land.dtype)],
        in_specs=[_HBM_SPEC], out_specs=[_SEM_SPEC, _SEM_SPEC, _HBM_SPEC], input_output_aliases={0: 2},
        compiler_params=pltpu.CompilerParams(has_side_effects=_SIDE_EFFECT),
    )(pltpu.with_memory_space_constraint(land, pltpu.HBM))
    return send, recv, thru


def _forward_wait(name, send_sems, recv_sems, land_thru):
    def body(land_ref, send_ref, recv_ref, got_ref):
        for k in range(len(SAME_CORE_PEERS)):
            cp = _forward_copy(land_ref, send_ref, recv_ref, k, True)
            cp.wait_send()
            cp.wait_recv()

    return pl.pallas_call(
        body, name=name, out_shape=pltpu.HBM(land_thru.shape, land_thru.dtype),
        in_specs=[_HBM_SPEC, _SEM_SPEC, _SEM_SPEC], out_specs=_HBM_SPEC, input_output_aliases={0: 0},
        compiler_params=pltpu.CompilerParams(has_side_effects=_SIDE_EFFECT),
    )(land_thru, send_sems, recv_sems)


def _shards_to_columns(g, n_main, tr=256):
    n_dev, d, s = g.shape

    def body(g_ref, main_ref, rest_ref):
        for j in range(n_dev):
            lo, hi = j * s, (j + 1) * s
            if hi <= n_main:
                main_ref[:, lo:hi] = g_ref[j]
            else:
                main_ref[:, lo:n_main] = g_ref[j, :, 0:n_main - lo]
                rest_ref[...] = jnp.zeros_like(rest_ref)
                rest_ref[:, 0:hi - n_main] = g_ref[j, :, n_main - lo:s]

    return pl.pallas_call(
        body, grid=(d // tr,), name="shards_to_columns",
        in_specs=[pl.BlockSpec((n_dev, tr, s), lambda i: (0, i, 0))],
        out_specs=[pl.BlockSpec((tr, n_main), lambda i: (i, 0)), pl.BlockSpec((tr, LANES), lambda i: (i, 0))],
        out_shape=[jax.ShapeDtypeStruct((d, n_main), g.dtype), jax.ShapeDtypeStruct((d, LANES), g.dtype)],
        compiler_params=_params(1),
    )(g)


def _columns_to_shards(main, rest, n_dev, s, tr=256):
    d, n_main = main.shape
    assert (n_dev - 1) * s <= n_main < n_dev * s

    def body(main_ref, rest_ref, o_ref):
        for j in range(n_dev):
            lo, hi = j * s, (j + 1) * s
            if hi <= n_main:
                o_ref[j] = main_ref[:, lo:hi]
            else:
                o_ref[j, :, 0:n_main - lo] = main_ref[:, lo:n_main]
                o_ref[j, :, n_main - lo:s] = rest_ref[:, 0:hi - n_main]

    return pl.pallas_call(
        body, grid=(d // tr,), name="columns_to_shards",
        in_specs=[pl.BlockSpec((tr, n_main), lambda i: (i, 0)), pl.BlockSpec((tr, LANES), lambda i: (i, 0))],
        out_specs=pl.BlockSpec((n_dev, tr, s), lambda i: (0, i, 0)),
        out_shape=jax.ShapeDtypeStruct((n_dev, d, s), main.dtype),
        compiler_params=_params(1),
    )(main, rest)


def _rmsnorm(x, g, tr=512, behind=None):
    t, d = x.shape
    tr = min(tr, t)
    dep_args, dep_specs = _behind(behind)

    def body(x_ref, g_ref, *rest):
        u_ref = rest[-1]
        xf = x_ref[...]
        r = lax.rsqrt(jnp.mean(xf * xf, axis=-1, keepdims=True) + EPS)
        u_ref[...] = (xf * r * g_ref[...]).astype(BF16)

    return pl.pallas_call(
        body, name="rmsnorm1", grid=(t // tr,),
        in_specs=[pl.BlockSpec((tr, d), lambda i: (i, 0)), pl.BlockSpec((1, d), lambda i: (0, 0))] + dep_specs,
        out_specs=pl.BlockSpec((tr, d), lambda i: (i, 0)),
        out_shape=jax.ShapeDtypeStruct((t, d), BF16),
        compiler_params=_params(1),
    )(x, g, *dep_args)


def _inproj(u, w_main, w_alow, tm=1024, tn=1024):
    t, d = u.shape
    tm = min(tm, t)
    n = w_main.shape[1]

    def body(u_ref, w_ref, wa_ref, z_ref, al_ref):
        @pl.when(pl.program_id(1) == 0)
        def _():
            al_ref[...] = _dot(u_ref[...], wa_ref[...])

        z_ref[...] = _dot(u_ref[...], w_ref[...])

    return pl.pallas_call(
        body, name="inproj", grid=(t // tm, n // tn),
        in_specs=[pl.BlockSpec((tm, d), lambda m, j: (m, 0)), pl.BlockSpec((d, tn), lambda m, j: (0, j)),
                  pl.BlockSpec((d, LANES), lambda m, j: (0, 0))],
        out_specs=[pl.BlockSpec((tm, tn), lambda m, j: (m, j)), pl.BlockSpec((tm, LANES), lambda m, j: (m, 0))],
        out_shape=[jax.ShapeDtypeStruct((t, n), F32), jax.ShapeDtypeStruct((t, LANES), F32)],
        compiler_params=_params(2),
    )(u, w_main, w_alow)


def _outproj(y, w_out, x, g2, tm=512):
    t, d = x.shape
    tm = min(tm, t)
    k = y.shape[1]

    def body(y_ref, w_ref, x_ref, g_ref, x1_ref, h_ref):
        x1 = x_ref[...] + _dot(y_ref[...], w_ref[...])
        x1_ref[...] = x1
        r = lax.rsqrt(jnp.mean(x1 * x1, axis=-1, keepdims=True) + EPS)
        h_ref[...] = (x1 * r * g_ref[...]).astype(BF16)

    return pl.pallas_call(
        body, name="outproj_rmsnorm", grid=(t // tm,),
        in_specs=[pl.BlockSpec((tm, k), lambda m: (m, 0)), pl.BlockSpec((k, d), lambda m: (0, 0)),
                  pl.BlockSpec((tm, d), lambda m: (m, 0)), pl.BlockSpec((1, d), lambda m: (0, 0))],
        out_specs=[pl.BlockSpec((tm, d), lambda m: (m, 0)), pl.BlockSpec((tm, d), lambda m: (m, 0))],
        out_shape=[jax.ShapeDtypeStruct((t, d), F32), jax.ShapeDtypeStruct((t, d), BF16)],
        compiler_params=_params(1),
    )(y, w_out, x, g2)


def _ff1(h, w1g, tm=1024):
    t, d = h.shape
    tm = min(tm, t)
    g, _, f = w1g.shape

    def body(h_ref, w_ref, a_ref):
        a_ref[...] = _dot(h_ref[...], w_ref[...]).astype(BF16)

    return pl.pallas_call(
        body, name="ff1", grid=(t // tm, g),
        in_specs=[pl.BlockSpec((tm, d), lambda m, j: (m, 0)), pl.BlockSpec((None, d, f), lambda m, j: (j, 0, 0))],
        out_specs=pl.BlockSpec((tm, f), lambda m, j: (m, j)),
        out_shape=jax.ShapeDtypeStruct((t, g * f), BF16),
        compiler_params=_params(2),
    )(h, w1g)


def _ff2(a, w2, x1, tm=1024, tn=1024, tk=2048):
    t, f = a.shape
    tm = min(tm, t)
    d = w2.shape[1]

    def body(a_ref, w_ref, x1_ref, o_ref):
        @pl.when(pl.program_id(2) == 0)
        def _():
            o_ref[...] = x1_ref[...]

        o_ref[...] += _dot(_relu_sq(a_ref[...]), w_ref[...])

    return pl.pallas_call(
        body, name="ff2_residual", grid=(t // tm, d // tn, f // tk),
        in_specs=[pl.BlockSpec((tm, tk), lambda m, j, kk: (m, kk)), pl.BlockSpec((tk, tn), lambda m, j, kk: (kk, j)),
                  pl.BlockSpec((tm, tn), lambda m, j, kk: (m, j))],
        out_specs=pl.BlockSpec((tm, tn), lambda m, j, kk: (m, j)),
        out_shape=jax.ShapeDtypeStruct((t, d), F32),
        compiler_params=_params(3),
    )(a, w2, x1)


def _dff2(dx2b, w2, a, tm=1024, tn=1024):
    t, d = dx2b.shape
    tm = min(tm, t)
    f = w2.shape[0]

    def body(g_ref, w_ref, a_ref, o_ref):
        dp = _dot(g_ref[...], w_ref[...], _NT)
        o_ref[...] = (dp * (2.0 * jnp.maximum(a_ref[...].astype(F32), 0.0))).astype(BF16)

    return pl.pallas_call(
        body, name="dff2", grid=(t // tm, f // tn),
        in_specs=[pl.BlockSpec((tm, d), lambda m, j: (m, 0)), pl.BlockSpec((tn, d), lambda m, j: (j, 0)),
                  pl.BlockSpec((tm, tn), lambda m, j: (m, j))],
        out_specs=pl.BlockSpec((tm, tn), lambda m, j: (m, j)),
        out_shape=jax.ShapeDtypeStruct((t, f), BF16),
        compiler_params=_params(2),
    )(dx2b, w2, a)


def _behind(token):
    if token is None:
        return [], []
    return [token], [pl.BlockSpec(token.shape, lambda *_: (0,) * token.ndim)]


def _tn_matmul(name, a, b, grid, a_spec, b_spec, out_shape, out_spec, acc_shape, a_fn=None, behind=None):
    nk = grid[-1]
    dep_args, dep_specs = _behind(behind)

    def body(a_ref, b_ref, *rest):
        o_ref, acc_ref = rest[-2:]
        kk = pl.program_id(len(grid) - 1)
        av = a_ref[...]
        if a_fn is not None:
            av = a_fn(av)
        part = _dot(av, b_ref[...], _TN)

        @pl.when(kk == 0)
        def _():
            acc_ref[...] = part

        @pl.when(kk > 0)
        def _():
            acc_ref[...] += part

        @pl.when(kk == nk - 1)
        def _():
            o_ref[...] = acc_ref[...].astype(o_ref.dtype)

    return pl.pallas_call(
        body, name=name, grid=grid, in_specs=[a_spec, b_spec] + dep_specs, out_specs=out_spec, out_shape=out_shape,
        scratch_shapes=[pltpu.VMEM(acc_shape, F32)], compiler_params=_params(len(grid)),
    )(a, b, *dep_args)


def _dw_in(u, dz, dzal, tk, tm=1024, tn=1024):
    t, d = u.shape
    n_main = dz.shape[1]
    nk = t // tk

    def body(a_ref, b_ref, al_ref, o_ref, oal_ref, acc_ref, accal_ref):
        j, kk = pl.program_id(1), pl.program_id(2)
        av = a_ref[...]

        def accumulate(acc, part, out):
            @pl.when(kk == 0)
            def _():
                acc[...] = part

            @pl.when(kk > 0)
            def _():
                acc[...] += part

            @pl.when(kk == nk - 1)
            def _():
                out[...] = acc[...].astype(out.dtype)

        accumulate(acc_ref, _dot(av, b_ref[...], _TN), o_ref)

        @pl.when(j == 0)
        def _():
            accumulate(accal_ref, _dot(av, al_ref[...], _TN), oal_ref)

    return pl.pallas_call(
        body, name="dw_in", grid=(d // tm, n_main // tn, nk),
        in_specs=[pl.BlockSpec((tk, tm), lambda m, j, kk: (kk, m)), pl.BlockSpec((tk, tn), lambda m, j, kk: (kk, j)),
                  pl.BlockSpec((tk, LANES), lambda m, j, kk: (kk, 0))],
        out_specs=[pl.BlockSpec((tm, tn), lambda m, j, kk: (m, j)), pl.BlockSpec((tm, LANES), lambda m, j, kk: (m, 0))],
        out_shape=[jax.ShapeDtypeStruct((d, n_main), BF16), jax.ShapeDtypeStruct((d, LANES), BF16)],
        scratch_shapes=[pltpu.VMEM((tm, tn), F32), pltpu.VMEM((tm, LANES), F32)],
        compiler_params=_params(3),
    )(u, dz, dzal)


class _SideAdamW:
    def __init__(self, side, grid):
        self.on = side is not None
        self.args, self.in_specs, self.out_specs, self.out_shape = [], [], [], []
        if not self.on:
            return
        parts, w, m, v = side
        n_parts, r, c = parts.shape
        steps = 1
        for extent in grid:
            steps *= extent
        rows = r // steps
        assert rows * steps == r and rows % (2 * SUBLANES) == 0, (r, steps)

        def step(*ids):
            lin = ids[0]
            for extent, idx in zip(grid[1:], ids[1:]):
                lin = lin * extent + idx
            return lin

        slab = pl.BlockSpec((rows, c), lambda *ids: (step(*ids), 0))
        self.args = [parts, w, m, v]
        self.in_specs = [pl.BlockSpec((n_parts, rows, c), lambda *ids: (0, step(*ids), 0)), slab, slab, slab]
        self.out_specs = [slab] * 4
        self.out_shape = [jax.ShapeDtypeStruct((r, c), F32)] * 4
        self.n_parts = n_parts

    def run(self, in_refs, out_refs):
        p_ref, w_ref, m_ref, v_ref = in_refs
        g = p_ref[0].astype(F32)
        for j in range(1, self.n_parts):
            g = g + p_ref[j].astype(F32)
        out_refs[0][...] = g
        out_refs[1][...], out_refs[2][...], out_refs[3][...] = _adamw_math(g, w_ref[...], m_ref[...], v_ref[...])


def _dh(da, w1g, tm=1024, tn=1024, shards_per_step=4, behind=None, side=None):
    t = da.shape[0]
    tm = min(tm, t)
    g, d, f = w1g.shape
    sps = shards_per_step
    grid = (t // tm, d // tn, g // sps)
    dep_args, dep_specs = _behind(behind)
    adam = _SideAdamW(side, grid)
    n_dep = len(dep_args)

    def body(a_ref, w_ref, *rest):
        o_ref = rest[n_dep + len(adam.args)]
        acc = _dot(a_ref[:, 0:f], w_ref[0], _NT)
        for s in range(1, sps):
            acc = acc + _dot(a_ref[:, s * f:(s + 1) * f], w_ref[s], _NT)

        @pl.when(pl.program_id(2) == 0)
        def _():
            o_ref[...] = acc

        @pl.when(pl.program_id(2) > 0)
        def _():
            o_ref[...] += acc

        if adam.on:
            adam.run(rest[n_dep:n_dep + 4], rest[n_dep + 5:])

    outs = pl.pallas_call(
        body, name="dh", grid=grid,
        in_specs=[pl.BlockSpec((tm, sps * f), lambda m, j, kk: (m, kk)),
                  pl.BlockSpec((sps, tn, f), lambda m, j, kk: (kk, j, 0))] + dep_specs + adam.in_specs,
        out_specs=[pl.BlockSpec((tm, tn), lambda m, j, kk: (m, j))] + adam.out_specs,
        out_shape=[jax.ShapeDtypeStruct((t, d), F32)] + adam.out_shape,
        compiler_params=_params(3),
    )(da, w1g, *dep_args, *adam.args)
    return outs[0], (outs[1:] if adam.on else None)


def _nt_matmul(name, a, b, tm=1024, tn=1024):
    t, k = a.shape
    tm = min(tm, t)
    n = b.shape[0]

    def body(a_ref, b_ref, o_ref):
        o_ref[...] = _dot(a_ref[...], b_ref[...], _NT)

    return pl.pallas_call(
        body, name=name, grid=(t // tm, n // tn),
        in_specs=[pl.BlockSpec((tm, k), lambda m, j: (m, 0)), pl.BlockSpec((tn, k), lambda m, j: (j, 0))],
        out_specs=pl.BlockSpec((tm, tn), lambda m, j: (m, j)),
        out_shape=jax.ShapeDtypeStruct((t, n), F32),
        compiler_params=_params(2),
    )(a, b)


def _du(dz, w_main, dzal, w_alow, tm=1024, tn=1024, tk=3072, behind=None, side=None):
    t, n = dz.shape
    tm = min(tm, t)
    d = w_main.shape[0]
    grid = (t // tm, d // tn, n // tk)
    dep_args, dep_specs = _behind(behind)
    adam = _SideAdamW(side, grid)
    n_dep = len(dep_args)

    def body(a_ref, w_ref, al_ref, wa_ref, *rest):
        o_ref = rest[n_dep + len(adam.args)]

        @pl.when(pl.program_id(2) == 0)
        def _():
            o_ref[...] = _dot(al_ref[...], wa_ref[...], _NT)

        o_ref[...] += _dot(a_ref[...], w_ref[...], _NT)
        if adam.on:
            adam.run(rest[n_dep:n_dep + 4], rest[n_dep + 5:])

    outs = pl.pallas_call(
        body, name="du", grid=grid,
        in_specs=[pl.BlockSpec((tm, tk), lambda m, j, kk: (m, kk)), pl.BlockSpec((tn, tk), lambda m, j, kk: (j, kk)),
                  pl.BlockSpec((tm, LANES), lambda m, j, kk: (m, 0)), pl.BlockSpec((tn, LANES), lambda m, j, kk: (j, 0))]
        + dep_specs + adam.in_specs,
        out_specs=[pl.BlockSpec((tm, tn), lambda m, j, kk: (m, j))] + adam.out_specs,
        out_shape=[jax.ShapeDtypeStruct((t, d), F32)] + adam.out_shape,
        compiler_params=_params(3),
    )(dz, w_main, dzal, w_alow, *dep_args, *adam.args)
    return outs[0], (outs[1:] if adam.on else None)


def _loss_head(x2, gf, tgt, tr=256):
    t, d = x2.shape

    def body(x_ref, g_ref, t_ref, dx_ref, dxb_ref, loss_ref, dg_ref):
        @pl.when(pl.program_id(0) == 0)
        def _():
            loss_ref[...] = jnp.zeros_like(loss_ref)
            dg_ref[...] = jnp.zeros_like(dg_ref)

        xf = x_ref[...]
        g = g_ref[...]
        r = lax.rsqrt(jnp.mean(xf * xf, axis=-1, keepdims=True) + EPS)
        xh = xf * r
        e = xh * g - t_ref[...]
        loss_ref[...] += 0.5 * jnp.sum(jnp.mean(e * e, axis=-1, keepdims=True))
        dy = e * (1.0 / d)
        dg_ref[...] += jnp.sum(dy * xh, axis=0, keepdims=True)
        dyg = dy * g
        dx = r * (dyg - xh * jnp.mean(dyg * xh, axis=-1, keepdims=True))
        dx_ref[...] = dx
        dxb_ref[...] = dx.astype(BF16)

    return pl.pallas_call(
        body, name="loss_head", grid=(t // tr,),
        in_specs=[pl.BlockSpec((tr, d), lambda i: (i, 0)), pl.BlockSpec((1, d), lambda i: (0, 0)),
                  pl.BlockSpec((tr, d), lambda i: (i, 0))],
        out_specs=[pl.BlockSpec((tr, d), lambda i: (i, 0)), pl.BlockSpec((tr, d), lambda i: (i, 0)),
                   pl.BlockSpec((SUBLANES, LANES), lambda i: (0, 0)), pl.BlockSpec((1, d), lambda i: (0, 0))],
        out_shape=[jax.ShapeDtypeStruct((t, d), F32), jax.ShapeDtypeStruct((t, d), BF16),
                   jax.ShapeDtypeStruct((SUBLANES, LANES), F32), jax.ShapeDtypeStruct((1, d), F32)],
        compiler_params=_params(1),
    )(x2, gf, tgt)


def _norm_bwd(name, dh, xin, g, dres, with_bf16, tr=256):
    t, d = xin.shape

    def body(dh_ref, x_ref, g_ref, dr_ref, dx_ref, *rest):
        dg_ref = rest[-1]

        @pl.when(pl.program_id(0) == 0)
        def _():
            dg_ref[...] = jnp.zeros_like(dg_ref)

        xf = x_ref[...]
        dhv = dh_ref[...]
        r = lax.rsqrt(jnp.mean(xf * xf, axis=-1, keepdims=True) + EPS)
        xh = xf * r
        dg_ref[...] += jnp.sum(dhv * xh, axis=0, keepdims=True)
        dyg = dhv * g_ref[...]
        dx = dr_ref[...] + r * (dyg - xh * jnp.mean(dyg * xh, axis=-1, keepdims=True))
        dx_ref[...] = dx
        if with_bf16:
            rest[0][...] = dx.astype(BF16)

    rows = pl.BlockSpec((tr, d), lambda i: (i, 0))
    vec = pl.BlockSpec((1, d), lambda i: (0, 0))
    return pl.pallas_call(
        body, name=name, grid=(t // tr,),
        in_specs=[rows, rows, vec, rows],
        out_specs=[rows] + [rows] * with_bf16 + [vec],
        out_shape=[jax.ShapeDtypeStruct((t, d), F32)] + [jax.ShapeDtypeStruct((t, d), BF16)] * with_bf16
        + [jax.ShapeDtypeStruct((1, d), F32)],
        compiler_params=_params(1),
    )(dh, xin, g, dres)


MIX_TILE = 256
CHUNKS_PER_TILE = MIX_TILE // CHUNK
CHUNK_SHIFT = CHUNK.bit_length() - 1
assert 1 << CHUNK_SHIFT == CHUNK


def _chunk_masks(n):
    row = lax.broadcasted_iota(jnp.int32, (n, n), 0)
    col = lax.broadcasted_iota(jnp.int32, (n, n), 1)
    same = lax.shift_right_logical(row, CHUNK_SHIFT) == lax.shift_right_logical(col, CHUNK_SHIFT)
    one = lambda m: jnp.where(m, 1.0, 0.0).astype(BF16)
    return one(same & (col > row)), one(same), one(same & (col < row))


def _mask_dot(mask, x):
    hi = x.astype(BF16)
    r1 = x - hi.astype(F32)
    mid = r1.astype(BF16)
    lo = (r1 - mid.astype(F32)).astype(BF16)
    return _dot(mask, hi) + _dot(mask, mid) + _dot(mask, lo)


def _log_sigmoid(x):
    return jnp.minimum(x, 0.0) - jnp.log1p(jnp.exp(-jnp.abs(x)))


def _conv_taps(prev8, uc, w):
    ext = jnp.concatenate([prev8, uc], axis=0)
    s1 = pltpu.roll(ext, 1, 0)[SUBLANES:]
    s2 = pltpu.roll(ext, 2, 0)[SUBLANES:]
    return s2 * w[0:1] + s1 * w[1:2] + uc * w[2:3], s1, s2


def _z_specs(tile, idx):
    d_conv = 1024
    wide = lambda c: pl.BlockSpec((tile, d_conv), lambda i, c=c: (idx(i), c))
    half = lambda c: pl.BlockSpec((tile, d_conv // 2), lambda i, c=c: (idx(i), c))
    return [wide(0), wide(1), wide(2), half(6), half(7), wide(4), wide(5)]


def _mixer_fwd(z, alow, wgu, b_gate, convw, conv_g, gla_g):
    t = z.shape[0]
    tb, cpt = MIX_TILE, CHUNKS_PER_TILE
    d_conv = conv_g.shape[1]
    dv = gla_g.shape[1]
    dk = dv // 2
    d_k = GLA_HEADS * dk
    gw = d_conv // CONV_GROUPS
    scale = dk ** -0.5

    def body(cb_ref, cc_ref, ch_ref, q_ref, k_ref, v_ref, og_ref, al_ref, wgu_ref, bg_ref, cw_ref, cg_ref, gg_ref,
             y_ref, sall_ref, carry_ref, s_ref):
        @pl.when(pl.program_id(0) == 0)
        def _():
            carry_ref[...] = jnp.zeros_like(carry_ref)
            s_ref[...] = jnp.zeros_like(s_ref)

        uc = cc_ref[...] * ch_ref[...]
        conv, _, _ = _conv_taps(carry_ref[...], uc, cw_ref[...])
        carry_ref[...] = uc[tb - SUBLANES:]
        ypre = cb_ref[...] * conv
        cg = cg_ref[...]
        for g in range(CONV_GROUPS):
            sl = slice(g * gw, (g + 1) * gw)
            seg = ypre[:, sl]
            r = lax.rsqrt(jnp.mean(seg * seg, axis=-1, keepdims=True) + EPS)
            y_ref[:, sl] = (seg * r * cg[:, sl]).astype(BF16)

        later, same, _ = _chunk_masks(tb)
        pre = _dot(al_ref[...].astype(BF16), wgu_ref[...]) + bg_ref[...]
        la = _log_sigmoid(pre) * (1.0 / GATE_NORMALIZER)
        e_dec = _mask_dot(later, la)
        dec_all = jnp.exp(_mask_dot(same, la))
        kdec = (k_ref[...] * jnp.exp(e_dec)).astype(BF16)
        qs = (q_ref[...] * scale).astype(BF16)
        vb = v_ref[...].astype(BF16)
        gg = gg_ref[...]
        rows = [slice(c * CHUNK, (c + 1) * CHUNK) for c in range(cpt)]
        ks = [slice(h * dk, (h + 1) * dk) for h in range(GLA_HEADS)]
        vs = [slice(h * dv, (h + 1) * dv) for h in range(GLA_HEADS)]
        kvt = [[_dot(vb[rows[c], vs[h]], kdec[rows[c], ks[h]], _TN) for h in range(GLA_HEADS)] for c in range(cpt)]
        state = [s_ref[h] for h in range(GLA_HEADS)]
        states = []
        for c in range(cpt):
            state = [state[h] * dec_all[c * CHUNK:c * CHUNK + 1, ks[h]] + kvt[c][h] for h in range(GLA_HEADS)]
            states.append(state)
            for h in range(GLA_HEADS):
                sall_ref[c, h] = state[h]
        for h in range(GLA_HEADS):
            s_ref[h] = state[h]
        for h in range(GLA_HEADS):
            o = jnp.concatenate(
                [_dot(qs[rows[c], ks[h]], states[c][h].astype(BF16), _NT) for c in range(cpt)], axis=0)
            ro = lax.rsqrt(jnp.mean(o * o, axis=-1, keepdims=True) + EPS)
            ogs = og_ref[:, vs[h]]
            yg = o * ro * gg * (ogs * jax.nn.sigmoid(ogs))
            y_ref[:, d_conv + h * dv:d_conv + (h + 1) * dv] = yg.astype(BF16)

    full = lambda shape: pl.BlockSpec(shape, lambda i: (0,) * len(shape))
    return pl.pallas_call(
        body, name="mixer_fwd", grid=(t // tb,),
        in_specs=_z_specs(tb, lambda i: i) + [
            pl.BlockSpec((tb, LANES), lambda i: (i, 0)), full(wgu.shape), full(b_gate.shape), full(convw.shape),
            full(conv_g.shape), full(gla_g.shape)],
        out_specs=[pl.BlockSpec((tb, d_conv + GLA_HEADS * dv), lambda i: (i, 0)),
                   pl.BlockSpec((cpt, GLA_HEADS, dv, dk), lambda i: (i, 0, 0, 0))],
        out_shape=[jax.ShapeDtypeStruct((t, d_conv + GLA_HEADS * dv), BF16),
                   jax.ShapeDtypeStruct((t // CHUNK, GLA_HEADS, dv, dk), F32)],
        scratch_shapes=[pltpu.VMEM((SUBLANES, d_conv), F32), pltpu.VMEM((GLA_HEADS, dv, dk), F32)],
        compiler_params=_params(1),
    )(z, z, z, z, z, z, z, alow, wgu, b_gate, convw, conv_g, gla_g)


def _mixer_bwd(z, alow, dy, sall, wgu, b_gate, convw, conv_g, gla_g, behind=None):
    t = z.shape[0]
    tb, cpt = MIX_TILE, CHUNKS_PER_TILE
    nt = t // tb
    d_conv = conv_g.shape[1]
    dv = gla_g.shape[1]
    dk = dv // 2
    d_k = GLA_HEADS * dk
    gw = d_conv // CONV_GROUPS
    scale = dk ** -0.5
    rev = lambda i: nt - 1 - i
    dep_args, dep_specs = _behind(behind)

    def body(cb_ref, cc_ref, ch_ref, q_ref, k_ref, v_ref, og_ref, ccp_ref, chp_ref, al_ref, dy_ref, sall_ref, sprev_ref,
             wgu_ref, bg_ref, cw_ref, cg_ref, gg_ref, *rest):
        dz_ref, dzal_ref, dcw_ref, dcg_ref, dgg_ref, dbg_ref, dwgu_ref, dcarry_ref, gd_ref = rest[-9:]
        i = pl.program_id(0)

        @pl.when(i == 0)
        def _():
            dcarry_ref[...] = jnp.zeros_like(dcarry_ref)
            gd_ref[...] = jnp.zeros_like(gd_ref)
            dcw_ref[...] = jnp.zeros_like(dcw_ref)
            dcg_ref[...] = jnp.zeros_like(dcg_ref)
            dgg_ref[...] = jnp.zeros_like(dgg_ref)
            dbg_ref[...] = jnp.zeros_like(dbg_ref)
            dwgu_ref[...] = jnp.zeros_like(dwgu_ref)

        first = rev(i) == 0

        cb, cc, ch = cb_ref[...], cc_ref[...], ch_ref[...]
        w = cw_ref[...]
        uc = cc * ch
        prev8 = jnp.where(first, 0.0, ccp_ref[...] * chp_ref[...])
        conv, s1, s2 = _conv_taps(prev8, uc, w)
        ypre = cb * conv
        cg = cg_ref[...]
        dypre_parts = []
        for g in range(CONV_GROUPS):
            sl = slice(g * gw, (g + 1) * gw)
            seg = ypre[:, sl]
            r = lax.rsqrt(jnp.mean(seg * seg, axis=-1, keepdims=True) + EPS)
            yn = seg * r
            dyc = dy_ref[:, sl]
            dcg_ref[:, sl] += jnp.sum(dyc * yn, axis=0, keepdims=True)
            dyn = dyc * cg[:, sl]
            dypre_parts.append(r * (dyn - yn * jnp.mean(dyn * yn, axis=-1, keepdims=True)))
        dypre = jnp.concatenate(dypre_parts, axis=1)
        dconv = dypre * cb
        dz_ref[:, 0:d_conv] = (dypre * conv).astype(BF16)
        dcw_ref[0:1] += jnp.sum(dconv * s2, axis=0, keepdims=True)
        dcw_ref[1:2] += jnp.sum(dconv * s1, axis=0, keepdims=True)
        dcw_ref[2:3] += jnp.sum(dconv * uc, axis=0, keepdims=True)
        ext = jnp.concatenate([dconv, dcarry_ref[...]], axis=0)
        f1 = pltpu.roll(ext, tb + SUBLANES - 1, 0)[:tb]
        f2 = pltpu.roll(ext, tb + SUBLANES - 2, 0)[:tb]
        dcarry_ref[...] = dconv[:SUBLANES]
        duc = dconv * w[2:3] + f1 * w[1:2] + f2 * w[0:1]
        dz_ref[:, d_conv:2 * d_conv] = (duc * ch).astype(BF16)
        dz_ref[:, 2 * d_conv:3 * d_conv] = (duc * cc).astype(BF16)

        q_off = 3 * d_conv
        k_off = q_off + d_k
        v_off = k_off + d_k
        og_off = v_off + GLA_HEADS * dv
        later, same, earlier = _chunk_masks(tb)
        alb = al_ref[...].astype(BF16)
        pre = _dot(alb, wgu_ref[...]) + bg_ref[...]
        la = _log_sigmoid(pre) * (1.0 / GATE_NORMALIZER)
        exp_e = jnp.exp(_mask_dot(later, la))
        dec_all = jnp.exp(_mask_dot(same, la))
        kdec = k_ref[...] * exp_e
        kdec_b = kdec.astype(BF16)
        qs = (q_ref[...] * scale).astype(BF16)
        vb = v_ref[...].astype(BF16)
        gg = gg_ref[...]
        rows = [slice(c * CHUNK, (c + 1) * CHUNK) for c in range(cpt)]
        ks = [slice(h * dk, (h + 1) * dk) for h in range(GLA_HEADS)]
        vs = [slice(h * dv, (h + 1) * dv) for h in range(GLA_HEADS)]
        st_b = [[sall_ref[c, h].astype(BF16) for h in range(GLA_HEADS)] for c in range(cpt)]
        do_b = []
        dgg = jnp.zeros_like(gg)
        for h in range(GLA_HEADS):
            o = jnp.concatenate([_dot(qs[rows[c], ks[h]], st_b[c][h], _NT) for c in range(cpt)], axis=0)
            ro = lax.rsqrt(jnp.mean(o * o, axis=-1, keepdims=True) + EPS)
            on = o * ro
            ogs = og_ref[:, vs[h]]
            sg = jax.nn.sigmoid(ogs)
            gate = ogs * sg
            dyg = dy_ref[:, d_conv + h * dv:d_conv + (h + 1) * dv]
            dgg = dgg + jnp.sum(dyg * on * gate, axis=0, keepdims=True)
            dz_ref[:, og_off + h * dv:og_off + (h + 1) * dv] = (
                dyg * on * gg * (sg * (1.0 + ogs * (1.0 - sg)))).astype(BF16)
            don = dyg * gg * gate
            do_b.append((ro * (don - on * jnp.mean(don * on, axis=-1, keepdims=True))).astype(BF16))
        dgg_ref[...] += dgg
        for h in range(GLA_HEADS):
            dq = jnp.concatenate([_dot(do_b[h][rows[c]], st_b[c][h]) for c in range(cpt)], axis=0)
            dz_ref[:, q_off + h * dk:q_off + (h + 1) * dk] = (dq * scale).astype(BF16)
        own = [[_dot(do_b[h][rows[c]], qs[rows[c], ks[h]], _TN) for h in range(GLA_HEADS)] for c in range(cpt)]
        carried = [gd_ref[h] for h in range(GLA_HEADS)]
        gt_b = [None] * cpt
        ddd = [None] * cpt
        for c in reversed(range(cpt)):
            gt = [own[c][h] + carried[h] for h in range(GLA_HEADS)]
            dec = [dec_all[c * CHUNK:c * CHUNK + 1, ks[h]] for h in range(GLA_HEADS)]
            carried = [gt[h] * dec[h] for h in range(GLA_HEADS)]
            if c > 0:
                st_prev = [sall_ref[c - 1, h] for h in range(GLA_HEADS)]
            else:
                st_prev = [jnp.where(first, 0.0, sprev_ref[0, h]) for h in range(GLA_HEADS)]
            ddec = [jnp.sum(gt[h] * st_prev[h], axis=0, keepdims=True) * dec[h] for h in range(GLA_HEADS)]
            ddd[c] = jnp.broadcast_to(jnp.concatenate(ddec, axis=1), (CHUNK, d_k))
            gt_b[c] = [gt[h].astype(BF16) for h in range(GLA_HEADS)]
        for h in range(GLA_HEADS):
            gd_ref[h] = carried[h]
        dkdec_cols = []
        for h in range(GLA_HEADS):
            dvh = jnp.concatenate([_dot(kdec_b[rows[c], ks[h]], gt_b[c][h], _NT) for c in range(cpt)], axis=0)
            dz_ref[:, v_off + h * dv:v_off + (h + 1) * dv] = dvh.astype(BF16)
            dkdec_cols.append(jnp.concatenate([_dot(vb[rows[c], vs[h]], gt_b[c][h]) for c in range(cpt)], axis=0))
        dkdec = jnp.concatenate(dkdec_cols, axis=1)
        dz_ref[:, k_off:k_off + d_k] = (dkdec * exp_e).astype(BF16)
        dla = _mask_dot(earlier, dkdec * kdec) + jnp.concatenate(ddd, axis=0)
        dpre = dla * (1.0 / GATE_NORMALIZER) * jax.nn.sigmoid(-pre)
        dbg_ref[...] += jnp.sum(dpre, axis=0, keepdims=True)
        dpre_b = dpre.astype(BF16)
        dwgu_ref[...] += _dot(alb, dpre_b, _TN)
        dzal_ref[...] = _dot(dpre_b, wgu_ref[...], _NT).astype(BF16)

    full = lambda shape: pl.BlockSpec(shape, lambda i: (0,) * len(shape))
    prev_rows = lambda c: pl.BlockSpec(
        (SUBLANES, d_conv), lambda i, c=c: (jnp.maximum(rev(i) * (tb // SUBLANES) - 1, 0), c))
    n_z = 3 * d_conv + 2 * d_k + 2 * GLA_HEADS * dv
    return pl.pallas_call(
        body, name="mixer_bwd", grid=(nt,),
        in_specs=_z_specs(tb, rev) + [
            prev_rows(1), prev_rows(2),
            pl.BlockSpec((tb, LANES), lambda i: (rev(i), 0)),
            pl.BlockSpec((tb, d_conv + GLA_HEADS * dv), lambda i: (rev(i), 0)),
            pl.BlockSpec((cpt, GLA_HEADS, dv, dk), lambda i: (rev(i), 0, 0, 0)),
            pl.BlockSpec((1, GLA_HEADS, dv, dk), lambda i: (jnp.maximum(rev(i) * cpt - 1, 0), 0, 0, 0)),
            full(wgu.shape), full(b_gate.shape), full(convw.shape), full(conv_g.shape), full(gla_g.shape)]
        + dep_specs,
        out_specs=[pl.BlockSpec((tb, n_z), lambda i: (rev(i), 0)), pl.BlockSpec((tb, LANES), lambda i: (rev(i), 0)),
                   full(convw.shape), full(conv_g.shape), full(gla_g.shape), full(b_gate.shape), full(wgu.shape)],
        out_shape=[jax.ShapeDtypeStruct((t, n_z), BF16), jax.ShapeDtypeStruct((t, LANES), BF16),
                   jax.ShapeDtypeStruct(convw.shape, F32), jax.ShapeDtypeStruct(conv_g.shape, F32),
                   jax.ShapeDtypeStruct(gla_g.shape, F32), jax.ShapeDtypeStruct(b_gate.shape, F32),
                   jax.ShapeDtypeStruct(wgu.shape, F32)],
        scratch_shapes=[pltpu.VMEM((SUBLANES, d_conv), F32), pltpu.VMEM((GLA_HEADS, dv, dk), F32)],
        compiler_params=_params(1),
    )(z, z, z, z, z, z, z, z, z, alow, dy, sall, sall, wgu, b_gate, convw, conv_g, gla_g, *dep_args)


def _adamw_math(g, w, m, v):
    m = ADAM_B1 * m + (1.0 - ADAM_B1) * g
    v = ADAM_B2 * v + (1.0 - ADAM_B2) * (g * g)
    m_hat = m / (1.0 - ADAM_B1 ** ADAM_STEP)
    v_hat = v / (1.0 - ADAM_B2 ** ADAM_STEP)
    delta = -ADAM_LR * (m_hat / (jnp.sqrt(v_hat) + ADAM_EPS) + ADAM_WD * w)
    return delta, m, v


def _adamw(name, parts, w, m, v, tr):
    r, c = w.shape
    n_parts = parts.shape[0]

    def body(p_ref, w_ref, m_ref, v_ref, g_ref, d_ref, nm_ref, nv_ref):
        g = p_ref[0].astype(F32)
        for j in range(1, n_parts):
            g = g + p_ref[j].astype(F32)
        g_ref[...] = g
        d_ref[...], nm_ref[...], nv_ref[...] = _adamw_math(g, w_ref[...], m_ref[...], v_ref[...])

    blk = pl.BlockSpec((tr, c), lambda i: (i, 0))
    return pl.pallas_call(
        body, name=name, grid=(r // tr,),
        in_specs=[pl.BlockSpec((n_parts, tr, c), lambda i: (0, i, 0)), blk, blk, blk],
        out_specs=[blk] * 4, out_shape=[jax.ShapeDtypeStruct((r, c), F32)] * 4,
        compiler_params=_params(1),
    )(parts, w, m, v)


def _adamw_small(grads, ws, ms, vs):
    n = len(grads)

    def body(*refs):
        g, w, m, v = (refs[k * n:(k + 1) * n] for k in range(4))
        d_out, m_out, v_out = (refs[(4 + k) * n:(5 + k) * n] for k in range(3))
        for i in range(n):
            d_out[i][...], m_out[i][...], v_out[i][...] = _adamw_math(g[i][...], w[i][...], m[i][...], v[i][...])

    vmem = pl.BlockSpec(memory_space=pltpu.VMEM)
    outs = pl.pallas_call(
        body, name="adamw_small", out_shape=[jax.ShapeDtypeStruct(w.shape, F32) for w in ws] * 3,
        in_specs=[vmem] * (4 * n), out_specs=[vmem] * (3 * n),
    )(*grads, *ws, *ms, *vs)
    return [outs[:n], outs[n:2 * n], outs[2 * n:]]


def _sum_partials(parts):
    n_parts, rows, lanes = parts.shape

    def body(p_ref, o_ref):
        g = p_ref[0]
        for j in range(1, n_parts):
            g = g + p_ref[j]
        o_ref[...] = g

    return pl.pallas_call(
        body, name="sum_small_partials", out_shape=jax.ShapeDtypeStruct((rows, lanes), F32),
        in_specs=[pl.BlockSpec(memory_space=pltpu.VMEM)], out_specs=pl.BlockSpec(memory_space=pltpu.VMEM),
    )(parts)


def _pack_rows(vectors, rows):
    flat = jnp.concatenate([a.reshape(-1).astype(F32) for a in vectors])
    return jnp.pad(flat, (0, rows * LANES - flat.shape[0])).reshape(rows, LANES)


def _unpack_rows(block, shapes):
    flat = block.reshape(-1)
    out, off = [], 0
    for s in shapes:
        n = 1
        for dim in s:
            n *= dim
        out.append(flat[off:off + n].reshape(s))
        off += n
    return out


def kernel(x, norm1_g, w_in, w_gate_up, b_gate, conv_w, conv_norm_g, gla_norm_g, w_out, norm2_g, w_ff1, w_ff2, norm_f_g, loss_target, m_norm1_g, m_w_in, m_w_gate_up, m_b_gate, m_conv_w, m_conv_norm_g, m_gla_norm_g, m_w_out, m_norm2_g, m_w_ff1, m_w_ff2, m_norm_f_g, v_norm1_g, v_w_in, v_w_gate_up, v_b_gate, v_conv_w, v_conv_norm_g, v_gla_norm_g, v_w_out, v_norm2_g, v_w_ff1, v_w_ff2, v_norm_f_g):
    me = _device_index()
    x2d, tgt = x[0], loss_target[0]
    t, d = x2d.shape
    d_in_shard = w_in.shape[2]
    d_in = N_DEV * d_in_shard
    n_main = d_in - GATE_RANK
    d_conv = conv_norm_g.shape[1]
    d_k = b_gate.shape[1]
    d_ff = N_DEV * w_ff1.shape[2]
    wmv = dict(
        norm1_g=(norm1_g, m_norm1_g, v_norm1_g), w_in=(w_in, m_w_in, v_w_in),
        w_gate_up=(w_gate_up, m_w_gate_up, v_w_gate_up), b_gate=(b_gate, m_b_gate, v_b_gate),
        conv_w=(conv_w, m_conv_w, v_conv_w), conv_norm_g=(conv_norm_g, m_conv_norm_g, v_conv_norm_g),
        gla_norm_g=(gla_norm_g, m_gla_norm_g, v_gla_norm_g), w_out=(w_out, m_w_out, v_w_out),
        norm2_g=(norm2_g, m_norm2_g, v_norm2_g), w_ff1=(w_ff1, m_w_ff1, v_w_ff1), w_ff2=(w_ff2, m_w_ff2, v_w_ff2),
        norm_f_g=(norm_f_g, m_norm_f_g, v_norm_f_g))

    small_rows = 16
    first_level = (SIBLING,) + SAME_CORE_PEERS
    win_shard = w_in[0].astype(BF16)
    in_send, in_recv, in_src, in_land, token = _exchange_start(
        "all_gather_start_w_in", [win_shard], [_land_zone(win_shard)], scatter=False, masks=[first_level])
    _, wgu_t, cw_t, wout_t, w1_t, w2_t = lax.optimization_barrier((token, w_gate_up, conv_w, w_out, w_ff1, w_ff2))
    small_shard = _pack_rows([wgu_t[0], cw_t[0]], small_rows)
    shards = [small_shard, wout_t[0].astype(BF16), w1_t[0].astype(BF16), w2_t[0].astype(BF16)]
    ag_send, ag_recv, ag_src, ag_land, token = _exchange_start(
        "all_gather_start", shards, [_land_zone(s) for s in shards], scatter=False, behind=token)

    def gathered(k, name, after):
        return _exchange_wait(name, ag_send[k], ag_recv[k], ag_src[k], ag_land[k], after, scatter=False)

    u = _rmsnorm(x2d, norm1_g, behind=token)
    tied = lax.optimization_barrier((token, w_in, m_w_in, v_w_in))
    wmv["w_in"] = tuple(tied[1:])
    small_g = gathered(0, "all_gather_wait_small", [u] + [a[0] for a in wmv["w_in"]])
    win_level1 = _exchange_wait(
        "all_gather_wait_w_in", in_send[0], in_recv[0], in_src[0], in_land[0], small_g, scatter=False,
        masks=first_level)
    win_g = _forward_wait("all_gather_wait_w_in_forwarded", *_forward_start("all_gather_forward_w_in", win_level1))
    w_main, w_alow = _shards_to_columns(win_g, n_main)
    small_flat = small_g.reshape(N_DEV, -1)
    n_wgu = GATE_RANK * (d_k // N_DEV)
    wgu_full = small_flat[:, :n_wgu].reshape(N_DEV, GATE_RANK, d_k // N_DEV).transpose(1, 0, 2).reshape(GATE_RANK, d_k)
    conv_w_full = small_flat[:, n_wgu:n_wgu + (d_conv // N_DEV) * CONV_WIDTH].reshape(d_conv, CONV_WIDTH)
    wgu_pad = jnp.pad(wgu_full, ((0, LANES - GATE_RANK), (0, 0))).astype(BF16)
    convw_taps = jnp.pad(conv_w_full.T, ((0, SUBLANES - CONV_WIDTH), (0, 0)))

    get_w_out = lambda after: gathered(1, "all_gather_wait_w_out", after).reshape(-1, d)
    get_w1 = lambda after: gathered(2, "all_gather_wait_w_ff1", after)
    get_w2 = lambda after: gathered(3, "all_gather_wait_w_ff2", after).reshape(d_ff, d)

    in_flight = {}

    def send_partials(name, parts):
        own = lax.dynamic_index_in_dim(parts, me, axis=0, keepdims=False)
        send, recv, src, land, token = _exchange_start("scatter_start_" + name, [parts], [_land_zone(own)], scatter=True)
        in_flight[name] = (send[0], recv[0], src[0], land[0])
        return token

    def on_grad(name, value):
        if name == "w_in":
            main, alow_part = value
            value = _columns_to_shards(main, alow_part, N_DEV, d_in_shard)
        elif name in ("w_out", "w_ff2"):
            value = value.reshape(N_DEV, -1, d)
        return send_partials(name, value)

    def received(name, after):
        send, recv, src, land = in_flight[name]
        return _exchange_wait("scatter_wait_" + name, send, recv, src, land, after, scatter=True)

    def side_for(name, after):
        return (received(name, after),) + tuple(a[0] for a in wmv[name])

    grads = _local_step(x2d, u, tgt, norm1_g, w_main, w_alow, wgu_pad, b_gate, convw_taps, conv_norm_g, gla_norm_g,
                        norm2_g, norm_f_g, get_w_out, get_w1, get_w2, on_grad, side_for)
    grad_x = grads["x"]

    small_shapes = [(1, d), (1, d_k), (1, d_conv), (1, gla_norm_g.shape[1]), (1, d), (d,),
                    (GATE_RANK, d_k), (d_conv, CONV_WIDTH), (1,)]
    small_grad_rows = 152
    small_part = _pack_rows(
        [grads["norm1_g"], grads["b_gate"], grads["conv_norm_g"], grads["gla_norm_g"], grads["norm2_g"],
         grads["norm_f_g"], grads["w_gate_up"][:GATE_RANK], grads["conv_w"][:CONV_WIDTH].T, grads["loss"][0, 0]],
        small_grad_rows)
    small_token = send_partials("small", jnp.broadcast_to(small_part[None], (N_DEV, small_grad_rows, LANES)))

    gin_r, gout_r = (received(nm, [grad_x, small_token]) for nm in ("w_in", "w_out"))
    get_small = lambda after: received("small", after)
    done = {"w_ff1": grads["adam_w_ff1"], "w_ff2": grads["adam_w_ff2"]}
    return _update(me, gin_r, gout_r, done, get_small, small_shapes, grad_x, wmv)


def _local_step(x2d, u, tgt, norm1_g, w_main, w_alow, wgu_pad, b_gate, convw_taps, conv_norm_g, gla_norm_g,
                norm2_g, norm_f_g, get_w_out, get_w1, get_w2, on_grad, side_for=lambda name, after: None):
    t, d = x2d.shape
    n_main = w_main.shape[1]

    z, alow = _inproj(u, w_main, w_alow)
    y, sall = _mixer_fwd(z, alow, wgu_pad, b_gate, convw_taps, conv_norm_g, gla_norm_g)
    w_out_full = get_w_out(y)
    x1, h = _outproj(y, w_out_full, x2d, norm2_g)
    w1g = get_w1(h)
    a = _ff1(h, w1g)
    w2_full = get_w2(a)
    d_ff = w2_full.shape[0]
    x2 = _ff2(a, w2_full, x1)
    dx2, dx2b, loss_part, d_normf = _loss_head(x2, norm_f_g.reshape(1, d), tgt)

    tk = min(4096, t)
    nk = t // tk
    da = _dff2(dx2b, w2_full, a)
    dw2 = _tn_matmul(
        "dw_ff2", a, dx2b, (d_ff // 1024, d // 1024, nk),
        pl.BlockSpec((tk, 1024), lambda m, j, kk: (kk, m)), pl.BlockSpec((tk, 1024), lambda m, j, kk: (kk, j)),
        jax.ShapeDtypeStruct((d_ff, d), BF16), pl.BlockSpec((1024, 1024), lambda m, j, kk: (m, j)), (1024, 1024),
        a_fn=_relu_sq)
    token = on_grad("w_ff2", dw2)
    f_shard = d_ff // N_DEV
    dw1 = _tn_matmul(
        "dw_ff1", h, da, (N_DEV, d // 1024, nk),
        pl.BlockSpec((tk, 1024), lambda g, m, kk: (kk, m)), pl.BlockSpec((tk, f_shard), lambda g, m, kk: (kk, g)),
        jax.ShapeDtypeStruct((N_DEV, d, f_shard), BF16), pl.BlockSpec((None, 1024, f_shard), lambda g, m, kk: (g, m, 0)),
        (1024, f_shard), behind=token)
    token = on_grad("w_ff1", dw1)
    dh, adam_ff2 = _dh(da, w1g, behind=token, side=side_for("w_ff2", token))
    dx1, dx1b, d_norm2 = _norm_bwd("norm2_bwd", dh, x1, norm2_g, dx2, with_bf16=True)
    dy = _nt_matmul("dy", dx1b, w_out_full)
    dwout = _tn_matmul(
        "dw_out", y, dx1b, (d // 1024, d // 1024, nk),
        pl.BlockSpec((tk, 1024), lambda m, j, kk: (kk, m)), pl.BlockSpec((tk, 1024), lambda m, j, kk: (kk, j)),
        jax.ShapeDtypeStruct((d, d), BF16), pl.BlockSpec((1024, 1024), lambda m, j, kk: (m, j)), (1024, 1024))
    token = on_grad("w_out", dwout)
    dz, dzal, d_convw, d_convg, d_glag, d_bgate, d_wgu = _mixer_bwd(
        z, alow, dy, sall, wgu_pad, b_gate, convw_taps, conv_norm_g, gla_norm_g, behind=token)
    token = on_grad("w_in", _dw_in(u, dz, dzal, tk))
    du, adam_ff1 = _du(dz, w_main, dzal, w_alow, behind=token, side=side_for("w_ff1", token))
    grad_x, d_norm1 = _norm_bwd("norm1_bwd", du, x2d, norm1_g, dx1, with_bf16=False)
    return dict(x=grad_x, loss=loss_part, adam_w_ff2=adam_ff2, adam_w_ff1=adam_ff1,
                norm1_g=d_norm1, w_gate_up=d_wgu, b_gate=d_bgate, conv_w=d_convw,
                conv_norm_g=d_convg, gla_norm_g=d_glag, norm2_g=d_norm2, norm_f_g=d_normf)


_WEIGHT_ORDER = ("norm1_g", "w_in", "w_gate_up", "b_gate", "conv_w", "conv_norm_g", "gla_norm_g", "w_out", "norm2_g",
                 "w_ff1", "w_ff2", "norm_f_g")
_SMALL_ORDER = ("norm1_g", "b_gate", "conv_norm_g", "gla_norm_g", "norm2_g", "norm_f_g", "w_gate_up", "conv_w")
def _update(me, gin_r, gout_r, done, get_small, small_shapes, grad_x, wmv):
    big = dict(done)
    big["w_in"] = _adamw("adamw_w_in", gin_r, *(a[0] for a in wmv["w_in"]), 256)
    big["w_out"] = _adamw("adamw_w_out", gout_r, *(a[0] for a in wmv["w_out"]), 128)

    wgu_cols = wmv["w_gate_up"][0].shape[2]
    cw_rows = wmv["conv_w"][0].shape[1]

    small_r = get_small([big[nm][3] for nm in ("w_in", "w_out")])
    summed = _unpack_rows(_sum_partials(small_r), small_shapes)
    summed[6] = lax.dynamic_slice_in_dim(summed[6], me * wgu_cols, wgu_cols, axis=1)
    summed[7] = lax.dynamic_slice_in_dim(summed[7], me * cw_rows, cw_rows, axis=0)
    as_2d = lambda a: a.reshape((1, -1) if a.ndim == 1 else a.shape[-2:])
    grads_2d = [as_2d(g) for g in summed[:len(_SMALL_ORDER)]]
    small = _adamw_small(grads_2d, *[[as_2d(wmv[nm][k]) for nm in _SMALL_ORDER] for k in range(3)])
    small = [grads_2d] + small

    outs = []
    for k in range(4):
        for nm in _WEIGHT_ORDER:
            if nm in big:
                outs.append(big[nm][k][None])
            else:
                outs.append(small[k][_SMALL_ORDER.index(nm)].reshape(wmv[nm][0].shape))
    loss = summed[8][0]
    return (loss, grad_x[None], *outs)
```

```python
import functools

import jax
import jax.numpy as jnp
from jax import lax
from jax.experimental import pallas as pl
from jax.experimental.pallas import tpu as pltpu

F32 = jnp.float32
BF16 = jnp.bfloat16

N_DEV = 8
CHUNK = 64
GLA_HEADS = 4
CONV_GROUPS = 8
CONV_WIDTH = 3
GATE_RANK = 16
GATE_NORMALIZER = 16.0
EPS = 1e-6
ADAM_LR = 0.001
ADAM_B1 = 0.9
ADAM_B2 = 0.999
ADAM_EPS = 1e-08
ADAM_WD = 0.01
ADAM_STEP = 10

LANES = 128
SUBLANES = 8
VMEM_LIMIT = 56 << 20

_NN = (((1,), (0,)), ((), ()))
_NT = (((1,), (1,)), ((), ()))
_TN = (((0,), (0,)), ((), ()))


def _dot(a, b, dims=_NN):
    return lax.dot_general(a, b, dims, preferred_element_type=F32)


def _params(n_grid):
    return pltpu.CompilerParams(dimension_semantics=("arbitrary",) * n_grid, vmem_limit_bytes=VMEM_LIMIT)


def _relu_sq(a):
    r = jnp.maximum(a, 0.0)
    return r * r


def _device_index():
    return 4 * lax.axis_index("x") + 2 * lax.axis_index("y") + lax.axis_index("c")


def _peer(mask):
    x, y, c = lax.axis_index("x"), lax.axis_index("y"), lax.axis_index("c")
    return (x ^ ((mask >> 2) & 1), y ^ ((mask >> 1) & 1), c ^ (mask & 1))


_HBM_SPEC = pl.BlockSpec(memory_space=pltpu.HBM)
_SEM_SPEC = pl.BlockSpec(memory_space=pltpu.SEMAPHORE)
_SIDE_EFFECT = pltpu.SideEffectType.DATAFLOW_SIDE_EFFECTING
N_PEERS = N_DEV - 1


def _exchange_copy(src_ref, land_ref, send_sems, recv_sems, mask, scatter, arriving):
    me = _device_index()
    src = src_ref.at[me ^ mask] if scatter else src_ref
    dst = land_ref.at[(me ^ mask) if arriving else me]
    return pltpu.make_async_remote_copy(
        src_ref=src, dst_ref=dst, send_sem=send_sems.at[mask - 1], recv_sem=recv_sems.at[mask - 1],
        device_id=_peer(mask), device_id_type=pl.DeviceIdType.MESH)


def _land_zone(own):
    zone = lax.empty((N_DEV,) + own.shape, own.dtype)
    return lax.dynamic_update_slice(zone, own[None], (_device_index(),) + (0,) * own.ndim)


ALL_PEERS = tuple(range(1, N_DEV))
SIBLING = 1
SAME_CORE_PEERS = (2, 4, 6)


def _exchange_start(name, srcs, lands, scatter, masks=None, behind=None):
    n = len(srcs)
    masks = masks or [ALL_PEERS] * n
    dep_args = [] if behind is None else [behind]

    def body(*refs):
        src, land = refs[:n], refs[n:2 * n]
        outs = refs[2 * n + len(dep_args):]
        send_sems, recv_sems = outs[:n], outs[n:2 * n]
        token = refs[-1]
        for a in range(n):
            for mask in masks[a]:
                _exchange_copy(src[a], land[a], send_sems[a], recv_sems[a], mask, scatter, False).start()
        token[...] = jnp.zeros_like(token)

    hbm = lambda a: pltpu.HBM(a.shape, a.dtype)
    outs = pl.pallas_call(
        body, name=name,
        out_shape=([pltpu.SemaphoreType.DMA((N_PEERS,))] * (2 * n) + [hbm(a) for a in srcs] + [hbm(a) for a in lands]
                   + [jax.ShapeDtypeStruct((SUBLANES, LANES), F32)]),
        in_specs=[_HBM_SPEC] * (2 * n) + [pl.BlockSpec(memory_space=pl.ANY)] * len(dep_args),
        out_specs=[_SEM_SPEC] * (2 * n) + [_HBM_SPEC] * (2 * n) + [pl.BlockSpec(memory_space=pltpu.VMEM)],
        input_output_aliases={a: 2 * n + a for a in range(2 * n)},
        compiler_params=pltpu.CompilerParams(has_side_effects=_SIDE_EFFECT),
    )(*[pltpu.with_memory_space_constraint(a, pltpu.HBM) for a in list(srcs) + list(lands)], *dep_args)
    send_sems, recv_sems = outs[:n], outs[n:2 * n]
    src_thru, land_thru = outs[2 * n:3 * n], outs[3 * n:4 * n]
    return send_sems, recv_sems, src_thru, land_thru, outs[-1]


def _exchange_wait(name, send_sems, recv_sems, src_thru, land_thru, after, scatter, masks=ALL_PEERS):
    after = list(after) if isinstance(after, (list, tuple)) else [after]

    def body(src_ref, land_ref, send_ref, recv_ref, *rest):
        for mask in masks:
            cp = _exchange_copy(src_ref, land_ref, send_ref, recv_ref, mask, scatter, True)
            cp.wait_send()
            cp.wait_recv()

    return pl.pallas_call(
        body, name=name,
        out_shape=(pltpu.HBM(src_thru.shape, src_thru.dtype), pltpu.HBM(land_thru.shape, land_thru.dtype)),
        in_specs=[_HBM_SPEC, _HBM_SPEC, _SEM_SPEC, _SEM_SPEC] + [pl.BlockSpec(memory_space=pl.ANY)] * len(after),
        out_specs=(_HBM_SPEC, _HBM_SPEC), input_output_aliases={0: 0, 1: 1},
        compiler_params=pltpu.CompilerParams(has_side_effects=_SIDE_EFFECT),
    )(src_thru, land_thru, send_sems, recv_sems, *after)[1]


def _forward_copy(land_ref, send_sems, recv_sems, k, arriving):
    me = _device_index()
    slot = me ^ SAME_CORE_PEERS[k]
    return pltpu.make_async_remote_copy(
        src_ref=land_ref.at[slot], dst_ref=land_ref.at[(slot ^ SIBLING) if arriving else slot],
        send_sem=send_sems.at[k], recv_sem=recv_sems.at[k],
        device_id=_peer(SIBLING), device_id_type=pl.DeviceIdType.MESH)


def _forward_start(name, land):
    n_fwd = len(SAME_CORE_PEERS)

    def body(land_ref, send_sems, recv_sems, land_thru):
        for k in range(n_fwd):
            _forward_copy(land_ref, send_sems, recv_sems, k, False).start()

    send, recv, thru = pl.pallas_call(
        body, name=name,
        out_shape=[pltpu.SemaphoreType.DMA((n_fwd,)), pltpu.SemaphoreType.DMA((n_fwd,)), pltpu.HBM(land.shape, land.dtype)],
        in_specs=[_HBM_SPEC], out_specs=[_SEM_SPEC, _SEM_SPEC, _HBM_SPEC], input_output_aliases={0: 2},
        compiler_params=pltpu.CompilerParams(has_side_effects=_SIDE_EFFECT),
    )(pltpu.with_memory_space_constraint(land, pltpu.HBM))
    return send, recv, thru


def _forward_wait(name, send_sems, recv_sems, land_thru):
    def body(land_ref, send_ref, recv_ref, got_ref):
        for k in range(len(SAME_CORE_PEERS)):
            cp = _forward_copy(land_ref, send_ref, recv_ref, k, True)
            cp.wait_send()
            cp.wait_recv()

    return pl.pallas_call(
        body, name=name, out_shape=pltpu.HBM(land_thru.shape, land_thru.dtype),
        in_specs=[_HBM_SPEC, _SEM_SPEC, _SEM_SPEC], out_specs=_HBM_SPEC, input_output_aliases={0: 0},
        compiler_params=pltpu.CompilerParams(has_side_effects=_SIDE_EFFECT),
    )(land_thru, send_sems, recv_sems)


def _shards_to_columns(g, n_main, tr=256):
    n_dev, d, s = g.shape

    def body(g_ref, main_ref, rest_ref):
        for j in range(n_dev):
            lo, hi = j * s, (j + 1) * s
            if hi <= n_main:
                main_ref[:, lo:hi] = g_ref[j]
            else:
                main_ref[:, lo:n_main] = g_ref[j, :, 0:n_main - lo]
                rest_ref[...] = jnp.zeros_like(rest_ref)
                rest_ref[:, 0:hi - n_main] = g_ref[j, :, n_main - lo:s]

    return pl.pallas_call(
        body, grid=(d // tr,), name="shards_to_columns",
        in_specs=[pl.BlockSpec((n_dev, tr, s), lambda i: (0, i, 0))],
        out_specs=[pl.BlockSpec((tr, n_main), lambda i: (i, 0)), pl.BlockSpec((tr, LANES), lambda i: (i, 0))],
        out_shape=[jax.ShapeDtypeStruct((d, n_main), g.dtype), jax.ShapeDtypeStruct((d, LANES), g.dtype)],
        compiler_params=_params(1),
    )(g)


def _columns_to_shards(main, rest, n_dev, s, tr=256):
    d, n_main = main.shape
    assert (n_dev - 1) * s <= n_main < n_dev * s

    def body(main_ref, rest_ref, o_ref):
        for j in range(n_dev):
            lo, hi = j * s, (j + 1) * s
            if hi <= n_main:
                o_ref[j] = main_ref[:, lo:hi]
            else:
                o_ref[j, :, 0:n_main - lo] = main_ref[:, lo:n_main]
                o_ref[j, :, n_main - lo:s] = rest_ref[:, 0:hi - n_main]

    return pl.pallas_call(
        body, grid=(d // tr,), name="columns_to_shards",
        in_specs=[pl.BlockSpec((tr, n_main), lambda i: (i, 0)), pl.BlockSpec((tr, LANES), lambda i: (i, 0))],
        out_specs=pl.BlockSpec((n_dev, tr, s), lambda i: (0, i, 0)),
        out_shape=jax.ShapeDtypeStruct((n_dev, d, s), main.dtype),
        compiler_params=_params(1),
    )(main, rest)


def _rmsnorm(x, g, tr=512, behind=None):
    t, d = x.shape
    tr = min(tr, t)
    dep_args, dep_specs = _behind(behind)

    def body(x_ref, g_ref, *rest):
        u_ref = rest[-1]
        xf = x_ref[...]
        r = lax.rsqrt(jnp.mean(xf * xf, axis=-1, keepdims=True) + EPS)
        u_ref[...] = (xf * r * g_ref[...]).astype(BF16)

    return pl.pallas_call(
        body, name="rmsnorm1", grid=(t // tr,),
        in_specs=[pl.BlockSpec((tr, d), lambda i: (i, 0)), pl.BlockSpec((1, d), lambda i: (0, 0))] + dep_specs,
        out_specs=pl.BlockSpec((tr, d), lambda i: (i, 0)),
        out_shape=jax.ShapeDtypeStruct((t, d), BF16),
        compiler_params=_params(1),
    )(x, g, *dep_args)


def _inproj(u, w_main, w_alow, tm=1024, tn=1024):
    t, d = u.shape
    tm = min(tm, t)
    n = w_main.shape[1]

    def body(u_ref, w_ref, wa_ref, z_ref, al_ref):
        @pl.when(pl.program_id(1) == 0)
        def _():
            al_ref[...] = _dot(u_ref[...], wa_ref[...])

        z_ref[...] = _dot(u_ref[...], w_ref[...])

    return pl.pallas_call(
        body, name="inproj", grid=(t // tm, n // tn),
        in_specs=[pl.BlockSpec((tm, d), lambda m, j: (m, 0)), pl.BlockSpec((d, tn), lambda m, j: (0, j)),
                  pl.BlockSpec((d, LANES), lambda m, j: (0, 0))],
        out_specs=[pl.BlockSpec((tm, tn), lambda m, j: (m, j)), pl.BlockSpec((tm, LANES), lambda m, j: (m, 0))],
        out_shape=[jax.ShapeDtypeStruct((t, n), F32), jax.ShapeDtypeStruct((t, LANES), F32)],
        compiler_params=_params(2),
    )(u, w_main, w_alow)


def _outproj(y, w_out, x, g2, tm=512):
    t, d = x.shape
    tm = min(tm, t)
    k = y.shape[1]

    def body(y_ref, w_ref, x_ref, g_ref, x1_ref, h_ref):
        x1 = x_ref[...] + _dot(y_ref[...], w_ref[...])
        x1_ref[...] = x1
        r = lax.rsqrt(jnp.mean(x1 * x1, axis=-1, keepdims=True) + EPS)
        h_ref[...] = (x1 * r * g_ref[...]).astype(BF16)

    return pl.pallas_call(
        body, name="outproj_rmsnorm", grid=(t // tm,),
        in_specs=[pl.BlockSpec((tm, k), lambda m: (m, 0)), pl.BlockSpec((k, d), lambda m: (0, 0)),
                  pl.BlockSpec((tm, d), lambda m: (m, 0)), pl.BlockSpec((1, d), lambda m: (0, 0))],
        out_specs=[pl.BlockSpec((tm, d), lambda m: (m, 0)), pl.BlockSpec((tm, d), lambda m: (m, 0))],
        out_shape=[jax.ShapeDtypeStruct((t, d), F32), jax.ShapeDtypeStruct((t, d), BF16)],
        compiler_params=_params(1),
    )(y, w_out, x, g2)


def _ff1(h, w1g, tm=1024):
    t, d = h.shape
    tm = min(tm, t)
    g, _, f = w1g.shape

    def body(h_ref, w_ref, a_ref):
        a_ref[...] = _dot(h_ref[...], w_ref[...]).astype(BF16)

    return pl.pallas_call(
        body, name="ff1", grid=(t // tm, g),
        in_specs=[pl.BlockSpec((tm, d), lambda m, j: (m, 0)), pl.BlockSpec((None, d, f), lambda m, j: (j, 0, 0))],
        out_specs=pl.BlockSpec((tm, f), lambda m, j: (m, j)),
        out_shape=jax.ShapeDtypeStruct((t, g * f), BF16),
        compiler_params=_params(2),
    )(h, w1g)


def _ff2(a, w2, x1, tm=1024, tn=1024, tk=2048):
    t, f = a.shape
    tm = min(tm, t)
    d = w2.shape[1]

    def body(a_ref, w_ref, x1_ref, o_ref):
        @pl.when(pl.program_id(2) == 0)
        def _():
            o_ref[...] = x1_ref[...]

        o_ref[...] += _dot(_relu_sq(a_ref[...]), w_ref[...])

    return pl.pallas_call(
        body, name="ff2_residual", grid=(t // tm, d // tn, f // tk),
        in_specs=[pl.BlockSpec((tm, tk), lambda m, j, kk: (m, kk)), pl.BlockSpec((tk, tn), lambda m, j, kk: (kk, j)),
                  pl.BlockSpec((tm, tn), lambda m, j, kk: (m, j))],
        out_specs=pl.BlockSpec((tm, tn), lambda m, j, kk: (m, j)),
        out_shape=jax.ShapeDtypeStruct((t, d), F32),
        compiler_params=_params(3),
    )(a, w2, x1)


def _dff2(dx2b, w2, a, tm=1024, tn=1024):
    t, d = dx2b.shape
    tm = min(tm, t)
    f = w2.shape[0]

    def body(g_ref, w_ref, a_ref, o_ref):
        dp = _dot(g_ref[...], w_ref[...], _NT)
        o_ref[...] = (dp * (2.0 * jnp.maximum(a_ref[...].astype(F32), 0.0))).astype(BF16)

    return pl.pallas_call(
        body, name="dff2", grid=(t // tm, f // tn),
        in_specs=[pl.BlockSpec((tm, d), lambda m, j: (m, 0)), pl.BlockSpec((tn, d), lambda m, j: (j, 0)),
                  pl.BlockSpec((tm, tn), lambda m, j: (m, j))],
        out_specs=pl.BlockSpec((tm, tn), lambda m, j: (m, j)),
        out_shape=jax.ShapeDtypeStruct((t, f), BF16),
        compiler_params=_params(2),
    )(dx2b, w2, a)


def _behind(token):
    if token is None:
        return [], []
    return [token], [pl.BlockSpec(token.shape, lambda *_: (0,) * token.ndim)]


def _tn_matmul(name, a, b, grid, a_spec, b_spec, out_shape, out_spec, acc_shape, a_fn=None, behind=None):
    nk = grid[-1]
    dep_args, dep_specs = _behind(behind)

    def body(a_ref, b_ref, *rest):
        o_ref, acc_ref = rest[-2:]
        kk = pl.program_id(len(grid) - 1)
        av = a_ref[...]
        if a_fn is not None:
            av = a_fn(av)
        part = _dot(av, b_ref[...], _TN)

        @pl.when(kk == 0)
        def _():
            acc_ref[...] = part

        @pl.when(kk > 0)
        def _():
            acc_ref[...] += part

        @pl.when(kk == nk - 1)
        def _():
            o_ref[...] = acc_ref[...].astype(o_ref.dtype)

    return pl.pallas_call(
        body, name=name, grid=grid, in_specs=[a_spec, b_spec] + dep_specs, out_specs=out_spec, out_shape=out_shape,
        scratch_shapes=[pltpu.VMEM(acc_shape, F32)], compiler_params=_params(len(grid)),
    )(a, b, *dep_args)


def _dw_in(u, dz, dzal, tk, tm=1024, tn=1024):
    t, d = u.shape
    n_main = dz.shape[1]
    nk = t // tk

    def body(a_ref, b_ref, al_ref, o_ref, oal_ref, acc_ref, accal_ref):
        j, kk = pl.program_id(1), pl.program_id(2)
        av = a_ref[...]

        def accumulate(acc, part, out):
            @pl.when(kk == 0)
            def _():
                acc[...] = part

            @pl.when(kk > 0)
            def _():
                acc[...] += part

            @pl.when(kk == nk - 1)
            def _():
                out[...] = acc[...].astype(out.dtype)

        accumulate(acc_ref, _dot(av, b_ref[...], _TN), o_ref)

        @pl.when(j == 0)
        def _():
            accumulate(accal_ref, _dot(av, al_ref[...], _TN), oal_ref)

    return pl.pallas_call(
        body, name="dw_in", grid=(d // tm, n_main // tn, nk),
        in_specs=[pl.BlockSpec((tk, tm), lambda m, j, kk: (kk, m)), pl.BlockSpec((tk, tn), lambda m, j, kk: (kk, j)),
                  pl.BlockSpec((tk, LANES), lambda m, j, kk: (kk, 0))],
        out_specs=[pl.BlockSpec((tm, tn), lambda m, j, kk: (m, j)), pl.BlockSpec((tm, LANES), lambda m, j, kk: (m, 0))],
        out_shape=[jax.ShapeDtypeStruct((d, n_main), BF16), jax.ShapeDtypeStruct((d, LANES), BF16)],
        scratch_shapes=[pltpu.VMEM((tm, tn), F32), pltpu.VMEM((tm, LANES), F32)],
        compiler_params=_params(3),
    )(u, dz, dzal)


class _SideAdamW:
    def __init__(self, side, grid):
        parts, w, m, v = side
        n_parts, r, c = parts.shape
        steps = 1
        for extent in grid:
            steps *= extent
        rows = r // steps
        assert rows * steps == r and rows % (2 * SUBLANES) == 0, (r, steps)

        def step(*ids):
            lin = ids[0]
            for extent, idx in zip(grid[1:], ids[1:]):
                lin = lin * extent + idx
            return lin

        slab = pl.BlockSpec((rows, c), lambda *ids: (step(*ids), 0))
        self.args = [parts, w, m, v]
        self.in_specs = [pl.BlockSpec((n_parts, rows, c), lambda *ids: (0, step(*ids), 0)), slab, slab, slab]
        self.out_specs = [slab] * 4
        self.out_shape = [jax.ShapeDtypeStruct((r, c), F32)] * 4
        self.n_parts = n_parts

    def run(self, in_refs, out_refs):
        p_ref, w_ref, m_ref, v_ref = in_refs
        g = p_ref[0].astype(F32)
        for j in range(1, self.n_parts):
            g = g + p_ref[j].astype(F32)
        out_refs[0][...] = g
        out_refs[1][...], out_refs[2][...], out_refs[3][...] = _adamw_math(g, w_ref[...], m_ref[...], v_ref[...])


def _dh(da, w1g, tm=1024, tn=1024, shards_per_step=4, behind=None):
    t = da.shape[0]
    tm = min(tm, t)
    g, d, f = w1g.shape
    sps = shards_per_step
    grid = (t // tm, d // tn, g // sps)
    dep_args, dep_specs = _behind(behind)

    def body(a_ref, w_ref, *rest):
        o_ref = rest[-1]
        acc = _dot(a_ref[:, 0:f], w_ref[0], _NT)
        for s in range(1, sps):
            acc = acc + _dot(a_ref[:, s * f:(s + 1) * f], w_ref[s], _NT)

        @pl.when(pl.program_id(2) == 0)
        def _():
            o_ref[...] = acc

        @pl.when(pl.program_id(2) > 0)
        def _():
            o_ref[...] += acc

    return pl.pallas_call(
        body, name="dh", grid=grid,
        in_specs=[pl.BlockSpec((tm, sps * f), lambda m, j, kk: (m, kk)),
                  pl.BlockSpec((sps, tn, f), lambda m, j, kk: (kk, j, 0))] + dep_specs,
        out_specs=pl.BlockSpec((tm, tn), lambda m, j, kk: (m, j)),
        out_shape=jax.ShapeDtypeStruct((t, d), F32),
        compiler_params=_params(3),
    )(da, w1g, *dep_args)


def _nt_matmul(name, a, b, tm=1024, tn=1024):
    t, k = a.shape
    tm = min(tm, t)
    n = b.shape[0]

    def body(a_ref, b_ref, o_ref):
        o_ref[...] = _dot(a_ref[...], b_ref[...], _NT)

    return pl.pallas_call(
        body, name=name, grid=(t // tm, n // tn),
        in_specs=[pl.BlockSpec((tm, k), lambda m, j: (m, 0)), pl.BlockSpec((tn, k), lambda m, j: (j, 0))],
        out_specs=pl.BlockSpec((tm, tn), lambda m, j: (m, j)),
        out_shape=jax.ShapeDtypeStruct((t, n), F32),
        compiler_params=_params(2),
    )(a, b)


def _du(dz, w_main, dzal, w_alow, tm=1024, tn=1024, tk=3072, behind=None, side=None):
    t, n = dz.shape
    tm = min(tm, t)
    d = w_main.shape[0]
    grid = (t // tm, d // tn, n // tk)
    dep_args, dep_specs = _behind(behind)
    adams = [_SideAdamW(s, grid) for s in (side or [])]
    n_dep, n_side = len(dep_args), len(adams)

    def body(a_ref, w_ref, al_ref, wa_ref, *rest):
        o_ref = rest[n_dep + 4 * n_side]

        @pl.when(pl.program_id(2) == 0)
        def _():
            o_ref[...] = _dot(al_ref[...], wa_ref[...], _NT)

        o_ref[...] += _dot(a_ref[...], w_ref[...], _NT)
        for k, adam in enumerate(adams):
            first_out = n_dep + 4 * n_side + 1 + 4 * k
            adam.run(rest[n_dep + 4 * k:n_dep + 4 * k + 4], rest[first_out:first_out + 4])

    outs = pl.pallas_call(
        body, name="du", grid=grid,
        in_specs=[pl.BlockSpec((tm, tk), lambda m, j, kk: (m, kk)), pl.BlockSpec((tn, tk), lambda m, j, kk: (j, kk)),
                  pl.BlockSpec((tm, LANES), lambda m, j, kk: (m, 0)), pl.BlockSpec((tn, LANES), lambda m, j, kk: (j, 0))]
        + dep_specs + [s for adam in adams for s in adam.in_specs],
        out_specs=[pl.BlockSpec((tm, tn), lambda m, j, kk: (m, j))] + [s for adam in adams for s in adam.out_specs],
        out_shape=[jax.ShapeDtypeStruct((t, d), F32)] + [s for adam in adams for s in adam.out_shape],
        compiler_params=_params(3),
    )(dz, w_main, dzal, w_alow, *dep_args, *[a for adam in adams for a in adam.args])
    return outs[0], [outs[1 + 4 * k:5 + 4 * k] for k in range(n_side)]


def _loss_head(x2, gf, tgt, tr=256):
    t, d = x2.shape

    def body(x_ref, g_ref, t_ref, dx_ref, dxb_ref, loss_ref, dg_ref):
        @pl.when(pl.program_id(0) == 0)
        def _():
            loss_ref[...] = jnp.zeros_like(loss_ref)
            dg_ref[...] = jnp.zeros_like(dg_ref)

        xf = x_ref[...]
        g = g_ref[...]
        r = lax.rsqrt(jnp.mean(xf * xf, axis=-1, keepdims=True) + EPS)
        xh = xf * r
        e = xh * g - t_ref[...]
        loss_ref[...] += 0.5 * jnp.sum(jnp.mean(e * e, axis=-1, keepdims=True))
        dy = e * (1.0 / d)
        dg_ref[...] += jnp.sum(dy * xh, axis=0, keepdims=True)
        dyg = dy * g
        dx = r * (dyg - xh * jnp.mean(dyg * xh, axis=-1, keepdims=True))
        dx_ref[...] = dx
        dxb_ref[...] = dx.astype(BF16)

    return pl.pallas_call(
        body, name="loss_head", grid=(t // tr,),
        in_specs=[pl.BlockSpec((tr, d), lambda i: (i, 0)), pl.BlockSpec((1, d), lambda i: (0, 0)),
                  pl.BlockSpec((tr, d), lambda i: (i, 0))],
        out_specs=[pl.BlockSpec((tr, d), lambda i: (i, 0)), pl.BlockSpec((tr, d), lambda i: (i, 0)),
                   pl.BlockSpec((SUBLANES, LANES), lambda i: (0, 0)), pl.BlockSpec((1, d), lambda i: (0, 0))],
        out_shape=[jax.ShapeDtypeStruct((t, d), F32), jax.ShapeDtypeStruct((t, d), BF16),
                   jax.ShapeDtypeStruct((SUBLANES, LANES), F32), jax.ShapeDtypeStruct((1, d), F32)],
        compiler_params=_params(1),
    )(x2, gf, tgt)


def _norm_bwd(name, dh, xin, g, dres, with_bf16, tr=256):
    t, d = xin.shape

    def body(dh_ref, x_ref, g_ref, dr_ref, dx_ref, *rest):
        dg_ref = rest[-1]

        @pl.when(pl.program_id(0) == 0)
        def _():
            dg_ref[...] = jnp.zeros_like(dg_ref)

        xf = x_ref[...]
        dhv = dh_ref[...]
        r = lax.rsqrt(jnp.mean(xf * xf, axis=-1, keepdims=True) + EPS)
        xh = xf * r
        dg_ref[...] += jnp.sum(dhv * xh, axis=0, keepdims=True)
        dyg = dhv * g_ref[...]
        dx = dr_ref[...] + r * (dyg - xh * jnp.mean(dyg * xh, axis=-1, keepdims=True))
        dx_ref[...] = dx
        if with_bf16:
            rest[0][...] = dx.astype(BF16)

    rows = pl.BlockSpec((tr, d), lambda i: (i, 0))
    vec = pl.BlockSpec((1, d), lambda i: (0, 0))
    return pl.pallas_call(
        body, name=name, grid=(t // tr,),
        in_specs=[rows, rows, vec, rows],
        out_specs=[rows] + [rows] * with_bf16 + [vec],
        out_shape=[jax.ShapeDtypeStruct((t, d), F32)] + [jax.ShapeDtypeStruct((t, d), BF16)] * with_bf16
        + [jax.ShapeDtypeStruct((1, d), F32)],
        compiler_params=_params(1),
    )(dh, xin, g, dres)


MIX_TILE = 256
CHUNKS_PER_TILE = MIX_TILE // CHUNK
CHUNK_SHIFT = CHUNK.bit_length() - 1
assert 1 << CHUNK_SHIFT == CHUNK


def _chunk_masks(n):
    row = lax.broadcasted_iota(jnp.int32, (n, n), 0)
    col = lax.broadcasted_iota(jnp.int32, (n, n), 1)
    same = lax.shift_right_logical(row, CHUNK_SHIFT) == lax.shift_right_logical(col, CHUNK_SHIFT)
    one = lambda m: jnp.where(m, 1.0, 0.0).astype(BF16)
    return one(same & (col > row)), one(same), one(same & (col < row))


def _mask_dot(mask, x):
    hi = x.astype(BF16)
    r1 = x - hi.astype(F32)
    mid = r1.astype(BF16)
    lo = (r1 - mid.astype(F32)).astype(BF16)
    return _dot(mask, hi) + _dot(mask, mid) + _dot(mask, lo)


def _log_sigmoid(x):
    return jnp.minimum(x, 0.0) - jnp.log1p(jnp.exp(-jnp.abs(x)))


def _conv_taps(prev8, uc, w):
    ext = jnp.concatenate([prev8, uc], axis=0)
    s1 = pltpu.roll(ext, 1, 0)[SUBLANES:]
    s2 = pltpu.roll(ext, 2, 0)[SUBLANES:]
    return s2 * w[0:1] + s1 * w[1:2] + uc * w[2:3], s1, s2


def _z_specs(tile, idx):
    d_conv = 1024
    wide = lambda c: pl.BlockSpec((tile, d_conv), lambda i, c=c: (idx(i), c))
    half = lambda c: pl.BlockSpec((tile, d_conv // 2), lambda i, c=c: (idx(i), c))
    return [wide(0), wide(1), wide(2), half(6), half(7), wide(4), wide(5)]


def _mixer_fwd(z, alow, wgu, b_gate, convw, conv_g, gla_g):
    t = z.shape[0]
    tb, cpt = MIX_TILE, CHUNKS_PER_TILE
    d_conv = conv_g.shape[1]
    dv = gla_g.shape[1]
    dk = dv // 2
    d_k = GLA_HEADS * dk
    gw = d_conv // CONV_GROUPS
    scale = dk ** -0.5

    def body(cb_ref, cc_ref, ch_ref, q_ref, k_ref, v_ref, og_ref, al_ref, wgu_ref, bg_ref, cw_ref, cg_ref, gg_ref,
             y_ref, sall_ref, carry_ref, s_ref):
        @pl.when(pl.program_id(0) == 0)
        def _():
            carry_ref[...] = jnp.zeros_like(carry_ref)
            s_ref[...] = jnp.zeros_like(s_ref)

        uc = cc_ref[...] * ch_ref[...]
        conv, _, _ = _conv_taps(carry_ref[...], uc, cw_ref[...])
        carry_ref[...] = uc[tb - SUBLANES:]
        ypre = cb_ref[...] * conv
        cg = cg_ref[...]
        for g in range(CONV_GROUPS):
            sl = slice(g * gw, (g + 1) * gw)
            seg = ypre[:, sl]
            r = lax.rsqrt(jnp.mean(seg * seg, axis=-1, keepdims=True) + EPS)
            y_ref[:, sl] = (seg * r * cg[:, sl]).astype(BF16)

        later, same, _ = _chunk_masks(tb)
        pre = _dot(al_ref[...].astype(BF16), wgu_ref[...]) + bg_ref[...]
        la = _log_sigmoid(pre) * (1.0 / GATE_NORMALIZER)
        e_dec = _mask_dot(later, la)
        dec_all = jnp.exp(_mask_dot(same, la))
        kdec = (k_ref[...] * jnp.exp(e_dec)).astype(BF16)
        qs = (q_ref[...] * scale).astype(BF16)
        vb = v_ref[...].astype(BF16)
        gg = gg_ref[...]
        rows = [slice(c * CHUNK, (c + 1) * CHUNK) for c in range(cpt)]
        ks = [slice(h * dk, (h + 1) * dk) for h in range(GLA_HEADS)]
        vs = [slice(h * dv, (h + 1) * dv) for h in range(GLA_HEADS)]
        kvt = [[_dot(vb[rows[c], vs[h]], kdec[rows[c], ks[h]], _TN) for h in range(GLA_HEADS)] for c in range(cpt)]
        state = [s_ref[h] for h in range(GLA_HEADS)]
        states = []
        for c in range(cpt):
            state = [state[h] * dec_all[c * CHUNK:c * CHUNK + 1, ks[h]] + kvt[c][h] for h in range(GLA_HEADS)]
            states.append(state)
            for h in range(GLA_HEADS):
                sall_ref[c, h] = state[h]
        for h in range(GLA_HEADS):
            s_ref[h] = state[h]
        for h in range(GLA_HEADS):
            o = jnp.concatenate(
                [_dot(qs[rows[c], ks[h]], states[c][h].astype(BF16), _NT) for c in range(cpt)], axis=0)
            ro = lax.rsqrt(jnp.mean(o * o, axis=-1, keepdims=True) + EPS)
            ogs = og_ref[:, vs[h]]
            yg = o * ro * gg * (ogs * jax.nn.sigmoid(ogs))
            y_ref[:, d_conv + h * dv:d_conv + (h + 1) * dv] = yg.astype(BF16)

    full = lambda shape: pl.BlockSpec(shape, lambda i: (0,) * len(shape))
    return pl.pallas_call(
        body, name="mixer_fwd", grid=(t // tb,),
        in_specs=_z_specs(tb, lambda i: i) + [
            pl.BlockSpec((tb, LANES), lambda i: (i, 0)), full(wgu.shape), full(b_gate.shape), full(convw.shape),
            full(conv_g.shape), full(gla_g.shape)],
        out_specs=[pl.BlockSpec((tb, d_conv + GLA_HEADS * dv), lambda i: (i, 0)),
                   pl.BlockSpec((cpt, GLA_HEADS, dv, dk), lambda i: (i, 0, 0, 0))],
        out_shape=[jax.ShapeDtypeStruct((t, d_conv + GLA_HEADS * dv), BF16),
                   jax.ShapeDtypeStruct((t // CHUNK, GLA_HEADS, dv, dk), F32)],
        scratch_shapes=[pltpu.VMEM((SUBLANES, d_conv), F32), pltpu.VMEM((GLA_HEADS, dv, dk), F32)],
        compiler_params=_params(1),
    )(z, z, z, z, z, z, z, alow, wgu, b_gate, convw, conv_g, gla_g)


def _mixer_bwd(z, alow, dy, sall, wgu, b_gate, convw, conv_g, gla_g, behind=None):
    t = z.shape[0]
    tb, cpt = MIX_TILE, CHUNKS_PER_TILE
    nt = t // tb
    d_conv = conv_g.shape[1]
    dv = gla_g.shape[1]
    dk = dv // 2
    d_k = GLA_HEADS * dk
    gw = d_conv // CONV_GROUPS
    scale = dk ** -0.5
    rev = lambda i: nt - 1 - i
    dep_args, dep_specs = _behind(behind)

    def body(cb_ref, cc_ref, ch_ref, q_ref, k_ref, v_ref, og_ref, ccp_ref, chp_ref, al_ref, dy_ref, sall_ref, sprev_ref,
             wgu_ref, bg_ref, cw_ref, cg_ref, gg_ref, *rest):
        dz_ref, dzal_ref, dcw_ref, dcg_ref, dgg_ref, dbg_ref, dwgu_ref, dcarry_ref, gd_ref = rest[-9:]
        i = pl.program_id(0)

        @pl.when(i == 0)
        def _():
            dcarry_ref[...] = jnp.zeros_like(dcarry_ref)
            gd_ref[...] = jnp.zeros_like(gd_ref)
            dcw_ref[...] = jnp.zeros_like(dcw_ref)
            dcg_ref[...] = jnp.zeros_like(dcg_ref)
            dgg_ref[...] = jnp.zeros_like(dgg_ref)
            dbg_ref[...] = jnp.zeros_like(dbg_ref)
            dwgu_ref[...] = jnp.zeros_like(dwgu_ref)

        first = rev(i) == 0

        cb, cc, ch = cb_ref[...], cc_ref[...], ch_ref[...]
        w = cw_ref[...]
        uc = cc * ch
        prev8 = jnp.where(first, 0.0, ccp_ref[...] * chp_ref[...])
        conv, s1, s2 = _conv_taps(prev8, uc, w)
        ypre = cb * conv
        cg = cg_ref[...]
        dypre_parts = []
        for g in range(CONV_GROUPS):
            sl = slice(g * gw, (g + 1) * gw)
            seg = ypre[:, sl]
            r = lax.rsqrt(jnp.mean(seg * seg, axis=-1, keepdims=True) + EPS)
            yn = seg * r
            dyc = dy_ref[:, sl]
            dcg_ref[:, sl] += jnp.sum(dyc * yn, axis=0, keepdims=True)
            dyn = dyc * cg[:, sl]
            dypre_parts.append(r * (dyn - yn * jnp.mean(dyn * yn, axis=-1, keepdims=True)))
        dypre = jnp.concatenate(dypre_parts, axis=1)
        dconv = dypre * cb
        dz_ref[:, 0:d_conv] = (dypre * conv).astype(BF16)
        dcw_ref[0:1] += jnp.sum(dconv * s2, axis=0, keepdims=True)
        dcw_ref[1:2] += jnp.sum(dconv * s1, axis=0, keepdims=True)
        dcw_ref[2:3] += jnp.sum(dconv * uc, axis=0, keepdims=True)
        ext = jnp.concatenate([dconv, dcarry_ref[...]], axis=0)
        f1 = pltpu.roll(ext, tb + SUBLANES - 1, 0)[:tb]
        f2 = pltpu.roll(ext, tb + SUBLANES - 2, 0)[:tb]
        dcarry_ref[...] = dconv[:SUBLANES]
        duc = dconv * w[2:3] + f1 * w[1:2] + f2 * w[0:1]
        dz_ref[:, d_conv:2 * d_conv] = (duc * ch).astype(BF16)
        dz_ref[:, 2 * d_conv:3 * d_conv] = (duc * cc).astype(BF16)

        q_off = 3 * d_conv
        k_off = q_off + d_k
        v_off = k_off + d_k
        og_off = v_off + GLA_HEADS * dv
        later, same, earlier = _chunk_masks(tb)
        alb = al_ref[...].astype(BF16)
        pre = _dot(alb, wgu_ref[...]) + bg_ref[...]
        la = _log_sigmoid(pre) * (1.0 / GATE_NORMALIZER)
        exp_e = jnp.exp(_mask_dot(later, la))
        dec_all = jnp.exp(_mask_dot(same, la))
        kdec = k_ref[...] * exp_e
        kdec_b = kdec.astype(BF16)
        qs = (q_ref[...] * scale).astype(BF16)
        vb = v_ref[...].astype(BF16)
        gg = gg_ref[...]
        rows = [slice(c * CHUNK, (c + 1) * CHUNK) for c in range(cpt)]
        ks = [slice(h * dk, (h + 1) * dk) for h in range(GLA_HEADS)]
        vs = [slice(h * dv, (h + 1) * dv) for h in range(GLA_HEADS)]
        st_b = [[sall_ref[c, h].astype(BF16) for h in range(GLA_HEADS)] for c in range(cpt)]
        do_b = []
        dgg = jnp.zeros_like(gg)
        for h in range(GLA_HEADS):
            o = jnp.concatenate([_dot(qs[rows[c], ks[h]], st_b[c][h], _NT) for c in range(cpt)], axis=0)
            ro = lax.rsqrt(jnp.mean(o * o, axis=-1, keepdims=True) + EPS)
            on = o * ro
            ogs = og_ref[:, vs[h]]
            sg = jax.nn.sigmoid(ogs)
            gate = ogs * sg
            dyg = dy_ref[:, d_conv + h * dv:d_conv + (h + 1) * dv]
            dgg = dgg + jnp.sum(dyg * on * gate, axis=0, keepdims=True)
            dz_ref[:, og_off + h * dv:og_off + (h + 1) * dv] = (
                dyg * on * gg * (sg * (1.0 + ogs * (1.0 - sg)))).astype(BF16)
            don = dyg * gg * gate
            do_b.append((ro * (don - on * jnp.mean(don * on, axis=-1, keepdims=True))).astype(BF16))
        dgg_ref[...] += dgg
        for h in range(GLA_HEADS):
            dq = jnp.concatenate([_dot(do_b[h][rows[c]], st_b[c][h]) for c in range(cpt)], axis=0)
            dz_ref[:, q_off + h * dk:q_off + (h + 1) * dk] = (dq * scale).astype(BF16)
        own = [[_dot(do_b[h][rows[c]], qs[rows[c], ks[h]], _TN) for h in range(GLA_HEADS)] for c in range(cpt)]
        carried = [gd_ref[h] for h in range(GLA_HEADS)]
        gt_b = [None] * cpt
        ddd = [None] * cpt
        for c in reversed(range(cpt)):
            gt = [own[c][h] + carried[h] for h in range(GLA_HEADS)]
            dec = [dec_all[c * CHUNK:c * CHUNK + 1, ks[h]] for h in range(GLA_HEADS)]
            carried = [gt[h] * dec[h] for h in range(GLA_HEADS)]
            if c > 0:
                st_prev = [sall_ref[c - 1, h] for h in range(GLA_HEADS)]
            else:
                st_prev = [jnp.where(first, 0.0, sprev_ref[0, h]) for h in range(GLA_HEADS)]
            ddec = [jnp.sum(gt[h] * st_prev[h], axis=0, keepdims=True) * dec[h] for h in range(GLA_HEADS)]
            ddd[c] = jnp.broadcast_to(jnp.concatenate(ddec, axis=1), (CHUNK, d_k))
            gt_b[c] = [gt[h].astype(BF16) for h in range(GLA_HEADS)]
        for h in range(GLA_HEADS):
            gd_ref[h] = carried[h]
        dkdec_cols = []
        for h in range(GLA_HEADS):
            dvh = jnp.concatenate([_dot(kdec_b[rows[c], ks[h]], gt_b[c][h], _NT) for c in range(cpt)], axis=0)
            dz_ref[:, v_off + h * dv:v_off + (h + 1) * dv] = dvh.astype(BF16)
            dkdec_cols.append(jnp.concatenate([_dot(vb[rows[c], vs[h]], gt_b[c][h]) for c in range(cpt)], axis=0))
        dkdec = jnp.concatenate(dkdec_cols, axis=1)
        dz_ref[:, k_off:k_off + d_k] = (dkdec * exp_e).astype(BF16)
        dla = _mask_dot(earlier, dkdec * kdec) + jnp.concatenate(ddd, axis=0)
        dpre = dla * (1.0 / GATE_NORMALIZER) * jax.nn.sigmoid(-pre)
        dbg_ref[...] += jnp.sum(dpre, axis=0, keepdims=True)
        dpre_b = dpre.astype(BF16)
        dwgu_ref[...] += _dot(alb, dpre_b, _TN)
        dzal_ref[...] = _dot(dpre_b, wgu_ref[...], _NT).astype(BF16)

    full = lambda shape: pl.BlockSpec(shape, lambda i: (0,) * len(shape))
    prev_rows = lambda c: pl.BlockSpec(
        (SUBLANES, d_conv), lambda i, c=c: (jnp.maximum(rev(i) * (tb // SUBLANES) - 1, 0), c))
    n_z = 3 * d_conv + 2 * d_k + 2 * GLA_HEADS * dv
    return pl.pallas_call(
        body, name="mixer_bwd", grid=(nt,),
        in_specs=_z_specs(tb, rev) + [
            prev_rows(1), prev_rows(2),
            pl.BlockSpec((tb, LANES), lambda i: (rev(i), 0)),
            pl.BlockSpec((tb, d_conv + GLA_HEADS * dv), lambda i: (rev(i), 0)),
            pl.BlockSpec((cpt, GLA_HEADS, dv, dk), lambda i: (rev(i), 0, 0, 0)),
            pl.BlockSpec((1, GLA_HEADS, dv, dk), lambda i: (jnp.maximum(rev(i) * cpt - 1, 0), 0, 0, 0)),
            full(wgu.shape), full(b_gate.shape), full(convw.shape), full(conv_g.shape), full(gla_g.shape)]
        + dep_specs,
        out_specs=[pl.BlockSpec((tb, n_z), lambda i: (rev(i), 0)), pl.BlockSpec((tb, LANES), lambda i: (rev(i), 0)),
                   full(convw.shape), full(conv_g.shape), full(gla_g.shape), full(b_gate.shape), full(wgu.shape)],
        out_shape=[jax.ShapeDtypeStruct((t, n_z), BF16), jax.ShapeDtypeStruct((t, LANES), BF16),
                   jax.ShapeDtypeStruct(convw.shape, F32), jax.ShapeDtypeStruct(conv_g.shape, F32),
                   jax.ShapeDtypeStruct(gla_g.shape, F32), jax.ShapeDtypeStruct(b_gate.shape, F32),
                   jax.ShapeDtypeStruct(wgu.shape, F32)],
        scratch_shapes=[pltpu.VMEM((SUBLANES, d_conv), F32), pltpu.VMEM((GLA_HEADS, dv, dk), F32)],
        compiler_params=_params(1),
    )(z, z, z, z, z, z, z, z, z, alow, dy, sall, sall, wgu, b_gate, convw, conv_g, gla_g, *dep_args)


def _adamw_math(g, w, m, v):
    m = ADAM_B1 * m + (1.0 - ADAM_B1) * g
    v = ADAM_B2 * v + (1.0 - ADAM_B2) * (g * g)
    m_hat = m / (1.0 - ADAM_B1 ** ADAM_STEP)
    v_hat = v / (1.0 - ADAM_B2 ** ADAM_STEP)
    delta = -ADAM_LR * (m_hat / (jnp.sqrt(v_hat) + ADAM_EPS) + ADAM_WD * w)
    return delta, m, v


def _adamw(name, parts, w, m, v, tr):
    r, c = w.shape
    n_parts = parts.shape[0]

    def body(p_ref, w_ref, m_ref, v_ref, g_ref, d_ref, nm_ref, nv_ref):
        g = p_ref[0].astype(F32)
        for j in range(1, n_parts):
            g = g + p_ref[j].astype(F32)
        g_ref[...] = g
        d_ref[...], nm_ref[...], nv_ref[...] = _adamw_math(g, w_ref[...], m_ref[...], v_ref[...])

    blk = pl.BlockSpec((tr, c), lambda i: (i, 0))
    return pl.pallas_call(
        body, name=name, grid=(r // tr,),
        in_specs=[pl.BlockSpec((n_parts, tr, c), lambda i: (0, i, 0)), blk, blk, blk],
        out_specs=[blk] * 4, out_shape=[jax.ShapeDtypeStruct((r, c), F32)] * 4,
        compiler_params=_params(1),
    )(parts, w, m, v)


def _adamw_small(grads, ws, ms, vs):
    n = len(grads)

    def body(*refs):
        g, w, m, v = (refs[k * n:(k + 1) * n] for k in range(4))
        d_out, m_out, v_out = (refs[(4 + k) * n:(5 + k) * n] for k in range(3))
        for i in range(n):
            d_out[i][...], m_out[i][...], v_out[i][...] = _adamw_math(g[i][...], w[i][...], m[i][...], v[i][...])

    vmem = pl.BlockSpec(memory_space=pltpu.VMEM)
    outs = pl.pallas_call(
        body, name="adamw_small", out_shape=[jax.ShapeDtypeStruct(w.shape, F32) for w in ws] * 3,
        in_specs=[vmem] * (4 * n), out_specs=[vmem] * (3 * n),
    )(*grads, *ws, *ms, *vs)
    return [outs[:n], outs[n:2 * n], outs[2 * n:]]


def _sum_partials(parts):
    n_parts, rows, lanes = parts.shape

    def body(p_ref, o_ref):
        g = p_ref[0]
        for j in range(1, n_parts):
            g = g + p_ref[j]
        o_ref[...] = g

    return pl.pallas_call(
        body, name="sum_small_partials", out_shape=jax.ShapeDtypeStruct((rows, lanes), F32),
        in_specs=[pl.BlockSpec(memory_space=pltpu.VMEM)], out_specs=pl.BlockSpec(memory_space=pltpu.VMEM),
    )(parts)


def _pack_rows(vectors, rows):
    flat = jnp.concatenate([a.reshape(-1).astype(F32) for a in vectors])
    return jnp.pad(flat, (0, rows * LANES - flat.shape[0])).reshape(rows, LANES)


def _unpack_rows(block, shapes):
    flat = block.reshape(-1)
    out, off = [], 0
    for s in shapes:
        n = 1
        for dim in s:
            n *= dim
        out.append(flat[off:off + n].reshape(s))
        off += n
    return out


def kernel(x, norm1_g, w_in, w_gate_up, b_gate, conv_w, conv_norm_g, gla_norm_g, w_out, norm2_g, w_ff1, w_ff2, norm_f_g, loss_target, m_norm1_g, m_w_in, m_w_gate_up, m_b_gate, m_conv_w, m_conv_norm_g, m_gla_norm_g, m_w_out, m_norm2_g, m_w_ff1, m_w_ff2, m_norm_f_g, v_norm1_g, v_w_in, v_w_gate_up, v_b_gate, v_conv_w, v_conv_norm_g, v_gla_norm_g, v_w_out, v_norm2_g, v_w_ff1, v_w_ff2, v_norm_f_g):
    me = _device_index()
    x2d, tgt = x[0], loss_target[0]
    t, d = x2d.shape
    d_in_shard = w_in.shape[2]
    d_in = N_DEV * d_in_shard
    n_main = d_in - GATE_RANK
    d_conv = conv_norm_g.shape[1]
    d_k = b_gate.shape[1]
    d_ff = N_DEV * w_ff1.shape[2]
    wmv = dict(
        norm1_g=(norm1_g, m_norm1_g, v_norm1_g), w_in=(w_in, m_w_in, v_w_in),
        w_gate_up=(w_gate_up, m_w_gate_up, v_w_gate_up), b_gate=(b_gate, m_b_gate, v_b_gate),
        conv_w=(conv_w, m_conv_w, v_conv_w), conv_norm_g=(conv_norm_g, m_conv_norm_g, v_conv_norm_g),
        gla_norm_g=(gla_norm_g, m_gla_norm_g, v_gla_norm_g), w_out=(w_out, m_w_out, v_w_out),
        norm2_g=(norm2_g, m_norm2_g, v_norm2_g), w_ff1=(w_ff1, m_w_ff1, v_w_ff1), w_ff2=(w_ff2, m_w_ff2, v_w_ff2),
        norm_f_g=(norm_f_g, m_norm_f_g, v_norm_f_g))

    small_rows = 16
    first_level = (SIBLING,) + SAME_CORE_PEERS
    win_shard = w_in[0].astype(BF16)
    in_send, in_recv, in_src, in_land, token = _exchange_start(
        "all_gather_start_w_in", [win_shard], [_land_zone(win_shard)], scatter=False, masks=[first_level])
    _, wgu_t, cw_t, wout_t, w1_t, w2_t = lax.optimization_barrier((token, w_gate_up, conv_w, w_out, w_ff1, w_ff2))
    small_shard = _pack_rows([wgu_t[0], cw_t[0]], small_rows)
    shards = [small_shard, wout_t[0].astype(BF16), w1_t[0].astype(BF16), w2_t[0].astype(BF16)]
    ag_send, ag_recv, ag_src, ag_land, token = _exchange_start(
        "all_gather_start", shards, [_land_zone(s) for s in shards], scatter=False, behind=token)

    def gathered(k, name, after):
        return _exchange_wait(name, ag_send[k], ag_recv[k], ag_src[k], ag_land[k], after, scatter=False)

    u = _rmsnorm(x2d, norm1_g, behind=token)
    tied = lax.optimization_barrier((token, w_in, m_w_in, v_w_in))
    wmv["w_in"] = tuple(tied[1:])
    small_g = gathered(0, "all_gather_wait_small", [u] + [a[0] for a in wmv["w_in"]])
    win_level1 = _exchange_wait(
        "all_gather_wait_w_in", in_send[0], in_recv[0], in_src[0], in_land[0], small_g, scatter=False,
        masks=first_level)
    win_g = _forward_wait("all_gather_wait_w_in_forwarded", *_forward_start("all_gather_forward_w_in", win_level1))
    w_main, w_alow = _shards_to_columns(win_g, n_main)
    small_flat = small_g.reshape(N_DEV, -1)
    n_wgu = GATE_RANK * (d_k // N_DEV)
    wgu_full = small_flat[:, :n_wgu].reshape(N_DEV, GATE_RANK, d_k // N_DEV).transpose(1, 0, 2).reshape(GATE_RANK, d_k)
    conv_w_full = small_flat[:, n_wgu:n_wgu + (d_conv // N_DEV) * CONV_WIDTH].reshape(d_conv, CONV_WIDTH)
    wgu_pad = jnp.pad(wgu_full, ((0, LANES - GATE_RANK), (0, 0))).astype(BF16)
    convw_taps = jnp.pad(conv_w_full.T, ((0, SUBLANES - CONV_WIDTH), (0, 0)))

    get_w_out = lambda after: gathered(1, "all_gather_wait_w_out", after).reshape(-1, d)
    get_w1 = lambda after: gathered(2, "all_gather_wait_w_ff1", after)
    get_w2 = lambda after: gathered(3, "all_gather_wait_w_ff2", after).reshape(d_ff, d)

    in_flight = {}

    def send_partials(name, parts):
        own = lax.dynamic_index_in_dim(parts, me, axis=0, keepdims=False)
        send, recv, src, land, token = _exchange_start("scatter_start_" + name, [parts], [_land_zone(own)], scatter=True)
        in_flight[name] = (send[0], recv[0], src[0], land[0])
        return token

    def on_grad(name, value):
        if name == "w_in":
            main, alow_part = value
            value = _columns_to_shards(main, alow_part, N_DEV, d_in_shard)
        elif name in ("w_out", "w_ff2"):
            value = value.reshape(N_DEV, -1, d)
        return send_partials(name, value)

    def received(name, after):
        send, recv, src, land = in_flight[name]
        return _exchange_wait("scatter_wait_" + name, send, recv, src, land, after, scatter=True)

    def side_for(name, after):
        return (received(name, after),) + tuple(a[0] for a in wmv[name])

    grads = _local_step(x2d, u, tgt, norm1_g, w_main, w_alow, wgu_pad, b_gate, convw_taps, conv_norm_g, gla_norm_g,
                        norm2_g, norm_f_g, get_w_out, get_w1, get_w2, on_grad, side_for)
    grad_x = grads["x"]

    small_shapes = [(1, d), (1, d_k), (1, d_conv), (1, gla_norm_g.shape[1]), (1, d), (d,),
                    (GATE_RANK, d_k), (d_conv, CONV_WIDTH), (1,)]
    small_grad_rows = 152
    small_part = _pack_rows(
        [grads["norm1_g"], grads["b_gate"], grads["conv_norm_g"], grads["gla_norm_g"], grads["norm2_g"],
         grads["norm_f_g"], grads["w_gate_up"][:GATE_RANK], grads["conv_w"][:CONV_WIDTH].T, grads["loss"][0, 0]],
        small_grad_rows)
    small_token = send_partials("small", jnp.broadcast_to(small_part[None], (N_DEV, small_grad_rows, LANES)))

    gin_r, gout_r = (received(nm, [grad_x, small_token]) for nm in ("w_in", "w_out"))
    get_small = lambda after: received("small", after)
    done = {"w_ff1": grads["adam_w_ff1"], "w_ff2": grads["adam_w_ff2"]}
    return _update(me, gin_r, gout_r, done, get_small, small_shapes, grad_x, wmv)


def _local_step(x2d, u, tgt, norm1_g, w_main, w_alow, wgu_pad, b_gate, convw_taps, conv_norm_g, gla_norm_g,
                norm2_g, norm_f_g, get_w_out, get_w1, get_w2, on_grad, side_for=lambda name, after: None):
    t, d = x2d.shape
    n_main = w_main.shape[1]

    z, alow = _inproj(u, w_main, w_alow)
    y, sall = _mixer_fwd(z, alow, wgu_pad, b_gate, convw_taps, conv_norm_g, gla_norm_g)
    w_out_full = get_w_out(y)
    x1, h = _outproj(y, w_out_full, x2d, norm2_g)
    w1g = get_w1(h)
    a = _ff1(h, w1g)
    w2_full = get_w2(a)
    d_ff = w2_full.shape[0]
    x2 = _ff2(a, w2_full, x1)
    dx2, dx2b, loss_part, d_normf = _loss_head(x2, norm_f_g.reshape(1, d), tgt)

    tk = min(4096, t)
    nk = t // tk
    da = _dff2(dx2b, w2_full, a)
    dw2 = _tn_matmul(
        "dw_ff2", a, dx2b, (d_ff // 1024, d // 1024, nk),
        pl.BlockSpec((tk, 1024), lambda m, j, kk: (kk, m)), pl.BlockSpec((tk, 1024), lambda m, j, kk: (kk, j)),
        jax.ShapeDtypeStruct((d_ff, d), BF16), pl.BlockSpec((1024, 1024), lambda m, j, kk: (m, j)), (1024, 1024),
        a_fn=_relu_sq)
    token = on_grad("w_ff2", dw2)
    f_shard = d_ff // N_DEV
    dw1 = _tn_matmul(
        "dw_ff1", h, da, (N_DEV, d // 1024, nk),
        pl.BlockSpec((tk, 1024), lambda g, m, kk: (kk, m)), pl.BlockSpec((tk, f_shard), lambda g, m, kk: (kk, g)),
        jax.ShapeDtypeStruct((N_DEV, d, f_shard), BF16), pl.BlockSpec((None, 1024, f_shard), lambda g, m, kk: (g, m, 0)),
        (1024, f_shard), behind=token)
    token = on_grad("w_ff1", dw1)
    dh = _dh(da, w1g, behind=token)
    dx1, dx1b, d_norm2 = _norm_bwd("norm2_bwd", dh, x1, norm2_g, dx2, with_bf16=True)
    dy = _nt_matmul("dy", dx1b, w_out_full)
    dwout = _tn_matmul(
        "dw_out", y, dx1b, (d // 1024, d // 1024, nk),
        pl.BlockSpec((tk, 1024), lambda m, j, kk: (kk, m)), pl.BlockSpec((tk, 1024), lambda m, j, kk: (kk, j)),
        jax.ShapeDtypeStruct((d, d), BF16), pl.BlockSpec((1024, 1024), lambda m, j, kk: (m, j)), (1024, 1024))
    token = on_grad("w_out", dwout)
    dz, dzal, d_convw, d_convg, d_glag, d_bgate, d_wgu = _mixer_bwd(
        z, alow, dy, sall, wgu_pad, b_gate, convw_taps, conv_norm_g, gla_norm_g, behind=token)
    token = on_grad("w_in", _dw_in(u, dz, dzal, tk))
    sides = [s for s in (side_for("w_ff2", token), side_for("w_ff1", token)) if s is not None]
    du, adam = _du(dz, w_main, dzal, w_alow, behind=token, side=sides)
    adam_ff2, adam_ff1 = adam if adam else (None, None)
    grad_x, d_norm1 = _norm_bwd("norm1_bwd", du, x2d, norm1_g, dx1, with_bf16=False)
    return dict(x=grad_x, loss=loss_part, adam_w_ff2=adam_ff2, adam_w_ff1=adam_ff1,
                norm1_g=d_norm1, w_gate_up=d_wgu, b_gate=d_bgate, conv_w=d_convw,
                conv_norm_g=d_convg, gla_norm_g=d_glag, norm2_g=d_norm2, norm_f_g=d_normf)


_WEIGHT_ORDER = ("norm1_g", "w_in", "w_gate_up", "b_gate", "conv_w", "conv_norm_g", "gla_norm_g", "w_out", "norm2_g",
                 "w_ff1", "w_ff2", "norm_f_g")
_SMALL_ORDER = ("norm1_g", "b_gate", "conv_norm_g", "gla_norm_g", "norm2_g", "norm_f_g", "w_gate_up", "conv_w")
def _update(me, gin_r, gout_r, done, get_small, small_shapes, grad_x, wmv):
    big = dict(done)
    big["w_in"] = _adamw("adamw_w_in", gin_r, *(a[0] for a in wmv["w_in"]), 256)
    big["w_out"] = _adamw("adamw_w_out", gout_r, *(a[0] for a in wmv["w_out"]), 128)

    wgu_cols = wmv["w_gate_up"][0].shape[2]
    cw_rows = wmv["conv_w"][0].shape[1]

    small_r = get_small([big[nm][3] for nm in ("w_in", "w_out")])
    summed = _unpack_rows(_sum_partials(small_r), small_shapes)
    summed[6] = lax.dynamic_slice_in_dim(summed[6], me * wgu_cols, wgu_cols, axis=1)
    summed[7] = lax.dynamic_slice_in_dim(summed[7], me * cw_rows, cw_rows, axis=0)
    as_2d = lambda a: a.reshape((1, -1) if a.ndim == 1 else a.shape[-2:])
    grads_2d = [as_2d(g) for g in summed[:len(_SMALL_ORDER)]]
    small = _adamw_small(grads_2d, *[[as_2d(wmv[nm][k]) for nm in _SMALL_ORDER] for k in range(3)])
    small = [grads_2d] + small

    outs = []
    for k in range(4):
        for nm in _WEIGHT_ORDER:
            if nm in big:
                outs.append(big[nm][k][None])
            else:
                outs.append(small[k][_SMALL_ORDER.index(nm)].reshape(wmv[nm][0].shape))
    loss = summed[8][0]
    return (loss, grad_x[None], *outs)
```

```python
import functools

import jax
import jax.numpy as jnp
from jax import lax
from jax.experimental import pallas as pl
from jax.experimental.pallas import tpu as pltpu

F32 = jnp.float32
BF16 = jnp.bfloat16

N_DEV = 8
CHUNK = 64
GLA_HEADS = 4
CONV_GROUPS = 8
CONV_WIDTH = 3
GATE_RANK = 16
GATE_NORMALIZER = 16.0
EPS = 1e-6
ADAM_LR = 0.001
ADAM_B1 = 0.9
ADAM_B2 = 0.999
ADAM_EPS = 1e-08
ADAM_WD = 0.01
ADAM_STEP = 10

LANES = 128
SUBLANES = 8
VMEM_LIMIT = 56 << 20

_NN = (((1,), (0,)), ((), ()))
_NT = (((1,), (1,)), ((), ()))
_TN = (((0,), (0,)), ((), ()))


def _dot(a, b, dims=_NN):
    return lax.dot_general(a, b, dims, preferred_element_type=F32)


def _params(n_grid):
    return pltpu.CompilerParams(dimension_semantics=("arbitrary",) * n_grid, vmem_limit_bytes=VMEM_LIMIT)


def _relu_sq(a):
    r = jnp.maximum(a, 0.0)
    return r * r


def _device_index():
    return 4 * lax.axis_index("x") + 2 * lax.axis_index("y") + lax.axis_index("c")


def _peer(mask):
    x, y, c = lax.axis_index("x"), lax.axis_index("y"), lax.axis_index("c")
    return (x ^ ((mask >> 2) & 1), y ^ ((mask >> 1) & 1), c ^ (mask & 1))


_HBM_SPEC = pl.BlockSpec(memory_space=pltpu.HBM)
_SEM_SPEC = pl.BlockSpec(memory_space=pltpu.SEMAPHORE)
_SIDE_EFFECT = pltpu.SideEffectType.DATAFLOW_SIDE_EFFECTING
N_PEERS = N_DEV - 1


def _exchange_copy(src_ref, land_ref, send_sems, recv_sems, mask, scatter, arriving):
    me = _device_index()
    src = src_ref.at[me ^ mask] if scatter else src_ref
    dst = land_ref.at[(me ^ mask) if arriving else me]
    return pltpu.make_async_remote_copy(
        src_ref=src, dst_ref=dst, send_sem=send_sems.at[mask - 1], recv_sem=recv_sems.at[mask - 1],
        device_id=_peer(mask), device_id_type=pl.DeviceIdType.MESH)


def _land_zone(own):
    zone = lax.empty((N_DEV,) + own.shape, own.dtype)
    return lax.dynamic_update_slice(zone, own[None], (_device_index(),) + (0,) * own.ndim)


ALL_PEERS = tuple(range(1, N_DEV))
SIBLING = 1
SAME_CORE_PEERS = (2, 4, 6)


def _exchange_start(name, srcs, lands, scatter, masks=None, behind=None):
    n = len(srcs)
    masks = masks or [ALL_PEERS] * n
    dep_args = [] if behind is None else [behind]

    def body(*refs):
        src, land = refs[:n], refs[n:2 * n]
        outs = refs[2 * n + len(dep_args):]
        send_sems, recv_sems = outs[:n], outs[n:2 * n]
        token = refs[-1]
        for a in range(n):
            for mask in masks[a]:
                _exchange_copy(src[a], land[a], send_sems[a], recv_sems[a], mask, scatter, False).start()
        token[...] = jnp.zeros_like(token)

    hbm = lambda a: pltpu.HBM(a.shape, a.dtype)
    outs = pl.pallas_call(
        body, name=name,
        out_shape=([pltpu.SemaphoreType.DMA((N_PEERS,))] * (2 * n) + [hbm(a) for a in srcs] + [hbm(a) for a in lands]
                   + [jax.ShapeDtypeStruct((SUBLANES, LANES), F32)]),
        in_specs=[_HBM_SPEC] * (2 * n) + [pl.BlockSpec(memory_space=pl.ANY)] * len(dep_args),
        out_specs=[_SEM_SPEC] * (2 * n) + [_HBM_SPEC] * (2 * n) + [pl.BlockSpec(memory_space=pltpu.VMEM)],
        input_output_aliases={a: 2 * n + a for a in range(2 * n)},
        compiler_params=pltpu.CompilerParams(has_side_effects=_SIDE_EFFECT),
    )(*[pltpu.with_memory_space_constraint(a, pltpu.HBM) for a in list(srcs) + list(lands)], *dep_args)
    send_sems, recv_sems = outs[:n], outs[n:2 * n]
    src_thru, land_thru = outs[2 * n:3 * n], outs[3 * n:4 * n]
    return send_sems, recv_sems, src_thru, land_thru, outs[-1]


def _exchange_wait(name, send_sems, recv_sems, src_thru, land_thru, after, scatter, masks=ALL_PEERS):
    after = list(after) if isinstance(after, (list, tuple)) else [after]

    def body(src_ref, land_ref, send_ref, recv_ref, *rest):
        for mask in masks:
            cp = _exchange_copy(src_ref, land_ref, send_ref, recv_ref, mask, scatter, True)
            cp.wait_send()
            cp.wait_recv()

    return pl.pallas_call(
        body, name=name,
        out_shape=(pltpu.HBM(src_thru.shape, src_thru.dtype), pltpu.HBM(land_thru.shape, land_thru.dtype)),
        in_specs=[_HBM_SPEC, _HBM_SPEC, _SEM_SPEC, _SEM_SPEC] + [pl.BlockSpec(memory_space=pl.ANY)] * len(after),
        out_specs=(_HBM_SPEC, _HBM_SPEC), input_output_aliases={0: 0, 1: 1},
        compiler_params=pltpu.CompilerParams(has_side_effects=_SIDE_EFFECT),
    )(src_thru, land_thru, send_sems, recv_sems, *after)[1]


def _forward_copy(land_ref, send_sems, recv_sems, k, arriving):
    me = _device_index()
    slot = me ^ SAME_CORE_PEERS[k]
    return pltpu.make_async_remote_copy(
        src_ref=land_ref.at[slot], dst_ref=land_ref.at[(slot ^ SIBLING) if arriving else slot],
        send_sem=send_sems.at[k], recv_sem=recv_sems.at[k],
        device_id=_peer(SIBLING), device_id_type=pl.DeviceIdType.MESH)


def _forward_start(name, land):
    n_fwd = len(SAME_CORE_PEERS)

    def body(land_ref, send_sems, recv_sems, land_thru):
        for k in range(n_fwd):
            _forward_copy(land_ref, send_sems, recv_sems, k, False).start()

    send, recv, thru = pl.pallas_call(
        body, name=name,
        out_shape=[pltpu.SemaphoreType.DMA((n_fwd,)), pltpu.SemaphoreType.DMA((n_fwd,)), pltpu.HBM(land.shape, land.dtype)],
        in_specs=[_HBM_SPEC], out_specs=[_SEM_SPEC, _SEM_SPEC, _HBM_SPEC], input_output_aliases={0: 2},
        compiler_params=pltpu.CompilerParams(has_side_effects=_SIDE_EFFECT),
    )(pltpu.with_memory_space_constraint(land, pltpu.HBM))
    return send, recv, thru


def _forward_wait(name, send_sems, recv_sems, land_thru):
    def body(land_ref, send_ref, recv_ref, got_ref):
        for k in range(len(SAME_CORE_PEERS)):
            cp = _forward_copy(land_ref, send_ref, recv_ref, k, True)
            cp.wait_send()
            cp.wait_recv()

    return pl.pallas_call(
        body, name=name, out_shape=pltpu.HBM(land_thru.shape, land_thru.dtype),
        in_specs=[_HBM_SPEC, _SEM_SPEC, _SEM_SPEC], out_specs=_HBM_SPEC, input_output_aliases={0: 0},
        compiler_params=pltpu.CompilerParams(has_side_effects=_SIDE_EFFECT),
    )(land_thru, send_sems, recv_sems)


def _shards_to_columns(g, n_main, tr=256):
    n_dev, d, s = g.shape

    def body(g_ref, main_ref, rest_ref):
        for j in range(n_dev):
            lo, hi = j * s, (j + 1) * s
            if hi <= n_main:
                main_ref[:, lo:hi] = g_ref[j]
            else:
                main_ref[:, lo:n_main] = g_ref[j, :, 0:n_main - lo]
                rest_ref[...] = jnp.zeros_like(rest_ref)
                rest_ref[:, 0:hi - n_main] = g_ref[j, :, n_main - lo:s]

    return pl.pallas_call(
        body, grid=(d // tr,), name="shards_to_columns",
        in_specs=[pl.BlockSpec((n_dev, tr, s), lambda i: (0, i, 0))],
        out_specs=[pl.BlockSpec((tr, n_main), lambda i: (i, 0)), pl.BlockSpec((tr, LANES), lambda i: (i, 0))],
        out_shape=[jax.ShapeDtypeStruct((d, n_main), g.dtype), jax.ShapeDtypeStruct((d, LANES), g.dtype)],
        compiler_params=_params(1),
    )(g)


def _columns_to_shards(main, rest, n_dev, s, tr=256):
    d, n_main = main.shape
    assert (n_dev - 1) * s <= n_main < n_dev * s

    def body(main_ref, rest_ref, o_ref):
        for j in range(n_dev):
            lo, hi = j * s, (j + 1) * s
            if hi <= n_main:
                o_ref[j] = main_ref[:, lo:hi]
            else:
                o_ref[j, :, 0:n_main - lo] = main_ref[:, lo:n_main]
                o_ref[j, :, n_main - lo:s] = rest_ref[:, 0:hi - n_main]

    return pl.pallas_call(
        body, grid=(d // tr,), name="columns_to_shards",
        in_specs=[pl.BlockSpec((tr, n_main), lambda i: (i, 0)), pl.BlockSpec((tr, LANES), lambda i: (i, 0))],
        out_specs=pl.BlockSpec((n_dev, tr, s), lambda i: (0, i, 0)),
        out_shape=jax.ShapeDtypeStruct((n_dev, d, s), main.dtype),
        compiler_params=_params(1),
    )(main, rest)


def _rmsnorm(x, g, tr=512, behind=None):
    t, d = x.shape
    tr = min(tr, t)
    dep_args, dep_specs = _behind(behind)

    def body(x_ref, g_ref, *rest):
        u_ref = rest[-1]
        xf = x_ref[...]
        r = lax.rsqrt(jnp.mean(xf * xf, axis=-1, keepdims=True) + EPS)
        u_ref[...] = (xf * r * g_ref[...]).astype(BF16)

    return pl.pallas_call(
        body, name="rmsnorm1", grid=(t // tr,),
        in_specs=[pl.BlockSpec((tr, d), lambda i: (i, 0)), pl.BlockSpec((1, d), lambda i: (0, 0))] + dep_specs,
        out_specs=pl.BlockSpec((tr, d), lambda i: (i, 0)),
        out_shape=jax.ShapeDtypeStruct((t, d), BF16),
        compiler_params=_params(1),
    )(x, g, *dep_args)


def _inproj(u, w_main, w_alow, tm=1024, tn=1024):
    t, d = u.shape
    tm = min(tm, t)
    n = w_main.shape[1]

    def body(u_ref, w_ref, wa_ref, z_ref, al_ref):
        @pl.when(pl.program_id(1) == 0)
        def _():
            al_ref[...] = _dot(u_ref[...], wa_ref[...])

        z_ref[...] = _dot(u_ref[...], w_ref[...])

    return pl.pallas_call(
        body, name="inproj", grid=(t // tm, n // tn),
        in_specs=[pl.BlockSpec((tm, d), lambda m, j: (m, 0)), pl.BlockSpec((d, tn), lambda m, j: (0, j)),
                  pl.BlockSpec((d, LANES), lambda m, j: (0, 0))],
        out_specs=[pl.BlockSpec((tm, tn), lambda m, j: (m, j)), pl.BlockSpec((tm, LANES), lambda m, j: (m, 0))],
        out_shape=[jax.ShapeDtypeStruct((t, n), F32), jax.ShapeDtypeStruct((t, LANES), F32)],
        compiler_params=_params(2),
    )(u, w_main, w_alow)


def _outproj(y, w_out, x, g2, tm=512):
    t, d = x.shape
    tm = min(tm, t)
    k = y.shape[1]

    def body(y_ref, w_ref, x_ref, g_ref, x1_ref, h_ref):
        x1 = x_ref[...] + _dot(y_ref[...], w_ref[...])
        x1_ref[...] = x1
        r = lax.rsqrt(jnp.mean(x1 * x1, axis=-1, keepdims=True) + EPS)
        h_ref[...] = (x1 * r * g_ref[...]).astype(BF16)

    return pl.pallas_call(
        body, name="outproj_rmsnorm", grid=(t // tm,),
        in_specs=[pl.BlockSpec((tm, k), lambda m: (m, 0)), pl.BlockSpec((k, d), lambda m: (0, 0)),
                  pl.BlockSpec((tm, d), lambda m: (m, 0)), pl.BlockSpec((1, d), lambda m: (0, 0))],
        out_specs=[pl.BlockSpec((tm, d), lambda m: (m, 0)), pl.BlockSpec((tm, d), lambda m: (m, 0))],
        out_shape=[jax.ShapeDtypeStruct((t, d), F32), jax.ShapeDtypeStruct((t, d), BF16)],
        compiler_params=_params(1),
    )(y, w_out, x, g2)


def _ff1(h, w1g, tm=1024):
    t, d = h.shape
    tm = min(tm, t)
    g, _, f = w1g.shape

    def body(h_ref, w_ref, a_ref):
        a_ref[...] = _dot(h_ref[...], w_ref[...]).astype(BF16)

    return pl.pallas_call(
        body, name="ff1", grid=(t // tm, g),
        in_specs=[pl.BlockSpec((tm, d), lambda m, j: (m, 0)), pl.BlockSpec((None, d, f), lambda m, j: (j, 0, 0))],
        out_specs=pl.BlockSpec((tm, f), lambda m, j: (m, j)),
        out_shape=jax.ShapeDtypeStruct((t, g * f), BF16),
        compiler_params=_params(2),
    )(h, w1g)


def _ff2(a, w2, x1, tm=1024, tn=1024, tk=2048):
    t, f = a.shape
    tm = min(tm, t)
    d = w2.shape[1]

    def body(a_ref, w_ref, x1_ref, o_ref):
        @pl.when(pl.program_id(2) == 0)
        def _():
            o_ref[...] = x1_ref[...]

        o_ref[...] += _dot(_relu_sq(a_ref[...]), w_ref[...])

    return pl.pallas_call(
        body, name="ff2_residual", grid=(t // tm, d // tn, f // tk),
        in_specs=[pl.BlockSpec((tm, tk), lambda m, j, kk: (m, kk)), pl.BlockSpec((tk, tn), lambda m, j, kk: (kk, j)),
                  pl.BlockSpec((tm, tn), lambda m, j, kk: (m, j))],
        out_specs=pl.BlockSpec((tm, tn), lambda m, j, kk: (m, j)),
        out_shape=jax.ShapeDtypeStruct((t, d), F32),
        compiler_params=_params(3),
    )(a, w2, x1)


def _dff2(dx2b, w2, a, tm=1024, tn=1024):
    t, d = dx2b.shape
    tm = min(tm, t)
    f = w2.shape[0]

    def body(g_ref, w_ref, a_ref, o_ref):
        dp = _dot(g_ref[...], w_ref[...], _NT)
        o_ref[...] = (dp * (2.0 * jnp.maximum(a_ref[...].astype(F32), 0.0))).astype(BF16)

    return pl.pallas_call(
        body, name="dff2", grid=(t // tm, f // tn),
        in_specs=[pl.BlockSpec((tm, d), lambda m, j: (m, 0)), pl.BlockSpec((tn, d), lambda m, j: (j, 0)),
                  pl.BlockSpec((tm, tn), lambda m, j: (m, j))],
        out_specs=pl.BlockSpec((tm, tn), lambda m, j: (m, j)),
        out_shape=jax.ShapeDtypeStruct((t, f), BF16),
        compiler_params=_params(2),
    )(dx2b, w2, a)


def _behind(token):
    if token is None:
        return [], []
    return [token], [pl.BlockSpec(token.shape, lambda *_: (0,) * token.ndim)]


def _tn_matmul(name, a, b, grid, a_spec, b_spec, out_shape, out_spec, acc_shape, a_fn=None, behind=None):
    nk = grid[-1]
    dep_args, dep_specs = _behind(behind)

    def body(a_ref, b_ref, *rest):
        o_ref, acc_ref = rest[-2:]
        kk = pl.program_id(len(grid) - 1)
        av = a_ref[...]
        if a_fn is not None:
            av = a_fn(av)
        part = _dot(av, b_ref[...], _TN)

        @pl.when(kk == 0)
        def _():
            acc_ref[...] = part

        @pl.when(kk > 0)
        def _():
            acc_ref[...] += part

        @pl.when(kk == nk - 1)
        def _():
            o_ref[...] = acc_ref[...].astype(o_ref.dtype)

    return pl.pallas_call(
        body, name=name, grid=grid, in_specs=[a_spec, b_spec] + dep_specs, out_specs=out_spec, out_shape=out_shape,
        scratch_shapes=[pltpu.VMEM(acc_shape, F32)], compiler_params=_params(len(grid)),
    )(a, b, *dep_args)


def _dw_in(u, dz, dzal, tk, tm=1024, tn=1024):
    t, d = u.shape
    n_main = dz.shape[1]
    nk = t // tk

    def body(a_ref, b_ref, al_ref, o_ref, oal_ref, acc_ref, accal_ref):
        j, kk = pl.program_id(1), pl.program_id(2)
        av = a_ref[...]

        def accumulate(acc, part, out):
            @pl.when(kk == 0)
            def _():
                acc[...] = part

            @pl.when(kk > 0)
            def _():
                acc[...] += part

            @pl.when(kk == nk - 1)
            def _():
                out[...] = acc[...].astype(out.dtype)

        accumulate(acc_ref, _dot(av, b_ref[...], _TN), o_ref)

        @pl.when(j == 0)
        def _():
            accumulate(accal_ref, _dot(av, al_ref[...], _TN), oal_ref)

    return pl.pallas_call(
        body, name="dw_in", grid=(d // tm, n_main // tn, nk),
        in_specs=[pl.BlockSpec((tk, tm), lambda m, j, kk: (kk, m)), pl.BlockSpec((tk, tn), lambda m, j, kk: (kk, j)),
                  pl.BlockSpec((tk, LANES), lambda m, j, kk: (kk, 0))],
        out_specs=[pl.BlockSpec((tm, tn), lambda m, j, kk: (m, j)), pl.BlockSpec((tm, LANES), lambda m, j, kk: (m, 0))],
        out_shape=[jax.ShapeDtypeStruct((d, n_main), BF16), jax.ShapeDtypeStruct((d, LANES), BF16)],
        scratch_shapes=[pltpu.VMEM((tm, tn), F32), pltpu.VMEM((tm, LANES), F32)],
        compiler_params=_params(3),
    )(u, dz, dzal)


class _SideAdamW:
    def __init__(self, side, grid):
        parts, w, m, v = side
        n_parts, r, c = parts.shape
        steps = 1
        for extent in grid:
            steps *= extent
        rows = r // steps
        assert rows * steps == r and rows % (2 * SUBLANES) == 0, (r, steps)

        def step(*ids):
            lin = ids[0]
            for extent, idx in zip(grid[1:], ids[1:]):
                lin = lin * extent + idx
            return lin

        slab = pl.BlockSpec((rows, c), lambda *ids: (step(*ids), 0))
        self.args = [parts, w, m, v]
        self.in_specs = [pl.BlockSpec((n_parts, rows, c), lambda *ids: (0, step(*ids), 0)), slab, slab, slab]
        self.out_specs = [slab] * 4
        self.out_shape = [jax.ShapeDtypeStruct((r, c), F32)] * 4
        self.n_parts = n_parts

    def run(self, in_refs, out_refs):
        p_ref, w_ref, m_ref, v_ref = in_refs
        g = p_ref[0].astype(F32)
        for j in range(1, self.n_parts):
            g = g + p_ref[j].astype(F32)
        out_refs[0][...] = g
        out_refs[1][...], out_refs[2][...], out_refs[3][...] = _adamw_math(g, w_ref[...], m_ref[...], v_ref[...])


def _dh(da, w1g, tm=1024, tn=1024, shards_per_step=4, behind=None):
    t = da.shape[0]
    tm = min(tm, t)
    g, d, f = w1g.shape
    sps = shards_per_step
    grid = (t // tm, d // tn, g // sps)
    dep_args, dep_specs = _behind(behind)

    def body(a_ref, w_ref, *rest):
        o_ref = rest[-1]
        acc = _dot(a_ref[:, 0:f], w_ref[0], _NT)
        for s in range(1, sps):
            acc = acc + _dot(a_ref[:, s * f:(s + 1) * f], w_ref[s], _NT)

        @pl.when(pl.program_id(2) == 0)
        def _():
            o_ref[...] = acc

        @pl.when(pl.program_id(2) > 0)
        def _():
            o_ref[...] += acc

    return pl.pallas_call(
        body, name="dh", grid=grid,
        in_specs=[pl.BlockSpec((tm, sps * f), lambda m, j, kk: (m, kk)),
                  pl.BlockSpec((sps, tn, f), lambda m, j, kk: (kk, j, 0))] + dep_specs,
        out_specs=pl.BlockSpec((tm, tn), lambda m, j, kk: (m, j)),
        out_shape=jax.ShapeDtypeStruct((t, d), F32),
        compiler_params=_params(3),
    )(da, w1g, *dep_args)


def _nt_matmul(name, a, b, tm=1024, tn=1024):
    t, k = a.shape
    tm = min(tm, t)
    n = b.shape[0]

    def body(a_ref, b_ref, o_ref):
        o_ref[...] = _dot(a_ref[...], b_ref[...], _NT)

    return pl.pallas_call(
        body, name=name, grid=(t // tm, n // tn),
        in_specs=[pl.BlockSpec((tm, k), lambda m, j: (m, 0)), pl.BlockSpec((tn, k), lambda m, j: (j, 0))],
        out_specs=pl.BlockSpec((tm, tn), lambda m, j: (m, j)),
        out_shape=jax.ShapeDtypeStruct((t, n), F32),
        compiler_params=_params(2),
    )(a, b)


def _du(dz, w_main, dzal, w_alow, tm=1024, tn=1024, tk=3072, behind=None, side=None):
    t, n = dz.shape
    tm = min(tm, t)
    d = w_main.shape[0]
    grid = (t // tm, d // tn, n // tk)
    dep_args, dep_specs = _behind(behind)
    adams = [_SideAdamW(s, grid) for s in (side or [])]
    n_dep, n_side = len(dep_args), len(adams)

    def body(a_ref, w_ref, al_ref, wa_ref, *rest):
        o_ref = rest[n_dep + 4 * n_side]

        @pl.when(pl.program_id(2) == 0)
        def _():
            o_ref[...] = _dot(al_ref[...], wa_ref[...], _NT)

        o_ref[...] += _dot(a_ref[...], w_ref[...], _NT)
        for k, adam in enumerate(adams):
            first_out = n_dep + 4 * n_side + 1 + 4 * k
            adam.run(rest[n_dep + 4 * k:n_dep + 4 * k + 4], rest[first_out:first_out + 4])

    outs = pl.pallas_call(
        body, name="du", grid=grid,
        in_specs=[pl.BlockSpec((tm, tk), lambda m, j, kk: (m, kk)), pl.BlockSpec((tn, tk), lambda m, j, kk: (j, kk)),
                  pl.BlockSpec((tm, LANES), lambda m, j, kk: (m, 0)), pl.BlockSpec((tn, LANES), lambda m, j, kk: (j, 0))]
        + dep_specs + [s for adam in adams for s in adam.in_specs],
        out_specs=[pl.BlockSpec((tm, tn), lambda m, j, kk: (m, j))] + [s for adam in adams for s in adam.out_specs],
        out_shape=[jax.ShapeDtypeStruct((t, d), F32)] + [s for adam in adams for s in adam.out_shape],
        compiler_params=_params(3),
    )(dz, w_main, dzal, w_alow, *dep_args, *[a for adam in adams for a in adam.args])
    return outs[0], [outs[1 + 4 * k:5 + 4 * k] for k in range(n_side)]


def _loss_head(x2, gf, tgt, tr=256):
    t, d = x2.shape

    def body(x_ref, g_ref, t_ref, dx_ref, dxb_ref, loss_ref, dg_ref):
        @pl.when(pl.program_id(0) == 0)
        def _():
            loss_ref[...] = jnp.zeros_like(loss_ref)
            dg_ref[...] = jnp.zeros_like(dg_ref)

        xf = x_ref[...]
        g = g_ref[...]
        r = lax.rsqrt(jnp.mean(xf * xf, axis=-1, keepdims=True) + EPS)
        xh = xf * r
        e = xh * g - t_ref[...]
        loss_ref[...] += 0.5 * jnp.sum(jnp.mean(e * e, axis=-1, keepdims=True))
        dy = e * (1.0 / d)
        dg_ref[...] += jnp.sum(dy * xh, axis=0, keepdims=True)
        dyg = dy * g
        dx = r * (dyg - xh * jnp.mean(dyg * xh, axis=-1, keepdims=True))
        dx_ref[...] = dx
        dxb_ref[...] = dx.astype(BF16)

    return pl.pallas_call(
        body, name="loss_head", grid=(t // tr,),
        in_specs=[pl.BlockSpec((tr, d), lambda i: (i, 0)), pl.BlockSpec((1, d), lambda i: (0, 0)),
                  pl.BlockSpec((tr, d), lambda i: (i, 0))],
        out_specs=[pl.BlockSpec((tr, d), lambda i: (i, 0)), pl.BlockSpec((tr, d), lambda i: (i, 0)),
                   pl.BlockSpec((SUBLANES, LANES), lambda i: (0, 0)), pl.BlockSpec((1, d), lambda i: (0, 0))],
        out_shape=[jax.ShapeDtypeStruct((t, d), F32), jax.ShapeDtypeStruct((t, d), BF16),
                   jax.ShapeDtypeStruct((SUBLANES, LANES), F32), jax.ShapeDtypeStruct((1, d), F32)],
        compiler_params=_params(1),
    )(x2, gf, tgt)


def _norm_bwd_dy(dh, x1, g2, dx2, w_out, tm=256):
    t, d = x1.shape
    k = w_out.shape[0]
    tm = min(tm, t)

    def body(dh_ref, x_ref, g_ref, dr_ref, w_ref, dx_ref, dxb_ref, dg_ref, dy_ref):
        @pl.when(pl.program_id(0) == 0)
        def _():
            dg_ref[...] = jnp.zeros_like(dg_ref)

        xf = x_ref[...]
        dhv = dh_ref[...]
        r = lax.rsqrt(jnp.mean(xf * xf, axis=-1, keepdims=True) + EPS)
        xh = xf * r
        dg_ref[...] += jnp.sum(dhv * xh, axis=0, keepdims=True)
        dyg = dhv * g_ref[...]
        dx = dr_ref[...] + r * (dyg - xh * jnp.mean(dyg * xh, axis=-1, keepdims=True))
        dx_ref[...] = dx
        dxb = dx.astype(BF16)
        dxb_ref[...] = dxb
        dy_ref[...] = _dot(dxb, w_ref[...], _NT)

    rows = pl.BlockSpec((tm, d), lambda i: (i, 0))
    vec = pl.BlockSpec((1, d), lambda i: (0, 0))
    return pl.pallas_call(
        body, name="norm2_bwd_dy", grid=(t // tm,),
        in_specs=[rows, rows, vec, rows, pl.BlockSpec((k, d), lambda i: (0, 0))],
        out_specs=[rows, rows, vec, pl.BlockSpec((tm, k), lambda i: (i, 0))],
        out_shape=[jax.ShapeDtypeStruct((t, d), F32), jax.ShapeDtypeStruct((t, d), BF16),
                   jax.ShapeDtypeStruct((1, d), F32), jax.ShapeDtypeStruct((t, k), F32)],
        compiler_params=_params(1),
    )(dh, x1, g2, dx2, w_out)


def _norm_bwd(name, dh, xin, g, dres, with_bf16, tr=256):
    t, d = xin.shape

    def body(dh_ref, x_ref, g_ref, dr_ref, dx_ref, *rest):
        dg_ref = rest[-1]

        @pl.when(pl.program_id(0) == 0)
        def _():
            dg_ref[...] = jnp.zeros_like(dg_ref)

        xf = x_ref[...]
        dhv = dh_ref[...]
        r = lax.rsqrt(jnp.mean(xf * xf, axis=-1, keepdims=True) + EPS)
        xh = xf * r
        dg_ref[...] += jnp.sum(dhv * xh, axis=0, keepdims=True)
        dyg = dhv * g_ref[...]
        dx = dr_ref[...] + r * (dyg - xh * jnp.mean(dyg * xh, axis=-1, keepdims=True))
        dx_ref[...] = dx
        if with_bf16:
            rest[0][...] = dx.astype(BF16)

    rows = pl.BlockSpec((tr, d), lambda i: (i, 0))
    vec = pl.BlockSpec((1, d), lambda i: (0, 0))
    return pl.pallas_call(
        body, name=name, grid=(t // tr,),
        in_specs=[rows, rows, vec, rows],
        out_specs=[rows] + [rows] * with_bf16 + [vec],
        out_shape=[jax.ShapeDtypeStruct((t, d), F32)] + [jax.ShapeDtypeStruct((t, d), BF16)] * with_bf16
        + [jax.ShapeDtypeStruct((1, d), F32)],
        compiler_params=_params(1),
    )(dh, xin, g, dres)


MIX_TILE = 256
CHUNKS_PER_TILE = MIX_TILE // CHUNK
CHUNK_SHIFT = CHUNK.bit_length() - 1
assert 1 << CHUNK_SHIFT == CHUNK


def _chunk_masks(n):
    row = lax.broadcasted_iota(jnp.int32, (n, n), 0)
    col = lax.broadcasted_iota(jnp.int32, (n, n), 1)
    same = lax.shift_right_logical(row, CHUNK_SHIFT) == lax.shift_right_logical(col, CHUNK_SHIFT)
    one = lambda m: jnp.where(m, 1.0, 0.0).astype(BF16)
    return one(same & (col > row)), one(same), one(same & (col < row))


def _mask_dot(mask, x):
    hi = x.astype(BF16)
    r1 = x - hi.astype(F32)
    mid = r1.astype(BF16)
    lo = (r1 - mid.astype(F32)).astype(BF16)
    return _dot(mask, hi) + _dot(mask, mid) + _dot(mask, lo)


def _log_sigmoid(x):
    return jnp.minimum(x, 0.0) - jnp.log1p(jnp.exp(-jnp.abs(x)))


def _conv_taps(prev8, uc, w):
    ext = jnp.concatenate([prev8, uc], axis=0)
    s1 = pltpu.roll(ext, 1, 0)[SUBLANES:]
    s2 = pltpu.roll(ext, 2, 0)[SUBLANES:]
    return s2 * w[0:1] + s1 * w[1:2] + uc * w[2:3], s1, s2


def _z_specs(tile, idx):
    d_conv = 1024
    wide = lambda c: pl.BlockSpec((tile, d_conv), lambda i, c=c: (idx(i), c))
    half = lambda c: pl.BlockSpec((tile, d_conv // 2), lambda i, c=c: (idx(i), c))
    return [wide(0), wide(1), wide(2), half(6), half(7), wide(4), wide(5)]


def _mixer_fwd(z, alow, wgu, b_gate, convw, conv_g, gla_g):
    t = z.shape[0]
    tb, cpt = MIX_TILE, CHUNKS_PER_TILE
    d_conv = conv_g.shape[1]
    dv = gla_g.shape[1]
    dk = dv // 2
    d_k = GLA_HEADS * dk
    gw = d_conv // CONV_GROUPS
    scale = dk ** -0.5

    def body(cb_ref, cc_ref, ch_ref, q_ref, k_ref, v_ref, og_ref, al_ref, wgu_ref, bg_ref, cw_ref, cg_ref, gg_ref,
             y_ref, sall_ref, carry_ref, s_ref):
        @pl.when(pl.program_id(0) == 0)
        def _():
            carry_ref[...] = jnp.zeros_like(carry_ref)
            s_ref[...] = jnp.zeros_like(s_ref)

        uc = cc_ref[...] * ch_ref[...]
        conv, _, _ = _conv_taps(carry_ref[...], uc, cw_ref[...])
        carry_ref[...] = uc[tb - SUBLANES:]
        ypre = cb_ref[...] * conv
        cg = cg_ref[...]
        for g in range(CONV_GROUPS):
            sl = slice(g * gw, (g + 1) * gw)
            seg = ypre[:, sl]
            r = lax.rsqrt(jnp.mean(seg * seg, axis=-1, keepdims=True) + EPS)
            y_ref[:, sl] = (seg * r * cg[:, sl]).astype(BF16)

        later, same, _ = _chunk_masks(tb)
        pre = _dot(al_ref[...].astype(BF16), wgu_ref[...]) + bg_ref[...]
        la = _log_sigmoid(pre) * (1.0 / GATE_NORMALIZER)
        e_dec = _mask_dot(later, la)
        dec_all = jnp.exp(_mask_dot(same, la))
        kdec = (k_ref[...] * jnp.exp(e_dec)).astype(BF16)
        qs = (q_ref[...] * scale).astype(BF16)
        vb = v_ref[...].astype(BF16)
        gg = gg_ref[...]
        rows = [slice(c * CHUNK, (c + 1) * CHUNK) for c in range(cpt)]
        ks = [slice(h * dk, (h + 1) * dk) for h in range(GLA_HEADS)]
        vs = [slice(h * dv, (h + 1) * dv) for h in range(GLA_HEADS)]
        kvt = [[_dot(vb[rows[c], vs[h]], kdec[rows[c], ks[h]], _TN) for h in range(GLA_HEADS)] for c in range(cpt)]
        state = [s_ref[h] for h in range(GLA_HEADS)]
        states = []
        for c in range(cpt):
            state = [state[h] * dec_all[c * CHUNK:c * CHUNK + 1, ks[h]] + kvt[c][h] for h in range(GLA_HEADS)]
            states.append(state)
            for h in range(GLA_HEADS):
                sall_ref[c, h] = state[h]
        for h in range(GLA_HEADS):
            s_ref[h] = state[h]
        for h in range(GLA_HEADS):
            o = jnp.concatenate(
                [_dot(qs[rows[c], ks[h]], states[c][h].astype(BF16), _NT) for c in range(cpt)], axis=0)
            ro = lax.rsqrt(jnp.mean(o * o, axis=-1, keepdims=True) + EPS)
            ogs = og_ref[:, vs[h]]
            yg = o * ro * gg * (ogs * jax.nn.sigmoid(ogs))
            y_ref[:, d_conv + h * dv:d_conv + (h + 1) * dv] = yg.astype(BF16)

    full = lambda shape: pl.BlockSpec(shape, lambda i: (0,) * len(shape))
    return pl.pallas_call(
        body, name="mixer_fwd", grid=(t // tb,),
        in_specs=_z_specs(tb, lambda i: i) + [
            pl.BlockSpec((tb, LANES), lambda i: (i, 0)), full(wgu.shape), full(b_gate.shape), full(convw.shape),
            full(conv_g.shape), full(gla_g.shape)],
        out_specs=[pl.BlockSpec((tb, d_conv + GLA_HEADS * dv), lambda i: (i, 0)),
                   pl.BlockSpec((cpt, GLA_HEADS, dv, dk), lambda i: (i, 0, 0, 0))],
        out_shape=[jax.ShapeDtypeStruct((t, d_conv + GLA_HEADS * dv), BF16),
                   jax.ShapeDtypeStruct((t // CHUNK, GLA_HEADS, dv, dk), F32)],
        scratch_shapes=[pltpu.VMEM((SUBLANES, d_conv), F32), pltpu.VMEM((GLA_HEADS, dv, dk), F32)],
        compiler_params=_params(1),
    )(z, z, z, z, z, z, z, alow, wgu, b_gate, convw, conv_g, gla_g)


def _mixer_bwd(z, alow, dy, sall, wgu, b_gate, convw, conv_g, gla_g, behind=None):
    t = z.shape[0]
    tb, cpt = MIX_TILE, CHUNKS_PER_TILE
    nt = t // tb
    d_conv = conv_g.shape[1]
    dv = gla_g.shape[1]
    dk = dv // 2
    d_k = GLA_HEADS * dk
    gw = d_conv // CONV_GROUPS
    scale = dk ** -0.5
    rev = lambda i: nt - 1 - i
    dep_args, dep_specs = _behind(behind)

    def body(cb_ref, cc_ref, ch_ref, q_ref, k_ref, v_ref, og_ref, ccp_ref, chp_ref, al_ref, dy_ref, sall_ref, sprev_ref,
             wgu_ref, bg_ref, cw_ref, cg_ref, gg_ref, *rest):
        dz_ref, dzal_ref, dcw_ref, dcg_ref, dgg_ref, dbg_ref, dwgu_ref, dcarry_ref, gd_ref = rest[-9:]
        i = pl.program_id(0)

        @pl.when(i == 0)
        def _():
            dcarry_ref[...] = jnp.zeros_like(dcarry_ref)
            gd_ref[...] = jnp.zeros_like(gd_ref)
            dcw_ref[...] = jnp.zeros_like(dcw_ref)
            dcg_ref[...] = jnp.zeros_like(dcg_ref)
            dgg_ref[...] = jnp.zeros_like(dgg_ref)
            dbg_ref[...] = jnp.zeros_like(dbg_ref)
            dwgu_ref[...] = jnp.zeros_like(dwgu_ref)

        first = rev(i) == 0

        cb, cc, ch = cb_ref[...], cc_ref[...], ch_ref[...]
        w = cw_ref[...]
        uc = cc * ch
        prev8 = jnp.where(first, 0.0, ccp_ref[...] * chp_ref[...])
        conv, s1, s2 = _conv_taps(prev8, uc, w)
        ypre = cb * conv
        cg = cg_ref[...]
        dypre_parts = []
        for g in range(CONV_GROUPS):
            sl = slice(g * gw, (g + 1) * gw)
            seg = ypre[:, sl]
            r = lax.rsqrt(jnp.mean(seg * seg, axis=-1, keepdims=True) + EPS)
            yn = seg * r
            dyc = dy_ref[:, sl]
            dcg_ref[:, sl] += jnp.sum(dyc * yn, axis=0, keepdims=True)
            dyn = dyc * cg[:, sl]
            dypre_parts.append(r * (dyn - yn * jnp.mean(dyn * yn, axis=-1, keepdims=True)))
        dypre = jnp.concatenate(dypre_parts, axis=1)
        dconv = dypre * cb
        dz_ref[:, 0:d_conv] = (dypre * conv).astype(BF16)
        dcw_ref[0:1] += jnp.sum(dconv * s2, axis=0, keepdims=True)
        dcw_ref[1:2] += jnp.sum(dconv * s1, axis=0, keepdims=True)
        dcw_ref[2:3] += jnp.sum(dconv * uc, axis=0, keepdims=True)
        ext = jnp.concatenate([dconv, dcarry_ref[...]], axis=0)
        f1 = pltpu.roll(ext, tb + SUBLANES - 1, 0)[:tb]
        f2 = pltpu.roll(ext, tb + SUBLANES - 2, 0)[:tb]
        dcarry_ref[...] = dconv[:SUBLANES]
        duc = dconv * w[2:3] + f1 * w[1:2] + f2 * w[0:1]
        dz_ref[:, d_conv:2 * d_conv] = (duc * ch).astype(BF16)
        dz_ref[:, 2 * d_conv:3 * d_conv] = (duc * cc).astype(BF16)

        q_off = 3 * d_conv
        k_off = q_off + d_k
        v_off = k_off + d_k
        og_off = v_off + GLA_HEADS * dv
        later, same, earlier = _chunk_masks(tb)
        alb = al_ref[...].astype(BF16)
        pre = _dot(alb, wgu_ref[...]) + bg_ref[...]
        la = _log_sigmoid(pre) * (1.0 / GATE_NORMALIZER)
        exp_e = jnp.exp(_mask_dot(later, la))
        dec_all = jnp.exp(_mask_dot(same, la))
        kdec = k_ref[...] * exp_e
        kdec_b = kdec.astype(BF16)
        qs = (q_ref[...] * scale).astype(BF16)
        vb = v_ref[...].astype(BF16)
        gg = gg_ref[...]
        rows = [slice(c * CHUNK, (c + 1) * CHUNK) for c in range(cpt)]
        ks = [slice(h * dk, (h + 1) * dk) for h in range(GLA_HEADS)]
        vs = [slice(h * dv, (h + 1) * dv) for h in range(GLA_HEADS)]
        st_b = [[sall_ref[c, h].astype(BF16) for h in range(GLA_HEADS)] for c in range(cpt)]
        do_b = []
        dgg = jnp.zeros_like(gg)
        for h in range(GLA_HEADS):
            o = jnp.concatenate([_dot(qs[rows[c], ks[h]], st_b[c][h], _NT) for c in range(cpt)], axis=0)
            ro = lax.rsqrt(jnp.mean(o * o, axis=-1, keepdims=True) + EPS)
            on = o * ro
            ogs = og_ref[:, vs[h]]
            sg = jax.nn.sigmoid(ogs)
            gate = ogs * sg
            dyg = dy_ref[:, d_conv + h * dv:d_conv + (h + 1) * dv]
            dgg = dgg + jnp.sum(dyg * on * gate, axis=0, keepdims=True)
            dz_ref[:, og_off + h * dv:og_off + (h + 1) * dv] = (
                dyg * on * gg * (sg * (1.0 + ogs * (1.0 - sg)))).astype(BF16)
            don = dyg * gg * gate
            do_b.append((ro * (don - on * jnp.mean(don * on, axis=-1, keepdims=True))).astype(BF16))
        dgg_ref[...] += dgg
        for h in range(GLA_HEADS):
            dq = jnp.concatenate([_dot(do_b[h][rows[c]], st_b[c][h]) for c in range(cpt)], axis=0)
            dz_ref[:, q_off + h * dk:q_off + (h + 1) * dk] = (dq * scale).astype(BF16)
        own = [[_dot(do_b[h][rows[c]], qs[rows[c], ks[h]], _TN) for h in range(GLA_HEADS)] for c in range(cpt)]
        carried = [gd_ref[h] for h in range(GLA_HEADS)]
        gt_b = [None] * cpt
        ddd = [None] * cpt
        for c in reversed(range(cpt)):
            gt = [own[c][h] + carried[h] for h in range(GLA_HEADS)]
            dec = [dec_all[c * CHUNK:c * CHUNK + 1, ks[h]] for h in range(GLA_HEADS)]
            carried = [gt[h] * dec[h] for h in range(GLA_HEADS)]
            if c > 0:
                st_prev = [sall_ref[c - 1, h] for h in range(GLA_HEADS)]
            else:
                st_prev = [jnp.where(first, 0.0, sprev_ref[0, h]) for h in range(GLA_HEADS)]
            ddec = [jnp.sum(gt[h] * st_prev[h], axis=0, keepdims=True) * dec[h] for h in range(GLA_HEADS)]
            ddd[c] = jnp.broadcast_to(jnp.concatenate(ddec, axis=1), (CHUNK, d_k))
            gt_b[c] = [gt[h].astype(BF16) for h in range(GLA_HEADS)]
        for h in range(GLA_HEADS):
            gd_ref[h] = carried[h]
        dkdec_cols = []
        for h in range(GLA_HEADS):
            dvh = jnp.concatenate([_dot(kdec_b[rows[c], ks[h]], gt_b[c][h], _NT) for c in range(cpt)], axis=0)
            dz_ref[:, v_off + h * dv:v_off + (h + 1) * dv] = dvh.astype(BF16)
            dkdec_cols.append(jnp.concatenate([_dot(vb[rows[c], vs[h]], gt_b[c][h]) for c in range(cpt)], axis=0))
        dkdec = jnp.concatenate(dkdec_cols, axis=1)
        dz_ref[:, k_off:k_off + d_k] = (dkdec * exp_e).astype(BF16)
        dla = _mask_dot(earlier, dkdec * kdec) + jnp.concatenate(ddd, axis=0)
        dpre = dla * (1.0 / GATE_NORMALIZER) * jax.nn.sigmoid(-pre)
        dbg_ref[...] += jnp.sum(dpre, axis=0, keepdims=True)
        dpre_b = dpre.astype(BF16)
        dwgu_ref[...] += _dot(alb, dpre_b, _TN)
        dzal_ref[...] = _dot(dpre_b, wgu_ref[...], _NT).astype(BF16)

    full = lambda shape: pl.BlockSpec(shape, lambda i: (0,) * len(shape))
    prev_rows = lambda c: pl.BlockSpec(
        (SUBLANES, d_conv), lambda i, c=c: (jnp.maximum(rev(i) * (tb // SUBLANES) - 1, 0), c))
    n_z = 3 * d_conv + 2 * d_k + 2 * GLA_HEADS * dv
    return pl.pallas_call(
        body, name="mixer_bwd", grid=(nt,),
        in_specs=_z_specs(tb, rev) + [
            prev_rows(1), prev_rows(2),
            pl.BlockSpec((tb, LANES), lambda i: (rev(i), 0)),
            pl.BlockSpec((tb, d_conv + GLA_HEADS * dv), lambda i: (rev(i), 0)),
            pl.BlockSpec((cpt, GLA_HEADS, dv, dk), lambda i: (rev(i), 0, 0, 0)),
            pl.BlockSpec((1, GLA_HEADS, dv, dk), lambda i: (jnp.maximum(rev(i) * cpt - 1, 0), 0, 0, 0)),
            full(wgu.shape), full(b_gate.shape), full(convw.shape), full(conv_g.shape), full(gla_g.shape)]
        + dep_specs,
        out_specs=[pl.BlockSpec((tb, n_z), lambda i: (rev(i), 0)), pl.BlockSpec((tb, LANES), lambda i: (rev(i), 0)),
                   full(convw.shape), full(conv_g.shape), full(gla_g.shape), full(b_gate.shape), full(wgu.shape)],
        out_shape=[jax.ShapeDtypeStruct((t, n_z), BF16), jax.ShapeDtypeStruct((t, LANES), BF16),
                   jax.ShapeDtypeStruct(convw.shape, F32), jax.ShapeDtypeStruct(conv_g.shape, F32),
                   jax.ShapeDtypeStruct(gla_g.shape, F32), jax.ShapeDtypeStruct(b_gate.shape, F32),
                   jax.ShapeDtypeStruct(wgu.shape, F32)],
        scratch_shapes=[pltpu.VMEM((SUBLANES, d_conv), F32), pltpu.VMEM((GLA_HEADS, dv, dk), F32)],
        compiler_params=_params(1),
    )(z, z, z, z, z, z, z, z, z, alow, dy, sall, sall, wgu, b_gate, convw, conv_g, gla_g, *dep_args)


def _adamw_math(g, w, m, v):
    m = ADAM_B1 * m + (1.0 - ADAM_B1) * g
    v = ADAM_B2 * v + (1.0 - ADAM_B2) * (g * g)
    m_hat = m / (1.0 - ADAM_B1 ** ADAM_STEP)
    v_hat = v / (1.0 - ADAM_B2 ** ADAM_STEP)
    delta = -ADAM_LR * (m_hat / (jnp.sqrt(v_hat) + ADAM_EPS) + ADAM_WD * w)
    return delta, m, v


def _adamw(name, parts, w, m, v, tr):
    r, c = w.shape
    n_parts = parts.shape[0]

    def body(p_ref, w_ref, m_ref, v_ref, g_ref, d_ref, nm_ref, nv_ref):
        g = p_ref[0].astype(F32)
        for j in range(1, n_parts):
            g = g + p_ref[j].astype(F32)
        g_ref[...] = g
        d_ref[...], nm_ref[...], nv_ref[...] = _adamw_math(g, w_ref[...], m_ref[...], v_ref[...])

    blk = pl.BlockSpec((tr, c), lambda i: (i, 0))
    return pl.pallas_call(
        body, name=name, grid=(r // tr,),
        in_specs=[pl.BlockSpec((n_parts, tr, c), lambda i: (0, i, 0)), blk, blk, blk],
        out_specs=[blk] * 4, out_shape=[jax.ShapeDtypeStruct((r, c), F32)] * 4,
        compiler_params=_params(1),
    )(parts, w, m, v)


def _adamw_small(grads, ws, ms, vs):
    n = len(grads)

    def body(*refs):
        g, w, m, v = (refs[k * n:(k + 1) * n] for k in range(4))
        d_out, m_out, v_out = (refs[(4 + k) * n:(5 + k) * n] for k in range(3))
        for i in range(n):
            d_out[i][...], m_out[i][...], v_out[i][...] = _adamw_math(g[i][...], w[i][...], m[i][...], v[i][...])

    vmem = pl.BlockSpec(memory_space=pltpu.VMEM)
    outs = pl.pallas_call(
        body, name="adamw_small", out_shape=[jax.ShapeDtypeStruct(w.shape, F32) for w in ws] * 3,
        in_specs=[vmem] * (4 * n), out_specs=[vmem] * (3 * n),
    )(*grads, *ws, *ms, *vs)
    return [outs[:n], outs[n:2 * n], outs[2 * n:]]


def _sum_partials(parts):
    n_parts, rows, lanes = parts.shape

    def body(p_ref, o_ref):
        g = p_ref[0]
        for j in range(1, n_parts):
            g = g + p_ref[j]
        o_ref[...] = g

    return pl.pallas_call(
        body, name="sum_small_partials", out_shape=jax.ShapeDtypeStruct((rows, lanes), F32),
        in_specs=[pl.BlockSpec(memory_space=pltpu.VMEM)], out_specs=pl.BlockSpec(memory_space=pltpu.VMEM),
    )(parts)


def _pack_rows(vectors, rows):
    flat = jnp.concatenate([a.reshape(-1).astype(F32) for a in vectors])
    return jnp.pad(flat, (0, rows * LANES - flat.shape[0])).reshape(rows, LANES)


def _unpack_rows(block, shapes):
    flat = block.reshape(-1)
    out, off = [], 0
    for s in shapes:
        n = 1
        for dim in s:
            n *= dim
        out.append(flat[off:off + n].reshape(s))
        off += n
    return out


def kernel(x, norm1_g, w_in, w_gate_up, b_gate, conv_w, conv_norm_g, gla_norm_g, w_out, norm2_g, w_ff1, w_ff2, norm_f_g, loss_target, m_norm1_g, m_w_in, m_w_gate_up, m_b_gate, m_conv_w, m_conv_norm_g, m_gla_norm_g, m_w_out, m_norm2_g, m_w_ff1, m_w_ff2, m_norm_f_g, v_norm1_g, v_w_in, v_w_gate_up, v_b_gate, v_conv_w, v_conv_norm_g, v_gla_norm_g, v_w_out, v_norm2_g, v_w_ff1, v_w_ff2, v_norm_f_g):
    me = _device_index()
    x2d, tgt = x[0], loss_target[0]
    t, d = x2d.shape
    d_in_shard = w_in.shape[2]
    d_in = N_DEV * d_in_shard
    n_main = d_in - GATE_RANK
    d_conv = conv_norm_g.shape[1]
    d_k = b_gate.shape[1]
    d_ff = N_DEV * w_ff1.shape[2]
    wmv = dict(
        norm1_g=(norm1_g, m_norm1_g, v_norm1_g), w_in=(w_in, m_w_in, v_w_in),
        w_gate_up=(w_gate_up, m_w_gate_up, v_w_gate_up), b_gate=(b_gate, m_b_gate, v_b_gate),
        conv_w=(conv_w, m_conv_w, v_conv_w), conv_norm_g=(conv_norm_g, m_conv_norm_g, v_conv_norm_g),
        gla_norm_g=(gla_norm_g, m_gla_norm_g, v_gla_norm_g), w_out=(w_out, m_w_out, v_w_out),
        norm2_g=(norm2_g, m_norm2_g, v_norm2_g), w_ff1=(w_ff1, m_w_ff1, v_w_ff1), w_ff2=(w_ff2, m_w_ff2, v_w_ff2),
        norm_f_g=(norm_f_g, m_norm_f_g, v_norm_f_g))

    small_rows = 16
    first_level = (SIBLING,) + SAME_CORE_PEERS
    win_shard = w_in[0].astype(BF16)
    in_send, in_recv, in_src, in_land, token = _exchange_start(
        "all_gather_start_w_in", [win_shard], [_land_zone(win_shard)], scatter=False, masks=[first_level])
    _, wgu_t, cw_t, wout_t, w1_t, w2_t = lax.optimization_barrier((token, w_gate_up, conv_w, w_out, w_ff1, w_ff2))
    small_shard = _pack_rows([wgu_t[0], cw_t[0]], small_rows)
    shards = [small_shard, wout_t[0].astype(BF16), w1_t[0].astype(BF16), w2_t[0].astype(BF16)]
    ag_send, ag_recv, ag_src, ag_land, token = _exchange_start(
        "all_gather_start", shards, [_land_zone(s) for s in shards], scatter=False, behind=token)

    def gathered(k, name, after):
        return _exchange_wait(name, ag_send[k], ag_recv[k], ag_src[k], ag_land[k], after, scatter=False)

    u = _rmsnorm(x2d, norm1_g, behind=token)
    tied = lax.optimization_barrier((token, w_in, m_w_in, v_w_in))
    wmv["w_in"] = tuple(tied[1:])
    small_g = gathered(0, "all_gather_wait_small", [u] + [a[0] for a in wmv["w_in"]])
    win_level1 = _exchange_wait(
        "all_gather_wait_w_in", in_send[0], in_recv[0], in_src[0], in_land[0], small_g, scatter=False,
        masks=first_level)
    win_g = _forward_wait("all_gather_wait_w_in_forwarded", *_forward_start("all_gather_forward_w_in", win_level1))
    w_main, w_alow = _shards_to_columns(win_g, n_main)
    small_flat = small_g.reshape(N_DEV, -1)
    n_wgu = GATE_RANK * (d_k // N_DEV)
    wgu_full = small_flat[:, :n_wgu].reshape(N_DEV, GATE_RANK, d_k // N_DEV).transpose(1, 0, 2).reshape(GATE_RANK, d_k)
    conv_w_full = small_flat[:, n_wgu:n_wgu + (d_conv // N_DEV) * CONV_WIDTH].reshape(d_conv, CONV_WIDTH)
    wgu_pad = jnp.pad(wgu_full, ((0, LANES - GATE_RANK), (0, 0))).astype(BF16)
    convw_taps = jnp.pad(conv_w_full.T, ((0, SUBLANES - CONV_WIDTH), (0, 0)))

    get_w_out = lambda after: gathered(1, "all_gather_wait_w_out", after).reshape(-1, d)
    get_w1 = lambda after: gathered(2, "all_gather_wait_w_ff1", after)
    get_w2 = lambda after: gathered(3, "all_gather_wait_w_ff2", after).reshape(d_ff, d)

    in_flight = {}

    def send_partials(name, parts):
        own = lax.dynamic_index_in_dim(parts, me, axis=0, keepdims=False)
        send, recv, src, land, token = _exchange_start("scatter_start_" + name, [parts], [_land_zone(own)], scatter=True)
        in_flight[name] = (send[0], recv[0], src[0], land[0])
        return token

    def on_grad(name, value):
        if name == "w_in":
            main, alow_part = value
            value = _columns_to_shards(main, alow_part, N_DEV, d_in_shard)
        elif name in ("w_out", "w_ff2"):
            value = value.reshape(N_DEV, -1, d)
        return send_partials(name, value)

    def received(name, after):
        send, recv, src, land = in_flight[name]
        return _exchange_wait("scatter_wait_" + name, send, recv, src, land, after, scatter=True)

    def side_for(name, after):
        return (received(name, after),) + tuple(a[0] for a in wmv[name])

    grads = _local_step(x2d, u, tgt, norm1_g, w_main, w_alow, wgu_pad, b_gate, convw_taps, conv_norm_g, gla_norm_g,
                        norm2_g, norm_f_g, get_w_out, get_w1, get_w2, on_grad, side_for)
    grad_x = grads["x"]

    small_shapes = [(1, d), (1, d_k), (1, d_conv), (1, gla_norm_g.shape[1]), (1, d), (d,),
                    (GATE_RANK, d_k), (d_conv, CONV_WIDTH), (1,)]
    small_grad_rows = 152
    small_part = _pack_rows(
        [grads["norm1_g"], grads["b_gate"], grads["conv_norm_g"], grads["gla_norm_g"], grads["norm2_g"],
         grads["norm_f_g"], grads["w_gate_up"][:GATE_RANK], grads["conv_w"][:CONV_WIDTH].T, grads["loss"][0, 0]],
        small_grad_rows)
    small_token = send_partials("small", jnp.broadcast_to(small_part[None], (N_DEV, small_grad_rows, LANES)))

    gin_r, gout_r = (received(nm, [grad_x, small_token]) for nm in ("w_in", "w_out"))
    get_small = lambda after: received("small", after)
    done = {"w_ff1": grads["adam_w_ff1"], "w_ff2": grads["adam_w_ff2"]}
    return _update(me, gin_r, gout_r, done, get_small, small_shapes, grad_x, wmv)


def _local_step(x2d, u, tgt, norm1_g, w_main, w_alow, wgu_pad, b_gate, convw_taps, conv_norm_g, gla_norm_g,
                norm2_g, norm_f_g, get_w_out, get_w1, get_w2, on_grad, side_for=lambda name, after: None):
    t, d = x2d.shape
    n_main = w_main.shape[1]

    z, alow = _inproj(u, w_main, w_alow)
    y, sall = _mixer_fwd(z, alow, wgu_pad, b_gate, convw_taps, conv_norm_g, gla_norm_g)
    w_out_full = get_w_out(y)
    x1, h = _outproj(y, w_out_full, x2d, norm2_g)
    w1g = get_w1(h)
    a = _ff1(h, w1g)
    w2_full = get_w2(a)
    d_ff = w2_full.shape[0]
    x2 = _ff2(a, w2_full, x1)
    dx2, dx2b, loss_part, d_normf = _loss_head(x2, norm_f_g.reshape(1, d), tgt)

    tk = min(4096, t)
    nk = t // tk
    da = _dff2(dx2b, w2_full, a)
    dw2 = _tn_matmul(
        "dw_ff2", a, dx2b, (d_ff // 1024, d // 1024, nk),
        pl.BlockSpec((tk, 1024), lambda m, j, kk: (kk, m)), pl.BlockSpec((tk, 1024), lambda m, j, kk: (kk, j)),
        jax.ShapeDtypeStruct((d_ff, d), BF16), pl.BlockSpec((1024, 1024), lambda m, j, kk: (m, j)), (1024, 1024),
        a_fn=_relu_sq)
    token = on_grad("w_ff2", dw2)
    f_shard = d_ff // N_DEV
    dw1 = _tn_matmul(
        "dw_ff1", h, da, (N_DEV, d // 1024, nk),
        pl.BlockSpec((tk, 1024), lambda g, m, kk: (kk, m)), pl.BlockSpec((tk, f_shard), lambda g, m, kk: (kk, g)),
        jax.ShapeDtypeStruct((N_DEV, d, f_shard), BF16), pl.BlockSpec((None, 1024, f_shard), lambda g, m, kk: (g, m, 0)),
        (1024, f_shard), behind=token)
    token = on_grad("w_ff1", dw1)
    dh = _dh(da, w1g, behind=token)
    dx1, dx1b, d_norm2, dy = _norm_bwd_dy(dh, x1, norm2_g, dx2, w_out_full)
    dwout = _tn_matmul(
        "dw_out", y, dx1b, (d // 1024, d // 1024, nk),
        pl.BlockSpec((tk, 1024), lambda m, j, kk: (kk, m)), pl.BlockSpec((tk, 1024), lambda m, j, kk: (kk, j)),
        jax.ShapeDtypeStruct((d, d), BF16), pl.BlockSpec((1024, 1024), lambda m, j, kk: (m, j)), (1024, 1024))
    token = on_grad("w_out", dwout)
    dz, dzal, d_convw, d_convg, d_glag, d_bgate, d_wgu = _mixer_bwd(
        z, alow, dy, sall, wgu_pad, b_gate, convw_taps, conv_norm_g, gla_norm_g, behind=token)
    token = on_grad("w_in", _dw_in(u, dz, dzal, tk))
    sides = [s for s in (side_for("w_ff2", token), side_for("w_ff1", token)) if s is not None]
    du, adam = _du(dz, w_main, dzal, w_alow, behind=token, side=sides)
    adam_ff2, adam_ff1 = adam if adam else (None, None)
    grad_x, d_norm1 = _norm_bwd("norm1_bwd", du, x2d, norm1_g, dx1, with_bf16=False)
    return dict(x=grad_x, loss=loss_part, adam_w_ff2=adam_ff2, adam_w_ff1=adam_ff1,
                norm1_g=d_norm1, w_gate_up=d_wgu, b_gate=d_bgate, conv_w=d_convw,
                conv_norm_g=d_convg, gla_norm_g=d_glag, norm2_g=d_norm2, norm_f_g=d_normf)


_WEIGHT_ORDER = ("norm1_g", "w_in", "w_gate_up", "b_gate", "conv_w", "conv_norm_g", "gla_norm_g", "w_out", "norm2_g",
                 "w_ff1", "w_ff2", "norm_f_g")
_SMALL_ORDER = ("norm1_g", "b_gate", "conv_norm_g", "gla_norm_g", "norm2_g", "norm_f_g", "w_gate_up", "conv_w")
def _update(me, gin_r, gout_r, done, get_small, small_shapes, grad_x, wmv):
    big = dict(done)
    big["w_in"] = _adamw("adamw_w_in", gin_r, *(a[0] for a in wmv["w_in"]), 256)
    big["w_out"] = _adamw("adamw_w_out", gout_r, *(a[0] for a in wmv["w_out"]), 128)

    wgu_cols = wmv["w_gate_up"][0].shape[2]
    cw_rows = wmv["conv_w"][0].shape[1]

    small_r = get_small([big[nm][3] for nm in ("w_in", "w_out")])
    summed = _unpack_rows(_sum_partials(small_r), small_shapes)
    summed[6] = lax.dynamic_slice_in_dim(summed[6], me * wgu_cols, wgu_cols, axis=1)
    summed[7] = lax.dynamic_slice_in_dim(summed[7], me * cw_rows, cw_rows, axis=0)
    as_2d = lambda a: a.reshape((1, -1) if a.ndim == 1 else a.shape[-2:])
    grads_2d = [as_2d(g) for g in summed[:len(_SMALL_ORDER)]]
    small = _adamw_small(grads_2d, *[[as_2d(wmv[nm][k]) for nm in _SMALL_ORDER] for k in range(3)])
    small = [grads_2d] + small

    outs = []
    for k in range(4):
        for nm in _WEIGHT_ORDER:
            if nm in big:
                outs.append(big[nm][k][None])
            else:
                outs.append(small[k][_SMALL_ORDER.index(nm)].reshape(wmv[nm][0].shape))
    loss = summed[8][0]
    return (loss, grad_x[None], *outs)
```

```python
import jax
import jax.numpy as jnp
from jax import lax
from jax.experimental import pallas as pl
from jax.experimental.pallas import tpu as pltpu

F32 = jnp.float32
BF16 = jnp.bfloat16

N_DEV = 8
CHUNK = 64
GLA_HEADS = 4
CONV_GROUPS = 8
CONV_WIDTH = 3
GATE_RANK = 16
GATE_NORMALIZER = 16.0
EPS = 1e-6
ADAM_LR = 0.001
ADAM_B1 = 0.9
ADAM_B2 = 0.999
ADAM_EPS = 1e-08
ADAM_WD = 0.01
ADAM_STEP = 10

LANES = 128
SUBLANES = 8
VMEM_LIMIT = 56 << 20

_NN = (((1,), (0,)), ((), ()))
_NT = (((1,), (1,)), ((), ()))
_TN = (((0,), (0,)), ((), ()))


def _dot(a, b, dims=_NN):
    return lax.dot_general(a, b, dims, preferred_element_type=F32)


def _params(n_grid):
    return pltpu.CompilerParams(dimension_semantics=("arbitrary",) * n_grid, vmem_limit_bytes=VMEM_LIMIT)


def _relu_sq(a):
    r = jnp.maximum(a, 0.0)
    return r * r


def _device_index():
    return 4 * lax.axis_index("x") + 2 * lax.axis_index("y") + lax.axis_index("c")


def _peer(mask):
    x, y, c = lax.axis_index("x"), lax.axis_index("y"), lax.axis_index("c")
    return (x ^ ((mask >> 2) & 1), y ^ ((mask >> 1) & 1), c ^ (mask & 1))


_HBM_SPEC = pl.BlockSpec(memory_space=pltpu.HBM)
_SEM_SPEC = pl.BlockSpec(memory_space=pltpu.SEMAPHORE)
_SIDE_EFFECT = pltpu.SideEffectType.DATAFLOW_SIDE_EFFECTING
N_PEERS = N_DEV - 1


def _exchange_copy(src_ref, land_ref, send_sems, recv_sems, mask, scatter, arriving):
    me = _device_index()
    src = src_ref.at[me ^ mask] if scatter else src_ref
    dst = land_ref.at[(me ^ mask) if arriving else me]
    return pltpu.make_async_remote_copy(
        src_ref=src, dst_ref=dst, send_sem=send_sems.at[mask - 1], recv_sem=recv_sems.at[mask - 1],
        device_id=_peer(mask), device_id_type=pl.DeviceIdType.MESH)


def _land_zone(own):
    zone = lax.empty((N_DEV,) + own.shape, own.dtype)
    return lax.dynamic_update_slice(zone, own[None], (_device_index(),) + (0,) * own.ndim)


ALL_PEERS = tuple(range(1, N_DEV))
SIBLING = 1
SAME_CORE_PEERS = (2, 4, 6)


def _exchange_start(name, srcs, lands, scatter, masks=None, behind=None):
    n = len(srcs)
    masks = masks or [ALL_PEERS] * n
    dep_args = [] if behind is None else [behind]

    def body(*refs):
        src, land = refs[:n], refs[n:2 * n]
        outs = refs[2 * n + len(dep_args):]
        send_sems, recv_sems = outs[:n], outs[n:2 * n]
        token = refs[-1]
        for a in range(n):
            for mask in masks[a]:
                _exchange_copy(src[a], land[a], send_sems[a], recv_sems[a], mask, scatter, False).start()
        token[...] = jnp.zeros_like(token)

    hbm = lambda a: pltpu.HBM(a.shape, a.dtype)
    outs = pl.pallas_call(
        body, name=name,
        out_shape=([pltpu.SemaphoreType.DMA((N_PEERS,))] * (2 * n) + [hbm(a) for a in srcs] + [hbm(a) for a in lands]
                   + [jax.ShapeDtypeStruct((SUBLANES, LANES), F32)]),
        in_specs=[_HBM_SPEC] * (2 * n) + [pl.BlockSpec(memory_space=pl.ANY)] * len(dep_args),
        out_specs=[_SEM_SPEC] * (2 * n) + [_HBM_SPEC] * (2 * n) + [pl.BlockSpec(memory_space=pltpu.VMEM)],
        input_output_aliases={a: 2 * n + a for a in range(2 * n)},
        compiler_params=pltpu.CompilerParams(has_side_effects=_SIDE_EFFECT),
    )(*[pltpu.with_memory_space_constraint(a, pltpu.HBM) for a in list(srcs) + list(lands)], *dep_args)
    send_sems, recv_sems = outs[:n], outs[n:2 * n]
    src_thru, land_thru = outs[2 * n:3 * n], outs[3 * n:4 * n]
    return send_sems, recv_sems, src_thru, land_thru, outs[-1]


def _exchange_wait(name, send_sems, recv_sems, src_thru, land_thru, after, scatter, masks=ALL_PEERS):
    after = list(after) if isinstance(after, (list, tuple)) else [after]

    def body(src_ref, land_ref, send_ref, recv_ref, *rest):
        for mask in masks:
            cp = _exchange_copy(src_ref, land_ref, send_ref, recv_ref, mask, scatter, True)
            cp.wait_send()
            cp.wait_recv()

    return pl.pallas_call(
        body, name=name,
        out_shape=(pltpu.HBM(src_thru.shape, src_thru.dtype), pltpu.HBM(land_thru.shape, land_thru.dtype)),
        in_specs=[_HBM_SPEC, _HBM_SPEC, _SEM_SPEC, _SEM_SPEC] + [pl.BlockSpec(memory_space=pl.ANY)] * len(after),
        out_specs=(_HBM_SPEC, _HBM_SPEC), input_output_aliases={0: 0, 1: 1},
        compiler_params=pltpu.CompilerParams(has_side_effects=_SIDE_EFFECT),
    )(src_thru, land_thru, send_sems, recv_sems, *after)[1]


def _forward_copy(land_ref, send_sems, recv_sems, k, arriving):
    me = _device_index()
    slot = me ^ SAME_CORE_PEERS[k]
    return pltpu.make_async_remote_copy(
        src_ref=land_ref.at[slot], dst_ref=land_ref.at[(slot ^ SIBLING) if arriving else slot],
        send_sem=send_sems.at[k], recv_sem=recv_sems.at[k],
        device_id=_peer(SIBLING), device_id_type=pl.DeviceIdType.MESH)


def _forward_start(name, land):
    n_fwd = len(SAME_CORE_PEERS)

    def body(land_ref, send_sems, recv_sems, land_thru):
        for k in range(n_fwd):
            _forward_copy(land_ref, send_sems, recv_sems, k, False).start()

    send, recv, thru = pl.pallas_call(
        body, name=name,
        out_shape=[pltpu.SemaphoreType.DMA((n_fwd,)), pltpu.SemaphoreType.DMA((n_fwd,)), pltpu.HBM(land.shape, land.dtype)],
        in_specs=[_HBM_SPEC], out_specs=[_SEM_SPEC, _SEM_SPEC, _HBM_SPEC], input_output_aliases={0: 2},
        compiler_params=pltpu.CompilerParams(has_side_effects=_SIDE_EFFECT),
    )(pltpu.with_memory_space_constraint(land, pltpu.HBM))
    return send, recv, thru


def _forward_wait(name, send_sems, recv_sems, land_thru):
    def body(land_ref, send_ref, recv_ref, got_ref):
        for k in range(len(SAME_CORE_PEERS)):
            cp = _forward_copy(land_ref, send_ref, recv_ref, k, True)
            cp.wait_send()
            cp.wait_recv()

    return pl.pallas_call(
        body, name=name, out_shape=pltpu.HBM(land_thru.shape, land_thru.dtype),
        in_specs=[_HBM_SPEC, _SEM_SPEC, _SEM_SPEC], out_specs=_HBM_SPEC, input_output_aliases={0: 0},
        compiler_params=pltpu.CompilerParams(has_side_effects=_SIDE_EFFECT),
    )(land_thru, send_sems, recv_sems)


def _shards_to_columns(g, n_main, tr=256):
    n_dev, d, s = g.shape

    def body(g_ref, main_ref, rest_ref):
        for j in range(n_dev):
            lo, hi = j * s, (j + 1) * s
            if hi <= n_main:
                main_ref[:, lo:hi] = g_ref[j]
            else:
                main_ref[:, lo:n_main] = g_ref[j, :, 0:n_main - lo]
                rest_ref[...] = jnp.zeros_like(rest_ref)
                rest_ref[:, 0:hi - n_main] = g_ref[j, :, n_main - lo:s]

    return pl.pallas_call(
        body, grid=(d // tr,), name="shards_to_columns",
        in_specs=[pl.BlockSpec((n_dev, tr, s), lambda i: (0, i, 0))],
        out_specs=[pl.BlockSpec((tr, n_main), lambda i: (i, 0)), pl.BlockSpec((tr, LANES), lambda i: (i, 0))],
        out_shape=[jax.ShapeDtypeStruct((d, n_main), g.dtype), jax.ShapeDtypeStruct((d, LANES), g.dtype)],
        compiler_params=_params(1),
    )(g)


def _columns_to_shards(main, rest, n_dev, s, tr=256):
    d, n_main = main.shape
    assert (n_dev - 1) * s <= n_main < n_dev * s

    def body(main_ref, rest_ref, o_ref):
        for j in range(n_dev):
            lo, hi = j * s, (j + 1) * s
            if hi <= n_main:
                o_ref[j] = main_ref[:, lo:hi]
            else:
                o_ref[j, :, 0:n_main - lo] = main_ref[:, lo:n_main]
                o_ref[j, :, n_main - lo:s] = rest_ref[:, 0:hi - n_main]

    return pl.pallas_call(
        body, grid=(d // tr,), name="columns_to_shards",
        in_specs=[pl.BlockSpec((tr, n_main), lambda i: (i, 0)), pl.BlockSpec((tr, LANES), lambda i: (i, 0))],
        out_specs=pl.BlockSpec((n_dev, tr, s), lambda i: (0, i, 0)),
        out_shape=jax.ShapeDtypeStruct((n_dev, d, s), main.dtype),
        compiler_params=_params(1),
    )(main, rest)


def _rmsnorm(x, g, tr=512, behind=None):
    t, d = x.shape
    tr = min(tr, t)
    dep_args, dep_specs = _behind(behind)

    def body(x_ref, g_ref, *rest):
        u_ref = rest[-1]
        xf = x_ref[...]
        r = lax.rsqrt(jnp.mean(xf * xf, axis=-1, keepdims=True) + EPS)
        u_ref[...] = (xf * r * g_ref[...]).astype(BF16)

    return pl.pallas_call(
        body, name="rmsnorm1", grid=(t // tr,),
        in_specs=[pl.BlockSpec((tr, d), lambda i: (i, 0)), pl.BlockSpec((1, d), lambda i: (0, 0))] + dep_specs,
        out_specs=pl.BlockSpec((tr, d), lambda i: (i, 0)),
        out_shape=jax.ShapeDtypeStruct((t, d), BF16),
        compiler_params=_params(1),
    )(x, g, *dep_args)


def _inproj(u, w_main, w_alow, tm=1024, tn=1024):
    t, d = u.shape
    tm = min(tm, t)
    n = w_main.shape[1]

    def body(u_ref, w_ref, wa_ref, z_ref, al_ref):
        @pl.when(pl.program_id(1) == 0)
        def _():
            al_ref[...] = _dot(u_ref[...], wa_ref[...])

        z_ref[...] = _dot(u_ref[...], w_ref[...])

    return pl.pallas_call(
        body, name="inproj", grid=(t // tm, n // tn),
        in_specs=[pl.BlockSpec((tm, d), lambda m, j: (m, 0)), pl.BlockSpec((d, tn), lambda m, j: (0, j)),
                  pl.BlockSpec((d, LANES), lambda m, j: (0, 0))],
        out_specs=[pl.BlockSpec((tm, tn), lambda m, j: (m, j)), pl.BlockSpec((tm, LANES), lambda m, j: (m, 0))],
        out_shape=[jax.ShapeDtypeStruct((t, n), F32), jax.ShapeDtypeStruct((t, LANES), F32)],
        compiler_params=_params(2),
    )(u, w_main, w_alow)


def _outproj(y, w_out, x, g2, tm=512):
    t, d = x.shape
    tm = min(tm, t)
    k = y.shape[1]

    def body(y_ref, w_ref, x_ref, g_ref, x1_ref, h_ref):
        x1 = x_ref[...] + _dot(y_ref[...], w_ref[...])
        x1_ref[...] = x1
        r = lax.rsqrt(jnp.mean(x1 * x1, axis=-1, keepdims=True) + EPS)
        h_ref[...] = (x1 * r * g_ref[...]).astype(BF16)

    return pl.pallas_call(
        body, name="outproj_rmsnorm", grid=(t // tm,),
        in_specs=[pl.BlockSpec((tm, k), lambda m: (m, 0)), pl.BlockSpec((k, d), lambda m: (0, 0)),
                  pl.BlockSpec((tm, d), lambda m: (m, 0)), pl.BlockSpec((1, d), lambda m: (0, 0))],
        out_specs=[pl.BlockSpec((tm, d), lambda m: (m, 0)), pl.BlockSpec((tm, d), lambda m: (m, 0))],
        out_shape=[jax.ShapeDtypeStruct((t, d), F32), jax.ShapeDtypeStruct((t, d), BF16)],
        compiler_params=_params(1),
    )(y, w_out, x, g2)


def _ff1(h, w1g, tm=1024):
    t, d = h.shape
    tm = min(tm, t)
    g, _, f = w1g.shape

    def body(h_ref, w_ref, a_ref):
        a_ref[...] = _dot(h_ref[...], w_ref[...]).astype(BF16)

    return pl.pallas_call(
        body, name="ff1", grid=(t // tm, g),
        in_specs=[pl.BlockSpec((tm, d), lambda m, j: (m, 0)), pl.BlockSpec((None, d, f), lambda m, j: (j, 0, 0))],
        out_specs=pl.BlockSpec((tm, f), lambda m, j: (m, j)),
        out_shape=jax.ShapeDtypeStruct((t, g * f), BF16),
        compiler_params=_params(2),
    )(h, w1g)


def _ff2(a, w2, x1, tm=1024, tn=1024, tk=2048):
    t, f = a.shape
    tm = min(tm, t)
    d = w2.shape[1]

    def body(a_ref, w_ref, x1_ref, o_ref):
        @pl.when(pl.program_id(2) == 0)
        def _():
            o_ref[...] = x1_ref[...]

        o_ref[...] += _dot(_relu_sq(a_ref[...]), w_ref[...])

    return pl.pallas_call(
        body, name="ff2_residual", grid=(t // tm, d // tn, f // tk),
        in_specs=[pl.BlockSpec((tm, tk), lambda m, j, kk: (m, kk)), pl.BlockSpec((tk, tn), lambda m, j, kk: (kk, j)),
                  pl.BlockSpec((tm, tn), lambda m, j, kk: (m, j))],
        out_specs=pl.BlockSpec((tm, tn), lambda m, j, kk: (m, j)),
        out_shape=jax.ShapeDtypeStruct((t, d), F32),
        compiler_params=_params(3),
    )(a, w2, x1)


def _dff2(dx2b, w2, a, tm=1024, tn=1024):
    t, d = dx2b.shape
    tm = min(tm, t)
    f = w2.shape[0]

    def body(g_ref, w_ref, a_ref, o_ref):
        dp = _dot(g_ref[...], w_ref[...], _NT)
        o_ref[...] = (dp * (2.0 * jnp.maximum(a_ref[...].astype(F32), 0.0))).astype(BF16)

    return pl.pallas_call(
        body, name="dff2", grid=(t // tm, f // tn),
        in_specs=[pl.BlockSpec((tm, d), lambda m, j: (m, 0)), pl.BlockSpec((tn, d), lambda m, j: (j, 0)),
                  pl.BlockSpec((tm, tn), lambda m, j: (m, j))],
        out_specs=pl.BlockSpec((tm, tn), lambda m, j: (m, j)),
        out_shape=jax.ShapeDtypeStruct((t, f), BF16),
        compiler_params=_params(2),
    )(dx2b, w2, a)


def _behind(token):
    if token is None:
        return [], []
    return [token], [pl.BlockSpec(token.shape, lambda *_: (0,) * token.ndim)]


def _tn_matmul(name, a, b, grid, a_spec, b_spec, out_shape, out_spec, acc_shape, a_fn=None, behind=None):
    nk = grid[-1]
    dep_args, dep_specs = _behind(behind)

    def body(a_ref, b_ref, *rest):
        o_ref, acc_ref = rest[-2:]
        kk = pl.program_id(len(grid) - 1)
        av = a_ref[...]
        if a_fn is not None:
            av = a_fn(av)
        part = _dot(av, b_ref[...], _TN)

        @pl.when(kk == 0)
        def _():
            acc_ref[...] = part

        @pl.when(kk > 0)
        def _():
            acc_ref[...] += part

        @pl.when(kk == nk - 1)
        def _():
            o_ref[...] = acc_ref[...].astype(o_ref.dtype)

    return pl.pallas_call(
        body, name=name, grid=grid, in_specs=[a_spec, b_spec] + dep_specs, out_specs=out_spec, out_shape=out_shape,
        scratch_shapes=[pltpu.VMEM(acc_shape, F32)], compiler_params=_params(len(grid)),
    )(a, b, *dep_args)


def _dw_in(u, dz, dzal, tk, tm=1024, tn=1024):
    t, d = u.shape
    n_main = dz.shape[1]
    nk = t // tk

    def body(a_ref, b_ref, al_ref, o_ref, oal_ref, acc_ref, accal_ref):
        j, kk = pl.program_id(1), pl.program_id(2)
        av = a_ref[...]

        def accumulate(acc, part, out):
            @pl.when(kk == 0)
            def _():
                acc[...] = part

            @pl.when(kk > 0)
            def _():
                acc[...] += part

            @pl.when(kk == nk - 1)
            def _():
                out[...] = acc[...].astype(out.dtype)

        accumulate(acc_ref, _dot(av, b_ref[...], _TN), o_ref)

        @pl.when(j == 0)
        def _():
            accumulate(accal_ref, _dot(av, al_ref[...], _TN), oal_ref)

    return pl.pallas_call(
        body, name="dw_in", grid=(d // tm, n_main // tn, nk),
        in_specs=[pl.BlockSpec((tk, tm), lambda m, j, kk: (kk, m)), pl.BlockSpec((tk, tn), lambda m, j, kk: (kk, j)),
                  pl.BlockSpec((tk, LANES), lambda m, j, kk: (kk, 0))],
        out_specs=[pl.BlockSpec((tm, tn), lambda m, j, kk: (m, j)), pl.BlockSpec((tm, LANES), lambda m, j, kk: (m, 0))],
        out_shape=[jax.ShapeDtypeStruct((d, n_main), BF16), jax.ShapeDtypeStruct((d, LANES), BF16)],
        scratch_shapes=[pltpu.VMEM((tm, tn), F32), pltpu.VMEM((tm, LANES), F32)],
        compiler_params=_params(3),
    )(u, dz, dzal)


class _SideAdamW:
    def __init__(self, side, grid):
        parts, w, m, v = side
        n_parts, r, c = parts.shape
        steps = 1
        for extent in grid:
            steps *= extent
        rows = r // steps
        assert rows * steps == r and rows % (2 * SUBLANES) == 0, (r, steps)

        def step(*ids):
            lin = ids[0]
            for extent, idx in zip(grid[1:], ids[1:]):
                lin = lin * extent + idx
            return lin

        slab = pl.BlockSpec((rows, c), lambda *ids: (step(*ids), 0))
        self.args = [parts, w, m, v]
        self.in_specs = [pl.BlockSpec((n_parts, rows, c), lambda *ids: (0, step(*ids), 0)), slab, slab, slab]
        self.out_specs = [slab] * 4
        self.out_shape = [jax.ShapeDtypeStruct((r, c), F32)] * 4
        self.n_parts = n_parts

    def run(self, in_refs, out_refs):
        p_ref, w_ref, m_ref, v_ref = in_refs
        g = p_ref[0].astype(F32)
        for j in range(1, self.n_parts):
            g = g + p_ref[j].astype(F32)
        out_refs[0][...] = g
        out_refs[1][...], out_refs[2][...], out_refs[3][...] = _adamw_math(g, w_ref[...], m_ref[...], v_ref[...])


def _dh(da, w1g, tm=1024, tn=1024, shards_per_step=4, behind=None):
    t = da.shape[0]
    tm = min(tm, t)
    g, d, f = w1g.shape
    sps = shards_per_step
    grid = (t // tm, d // tn, g // sps)
    dep_args, dep_specs = _behind(behind)

    def body(a_ref, w_ref, *rest):
        o_ref = rest[-1]
        acc = _dot(a_ref[:, 0:f], w_ref[0], _NT)
        for s in range(1, sps):
            acc = acc + _dot(a_ref[:, s * f:(s + 1) * f], w_ref[s], _NT)

        @pl.when(pl.program_id(2) == 0)
        def _():
            o_ref[...] = acc

        @pl.when(pl.program_id(2) > 0)
        def _():
            o_ref[...] += acc

    return pl.pallas_call(
        body, name="dh", grid=grid,
        in_specs=[pl.BlockSpec((tm, sps * f), lambda m, j, kk: (m, kk)),
                  pl.BlockSpec((sps, tn, f), lambda m, j, kk: (kk, j, 0))] + dep_specs,
        out_specs=pl.BlockSpec((tm, tn), lambda m, j, kk: (m, j)),
        out_shape=jax.ShapeDtypeStruct((t, d), F32),
        compiler_params=_params(3),
    )(da, w1g, *dep_args)


def _du(dz, w_main, dzal, w_alow, tm=1024, tn=1024, tk=3072, behind=None, side=None):
    t, n = dz.shape
    tm = min(tm, t)
    d = w_main.shape[0]
    grid = (t // tm, d // tn, n // tk)
    dep_args, dep_specs = _behind(behind)
    adams = [_SideAdamW(s, grid) for s in (side or [])]
    n_dep, n_side = len(dep_args), len(adams)

    def body(a_ref, w_ref, al_ref, wa_ref, *rest):
        o_ref = rest[n_dep + 4 * n_side]

        @pl.when(pl.program_id(2) == 0)
        def _():
            o_ref[...] = _dot(al_ref[...], wa_ref[...], _NT)

        o_ref[...] += _dot(a_ref[...], w_ref[...], _NT)
        for k, adam in enumerate(adams):
            first_out = n_dep + 4 * n_side + 1 + 4 * k
            adam.run(rest[n_dep + 4 * k:n_dep + 4 * k + 4], rest[first_out:first_out + 4])

    outs = pl.pallas_call(
        body, name="du", grid=grid,
        in_specs=[pl.BlockSpec((tm, tk), lambda m, j, kk: (m, kk)), pl.BlockSpec((tn, tk), lambda m, j, kk: (j, kk)),
                  pl.BlockSpec((tm, LANES), lambda m, j, kk: (m, 0)), pl.BlockSpec((tn, LANES), lambda m, j, kk: (j, 0))]
        + dep_specs + [s for adam in adams for s in adam.in_specs],
        out_specs=[pl.BlockSpec((tm, tn), lambda m, j, kk: (m, j))] + [s for adam in adams for s in adam.out_specs],
        out_shape=[jax.ShapeDtypeStruct((t, d), F32)] + [s for adam in adams for s in adam.out_shape],
        compiler_params=_params(3),
    )(dz, w_main, dzal, w_alow, *dep_args, *[a for adam in adams for a in adam.args])
    return outs[0], [outs[1 + 4 * k:5 + 4 * k] for k in range(n_side)]


def _loss_head(x2, gf, tgt, tr=256):
    t, d = x2.shape

    def body(x_ref, g_ref, t_ref, dx_ref, dxb_ref, loss_ref, dg_ref):
        @pl.when(pl.program_id(0) == 0)
        def _():
            loss_ref[...] = jnp.zeros_like(loss_ref)
            dg_ref[...] = jnp.zeros_like(dg_ref)

        xf = x_ref[...]
        g = g_ref[...]
        r = lax.rsqrt(jnp.mean(xf * xf, axis=-1, keepdims=True) + EPS)
        xh = xf * r
        e = xh * g - t_ref[...]
        loss_ref[...] += 0.5 * jnp.sum(jnp.mean(e * e, axis=-1, keepdims=True))
        dy = e * (1.0 / d)
        dg_ref[...] += jnp.sum(dy * xh, axis=0, keepdims=True)
        dyg = dy * g
        dx = r * (dyg - xh * jnp.mean(dyg * xh, axis=-1, keepdims=True))
        dx_ref[...] = dx
        dxb_ref[...] = dx.astype(BF16)

    return pl.pallas_call(
        body, name="loss_head", grid=(t // tr,),
        in_specs=[pl.BlockSpec((tr, d), lambda i: (i, 0)), pl.BlockSpec((1, d), lambda i: (0, 0)),
                  pl.BlockSpec((tr, d), lambda i: (i, 0))],
        out_specs=[pl.BlockSpec((tr, d), lambda i: (i, 0)), pl.BlockSpec((tr, d), lambda i: (i, 0)),
                   pl.BlockSpec((SUBLANES, LANES), lambda i: (0, 0)), pl.BlockSpec((1, d), lambda i: (0, 0))],
        out_shape=[jax.ShapeDtypeStruct((t, d), F32), jax.ShapeDtypeStruct((t, d), BF16),
                   jax.ShapeDtypeStruct((SUBLANES, LANES), F32), jax.ShapeDtypeStruct((1, d), F32)],
        compiler_params=_params(1),
    )(x2, gf, tgt)


def _norm_bwd_dy(dh, x1, g2, dx2, w_out, tm=256):
    t, d = x1.shape
    k = w_out.shape[0]
    tm = min(tm, t)

    def body(dh_ref, x_ref, g_ref, dr_ref, w_ref, dx_ref, dxb_ref, dg_ref, dy_ref):
        @pl.when(pl.program_id(0) == 0)
        def _():
            dg_ref[...] = jnp.zeros_like(dg_ref)

        xf = x_ref[...]
        dhv = dh_ref[...]
        r = lax.rsqrt(jnp.mean(xf * xf, axis=-1, keepdims=True) + EPS)
        xh = xf * r
        dg_ref[...] += jnp.sum(dhv * xh, axis=0, keepdims=True)
        dyg = dhv * g_ref[...]
        dx = dr_ref[...] + r * (dyg - xh * jnp.mean(dyg * xh, axis=-1, keepdims=True))
        dx_ref[...] = dx
        dxb = dx.astype(BF16)
        dxb_ref[...] = dxb
        dy_ref[...] = _dot(dxb, w_ref[...], _NT)

    rows = pl.BlockSpec((tm, d), lambda i: (i, 0))
    vec = pl.BlockSpec((1, d), lambda i: (0, 0))
    return pl.pallas_call(
        body, name="norm2_bwd_dy", grid=(t // tm,),
        in_specs=[rows, rows, vec, rows, pl.BlockSpec((k, d), lambda i: (0, 0))],
        out_specs=[rows, rows, vec, pl.BlockSpec((tm, k), lambda i: (i, 0))],
        out_shape=[jax.ShapeDtypeStruct((t, d), F32), jax.ShapeDtypeStruct((t, d), BF16),
                   jax.ShapeDtypeStruct((1, d), F32), jax.ShapeDtypeStruct((t, k), F32)],
        compiler_params=_params(1),
    )(dh, x1, g2, dx2, w_out)


def _norm_bwd(name, dh, xin, g, dres, tr=256):
    t, d = xin.shape

    def body(dh_ref, x_ref, g_ref, dr_ref, dx_ref, dg_ref):
        @pl.when(pl.program_id(0) == 0)
        def _():
            dg_ref[...] = jnp.zeros_like(dg_ref)

        xf = x_ref[...]
        dhv = dh_ref[...]
        r = lax.rsqrt(jnp.mean(xf * xf, axis=-1, keepdims=True) + EPS)
        xh = xf * r
        dg_ref[...] += jnp.sum(dhv * xh, axis=0, keepdims=True)
        dyg = dhv * g_ref[...]
        dx = dr_ref[...] + r * (dyg - xh * jnp.mean(dyg * xh, axis=-1, keepdims=True))
        dx_ref[...] = dx

    rows = pl.BlockSpec((tr, d), lambda i: (i, 0))
    vec = pl.BlockSpec((1, d), lambda i: (0, 0))
    return pl.pallas_call(
        body, name=name, grid=(t // tr,),
        in_specs=[rows, rows, vec, rows], out_specs=[rows, vec],
        out_shape=[jax.ShapeDtypeStruct((t, d), F32), jax.ShapeDtypeStruct((1, d), F32)],
        compiler_params=_params(1),
    )(dh, xin, g, dres)


MIX_TILE = 256
CHUNKS_PER_TILE = MIX_TILE // CHUNK
CHUNK_SHIFT = CHUNK.bit_length() - 1
assert 1 << CHUNK_SHIFT == CHUNK


def _chunk_masks(n):
    row = lax.broadcasted_iota(jnp.int32, (n, n), 0)
    col = lax.broadcasted_iota(jnp.int32, (n, n), 1)
    same = lax.shift_right_logical(row, CHUNK_SHIFT) == lax.shift_right_logical(col, CHUNK_SHIFT)
    one = lambda m: jnp.where(m, 1.0, 0.0).astype(BF16)
    return jnp.concatenate([one(same & (col > row)), one(same)], axis=0), one(same & (col < row))


def _mask_dot(mask, x):
    hi = x.astype(BF16)
    r1 = x - hi.astype(F32)
    mid = r1.astype(BF16)
    lo = (r1 - mid.astype(F32)).astype(BF16)
    return _dot(mask, hi) + _dot(mask, mid) + _dot(mask, lo)


def _log_sigmoid(x):
    return jnp.minimum(x, 0.0) - jnp.log1p(jnp.exp(-jnp.abs(x)))


def _conv_taps(prev8, uc, w):
    ext = jnp.concatenate([prev8, uc], axis=0)
    s1 = pltpu.roll(ext, 1, 0)[SUBLANES:]
    s2 = pltpu.roll(ext, 2, 0)[SUBLANES:]
    return s2 * w[0:1] + s1 * w[1:2] + uc * w[2:3], s1, s2


def _z_specs(tile, idx):
    d_conv = 1024
    wide = lambda c: pl.BlockSpec((tile, d_conv), lambda i, c=c: (idx(i), c))
    half = lambda c: pl.BlockSpec((tile, d_conv // 2), lambda i, c=c: (idx(i), c))
    return [wide(0), wide(1), wide(2), half(6), half(7), wide(4), wide(5)]


def _mixer_fwd(z, alow, wgu, b_gate, convw, conv_g, gla_g):
    t = z.shape[0]
    tb, cpt = MIX_TILE, CHUNKS_PER_TILE
    d_conv = conv_g.shape[1]
    dv = gla_g.shape[1]
    dk = dv // 2
    gw = d_conv // CONV_GROUPS
    scale = dk ** -0.5

    def body(cb_ref, cc_ref, ch_ref, q_ref, k_ref, v_ref, og_ref, al_ref, wgu_ref, bg_ref, cw_ref, cg_ref, gg_ref,
             y_ref, sall_ref, carry_ref, s_ref):
        @pl.when(pl.program_id(0) == 0)
        def _():
            carry_ref[...] = jnp.zeros_like(carry_ref)
            s_ref[...] = jnp.zeros_like(s_ref)

        uc = cc_ref[...] * ch_ref[...]
        conv, _, _ = _conv_taps(carry_ref[...], uc, cw_ref[...])
        carry_ref[...] = uc[tb - SUBLANES:]
        ypre = cb_ref[...] * conv
        cg = cg_ref[...]
        for g in range(CONV_GROUPS):
            sl = slice(g * gw, (g + 1) * gw)
            seg = ypre[:, sl]
            r = lax.rsqrt(jnp.mean(seg * seg, axis=-1, keepdims=True) + EPS)
            y_ref[:, sl] = (seg * r * cg[:, sl]).astype(BF16)

        later_and_same, _ = _chunk_masks(tb)
        pre = _dot(al_ref[...].astype(BF16), wgu_ref[...]) + bg_ref[...]
        la = _log_sigmoid(pre) * (1.0 / GATE_NORMALIZER)
        sums = _mask_dot(later_and_same, la)
        e_dec = sums[:tb]
        dec_all = jnp.exp(sums[tb:])
        kdec = (k_ref[...] * jnp.exp(e_dec)).astype(BF16)
        qs = (q_ref[...] * scale).astype(BF16)
        vb = v_ref[...].astype(BF16)
        gg = gg_ref[...]
        rows = [slice(c * CHUNK, (c + 1) * CHUNK) for c in range(cpt)]
        ks = [slice(h * dk, (h + 1) * dk) for h in range(GLA_HEADS)]
        vs = [slice(h * dv, (h + 1) * dv) for h in range(GLA_HEADS)]
        kvt = [[_dot(vb[rows[c], vs[h]], kdec[rows[c], ks[h]], _TN) for h in range(GLA_HEADS)] for c in range(cpt)]
        state = [s_ref[h] for h in range(GLA_HEADS)]
        states = []
        for c in range(cpt):
            state = [state[h] * dec_all[c * CHUNK:c * CHUNK + 1, ks[h]] + kvt[c][h] for h in range(GLA_HEADS)]
            states.append(state)
            for h in range(GLA_HEADS):
                sall_ref[c, h] = state[h]
        for h in range(GLA_HEADS):
            s_ref[h] = state[h]
        for h in range(GLA_HEADS):
            o = jnp.concatenate(
                [_dot(qs[rows[c], ks[h]], states[c][h].astype(BF16), _NT) for c in range(cpt)], axis=0)
            ro = lax.rsqrt(jnp.mean(o * o, axis=-1, keepdims=True) + EPS)
            ogs = og_ref[:, vs[h]]
            yg = o * ro * gg * (ogs * jax.nn.sigmoid(ogs))
            y_ref[:, d_conv + h * dv:d_conv + (h + 1) * dv] = yg.astype(BF16)

    full = lambda shape: pl.BlockSpec(shape, lambda i: (0,) * len(shape))
    return pl.pallas_call(
        body, name="mixer_fwd", grid=(t // tb,),
        in_specs=_z_specs(tb, lambda i: i) + [
            pl.BlockSpec((tb, LANES), lambda i: (i, 0)), full(wgu.shape), full(b_gate.shape), full(convw.shape),
            full(conv_g.shape), full(gla_g.shape)],
        out_specs=[pl.BlockSpec((tb, d_conv + GLA_HEADS * dv), lambda i: (i, 0)),
                   pl.BlockSpec((cpt, GLA_HEADS, dv, dk), lambda i: (i, 0, 0, 0))],
        out_shape=[jax.ShapeDtypeStruct((t, d_conv + GLA_HEADS * dv), BF16),
                   jax.ShapeDtypeStruct((t // CHUNK, GLA_HEADS, dv, dk), F32)],
        scratch_shapes=[pltpu.VMEM((SUBLANES, d_conv), F32), pltpu.VMEM((GLA_HEADS, dv, dk), F32)],
        compiler_params=_params(1),
    )(z, z, z, z, z, z, z, alow, wgu, b_gate, convw, conv_g, gla_g)


def _mixer_bwd(z, alow, dy, sall, wgu, b_gate, convw, conv_g, gla_g, behind=None):
    t = z.shape[0]
    tb, cpt = MIX_TILE, CHUNKS_PER_TILE
    nt = t // tb
    d_conv = conv_g.shape[1]
    dv = gla_g.shape[1]
    dk = dv // 2
    d_k = GLA_HEADS * dk
    gw = d_conv // CONV_GROUPS
    scale = dk ** -0.5
    rev = lambda i: nt - 1 - i
    dep_args, dep_specs = _behind(behind)

    def body(cb_ref, cc_ref, ch_ref, q_ref, k_ref, v_ref, og_ref, ccp_ref, chp_ref, al_ref, dy_ref, sall_ref, sprev_ref,
             wgu_ref, bg_ref, cw_ref, cg_ref, gg_ref, *rest):
        dz_ref, dzal_ref, dcw_ref, dcg_ref, dgg_ref, dbg_ref, dwgu_ref, dcarry_ref, gd_ref = rest[-9:]
        i = pl.program_id(0)

        @pl.when(i == 0)
        def _():
            dcarry_ref[...] = jnp.zeros_like(dcarry_ref)
            gd_ref[...] = jnp.zeros_like(gd_ref)
            dcw_ref[...] = jnp.zeros_like(dcw_ref)
            dcg_ref[...] = jnp.zeros_like(dcg_ref)
            dgg_ref[...] = jnp.zeros_like(dgg_ref)
            dbg_ref[...] = jnp.zeros_like(dbg_ref)
            dwgu_ref[...] = jnp.zeros_like(dwgu_ref)

        first = rev(i) == 0

        cb, cc, ch = cb_ref[...], cc_ref[...], ch_ref[...]
        w = cw_ref[...]
        uc = cc * ch
        prev8 = jnp.where(first, 0.0, ccp_ref[...] * chp_ref[...])
        conv, s1, s2 = _conv_taps(prev8, uc, w)
        ypre = cb * conv
        cg = cg_ref[...]
        dypre_parts = []
        for g in range(CONV_GROUPS):
            sl = slice(g * gw, (g + 1) * gw)
            seg = ypre[:, sl]
            r = lax.rsqrt(jnp.mean(seg * seg, axis=-1, keepdims=True) + EPS)
            yn = seg * r
            dyc = dy_ref[:, sl]
            dcg_ref[:, sl] += jnp.sum(dyc * yn, axis=0, keepdims=True)
            dyn = dyc * cg[:, sl]
            dypre_parts.append(r * (dyn - yn * jnp.mean(dyn * yn, axis=-1, keepdims=True)))
        dypre = jnp.concatenate(dypre_parts, axis=1)
        dconv = dypre * cb
        dz_ref[:, 0:d_conv] = (dypre * conv).astype(BF16)
        dcw_ref[0:1] += jnp.sum(dconv * s2, axis=0, keepdims=True)
        dcw_ref[1:2] += jnp.sum(dconv * s1, axis=0, keepdims=True)
        dcw_ref[2:3] += jnp.sum(dconv * uc, axis=0, keepdims=True)
        ext = jnp.concatenate([dconv, dcarry_ref[...]], axis=0)
        f1 = pltpu.roll(ext, tb + SUBLANES - 1, 0)[:tb]
        f2 = pltpu.roll(ext, tb + SUBLANES - 2, 0)[:tb]
        dcarry_ref[...] = dconv[:SUBLANES]
        duc = dconv * w[2:3] + f1 * w[1:2] + f2 * w[0:1]
        dz_ref[:, d_conv:2 * d_conv] = (duc * ch).astype(BF16)
        dz_ref[:, 2 * d_conv:3 * d_conv] = (duc * cc).astype(BF16)

        q_off = 3 * d_conv
        k_off = q_off + d_k
        v_off = k_off + d_k
        og_off = v_off + GLA_HEADS * dv
        later_and_same, earlier = _chunk_masks(tb)
        alb = al_ref[...].astype(BF16)
        pre = _dot(alb, wgu_ref[...]) + bg_ref[...]
        la = _log_sigmoid(pre) * (1.0 / GATE_NORMALIZER)
        decays = jnp.exp(_mask_dot(later_and_same, la))
        exp_e, dec_all = decays[:tb], decays[tb:]
        kdec = k_ref[...] * exp_e
        kdec_b = kdec.astype(BF16)
        qs = (q_ref[...] * scale).astype(BF16)
        vb = v_ref[...].astype(BF16)
        gg = gg_ref[...]
        rows = [slice(c * CHUNK, (c + 1) * CHUNK) for c in range(cpt)]
        ks = [slice(h * dk, (h + 1) * dk) for h in range(GLA_HEADS)]
        vs = [slice(h * dv, (h + 1) * dv) for h in range(GLA_HEADS)]
        st_b = [[sall_ref[c, h].astype(BF16) for h in range(GLA_HEADS)] for c in range(cpt)]
        do_b = []
        dgg = jnp.zeros_like(gg)
        for h in range(GLA_HEADS):
            o = jnp.concatenate([_dot(qs[rows[c], ks[h]], st_b[c][h], _NT) for c in range(cpt)], axis=0)
            ro = lax.rsqrt(jnp.mean(o * o, axis=-1, keepdims=True) + EPS)
            on = o * ro
            ogs = og_ref[:, vs[h]]
            sg = jax.nn.sigmoid(ogs)
            gate = ogs * sg
            dyg = dy_ref[:, d_conv + h * dv:d_conv + (h + 1) * dv]
            dgg = dgg + jnp.sum(dyg * on * gate, axis=0, keepdims=True)
            dz_ref[:, og_off + h * dv:og_off + (h + 1) * dv] = (
                dyg * on * gg * (sg * (1.0 + ogs * (1.0 - sg)))).astype(BF16)
            don = dyg * gg * gate
            do_b.append((ro * (don - on * jnp.mean(don * on, axis=-1, keepdims=True))).astype(BF16))
        dgg_ref[...] += dgg
        for h in range(GLA_HEADS):
            dq = jnp.concatenate([_dot(do_b[h][rows[c]], st_b[c][h]) for c in range(cpt)], axis=0)
            dz_ref[:, q_off + h * dk:q_off + (h + 1) * dk] = (dq * scale).astype(BF16)
        own = [[_dot(do_b[h][rows[c]], qs[rows[c], ks[h]], _TN) for h in range(GLA_HEADS)] for c in range(cpt)]
        carried = [gd_ref[h] for h in range(GLA_HEADS)]
        gt_b = [None] * cpt
        ddd = [None] * cpt
        for c in reversed(range(cpt)):
            gt = [own[c][h] + carried[h] for h in range(GLA_HEADS)]
            dec = [dec_all[c * CHUNK:c * CHUNK + 1, ks[h]] for h in range(GLA_HEADS)]
            carried = [gt[h] * dec[h] for h in range(GLA_HEADS)]
            if c > 0:
                st_prev = [sall_ref[c - 1, h] for h in range(GLA_HEADS)]
            else:
                st_prev = [jnp.where(first, 0.0, sprev_ref[0, h]) for h in range(GLA_HEADS)]
            ddec = [jnp.sum(gt[h] * st_prev[h], axis=0, keepdims=True) * dec[h] for h in range(GLA_HEADS)]
            ddd[c] = jnp.broadcast_to(jnp.concatenate(ddec, axis=1), (CHUNK, d_k))
            gt_b[c] = [gt[h].astype(BF16) for h in range(GLA_HEADS)]
        for h in range(GLA_HEADS):
            gd_ref[h] = carried[h]
        dkdec_cols = []
        for h in range(GLA_HEADS):
            dvh = jnp.concatenate([_dot(kdec_b[rows[c], ks[h]], gt_b[c][h], _NT) for c in range(cpt)], axis=0)
            dz_ref[:, v_off + h * dv:v_off + (h + 1) * dv] = dvh.astype(BF16)
            dkdec_cols.append(jnp.concatenate([_dot(vb[rows[c], vs[h]], gt_b[c][h]) for c in range(cpt)], axis=0))
        dkdec = jnp.concatenate(dkdec_cols, axis=1)
        dz_ref[:, k_off:k_off + d_k] = (dkdec * exp_e).astype(BF16)
        dla = _mask_dot(earlier, dkdec * kdec) + jnp.concatenate(ddd, axis=0)
        dpre = dla * (1.0 / GATE_NORMALIZER) * jax.nn.sigmoid(-pre)
        dbg_ref[...] += jnp.sum(dpre, axis=0, keepdims=True)
        dpre_b = dpre.astype(BF16)
        dwgu_ref[...] += _dot(alb, dpre_b, _TN)
        dzal_ref[...] = _dot(dpre_b, wgu_ref[...], _NT).astype(BF16)

    full = lambda shape: pl.BlockSpec(shape, lambda i: (0,) * len(shape))
    prev_rows = lambda c: pl.BlockSpec(
        (SUBLANES, d_conv), lambda i, c=c: (jnp.maximum(rev(i) * (tb // SUBLANES) - 1, 0), c))
    n_z = 3 * d_conv + 2 * d_k + 2 * GLA_HEADS * dv
    return pl.pallas_call(
        body, name="mixer_bwd", grid=(nt,),
        in_specs=_z_specs(tb, rev) + [
            prev_rows(1), prev_rows(2),
            pl.BlockSpec((tb, LANES), lambda i: (rev(i), 0)),
            pl.BlockSpec((tb, d_conv + GLA_HEADS * dv), lambda i: (rev(i), 0)),
            pl.BlockSpec((cpt, GLA_HEADS, dv, dk), lambda i: (rev(i), 0, 0, 0)),
            pl.BlockSpec((1, GLA_HEADS, dv, dk), lambda i: (jnp.maximum(rev(i) * cpt - 1, 0), 0, 0, 0)),
            full(wgu.shape), full(b_gate.shape), full(convw.shape), full(conv_g.shape), full(gla_g.shape)]
        + dep_specs,
        out_specs=[pl.BlockSpec((tb, n_z), lambda i: (rev(i), 0)), pl.BlockSpec((tb, LANES), lambda i: (rev(i), 0)),
                   full(convw.shape), full(conv_g.shape), full(gla_g.shape), full(b_gate.shape), full(wgu.shape)],
        out_shape=[jax.ShapeDtypeStruct((t, n_z), BF16), jax.ShapeDtypeStruct((t, LANES), BF16),
                   jax.ShapeDtypeStruct(convw.shape, F32), jax.ShapeDtypeStruct(conv_g.shape, F32),
                   jax.ShapeDtypeStruct(gla_g.shape, F32), jax.ShapeDtypeStruct(b_gate.shape, F32),
                   jax.ShapeDtypeStruct(wgu.shape, F32)],
        scratch_shapes=[pltpu.VMEM((SUBLANES, d_conv), F32), pltpu.VMEM((GLA_HEADS, dv, dk), F32)],
        compiler_params=_params(1),
    )(z, z, z, z, z, z, z, z, z, alow, dy, sall, sall, wgu, b_gate, convw, conv_g, gla_g, *dep_args)


def _adamw_math(g, w, m, v):
    m = ADAM_B1 * m + (1.0 - ADAM_B1) * g
    v = ADAM_B2 * v + (1.0 - ADAM_B2) * (g * g)
    m_hat = m / (1.0 - ADAM_B1 ** ADAM_STEP)
    v_hat = v / (1.0 - ADAM_B2 ** ADAM_STEP)
    delta = -ADAM_LR * (m_hat / (jnp.sqrt(v_hat) + ADAM_EPS) + ADAM_WD * w)
    return delta, m, v


def _adamw(name, parts, w, m, v, tr):
    r, c = w.shape
    n_parts = parts.shape[0]

    def body(p_ref, w_ref, m_ref, v_ref, g_ref, d_ref, nm_ref, nv_ref):
        g = p_ref[0].astype(F32)
        for j in range(1, n_parts):
            g = g + p_ref[j].astype(F32)
        g_ref[...] = g
        d_ref[...], nm_ref[...], nv_ref[...] = _adamw_math(g, w_ref[...], m_ref[...], v_ref[...])

    blk = pl.BlockSpec((tr, c), lambda i: (i, 0))
    return pl.pallas_call(
        body, name=name, grid=(r // tr,),
        in_specs=[pl.BlockSpec((n_parts, tr, c), lambda i: (0, i, 0)), blk, blk, blk],
        out_specs=[blk] * 4, out_shape=[jax.ShapeDtypeStruct((r, c), F32)] * 4,
        compiler_params=_params(1),
    )(parts, w, m, v)


def _adamw_small(grads, ws, ms, vs):
    n = len(grads)

    def body(*refs):
        g, w, m, v = (refs[k * n:(k + 1) * n] for k in range(4))
        d_out, m_out, v_out = (refs[(4 + k) * n:(5 + k) * n] for k in range(3))
        for i in range(n):
            d_out[i][...], m_out[i][...], v_out[i][...] = _adamw_math(g[i][...], w[i][...], m[i][...], v[i][...])

    vmem = pl.BlockSpec(memory_space=pltpu.VMEM)
    outs = pl.pallas_call(
        body, name="adamw_small", out_shape=[jax.ShapeDtypeStruct(w.shape, F32) for w in ws] * 3,
        in_specs=[vmem] * (4 * n), out_specs=[vmem] * (3 * n),
    )(*grads, *ws, *ms, *vs)
    return [outs[:n], outs[n:2 * n], outs[2 * n:]]


def _sum_partials(parts):
    n_parts, rows, lanes = parts.shape

    def body(p_ref, o_ref):
        g = p_ref[0]
        for j in range(1, n_parts):
            g = g + p_ref[j]
        o_ref[...] = g

    return pl.pallas_call(
        body, name="sum_small_partials", out_shape=jax.ShapeDtypeStruct((rows, lanes), F32),
        in_specs=[pl.BlockSpec(memory_space=pltpu.VMEM)], out_specs=pl.BlockSpec(memory_space=pltpu.VMEM),
    )(parts)


def _pack_rows(vectors, rows):
    flat = jnp.concatenate([a.reshape(-1).astype(F32) for a in vectors])
    return jnp.pad(flat, (0, rows * LANES - flat.shape[0])).reshape(rows, LANES)


def _unpack_rows(block, shapes):
    flat = block.reshape(-1)
    out, off = [], 0
    for s in shapes:
        n = 1
        for dim in s:
            n *= dim
        out.append(flat[off:off + n].reshape(s))
        off += n
    return out


def kernel(x, norm1_g, w_in, w_gate_up, b_gate, conv_w, conv_norm_g, gla_norm_g, w_out, norm2_g, w_ff1, w_ff2, norm_f_g, loss_target, m_norm1_g, m_w_in, m_w_gate_up, m_b_gate, m_conv_w, m_conv_norm_g, m_gla_norm_g, m_w_out, m_norm2_g, m_w_ff1, m_w_ff2, m_norm_f_g, v_norm1_g, v_w_in, v_w_gate_up, v_b_gate, v_conv_w, v_conv_norm_g, v_gla_norm_g, v_w_out, v_norm2_g, v_w_ff1, v_w_ff2, v_norm_f_g):
    me = _device_index()
    x2d, tgt = x[0], loss_target[0]
    d = x2d.shape[1]
    d_in_shard = w_in.shape[2]
    n_main = N_DEV * d_in_shard - GATE_RANK
    d_conv = conv_norm_g.shape[1]
    d_k = b_gate.shape[1]
    d_ff = N_DEV * w_ff1.shape[2]
    wmv = dict(
        norm1_g=(norm1_g, m_norm1_g, v_norm1_g), w_in=(w_in, m_w_in, v_w_in),
        w_gate_up=(w_gate_up, m_w_gate_up, v_w_gate_up), b_gate=(b_gate, m_b_gate, v_b_gate),
        conv_w=(conv_w, m_conv_w, v_conv_w), conv_norm_g=(conv_norm_g, m_conv_norm_g, v_conv_norm_g),
        gla_norm_g=(gla_norm_g, m_gla_norm_g, v_gla_norm_g), w_out=(w_out, m_w_out, v_w_out),
        norm2_g=(norm2_g, m_norm2_g, v_norm2_g), w_ff1=(w_ff1, m_w_ff1, v_w_ff1), w_ff2=(w_ff2, m_w_ff2, v_w_ff2),
        norm_f_g=(norm_f_g, m_norm_f_g, v_norm_f_g))

    small_rows = 16
    first_level = (SIBLING,) + SAME_CORE_PEERS
    win_shard = w_in[0].astype(BF16)
    in_send, in_recv, in_src, in_land, token = _exchange_start(
        "all_gather_start_w_in", [win_shard], [_land_zone(win_shard)], scatter=False, masks=[first_level])
    _, wgu_t, cw_t, wout_t, w1_t, w2_t = lax.optimization_barrier((token, w_gate_up, conv_w, w_out, w_ff1, w_ff2))
    small_shard = _pack_rows([wgu_t[0], cw_t[0]], small_rows)
    shards = [small_shard, wout_t[0].astype(BF16), w1_t[0].astype(BF16), w2_t[0].astype(BF16)]
    ag_send, ag_recv, ag_src, ag_land, token = _exchange_start(
        "all_gather_start", shards, [_land_zone(s) for s in shards], scatter=False, behind=token)

    def gathered(k, name, after):
        return _exchange_wait(name, ag_send[k], ag_recv[k], ag_src[k], ag_land[k], after, scatter=False)

    u = _rmsnorm(x2d, norm1_g, behind=token)
    tied = lax.optimization_barrier((token, w_in, m_w_in, v_w_in))
    wmv["w_in"] = tuple(tied[1:])
    small_g = gathered(0, "all_gather_wait_small", [u] + [a[0] for a in wmv["w_in"]])
    win_level1 = _exchange_wait(
        "all_gather_wait_w_in", in_send[0], in_recv[0], in_src[0], in_land[0], small_g, scatter=False,
        masks=first_level)
    win_g = _forward_wait("all_gather_wait_w_in_forwarded", *_forward_start("all_gather_forward_w_in", win_level1))
    w_main, w_alow = _shards_to_columns(win_g, n_main)
    small_flat = small_g.reshape(N_DEV, -1)
    n_wgu = GATE_RANK * (d_k // N_DEV)
    wgu_full = small_flat[:, :n_wgu].reshape(N_DEV, GATE_RANK, d_k // N_DEV).transpose(1, 0, 2).reshape(GATE_RANK, d_k)
    conv_w_full = small_flat[:, n_wgu:n_wgu + (d_conv // N_DEV) * CONV_WIDTH].reshape(d_conv, CONV_WIDTH)
    wgu_pad = jnp.pad(wgu_full, ((0, LANES - GATE_RANK), (0, 0))).astype(BF16)
    convw_taps = jnp.pad(conv_w_full.T, ((0, SUBLANES - CONV_WIDTH), (0, 0)))

    get_w_out = lambda after: gathered(1, "all_gather_wait_w_out", after).reshape(-1, d)
    get_w1 = lambda after: gathered(2, "all_gather_wait_w_ff1", after)
    get_w2 = lambda after: gathered(3, "all_gather_wait_w_ff2", after).reshape(d_ff, d)

    in_flight = {}

    def send_partials(name, parts):
        own = lax.dynamic_index_in_dim(parts, me, axis=0, keepdims=False)
        send, recv, src, land, token = _exchange_start("scatter_start_" + name, [parts], [_land_zone(own)], scatter=True)
        in_flight[name] = (send[0], recv[0], src[0], land[0])
        return token

    def on_grad(name, value):
        if name == "w_in":
            main, alow_part = value
            value = _columns_to_shards(main, alow_part, N_DEV, d_in_shard)
        elif name in ("w_out", "w_ff2"):
            value = value.reshape(N_DEV, -1, d)
        return send_partials(name, value)

    def received(name, after):
        send, recv, src, land = in_flight[name]
        return _exchange_wait("scatter_wait_" + name, send, recv, src, land, after, scatter=True)

    def side_for(name, after):
        return (received(name, after),) + tuple(a[0] for a in wmv[name])

    grads = _local_step(x2d, u, tgt, norm1_g, w_main, w_alow, wgu_pad, b_gate, convw_taps, conv_norm_g, gla_norm_g,
                        norm2_g, norm_f_g, get_w_out, get_w1, get_w2, on_grad, side_for)
    grad_x = grads["x"]

    small_shapes = [(1, d), (1, d_k), (1, d_conv), (1, gla_norm_g.shape[1]), (1, d), (d,),
                    (GATE_RANK, d_k), (d_conv, CONV_WIDTH), (1,)]
    small_grad_rows = 152
    small_part = _pack_rows(
        [grads["norm1_g"], grads["b_gate"], grads["conv_norm_g"], grads["gla_norm_g"], grads["norm2_g"],
         grads["norm_f_g"], grads["w_gate_up"][:GATE_RANK], grads["conv_w"][:CONV_WIDTH].T, grads["loss"][0, 0]],
        small_grad_rows)
    small_token = send_partials("small", jnp.broadcast_to(small_part[None], (N_DEV, small_grad_rows, LANES)))

    gin_r, gout_r = (received(nm, [grad_x, small_token]) for nm in ("w_in", "w_out"))
    get_small = lambda after: received("small", after)
    done = {"w_ff1": grads["adam_w_ff1"], "w_ff2": grads["adam_w_ff2"]}
    return _update(me, gin_r, gout_r, done, get_small, small_shapes, grad_x, wmv)


def _local_step(x2d, u, tgt, norm1_g, w_main, w_alow, wgu_pad, b_gate, convw_taps, conv_norm_g, gla_norm_g,
                norm2_g, norm_f_g, get_w_out, get_w1, get_w2, on_grad, side_for=lambda name, after: None):
    t, d = x2d.shape

    z, alow = _inproj(u, w_main, w_alow)
    y, sall = _mixer_fwd(z, alow, wgu_pad, b_gate, convw_taps, conv_norm_g, gla_norm_g)
    w_out_full = get_w_out(y)
    x1, h = _outproj(y, w_out_full, x2d, norm2_g)
    w1g = get_w1(h)
    a = _ff1(h, w1g)
    w2_full = get_w2(a)
    d_ff = w2_full.shape[0]
    x2 = _ff2(a, w2_full, x1)
    dx2, dx2b, loss_part, d_normf = _loss_head(x2, norm_f_g.reshape(1, d), tgt)

    tk = min(4096, t)
    nk = t // tk
    da = _dff2(dx2b, w2_full, a)
    dw2 = _tn_matmul(
        "dw_ff2", a, dx2b, (d_ff // 1024, d // 1024, nk),
        pl.BlockSpec((tk, 1024), lambda m, j, kk: (kk, m)), pl.BlockSpec((tk, 1024), lambda m, j, kk: (kk, j)),
        jax.ShapeDtypeStruct((d_ff, d), BF16), pl.BlockSpec((1024, 1024), lambda m, j, kk: (m, j)), (1024, 1024),
        a_fn=_relu_sq)
    token = on_grad("w_ff2", dw2)
    f_shard = d_ff // N_DEV
    dw1 = _tn_matmul(
        "dw_ff1", h, da, (N_DEV, d // 1024, nk),
        pl.BlockSpec((tk, 1024), lambda g, m, kk: (kk, m)), pl.BlockSpec((tk, f_shard), lambda g, m, kk: (kk, g)),
        jax.ShapeDtypeStruct((N_DEV, d, f_shard), BF16), pl.BlockSpec((None, 1024, f_shard), lambda g, m, kk: (g, m, 0)),
        (1024, f_shard), behind=token)
    token = on_grad("w_ff1", dw1)
    dh = _dh(da, w1g, behind=token)
    dx1, dx1b, d_norm2, dy = _norm_bwd_dy(dh, x1, norm2_g, dx2, w_out_full)
    dwout = _tn_matmul(
        "dw_out", y, dx1b, (d // 1024, d // 1024, nk),
        pl.BlockSpec((tk, 1024), lambda m, j, kk: (kk, m)), pl.BlockSpec((tk, 1024), lambda m, j, kk: (kk, j)),
        jax.ShapeDtypeStruct((d, d), BF16), pl.BlockSpec((1024, 1024), lambda m, j, kk: (m, j)), (1024, 1024))
    token = on_grad("w_out", dwout)
    dz, dzal, d_convw, d_convg, d_glag, d_bgate, d_wgu = _mixer_bwd(
        z, alow, dy, sall, wgu_pad, b_gate, convw_taps, conv_norm_g, gla_norm_g, behind=token)
    token = on_grad("w_in", _dw_in(u, dz, dzal, tk))
    sides = [s for s in (side_for("w_ff2", token), side_for("w_ff1", token)) if s is not None]
    du, adam = _du(dz, w_main, dzal, w_alow, behind=token, side=sides)
    adam_ff2, adam_ff1 = adam if adam else (None, None)
    grad_x, d_norm1 = _norm_bwd("norm1_bwd", du, x2d, norm1_g, dx1)
    return dict(x=grad_x, loss=loss_part, adam_w_ff2=adam_ff2, adam_w_ff1=adam_ff1,
                norm1_g=d_norm1, w_gate_up=d_wgu, b_gate=d_bgate, conv_w=d_convw,
                conv_norm_g=d_convg, gla_norm_g=d_glag, norm2_g=d_norm2, norm_f_g=d_normf)


_WEIGHT_ORDER = ("norm1_g", "w_in", "w_gate_up", "b_gate", "conv_w", "conv_norm_g", "gla_norm_g", "w_out", "norm2_g",
                 "w_ff1", "w_ff2", "norm_f_g")
_SMALL_ORDER = ("norm1_g", "b_gate", "conv_norm_g", "gla_norm_g", "norm2_g", "norm_f_g", "w_gate_up", "conv_w")
def _update(me, gin_r, gout_r, done, get_small, small_shapes, grad_x, wmv):
    big = dict(done)
    big["w_in"] = _adamw("adamw_w_in", gin_r, *(a[0] for a in wmv["w_in"]), 256)
    big["w_out"] = _adamw("adamw_w_out", gout_r, *(a[0] for a in wmv["w_out"]), 128)

    wgu_cols = wmv["w_gate_up"][0].shape[2]
    cw_rows = wmv["conv_w"][0].shape[1]

    small_r = get_small([big[nm][3] for nm in ("w_in", "w_out")])
    summed = _unpack_rows(_sum_partials(small_r), small_shapes)
    summed[6] = lax.dynamic_slice_in_dim(summed[6], me * wgu_cols, wgu_cols, axis=1)
    summed[7] = lax.dynamic_slice_in_dim(summed[7], me * cw_rows, cw_rows, axis=0)
    as_2d = lambda a: a.reshape((1, -1) if a.ndim == 1 else a.shape[-2:])
    grads_2d = [as_2d(g) for g in summed[:len(_SMALL_ORDER)]]
    small = _adamw_small(grads_2d, *[[as_2d(wmv[nm][k]) for nm in _SMALL_ORDER] for k in range(3)])
    small = [grads_2d] + small

    outs = []
    for k in range(4):
        for nm in _WEIGHT_ORDER:
            if nm in big:
                outs.append(big[nm][k][None])
            else:
                outs.append(small[k][_SMALL_ORDER.index(nm)].reshape(wmv[nm][0].shape))
    loss = summed[8][0]
    return (loss, grad_x[None], *outs)
```

```python
import jax
import jax.numpy as jnp
from jax import lax
from jax.experimental import pallas as pl
from jax.experimental.pallas import tpu as pltpu

F32 = jnp.float32
BF16 = jnp.bfloat16

N_DEV = 8
CHUNK = 64
GLA_HEADS = 4
CONV_GROUPS = 8
CONV_WIDTH = 3
GATE_RANK = 16
GATE_NORMALIZER = 16.0
EPS = 1e-6
ADAM_LR = 0.001
ADAM_B1 = 0.9
ADAM_B2 = 0.999
ADAM_EPS = 1e-08
ADAM_WD = 0.01
ADAM_STEP = 10

LANES = 128
SUBLANES = 8
VMEM_LIMIT = 56 << 20

_NN = (((1,), (0,)), ((), ()))
_NT = (((1,), (1,)), ((), ()))
_TN = (((0,), (0,)), ((), ()))


def _dot(a, b, dims=_NN):
    return lax.dot_general(a, b, dims, preferred_element_type=F32)


def _params(n_grid):
    return pltpu.CompilerParams(dimension_semantics=("arbitrary",) * n_grid, vmem_limit_bytes=VMEM_LIMIT)


def _relu_sq(a):
    r = jnp.maximum(a, 0.0)
    return r * r


def _device_index():
    return 4 * lax.axis_index("x") + 2 * lax.axis_index("y") + lax.axis_index("c")


def _peer(mask):
    x, y, c = lax.axis_index("x"), lax.axis_index("y"), lax.axis_index("c")
    return (x ^ ((mask >> 2) & 1), y ^ ((mask >> 1) & 1), c ^ (mask & 1))


_HBM_SPEC = pl.BlockSpec(memory_space=pltpu.HBM)
_SEM_SPEC = pl.BlockSpec(memory_space=pltpu.SEMAPHORE)
_SIDE_EFFECT = pltpu.SideEffectType.DATAFLOW_SIDE_EFFECTING
N_PEERS = N_DEV - 1


def _exchange_copy(src_ref, land_ref, send_sems, recv_sems, mask, scatter, arriving):
    me = _device_index()
    src = src_ref.at[me ^ mask] if scatter else src_ref
    dst = land_ref.at[(me ^ mask) if arriving else me]
    return pltpu.make_async_remote_copy(
        src_ref=src, dst_ref=dst, send_sem=send_sems.at[mask - 1], recv_sem=recv_sems.at[mask - 1],
        device_id=_peer(mask), device_id_type=pl.DeviceIdType.MESH)


def _land_zone(own):
    zone = lax.empty((N_DEV,) + own.shape, own.dtype)
    return lax.dynamic_update_slice(zone, own[None], (_device_index(),) + (0,) * own.ndim)


ALL_PEERS = tuple(range(1, N_DEV))
SIBLING = 1
SAME_CORE_PEERS = (2, 4, 6)


def _exchange_start(name, srcs, lands, scatter, masks=None, behind=None):
    n = len(srcs)
    masks = masks or [ALL_PEERS] * n
    dep_args = [] if behind is None else [behind]

    def body(*refs):
        src, land = refs[:n], refs[n:2 * n]
        outs = refs[2 * n + len(dep_args):]
        send_sems, recv_sems = outs[:n], outs[n:2 * n]
        token = refs[-1]
        for a in range(n):
            for mask in masks[a]:
                _exchange_copy(src[a], land[a], send_sems[a], recv_sems[a], mask, scatter, False).start()
        token[...] = jnp.zeros_like(token)

    hbm = lambda a: pltpu.HBM(a.shape, a.dtype)
    outs = pl.pallas_call(
        body, name=name,
        out_shape=([pltpu.SemaphoreType.DMA((N_PEERS,))] * (2 * n) + [hbm(a) for a in srcs] + [hbm(a) for a in lands]
                   + [jax.ShapeDtypeStruct((SUBLANES, LANES), F32)]),
        in_specs=[_HBM_SPEC] * (2 * n) + [pl.BlockSpec(memory_space=pl.ANY)] * len(dep_args),
        out_specs=[_SEM_SPEC] * (2 * n) + [_HBM_SPEC] * (2 * n) + [pl.BlockSpec(memory_space=pltpu.VMEM)],
        input_output_aliases={a: 2 * n + a for a in range(2 * n)},
        compiler_params=pltpu.CompilerParams(has_side_effects=_SIDE_EFFECT),
    )(*[pltpu.with_memory_space_constraint(a, pltpu.HBM) for a in list(srcs) + list(lands)], *dep_args)
    send_sems, recv_sems = outs[:n], outs[n:2 * n]
    src_thru, land_thru = outs[2 * n:3 * n], outs[3 * n:4 * n]
    return send_sems, recv_sems, src_thru, land_thru, outs[-1]


def _exchange_wait(name, send_sems, recv_sems, src_thru, land_thru, after, scatter, masks=ALL_PEERS):
    after = list(after) if isinstance(after, (list, tuple)) else [after]

    def body(src_ref, land_ref, send_ref, recv_ref, *rest):
        for mask in masks:
            cp = _exchange_copy(src_ref, land_ref, send_ref, recv_ref, mask, scatter, True)
            cp.wait_send()
            cp.wait_recv()

    return pl.pallas_call(
        body, name=name,
        out_shape=(pltpu.HBM(src_thru.shape, src_thru.dtype), pltpu.HBM(land_thru.shape, land_thru.dtype)),
        in_specs=[_HBM_SPEC, _HBM_SPEC, _SEM_SPEC, _SEM_SPEC] + [pl.BlockSpec(memory_space=pl.ANY)] * len(after),
        out_specs=(_HBM_SPEC, _HBM_SPEC), input_output_aliases={0: 0, 1: 1},
        compiler_params=pltpu.CompilerParams(has_side_effects=_SIDE_EFFECT),
    )(src_thru, land_thru, send_sems, recv_sems, *after)[1]


def _forward_copy(land_ref, send_sems, recv_sems, k, arriving):
    me = _device_index()
    slot = me ^ SAME_CORE_PEERS[k]
    return pltpu.make_async_remote_copy(
        src_ref=land_ref.at[slot], dst_ref=land_ref.at[(slot ^ SIBLING) if arriving else slot],
        send_sem=send_sems.at[k], recv_sem=recv_sems.at[k],
        device_id=_peer(SIBLING), device_id_type=pl.DeviceIdType.MESH)


def _forward_start(name, land):
    n_fwd = len(SAME_CORE_PEERS)

    def body(land_ref, send_sems, recv_sems, land_thru):
        for k in range(n_fwd):
            _forward_copy(land_ref, send_sems, recv_sems, k, False).start()

    send, recv, thru = pl.pallas_call(
        body, name=name,
        out_shape=[pltpu.SemaphoreType.DMA((n_fwd,)), pltpu.SemaphoreType.DMA((n_fwd,)), pltpu.HBM(land.shape, land.dtype)],
        in_specs=[_HBM_SPEC], out_specs=[_SEM_SPEC, _SEM_SPEC, _HBM_SPEC], input_output_aliases={0: 2},
        compiler_params=pltpu.CompilerParams(has_side_effects=_SIDE_EFFECT),
    )(pltpu.with_memory_space_constraint(land, pltpu.HBM))
    return send, recv, thru


def _forward_wait(name, send_sems, recv_sems, land_thru):
    def body(land_ref, send_ref, recv_ref, got_ref):
        for k in range(len(SAME_CORE_PEERS)):
            cp = _forward_copy(land_ref, send_ref, recv_ref, k, True)
            cp.wait_send()
            cp.wait_recv()

    return pl.pallas_call(
        body, name=name, out_shape=pltpu.HBM(land_thru.shape, land_thru.dtype),
        in_specs=[_HBM_SPEC, _SEM_SPEC, _SEM_SPEC], out_specs=_HBM_SPEC, input_output_aliases={0: 0},
        compiler_params=pltpu.CompilerParams(has_side_effects=_SIDE_EFFECT),
    )(land_thru, send_sems, recv_sems)


def _shards_to_columns(g, n_main, tr=512):
    n_dev, d, s = g.shape

    def body(g_ref, main_ref, rest_ref):
        for j in range(n_dev):
            lo, hi = j * s, (j + 1) * s
            if hi <= n_main:
                main_ref[:, lo:hi] = g_ref[j]
            else:
                main_ref[:, lo:n_main] = g_ref[j, :, 0:n_main - lo]
                rest_ref[...] = jnp.zeros_like(rest_ref)
                rest_ref[:, 0:hi - n_main] = g_ref[j, :, n_main - lo:s]

    return pl.pallas_call(
        body, grid=(d // tr,), name="shards_to_columns",
        in_specs=[pl.BlockSpec((n_dev, tr, s), lambda i: (0, i, 0))],
        out_specs=[pl.BlockSpec((tr, n_main), lambda i: (i, 0)), pl.BlockSpec((tr, LANES), lambda i: (i, 0))],
        out_shape=[jax.ShapeDtypeStruct((d, n_main), g.dtype), jax.ShapeDtypeStruct((d, LANES), g.dtype)],
        compiler_params=_params(1),
    )(g)


def _columns_to_shards(main, rest, n_dev, s, tr=512):
    d, n_main = main.shape
    assert (n_dev - 1) * s <= n_main < n_dev * s

    def body(main_ref, rest_ref, o_ref):
        for j in range(n_dev):
            lo, hi = j * s, (j + 1) * s
            if hi <= n_main:
                o_ref[j] = main_ref[:, lo:hi]
            else:
                o_ref[j, :, 0:n_main - lo] = main_ref[:, lo:n_main]
                o_ref[j, :, n_main - lo:s] = rest_ref[:, 0:hi - n_main]

    return pl.pallas_call(
        body, grid=(d // tr,), name="columns_to_shards",
        in_specs=[pl.BlockSpec((tr, n_main), lambda i: (i, 0)), pl.BlockSpec((tr, LANES), lambda i: (i, 0))],
        out_specs=pl.BlockSpec((n_dev, tr, s), lambda i: (0, i, 0)),
        out_shape=jax.ShapeDtypeStruct((n_dev, d, s), main.dtype),
        compiler_params=_params(1),
    )(main, rest)


def _rmsnorm(x, g, tr=512, behind=None):
    t, d = x.shape
    tr = min(tr, t)
    dep_args, dep_specs = _behind(behind)

    def body(x_ref, g_ref, *rest):
        u_ref = rest[-1]
        xf = x_ref[...]
        r = lax.rsqrt(jnp.mean(xf * xf, axis=-1, keepdims=True) + EPS)
        u_ref[...] = (xf * r * g_ref[...]).astype(BF16)

    return pl.pallas_call(
        body, name="rmsnorm1", grid=(t // tr,),
        in_specs=[pl.BlockSpec((tr, d), lambda i: (i, 0)), pl.BlockSpec((1, d), lambda i: (0, 0))] + dep_specs,
        out_specs=pl.BlockSpec((tr, d), lambda i: (i, 0)),
        out_shape=jax.ShapeDtypeStruct((t, d), BF16),
        compiler_params=_params(1),
    )(x, g, *dep_args)


def _inproj(u, w_main, w_alow, tm=1024, tn=1024):
    t, d = u.shape
    tm = min(tm, t)
    n = w_main.shape[1]

    def body(u_ref, w_ref, wa_ref, z_ref, al_ref):
        @pl.when(pl.program_id(1) == 0)
        def _():
            al_ref[...] = _dot(u_ref[...], wa_ref[...])

        z_ref[...] = _dot(u_ref[...], w_ref[...])

    return pl.pallas_call(
        body, name="inproj", grid=(t // tm, n // tn),
        in_specs=[pl.BlockSpec((tm, d), lambda m, j: (m, 0)), pl.BlockSpec((d, tn), lambda m, j: (0, j)),
                  pl.BlockSpec((d, LANES), lambda m, j: (0, 0))],
        out_specs=[pl.BlockSpec((tm, tn), lambda m, j: (m, j)), pl.BlockSpec((tm, LANES), lambda m, j: (m, 0))],
        out_shape=[jax.ShapeDtypeStruct((t, n), F32), jax.ShapeDtypeStruct((t, LANES), F32)],
        compiler_params=_params(2),
    )(u, w_main, w_alow)


def _outproj(y, w_out, x, g2, tm=512):
    t, d = x.shape
    tm = min(tm, t)
    k = y.shape[1]

    def body(y_ref, w_ref, x_ref, g_ref, x1_ref, h_ref):
        x1 = x_ref[...] + _dot(y_ref[...], w_ref[...])
        x1_ref[...] = x1
        r = lax.rsqrt(jnp.mean(x1 * x1, axis=-1, keepdims=True) + EPS)
        h_ref[...] = (x1 * r * g_ref[...]).astype(BF16)

    return pl.pallas_call(
        body, name="outproj_rmsnorm", grid=(t // tm,),
        in_specs=[pl.BlockSpec((tm, k), lambda m: (m, 0)), pl.BlockSpec((k, d), lambda m: (0, 0)),
                  pl.BlockSpec((tm, d), lambda m: (m, 0)), pl.BlockSpec((1, d), lambda m: (0, 0))],
        out_specs=[pl.BlockSpec((tm, d), lambda m: (m, 0)), pl.BlockSpec((tm, d), lambda m: (m, 0))],
        out_shape=[jax.ShapeDtypeStruct((t, d), F32), jax.ShapeDtypeStruct((t, d), BF16)],
        compiler_params=_params(1),
    )(y, w_out, x, g2)


def _ff1(h, w1g, tm=1024):
    t, d = h.shape
    tm = min(tm, t)
    g, _, f = w1g.shape

    def body(h_ref, w_ref, a_ref):
        a_ref[...] = _dot(h_ref[...], w_ref[...]).astype(BF16)

    return pl.pallas_call(
        body, name="ff1", grid=(t // tm, g),
        in_specs=[pl.BlockSpec((tm, d), lambda m, j: (m, 0)), pl.BlockSpec((None, d, f), lambda m, j: (j, 0, 0))],
        out_specs=pl.BlockSpec((tm, f), lambda m, j: (m, j)),
        out_shape=jax.ShapeDtypeStruct((t, g * f), BF16),
        compiler_params=_params(2),
    )(h, w1g)


def _ff2(a, w2, x1, tm=1024, tn=1024, tk=2048):
    t, f = a.shape
    tm = min(tm, t)
    d = w2.shape[1]

    def body(a_ref, w_ref, x1_ref, o_ref):
        @pl.when(pl.program_id(2) == 0)
        def _():
            o_ref[...] = x1_ref[...]

        o_ref[...] += _dot(_relu_sq(a_ref[...]), w_ref[...])

    return pl.pallas_call(
        body, name="ff2_residual", grid=(t // tm, d // tn, f // tk),
        in_specs=[pl.BlockSpec((tm, tk), lambda m, j, kk: (m, kk)), pl.BlockSpec((tk, tn), lambda m, j, kk: (kk, j)),
                  pl.BlockSpec((tm, tn), lambda m, j, kk: (m, j))],
        out_specs=pl.BlockSpec((tm, tn), lambda m, j, kk: (m, j)),
        out_shape=jax.ShapeDtypeStruct((t, d), F32),
        compiler_params=_params(3),
    )(a, w2, x1)


def _dff2(dx2b, w2, a, tm=1024, tn=1024):
    t, d = dx2b.shape
    tm = min(tm, t)
    f = w2.shape[0]

    def body(g_ref, w_ref, a_ref, o_ref):
        dp = _dot(g_ref[...], w_ref[...], _NT)
        o_ref[...] = (dp * (2.0 * jnp.maximum(a_ref[...].astype(F32), 0.0))).astype(BF16)

    return pl.pallas_call(
        body, name="dff2", grid=(t // tm, f // tn),
        in_specs=[pl.BlockSpec((tm, d), lambda m, j: (m, 0)), pl.BlockSpec((tn, d), lambda m, j: (j, 0)),
                  pl.BlockSpec((tm, tn), lambda m, j: (m, j))],
        out_specs=pl.BlockSpec((tm, tn), lambda m, j: (m, j)),
        out_shape=jax.ShapeDtypeStruct((t, f), BF16),
        compiler_params=_params(2),
    )(dx2b, w2, a)


def _behind(token):
    if token is None:
        return [], []
    return [token], [pl.BlockSpec(token.shape, lambda *_: (0,) * token.ndim)]


def _tn_matmul(name, a, b, grid, a_spec, b_spec, out_shape, out_spec, acc_shape, a_fn=None, behind=None):
    nk = grid[-1]
    dep_args, dep_specs = _behind(behind)

    def body(a_ref, b_ref, *rest):
        o_ref, acc_ref = rest[-2:]
        kk = pl.program_id(len(grid) - 1)
        av = a_ref[...]
        if a_fn is not None:
            av = a_fn(av)
        part = _dot(av, b_ref[...], _TN)

        @pl.when(kk == 0)
        def _():
            acc_ref[...] = part

        @pl.when(kk > 0)
        def _():
            acc_ref[...] += part

        @pl.when(kk == nk - 1)
        def _():
            o_ref[...] = acc_ref[...].astype(o_ref.dtype)

    return pl.pallas_call(
        body, name=name, grid=grid, in_specs=[a_spec, b_spec] + dep_specs, out_specs=out_spec, out_shape=out_shape,
        scratch_shapes=[pltpu.VMEM(acc_shape, F32)], compiler_params=_params(len(grid)),
    )(a, b, *dep_args)


def _dw_in(u, dz, dzal, tk, tm=1024, tn=1024):
    t, d = u.shape
    n_main = dz.shape[1]
    nk = t // tk

    def body(a_ref, b_ref, al_ref, o_ref, oal_ref, acc_ref, accal_ref):
        j, kk = pl.program_id(1), pl.program_id(2)
        av = a_ref[...]

        def accumulate(acc, part, out):
            @pl.when(kk == 0)
            def _():
                acc[...] = part

            @pl.when(kk > 0)
            def _():
                acc[...] += part

            @pl.when(kk == nk - 1)
            def _():
                out[...] = acc[...].astype(out.dtype)

        accumulate(acc_ref, _dot(av, b_ref[...], _TN), o_ref)

        @pl.when(j == 0)
        def _():
            accumulate(accal_ref, _dot(av, al_ref[...], _TN), oal_ref)

    return pl.pallas_call(
        body, name="dw_in", grid=(d // tm, n_main // tn, nk),
        in_specs=[pl.BlockSpec((tk, tm), lambda m, j, kk: (kk, m)), pl.BlockSpec((tk, tn), lambda m, j, kk: (kk, j)),
                  pl.BlockSpec((tk, LANES), lambda m, j, kk: (kk, 0))],
        out_specs=[pl.BlockSpec((tm, tn), lambda m, j, kk: (m, j)), pl.BlockSpec((tm, LANES), lambda m, j, kk: (m, 0))],
        out_shape=[jax.ShapeDtypeStruct((d, n_main), BF16), jax.ShapeDtypeStruct((d, LANES), BF16)],
        scratch_shapes=[pltpu.VMEM((tm, tn), F32), pltpu.VMEM((tm, LANES), F32)],
        compiler_params=_params(3),
    )(u, dz, dzal)


class _SideAdamW:
    def __init__(self, side, grid):
        parts, w, m, v = side
        n_parts, r, c = parts.shape
        steps = 1
        for extent in grid:
            steps *= extent
        rows = r // steps
        assert rows * steps == r and rows % (2 * SUBLANES) == 0, (r, steps)

        def step(*ids):
            lin = ids[0]
            for extent, idx in zip(grid[1:], ids[1:]):
                lin = lin * extent + idx
            return lin

        slab = pl.BlockSpec((rows, c), lambda *ids: (step(*ids), 0))
        self.args = [parts, w, m, v]
        self.in_specs = [pl.BlockSpec((n_parts, rows, c), lambda *ids: (0, step(*ids), 0)), slab, slab, slab]
        self.out_specs = [slab] * 4
        self.out_shape = [jax.ShapeDtypeStruct((r, c), F32)] * 4
        self.n_parts = n_parts

    def run(self, in_refs, out_refs):
        p_ref, w_ref, m_ref, v_ref = in_refs
        g = p_ref[0].astype(F32)
        for j in range(1, self.n_parts):
            g = g + p_ref[j].astype(F32)
        out_refs[0][...] = g
        out_refs[1][...], out_refs[2][...], out_refs[3][...] = _adamw_math(g, w_ref[...], m_ref[...], v_ref[...])


def _dh(da, w1g, tm=1024, tn=1024, shards_per_step=4, behind=None):
    t = da.shape[0]
    tm = min(tm, t)
    g, d, f = w1g.shape
    sps = shards_per_step
    grid = (t // tm, d // tn, g // sps)
    dep_args, dep_specs = _behind(behind)

    def body(a_ref, w_ref, *rest):
        o_ref = rest[-1]
        acc = _dot(a_ref[:, 0:f], w_ref[0], _NT)
        for s in range(1, sps):
            acc = acc + _dot(a_ref[:, s * f:(s + 1) * f], w_ref[s], _NT)

        @pl.when(pl.program_id(2) == 0)
        def _():
            o_ref[...] = acc

        @pl.when(pl.program_id(2) > 0)
        def _():
            o_ref[...] += acc

    return pl.pallas_call(
        body, name="dh", grid=grid,
        in_specs=[pl.BlockSpec((tm, sps * f), lambda m, j, kk: (m, kk)),
                  pl.BlockSpec((sps, tn, f), lambda m, j, kk: (kk, j, 0))] + dep_specs,
        out_specs=pl.BlockSpec((tm, tn), lambda m, j, kk: (m, j)),
        out_shape=jax.ShapeDtypeStruct((t, d), F32),
        compiler_params=_params(3),
    )(da, w1g, *dep_args)


def _du(dz, w_main, dzal, w_alow, tm=1024, tn=1024, tk=3072, behind=None, side=None):
    t, n = dz.shape
    tm = min(tm, t)
    d = w_main.shape[0]
    grid = (t // tm, d // tn, n // tk)
    dep_args, dep_specs = _behind(behind)
    adams = [_SideAdamW(s, grid) for s in (side or [])]
    n_dep, n_side = len(dep_args), len(adams)

    def body(a_ref, w_ref, al_ref, wa_ref, *rest):
        o_ref = rest[n_dep + 4 * n_side]

        @pl.when(pl.program_id(2) == 0)
        def _():
            o_ref[...] = _dot(al_ref[...], wa_ref[...], _NT)

        o_ref[...] += _dot(a_ref[...], w_ref[...], _NT)
        for k, adam in enumerate(adams):
            first_out = n_dep + 4 * n_side + 1 + 4 * k
            adam.run(rest[n_dep + 4 * k:n_dep + 4 * k + 4], rest[first_out:first_out + 4])

    outs = pl.pallas_call(
        body, name="du", grid=grid,
        in_specs=[pl.BlockSpec((tm, tk), lambda m, j, kk: (m, kk)), pl.BlockSpec((tn, tk), lambda m, j, kk: (j, kk)),
                  pl.BlockSpec((tm, LANES), lambda m, j, kk: (m, 0)), pl.BlockSpec((tn, LANES), lambda m, j, kk: (j, 0))]
        + dep_specs + [s for adam in adams for s in adam.in_specs],
        out_specs=[pl.BlockSpec((tm, tn), lambda m, j, kk: (m, j))] + [s for adam in adams for s in adam.out_specs],
        out_shape=[jax.ShapeDtypeStruct((t, d), F32)] + [s for adam in adams for s in adam.out_shape],
        compiler_params=_params(3),
    )(dz, w_main, dzal, w_alow, *dep_args, *[a for adam in adams for a in adam.args])
    return outs[0], [outs[1 + 4 * k:5 + 4 * k] for k in range(n_side)]


def _loss_head(x2, gf, tgt, tr=512):
    t, d = x2.shape
    tr = min(tr, t)

    def body(x_ref, g_ref, t_ref, dx_ref, dxb_ref, loss_ref, dg_ref):
        @pl.when(pl.program_id(0) == 0)
        def _():
            loss_ref[...] = jnp.zeros_like(loss_ref)
            dg_ref[...] = jnp.zeros_like(dg_ref)

        xf = x_ref[...]
        g = g_ref[...]
        r = lax.rsqrt(jnp.mean(xf * xf, axis=-1, keepdims=True) + EPS)
        xh = xf * r
        e = xh * g - t_ref[...]
        loss_ref[...] += 0.5 * jnp.sum(jnp.mean(e * e, axis=-1, keepdims=True))
        dy = e * (1.0 / d)
        dg_ref[...] += jnp.sum(dy * xh, axis=0, keepdims=True)
        dyg = dy * g
        dx = r * (dyg - xh * jnp.mean(dyg * xh, axis=-1, keepdims=True))
        dx_ref[...] = dx
        dxb_ref[...] = dx.astype(BF16)

    return pl.pallas_call(
        body, name="loss_head", grid=(t // tr,),
        in_specs=[pl.BlockSpec((tr, d), lambda i: (i, 0)), pl.BlockSpec((1, d), lambda i: (0, 0)),
                  pl.BlockSpec((tr, d), lambda i: (i, 0))],
        out_specs=[pl.BlockSpec((tr, d), lambda i: (i, 0)), pl.BlockSpec((tr, d), lambda i: (i, 0)),
                   pl.BlockSpec((SUBLANES, LANES), lambda i: (0, 0)), pl.BlockSpec((1, d), lambda i: (0, 0))],
        out_shape=[jax.ShapeDtypeStruct((t, d), F32), jax.ShapeDtypeStruct((t, d), BF16),
                   jax.ShapeDtypeStruct((SUBLANES, LANES), F32), jax.ShapeDtypeStruct((1, d), F32)],
        compiler_params=_params(1),
    )(x2, gf, tgt)


def _norm_bwd_dy(dh, x1, g2, dx2, w_out, tm=256):
    t, d = x1.shape
    k = w_out.shape[0]
    tm = min(tm, t)

    def body(dh_ref, x_ref, g_ref, dr_ref, w_ref, dx_ref, dxb_ref, dg_ref, dy_ref):
        @pl.when(pl.program_id(0) == 0)
        def _():
            dg_ref[...] = jnp.zeros_like(dg_ref)

        xf = x_ref[...]
        dhv = dh_ref[...]
        r = lax.rsqrt(jnp.mean(xf * xf, axis=-1, keepdims=True) + EPS)
        xh = xf * r
        dg_ref[...] += jnp.sum(dhv * xh, axis=0, keepdims=True)
        dyg = dhv * g_ref[...]
        dx = dr_ref[...] + r * (dyg - xh * jnp.mean(dyg * xh, axis=-1, keepdims=True))
        dx_ref[...] = dx
        dxb = dx.astype(BF16)
        dxb_ref[...] = dxb
        dy_ref[...] = _dot(dxb, w_ref[...], _NT)

    rows = pl.BlockSpec((tm, d), lambda i: (i, 0))
    vec = pl.BlockSpec((1, d), lambda i: (0, 0))
    return pl.pallas_call(
        body, name="norm2_bwd_dy", grid=(t // tm,),
        in_specs=[rows, rows, vec, rows, pl.BlockSpec((k, d), lambda i: (0, 0))],
        out_specs=[rows, rows, vec, pl.BlockSpec((tm, k), lambda i: (i, 0))],
        out_shape=[jax.ShapeDtypeStruct((t, d), F32), jax.ShapeDtypeStruct((t, d), BF16),
                   jax.ShapeDtypeStruct((1, d), F32), jax.ShapeDtypeStruct((t, k), F32)],
        compiler_params=_params(1),
    )(dh, x1, g2, dx2, w_out)


def _norm_bwd(name, dh, xin, g, dres, tr=512):
    t, d = xin.shape
    tr = min(tr, t)

    def body(dh_ref, x_ref, g_ref, dr_ref, dx_ref, dg_ref):
        @pl.when(pl.program_id(0) == 0)
        def _():
            dg_ref[...] = jnp.zeros_like(dg_ref)

        xf = x_ref[...]
        dhv = dh_ref[...]
        r = lax.rsqrt(jnp.mean(xf * xf, axis=-1, keepdims=True) + EPS)
        xh = xf * r
        dg_ref[...] += jnp.sum(dhv * xh, axis=0, keepdims=True)
        dyg = dhv * g_ref[...]
        dx = dr_ref[...] + r * (dyg - xh * jnp.mean(dyg * xh, axis=-1, keepdims=True))
        dx_ref[...] = dx

    rows = pl.BlockSpec((tr, d), lambda i: (i, 0))
    vec = pl.BlockSpec((1, d), lambda i: (0, 0))
    return pl.pallas_call(
        body, name=name, grid=(t // tr,),
        in_specs=[rows, rows, vec, rows], out_specs=[rows, vec],
        out_shape=[jax.ShapeDtypeStruct((t, d), F32), jax.ShapeDtypeStruct((1, d), F32)],
        compiler_params=_params(1),
    )(dh, xin, g, dres)


MIX_TILE = 256
CHUNKS_PER_TILE = MIX_TILE // CHUNK
CHUNK_SHIFT = CHUNK.bit_length() - 1
assert 1 << CHUNK_SHIFT == CHUNK


def _chunk_masks(n):
    row = lax.broadcasted_iota(jnp.int32, (n, n), 0)
    col = lax.broadcasted_iota(jnp.int32, (n, n), 1)
    same = lax.shift_right_logical(row, CHUNK_SHIFT) == lax.shift_right_logical(col, CHUNK_SHIFT)
    one = lambda m: jnp.where(m, 1.0, 0.0).astype(BF16)
    return jnp.concatenate([one(same & (col > row)), one(same)], axis=0), one(same & (col < row))


def _mask_dot(mask, x):
    hi = x.astype(BF16)
    r1 = x - hi.astype(F32)
    mid = r1.astype(BF16)
    lo = (r1 - mid.astype(F32)).astype(BF16)
    return _dot(mask, hi) + _dot(mask, mid) + _dot(mask, lo)


def _log_sigmoid(x):
    return jnp.minimum(x, 0.0) - jnp.log1p(jnp.exp(-jnp.abs(x)))


def _conv_taps(prev8, uc, w):
    ext = jnp.concatenate([prev8, uc], axis=0)
    s1 = pltpu.roll(ext, 1, 0)[SUBLANES:]
    s2 = pltpu.roll(ext, 2, 0)[SUBLANES:]
    return s2 * w[0:1] + s1 * w[1:2] + uc * w[2:3], s1, s2


def _z_specs(tile, idx):
    d_conv = 1024
    wide = lambda c: pl.BlockSpec((tile, d_conv), lambda i, c=c: (idx(i), c))
    half = lambda c: pl.BlockSpec((tile, d_conv // 2), lambda i, c=c: (idx(i), c))
    return [wide(0), wide(1), wide(2), half(6), half(7), wide(4), wide(5)]


def _mixer_fwd(z, alow, wgu, b_gate, convw, conv_g, gla_g):
    t = z.shape[0]
    tb, cpt = MIX_TILE, CHUNKS_PER_TILE
    d_conv = conv_g.shape[1]
    dv = gla_g.shape[1]
    dk = dv // 2
    gw = d_conv // CONV_GROUPS
    scale = dk ** -0.5

    def body(cb_ref, cc_ref, ch_ref, q_ref, k_ref, v_ref, og_ref, al_ref, wgu_ref, bg_ref, cw_ref, cg_ref, gg_ref,
             y_ref, sall_ref, carry_ref, s_ref):
        @pl.when(pl.program_id(0) == 0)
        def _():
            carry_ref[...] = jnp.zeros_like(carry_ref)
            s_ref[...] = jnp.zeros_like(s_ref)

        uc = cc_ref[...] * ch_ref[...]
        conv, _, _ = _conv_taps(carry_ref[...], uc, cw_ref[...])
        carry_ref[...] = uc[tb - SUBLANES:]
        ypre = cb_ref[...] * conv
        cg = cg_ref[...]
        for g in range(CONV_GROUPS):
            sl = slice(g * gw, (g + 1) * gw)
            seg = ypre[:, sl]
            r = lax.rsqrt(jnp.mean(seg * seg, axis=-1, keepdims=True) + EPS)
            y_ref[:, sl] = (seg * r * cg[:, sl]).astype(BF16)

        later_and_same, _ = _chunk_masks(tb)
        pre = _dot(al_ref[...].astype(BF16), wgu_ref[...]) + bg_ref[...]
        la = _log_sigmoid(pre) * (1.0 / GATE_NORMALIZER)
        sums = _mask_dot(later_and_same, la)
        e_dec = sums[:tb]
        dec_all = jnp.exp(sums[tb:])
        kdec = (k_ref[...] * jnp.exp(e_dec)).astype(BF16)
        qs = (q_ref[...] * scale).astype(BF16)
        vb = v_ref[...].astype(BF16)
        gg = gg_ref[...]
        rows = [slice(c * CHUNK, (c + 1) * CHUNK) for c in range(cpt)]
        ks = [slice(h * dk, (h + 1) * dk) for h in range(GLA_HEADS)]
        vs = [slice(h * dv, (h + 1) * dv) for h in range(GLA_HEADS)]
        kvt = [[_dot(vb[rows[c], vs[h]], kdec[rows[c], ks[h]], _TN) for h in range(GLA_HEADS)] for c in range(cpt)]
        state = [s_ref[h] for h in range(GLA_HEADS)]
        states = []
        for c in range(cpt):
            state = [state[h] * dec_all[c * CHUNK:c * CHUNK + 1, ks[h]] + kvt[c][h] for h in range(GLA_HEADS)]
            states.append(state)
            for h in range(GLA_HEADS):
                sall_ref[c, h] = state[h]
        for h in range(GLA_HEADS):
            s_ref[h] = state[h]
        for h in range(GLA_HEADS):
            o = jnp.concatenate(
                [_dot(qs[rows[c], ks[h]], states[c][h].astype(BF16), _NT) for c in range(cpt)], axis=0)
            ro = lax.rsqrt(jnp.mean(o * o, axis=-1, keepdims=True) + EPS)
            ogs = og_ref[:, vs[h]]
            yg = o * ro * gg * (ogs * jax.nn.sigmoid(ogs))
            y_ref[:, d_conv + h * dv:d_conv + (h + 1) * dv] = yg.astype(BF16)

    full = lambda shape: pl.BlockSpec(shape, lambda i: (0,) * len(shape))
    return pl.pallas_call(
        body, name="mixer_fwd", grid=(t // tb,),
        in_specs=_z_specs(tb, lambda i: i) + [
            pl.BlockSpec((tb, LANES), lambda i: (i, 0)), full(wgu.shape), full(b_gate.shape), full(convw.shape),
            full(conv_g.shape), full(gla_g.shape)],
        out_specs=[pl.BlockSpec((tb, d_conv + GLA_HEADS * dv), lambda i: (i, 0)),
                   pl.BlockSpec((cpt, GLA_HEADS, dv, dk), lambda i: (i, 0, 0, 0))],
        out_shape=[jax.ShapeDtypeStruct((t, d_conv + GLA_HEADS * dv), BF16),
                   jax.ShapeDtypeStruct((t // CHUNK, GLA_HEADS, dv, dk), F32)],
        scratch_shapes=[pltpu.VMEM((SUBLANES, d_conv), F32), pltpu.VMEM((GLA_HEADS, dv, dk), F32)],
        compiler_params=_params(1),
    )(z, z, z, z, z, z, z, alow, wgu, b_gate, convw, conv_g, gla_g)


def _mixer_bwd(z, alow, dy, sall, wgu, b_gate, convw, conv_g, gla_g, behind=None):
    t = z.shape[0]
    tb, cpt = MIX_TILE, CHUNKS_PER_TILE
    nt = t // tb
    d_conv = conv_g.shape[1]
    dv = gla_g.shape[1]
    dk = dv // 2
    d_k = GLA_HEADS * dk
    gw = d_conv // CONV_GROUPS
    scale = dk ** -0.5
    rev = lambda i: nt - 1 - i
    dep_args, dep_specs = _behind(behind)

    def body(cb_ref, cc_ref, ch_ref, q_ref, k_ref, v_ref, og_ref, ccp_ref, chp_ref, al_ref, dy_ref, sall_ref, sprev_ref,
             wgu_ref, bg_ref, cw_ref, cg_ref, gg_ref, *rest):
        dz_ref, dzal_ref, dcw_ref, dcg_ref, dgg_ref, dbg_ref, dwgu_ref, dcarry_ref, gd_ref = rest[-9:]
        i = pl.program_id(0)

        @pl.when(i == 0)
        def _():
            dcarry_ref[...] = jnp.zeros_like(dcarry_ref)
            gd_ref[...] = jnp.zeros_like(gd_ref)
            dcw_ref[...] = jnp.zeros_like(dcw_ref)
            dcg_ref[...] = jnp.zeros_like(dcg_ref)
            dgg_ref[...] = jnp.zeros_like(dgg_ref)
            dbg_ref[...] = jnp.zeros_like(dbg_ref)
            dwgu_ref[...] = jnp.zeros_like(dwgu_ref)

        first = rev(i) == 0

        cb, cc, ch = cb_ref[...], cc_ref[...], ch_ref[...]
        w = cw_ref[...]
        uc = cc * ch
        prev8 = jnp.where(first, 0.0, ccp_ref[...] * chp_ref[...])
        conv, s1, s2 = _conv_taps(prev8, uc, w)
        ypre = cb * conv
        cg = cg_ref[...]
        dypre_parts = []
        for g in range(CONV_GROUPS):
            sl = slice(g * gw, (g + 1) * gw)
            seg = ypre[:, sl]
            r = lax.rsqrt(jnp.mean(seg * seg, axis=-1, keepdims=True) + EPS)
            yn = seg * r
            dyc = dy_ref[:, sl]
            dcg_ref[:, sl] += jnp.sum(dyc * yn, axis=0, keepdims=True)
            dyn = dyc * cg[:, sl]
            dypre_parts.append(r * (dyn - yn * jnp.mean(dyn * yn, axis=-1, keepdims=True)))
        dypre = jnp.concatenate(dypre_parts, axis=1)
        dconv = dypre * cb
        dz_ref[:, 0:d_conv] = (dypre * conv).astype(BF16)
        dcw_ref[0:1] += jnp.sum(dconv * s2, axis=0, keepdims=True)
        dcw_ref[1:2] += jnp.sum(dconv * s1, axis=0, keepdims=True)
        dcw_ref[2:3] += jnp.sum(dconv * uc, axis=0, keepdims=True)
        ext = jnp.concatenate([dconv, dcarry_ref[...]], axis=0)
        f1 = pltpu.roll(ext, tb + SUBLANES - 1, 0)[:tb]
        f2 = pltpu.roll(ext, tb + SUBLANES - 2, 0)[:tb]
        dcarry_ref[...] = dconv[:SUBLANES]
        duc = dconv * w[2:3] + f1 * w[1:2] + f2 * w[0:1]
        dz_ref[:, d_conv:2 * d_conv] = (duc * ch).astype(BF16)
        dz_ref[:, 2 * d_conv:3 * d_conv] = (duc * cc).astype(BF16)

        q_off = 3 * d_conv
        k_off = q_off + d_k
        v_off = k_off + d_k
        og_off = v_off + GLA_HEADS * dv
        later_and_same, earlier = _chunk_masks(tb)
        alb = al_ref[...].astype(BF16)
        pre = _dot(alb, wgu_ref[...]) + bg_ref[...]
        la = _log_sigmoid(pre) * (1.0 / GATE_NORMALIZER)
        decays = jnp.exp(_mask_dot(later_and_same, la))
        exp_e, dec_all = decays[:tb], decays[tb:]
        kdec = k_ref[...] * exp_e
        kdec_b = kdec.astype(BF16)
        qs = (q_ref[...] * scale).astype(BF16)
        vb = v_ref[...].astype(BF16)
        gg = gg_ref[...]
        rows = [slice(c * CHUNK, (c + 1) * CHUNK) for c in range(cpt)]
        ks = [slice(h * dk, (h + 1) * dk) for h in range(GLA_HEADS)]
        vs = [slice(h * dv, (h + 1) * dv) for h in range(GLA_HEADS)]
        st_b = [[sall_ref[c, h].astype(BF16) for h in range(GLA_HEADS)] for c in range(cpt)]
        do_b = []
        dgg = jnp.zeros_like(gg)
        for h in range(GLA_HEADS):
            o = jnp.concatenate([_dot(qs[rows[c], ks[h]], st_b[c][h], _NT) for c in range(cpt)], axis=0)
            ro = lax.rsqrt(jnp.mean(o * o, axis=-1, keepdims=True) + EPS)
            on = o * ro
            ogs = og_ref[:, vs[h]]
            sg = jax.nn.sigmoid(ogs)
            gate = ogs * sg
            dyg = dy_ref[:, d_conv + h * dv:d_conv + (h + 1) * dv]
            dgg = dgg + jnp.sum(dyg * on * gate, axis=0, keepdims=True)
            dz_ref[:, og_off + h * dv:og_off + (h + 1) * dv] = (
                dyg * on * gg * (sg * (1.0 + ogs * (1.0 - sg)))).astype(BF16)
            don = dyg * gg * gate
            do_b.append((ro * (don - on * jnp.mean(don * on, axis=-1, keepdims=True))).astype(BF16))
        dgg_ref[...] += dgg
        for h in range(GLA_HEADS):
            dq = jnp.concatenate([_dot(do_b[h][rows[c]], st_b[c][h]) for c in range(cpt)], axis=0)
            dz_ref[:, q_off + h * dk:q_off + (h + 1) * dk] = (dq * scale).astype(BF16)
        own = [[_dot(do_b[h][rows[c]], qs[rows[c], ks[h]], _TN) for h in range(GLA_HEADS)] for c in range(cpt)]
        carried = [gd_ref[h] for h in range(GLA_HEADS)]
        gt_b = [None] * cpt
        ddd = [None] * cpt
        for c in reversed(range(cpt)):
            gt = [own[c][h] + carried[h] for h in range(GLA_HEADS)]
            dec = [dec_all[c * CHUNK:c * CHUNK + 1, ks[h]] for h in range(GLA_HEADS)]
            carried = [gt[h] * dec[h] for h in range(GLA_HEADS)]
            if c > 0:
                st_prev = [sall_ref[c - 1, h] for h in range(GLA_HEADS)]
            else:
                st_prev = [jnp.where(first, 0.0, sprev_ref[0, h]) for h in range(GLA_HEADS)]
            ddec = [jnp.sum(gt[h] * st_prev[h], axis=0, keepdims=True) * dec[h] for h in range(GLA_HEADS)]
            ddd[c] = jnp.broadcast_to(jnp.concatenate(ddec, axis=1), (CHUNK, d_k))
            gt_b[c] = [gt[h].astype(BF16) for h in range(GLA_HEADS)]
        for h in range(GLA_HEADS):
            gd_ref[h] = carried[h]
        dkdec_cols = []
        for h in range(GLA_HEADS):
            dvh = jnp.concatenate([_dot(kdec_b[rows[c], ks[h]], gt_b[c][h], _NT) for c in range(cpt)], axis=0)
            dz_ref[:, v_off + h * dv:v_off + (h + 1) * dv] = dvh.astype(BF16)
            dkdec_cols.append(jnp.concatenate([_dot(vb[rows[c], vs[h]], gt_b[c][h]) for c in range(cpt)], axis=0))
        dkdec = jnp.concatenate(dkdec_cols, axis=1)
        dz_ref[:, k_off:k_off + d_k] = (dkdec * exp_e).astype(BF16)
        dla = _mask_dot(earlier, dkdec * kdec) + jnp.concatenate(ddd, axis=0)
        dpre = dla * (1.0 / GATE_NORMALIZER) * jax.nn.sigmoid(-pre)
        dbg_ref[...] += jnp.sum(dpre, axis=0, keepdims=True)
        dpre_b = dpre.astype(BF16)
        dwgu_ref[...] += _dot(alb, dpre_b, _TN)
        dzal_ref[...] = _dot(dpre_b, wgu_ref[...], _NT).astype(BF16)

    full = lambda shape: pl.BlockSpec(shape, lambda i: (0,) * len(shape))
    prev_rows = lambda c: pl.BlockSpec(
        (SUBLANES, d_conv), lambda i, c=c: (jnp.maximum(rev(i) * (tb // SUBLANES) - 1, 0), c))
    n_z = 3 * d_conv + 2 * d_k + 2 * GLA_HEADS * dv
    return pl.pallas_call(
        body, name="mixer_bwd", grid=(nt,),
        in_specs=_z_specs(tb, rev) + [
            prev_rows(1), prev_rows(2),
            pl.BlockSpec((tb, LANES), lambda i: (rev(i), 0)),
            pl.BlockSpec((tb, d_conv + GLA_HEADS * dv), lambda i: (rev(i), 0)),
            pl.BlockSpec((cpt, GLA_HEADS, dv, dk), lambda i: (rev(i), 0, 0, 0)),
            pl.BlockSpec((1, GLA_HEADS, dv, dk), lambda i: (jnp.maximum(rev(i) * cpt - 1, 0), 0, 0, 0)),
            full(wgu.shape), full(b_gate.shape), full(convw.shape), full(conv_g.shape), full(gla_g.shape)]
        + dep_specs,
        out_specs=[pl.BlockSpec((tb, n_z), lambda i: (rev(i), 0)), pl.BlockSpec((tb, LANES), lambda i: (rev(i), 0)),
                   full(convw.shape), full(conv_g.shape), full(gla_g.shape), full(b_gate.shape), full(wgu.shape)],
        out_shape=[jax.ShapeDtypeStruct((t, n_z), BF16), jax.ShapeDtypeStruct((t, LANES), BF16),
                   jax.ShapeDtypeStruct(convw.shape, F32), jax.ShapeDtypeStruct(conv_g.shape, F32),
                   jax.ShapeDtypeStruct(gla_g.shape, F32), jax.ShapeDtypeStruct(b_gate.shape, F32),
                   jax.ShapeDtypeStruct(wgu.shape, F32)],
        scratch_shapes=[pltpu.VMEM((SUBLANES, d_conv), F32), pltpu.VMEM((GLA_HEADS, dv, dk), F32)],
        compiler_params=_params(1),
    )(z, z, z, z, z, z, z, z, z, alow, dy, sall, sall, wgu, b_gate, convw, conv_g, gla_g, *dep_args)


def _adamw_math(g, w, m, v):
    m = ADAM_B1 * m + (1.0 - ADAM_B1) * g
    v = ADAM_B2 * v + (1.0 - ADAM_B2) * (g * g)
    m_hat = m / (1.0 - ADAM_B1 ** ADAM_STEP)
    v_hat = v / (1.0 - ADAM_B2 ** ADAM_STEP)
    delta = -ADAM_LR * (m_hat / (jnp.sqrt(v_hat) + ADAM_EPS) + ADAM_WD * w)
    return delta, m, v


def _adamw(name, parts, w, m, v, tr):
    r, c = w.shape
    n_parts = parts.shape[0]

    def body(p_ref, w_ref, m_ref, v_ref, g_ref, d_ref, nm_ref, nv_ref):
        g = p_ref[0].astype(F32)
        for j in range(1, n_parts):
            g = g + p_ref[j].astype(F32)
        g_ref[...] = g
        d_ref[...], nm_ref[...], nv_ref[...] = _adamw_math(g, w_ref[...], m_ref[...], v_ref[...])

    blk = pl.BlockSpec((tr, c), lambda i: (i, 0))
    return pl.pallas_call(
        body, name=name, grid=(r // tr,),
        in_specs=[pl.BlockSpec((n_parts, tr, c), lambda i: (0, i, 0)), blk, blk, blk],
        out_specs=[blk] * 4, out_shape=[jax.ShapeDtypeStruct((r, c), F32)] * 4,
        compiler_params=_params(1),
    )(parts, w, m, v)


def _adamw_small(grads, ws, ms, vs):
    n = len(grads)

    def body(*refs):
        g, w, m, v = (refs[k * n:(k + 1) * n] for k in range(4))
        d_out, m_out, v_out = (refs[(4 + k) * n:(5 + k) * n] for k in range(3))
        for i in range(n):
            d_out[i][...], m_out[i][...], v_out[i][...] = _adamw_math(g[i][...], w[i][...], m[i][...], v[i][...])

    vmem = pl.BlockSpec(memory_space=pltpu.VMEM)
    outs = pl.pallas_call(
        body, name="adamw_small", out_shape=[jax.ShapeDtypeStruct(w.shape, F32) for w in ws] * 3,
        in_specs=[vmem] * (4 * n), out_specs=[vmem] * (3 * n),
    )(*grads, *ws, *ms, *vs)
    return [outs[:n], outs[n:2 * n], outs[2 * n:]]


def _sum_partials(parts):
    n_parts, rows, lanes = parts.shape

    def body(p_ref, o_ref):
        g = p_ref[0]
        for j in range(1, n_parts):
            g = g + p_ref[j]
        o_ref[...] = g

    return pl.pallas_call(
        body, name="sum_small_partials", out_shape=jax.ShapeDtypeStruct((rows, lanes), F32),
        in_specs=[pl.BlockSpec(memory_space=pltpu.VMEM)], out_specs=pl.BlockSpec(memory_space=pltpu.VMEM),
    )(parts)


def _pack_rows(vectors, rows):
    flat = jnp.concatenate([a.reshape(-1).astype(F32) for a in vectors])
    return jnp.pad(flat, (0, rows * LANES - flat.shape[0])).reshape(rows, LANES)


def _unpack_rows(block, shapes):
    flat = block.reshape(-1)
    out, off = [], 0
    for s in shapes:
        n = 1
        for dim in s:
            n *= dim
        out.append(flat[off:off + n].reshape(s))
        off += n
    return out


def kernel(x, norm1_g, w_in, w_gate_up, b_gate, conv_w, conv_norm_g, gla_norm_g, w_out, norm2_g, w_ff1, w_ff2, norm_f_g, loss_target, m_norm1_g, m_w_in, m_w_gate_up, m_b_gate, m_conv_w, m_conv_norm_g, m_gla_norm_g, m_w_out, m_norm2_g, m_w_ff1, m_w_ff2, m_norm_f_g, v_norm1_g, v_w_in, v_w_gate_up, v_b_gate, v_conv_w, v_conv_norm_g, v_gla_norm_g, v_w_out, v_norm2_g, v_w_ff1, v_w_ff2, v_norm_f_g):
    me = _device_index()
    x2d, tgt = x[0], loss_target[0]
    d = x2d.shape[1]
    d_in_shard = w_in.shape[2]
    n_main = N_DEV * d_in_shard - GATE_RANK
    d_conv = conv_norm_g.shape[1]
    d_k = b_gate.shape[1]
    d_ff = N_DEV * w_ff1.shape[2]
    wmv = dict(
        norm1_g=(norm1_g, m_norm1_g, v_norm1_g), w_in=(w_in, m_w_in, v_w_in),
        w_gate_up=(w_gate_up, m_w_gate_up, v_w_gate_up), b_gate=(b_gate, m_b_gate, v_b_gate),
        conv_w=(conv_w, m_conv_w, v_conv_w), conv_norm_g=(conv_norm_g, m_conv_norm_g, v_conv_norm_g),
        gla_norm_g=(gla_norm_g, m_gla_norm_g, v_gla_norm_g), w_out=(w_out, m_w_out, v_w_out),
        norm2_g=(norm2_g, m_norm2_g, v_norm2_g), w_ff1=(w_ff1, m_w_ff1, v_w_ff1), w_ff2=(w_ff2, m_w_ff2, v_w_ff2),
        norm_f_g=(norm_f_g, m_norm_f_g, v_norm_f_g))

    small_rows = 16
    first_level = (SIBLING,) + SAME_CORE_PEERS
    win_shard = w_in[0].astype(BF16)
    in_send, in_recv, in_src, in_land, token = _exchange_start(
        "all_gather_start_w_in", [win_shard], [_land_zone(win_shard)], scatter=False, masks=[first_level])
    _, wgu_t, cw_t, wout_t, w1_t, w2_t = lax.optimization_barrier((token, w_gate_up, conv_w, w_out, w_ff1, w_ff2))
    small_shard = _pack_rows([wgu_t[0], cw_t[0]], small_rows)
    shards = [small_shard, wout_t[0].astype(BF16), w1_t[0].astype(BF16), w2_t[0].astype(BF16)]
    ag_send, ag_recv, ag_src, ag_land, token = _exchange_start(
        "all_gather_start", shards, [_land_zone(s) for s in shards], scatter=False, behind=token)

    def gathered(k, name, after):
        return _exchange_wait(name, ag_send[k], ag_recv[k], ag_src[k], ag_land[k], after, scatter=False)

    u = _rmsnorm(x2d, norm1_g, behind=token)
    tied = lax.optimization_barrier((token, w_in, m_w_in, v_w_in))
    wmv["w_in"] = tuple(tied[1:])
    small_g = gathered(0, "all_gather_wait_small", [u] + [a[0] for a in wmv["w_in"]])
    win_level1 = _exchange_wait(
        "all_gather_wait_w_in", in_send[0], in_recv[0], in_src[0], in_land[0], small_g, scatter=False,
        masks=first_level)
    win_g = _forward_wait("all_gather_wait_w_in_forwarded", *_forward_start("all_gather_forward_w_in", win_level1))
    w_main, w_alow = _shards_to_columns(win_g, n_main)
    small_flat = small_g.reshape(N_DEV, -1)
    n_wgu = GATE_RANK * (d_k // N_DEV)
    wgu_full = small_flat[:, :n_wgu].reshape(N_DEV, GATE_RANK, d_k // N_DEV).transpose(1, 0, 2).reshape(GATE_RANK, d_k)
    conv_w_full = small_flat[:, n_wgu:n_wgu + (d_conv // N_DEV) * CONV_WIDTH].reshape(d_conv, CONV_WIDTH)
    wgu_pad = jnp.pad(wgu_full, ((0, LANES - GATE_RANK), (0, 0))).astype(BF16)
    convw_taps = jnp.pad(conv_w_full.T, ((0, SUBLANES - CONV_WIDTH), (0, 0)))

    get_w_out = lambda after: gathered(1, "all_gather_wait_w_out", after).reshape(-1, d)
    get_w1 = lambda after: gathered(2, "all_gather_wait_w_ff1", after)
    get_w2 = lambda after: gathered(3, "all_gather_wait_w_ff2", after).reshape(d_ff, d)

    in_flight = {}

    def send_partials(name, parts):
        own = lax.dynamic_index_in_dim(parts, me, axis=0, keepdims=False)
        send, recv, src, land, token = _exchange_start("scatter_start_" + name, [parts], [_land_zone(own)], scatter=True)
        in_flight[name] = (send[0], recv[0], src[0], land[0])
        return token

    def on_grad(name, value):
        if name == "w_in":
            main, alow_part = value
            value = _columns_to_shards(main, alow_part, N_DEV, d_in_shard)
        elif name in ("w_out", "w_ff2"):
            value = value.reshape(N_DEV, -1, d)
        return send_partials(name, value)

    def received(name, after):
        send, recv, src, land = in_flight[name]
        return _exchange_wait("scatter_wait_" + name, send, recv, src, land, after, scatter=True)

    def side_for(name, after):
        return (received(name, after),) + tuple(a[0] for a in wmv[name])

    grads = _local_step(x2d, u, tgt, norm1_g, w_main, w_alow, wgu_pad, b_gate, convw_taps, conv_norm_g, gla_norm_g,
                        norm2_g, norm_f_g, get_w_out, get_w1, get_w2, on_grad, side_for)
    grad_x = grads["x"]

    small_shapes = [(1, d), (1, d_k), (1, d_conv), (1, gla_norm_g.shape[1]), (1, d), (d,),
                    (GATE_RANK, d_k), (d_conv, CONV_WIDTH), (1,)]
    small_grad_rows = 152
    small_part = _pack_rows(
        [grads["norm1_g"], grads["b_gate"], grads["conv_norm_g"], grads["gla_norm_g"], grads["norm2_g"],
         grads["norm_f_g"], grads["w_gate_up"][:GATE_RANK], grads["conv_w"][:CONV_WIDTH].T, grads["loss"][0, 0]],
        small_grad_rows)
    small_token = send_partials("small", jnp.broadcast_to(small_part[None], (N_DEV, small_grad_rows, LANES)))

    gin_r, gout_r = (received(nm, [grad_x, small_token]) for nm in ("w_in", "w_out"))
    get_small = lambda after: received("small", after)
    done = {"w_ff1": grads["adam_w_ff1"], "w_ff2": grads["adam_w_ff2"]}
    return _update(me, gin_r, gout_r, done, get_small, small_shapes, grad_x, wmv)


def _local_step(x2d, u, tgt, norm1_g, w_main, w_alow, wgu_pad, b_gate, convw_taps, conv_norm_g, gla_norm_g,
                norm2_g, norm_f_g, get_w_out, get_w1, get_w2, on_grad, side_for=lambda name, after: None):
    t, d = x2d.shape

    z, alow = _inproj(u, w_main, w_alow)
    y, sall = _mixer_fwd(z, alow, wgu_pad, b_gate, convw_taps, conv_norm_g, gla_norm_g)
    w_out_full = get_w_out(y)
    x1, h = _outproj(y, w_out_full, x2d, norm2_g)
    w1g = get_w1(h)
    a = _ff1(h, w1g)
    w2_full = get_w2(a)
    d_ff = w2_full.shape[0]
    x2 = _ff2(a, w2_full, x1)
    dx2, dx2b, loss_part, d_normf = _loss_head(x2, norm_f_g.reshape(1, d), tgt)

    tk = min(4096, t)
    nk = t // tk
    da = _dff2(dx2b, w2_full, a)
    dw2 = _tn_matmul(
        "dw_ff2", a, dx2b, (d_ff // 1024, d // 1024, nk),
        pl.BlockSpec((tk, 1024), lambda m, j, kk: (kk, m)), pl.BlockSpec((tk, 1024), lambda m, j, kk: (kk, j)),
        jax.ShapeDtypeStruct((d_ff, d), BF16), pl.BlockSpec((1024, 1024), lambda m, j, kk: (m, j)), (1024, 1024),
        a_fn=_relu_sq)
    token = on_grad("w_ff2", dw2)
    f_shard = d_ff // N_DEV
    dw1 = _tn_matmul(
        "dw_ff1", h, da, (N_DEV, d // 1024, nk),
        pl.BlockSpec((tk, 1024), lambda g, m, kk: (kk, m)), pl.BlockSpec((tk, f_shard), lambda g, m, kk: (kk, g)),
        jax.ShapeDtypeStruct((N_DEV, d, f_shard), BF16), pl.BlockSpec((None, 1024, f_shard), lambda g, m, kk: (g, m, 0)),
        (1024, f_shard), behind=token)
    token = on_grad("w_ff1", dw1)
    dh = _dh(da, w1g, behind=token)
    dx1, dx1b, d_norm2, dy = _norm_bwd_dy(dh, x1, norm2_g, dx2, w_out_full)
    dwout = _tn_matmul(
        "dw_out", y, dx1b, (d // 1024, d // 1024, nk),
        pl.BlockSpec((tk, 1024), lambda m, j, kk: (kk, m)), pl.BlockSpec((tk, 1024), lambda m, j, kk: (kk, j)),
        jax.ShapeDtypeStruct((d, d), BF16), pl.BlockSpec((1024, 1024), lambda m, j, kk: (m, j)), (1024, 1024))
    token = on_grad("w_out", dwout)
    dz, dzal, d_convw, d_convg, d_glag, d_bgate, d_wgu = _mixer_bwd(
        z, alow, dy, sall, wgu_pad, b_gate, convw_taps, conv_norm_g, gla_norm_g, behind=token)
    token = on_grad("w_in", _dw_in(u, dz, dzal, tk))
    sides = [s for s in (side_for("w_ff2", token), side_for("w_ff1", token)) if s is not None]
    du, adam = _du(dz, w_main, dzal, w_alow, behind=token, side=sides)
    adam_ff2, adam_ff1 = adam if adam else (None, None)
    grad_x, d_norm1 = _norm_bwd("norm1_bwd", du, x2d, norm1_g, dx1)
    return dict(x=grad_x, loss=loss_part, adam_w_ff2=adam_ff2, adam_w_ff1=adam_ff1,
                norm1_g=d_norm1, w_gate_up=d_wgu, b_gate=d_bgate, conv_w=d_convw,
                conv_norm_g=d_convg, gla_norm_g=d_glag, norm2_g=d_norm2, norm_f_g=d_normf)


_WEIGHT_ORDER = ("norm1_g", "w_in", "w_gate_up", "b_gate", "conv_w", "conv_norm_g", "gla_norm_g", "w_out", "norm2_g",
                 "w_ff1", "w_ff2", "norm_f_g")
_SMALL_ORDER = ("norm1_g", "b_gate", "conv_norm_g", "gla_norm_g", "norm2_g", "norm_f_g", "w_gate_up", "conv_w")
def _update(me, gin_r, gout_r, done, get_small, small_shapes, grad_x, wmv):
    big = dict(done)
    big["w_in"] = _adamw("adamw_w_in", gin_r, *(a[0] for a in wmv["w_in"]), 256)
    big["w_out"] = _adamw("adamw_w_out", gout_r, *(a[0] for a in wmv["w_out"]), 128)

    wgu_cols = wmv["w_gate_up"][0].shape[2]
    cw_rows = wmv["conv_w"][0].shape[1]

    small_r = get_small([big[nm][3] for nm in ("w_in", "w_out")])
    summed = _unpack_rows(_sum_partials(small_r), small_shapes)
    summed[6] = lax.dynamic_slice_in_dim(summed[6], me * wgu_cols, wgu_cols, axis=1)
    summed[7] = lax.dynamic_slice_in_dim(summed[7], me * cw_rows, cw_rows, axis=0)
    as_2d = lambda a: a.reshape((1, -1) if a.ndim == 1 else a.shape[-2:])
    grads_2d = [as_2d(g) for g in summed[:len(_SMALL_ORDER)]]
    small = _adamw_small(grads_2d, *[[as_2d(wmv[nm][k]) for nm in _SMALL_ORDER] for k in range(3)])
    small = [grads_2d] + small

    outs = []
    for k in range(4):
        for nm in _WEIGHT_ORDER:
            if nm in big:
                outs.append(big[nm][k][None])
            else:
                outs.append(small[k][_SMALL_ORDER.index(nm)].reshape(wmv[nm][0].shape))
    loss = summed[8][0]
    return (loss, grad_x[None], *outs)
```

```python
import jax
import jax.numpy as jnp
from jax import lax
from jax.experimental import pallas as pl
from jax.experimental.pallas import tpu as pltpu

F32 = jnp.float32
BF16 = jnp.bfloat16

N_DEV = 8
CHUNK = 64
GLA_HEADS = 4
CONV_GROUPS = 8
CONV_WIDTH = 3
GATE_RANK = 16
GATE_NORMALIZER = 16.0
EPS = 1e-6
ADAM_LR = 0.001
ADAM_B1 = 0.9
ADAM_B2 = 0.999
ADAM_EPS = 1e-08
ADAM_WD = 0.01
ADAM_STEP = 10

LANES = 128
SUBLANES = 8
VMEM_LIMIT = 56 << 20

_NN = (((1,), (0,)), ((), ()))
_NT = (((1,), (1,)), ((), ()))
_TN = (((0,), (0,)), ((), ()))


def _dot(a, b, dims=_NN):
    return lax.dot_general(a, b, dims, preferred_element_type=F32)


def _params(n_grid):
    return pltpu.CompilerParams(dimension_semantics=("arbitrary",) * n_grid, vmem_limit_bytes=VMEM_LIMIT)


def _relu_sq(a):
    r = jnp.maximum(a, 0.0)
    return r * r


def _device_index():
    return 4 * lax.axis_index("x") + 2 * lax.axis_index("y") + lax.axis_index("c")


def _peer(mask):
    x, y, c = lax.axis_index("x"), lax.axis_index("y"), lax.axis_index("c")
    return (x ^ ((mask >> 2) & 1), y ^ ((mask >> 1) & 1), c ^ (mask & 1))


_HBM_SPEC = pl.BlockSpec(memory_space=pltpu.HBM)
_SEM_SPEC = pl.BlockSpec(memory_space=pltpu.SEMAPHORE)
_SIDE_EFFECT = pltpu.SideEffectType.DATAFLOW_SIDE_EFFECTING
N_PEERS = N_DEV - 1


def _exchange_copy(src_ref, land_ref, send_sems, recv_sems, mask, scatter, arriving):
    me = _device_index()
    src = src_ref.at[me ^ mask] if scatter else src_ref
    dst = land_ref.at[(me ^ mask) if arriving else me]
    return pltpu.make_async_remote_copy(
        src_ref=src, dst_ref=dst, send_sem=send_sems.at[mask - 1], recv_sem=recv_sems.at[mask - 1],
        device_id=_peer(mask), device_id_type=pl.DeviceIdType.MESH)


def _land_zone(own):
    zone = lax.empty((N_DEV,) + own.shape, own.dtype)
    return lax.dynamic_update_slice(zone, own[None], (_device_index(),) + (0,) * own.ndim)


ALL_PEERS = tuple(range(1, N_DEV))
SIBLING = 1
SAME_CORE_PEERS = (2, 4, 6)


def _exchange_start(name, srcs, lands, scatter, masks=None, behind=None):
    n = len(srcs)
    masks = masks or [ALL_PEERS] * n
    dep_args = [] if behind is None else [behind]

    def body(*refs):
        src, land = refs[:n], refs[n:2 * n]
        outs = refs[2 * n + len(dep_args):]
        send_sems, recv_sems = outs[:n], outs[n:2 * n]
        token = refs[-1]
        for a in range(n):
            for mask in masks[a]:
                _exchange_copy(src[a], land[a], send_sems[a], recv_sems[a], mask, scatter, False).start()
        token[...] = jnp.zeros_like(token)

    hbm = lambda a: pltpu.HBM(a.shape, a.dtype)
    outs = pl.pallas_call(
        body, name=name,
        out_shape=([pltpu.SemaphoreType.DMA((N_PEERS,))] * (2 * n) + [hbm(a) for a in srcs] + [hbm(a) for a in lands]
                   + [jax.ShapeDtypeStruct((SUBLANES, LANES), F32)]),
        in_specs=[_HBM_SPEC] * (2 * n) + [pl.BlockSpec(memory_space=pl.ANY)] * len(dep_args),
        out_specs=[_SEM_SPEC] * (2 * n) + [_HBM_SPEC] * (2 * n) + [pl.BlockSpec(memory_space=pltpu.VMEM)],
        input_output_aliases={a: 2 * n + a for a in range(2 * n)},
        compiler_params=pltpu.CompilerParams(has_side_effects=_SIDE_EFFECT),
    )(*[pltpu.with_memory_space_constraint(a, pltpu.HBM) for a in list(srcs) + list(lands)], *dep_args)
    send_sems, recv_sems = outs[:n], outs[n:2 * n]
    src_thru, land_thru = outs[2 * n:3 * n], outs[3 * n:4 * n]
    return send_sems, recv_sems, src_thru, land_thru, outs[-1]


def _exchange_wait(name, send_sems, recv_sems, src_thru, land_thru, after, scatter, masks=ALL_PEERS):
    after = list(after) if isinstance(after, (list, tuple)) else [after]

    def body(src_ref, land_ref, send_ref, recv_ref, *rest):
        for mask in masks:
            cp = _exchange_copy(src_ref, land_ref, send_ref, recv_ref, mask, scatter, True)
            cp.wait_send()
            cp.wait_recv()

    return pl.pallas_call(
        body, name=name,
        out_shape=(pltpu.HBM(src_thru.shape, src_thru.dtype), pltpu.HBM(land_thru.shape, land_thru.dtype)),
        in_specs=[_HBM_SPEC, _HBM_SPEC, _SEM_SPEC, _SEM_SPEC] + [pl.BlockSpec(memory_space=pl.ANY)] * len(after),
        out_specs=(_HBM_SPEC, _HBM_SPEC), input_output_aliases={0: 0, 1: 1},
        compiler_params=pltpu.CompilerParams(has_side_effects=_SIDE_EFFECT),
    )(src_thru, land_thru, send_sems, recv_sems, *after)[1]


def _forward_copy(land_ref, send_sems, recv_sems, k, arriving):
    me = _device_index()
    slot = me ^ SAME_CORE_PEERS[k]
    return pltpu.make_async_remote_copy(
        src_ref=land_ref.at[slot], dst_ref=land_ref.at[(slot ^ SIBLING) if arriving else slot],
        send_sem=send_sems.at[k], recv_sem=recv_sems.at[k],
        device_id=_peer(SIBLING), device_id_type=pl.DeviceIdType.MESH)


def _forward_start(name, land):
    n_fwd = len(SAME_CORE_PEERS)

    def body(land_ref, send_sems, recv_sems, land_thru):
        for k in range(n_fwd):
            _forward_copy(land_ref, send_sems, recv_sems, k, False).start()

    send, recv, thru = pl.pallas_call(
        body, name=name,
        out_shape=[pltpu.SemaphoreType.DMA((n_fwd,)), pltpu.SemaphoreType.DMA((n_fwd,)), pltpu.HBM(land.shape, land.dtype)],
        in_specs=[_HBM_SPEC], out_specs=[_SEM_SPEC, _SEM_SPEC, _HBM_SPEC], input_output_aliases={0: 2},
        compiler_params=pltpu.CompilerParams(has_side_effects=_SIDE_EFFECT),
    )(pltpu.with_memory_space_constraint(land, pltpu.HBM))
    return send, recv, thru


def _forward_wait(name, send_sems, recv_sems, land_thru):
    def body(land_ref, send_ref, recv_ref, got_ref):
        for k in range(len(SAME_CORE_PEERS)):
            cp = _forward_copy(land_ref, send_ref, recv_ref, k, True)
            cp.wait_send()
            cp.wait_recv()

    return pl.pallas_call(
        body, name=name, out_shape=pltpu.HBM(land_thru.shape, land_thru.dtype),
        in_specs=[_HBM_SPEC, _SEM_SPEC, _SEM_SPEC], out_specs=_HBM_SPEC, input_output_aliases={0: 0},
        compiler_params=pltpu.CompilerParams(has_side_effects=_SIDE_EFFECT),
    )(land_thru, send_sems, recv_sems)


def _shards_to_columns(g, n_main, tr=512):
    n_dev, d, s = g.shape

    def body(g_ref, main_ref, rest_ref):
        for j in range(n_dev):
            lo, hi = j * s, (j + 1) * s
            if hi <= n_main:
                main_ref[:, lo:hi] = g_ref[j]
            else:
                main_ref[:, lo:n_main] = g_ref[j, :, 0:n_main - lo]
                rest_ref[...] = jnp.zeros_like(rest_ref)
                rest_ref[:, 0:hi - n_main] = g_ref[j, :, n_main - lo:s]

    return pl.pallas_call(
        body, grid=(d // tr,), name="shards_to_columns",
        in_specs=[pl.BlockSpec((n_dev, tr, s), lambda i: (0, i, 0))],
        out_specs=[pl.BlockSpec((tr, n_main), lambda i: (i, 0)), pl.BlockSpec((tr, LANES), lambda i: (i, 0))],
        out_shape=[jax.ShapeDtypeStruct((d, n_main), g.dtype), jax.ShapeDtypeStruct((d, LANES), g.dtype)],
        compiler_params=_params(1),
    )(g)


def _columns_to_shards(main, rest, n_dev, s, tr=512):
    d, n_main = main.shape
    assert (n_dev - 1) * s <= n_main < n_dev * s

    def body(main_ref, rest_ref, o_ref):
        for j in range(n_dev):
            lo, hi = j * s, (j + 1) * s
            if hi <= n_main:
                o_ref[j] = main_ref[:, lo:hi]
            else:
                o_ref[j, :, 0:n_main - lo] = main_ref[:, lo:n_main]
                o_ref[j, :, n_main - lo:s] = rest_ref[:, 0:hi - n_main]

    return pl.pallas_call(
        body, grid=(d // tr,), name="columns_to_shards",
        in_specs=[pl.BlockSpec((tr, n_main), lambda i: (i, 0)), pl.BlockSpec((tr, LANES), lambda i: (i, 0))],
        out_specs=pl.BlockSpec((n_dev, tr, s), lambda i: (0, i, 0)),
        out_shape=jax.ShapeDtypeStruct((n_dev, d, s), main.dtype),
        compiler_params=_params(1),
    )(main, rest)


def _rmsnorm(x, g, tr=512, behind=None):
    t, d = x.shape
    tr = min(tr, t)
    dep_args, dep_specs = _behind(behind)

    def body(x_ref, g_ref, *rest):
        u_ref = rest[-1]
        xf = x_ref[...]
        r = lax.rsqrt(jnp.mean(xf * xf, axis=-1, keepdims=True) + EPS)
        u_ref[...] = (xf * r * g_ref[...]).astype(BF16)

    return pl.pallas_call(
        body, name="rmsnorm1", grid=(t // tr,),
        in_specs=[pl.BlockSpec((tr, d), lambda i: (i, 0)), pl.BlockSpec((1, d), lambda i: (0, 0))] + dep_specs,
        out_specs=pl.BlockSpec((tr, d), lambda i: (i, 0)),
        out_shape=jax.ShapeDtypeStruct((t, d), BF16),
        compiler_params=_params(1),
    )(x, g, *dep_args)


def _inproj(u, w_main, w_alow, tm=1024, tn=1024):
    t, d = u.shape
    tm = min(tm, t)
    n = w_main.shape[1]

    def body(u_ref, w_ref, wa_ref, z_ref, al_ref):
        @pl.when(pl.program_id(1) == 0)
        def _():
            al_ref[...] = _dot(u_ref[...], wa_ref[...])

        z_ref[...] = _dot(u_ref[...], w_ref[...])

    return pl.pallas_call(
        body, name="inproj", grid=(t // tm, n // tn),
        in_specs=[pl.BlockSpec((tm, d), lambda m, j: (m, 0)), pl.BlockSpec((d, tn), lambda m, j: (0, j)),
                  pl.BlockSpec((d, LANES), lambda m, j: (0, 0))],
        out_specs=[pl.BlockSpec((tm, tn), lambda m, j: (m, j)), pl.BlockSpec((tm, LANES), lambda m, j: (m, 0))],
        out_shape=[jax.ShapeDtypeStruct((t, n), F32), jax.ShapeDtypeStruct((t, LANES), F32)],
        compiler_params=_params(2),
    )(u, w_main, w_alow)


def _outproj(y, w_out, x, g2, tm=512):
    t, d = x.shape
    tm = min(tm, t)
    k = y.shape[1]

    def body(y_ref, w_ref, x_ref, g_ref, x1_ref, h_ref):
        x1 = x_ref[...] + _dot(y_ref[...], w_ref[...])
        x1_ref[...] = x1
        r = lax.rsqrt(jnp.mean(x1 * x1, axis=-1, keepdims=True) + EPS)
        h_ref[...] = (x1 * r * g_ref[...]).astype(BF16)

    return pl.pallas_call(
        body, name="outproj_rmsnorm", grid=(t // tm,),
        in_specs=[pl.BlockSpec((tm, k), lambda m: (m, 0)), pl.BlockSpec((k, d), lambda m: (0, 0)),
                  pl.BlockSpec((tm, d), lambda m: (m, 0)), pl.BlockSpec((1, d), lambda m: (0, 0))],
        out_specs=[pl.BlockSpec((tm, d), lambda m: (m, 0)), pl.BlockSpec((tm, d), lambda m: (m, 0))],
        out_shape=[jax.ShapeDtypeStruct((t, d), F32), jax.ShapeDtypeStruct((t, d), BF16)],
        compiler_params=_params(1),
    )(y, w_out, x, g2)


def _ff1(h, w1g, tm=1024):
    t, d = h.shape
    tm = min(tm, t)
    g, _, f = w1g.shape

    def body(h_ref, w_ref, a_ref):
        a_ref[...] = _dot(h_ref[...], w_ref[...]).astype(BF16)

    return pl.pallas_call(
        body, name="ff1", grid=(t // tm, g),
        in_specs=[pl.BlockSpec((tm, d), lambda m, j: (m, 0)), pl.BlockSpec((None, d, f), lambda m, j: (j, 0, 0))],
        out_specs=pl.BlockSpec((tm, f), lambda m, j: (m, j)),
        out_shape=jax.ShapeDtypeStruct((t, g * f), BF16),
        compiler_params=_params(2),
    )(h, w1g)


def _ff2(a, w2, x1, tm=1024, tn=1024, tk=2048):
    t, f = a.shape
    tm = min(tm, t)
    d = w2.shape[1]

    def body(a_ref, w_ref, x1_ref, o_ref):
        part = _dot(_relu_sq(a_ref[...]), w_ref[...])

        @pl.when(pl.program_id(2) == 0)
        def _():
            o_ref[...] = x1_ref[...] + part

        @pl.when(pl.program_id(2) > 0)
        def _():
            o_ref[...] += part

    return pl.pallas_call(
        body, name="ff2_residual", grid=(t // tm, d // tn, f // tk),
        in_specs=[pl.BlockSpec((tm, tk), lambda m, j, kk: (m, kk)), pl.BlockSpec((tk, tn), lambda m, j, kk: (kk, j)),
                  pl.BlockSpec((tm, tn), lambda m, j, kk: (m, j))],
        out_specs=pl.BlockSpec((tm, tn), lambda m, j, kk: (m, j)),
        out_shape=jax.ShapeDtypeStruct((t, d), F32),
        compiler_params=_params(3),
    )(a, w2, x1)


def _dff2(dx2b, w2, a, tm=1024, tn=1024):
    t, d = dx2b.shape
    tm = min(tm, t)
    f = w2.shape[0]

    def body(g_ref, w_ref, a_ref, o_ref):
        dp = _dot(g_ref[...], w_ref[...], _NT)
        o_ref[...] = (dp * (2.0 * jnp.maximum(a_ref[...].astype(F32), 0.0))).astype(BF16)

    return pl.pallas_call(
        body, name="dff2", grid=(t // tm, f // tn),
        in_specs=[pl.BlockSpec((tm, d), lambda m, j: (m, 0)), pl.BlockSpec((tn, d), lambda m, j: (j, 0)),
                  pl.BlockSpec((tm, tn), lambda m, j: (m, j))],
        out_specs=pl.BlockSpec((tm, tn), lambda m, j: (m, j)),
        out_shape=jax.ShapeDtypeStruct((t, f), BF16),
        compiler_params=_params(2),
    )(dx2b, w2, a)


def _behind(token):
    if token is None:
        return [], []
    return [token], [pl.BlockSpec(token.shape, lambda *_: (0,) * token.ndim)]


def _accumulate(kk, nk, acc_ref, part, out_ref):
    if nk == 1:
        out_ref[...] = part.astype(out_ref.dtype)
        return

    @pl.when(kk == 0)
    def _():
        acc_ref[...] = part

    if nk > 2:
        @pl.when((kk > 0) & (kk < nk - 1))
        def _():
            acc_ref[...] += part

    @pl.when(kk == nk - 1)
    def _():
        out_ref[...] = (acc_ref[...] + part).astype(out_ref.dtype)


def _tn_matmul(name, a, b, grid, a_spec, b_spec, out_shape, out_spec, acc_shape, a_fn=None, behind=None):
    nk = grid[-1]
    dep_args, dep_specs = _behind(behind)

    def body(a_ref, b_ref, *rest):
        o_ref, acc_ref = rest[-2:]
        kk = pl.program_id(len(grid) - 1)
        av = a_ref[...]
        if a_fn is not None:
            av = a_fn(av)
        _accumulate(kk, nk, acc_ref, _dot(av, b_ref[...], _TN), o_ref)

    return pl.pallas_call(
        body, name=name, grid=grid, in_specs=[a_spec, b_spec] + dep_specs, out_specs=out_spec, out_shape=out_shape,
        scratch_shapes=[pltpu.VMEM(acc_shape, F32)], compiler_params=_params(len(grid)),
    )(a, b, *dep_args)


def _dw_in(u, dz, dzal, tk, tm=1024, tn=1024):
    t, d = u.shape
    n_main = dz.shape[1]
    nk = t // tk

    def body(a_ref, b_ref, al_ref, o_ref, oal_ref, acc_ref, accal_ref):
        j, kk = pl.program_id(1), pl.program_id(2)
        av = a_ref[...]
        _accumulate(kk, nk, acc_ref, _dot(av, b_ref[...], _TN), o_ref)

        @pl.when(j == 0)
        def _():
            _accumulate(kk, nk, accal_ref, _dot(av, al_ref[...], _TN), oal_ref)

    return pl.pallas_call(
        body, name="dw_in", grid=(d // tm, n_main // tn, nk),
        in_specs=[pl.BlockSpec((tk, tm), lambda m, j, kk: (kk, m)), pl.BlockSpec((tk, tn), lambda m, j, kk: (kk, j)),
                  pl.BlockSpec((tk, LANES), lambda m, j, kk: (kk, 0))],
        out_specs=[pl.BlockSpec((tm, tn), lambda m, j, kk: (m, j)), pl.BlockSpec((tm, LANES), lambda m, j, kk: (m, 0))],
        out_shape=[jax.ShapeDtypeStruct((d, n_main), BF16), jax.ShapeDtypeStruct((d, LANES), BF16)],
        scratch_shapes=[pltpu.VMEM((tm, tn), F32), pltpu.VMEM((tm, LANES), F32)],
        compiler_params=_params(3),
    )(u, dz, dzal)


class _SideAdamW:
    def __init__(self, side, grid):
        parts, w, m, v = side
        n_parts, r, c = parts.shape
        steps = 1
        for extent in grid:
            steps *= extent
        rows = r // steps
        assert rows * steps == r and rows % (2 * SUBLANES) == 0, (r, steps)

        def step(*ids):
            lin = ids[0]
            for extent, idx in zip(grid[1:], ids[1:]):
                lin = lin * extent + idx
            return lin

        slab = pl.BlockSpec((rows, c), lambda *ids: (step(*ids), 0))
        self.args = [parts, w, m, v]
        self.in_specs = [pl.BlockSpec((n_parts, rows, c), lambda *ids: (0, step(*ids), 0)), slab, slab, slab]
        self.out_specs = [slab] * 4
        self.out_shape = [jax.ShapeDtypeStruct((r, c), F32)] * 4
        self.n_parts = n_parts

    def run(self, in_refs, out_refs):
        p_ref, w_ref, m_ref, v_ref = in_refs
        g = p_ref[0].astype(F32)
        for j in range(1, self.n_parts):
            g = g + p_ref[j].astype(F32)
        out_refs[0][...] = g
        out_refs[1][...], out_refs[2][...], out_refs[3][...] = _adamw_math(g, w_ref[...], m_ref[...], v_ref[...])


def _dh(da, w1g, tm=1024, tn=1024, shards_per_step=4, behind=None):
    t = da.shape[0]
    tm = min(tm, t)
    g, d, f = w1g.shape
    sps = shards_per_step
    grid = (t // tm, d // tn, g // sps)
    dep_args, dep_specs = _behind(behind)

    def body(a_ref, w_ref, *rest):
        o_ref = rest[-1]
        acc = _dot(a_ref[:, 0:f], w_ref[0], _NT)
        for s in range(1, sps):
            acc = acc + _dot(a_ref[:, s * f:(s + 1) * f], w_ref[s], _NT)

        @pl.when(pl.program_id(2) == 0)
        def _():
            o_ref[...] = acc

        @pl.when(pl.program_id(2) > 0)
        def _():
            o_ref[...] += acc

    return pl.pallas_call(
        body, name="dh", grid=grid,
        in_specs=[pl.BlockSpec((tm, sps * f), lambda m, j, kk: (m, kk)),
                  pl.BlockSpec((sps, tn, f), lambda m, j, kk: (kk, j, 0))] + dep_specs,
        out_specs=pl.BlockSpec((tm, tn), lambda m, j, kk: (m, j)),
        out_shape=jax.ShapeDtypeStruct((t, d), F32),
        compiler_params=_params(3),
    )(da, w1g, *dep_args)


def _du(dz, w_main, dzal, w_alow, tm=1024, tn=1024, tk=3072, behind=None, side=None):
    t, n = dz.shape
    tm = min(tm, t)
    d = w_main.shape[0]
    grid = (t // tm, d // tn, n // tk)
    dep_args, dep_specs = _behind(behind)
    adams = [_SideAdamW(s, grid) for s in (side or [])]
    n_dep, n_side = len(dep_args), len(adams)

    def body(a_ref, w_ref, al_ref, wa_ref, *rest):
        o_ref = rest[n_dep + 4 * n_side]
        part = _dot(a_ref[...], w_ref[...], _NT)

        @pl.when(pl.program_id(2) == 0)
        def _():
            o_ref[...] = _dot(al_ref[...], wa_ref[...], _NT) + part

        @pl.when(pl.program_id(2) > 0)
        def _():
            o_ref[...] += part
        for k, adam in enumerate(adams):
            first_out = n_dep + 4 * n_side + 1 + 4 * k
            adam.run(rest[n_dep + 4 * k:n_dep + 4 * k + 4], rest[first_out:first_out + 4])

    outs = pl.pallas_call(
        body, name="du", grid=grid,
        in_specs=[pl.BlockSpec((tm, tk), lambda m, j, kk: (m, kk)), pl.BlockSpec((tn, tk), lambda m, j, kk: (j, kk)),
                  pl.BlockSpec((tm, LANES), lambda m, j, kk: (m, 0)), pl.BlockSpec((tn, LANES), lambda m, j, kk: (j, 0))]
        + dep_specs + [s for adam in adams for s in adam.in_specs],
        out_specs=[pl.BlockSpec((tm, tn), lambda m, j, kk: (m, j))] + [s for adam in adams for s in adam.out_specs],
        out_shape=[jax.ShapeDtypeStruct((t, d), F32)] + [s for adam in adams for s in adam.out_shape],
        compiler_params=_params(3),
    )(dz, w_main, dzal, w_alow, *dep_args, *[a for adam in adams for a in adam.args])
    return outs[0], [outs[1 + 4 * k:5 + 4 * k] for k in range(n_side)]


def _loss_head(x2, gf, tgt, tr=512):
    t, d = x2.shape
    tr = min(tr, t)

    def body(x_ref, g_ref, t_ref, dx_ref, dxb_ref, loss_ref, dg_ref):
        @pl.when(pl.program_id(0) == 0)
        def _():
            loss_ref[...] = jnp.zeros_like(loss_ref)
            dg_ref[...] = jnp.zeros_like(dg_ref)

        xf = x_ref[...]
        g = g_ref[...]
        r = lax.rsqrt(jnp.mean(xf * xf, axis=-1, keepdims=True) + EPS)
        xh = xf * r
        e = xh * g - t_ref[...]
        loss_ref[...] += 0.5 * jnp.sum(jnp.mean(e * e, axis=-1, keepdims=True))
        dy = e * (1.0 / d)
        dg_ref[...] += jnp.sum(dy * xh, axis=0, keepdims=True)
        dyg = dy * g
        dx = r * (dyg - xh * jnp.mean(dyg * xh, axis=-1, keepdims=True))
        dx_ref[...] = dx
        dxb_ref[...] = dx.astype(BF16)

    return pl.pallas_call(
        body, name="loss_head", grid=(t // tr,),
        in_specs=[pl.BlockSpec((tr, d), lambda i: (i, 0)), pl.BlockSpec((1, d), lambda i: (0, 0)),
                  pl.BlockSpec((tr, d), lambda i: (i, 0))],
        out_specs=[pl.BlockSpec((tr, d), lambda i: (i, 0)), pl.BlockSpec((tr, d), lambda i: (i, 0)),
                   pl.BlockSpec((SUBLANES, LANES), lambda i: (0, 0)), pl.BlockSpec((1, d), lambda i: (0, 0))],
        out_shape=[jax.ShapeDtypeStruct((t, d), F32), jax.ShapeDtypeStruct((t, d), BF16),
                   jax.ShapeDtypeStruct((SUBLANES, LANES), F32), jax.ShapeDtypeStruct((1, d), F32)],
        compiler_params=_params(1),
    )(x2, gf, tgt)


def _norm_bwd_dy(dh, x1, g2, dx2, w_out, tm=256):
    t, d = x1.shape
    k = w_out.shape[0]
    tm = min(tm, t)

    def body(dh_ref, x_ref, g_ref, dr_ref, w_ref, dx_ref, dxb_ref, dg_ref, dy_ref):
        @pl.when(pl.program_id(0) == 0)
        def _():
            dg_ref[...] = jnp.zeros_like(dg_ref)

        xf = x_ref[...]
        dhv = dh_ref[...]
        r = lax.rsqrt(jnp.mean(xf * xf, axis=-1, keepdims=True) + EPS)
        xh = xf * r
        dg_ref[...] += jnp.sum(dhv * xh, axis=0, keepdims=True)
        dyg = dhv * g_ref[...]
        dx = dr_ref[...] + r * (dyg - xh * jnp.mean(dyg * xh, axis=-1, keepdims=True))
        dx_ref[...] = dx
        dxb = dx.astype(BF16)
        dxb_ref[...] = dxb
        dy_ref[...] = _dot(dxb, w_ref[...], _NT)

    rows = pl.BlockSpec((tm, d), lambda i: (i, 0))
    vec = pl.BlockSpec((1, d), lambda i: (0, 0))
    return pl.pallas_call(
        body, name="norm2_bwd_dy", grid=(t // tm,),
        in_specs=[rows, rows, vec, rows, pl.BlockSpec((k, d), lambda i: (0, 0))],
        out_specs=[rows, rows, vec, pl.BlockSpec((tm, k), lambda i: (i, 0))],
        out_shape=[jax.ShapeDtypeStruct((t, d), F32), jax.ShapeDtypeStruct((t, d), BF16),
                   jax.ShapeDtypeStruct((1, d), F32), jax.ShapeDtypeStruct((t, k), F32)],
        compiler_params=_params(1),
    )(dh, x1, g2, dx2, w_out)


def _norm_bwd(name, dh, xin, g, dres, tr=512):
    t, d = xin.shape
    tr = min(tr, t)

    def body(dh_ref, x_ref, g_ref, dr_ref, dx_ref, dg_ref):
        @pl.when(pl.program_id(0) == 0)
        def _():
            dg_ref[...] = jnp.zeros_like(dg_ref)

        xf = x_ref[...]
        dhv = dh_ref[...]
        r = lax.rsqrt(jnp.mean(xf * xf, axis=-1, keepdims=True) + EPS)
        xh = xf * r
        dg_ref[...] += jnp.sum(dhv * xh, axis=0, keepdims=True)
        dyg = dhv * g_ref[...]
        dx = dr_ref[...] + r * (dyg - xh * jnp.mean(dyg * xh, axis=-1, keepdims=True))
        dx_ref[...] = dx

    rows = pl.BlockSpec((tr, d), lambda i: (i, 0))
    vec = pl.BlockSpec((1, d), lambda i: (0, 0))
    return pl.pallas_call(
        body, name=name, grid=(t // tr,),
        in_specs=[rows, rows, vec, rows], out_specs=[rows, vec],
        out_shape=[jax.ShapeDtypeStruct((t, d), F32), jax.ShapeDtypeStruct((1, d), F32)],
        compiler_params=_params(1),
    )(dh, xin, g, dres)


MIX_TILE = 256
CHUNKS_PER_TILE = MIX_TILE // CHUNK
CHUNK_SHIFT = CHUNK.bit_length() - 1
assert 1 << CHUNK_SHIFT == CHUNK


def _chunk_masks(n):
    row = lax.broadcasted_iota(jnp.int32, (n, n), 0)
    col = lax.broadcasted_iota(jnp.int32, (n, n), 1)
    same = lax.shift_right_logical(row, CHUNK_SHIFT) == lax.shift_right_logical(col, CHUNK_SHIFT)
    one = lambda m: jnp.where(m, 1.0, 0.0).astype(BF16)
    return jnp.concatenate([one(same & (col > row)), one(same)], axis=0), one(same & (col < row))


def _mask_dot(mask, x):
    hi = x.astype(BF16)
    r1 = x - hi.astype(F32)
    mid = r1.astype(BF16)
    lo = (r1 - mid.astype(F32)).astype(BF16)
    return _dot(mask, hi) + _dot(mask, mid) + _dot(mask, lo)


def _log_sigmoid(x):
    return jnp.minimum(x, 0.0) - jnp.log1p(jnp.exp(-jnp.abs(x)))


def _conv_taps(prev8, uc, w):
    ext = jnp.concatenate([prev8, uc], axis=0)
    s1 = pltpu.roll(ext, 1, 0)[SUBLANES:]
    s2 = pltpu.roll(ext, 2, 0)[SUBLANES:]
    return s2 * w[0:1] + s1 * w[1:2] + uc * w[2:3], s1, s2


def _z_specs(tile, idx):
    d_conv = 1024
    wide = lambda c: pl.BlockSpec((tile, d_conv), lambda i, c=c: (idx(i), c))
    half = lambda c: pl.BlockSpec((tile, d_conv // 2), lambda i, c=c: (idx(i), c))
    return [wide(0), wide(1), wide(2), half(6), half(7), wide(4), wide(5)]


def _mixer_fwd(z, alow, wgu, b_gate, convw, conv_g, gla_g):
    t = z.shape[0]
    tb, cpt = MIX_TILE, CHUNKS_PER_TILE
    d_conv = conv_g.shape[1]
    dv = gla_g.shape[1]
    dk = dv // 2
    gw = d_conv // CONV_GROUPS
    scale = dk ** -0.5

    def body(cb_ref, cc_ref, ch_ref, q_ref, k_ref, v_ref, og_ref, al_ref, wgu_ref, bg_ref, cw_ref, cg_ref, gg_ref,
             y_ref, sall_ref, carry_ref, s_ref):
        @pl.when(pl.program_id(0) == 0)
        def _():
            carry_ref[...] = jnp.zeros_like(carry_ref)
            s_ref[...] = jnp.zeros_like(s_ref)

        uc = cc_ref[...] * ch_ref[...]
        conv, _, _ = _conv_taps(carry_ref[...], uc, cw_ref[...])
        carry_ref[...] = uc[tb - SUBLANES:]
        ypre = cb_ref[...] * conv
        cg = cg_ref[...]
        for g in range(CONV_GROUPS):
            sl = slice(g * gw, (g + 1) * gw)
            seg = ypre[:, sl]
            r = lax.rsqrt(jnp.mean(seg * seg, axis=-1, keepdims=True) + EPS)
            y_ref[:, sl] = (seg * r * cg[:, sl]).astype(BF16)

        later_and_same, _ = _chunk_masks(tb)
        pre = _dot(al_ref[...].astype(BF16), wgu_ref[...]) + bg_ref[...]
        la = _log_sigmoid(pre) * (1.0 / GATE_NORMALIZER)
        sums = _mask_dot(later_and_same, la)
        e_dec = sums[:tb]
        dec_all = jnp.exp(sums[tb:])
        kdec = (k_ref[...] * jnp.exp(e_dec)).astype(BF16)
        qs = (q_ref[...] * scale).astype(BF16)
        vb = v_ref[...].astype(BF16)
        gg = gg_ref[...]
        rows = [slice(c * CHUNK, (c + 1) * CHUNK) for c in range(cpt)]
        ks = [slice(h * dk, (h + 1) * dk) for h in range(GLA_HEADS)]
        vs = [slice(h * dv, (h + 1) * dv) for h in range(GLA_HEADS)]
        kvt = [[_dot(vb[rows[c], vs[h]], kdec[rows[c], ks[h]], _TN) for h in range(GLA_HEADS)] for c in range(cpt)]
        state = [s_ref[h] for h in range(GLA_HEADS)]
        states = []
        for c in range(cpt):
            state = [state[h] * dec_all[c * CHUNK:c * CHUNK + 1, ks[h]] + kvt[c][h] for h in range(GLA_HEADS)]
            states.append(state)
            for h in range(GLA_HEADS):
                sall_ref[c, h] = state[h]
        for h in range(GLA_HEADS):
            s_ref[h] = state[h]
        for h in range(GLA_HEADS):
            o = jnp.concatenate(
                [_dot(qs[rows[c], ks[h]], states[c][h].astype(BF16), _NT) for c in range(cpt)], axis=0)
            ro = lax.rsqrt(jnp.mean(o * o, axis=-1, keepdims=True) + EPS)
            ogs = og_ref[:, vs[h]]
            yg = o * ro * gg * (ogs * jax.nn.sigmoid(ogs))
            y_ref[:, d_conv + h * dv:d_conv + (h + 1) * dv] = yg.astype(BF16)

    full = lambda shape: pl.BlockSpec(shape, lambda i: (0,) * len(shape))
    return pl.pallas_call(
        body, name="mixer_fwd", grid=(t // tb,),
        in_specs=_z_specs(tb, lambda i: i) + [
            pl.BlockSpec((tb, LANES), lambda i: (i, 0)), full(wgu.shape), full(b_gate.shape), full(convw.shape),
            full(conv_g.shape), full(gla_g.shape)],
        out_specs=[pl.BlockSpec((tb, d_conv + GLA_HEADS * dv), lambda i: (i, 0)),
                   pl.BlockSpec((cpt, GLA_HEADS, dv, dk), lambda i: (i, 0, 0, 0))],
        out_shape=[jax.ShapeDtypeStruct((t, d_conv + GLA_HEADS * dv), BF16),
                   jax.ShapeDtypeStruct((t // CHUNK, GLA_HEADS, dv, dk), F32)],
        scratch_shapes=[pltpu.VMEM((SUBLANES, d_conv), F32), pltpu.VMEM((GLA_HEADS, dv, dk), F32)],
        compiler_params=_params(1),
    )(z, z, z, z, z, z, z, alow, wgu, b_gate, convw, conv_g, gla_g)


def _mixer_bwd(z, alow, dy, sall, wgu, b_gate, convw, conv_g, gla_g, behind=None):
    t = z.shape[0]
    tb, cpt = MIX_TILE, CHUNKS_PER_TILE
    nt = t // tb
    d_conv = conv_g.shape[1]
    dv = gla_g.shape[1]
    dk = dv // 2
    d_k = GLA_HEADS * dk
    gw = d_conv // CONV_GROUPS
    scale = dk ** -0.5
    rev = lambda i: nt - 1 - i
    dep_args, dep_specs = _behind(behind)

    def body(cb_ref, cc_ref, ch_ref, q_ref, k_ref, v_ref, og_ref, ccp_ref, chp_ref, al_ref, dy_ref, sall_ref, sprev_ref,
             wgu_ref, bg_ref, cw_ref, cg_ref, gg_ref, *rest):
        dz_ref, dzal_ref, dcw_ref, dcg_ref, dgg_ref, dbg_ref, dwgu_ref, dcarry_ref, gd_ref = rest[-9:]
        i = pl.program_id(0)

        @pl.when(i == 0)
        def _():
            dcarry_ref[...] = jnp.zeros_like(dcarry_ref)
            gd_ref[...] = jnp.zeros_like(gd_ref)
            dcw_ref[...] = jnp.zeros_like(dcw_ref)
            dcg_ref[...] = jnp.zeros_like(dcg_ref)
            dgg_ref[...] = jnp.zeros_like(dgg_ref)
            dbg_ref[...] = jnp.zeros_like(dbg_ref)
            dwgu_ref[...] = jnp.zeros_like(dwgu_ref)

        first = rev(i) == 0

        cb, cc, ch = cb_ref[...], cc_ref[...], ch_ref[...]
        w = cw_ref[...]
        uc = cc * ch
        prev8 = jnp.where(first, 0.0, ccp_ref[...] * chp_ref[...])
        conv, s1, s2 = _conv_taps(prev8, uc, w)
        ypre = cb * conv
        cg = cg_ref[...]
        dypre_parts = []
        for g in range(CONV_GROUPS):
            sl = slice(g * gw, (g + 1) * gw)
            seg = ypre[:, sl]
            r = lax.rsqrt(jnp.mean(seg * seg, axis=-1, keepdims=True) + EPS)
            yn = seg * r
            dyc = dy_ref[:, sl]
            dcg_ref[:, sl] += jnp.sum(dyc * yn, axis=0, keepdims=True)
            dyn = dyc * cg[:, sl]
            dypre_parts.append(r * (dyn - yn * jnp.mean(dyn * yn, axis=-1, keepdims=True)))
        dypre = jnp.concatenate(dypre_parts, axis=1)
        dconv = dypre * cb
        dz_ref[:, 0:d_conv] = (dypre * conv).astype(BF16)
        dcw_ref[0:1] += jnp.sum(dconv * s2, axis=0, keepdims=True)
        dcw_ref[1:2] += jnp.sum(dconv * s1, axis=0, keepdims=True)
        dcw_ref[2:3] += jnp.sum(dconv * uc, axis=0, keepdims=True)
        ext = jnp.concatenate([dconv, dcarry_ref[...]], axis=0)
        f1 = pltpu.roll(ext, tb + SUBLANES - 1, 0)[:tb]
        f2 = pltpu.roll(ext, tb + SUBLANES - 2, 0)[:tb]
        dcarry_ref[...] = dconv[:SUBLANES]
        duc = dconv * w[2:3] + f1 * w[1:2] + f2 * w[0:1]
        dz_ref[:, d_conv:2 * d_conv] = (duc * ch).astype(BF16)
        dz_ref[:, 2 * d_conv:3 * d_conv] = (duc * cc).astype(BF16)

        q_off = 3 * d_conv
        k_off = q_off + d_k
        v_off = k_off + d_k
        og_off = v_off + GLA_HEADS * dv
        later_and_same, earlier = _chunk_masks(tb)
        alb = al_ref[...].astype(BF16)
        pre = _dot(alb, wgu_ref[...]) + bg_ref[...]
        la = _log_sigmoid(pre) * (1.0 / GATE_NORMALIZER)
        decays = jnp.exp(_mask_dot(later_and_same, la))
        exp_e, dec_all = decays[:tb], decays[tb:]
        kdec = k_ref[...] * exp_e
        kdec_b = kdec.astype(BF16)
        qs = (q_ref[...] * scale).astype(BF16)
        vb = v_ref[...].astype(BF16)
        gg = gg_ref[...]
        rows = [slice(c * CHUNK, (c + 1) * CHUNK) for c in range(cpt)]
        ks = [slice(h * dk, (h + 1) * dk) for h in range(GLA_HEADS)]
        vs = [slice(h * dv, (h + 1) * dv) for h in range(GLA_HEADS)]
        st_b = [[sall_ref[c, h].astype(BF16) for h in range(GLA_HEADS)] for c in range(cpt)]
        do_b = []
        dgg = jnp.zeros_like(gg)
        for h in range(GLA_HEADS):
            o = jnp.concatenate([_dot(qs[rows[c], ks[h]], st_b[c][h], _NT) for c in range(cpt)], axis=0)
            ro = lax.rsqrt(jnp.mean(o * o, axis=-1, keepdims=True) + EPS)
            on = o * ro
            ogs = og_ref[:, vs[h]]
            sg = jax.nn.sigmoid(ogs)
            gate = ogs * sg
            dyg = dy_ref[:, d_conv + h * dv:d_conv + (h + 1) * dv]
            dgg = dgg + jnp.sum(dyg * on * gate, axis=0, keepdims=True)
            dz_ref[:, og_off + h * dv:og_off + (h + 1) * dv] = (
                dyg * on * gg * (sg * (1.0 + ogs * (1.0 - sg)))).astype(BF16)
            don = dyg * gg * gate
            do_b.append((ro * (don - on * jnp.mean(don * on, axis=-1, keepdims=True))).astype(BF16))
        dgg_ref[...] += dgg
        for h in range(GLA_HEADS):
            dq = jnp.concatenate([_dot(do_b[h][rows[c]], st_b[c][h]) for c in range(cpt)], axis=0)
            dz_ref[:, q_off + h * dk:q_off + (h + 1) * dk] = (dq * scale).astype(BF16)
        own = [[_dot(do_b[h][rows[c]], qs[rows[c], ks[h]], _TN) for h in range(GLA_HEADS)] for c in range(cpt)]
        carried = [gd_ref[h] for h in range(GLA_HEADS)]
        gt_b = [None] * cpt
        ddd = [None] * cpt
        for c in reversed(range(cpt)):
            gt = [own[c][h] + carried[h] for h in range(GLA_HEADS)]
            dec = [dec_all[c * CHUNK:c * CHUNK + 1, ks[h]] for h in range(GLA_HEADS)]
            carried = [gt[h] * dec[h] for h in range(GLA_HEADS)]
            if c > 0:
                st_prev = [sall_ref[c - 1, h] for h in range(GLA_HEADS)]
            else:
                st_prev = [jnp.where(first, 0.0, sprev_ref[0, h]) for h in range(GLA_HEADS)]
            ddec = [jnp.sum(gt[h] * st_prev[h], axis=0, keepdims=True) * dec[h] for h in range(GLA_HEADS)]
            ddd[c] = jnp.broadcast_to(jnp.concatenate(ddec, axis=1), (CHUNK, d_k))
            gt_b[c] = [gt[h].astype(BF16) for h in range(GLA_HEADS)]
        for h in range(GLA_HEADS):
            gd_ref[h] = carried[h]
        dkdec_cols = []
        for h in range(GLA_HEADS):
            dvh = jnp.concatenate([_dot(kdec_b[rows[c], ks[h]], gt_b[c][h], _NT) for c in range(cpt)], axis=0)
            dz_ref[:, v_off + h * dv:v_off + (h + 1) * dv] = dvh.astype(BF16)
            dkdec_cols.append(jnp.concatenate([_dot(vb[rows[c], vs[h]], gt_b[c][h]) for c in range(cpt)], axis=0))
        dkdec = jnp.concatenate(dkdec_cols, axis=1)
        dz_ref[:, k_off:k_off + d_k] = (dkdec * exp_e).astype(BF16)
        dla = _mask_dot(earlier, dkdec * kdec) + jnp.concatenate(ddd, axis=0)
        dpre = dla * (1.0 / GATE_NORMALIZER) * jax.nn.sigmoid(-pre)
        dbg_ref[...] += jnp.sum(dpre, axis=0, keepdims=True)
        dpre_b = dpre.astype(BF16)
        dwgu_ref[...] += _dot(alb, dpre_b, _TN)
        dzal_ref[...] = _dot(dpre_b, wgu_ref[...], _NT).astype(BF16)

    full = lambda shape: pl.BlockSpec(shape, lambda i: (0,) * len(shape))
    prev_rows = lambda c: pl.BlockSpec(
        (SUBLANES, d_conv), lambda i, c=c: (jnp.maximum(rev(i) * (tb // SUBLANES) - 1, 0), c))
    n_z = 3 * d_conv + 2 * d_k + 2 * GLA_HEADS * dv
    return pl.pallas_call(
        body, name="mixer_bwd", grid=(nt,),
        in_specs=_z_specs(tb, rev) + [
            prev_rows(1), prev_rows(2),
            pl.BlockSpec((tb, LANES), lambda i: (rev(i), 0)),
            pl.BlockSpec((tb, d_conv + GLA_HEADS * dv), lambda i: (rev(i), 0)),
            pl.BlockSpec((cpt, GLA_HEADS, dv, dk), lambda i: (rev(i), 0, 0, 0)),
            pl.BlockSpec((1, GLA_HEADS, dv, dk), lambda i: (jnp.maximum(rev(i) * cpt - 1, 0), 0, 0, 0)),
            full(wgu.shape), full(b_gate.shape), full(convw.shape), full(conv_g.shape), full(gla_g.shape)]
        + dep_specs,
        out_specs=[pl.BlockSpec((tb, n_z), lambda i: (rev(i), 0)), pl.BlockSpec((tb, LANES), lambda i: (rev(i), 0)),
                   full(convw.shape), full(conv_g.shape), full(gla_g.shape), full(b_gate.shape), full(wgu.shape)],
        out_shape=[jax.ShapeDtypeStruct((t, n_z), BF16), jax.ShapeDtypeStruct((t, LANES), BF16),
                   jax.ShapeDtypeStruct(convw.shape, F32), jax.ShapeDtypeStruct(conv_g.shape, F32),
                   jax.ShapeDtypeStruct(gla_g.shape, F32), jax.ShapeDtypeStruct(b_gate.shape, F32),
                   jax.ShapeDtypeStruct(wgu.shape, F32)],
        scratch_shapes=[pltpu.VMEM((SUBLANES, d_conv), F32), pltpu.VMEM((GLA_HEADS, dv, dk), F32)],
        compiler_params=_params(1),
    )(z, z, z, z, z, z, z, z, z, alow, dy, sall, sall, wgu, b_gate, convw, conv_g, gla_g, *dep_args)


def _adamw_math(g, w, m, v):
    m = ADAM_B1 * m + (1.0 - ADAM_B1) * g
    v = ADAM_B2 * v + (1.0 - ADAM_B2) * (g * g)
    m_hat = m / (1.0 - ADAM_B1 ** ADAM_STEP)
    v_hat = v / (1.0 - ADAM_B2 ** ADAM_STEP)
    delta = -ADAM_LR * (m_hat / (jnp.sqrt(v_hat) + ADAM_EPS) + ADAM_WD * w)
    return delta, m, v


def _adamw(name, parts, w, m, v, tr):
    r, c = w.shape
    n_parts = parts.shape[0]

    def body(p_ref, w_ref, m_ref, v_ref, g_ref, d_ref, nm_ref, nv_ref):
        g = p_ref[0].astype(F32)
        for j in range(1, n_parts):
            g = g + p_ref[j].astype(F32)
        g_ref[...] = g
        d_ref[...], nm_ref[...], nv_ref[...] = _adamw_math(g, w_ref[...], m_ref[...], v_ref[...])

    blk = pl.BlockSpec((tr, c), lambda i: (i, 0))
    return pl.pallas_call(
        body, name=name, grid=(r // tr,),
        in_specs=[pl.BlockSpec((n_parts, tr, c), lambda i: (0, i, 0)), blk, blk, blk],
        out_specs=[blk] * 4, out_shape=[jax.ShapeDtypeStruct((r, c), F32)] * 4,
        compiler_params=_params(1),
    )(parts, w, m, v)


def _adamw_small(grads, ws, ms, vs):
    n = len(grads)

    def body(*refs):
        g, w, m, v = (refs[k * n:(k + 1) * n] for k in range(4))
        d_out, m_out, v_out = (refs[(4 + k) * n:(5 + k) * n] for k in range(3))
        for i in range(n):
            d_out[i][...], m_out[i][...], v_out[i][...] = _adamw_math(g[i][...], w[i][...], m[i][...], v[i][...])

    vmem = pl.BlockSpec(memory_space=pltpu.VMEM)
    outs = pl.pallas_call(
        body, name="adamw_small", out_shape=[jax.ShapeDtypeStruct(w.shape, F32) for w in ws] * 3,
        in_specs=[vmem] * (4 * n), out_specs=[vmem] * (3 * n),
    )(*grads, *ws, *ms, *vs)
    return [outs[:n], outs[n:2 * n], outs[2 * n:]]


def _sum_partials(parts):
    n_parts, rows, lanes = parts.shape

    def body(p_ref, o_ref):
        g = p_ref[0]
        for j in range(1, n_parts):
            g = g + p_ref[j]
        o_ref[...] = g

    return pl.pallas_call(
        body, name="sum_small_partials", out_shape=jax.ShapeDtypeStruct((rows, lanes), F32),
        in_specs=[pl.BlockSpec(memory_space=pltpu.VMEM)], out_specs=pl.BlockSpec(memory_space=pltpu.VMEM),
    )(parts)


def _pack_rows(vectors, rows):
    flat = jnp.concatenate([a.reshape(-1).astype(F32) for a in vectors])
    return jnp.pad(flat, (0, rows * LANES - flat.shape[0])).reshape(rows, LANES)


def _unpack_rows(block, shapes):
    flat = block.reshape(-1)
    out, off = [], 0
    for s in shapes:
        n = 1
        for dim in s:
            n *= dim
        out.append(flat[off:off + n].reshape(s))
        off += n
    return out


def kernel(x, norm1_g, w_in, w_gate_up, b_gate, conv_w, conv_norm_g, gla_norm_g, w_out, norm2_g, w_ff1, w_ff2, norm_f_g, loss_target, m_norm1_g, m_w_in, m_w_gate_up, m_b_gate, m_conv_w, m_conv_norm_g, m_gla_norm_g, m_w_out, m_norm2_g, m_w_ff1, m_w_ff2, m_norm_f_g, v_norm1_g, v_w_in, v_w_gate_up, v_b_gate, v_conv_w, v_conv_norm_g, v_gla_norm_g, v_w_out, v_norm2_g, v_w_ff1, v_w_ff2, v_norm_f_g):
    me = _device_index()
    x2d, tgt = x[0], loss_target[0]
    d = x2d.shape[1]
    d_in_shard = w_in.shape[2]
    n_main = N_DEV * d_in_shard - GATE_RANK
    d_conv = conv_norm_g.shape[1]
    d_k = b_gate.shape[1]
    d_ff = N_DEV * w_ff1.shape[2]
    wmv = dict(
        norm1_g=(norm1_g, m_norm1_g, v_norm1_g), w_in=(w_in, m_w_in, v_w_in),
        w_gate_up=(w_gate_up, m_w_gate_up, v_w_gate_up), b_gate=(b_gate, m_b_gate, v_b_gate),
        conv_w=(conv_w, m_conv_w, v_conv_w), conv_norm_g=(conv_norm_g, m_conv_norm_g, v_conv_norm_g),
        gla_norm_g=(gla_norm_g, m_gla_norm_g, v_gla_norm_g), w_out=(w_out, m_w_out, v_w_out),
        norm2_g=(norm2_g, m_norm2_g, v_norm2_g), w_ff1=(w_ff1, m_w_ff1, v_w_ff1), w_ff2=(w_ff2, m_w_ff2, v_w_ff2),
        norm_f_g=(norm_f_g, m_norm_f_g, v_norm_f_g))

    small_rows = 16
    first_level = (SIBLING,) + SAME_CORE_PEERS
    win_shard = w_in[0].astype(BF16)
    in_send, in_recv, in_src, in_land, token = _exchange_start(
        "all_gather_start_w_in", [win_shard], [_land_zone(win_shard)], scatter=False, masks=[first_level])
    _, wgu_t, cw_t, wout_t, w1_t, w2_t = lax.optimization_barrier((token, w_gate_up, conv_w, w_out, w_ff1, w_ff2))
    small_shard = _pack_rows([wgu_t[0], cw_t[0]], small_rows)
    shards = [small_shard, wout_t[0].astype(BF16), w1_t[0].astype(BF16), w2_t[0].astype(BF16)]
    ag_send, ag_recv, ag_src, ag_land, token = _exchange_start(
        "all_gather_start", shards, [_land_zone(s) for s in shards], scatter=False, behind=token)

    def gathered(k, name, after):
        return _exchange_wait(name, ag_send[k], ag_recv[k], ag_src[k], ag_land[k], after, scatter=False)

    u = _rmsnorm(x2d, norm1_g, behind=token)
    tied = lax.optimization_barrier((token, w_in, m_w_in, v_w_in))
    wmv["w_in"] = tuple(tied[1:])
    small_g = gathered(0, "all_gather_wait_small", [u] + [a[0] for a in wmv["w_in"]])
    win_level1 = _exchange_wait(
        "all_gather_wait_w_in", in_send[0], in_recv[0], in_src[0], in_land[0], small_g, scatter=False,
        masks=first_level)
    win_g = _forward_wait("all_gather_wait_w_in_forwarded", *_forward_start("all_gather_forward_w_in", win_level1))
    w_main, w_alow = _shards_to_columns(win_g, n_main)
    small_flat = small_g.reshape(N_DEV, -1)
    n_wgu = GATE_RANK * (d_k // N_DEV)
    wgu_full = small_flat[:, :n_wgu].reshape(N_DEV, GATE_RANK, d_k // N_DEV).transpose(1, 0, 2).reshape(GATE_RANK, d_k)
    conv_w_full = small_flat[:, n_wgu:n_wgu + (d_conv // N_DEV) * CONV_WIDTH].reshape(d_conv, CONV_WIDTH)
    wgu_pad = jnp.pad(wgu_full, ((0, LANES - GATE_RANK), (0, 0))).astype(BF16)
    convw_taps = jnp.pad(conv_w_full.T, ((0, SUBLANES - CONV_WIDTH), (0, 0)))

    get_w_out = lambda after: gathered(1, "all_gather_wait_w_out", after).reshape(-1, d)
    get_w1 = lambda after: gathered(2, "all_gather_wait_w_ff1", after)
    get_w2 = lambda after: gathered(3, "all_gather_wait_w_ff2", after).reshape(d_ff, d)

    in_flight = {}

    def send_partials(name, parts):
        own = lax.dynamic_index_in_dim(parts, me, axis=0, keepdims=False)
        send, recv, src, land, token = _exchange_start("scatter_start_" + name, [parts], [_land_zone(own)], scatter=True)
        in_flight[name] = (send[0], recv[0], src[0], land[0])
        return token

    def on_grad(name, value):
        if name == "w_in":
            main, alow_part = value
            value = _columns_to_shards(main, alow_part, N_DEV, d_in_shard)
        elif name in ("w_out", "w_ff2"):
            value = value.reshape(N_DEV, -1, d)
        return send_partials(name, value)

    def received(name, after):
        send, recv, src, land = in_flight[name]
        return _exchange_wait("scatter_wait_" + name, send, recv, src, land, after, scatter=True)

    def side_for(name, after):
        return (received(name, after),) + tuple(a[0] for a in wmv[name])

    grads = _local_step(x2d, u, tgt, norm1_g, w_main, w_alow, wgu_pad, b_gate, convw_taps, conv_norm_g, gla_norm_g,
                        norm2_g, norm_f_g, get_w_out, get_w1, get_w2, on_grad, side_for)
    grad_x = grads["x"]

    small_shapes = [(1, d), (1, d_k), (1, d_conv), (1, gla_norm_g.shape[1]), (1, d), (d,),
                    (GATE_RANK, d_k), (d_conv, CONV_WIDTH), (1,)]
    small_grad_rows = 152
    small_part = _pack_rows(
        [grads["norm1_g"], grads["b_gate"], grads["conv_norm_g"], grads["gla_norm_g"], grads["norm2_g"],
         grads["norm_f_g"], grads["w_gate_up"][:GATE_RANK], grads["conv_w"][:CONV_WIDTH].T, grads["loss"][0, 0]],
        small_grad_rows)
    small_token = send_partials("small", jnp.broadcast_to(small_part[None], (N_DEV, small_grad_rows, LANES)))

    gin_r, gout_r = (received(nm, [grad_x, small_token]) for nm in ("w_in", "w_out"))
    get_small = lambda after: received("small", after)
    done = {"w_ff1": grads["adam_w_ff1"], "w_ff2": grads["adam_w_ff2"]}
    return _update(me, gin_r, gout_r, done, get_small, small_shapes, grad_x, wmv)


def _local_step(x2d, u, tgt, norm1_g, w_main, w_alow, wgu_pad, b_gate, convw_taps, conv_norm_g, gla_norm_g,
                norm2_g, norm_f_g, get_w_out, get_w1, get_w2, on_grad, side_for=lambda name, after: None):
    t, d = x2d.shape

    z, alow = _inproj(u, w_main, w_alow)
    y, sall = _mixer_fwd(z, alow, wgu_pad, b_gate, convw_taps, conv_norm_g, gla_norm_g)
    w_out_full = get_w_out(y)
    x1, h = _outproj(y, w_out_full, x2d, norm2_g)
    w1g = get_w1(h)
    a = _ff1(h, w1g)
    w2_full = get_w2(a)
    d_ff = w2_full.shape[0]
    x2 = _ff2(a, w2_full, x1)
    dx2, dx2b, loss_part, d_normf = _loss_head(x2, norm_f_g.reshape(1, d), tgt)

    tk = min(4096, t)
    nk = t // tk
    da = _dff2(dx2b, w2_full, a)
    dw2 = _tn_matmul(
        "dw_ff2", a, dx2b, (d_ff // 1024, d // 1024, nk),
        pl.BlockSpec((tk, 1024), lambda m, j, kk: (kk, m)), pl.BlockSpec((tk, 1024), lambda m, j, kk: (kk, j)),
        jax.ShapeDtypeStruct((d_ff, d), BF16), pl.BlockSpec((1024, 1024), lambda m, j, kk: (m, j)), (1024, 1024),
        a_fn=_relu_sq)
    token = on_grad("w_ff2", dw2)
    f_shard = d_ff // N_DEV
    dw1 = _tn_matmul(
        "dw_ff1", h, da, (N_DEV, d // 1024, nk),
        pl.BlockSpec((tk, 1024), lambda g, m, kk: (kk, m)), pl.BlockSpec((tk, f_shard), lambda g, m, kk: (kk, g)),
        jax.ShapeDtypeStruct((N_DEV, d, f_shard), BF16), pl.BlockSpec((None, 1024, f_shard), lambda g, m, kk: (g, m, 0)),
        (1024, f_shard), behind=token)
    token = on_grad("w_ff1", dw1)
    dh = _dh(da, w1g, behind=token)
    dx1, dx1b, d_norm2, dy = _norm_bwd_dy(dh, x1, norm2_g, dx2, w_out_full)
    dwout = _tn_matmul(
        "dw_out", y, dx1b, (d // 1024, d // 1024, nk),
        pl.BlockSpec((tk, 1024), lambda m, j, kk: (kk, m)), pl.BlockSpec((tk, 1024), lambda m, j, kk: (kk, j)),
        jax.ShapeDtypeStruct((d, d), BF16), pl.BlockSpec((1024, 1024), lambda m, j, kk: (m, j)), (1024, 1024))
    token = on_grad("w_out", dwout)
    dz, dzal, d_convw, d_convg, d_glag, d_bgate, d_wgu = _mixer_bwd(
        z, alow, dy, sall, wgu_pad, b_gate, convw_taps, conv_norm_g, gla_norm_g, behind=token)
    token = on_grad("w_in", _dw_in(u, dz, dzal, tk))
    sides = [s for s in (side_for("w_ff2", token), side_for("w_ff1", token)) if s is not None]
    du, adam = _du(dz, w_main, dzal, w_alow, behind=token, side=sides)
    adam_ff2, adam_ff1 = adam if adam else (None, None)
    grad_x, d_norm1 = _norm_bwd("norm1_bwd", du, x2d, norm1_g, dx1)
    return dict(x=grad_x, loss=loss_part, adam_w_ff2=adam_ff2, adam_w_ff1=adam_ff1,
                norm1_g=d_norm1, w_gate_up=d_wgu, b_gate=d_bgate, conv_w=d_convw,
                conv_norm_g=d_convg, gla_norm_g=d_glag, norm2_g=d_norm2, norm_f_g=d_normf)


_WEIGHT_ORDER = ("norm1_g", "w_in", "w_gate_up", "b_gate", "conv_w", "conv_norm_g", "gla_norm_g", "w_out", "norm2_g",
                 "w_ff1", "w_ff2", "norm_f_g")
_SMALL_ORDER = ("norm1_g", "b_gate", "conv_norm_g", "gla_norm_g", "norm2_g", "norm_f_g", "w_gate_up", "conv_w")
def _update(me, gin_r, gout_r, done, get_small, small_shapes, grad_x, wmv):
    big = dict(done)
    big["w_in"] = _adamw("adamw_w_in", gin_r, *(a[0] for a in wmv["w_in"]), 256)
    big["w_out"] = _adamw("adamw_w_out", gout_r, *(a[0] for a in wmv["w_out"]), 128)

    wgu_cols = wmv["w_gate_up"][0].shape[2]
    cw_rows = wmv["conv_w"][0].shape[1]

    small_r = get_small([big[nm][3] for nm in ("w_in", "w_out")])
    summed = _unpack_rows(_sum_partials(small_r), small_shapes)
    summed[6] = lax.dynamic_slice_in_dim(summed[6], me * wgu_cols, wgu_cols, axis=1)
    summed[7] = lax.dynamic_slice_in_dim(summed[7], me * cw_rows, cw_rows, axis=0)
    as_2d = lambda a: a.reshape((1, -1) if a.ndim == 1 else a.shape[-2:])
    grads_2d = [as_2d(g) for g in summed[:len(_SMALL_ORDER)]]
    small = _adamw_small(grads_2d, *[[as_2d(wmv[nm][k]) for nm in _SMALL_ORDER] for k in range(3)])
    small = [grads_2d] + small

    outs = []
    for k in range(4):
        for nm in _WEIGHT_ORDER:
            if nm in big:
                outs.append(big[nm][k][None])
            else:
                outs.append(small[k][_SMALL_ORDER.index(nm)].reshape(wmv[nm][0].shape))
    loss = summed[8][0]
    return (loss, grad_x[None], *outs)
```

```python
import jax
import jax.numpy as jnp
from jax import lax
from jax.experimental import pallas as pl
from jax.experimental.pallas import tpu as pltpu

F32 = jnp.float32
BF16 = jnp.bfloat16

N_DEV = 8
CHUNK = 64
GLA_HEADS = 4
CONV_GROUPS = 8
CONV_WIDTH = 3
GATE_RANK = 16
GATE_NORMALIZER = 16.0
EPS = 1e-6
ADAM_LR = 0.001
ADAM_B1 = 0.9
ADAM_B2 = 0.999
ADAM_EPS = 1e-08
ADAM_WD = 0.01
ADAM_STEP = 10

LANES = 128
SUBLANES = 8
VMEM_LIMIT = 56 << 20

_NN = (((1,), (0,)), ((), ()))
_NT = (((1,), (1,)), ((), ()))
_TN = (((0,), (0,)), ((), ()))


def _dot(a, b, dims=_NN):
    return lax.dot_general(a, b, dims, preferred_element_type=F32)


def _params(n_grid):
    return pltpu.CompilerParams(dimension_semantics=("arbitrary",) * n_grid, vmem_limit_bytes=VMEM_LIMIT)


def _relu_sq(a):
    r = jnp.maximum(a, 0.0)
    return r * r


def _device_index():
    return 4 * lax.axis_index("x") + 2 * lax.axis_index("y") + lax.axis_index("c")


def _peer(mask):
    x, y, c = lax.axis_index("x"), lax.axis_index("y"), lax.axis_index("c")
    return (x ^ ((mask >> 2) & 1), y ^ ((mask >> 1) & 1), c ^ (mask & 1))


_HBM_SPEC = pl.BlockSpec(memory_space=pltpu.HBM)
_SEM_SPEC = pl.BlockSpec(memory_space=pltpu.SEMAPHORE)
_SIDE_EFFECT = pltpu.SideEffectType.DATAFLOW_SIDE_EFFECTING
N_PEERS = N_DEV - 1


def _exchange_copy(src_ref, land_ref, send_sems, recv_sems, mask, scatter, arriving):
    me = _device_index()
    src = src_ref.at[me ^ mask] if scatter else src_ref
    dst = land_ref.at[(me ^ mask) if arriving else me]
    return pltpu.make_async_remote_copy(
        src_ref=src, dst_ref=dst, send_sem=send_sems.at[mask - 1], recv_sem=recv_sems.at[mask - 1],
        device_id=_peer(mask), device_id_type=pl.DeviceIdType.MESH)


def _land_zone(own):
    zone = lax.empty((N_DEV,) + own.shape, own.dtype)
    return lax.dynamic_update_slice(zone, own[None], (_device_index(),) + (0,) * own.ndim)


ALL_PEERS = tuple(range(1, N_DEV))
SIBLING = 1
SAME_CORE_PEERS = (2, 4, 6)


def _exchange_start(name, srcs, lands, scatter, masks=None, behind=None):
    n = len(srcs)
    masks = masks or [ALL_PEERS] * n
    dep_args = [] if behind is None else [behind]

    def body(*refs):
        src, land = refs[:n], refs[n:2 * n]
        outs = refs[2 * n + len(dep_args):]
        send_sems, recv_sems = outs[:n], outs[n:2 * n]
        token = refs[-1]
        for a in range(n):
            for mask in masks[a]:
                _exchange_copy(src[a], land[a], send_sems[a], recv_sems[a], mask, scatter, False).start()
        token[...] = jnp.zeros_like(token)

    hbm = lambda a: pltpu.HBM(a.shape, a.dtype)
    outs = pl.pallas_call(
        body, name=name,
        out_shape=([pltpu.SemaphoreType.DMA((N_PEERS,))] * (2 * n) + [hbm(a) for a in srcs] + [hbm(a) for a in lands]
                   + [jax.ShapeDtypeStruct((SUBLANES, LANES), F32)]),
        in_specs=[_HBM_SPEC] * (2 * n) + [pl.BlockSpec(memory_space=pl.ANY)] * len(dep_args),
        out_specs=[_SEM_SPEC] * (2 * n) + [_HBM_SPEC] * (2 * n) + [pl.BlockSpec(memory_space=pltpu.VMEM)],
        input_output_aliases={a: 2 * n + a for a in range(2 * n)},
        compiler_params=pltpu.CompilerParams(has_side_effects=_SIDE_EFFECT),
    )(*[pltpu.with_memory_space_constraint(a, pltpu.HBM) for a in list(srcs) + list(lands)], *dep_args)
    send_sems, recv_sems = outs[:n], outs[n:2 * n]
    src_thru, land_thru = outs[2 * n:3 * n], outs[3 * n:4 * n]
    return send_sems, recv_sems, src_thru, land_thru, outs[-1]


def _exchange_wait(name, send_sems, recv_sems, src_thru, land_thru, after, scatter, masks=ALL_PEERS):
    after = list(after) if isinstance(after, (list, tuple)) else [after]

    def body(src_ref, land_ref, send_ref, recv_ref, *rest):
        for mask in masks:
            cp = _exchange_copy(src_ref, land_ref, send_ref, recv_ref, mask, scatter, True)
            cp.wait_send()
            cp.wait_recv()

    return pl.pallas_call(
        body, name=name,
        out_shape=(pltpu.HBM(src_thru.shape, src_thru.dtype), pltpu.HBM(land_thru.shape, land_thru.dtype)),
        in_specs=[_HBM_SPEC, _HBM_SPEC, _SEM_SPEC, _SEM_SPEC] + [pl.BlockSpec(memory_space=pl.ANY)] * len(after),
        out_specs=(_HBM_SPEC, _HBM_SPEC), input_output_aliases={0: 0, 1: 1},
        compiler_params=pltpu.CompilerParams(has_side_effects=_SIDE_EFFECT),
    )(src_thru, land_thru, send_sems, recv_sems, *after)[1]


def _forward_copy(land_ref, send_sems, recv_sems, k, arriving):
    me = _device_index()
    slot = me ^ SAME_CORE_PEERS[k]
    return pltpu.make_async_remote_copy(
        src_ref=land_ref.at[slot], dst_ref=land_ref.at[(slot ^ SIBLING) if arriving else slot],
        send_sem=send_sems.at[k], recv_sem=recv_sems.at[k],
        device_id=_peer(SIBLING), device_id_type=pl.DeviceIdType.MESH)


def _forward_start(name, land):
    n_fwd = len(SAME_CORE_PEERS)

    def body(land_ref, send_sems, recv_sems, land_thru):
        for k in range(n_fwd):
            _forward_copy(land_ref, send_sems, recv_sems, k, False).start()

    send, recv, thru = pl.pallas_call(
        body, name=name,
        out_shape=[pltpu.SemaphoreType.DMA((n_fwd,)), pltpu.SemaphoreType.DMA((n_fwd,)), pltpu.HBM(land.shape, land.dtype)],
        in_specs=[_HBM_SPEC], out_specs=[_SEM_SPEC, _SEM_SPEC, _HBM_SPEC], input_output_aliases={0: 2},
        compiler_params=pltpu.CompilerParams(has_side_effects=_SIDE_EFFECT),
    )(pltpu.with_memory_space_constraint(land, pltpu.HBM))
    return send, recv, thru


def _forward_wait(name, send_sems, recv_sems, land_thru):
    def body(land_ref, send_ref, recv_ref, got_ref):
        for k in range(len(SAME_CORE_PEERS)):
            cp = _forward_copy(land_ref, send_ref, recv_ref, k, True)
            cp.wait_send()
            cp.wait_recv()

    return pl.pallas_call(
        body, name=name, out_shape=pltpu.HBM(land_thru.shape, land_thru.dtype),
        in_specs=[_HBM_SPEC, _SEM_SPEC, _SEM_SPEC], out_specs=_HBM_SPEC, input_output_aliases={0: 0},
        compiler_params=pltpu.CompilerParams(has_side_effects=_SIDE_EFFECT),
    )(land_thru, send_sems, recv_sems)


def _shards_to_columns(g, n_main, tr=512):
    n_dev, d, s = g.shape

    def body(g_ref, main_ref, rest_ref):
        for j in range(n_dev):
            lo, hi = j * s, (j + 1) * s
            if hi <= n_main:
                main_ref[:, lo:hi] = g_ref[j]
            else:
                main_ref[:, lo:n_main] = g_ref[j, :, 0:n_main - lo]
                rest_ref[...] = jnp.zeros_like(rest_ref)
                rest_ref[:, 0:hi - n_main] = g_ref[j, :, n_main - lo:s]

    return pl.pallas_call(
        body, grid=(d // tr,), name="shards_to_columns",
        in_specs=[pl.BlockSpec((n_dev, tr, s), lambda i: (0, i, 0))],
        out_specs=[pl.BlockSpec((tr, n_main), lambda i: (i, 0)), pl.BlockSpec((tr, LANES), lambda i: (i, 0))],
        out_shape=[jax.ShapeDtypeStruct((d, n_main), g.dtype), jax.ShapeDtypeStruct((d, LANES), g.dtype)],
        compiler_params=_params(1),
    )(g)


def _columns_to_shards(main, rest, n_dev, s, tr=512):
    d, n_main = main.shape
    assert (n_dev - 1) * s <= n_main < n_dev * s

    def body(main_ref, rest_ref, o_ref):
        for j in range(n_dev):
            lo, hi = j * s, (j + 1) * s
            if hi <= n_main:
                o_ref[j] = main_ref[:, lo:hi]
            else:
                o_ref[j, :, 0:n_main - lo] = main_ref[:, lo:n_main]
                o_ref[j, :, n_main - lo:s] = rest_ref[:, 0:hi - n_main]

    return pl.pallas_call(
        body, grid=(d // tr,), name="columns_to_shards",
        in_specs=[pl.BlockSpec((tr, n_main), lambda i: (i, 0)), pl.BlockSpec((tr, LANES), lambda i: (i, 0))],
        out_specs=pl.BlockSpec((n_dev, tr, s), lambda i: (0, i, 0)),
        out_shape=jax.ShapeDtypeStruct((n_dev, d, s), main.dtype),
        compiler_params=_params(1),
    )(main, rest)


def _rmsnorm(x, g, tr=512, behind=None):
    t, d = x.shape
    tr = min(tr, t)
    dep_args, dep_specs = _behind(behind)

    def body(x_ref, g_ref, *rest):
        u_ref = rest[-1]
        xf = x_ref[...]
        r = lax.rsqrt(jnp.mean(xf * xf, axis=-1, keepdims=True) + EPS)
        u_ref[...] = (xf * r * g_ref[...]).astype(BF16)

    return pl.pallas_call(
        body, name="rmsnorm1", grid=(t // tr,),
        in_specs=[pl.BlockSpec((tr, d), lambda i: (i, 0)), pl.BlockSpec((1, d), lambda i: (0, 0))] + dep_specs,
        out_specs=pl.BlockSpec((tr, d), lambda i: (i, 0)),
        out_shape=jax.ShapeDtypeStruct((t, d), BF16),
        compiler_params=_params(1),
    )(x, g, *dep_args)


def _inproj(u, w_main, w_alow, tm=1024, tn=1024):
    t, d = u.shape
    tm = min(tm, t)
    n = w_main.shape[1]

    def body(u_ref, w_ref, wa_ref, z_ref, al_ref):
        @pl.when(pl.program_id(1) == 0)
        def _():
            al_ref[...] = _dot(u_ref[...], wa_ref[...])

        z_ref[...] = _dot(u_ref[...], w_ref[...])

    return pl.pallas_call(
        body, name="inproj", grid=(t // tm, n // tn),
        in_specs=[pl.BlockSpec((tm, d), lambda m, j: (m, 0)), pl.BlockSpec((d, tn), lambda m, j: (0, j)),
                  pl.BlockSpec((d, LANES), lambda m, j: (0, 0))],
        out_specs=[pl.BlockSpec((tm, tn), lambda m, j: (m, j)), pl.BlockSpec((tm, LANES), lambda m, j: (m, 0))],
        out_shape=[jax.ShapeDtypeStruct((t, n), F32), jax.ShapeDtypeStruct((t, LANES), F32)],
        compiler_params=_params(2),
    )(u, w_main, w_alow)


def _outproj(y, w_out, x, g2, tm=512):
    t, d = x.shape
    tm = min(tm, t)
    k = y.shape[1]

    def body(y_ref, w_ref, x_ref, g_ref, x1_ref, h_ref):
        x1 = x_ref[...] + _dot(y_ref[...], w_ref[...])
        x1_ref[...] = x1
        r = lax.rsqrt(jnp.mean(x1 * x1, axis=-1, keepdims=True) + EPS)
        h_ref[...] = (x1 * r * g_ref[...]).astype(BF16)

    return pl.pallas_call(
        body, name="outproj_rmsnorm", grid=(t // tm,),
        in_specs=[pl.BlockSpec((tm, k), lambda m: (m, 0)), pl.BlockSpec((k, d), lambda m: (0, 0)),
                  pl.BlockSpec((tm, d), lambda m: (m, 0)), pl.BlockSpec((1, d), lambda m: (0, 0))],
        out_specs=[pl.BlockSpec((tm, d), lambda m: (m, 0)), pl.BlockSpec((tm, d), lambda m: (m, 0))],
        out_shape=[jax.ShapeDtypeStruct((t, d), F32), jax.ShapeDtypeStruct((t, d), BF16)],
        compiler_params=_params(1),
    )(y, w_out, x, g2)


def _ff1(h, w1g, tm=1024):
    t, d = h.shape
    tm = min(tm, t)
    g, _, f = w1g.shape

    def body(h_ref, w_ref, a_ref):
        a_ref[...] = _dot(h_ref[...], w_ref[...]).astype(BF16)

    return pl.pallas_call(
        body, name="ff1", grid=(t // tm, g),
        in_specs=[pl.BlockSpec((tm, d), lambda m, j: (m, 0)), pl.BlockSpec((None, d, f), lambda m, j: (j, 0, 0))],
        out_specs=pl.BlockSpec((tm, f), lambda m, j: (m, j)),
        out_shape=jax.ShapeDtypeStruct((t, g * f), BF16),
        compiler_params=_params(2),
    )(h, w1g)


def _ff2(a, w2, x1, tm=1024, tn=1024, tk=2048):
    t, f = a.shape
    tm = min(tm, t)
    d = w2.shape[1]

    def body(a_ref, w_ref, x1_ref, o_ref):
        @pl.when(pl.program_id(2) == 0)
        def _():
            o_ref[...] = x1_ref[...]

        o_ref[...] += _dot(_relu_sq(a_ref[...]), w_ref[...])

    return pl.pallas_call(
        body, name="ff2_residual", grid=(t // tm, d // tn, f // tk),
        in_specs=[pl.BlockSpec((tm, tk), lambda m, j, kk: (m, kk)), pl.BlockSpec((tk, tn), lambda m, j, kk: (kk, j)),
                  pl.BlockSpec((tm, tn), lambda m, j, kk: (m, j))],
        out_specs=pl.BlockSpec((tm, tn), lambda m, j, kk: (m, j)),
        out_shape=jax.ShapeDtypeStruct((t, d), F32),
        compiler_params=_params(3),
    )(a, w2, x1)


def _dff2(dx2b, w2, a, tm=1024, tn=1024):
    t, d = dx2b.shape
    tm = min(tm, t)
    f = w2.shape[0]

    def body(g_ref, w_ref, a_ref, o_ref):
        dp = _dot(g_ref[...], w_ref[...], _NT)
        o_ref[...] = (dp * (2.0 * jnp.maximum(a_ref[...].astype(F32), 0.0))).astype(BF16)

    return pl.pallas_call(
        body, name="dff2", grid=(t // tm, f // tn),
        in_specs=[pl.BlockSpec((tm, d), lambda m, j: (m, 0)), pl.BlockSpec((tn, d), lambda m, j: (j, 0)),
                  pl.BlockSpec((tm, tn), lambda m, j: (m, j))],
        out_specs=pl.BlockSpec((tm, tn), lambda m, j: (m, j)),
        out_shape=jax.ShapeDtypeStruct((t, f), BF16),
        compiler_params=_params(2),
    )(dx2b, w2, a)


def _behind(token):
    if token is None:
        return [], []
    return [token], [pl.BlockSpec(token.shape, lambda *_: (0,) * token.ndim)]


def _accumulate(kk, nk, acc_ref, part, out_ref):
    if nk == 1:
        out_ref[...] = part.astype(out_ref.dtype)
        return

    @pl.when(kk == 0)
    def _():
        acc_ref[...] = part

    if nk > 2:
        @pl.when((kk > 0) & (kk < nk - 1))
        def _():
            acc_ref[...] += part

    @pl.when(kk == nk - 1)
    def _():
        out_ref[...] = (acc_ref[...] + part).astype(out_ref.dtype)


def _tn_matmul(name, a, b, grid, a_spec, b_spec, out_shape, out_spec, acc_shape, a_fn=None, behind=None):
    nk = grid[-1]
    dep_args, dep_specs = _behind(behind)

    def body(a_ref, b_ref, *rest):
        o_ref, acc_ref = rest[-2:]
        kk = pl.program_id(len(grid) - 1)
        av = a_ref[...]
        if a_fn is not None:
            av = a_fn(av)
        _accumulate(kk, nk, acc_ref, _dot(av, b_ref[...], _TN), o_ref)

    return pl.pallas_call(
        body, name=name, grid=grid, in_specs=[a_spec, b_spec] + dep_specs, out_specs=out_spec, out_shape=out_shape,
        scratch_shapes=[pltpu.VMEM(acc_shape, F32)], compiler_params=_params(len(grid)),
    )(a, b, *dep_args)


def _dw_in(u, dz, dzal, tk, tm=1024, tn=1024):
    t, d = u.shape
    n_main = dz.shape[1]
    nk = t // tk

    def body(a_ref, b_ref, al_ref, o_ref, oal_ref, acc_ref, accal_ref):
        j, kk = pl.program_id(1), pl.program_id(2)
        av = a_ref[...]
        _accumulate(kk, nk, acc_ref, _dot(av, b_ref[...], _TN), o_ref)

        @pl.when(j == 0)
        def _():
            _accumulate(kk, nk, accal_ref, _dot(av, al_ref[...], _TN), oal_ref)

    return pl.pallas_call(
        body, name="dw_in", grid=(d // tm, n_main // tn, nk),
        in_specs=[pl.BlockSpec((tk, tm), lambda m, j, kk: (kk, m)), pl.BlockSpec((tk, tn), lambda m, j, kk: (kk, j)),
                  pl.BlockSpec((tk, LANES), lambda m, j, kk: (kk, 0))],
        out_specs=[pl.BlockSpec((tm, tn), lambda m, j, kk: (m, j)), pl.BlockSpec((tm, LANES), lambda m, j, kk: (m, 0))],
        out_shape=[jax.ShapeDtypeStruct((d, n_main), BF16), jax.ShapeDtypeStruct((d, LANES), BF16)],
        scratch_shapes=[pltpu.VMEM((tm, tn), F32), pltpu.VMEM((tm, LANES), F32)],
        compiler_params=_params(3),
    )(u, dz, dzal)


class _SideAdamW:
    def __init__(self, side, grid):
        parts, w, m, v = side
        n_parts, r, c = parts.shape
        steps = 1
        for extent in grid:
            steps *= extent
        rows = r // steps
        assert rows * steps == r and rows % (2 * SUBLANES) == 0, (r, steps)

        def step(*ids):
            lin = ids[0]
            for extent, idx in zip(grid[1:], ids[1:]):
                lin = lin * extent + idx
            return lin

        slab = pl.BlockSpec((rows, c), lambda *ids: (step(*ids), 0))
        self.args = [parts, w, m, v]
        self.in_specs = [pl.BlockSpec((n_parts, rows, c), lambda *ids: (0, step(*ids), 0)), slab, slab, slab]
        self.out_specs = [slab] * 4
        self.out_shape = [jax.ShapeDtypeStruct((r, c), F32)] * 4
        self.n_parts = n_parts

    def run(self, in_refs, out_refs):
        p_ref, w_ref, m_ref, v_ref = in_refs
        g = p_ref[0].astype(F32)
        for j in range(1, self.n_parts):
            g = g + p_ref[j].astype(F32)
        out_refs[0][...] = g
        out_refs[1][...], out_refs[2][...], out_refs[3][...] = _adamw_math(g, w_ref[...], m_ref[...], v_ref[...])


def _dh(da, w1g, tm=1024, tn=1024, shards_per_step=4, behind=None):
    t = da.shape[0]
    tm = min(tm, t)
    g, d, f = w1g.shape
    sps = shards_per_step
    grid = (t // tm, d // tn, g // sps)
    dep_args, dep_specs = _behind(behind)

    def body(a_ref, w_ref, *rest):
        o_ref = rest[-1]
        acc = _dot(a_ref[:, 0:f], w_ref[0], _NT)
        for s in range(1, sps):
            acc = acc + _dot(a_ref[:, s * f:(s + 1) * f], w_ref[s], _NT)

        @pl.when(pl.program_id(2) == 0)
        def _():
            o_ref[...] = acc

        @pl.when(pl.program_id(2) > 0)
        def _():
            o_ref[...] += acc

    return pl.pallas_call(
        body, name="dh", grid=grid,
        in_specs=[pl.BlockSpec((tm, sps * f), lambda m, j, kk: (m, kk)),
                  pl.BlockSpec((sps, tn, f), lambda m, j, kk: (kk, j, 0))] + dep_specs,
        out_specs=pl.BlockSpec((tm, tn), lambda m, j, kk: (m, j)),
        out_shape=jax.ShapeDtypeStruct((t, d), F32),
        compiler_params=_params(3),
    )(da, w1g, *dep_args)


def _du(dz, w_main, dzal, w_alow, tm=1024, tn=1024, tk=3072, behind=None, side=None):
    t, n = dz.shape
    tm = min(tm, t)
    d = w_main.shape[0]
    grid = (t // tm, d // tn, n // tk)
    dep_args, dep_specs = _behind(behind)
    adams = [_SideAdamW(s, grid) for s in (side or [])]
    n_dep, n_side = len(dep_args), len(adams)

    def body(a_ref, w_ref, al_ref, wa_ref, *rest):
        o_ref = rest[n_dep + 4 * n_side]
        part = _dot(a_ref[...], w_ref[...], _NT)

        @pl.when(pl.program_id(2) == 0)
        def _():
            o_ref[...] = _dot(al_ref[...], wa_ref[...], _NT) + part

        @pl.when(pl.program_id(2) > 0)
        def _():
            o_ref[...] += part
        for k, adam in enumerate(adams):
            first_out = n_dep + 4 * n_side + 1 + 4 * k
            adam.run(rest[n_dep + 4 * k:n_dep + 4 * k + 4], rest[first_out:first_out + 4])

    outs = pl.pallas_call(
        body, name="du", grid=grid,
        in_specs=[pl.BlockSpec((tm, tk), lambda m, j, kk: (m, kk)), pl.BlockSpec((tn, tk), lambda m, j, kk: (j, kk)),
                  pl.BlockSpec((tm, LANES), lambda m, j, kk: (m, 0)), pl.BlockSpec((tn, LANES), lambda m, j, kk: (j, 0))]
        + dep_specs + [s for adam in adams for s in adam.in_specs],
        out_specs=[pl.BlockSpec((tm, tn), lambda m, j, kk: (m, j))] + [s for adam in adams for s in adam.out_specs],
        out_shape=[jax.ShapeDtypeStruct((t, d), F32)] + [s for adam in adams for s in adam.out_shape],
        compiler_params=_params(3),
    )(dz, w_main, dzal, w_alow, *dep_args, *[a for adam in adams for a in adam.args])
    return outs[0], [outs[1 + 4 * k:5 + 4 * k] for k in range(n_side)]


def _loss_head(x2, gf, tgt, tr=512):
    t, d = x2.shape
    tr = min(tr, t)

    def body(x_ref, g_ref, t_ref, dx_ref, dxb_ref, loss_ref, dg_ref):
        @pl.when(pl.program_id(0) == 0)
        def _():
            loss_ref[...] = jnp.zeros_like(loss_ref)
            dg_ref[...] = jnp.zeros_like(dg_ref)

        xf = x_ref[...]
        g = g_ref[...]
        r = lax.rsqrt(jnp.mean(xf * xf, axis=-1, keepdims=True) + EPS)
        xh = xf * r
        e = xh * g - t_ref[...]
        loss_ref[...] += 0.5 * jnp.sum(jnp.mean(e * e, axis=-1, keepdims=True))
        dy = e * (1.0 / d)
        dg_ref[...] += jnp.sum(dy * xh, axis=0, keepdims=True)
        dyg = dy * g
        dx = r * (dyg - xh * jnp.mean(dyg * xh, axis=-1, keepdims=True))
        dx_ref[...] = dx
        dxb_ref[...] = dx.astype(BF16)

    return pl.pallas_call(
        body, name="loss_head", grid=(t // tr,),
        in_specs=[pl.BlockSpec((tr, d), lambda i: (i, 0)), pl.BlockSpec((1, d), lambda i: (0, 0)),
                  pl.BlockSpec((tr, d), lambda i: (i, 0))],
        out_specs=[pl.BlockSpec((tr, d), lambda i: (i, 0)), pl.BlockSpec((tr, d), lambda i: (i, 0)),
                   pl.BlockSpec((SUBLANES, LANES), lambda i: (0, 0)), pl.BlockSpec((1, d), lambda i: (0, 0))],
        out_shape=[jax.ShapeDtypeStruct((t, d), F32), jax.ShapeDtypeStruct((t, d), BF16),
                   jax.ShapeDtypeStruct((SUBLANES, LANES), F32), jax.ShapeDtypeStruct((1, d), F32)],
        compiler_params=_params(1),
    )(x2, gf, tgt)


def _norm_bwd_dy(dh, x1, g2, dx2, w_out, tm=256):
    t, d = x1.shape
    k = w_out.shape[0]
    tm = min(tm, t)

    def body(dh_ref, x_ref, g_ref, dr_ref, w_ref, dx_ref, dxb_ref, dg_ref, dy_ref):
        @pl.when(pl.program_id(0) == 0)
        def _():
            dg_ref[...] = jnp.zeros_like(dg_ref)

        xf = x_ref[...]
        dhv = dh_ref[...]
        r = lax.rsqrt(jnp.mean(xf * xf, axis=-1, keepdims=True) + EPS)
        xh = xf * r
        dg_ref[...] += jnp.sum(dhv * xh, axis=0, keepdims=True)
        dyg = dhv * g_ref[...]
        dx = dr_ref[...] + r * (dyg - xh * jnp.mean(dyg * xh, axis=-1, keepdims=True))
        dx_ref[...] = dx
        dxb = dx.astype(BF16)
        dxb_ref[...] = dxb
        dy_ref[...] = _dot(dxb, w_ref[...], _NT)

    rows = pl.BlockSpec((tm, d), lambda i: (i, 0))
    vec = pl.BlockSpec((1, d), lambda i: (0, 0))
    return pl.pallas_call(
        body, name="norm2_bwd_dy", grid=(t // tm,),
        in_specs=[rows, rows, vec, rows, pl.BlockSpec((k, d), lambda i: (0, 0))],
        out_specs=[rows, rows, vec, pl.BlockSpec((tm, k), lambda i: (i, 0))],
        out_shape=[jax.ShapeDtypeStruct((t, d), F32), jax.ShapeDtypeStruct((t, d), BF16),
                   jax.ShapeDtypeStruct((1, d), F32), jax.ShapeDtypeStruct((t, k), F32)],
        compiler_params=_params(1),
    )(dh, x1, g2, dx2, w_out)


def _norm_bwd(name, dh, xin, g, dres, tr=512):
    t, d = xin.shape
    tr = min(tr, t)

    def body(dh_ref, x_ref, g_ref, dr_ref, dx_ref, dg_ref):
        @pl.when(pl.program_id(0) == 0)
        def _():
            dg_ref[...] = jnp.zeros_like(dg_ref)

        xf = x_ref[...]
        dhv = dh_ref[...]
        r = lax.rsqrt(jnp.mean(xf * xf, axis=-1, keepdims=True) + EPS)
        xh = xf * r
        dg_ref[...] += jnp.sum(dhv * xh, axis=0, keepdims=True)
        dyg = dhv * g_ref[...]
        dx = dr_ref[...] + r * (dyg - xh * jnp.mean(dyg * xh, axis=-1, keepdims=True))
        dx_ref[...] = dx

    rows = pl.BlockSpec((tr, d), lambda i: (i, 0))
    vec = pl.BlockSpec((1, d), lambda i: (0, 0))
    return pl.pallas_call(
        body, name=name, grid=(t // tr,),
        in_specs=[rows, rows, vec, rows], out_specs=[rows, vec],
        out_shape=[jax.ShapeDtypeStruct((t, d), F32), jax.ShapeDtypeStruct((1, d), F32)],
        compiler_params=_params(1),
    )(dh, xin, g, dres)


MIX_TILE = 256
CHUNKS_PER_TILE = MIX_TILE // CHUNK
CHUNK_SHIFT = CHUNK.bit_length() - 1
assert 1 << CHUNK_SHIFT == CHUNK


def _chunk_masks(n):
    row = lax.broadcasted_iota(jnp.int32, (n, n), 0)
    col = lax.broadcasted_iota(jnp.int32, (n, n), 1)
    same = lax.shift_right_logical(row, CHUNK_SHIFT) == lax.shift_right_logical(col, CHUNK_SHIFT)
    one = lambda m: jnp.where(m, 1.0, 0.0).astype(BF16)
    return jnp.concatenate([one(same & (col > row)), one(same)], axis=0), one(same & (col < row))


def _mask_dot(mask, x):
    hi = x.astype(BF16)
    r1 = x - hi.astype(F32)
    mid = r1.astype(BF16)
    lo = (r1 - mid.astype(F32)).astype(BF16)
    return _dot(mask, hi) + _dot(mask, mid) + _dot(mask, lo)


def _log_sigmoid(x):
    return jnp.minimum(x, 0.0) - jnp.log1p(jnp.exp(-jnp.abs(x)))


def _conv_taps(prev8, uc, w):
    ext = jnp.concatenate([prev8, uc], axis=0)
    s1 = pltpu.roll(ext, 1, 0)[SUBLANES:]
    s2 = pltpu.roll(ext, 2, 0)[SUBLANES:]
    return s2 * w[0:1] + s1 * w[1:2] + uc * w[2:3], s1, s2


def _z_specs(tile, idx):
    d_conv = 1024
    wide = lambda c: pl.BlockSpec((tile, d_conv), lambda i, c=c: (idx(i), c))
    half = lambda c: pl.BlockSpec((tile, d_conv // 2), lambda i, c=c: (idx(i), c))
    return [wide(0), wide(1), wide(2), half(6), half(7), wide(4), wide(5)]


def _mixer_fwd(z, alow, wgu, b_gate, convw, conv_g, gla_g):
    t = z.shape[0]
    tb, cpt = MIX_TILE, CHUNKS_PER_TILE
    d_conv = conv_g.shape[1]
    dv = gla_g.shape[1]
    dk = dv // 2
    gw = d_conv // CONV_GROUPS
    scale = dk ** -0.5

    def body(cb_ref, cc_ref, ch_ref, q_ref, k_ref, v_ref, og_ref, al_ref, wgu_ref, bg_ref, cw_ref, cg_ref, gg_ref,
             y_ref, sall_ref, carry_ref, s_ref):
        @pl.when(pl.program_id(0) == 0)
        def _():
            carry_ref[...] = jnp.zeros_like(carry_ref)
            s_ref[...] = jnp.zeros_like(s_ref)

        uc = cc_ref[...] * ch_ref[...]
        conv, _, _ = _conv_taps(carry_ref[...], uc, cw_ref[...])
        carry_ref[...] = uc[tb - SUBLANES:]
        ypre = cb_ref[...] * conv
        cg = cg_ref[...]
        for g in range(CONV_GROUPS):
            sl = slice(g * gw, (g + 1) * gw)
            seg = ypre[:, sl]
            r = lax.rsqrt(jnp.mean(seg * seg, axis=-1, keepdims=True) + EPS)
            y_ref[:, sl] = (seg * r * cg[:, sl]).astype(BF16)

        later_and_same, _ = _chunk_masks(tb)
        pre = _dot(al_ref[...].astype(BF16), wgu_ref[...]) + bg_ref[...]
        la = _log_sigmoid(pre) * (1.0 / GATE_NORMALIZER)
        sums = _mask_dot(later_and_same, la)
        e_dec = sums[:tb]
        dec_all = jnp.exp(sums[tb:])
        kdec = (k_ref[...] * jnp.exp(e_dec)).astype(BF16)
        qs = (q_ref[...] * scale).astype(BF16)
        vb = v_ref[...].astype(BF16)
        gg = gg_ref[...]
        rows = [slice(c * CHUNK, (c + 1) * CHUNK) for c in range(cpt)]
        ks = [slice(h * dk, (h + 1) * dk) for h in range(GLA_HEADS)]
        vs = [slice(h * dv, (h + 1) * dv) for h in range(GLA_HEADS)]
        kvt = [[_dot(vb[rows[c], vs[h]], kdec[rows[c], ks[h]], _TN) for h in range(GLA_HEADS)] for c in range(cpt)]
        state = [s_ref[h] for h in range(GLA_HEADS)]
        states = []
        for c in range(cpt):
            state = [state[h] * dec_all[c * CHUNK:c * CHUNK + 1, ks[h]] + kvt[c][h] for h in range(GLA_HEADS)]
            states.append(state)
            for h in range(GLA_HEADS):
                sall_ref[c, h] = state[h]
        for h in range(GLA_HEADS):
            s_ref[h] = state[h]
        for h in range(GLA_HEADS):
            o = jnp.concatenate(
                [_dot(qs[rows[c], ks[h]], states[c][h].astype(BF16), _NT) for c in range(cpt)], axis=0)
            ro = lax.rsqrt(jnp.mean(o * o, axis=-1, keepdims=True) + EPS)
            ogs = og_ref[:, vs[h]]
            yg = o * ro * gg * (ogs * jax.nn.sigmoid(ogs))
            y_ref[:, d_conv + h * dv:d_conv + (h + 1) * dv] = yg.astype(BF16)

    full = lambda shape: pl.BlockSpec(shape, lambda i: (0,) * len(shape))
    return pl.pallas_call(
        body, name="mixer_fwd", grid=(t // tb,),
        in_specs=_z_specs(tb, lambda i: i) + [
            pl.BlockSpec((tb, LANES), lambda i: (i, 0)), full(wgu.shape), full(b_gate.shape), full(convw.shape),
            full(conv_g.shape), full(gla_g.shape)],
        out_specs=[pl.BlockSpec((tb, d_conv + GLA_HEADS * dv), lambda i: (i, 0)),
                   pl.BlockSpec((cpt, GLA_HEADS, dv, dk), lambda i: (i, 0, 0, 0))],
        out_shape=[jax.ShapeDtypeStruct((t, d_conv + GLA_HEADS * dv), BF16),
                   jax.ShapeDtypeStruct((t // CHUNK, GLA_HEADS, dv, dk), F32)],
        scratch_shapes=[pltpu.VMEM((SUBLANES, d_conv), F32), pltpu.VMEM((GLA_HEADS, dv, dk), F32)],
        compiler_params=_params(1),
    )(z, z, z, z, z, z, z, alow, wgu, b_gate, convw, conv_g, gla_g)


def _mixer_bwd(z, alow, dy, sall, wgu, b_gate, convw, conv_g, gla_g, behind=None):
    t = z.shape[0]
    tb, cpt = MIX_TILE, CHUNKS_PER_TILE
    nt = t // tb
    d_conv = conv_g.shape[1]
    dv = gla_g.shape[1]
    dk = dv // 2
    d_k = GLA_HEADS * dk
    gw = d_conv // CONV_GROUPS
    scale = dk ** -0.5
    rev = lambda i: nt - 1 - i
    dep_args, dep_specs = _behind(behind)

    def body(cb_ref, cc_ref, ch_ref, q_ref, k_ref, v_ref, og_ref, ccp_ref, chp_ref, al_ref, dy_ref, sall_ref, sprev_ref,
             wgu_ref, bg_ref, cw_ref, cg_ref, gg_ref, *rest):
        dz_ref, dzal_ref, dcw_ref, dcg_ref, dgg_ref, dbg_ref, dwgu_ref, dcarry_ref, gd_ref = rest[-9:]
        i = pl.program_id(0)

        @pl.when(i == 0)
        def _():
            dcarry_ref[...] = jnp.zeros_like(dcarry_ref)
            gd_ref[...] = jnp.zeros_like(gd_ref)
            dcw_ref[...] = jnp.zeros_like(dcw_ref)
            dcg_ref[...] = jnp.zeros_like(dcg_ref)
            dgg_ref[...] = jnp.zeros_like(dgg_ref)
            dbg_ref[...] = jnp.zeros_like(dbg_ref)
            dwgu_ref[...] = jnp.zeros_like(dwgu_ref)

        first = rev(i) == 0

        cb, cc, ch = cb_ref[...], cc_ref[...], ch_ref[...]
        w = cw_ref[...]
        uc = cc * ch
        prev8 = jnp.where(first, 0.0, ccp_ref[...] * chp_ref[...])
        conv, s1, s2 = _conv_taps(prev8, uc, w)
        ypre = cb * conv
        cg = cg_ref[...]
        dypre_parts = []
        for g in range(CONV_GROUPS):
            sl = slice(g * gw, (g + 1) * gw)
            seg = ypre[:, sl]
            r = lax.rsqrt(jnp.mean(seg * seg, axis=-1, keepdims=True) + EPS)
            yn = seg * r
            dyc = dy_ref[:, sl]
            dcg_ref[:, sl] += jnp.sum(dyc * yn, axis=0, keepdims=True)
            dyn = dyc * cg[:, sl]
            dypre_parts.append(r * (dyn - yn * jnp.mean(dyn * yn, axis=-1, keepdims=True)))
        dypre = jnp.concatenate(dypre_parts, axis=1)
        dconv = dypre * cb
        dz_ref[:, 0:d_conv] = (dypre * conv).astype(BF16)
        dcw_ref[0:1] += jnp.sum(dconv * s2, axis=0, keepdims=True)
        dcw_ref[1:2] += jnp.sum(dconv * s1, axis=0, keepdims=True)
        dcw_ref[2:3] += jnp.sum(dconv * uc, axis=0, keepdims=True)
        ext = jnp.concatenate([dconv, dcarry_ref[...]], axis=0)
        f1 = pltpu.roll(ext, tb + SUBLANES - 1, 0)[:tb]
        f2 = pltpu.roll(ext, tb + SUBLANES - 2, 0)[:tb]
        dcarry_ref[...] = dconv[:SUBLANES]
        duc = dconv * w[2:3] + f1 * w[1:2] + f2 * w[0:1]
        dz_ref[:, d_conv:2 * d_conv] = (duc * ch).astype(BF16)
        dz_ref[:, 2 * d_conv:3 * d_conv] = (duc * cc).astype(BF16)

        q_off = 3 * d_conv
        k_off = q_off + d_k
        v_off = k_off + d_k
        og_off = v_off + GLA_HEADS * dv
        later_and_same, earlier = _chunk_masks(tb)
        alb = al_ref[...].astype(BF16)
        pre = _dot(alb, wgu_ref[...]) + bg_ref[...]
        la = _log_sigmoid(pre) * (1.0 / GATE_NORMALIZER)
        decays = jnp.exp(_mask_dot(later_and_same, la))
        exp_e, dec_all = decays[:tb], decays[tb:]
        kdec = k_ref[...] * exp_e
        kdec_b = kdec.astype(BF16)
        qs = (q_ref[...] * scale).astype(BF16)
        vb = v_ref[...].astype(BF16)
        gg = gg_ref[...]
        rows = [slice(c * CHUNK, (c + 1) * CHUNK) for c in range(cpt)]
        ks = [slice(h * dk, (h + 1) * dk) for h in range(GLA_HEADS)]
        vs = [slice(h * dv, (h + 1) * dv) for h in range(GLA_HEADS)]
        st_b = [[sall_ref[c, h].astype(BF16) for h in range(GLA_HEADS)] for c in range(cpt)]
        do_b = []
        dgg = jnp.zeros_like(gg)
        for h in range(GLA_HEADS):
            o = jnp.concatenate([_dot(qs[rows[c], ks[h]], st_b[c][h], _NT) for c in range(cpt)], axis=0)
            ro = lax.rsqrt(jnp.mean(o * o, axis=-1, keepdims=True) + EPS)
            on = o * ro
            ogs = og_ref[:, vs[h]]
            sg = jax.nn.sigmoid(ogs)
            gate = ogs * sg
            dyg = dy_ref[:, d_conv + h * dv:d_conv + (h + 1) * dv]
            dgg = dgg + jnp.sum(dyg * on * gate, axis=0, keepdims=True)
            dz_ref[:, og_off + h * dv:og_off + (h + 1) * dv] = (
                dyg * on * gg * (sg * (1.0 + ogs * (1.0 - sg)))).astype(BF16)
            don = dyg * gg * gate
            do_b.append((ro * (don - on * jnp.mean(don * on, axis=-1, keepdims=True))).astype(BF16))
        dgg_ref[...] += dgg
        for h in range(GLA_HEADS):
            dq = jnp.concatenate([_dot(do_b[h][rows[c]], st_b[c][h]) for c in range(cpt)], axis=0)
            dz_ref[:, q_off + h * dk:q_off + (h + 1) * dk] = (dq * scale).astype(BF16)
        own = [[_dot(do_b[h][rows[c]], qs[rows[c], ks[h]], _TN) for h in range(GLA_HEADS)] for c in range(cpt)]
        carried = [gd_ref[h] for h in range(GLA_HEADS)]
        gt_b = [None] * cpt
        ddd = [None] * cpt
        for c in reversed(range(cpt)):
            gt = [own[c][h] + carried[h] for h in range(GLA_HEADS)]
            dec = [dec_all[c * CHUNK:c * CHUNK + 1, ks[h]] for h in range(GLA_HEADS)]
            carried = [gt[h] * dec[h] for h in range(GLA_HEADS)]
            if c > 0:
                st_prev = [sall_ref[c - 1, h] for h in range(GLA_HEADS)]
            else:
                st_prev = [jnp.where(first, 0.0, sprev_ref[0, h]) for h in range(GLA_HEADS)]
            ddec = [jnp.sum(gt[h] * st_prev[h], axis=0, keepdims=True) * dec[h] for h in range(GLA_HEADS)]
            ddd[c] = jnp.broadcast_to(jnp.concatenate(ddec, axis=1), (CHUNK, d_k))
            gt_b[c] = [gt[h].astype(BF16) for h in range(GLA_HEADS)]
        for h in range(GLA_HEADS):
            gd_ref[h] = carried[h]
        dkdec_cols = []
        for h in range(GLA_HEADS):
            dvh = jnp.concatenate([_dot(kdec_b[rows[c], ks[h]], gt_b[c][h], _NT) for c in range(cpt)], axis=0)
            dz_ref[:, v_off + h * dv:v_off + (h + 1) * dv] = dvh.astype(BF16)
            dkdec_cols.append(jnp.concatenate([_dot(vb[rows[c], vs[h]], gt_b[c][h]) for c in range(cpt)], axis=0))
        dkdec = jnp.concatenate(dkdec_cols, axis=1)
        dz_ref[:, k_off:k_off + d_k] = (dkdec * exp_e).astype(BF16)
        dla = _mask_dot(earlier, dkdec * kdec) + jnp.concatenate(ddd, axis=0)
        dpre = dla * (1.0 / GATE_NORMALIZER) * jax.nn.sigmoid(-pre)
        dbg_ref[...] += jnp.sum(dpre, axis=0, keepdims=True)
        dpre_b = dpre.astype(BF16)
        dwgu_ref[...] += _dot(alb, dpre_b, _TN)
        dzal_ref[...] = _dot(dpre_b, wgu_ref[...], _NT).astype(BF16)

    full = lambda shape: pl.BlockSpec(shape, lambda i: (0,) * len(shape))
    prev_rows = lambda c: pl.BlockSpec(
        (SUBLANES, d_conv), lambda i, c=c: (jnp.maximum(rev(i) * (tb // SUBLANES) - 1, 0), c))
    n_z = 3 * d_conv + 2 * d_k + 2 * GLA_HEADS * dv
    return pl.pallas_call(
        body, name="mixer_bwd", grid=(nt,),
        in_specs=_z_specs(tb, rev) + [
            prev_rows(1), prev_rows(2),
            pl.BlockSpec((tb, LANES), lambda i: (rev(i), 0)),
            pl.BlockSpec((tb, d_conv + GLA_HEADS * dv), lambda i: (rev(i), 0)),
            pl.BlockSpec((cpt, GLA_HEADS, dv, dk), lambda i: (rev(i), 0, 0, 0)),
            pl.BlockSpec((1, GLA_HEADS, dv, dk), lambda i: (jnp.maximum(rev(i) * cpt - 1, 0), 0, 0, 0)),
            full(wgu.shape), full(b_gate.shape), full(convw.shape), full(conv_g.shape), full(gla_g.shape)]
        + dep_specs,
        out_specs=[pl.BlockSpec((tb, n_z), lambda i: (rev(i), 0)), pl.BlockSpec((tb, LANES), lambda i: (rev(i), 0)),
                   full(convw.shape), full(conv_g.shape), full(gla_g.shape), full(b_gate.shape), full(wgu.shape)],
        out_shape=[jax.ShapeDtypeStruct((t, n_z), BF16), jax.ShapeDtypeStruct((t, LANES), BF16),
                   jax.ShapeDtypeStruct(convw.shape, F32), jax.ShapeDtypeStruct(conv_g.shape, F32),
                   jax.ShapeDtypeStruct(gla_g.shape, F32), jax.ShapeDtypeStruct(b_gate.shape, F32),
                   jax.ShapeDtypeStruct(wgu.shape, F32)],
        scratch_shapes=[pltpu.VMEM((SUBLANES, d_conv), F32), pltpu.VMEM((GLA_HEADS, dv, dk), F32)],
        compiler_params=_params(1),
    )(z, z, z, z, z, z, z, z, z, alow, dy, sall, sall, wgu, b_gate, convw, conv_g, gla_g, *dep_args)


def _adamw_math(g, w, m, v):
    m = ADAM_B1 * m + (1.0 - ADAM_B1) * g
    v = ADAM_B2 * v + (1.0 - ADAM_B2) * (g * g)
    m_hat = m / (1.0 - ADAM_B1 ** ADAM_STEP)
    v_hat = v / (1.0 - ADAM_B2 ** ADAM_STEP)
    delta = -ADAM_LR * (m_hat / (jnp.sqrt(v_hat) + ADAM_EPS) + ADAM_WD * w)
    return delta, m, v


def _adamw(name, parts, w, m, v, tr):
    r, c = w.shape
    n_parts = parts.shape[0]

    def body(p_ref, w_ref, m_ref, v_ref, g_ref, d_ref, nm_ref, nv_ref):
        g = p_ref[0].astype(F32)
        for j in range(1, n_parts):
            g = g + p_ref[j].astype(F32)
        g_ref[...] = g
        d_ref[...], nm_ref[...], nv_ref[...] = _adamw_math(g, w_ref[...], m_ref[...], v_ref[...])

    blk = pl.BlockSpec((tr, c), lambda i: (i, 0))
    return pl.pallas_call(
        body, name=name, grid=(r // tr,),
        in_specs=[pl.BlockSpec((n_parts, tr, c), lambda i: (0, i, 0)), blk, blk, blk],
        out_specs=[blk] * 4, out_shape=[jax.ShapeDtypeStruct((r, c), F32)] * 4,
        compiler_params=_params(1),
    )(parts, w, m, v)


def _adamw_small(grads, ws, ms, vs):
    n = len(grads)

    def body(*refs):
        g, w, m, v = (refs[k * n:(k + 1) * n] for k in range(4))
        d_out, m_out, v_out = (refs[(4 + k) * n:(5 + k) * n] for k in range(3))
        for i in range(n):
            d_out[i][...], m_out[i][...], v_out[i][...] = _adamw_math(g[i][...], w[i][...], m[i][...], v[i][...])

    vmem = pl.BlockSpec(memory_space=pltpu.VMEM)
    outs = pl.pallas_call(
        body, name="adamw_small", out_shape=[jax.ShapeDtypeStruct(w.shape, F32) for w in ws] * 3,
        in_specs=[vmem] * (4 * n), out_specs=[vmem] * (3 * n),
    )(*grads, *ws, *ms, *vs)
    return [outs[:n], outs[n:2 * n], outs[2 * n:]]


def _sum_partials(parts):
    n_parts, rows, lanes = parts.shape

    def body(p_ref, o_ref):
        g = p_ref[0]
        for j in range(1, n_parts):
            g = g + p_ref[j]
        o_ref[...] = g

    return pl.pallas_call(
        body, name="sum_small_partials", out_shape=jax.ShapeDtypeStruct((rows, lanes), F32),
        in_specs=[pl.BlockSpec(memory_space=pltpu.VMEM)], out_specs=pl.BlockSpec(memory_space=pltpu.VMEM),
    )(parts)


def _pack_rows(vectors, rows):
    flat = jnp.concatenate([a.reshape(-1).astype(F32) for a in vectors])
    return jnp.pad(flat, (0, rows * LANES - flat.shape[0])).reshape(rows, LANES)


def _unpack_rows(block, shapes):
    flat = block.reshape(-1)
    out, off = [], 0
    for s in shapes:
        n = 1
        for dim in s:
            n *= dim
        out.append(flat[off:off + n].reshape(s))
        off += n
    return out


def kernel(x, norm1_g, w_in, w_gate_up, b_gate, conv_w, conv_norm_g, gla_norm_g, w_out, norm2_g, w_ff1, w_ff2, norm_f_g, loss_target, m_norm1_g, m_w_in, m_w_gate_up, m_b_gate, m_conv_w, m_conv_norm_g, m_gla_norm_g, m_w_out, m_norm2_g, m_w_ff1, m_w_ff2, m_norm_f_g, v_norm1_g, v_w_in, v_w_gate_up, v_b_gate, v_conv_w, v_conv_norm_g, v_gla_norm_g, v_w_out, v_norm2_g, v_w_ff1, v_w_ff2, v_norm_f_g):
    me = _device_index()
    x2d, tgt = x[0], loss_target[0]
    d = x2d.shape[1]
    d_in_shard = w_in.shape[2]
    n_main = N_DEV * d_in_shard - GATE_RANK
    d_conv = conv_norm_g.shape[1]
    d_k = b_gate.shape[1]
    d_ff = N_DEV * w_ff1.shape[2]
    wmv = dict(
        norm1_g=(norm1_g, m_norm1_g, v_norm1_g), w_in=(w_in, m_w_in, v_w_in),
        w_gate_up=(w_gate_up, m_w_gate_up, v_w_gate_up), b_gate=(b_gate, m_b_gate, v_b_gate),
        conv_w=(conv_w, m_conv_w, v_conv_w), conv_norm_g=(conv_norm_g, m_conv_norm_g, v_conv_norm_g),
        gla_norm_g=(gla_norm_g, m_gla_norm_g, v_gla_norm_g), w_out=(w_out, m_w_out, v_w_out),
        norm2_g=(norm2_g, m_norm2_g, v_norm2_g), w_ff1=(w_ff1, m_w_ff1, v_w_ff1), w_ff2=(w_ff2, m_w_ff2, v_w_ff2),
        norm_f_g=(norm_f_g, m_norm_f_g, v_norm_f_g))

    small_rows = 16
    first_level = (SIBLING,) + SAME_CORE_PEERS
    win_shard = w_in[0].astype(BF16)
    in_send, in_recv, in_src, in_land, token = _exchange_start(
        "all_gather_start_w_in", [win_shard], [_land_zone(win_shard)], scatter=False, masks=[first_level])
    _, wgu_t, cw_t, wout_t, w1_t, w2_t = lax.optimization_barrier((token, w_gate_up, conv_w, w_out, w_ff1, w_ff2))
    small_shard = _pack_rows([wgu_t[0], cw_t[0]], small_rows)
    shards = [small_shard, wout_t[0].astype(BF16), w1_t[0].astype(BF16), w2_t[0].astype(BF16)]
    ag_send, ag_recv, ag_src, ag_land, token = _exchange_start(
        "all_gather_start", shards, [_land_zone(s) for s in shards], scatter=False, behind=token)

    def gathered(k, name, after):
        return _exchange_wait(name, ag_send[k], ag_recv[k], ag_src[k], ag_land[k], after, scatter=False)

    u = _rmsnorm(x2d, norm1_g, behind=token)
    tied = lax.optimization_barrier((token, w_in, m_w_in, v_w_in))
    wmv["w_in"] = tuple(tied[1:])
    small_g = gathered(0, "all_gather_wait_small", [u] + [a[0] for a in wmv["w_in"]])
    win_level1 = _exchange_wait(
        "all_gather_wait_w_in", in_send[0], in_recv[0], in_src[0], in_land[0], small_g, scatter=False,
        masks=first_level)
    win_g = _forward_wait("all_gather_wait_w_in_forwarded", *_forward_start("all_gather_forward_w_in", win_level1))
    w_main, w_alow = _shards_to_columns(win_g, n_main)
    small_flat = small_g.reshape(N_DEV, -1)
    n_wgu = GATE_RANK * (d_k // N_DEV)
    wgu_full = small_flat[:, :n_wgu].reshape(N_DEV, GATE_RANK, d_k // N_DEV).transpose(1, 0, 2).reshape(GATE_RANK, d_k)
    conv_w_full = small_flat[:, n_wgu:n_wgu + (d_conv // N_DEV) * CONV_WIDTH].reshape(d_conv, CONV_WIDTH)
    wgu_pad = jnp.pad(wgu_full, ((0, LANES - GATE_RANK), (0, 0))).astype(BF16)
    convw_taps = jnp.pad(conv_w_full.T, ((0, SUBLANES - CONV_WIDTH), (0, 0)))

    get_w_out = lambda after: gathered(1, "all_gather_wait_w_out", after).reshape(-1, d)
    get_w1 = lambda after: gathered(2, "all_gather_wait_w_ff1", after)
    get_w2 = lambda after: gathered(3, "all_gather_wait_w_ff2", after).reshape(d_ff, d)

    in_flight = {}

    def send_partials(name, parts):
        own = lax.dynamic_index_in_dim(parts, me, axis=0, keepdims=False)
        send, recv, src, land, token = _exchange_start("scatter_start_" + name, [parts], [_land_zone(own)], scatter=True)
        in_flight[name] = (send[0], recv[0], src[0], land[0])
        return token

    def on_grad(name, value):
        if name == "w_in":
            main, alow_part = value
            value = _columns_to_shards(main, alow_part, N_DEV, d_in_shard)
        elif name in ("w_out", "w_ff2"):
            value = value.reshape(N_DEV, -1, d)
        return send_partials(name, value)

    def received(name, after):
        send, recv, src, land = in_flight[name]
        return _exchange_wait("scatter_wait_" + name, send, recv, src, land, after, scatter=True)

    def side_for(name, after):
        return (received(name, after),) + tuple(a[0] for a in wmv[name])

    grads = _local_step(x2d, u, tgt, norm1_g, w_main, w_alow, wgu_pad, b_gate, convw_taps, conv_norm_g, gla_norm_g,
                        norm2_g, norm_f_g, get_w_out, get_w1, get_w2, on_grad, side_for)
    grad_x = grads["x"]

    small_shapes = [(1, d), (1, d_k), (1, d_conv), (1, gla_norm_g.shape[1]), (1, d), (d,),
                    (GATE_RANK, d_k), (d_conv, CONV_WIDTH), (1,)]
    small_grad_rows = 152
    small_part = _pack_rows(
        [grads["norm1_g"], grads["b_gate"], grads["conv_norm_g"], grads["gla_norm_g"], grads["norm2_g"],
         grads["norm_f_g"], grads["w_gate_up"][:GATE_RANK], grads["conv_w"][:CONV_WIDTH].T, grads["loss"][0, 0]],
        small_grad_rows)
    small_token = send_partials("small", jnp.broadcast_to(small_part[None], (N_DEV, small_grad_rows, LANES)))

    gin_r, gout_r = (received(nm, [grad_x, small_token]) for nm in ("w_in", "w_out"))
    get_small = lambda after: received("small", after)
    done = {"w_ff1": grads["adam_w_ff1"], "w_ff2": grads["adam_w_ff2"]}
    return _update(me, gin_r, gout_r, done, get_small, small_shapes, grad_x, wmv)


def _local_step(x2d, u, tgt, norm1_g, w_main, w_alow, wgu_pad, b_gate, convw_taps, conv_norm_g, gla_norm_g,
                norm2_g, norm_f_g, get_w_out, get_w1, get_w2, on_grad, side_for=lambda name, after: None):
    t, d = x2d.shape

    z, alow = _inproj(u, w_main, w_alow)
    y, sall = _mixer_fwd(z, alow, wgu_pad, b_gate, convw_taps, conv_norm_g, gla_norm_g)
    w_out_full = get_w_out(y)
    x1, h = _outproj(y, w_out_full, x2d, norm2_g)
    w1g = get_w1(h)
    a = _ff1(h, w1g)
    w2_full = get_w2(a)
    d_ff = w2_full.shape[0]
    x2 = _ff2(a, w2_full, x1)
    dx2, dx2b, loss_part, d_normf = _loss_head(x2, norm_f_g.reshape(1, d), tgt)

    tk = min(4096, t)
    nk = t // tk
    da = _dff2(dx2b, w2_full, a)
    dw2 = _tn_matmul(
        "dw_ff2", a, dx2b, (d_ff // 1024, d // 1024, nk),
        pl.BlockSpec((tk, 1024), lambda m, j, kk: (kk, m)), pl.BlockSpec((tk, 1024), lambda m, j, kk: (kk, j)),
        jax.ShapeDtypeStruct((d_ff, d), BF16), pl.BlockSpec((1024, 1024), lambda m, j, kk: (m, j)), (1024, 1024),
        a_fn=_relu_sq)
    token = on_grad("w_ff2", dw2)
    f_shard = d_ff // N_DEV
    dw1 = _tn_matmul(
        "dw_ff1", h, da, (N_DEV, d // 1024, nk),
        pl.BlockSpec((tk, 1024), lambda g, m, kk: (kk, m)), pl.BlockSpec((tk, f_shard), lambda g, m, kk: (kk, g)),
        jax.ShapeDtypeStruct((N_DEV, d, f_shard), BF16), pl.BlockSpec((None, 1024, f_shard), lambda g, m, kk: (g, m, 0)),
        (1024, f_shard), behind=token)
    token = on_grad("w_ff1", dw1)
    dh = _dh(da, w1g, behind=token)
    dx1, dx1b, d_norm2, dy = _norm_bwd_dy(dh, x1, norm2_g, dx2, w_out_full)
    dwout = _tn_matmul(
        "dw_out", y, dx1b, (d // 1024, d // 1024, nk),
        pl.BlockSpec((tk, 1024), lambda m, j, kk: (kk, m)), pl.BlockSpec((tk, 1024), lambda m, j, kk: (kk, j)),
        jax.ShapeDtypeStruct((d, d), BF16), pl.BlockSpec((1024, 1024), lambda m, j, kk: (m, j)), (1024, 1024))
    token = on_grad("w_out", dwout)
    dz, dzal, d_convw, d_convg, d_glag, d_bgate, d_wgu = _mixer_bwd(
        z, alow, dy, sall, wgu_pad, b_gate, convw_taps, conv_norm_g, gla_norm_g, behind=token)
    token = on_grad("w_in", _dw_in(u, dz, dzal, tk))
    sides = [s for s in (side_for("w_ff2", token), side_for("w_ff1", token)) if s is not None]
    du, adam = _du(dz, w_main, dzal, w_alow, behind=token, side=sides)
    adam_ff2, adam_ff1 = adam if adam else (None, None)
    grad_x, d_norm1 = _norm_bwd("norm1_bwd", du, x2d, norm1_g, dx1)
    return dict(x=grad_x, loss=loss_part, adam_w_ff2=adam_ff2, adam_w_ff1=adam_ff1,
                norm1_g=d_norm1, w_gate_up=d_wgu, b_gate=d_bgate, conv_w=d_convw,
                conv_norm_g=d_convg, gla_norm_g=d_glag, norm2_g=d_norm2, norm_f_g=d_normf)


_WEIGHT_ORDER = ("norm1_g", "w_in", "w_gate_up", "b_gate", "conv_w", "conv_norm_g", "gla_norm_g", "w_out", "norm2_g",
                 "w_ff1", "w_ff2", "norm_f_g")
_SMALL_ORDER = ("norm1_g", "b_gate", "conv_norm_g", "gla_norm_g", "norm2_g", "norm_f_g", "w_gate_up", "conv_w")
def _update(me, gin_r, gout_r, done, get_small, small_shapes, grad_x, wmv):
    big = dict(done)
    big["w_in"] = _adamw("adamw_w_in", gin_r, *(a[0] for a in wmv["w_in"]), 256)
    big["w_out"] = _adamw("adamw_w_out", gout_r, *(a[0] for a in wmv["w_out"]), 128)

    wgu_cols = wmv["w_gate_up"][0].shape[2]
    cw_rows = wmv["conv_w"][0].shape[1]

    small_r = get_small([big[nm][3] for nm in ("w_in", "w_out")])
    summed = _unpack_rows(_sum_partials(small_r), small_shapes)
    summed[6] = lax.dynamic_slice_in_dim(summed[6], me * wgu_cols, wgu_cols, axis=1)
    summed[7] = lax.dynamic_slice_in_dim(summed[7], me * cw_rows, cw_rows, axis=0)
    as_2d = lambda a: a.reshape((1, -1) if a.ndim == 1 else a.shape[-2:])
    grads_2d = [as_2d(g) for g in summed[:len(_SMALL_ORDER)]]
    small = _adamw_small(grads_2d, *[[as_2d(wmv[nm][k]) for nm in _SMALL_ORDER] for k in range(3)])
    small = [grads_2d] + small

    outs = []
    for k in range(4):
        for nm in _WEIGHT_ORDER:
            if nm in big:
                outs.append(big[nm][k][None])
            else:
                outs.append(small[k][_SMALL_ORDER.index(nm)].reshape(wmv[nm][0].shape))
    loss = summed[8][0]
    return (loss, grad_x[None], *outs)
```

```python
import jax
import jax.numpy as jnp
from jax import lax
from jax.experimental import pallas as pl
from jax.experimental.pallas import tpu as pltpu

F32 = jnp.float32
BF16 = jnp.bfloat16

N_DEV = 8
CHUNK = 64
GLA_HEADS = 4
CONV_GROUPS = 8
CONV_WIDTH = 3
GATE_RANK = 16
GATE_NORMALIZER = 16.0
EPS = 1e-6
ADAM_LR = 0.001
ADAM_B1 = 0.9
ADAM_B2 = 0.999
ADAM_EPS = 1e-08
ADAM_WD = 0.01
ADAM_STEP = 10

LANES = 128
SUBLANES = 8
VMEM_LIMIT = 56 << 20

_NN = (((1,), (0,)), ((), ()))
_NT = (((1,), (1,)), ((), ()))
_TN = (((0,), (0,)), ((), ()))


def _dot(a, b, dims=_NN):
    return lax.dot_general(a, b, dims, preferred_element_type=F32)


def _params(n_grid):
    return pltpu.CompilerParams(dimension_semantics=("arbitrary",) * n_grid, vmem_limit_bytes=VMEM_LIMIT)


def _relu_sq(a):
    r = jnp.maximum(a, 0.0)
    return r * r


def _device_index():
    return 4 * lax.axis_index("x") + 2 * lax.axis_index("y") + lax.axis_index("c")


def _peer(mask):
    x, y, c = lax.axis_index("x"), lax.axis_index("y"), lax.axis_index("c")
    return (x ^ ((mask >> 2) & 1), y ^ ((mask >> 1) & 1), c ^ (mask & 1))


_HBM_SPEC = pl.BlockSpec(memory_space=pltpu.HBM)
_SEM_SPEC = pl.BlockSpec(memory_space=pltpu.SEMAPHORE)
_SIDE_EFFECT = pltpu.SideEffectType.DATAFLOW_SIDE_EFFECTING
N_PEERS = N_DEV - 1


def _exchange_copy(src_ref, land_ref, send_sems, recv_sems, mask, scatter, arriving):
    me = _device_index()
    src = src_ref.at[me ^ mask] if scatter else src_ref
    dst = land_ref.at[(me ^ mask) if arriving else me]
    return pltpu.make_async_remote_copy(
        src_ref=src, dst_ref=dst, send_sem=send_sems.at[mask - 1], recv_sem=recv_sems.at[mask - 1],
        device_id=_peer(mask), device_id_type=pl.DeviceIdType.MESH)


def _land_zone(own):
    zone = lax.empty((N_DEV,) + own.shape, own.dtype)
    return lax.dynamic_update_slice(zone, own[None], (_device_index(),) + (0,) * own.ndim)


ALL_PEERS = tuple(range(1, N_DEV))
SIBLING = 1
SAME_CORE_PEERS = (2, 4, 6)


def _exchange_start(name, srcs, lands, scatter, masks=None, behind=None):
    n = len(srcs)
    masks = masks or [ALL_PEERS] * n
    dep_args = [] if behind is None else [behind]

    def body(*refs):
        src, land = refs[:n], refs[n:2 * n]
        outs = refs[2 * n + len(dep_args):]
        send_sems, recv_sems = outs[:n], outs[n:2 * n]
        token = refs[-1]
        for a in range(n):
            for mask in masks[a]:
                _exchange_copy(src[a], land[a], send_sems[a], recv_sems[a], mask, scatter, False).start()
        token[...] = jnp.zeros_like(token)

    hbm = lambda a: pltpu.HBM(a.shape, a.dtype)
    outs = pl.pallas_call(
        body, name=name,
        out_shape=([pltpu.SemaphoreType.DMA((N_PEERS,))] * (2 * n) + [hbm(a) for a in srcs] + [hbm(a) for a in lands]
                   + [jax.ShapeDtypeStruct((SUBLANES, LANES), F32)]),
        in_specs=[_HBM_SPEC] * (2 * n) + [pl.BlockSpec(memory_space=pl.ANY)] * len(dep_args),
        out_specs=[_SEM_SPEC] * (2 * n) + [_HBM_SPEC] * (2 * n) + [pl.BlockSpec(memory_space=pltpu.VMEM)],
        input_output_aliases={a: 2 * n + a for a in range(2 * n)},
        compiler_params=pltpu.CompilerParams(has_side_effects=_SIDE_EFFECT),
    )(*[pltpu.with_memory_space_constraint(a, pltpu.HBM) for a in list(srcs) + list(lands)], *dep_args)
    send_sems, recv_sems = outs[:n], outs[n:2 * n]
    src_thru, land_thru = outs[2 * n:3 * n], outs[3 * n:4 * n]
    return send_sems, recv_sems, src_thru, land_thru, outs[-1]


def _exchange_wait(name, send_sems, recv_sems, src_thru, land_thru, after, scatter, masks=ALL_PEERS):
    after = list(after) if isinstance(after, (list, tuple)) else [after]

    def body(src_ref, land_ref, send_ref, recv_ref, *rest):
        for mask in masks:
            cp = _exchange_copy(src_ref, land_ref, send_ref, recv_ref, mask, scatter, True)
            cp.wait_send()
            cp.wait_recv()

    return pl.pallas_call(
        body, name=name,
        out_shape=(pltpu.HBM(src_thru.shape, src_thru.dtype), pltpu.HBM(land_thru.shape, land_thru.dtype)),
        in_specs=[_HBM_SPEC, _HBM_SPEC, _SEM_SPEC, _SEM_SPEC] + [pl.BlockSpec(memory_space=pl.ANY)] * len(after),
        out_specs=(_HBM_SPEC, _HBM_SPEC), input_output_aliases={0: 0, 1: 1},
        compiler_params=pltpu.CompilerParams(has_side_effects=_SIDE_EFFECT),
    )(src_thru, land_thru, send_sems, recv_sems, *after)[1]


def _forward_copy(land_ref, send_sems, recv_sems, k, arriving):
    me = _device_index()
    slot = me ^ SAME_CORE_PEERS[k]
    return pltpu.make_async_remote_copy(
        src_ref=land_ref.at[slot], dst_ref=land_ref.at[(slot ^ SIBLING) if arriving else slot],
        send_sem=send_sems.at[k], recv_sem=recv_sems.at[k],
        device_id=_peer(SIBLING), device_id_type=pl.DeviceIdType.MESH)


def _forward_start(name, land):
    n_fwd = len(SAME_CORE_PEERS)

    def body(land_ref, send_sems, recv_sems, land_thru):
        for k in range(n_fwd):
            _forward_copy(land_ref, send_sems, recv_sems, k, False).start()

    send, recv, thru = pl.pallas_call(
        body, name=name,
        out_shape=[pltpu.SemaphoreType.DMA((n_fwd,)), pltpu.SemaphoreType.DMA((n_fwd,)), pltpu.HBM(land.shape, land.dtype)],
        in_specs=[_HBM_SPEC], out_specs=[_SEM_SPEC, _SEM_SPEC, _HBM_SPEC], input_output_aliases={0: 2},
        compiler_params=pltpu.CompilerParams(has_side_effects=_SIDE_EFFECT),
    )(pltpu.with_memory_space_constraint(land, pltpu.HBM))
    return send, recv, thru


def _forward_wait(name, send_sems, recv_sems, land_thru):
    def body(land_ref, send_ref, recv_ref, got_ref):
        for k in range(len(SAME_CORE_PEERS)):
            cp = _forward_copy(land_ref, send_ref, recv_ref, k, True)
            cp.wait_send()
            cp.wait_recv()

    return pl.pallas_call(
        body, name=name, out_shape=pltpu.HBM(land_thru.shape, land_thru.dtype),
        in_specs=[_HBM_SPEC, _SEM_SPEC, _SEM_SPEC], out_specs=_HBM_SPEC, input_output_aliases={0: 0},
        compiler_params=pltpu.CompilerParams(has_side_effects=_SIDE_EFFECT),
    )(land_thru, send_sems, recv_sems)


def _shards_to_columns(g, n_main, tr=512):
    n_dev, d, s = g.shape

    def body(g_ref, main_ref, rest_ref):
        for j in range(n_dev):
            lo, hi = j * s, (j + 1) * s
            if hi <= n_main:
                main_ref[:, lo:hi] = g_ref[j]
            else:
                main_ref[:, lo:n_main] = g_ref[j, :, 0:n_main - lo]
                rest_ref[...] = jnp.zeros_like(rest_ref)
                rest_ref[:, 0:hi - n_main] = g_ref[j, :, n_main - lo:s]

    return pl.pallas_call(
        body, grid=(d // tr,), name="shards_to_columns",
        in_specs=[pl.BlockSpec((n_dev, tr, s), lambda i: (0, i, 0))],
        out_specs=[pl.BlockSpec((tr, n_main), lambda i: (i, 0)), pl.BlockSpec((tr, LANES), lambda i: (i, 0))],
        out_shape=[jax.ShapeDtypeStruct((d, n_main), g.dtype), jax.ShapeDtypeStruct((d, LANES), g.dtype)],
        compiler_params=_params(1),
    )(g)


def _columns_to_shards(main, rest, n_dev, s, tr=512):
    d, n_main = main.shape
    assert (n_dev - 1) * s <= n_main < n_dev * s

    def body(main_ref, rest_ref, o_ref):
        for j in range(n_dev):
            lo, hi = j * s, (j + 1) * s
            if hi <= n_main:
                o_ref[j] = main_ref[:, lo:hi]
            else:
                o_ref[j, :, 0:n_main - lo] = main_ref[:, lo:n_main]
                o_ref[j, :, n_main - lo:s] = rest_ref[:, 0:hi - n_main]

    return pl.pallas_call(
        body, grid=(d // tr,), name="columns_to_shards",
        in_specs=[pl.BlockSpec((tr, n_main), lambda i: (i, 0)), pl.BlockSpec((tr, LANES), lambda i: (i, 0))],
        out_specs=pl.BlockSpec((n_dev, tr, s), lambda i: (0, i, 0)),
        out_shape=jax.ShapeDtypeStruct((n_dev, d, s), main.dtype),
        compiler_params=_params(1),
    )(main, rest)


def _rmsnorm(x, g, tr=512, behind=None):
    t, d = x.shape
    tr = min(tr, t)
    dep_args, dep_specs = _behind(behind)

    def body(x_ref, g_ref, *rest):
        u_ref = rest[-1]
        xf = x_ref[...]
        r = lax.rsqrt(jnp.mean(xf * xf, axis=-1, keepdims=True) + EPS)
        u_ref[...] = (xf * r * g_ref[...]).astype(BF16)

    return pl.pallas_call(
        body, name="rmsnorm1", grid=(t // tr,),
        in_specs=[pl.BlockSpec((tr, d), lambda i: (i, 0)), pl.BlockSpec((1, d), lambda i: (0, 0))] + dep_specs,
        out_specs=pl.BlockSpec((tr, d), lambda i: (i, 0)),
        out_shape=jax.ShapeDtypeStruct((t, d), BF16),
        compiler_params=_params(1),
    )(x, g, *dep_args)


def _inproj(u, w_main, w_alow, tm=1024, tn=1536):
    t, d = u.shape
    tm = min(tm, t)
    n = w_main.shape[1]

    def body(u_ref, w_ref, wa_ref, z_ref, al_ref):
        @pl.when(pl.program_id(1) == 0)
        def _():
            al_ref[...] = _dot(u_ref[...], wa_ref[...])

        z_ref[...] = _dot(u_ref[...], w_ref[...])

    return pl.pallas_call(
        body, name="inproj", grid=(t // tm, n // tn),
        in_specs=[pl.BlockSpec((tm, d), lambda m, j: (m, 0)), pl.BlockSpec((d, tn), lambda m, j: (0, j)),
                  pl.BlockSpec((d, LANES), lambda m, j: (0, 0))],
        out_specs=[pl.BlockSpec((tm, tn), lambda m, j: (m, j)), pl.BlockSpec((tm, LANES), lambda m, j: (m, 0))],
        out_shape=[jax.ShapeDtypeStruct((t, n), F32), jax.ShapeDtypeStruct((t, LANES), F32)],
        compiler_params=_params(2),
    )(u, w_main, w_alow)


def _outproj(y, w_out, x, g2, tm=512):
    t, d = x.shape
    tm = min(tm, t)
    k = y.shape[1]

    def body(y_ref, w_ref, x_ref, g_ref, x1_ref, h_ref):
        x1 = x_ref[...] + _dot(y_ref[...], w_ref[...])
        x1_ref[...] = x1
        r = lax.rsqrt(jnp.mean(x1 * x1, axis=-1, keepdims=True) + EPS)
        h_ref[...] = (x1 * r * g_ref[...]).astype(BF16)

    return pl.pallas_call(
        body, name="outproj_rmsnorm", grid=(t // tm,),
        in_specs=[pl.BlockSpec((tm, k), lambda m: (m, 0)), pl.BlockSpec((k, d), lambda m: (0, 0)),
                  pl.BlockSpec((tm, d), lambda m: (m, 0)), pl.BlockSpec((1, d), lambda m: (0, 0))],
        out_specs=[pl.BlockSpec((tm, d), lambda m: (m, 0)), pl.BlockSpec((tm, d), lambda m: (m, 0))],
        out_shape=[jax.ShapeDtypeStruct((t, d), F32), jax.ShapeDtypeStruct((t, d), BF16)],
        compiler_params=_params(1),
    )(y, w_out, x, g2)


def _ff1(h, w1g, tm=1024):
    t, d = h.shape
    tm = min(tm, t)
    g, _, f = w1g.shape

    def body(h_ref, w_ref, a_ref):
        a_ref[...] = _dot(h_ref[...], w_ref[...]).astype(BF16)

    return pl.pallas_call(
        body, name="ff1", grid=(t // tm, g),
        in_specs=[pl.BlockSpec((tm, d), lambda m, j: (m, 0)), pl.BlockSpec((None, d, f), lambda m, j: (j, 0, 0))],
        out_specs=pl.BlockSpec((tm, f), lambda m, j: (m, j)),
        out_shape=jax.ShapeDtypeStruct((t, g * f), BF16),
        compiler_params=_params(2),
    )(h, w1g)


def _ff2(a, w2, x1, tm=1024, tn=1024, tk=2048):
    t, f = a.shape
    tm = min(tm, t)
    d = w2.shape[1]

    def body(a_ref, w_ref, x1_ref, o_ref):
        @pl.when(pl.program_id(2) == 0)
        def _():
            o_ref[...] = x1_ref[...]

        o_ref[...] += _dot(_relu_sq(a_ref[...]), w_ref[...])

    return pl.pallas_call(
        body, name="ff2_residual", grid=(t // tm, d // tn, f // tk),
        in_specs=[pl.BlockSpec((tm, tk), lambda m, j, kk: (m, kk)), pl.BlockSpec((tk, tn), lambda m, j, kk: (kk, j)),
                  pl.BlockSpec((tm, tn), lambda m, j, kk: (m, j))],
        out_specs=pl.BlockSpec((tm, tn), lambda m, j, kk: (m, j)),
        out_shape=jax.ShapeDtypeStruct((t, d), F32),
        compiler_params=_params(3),
    )(a, w2, x1)


def _dff2(dx2b, w2, a, tm=1024, tn=1024):
    t, d = dx2b.shape
    tm = min(tm, t)
    f = w2.shape[0]

    def body(g_ref, w_ref, a_ref, o_ref):
        dp = _dot(g_ref[...], w_ref[...], _NT)
        o_ref[...] = (dp * (2.0 * jnp.maximum(a_ref[...].astype(F32), 0.0))).astype(BF16)

    return pl.pallas_call(
        body, name="dff2", grid=(t // tm, f // tn),
        in_specs=[pl.BlockSpec((tm, d), lambda m, j: (m, 0)), pl.BlockSpec((tn, d), lambda m, j: (j, 0)),
                  pl.BlockSpec((tm, tn), lambda m, j: (m, j))],
        out_specs=pl.BlockSpec((tm, tn), lambda m, j: (m, j)),
        out_shape=jax.ShapeDtypeStruct((t, f), BF16),
        compiler_params=_params(2),
    )(dx2b, w2, a)


def _behind(token):
    if token is None:
        return [], []
    return [token], [pl.BlockSpec(token.shape, lambda *_: (0,) * token.ndim)]


def _accumulate(kk, nk, acc_ref, part, out_ref):
    if nk == 1:
        out_ref[...] = part.astype(out_ref.dtype)
        return

    @pl.when(kk == 0)
    def _():
        acc_ref[...] = part

    if nk > 2:
        @pl.when((kk > 0) & (kk < nk - 1))
        def _():
            acc_ref[...] += part

    @pl.when(kk == nk - 1)
    def _():
        out_ref[...] = (acc_ref[...] + part).astype(out_ref.dtype)


def _tn_matmul(name, a, b, grid, a_spec, b_spec, out_shape, out_spec, acc_shape, a_fn=None, behind=None):
    nk = grid[-1]
    dep_args, dep_specs = _behind(behind)

    def body(a_ref, b_ref, *rest):
        o_ref, acc_ref = rest[-2:]
        kk = pl.program_id(len(grid) - 1)
        av = a_ref[...]
        if a_fn is not None:
            av = a_fn(av)
        _accumulate(kk, nk, acc_ref, _dot(av, b_ref[...], _TN), o_ref)

    return pl.pallas_call(
        body, name=name, grid=grid, in_specs=[a_spec, b_spec] + dep_specs, out_specs=out_spec, out_shape=out_shape,
        scratch_shapes=[pltpu.VMEM(acc_shape, F32)], compiler_params=_params(len(grid)),
    )(a, b, *dep_args)


def _dw_in(u, dz, dzal, tk, tm=1024, tn=1024):
    t, d = u.shape
    n_main = dz.shape[1]
    nk = t // tk

    def body(a_ref, b_ref, al_ref, o_ref, oal_ref, acc_ref, accal_ref):
        j, kk = pl.program_id(1), pl.program_id(2)
        av = a_ref[...]
        _accumulate(kk, nk, acc_ref, _dot(av, b_ref[...], _TN), o_ref)

        @pl.when(j == 0)
        def _():
            _accumulate(kk, nk, accal_ref, _dot(av, al_ref[...], _TN), oal_ref)

    return pl.pallas_call(
        body, name="dw_in", grid=(d // tm, n_main // tn, nk),
        in_specs=[pl.BlockSpec((tk, tm), lambda m, j, kk: (kk, m)), pl.BlockSpec((tk, tn), lambda m, j, kk: (kk, j)),
                  pl.BlockSpec((tk, LANES), lambda m, j, kk: (kk, 0))],
        out_specs=[pl.BlockSpec((tm, tn), lambda m, j, kk: (m, j)), pl.BlockSpec((tm, LANES), lambda m, j, kk: (m, 0))],
        out_shape=[jax.ShapeDtypeStruct((d, n_main), BF16), jax.ShapeDtypeStruct((d, LANES), BF16)],
        scratch_shapes=[pltpu.VMEM((tm, tn), F32), pltpu.VMEM((tm, LANES), F32)],
        compiler_params=_params(3),
    )(u, dz, dzal)


class _SideAdamW:
    def __init__(self, side, grid):
        parts, w, m, v = side
        n_parts, r, c = parts.shape
        steps = 1
        for extent in grid:
            steps *= extent
        rows = r // steps
        assert rows * steps == r and rows % (2 * SUBLANES) == 0, (r, steps)

        def step(*ids):
            lin = ids[0]
            for extent, idx in zip(grid[1:], ids[1:]):
                lin = lin * extent + idx
            return lin

        slab = pl.BlockSpec((rows, c), lambda *ids: (step(*ids), 0))
        self.args = [parts, w, m, v]
        self.in_specs = [pl.BlockSpec((n_parts, rows, c), lambda *ids: (0, step(*ids), 0)), slab, slab, slab]
        self.out_specs = [slab] * 4
        self.out_shape = [jax.ShapeDtypeStruct((r, c), F32)] * 4
        self.n_parts = n_parts

    def run(self, in_refs, out_refs):
        p_ref, w_ref, m_ref, v_ref = in_refs
        g = p_ref[0].astype(F32)
        for j in range(1, self.n_parts):
            g = g + p_ref[j].astype(F32)
        out_refs[0][...] = g
        out_refs[1][...], out_refs[2][...], out_refs[3][...] = _adamw_math(g, w_ref[...], m_ref[...], v_ref[...])


def _dh(da, w1g, tm=1024, tn=1024, shards_per_step=4, behind=None):
    t = da.shape[0]
    tm = min(tm, t)
    g, d, f = w1g.shape
    sps = shards_per_step
    grid = (t // tm, d // tn, g // sps)
    dep_args, dep_specs = _behind(behind)

    def body(a_ref, w_ref, *rest):
        o_ref = rest[-1]
        acc = _dot(a_ref[:, 0:f], w_ref[0], _NT)
        for s in range(1, sps):
            acc = acc + _dot(a_ref[:, s * f:(s + 1) * f], w_ref[s], _NT)

        @pl.when(pl.program_id(2) == 0)
        def _():
            o_ref[...] = acc

        @pl.when(pl.program_id(2) > 0)
        def _():
            o_ref[...] += acc

    return pl.pallas_call(
        body, name="dh", grid=grid,
        in_specs=[pl.BlockSpec((tm, sps * f), lambda m, j, kk: (m, kk)),
                  pl.BlockSpec((sps, tn, f), lambda m, j, kk: (kk, j, 0))] + dep_specs,
        out_specs=pl.BlockSpec((tm, tn), lambda m, j, kk: (m, j)),
        out_shape=jax.ShapeDtypeStruct((t, d), F32),
        compiler_params=_params(3),
    )(da, w1g, *dep_args)


def _du(dz, w_main, dzal, w_alow, tm=1024, tn=1024, tk=3072, behind=None, side=None):
    t, n = dz.shape
    tm = min(tm, t)
    d = w_main.shape[0]
    grid = (t // tm, d // tn, n // tk)
    dep_args, dep_specs = _behind(behind)
    adams = [_SideAdamW(s, grid) for s in (side or [])]
    n_dep, n_side = len(dep_args), len(adams)

    def body(a_ref, w_ref, al_ref, wa_ref, *rest):
        o_ref = rest[n_dep + 4 * n_side]
        part = _dot(a_ref[...], w_ref[...], _NT)

        @pl.when(pl.program_id(2) == 0)
        def _():
            o_ref[...] = _dot(al_ref[...], wa_ref[...], _NT) + part

        @pl.when(pl.program_id(2) > 0)
        def _():
            o_ref[...] += part
        for k, adam in enumerate(adams):
            first_out = n_dep + 4 * n_side + 1 + 4 * k
            adam.run(rest[n_dep + 4 * k:n_dep + 4 * k + 4], rest[first_out:first_out + 4])

    outs = pl.pallas_call(
        body, name="du", grid=grid,
        in_specs=[pl.BlockSpec((tm, tk), lambda m, j, kk: (m, kk)), pl.BlockSpec((tn, tk), lambda m, j, kk: (j, kk)),
                  pl.BlockSpec((tm, LANES), lambda m, j, kk: (m, 0)), pl.BlockSpec((tn, LANES), lambda m, j, kk: (j, 0))]
        + dep_specs + [s for adam in adams for s in adam.in_specs],
        out_specs=[pl.BlockSpec((tm, tn), lambda m, j, kk: (m, j))] + [s for adam in adams for s in adam.out_specs],
        out_shape=[jax.ShapeDtypeStruct((t, d), F32)] + [s for adam in adams for s in adam.out_shape],
        compiler_params=_params(3),
    )(dz, w_main, dzal, w_alow, *dep_args, *[a for adam in adams for a in adam.args])
    return outs[0], [outs[1 + 4 * k:5 + 4 * k] for k in range(n_side)]


def _loss_head(x2, gf, tgt, tr=512):
    t, d = x2.shape
    tr = min(tr, t)

    def body(x_ref, g_ref, t_ref, dx_ref, dxb_ref, loss_ref, dg_ref):
        @pl.when(pl.program_id(0) == 0)
        def _():
            loss_ref[...] = jnp.zeros_like(loss_ref)
            dg_ref[...] = jnp.zeros_like(dg_ref)

        xf = x_ref[...]
        g = g_ref[...]
        r = lax.rsqrt(jnp.mean(xf * xf, axis=-1, keepdims=True) + EPS)
        xh = xf * r
        e = xh * g - t_ref[...]
        loss_ref[...] += 0.5 * jnp.sum(jnp.mean(e * e, axis=-1, keepdims=True))
        dy = e * (1.0 / d)
        dg_ref[...] += jnp.sum(dy * xh, axis=0, keepdims=True)
        dyg = dy * g
        dx = r * (dyg - xh * jnp.mean(dyg * xh, axis=-1, keepdims=True))
        dx_ref[...] = dx
        dxb_ref[...] = dx.astype(BF16)

    return pl.pallas_call(
        body, name="loss_head", grid=(t // tr,),
        in_specs=[pl.BlockSpec((tr, d), lambda i: (i, 0)), pl.BlockSpec((1, d), lambda i: (0, 0)),
                  pl.BlockSpec((tr, d), lambda i: (i, 0))],
        out_specs=[pl.BlockSpec((tr, d), lambda i: (i, 0)), pl.BlockSpec((tr, d), lambda i: (i, 0)),
                   pl.BlockSpec((SUBLANES, LANES), lambda i: (0, 0)), pl.BlockSpec((1, d), lambda i: (0, 0))],
        out_shape=[jax.ShapeDtypeStruct((t, d), F32), jax.ShapeDtypeStruct((t, d), BF16),
                   jax.ShapeDtypeStruct((SUBLANES, LANES), F32), jax.ShapeDtypeStruct((1, d), F32)],
        compiler_params=_params(1),
    )(x2, gf, tgt)


def _norm_bwd_dy(dh, x1, g2, dx2, w_out, tm=256):
    t, d = x1.shape
    k = w_out.shape[0]
    tm = min(tm, t)

    def body(dh_ref, x_ref, g_ref, dr_ref, w_ref, dx_ref, dxb_ref, dg_ref, dy_ref):
        @pl.when(pl.program_id(0) == 0)
        def _():
            dg_ref[...] = jnp.zeros_like(dg_ref)

        xf = x_ref[...]
        dhv = dh_ref[...]
        r = lax.rsqrt(jnp.mean(xf * xf, axis=-1, keepdims=True) + EPS)
        xh = xf * r
        dg_ref[...] += jnp.sum(dhv * xh, axis=0, keepdims=True)
        dyg = dhv * g_ref[...]
        dx = dr_ref[...] + r * (dyg - xh * jnp.mean(dyg * xh, axis=-1, keepdims=True))
        dx_ref[...] = dx
        dxb = dx.astype(BF16)
        dxb_ref[...] = dxb
        dy_ref[...] = _dot(dxb, w_ref[...], _NT)

    rows = pl.BlockSpec((tm, d), lambda i: (i, 0))
    vec = pl.BlockSpec((1, d), lambda i: (0, 0))
    return pl.pallas_call(
        body, name="norm2_bwd_dy", grid=(t // tm,),
        in_specs=[rows, rows, vec, rows, pl.BlockSpec((k, d), lambda i: (0, 0))],
        out_specs=[rows, rows, vec, pl.BlockSpec((tm, k), lambda i: (i, 0))],
        out_shape=[jax.ShapeDtypeStruct((t, d), F32), jax.ShapeDtypeStruct((t, d), BF16),
                   jax.ShapeDtypeStruct((1, d), F32), jax.ShapeDtypeStruct((t, k), F32)],
        compiler_params=_params(1),
    )(dh, x1, g2, dx2, w_out)


def _norm_bwd(name, dh, xin, g, dres, tr=512):
    t, d = xin.shape
    tr = min(tr, t)

    def body(dh_ref, x_ref, g_ref, dr_ref, dx_ref, dg_ref):
        @pl.when(pl.program_id(0) == 0)
        def _():
            dg_ref[...] = jnp.zeros_like(dg_ref)

        xf = x_ref[...]
        dhv = dh_ref[...]
        r = lax.rsqrt(jnp.mean(xf * xf, axis=-1, keepdims=True) + EPS)
        xh = xf * r
        dg_ref[...] += jnp.sum(dhv * xh, axis=0, keepdims=True)
        dyg = dhv * g_ref[...]
        dx = dr_ref[...] + r * (dyg - xh * jnp.mean(dyg * xh, axis=-1, keepdims=True))
        dx_ref[...] = dx

    rows = pl.BlockSpec((tr, d), lambda i: (i, 0))
    vec = pl.BlockSpec((1, d), lambda i: (0, 0))
    return pl.pallas_call(
        body, name=name, grid=(t // tr,),
        in_specs=[rows, rows, vec, rows], out_specs=[rows, vec],
        out_shape=[jax.ShapeDtypeStruct((t, d), F32), jax.ShapeDtypeStruct((1, d), F32)],
        compiler_params=_params(1),
    )(dh, xin, g, dres)


MIX_TILE = 256
CHUNKS_PER_TILE = MIX_TILE // CHUNK
CHUNK_SHIFT = CHUNK.bit_length() - 1
assert 1 << CHUNK_SHIFT == CHUNK


def _chunk_masks(n):
    row = lax.broadcasted_iota(jnp.int32, (n, n), 0)
    col = lax.broadcasted_iota(jnp.int32, (n, n), 1)
    same = lax.shift_right_logical(row, CHUNK_SHIFT) == lax.shift_right_logical(col, CHUNK_SHIFT)
    one = lambda m: jnp.where(m, 1.0, 0.0).astype(BF16)
    return jnp.concatenate([one(same & (col > row)), one(same)], axis=0), one(same & (col < row))


def _mask_dot(mask, x):
    hi = x.astype(BF16)
    r1 = x - hi.astype(F32)
    mid = r1.astype(BF16)
    lo = (r1 - mid.astype(F32)).astype(BF16)
    return _dot(mask, hi) + _dot(mask, mid) + _dot(mask, lo)


def _log_sigmoid(x):
    return jnp.minimum(x, 0.0) - jnp.log1p(jnp.exp(-jnp.abs(x)))


def _conv_taps(prev8, uc, w):
    ext = jnp.concatenate([prev8, uc], axis=0)
    s1 = pltpu.roll(ext, 1, 0)[SUBLANES:]
    s2 = pltpu.roll(ext, 2, 0)[SUBLANES:]
    return s2 * w[0:1] + s1 * w[1:2] + uc * w[2:3], s1, s2


def _z_specs(tile, idx):
    d_conv = 1024
    wide = lambda c: pl.BlockSpec((tile, d_conv), lambda i, c=c: (idx(i), c))
    half = lambda c: pl.BlockSpec((tile, d_conv // 2), lambda i, c=c: (idx(i), c))
    return [wide(0), wide(1), wide(2), half(6), half(7), wide(4), wide(5)]


def _mixer_fwd(z, alow, wgu, b_gate, convw, conv_g, gla_g):
    t = z.shape[0]
    tb, cpt = MIX_TILE, CHUNKS_PER_TILE
    d_conv = conv_g.shape[1]
    dv = gla_g.shape[1]
    dk = dv // 2
    gw = d_conv // CONV_GROUPS
    scale = dk ** -0.5

    def body(cb_ref, cc_ref, ch_ref, q_ref, k_ref, v_ref, og_ref, al_ref, wgu_ref, bg_ref, cw_ref, cg_ref, gg_ref,
             y_ref, sall_ref, carry_ref, s_ref):
        @pl.when(pl.program_id(0) == 0)
        def _():
            carry_ref[...] = jnp.zeros_like(carry_ref)
            s_ref[...] = jnp.zeros_like(s_ref)

        uc = cc_ref[...] * ch_ref[...]
        conv, _, _ = _conv_taps(carry_ref[...], uc, cw_ref[...])
        carry_ref[...] = uc[tb - SUBLANES:]
        ypre = cb_ref[...] * conv
        cg = cg_ref[...]
        for g in range(CONV_GROUPS):
            sl = slice(g * gw, (g + 1) * gw)
            seg = ypre[:, sl]
            r = lax.rsqrt(jnp.mean(seg * seg, axis=-1, keepdims=True) + EPS)
            y_ref[:, sl] = (seg * r * cg[:, sl]).astype(BF16)

        later_and_same, _ = _chunk_masks(tb)
        pre = _dot(al_ref[...].astype(BF16), wgu_ref[...]) + bg_ref[...]
        la = _log_sigmoid(pre) * (1.0 / GATE_NORMALIZER)
        sums = _mask_dot(later_and_same, la)
        e_dec = sums[:tb]
        dec_all = jnp.exp(sums[tb:])
        kdec = (k_ref[...] * jnp.exp(e_dec)).astype(BF16)
        qs = (q_ref[...] * scale).astype(BF16)
        vb = v_ref[...].astype(BF16)
        gg = gg_ref[...]
        rows = [slice(c * CHUNK, (c + 1) * CHUNK) for c in range(cpt)]
        ks = [slice(h * dk, (h + 1) * dk) for h in range(GLA_HEADS)]
        vs = [slice(h * dv, (h + 1) * dv) for h in range(GLA_HEADS)]
        kvt = [[_dot(vb[rows[c], vs[h]], kdec[rows[c], ks[h]], _TN) for h in range(GLA_HEADS)] for c in range(cpt)]
        state = [s_ref[h] for h in range(GLA_HEADS)]
        states = []
        for c in range(cpt):
            state = [state[h] * dec_all[c * CHUNK:c * CHUNK + 1, ks[h]] + kvt[c][h] for h in range(GLA_HEADS)]
            states.append(state)
            for h in range(GLA_HEADS):
                sall_ref[c, h] = state[h]
        for h in range(GLA_HEADS):
            s_ref[h] = state[h]
        for h in range(GLA_HEADS):
            o = jnp.concatenate(
                [_dot(qs[rows[c], ks[h]], states[c][h].astype(BF16), _NT) for c in range(cpt)], axis=0)
            ro = lax.rsqrt(jnp.mean(o * o, axis=-1, keepdims=True) + EPS)
            ogs = og_ref[:, vs[h]]
            yg = o * ro * gg * (ogs * jax.nn.sigmoid(ogs))
            y_ref[:, d_conv + h * dv:d_conv + (h + 1) * dv] = yg.astype(BF16)

    full = lambda shape: pl.BlockSpec(shape, lambda i: (0,) * len(shape))
    return pl.pallas_call(
        body, name="mixer_fwd", grid=(t // tb,),
        in_specs=_z_specs(tb, lambda i: i) + [
            pl.BlockSpec((tb, LANES), lambda i: (i, 0)), full(wgu.shape), full(b_gate.shape), full(convw.shape),
            full(conv_g.shape), full(gla_g.shape)],
        out_specs=[pl.BlockSpec((tb, d_conv + GLA_HEADS * dv), lambda i: (i, 0)),
                   pl.BlockSpec((cpt, GLA_HEADS, dv, dk), lambda i: (i, 0, 0, 0))],
        out_shape=[jax.ShapeDtypeStruct((t, d_conv + GLA_HEADS * dv), BF16),
                   jax.ShapeDtypeStruct((t // CHUNK, GLA_HEADS, dv, dk), F32)],
        scratch_shapes=[pltpu.VMEM((SUBLANES, d_conv), F32), pltpu.VMEM((GLA_HEADS, dv, dk), F32)],
        compiler_params=_params(1),
    )(z, z, z, z, z, z, z, alow, wgu, b_gate, convw, conv_g, gla_g)


def _mixer_bwd(z, alow, dy, sall, wgu, b_gate, convw, conv_g, gla_g, behind=None):
    t = z.shape[0]
    tb, cpt = MIX_TILE, CHUNKS_PER_TILE
    nt = t // tb
    d_conv = conv_g.shape[1]
    dv = gla_g.shape[1]
    dk = dv // 2
    d_k = GLA_HEADS * dk
    gw = d_conv // CONV_GROUPS
    scale = dk ** -0.5
    rev = lambda i: nt - 1 - i
    dep_args, dep_specs = _behind(behind)

    def body(cb_ref, cc_ref, ch_ref, q_ref, k_ref, v_ref, og_ref, ccp_ref, chp_ref, al_ref, dy_ref, sall_ref, sprev_ref,
             wgu_ref, bg_ref, cw_ref, cg_ref, gg_ref, *rest):
        dz_ref, dzal_ref, dcw_ref, dcg_ref, dgg_ref, dbg_ref, dwgu_ref, dcarry_ref, gd_ref = rest[-9:]
        i = pl.program_id(0)

        @pl.when(i == 0)
        def _():
            dcarry_ref[...] = jnp.zeros_like(dcarry_ref)
            gd_ref[...] = jnp.zeros_like(gd_ref)
            dcw_ref[...] = jnp.zeros_like(dcw_ref)
            dcg_ref[...] = jnp.zeros_like(dcg_ref)
            dgg_ref[...] = jnp.zeros_like(dgg_ref)
            dbg_ref[...] = jnp.zeros_like(dbg_ref)
            dwgu_ref[...] = jnp.zeros_like(dwgu_ref)

        first = rev(i) == 0

        cb, cc, ch = cb_ref[...], cc_ref[...], ch_ref[...]
        w = cw_ref[...]
        uc = cc * ch
        prev8 = jnp.where(first, 0.0, ccp_ref[...] * chp_ref[...])
        conv, s1, s2 = _conv_taps(prev8, uc, w)
        ypre = cb * conv
        cg = cg_ref[...]
        dypre_parts = []
        for g in range(CONV_GROUPS):
            sl = slice(g * gw, (g + 1) * gw)
            seg = ypre[:, sl]
            r = lax.rsqrt(jnp.mean(seg * seg, axis=-1, keepdims=True) + EPS)
            yn = seg * r
            dyc = dy_ref[:, sl]
            dcg_ref[:, sl] += jnp.sum(dyc * yn, axis=0, keepdims=True)
            dyn = dyc * cg[:, sl]
            dypre_parts.append(r * (dyn - yn * jnp.mean(dyn * yn, axis=-1, keepdims=True)))
        dypre = jnp.concatenate(dypre_parts, axis=1)
        dconv = dypre * cb
        dz_ref[:, 0:d_conv] = (dypre * conv).astype(BF16)
        dcw_ref[0:1] += jnp.sum(dconv * s2, axis=0, keepdims=True)
        dcw_ref[1:2] += jnp.sum(dconv * s1, axis=0, keepdims=True)
        dcw_ref[2:3] += jnp.sum(dconv * uc, axis=0, keepdims=True)
        ext = jnp.concatenate([dconv, dcarry_ref[...]], axis=0)
        f1 = pltpu.roll(ext, tb + SUBLANES - 1, 0)[:tb]
        f2 = pltpu.roll(ext, tb + SUBLANES - 2, 0)[:tb]
        dcarry_ref[...] = dconv[:SUBLANES]
        duc = dconv * w[2:3] + f1 * w[1:2] + f2 * w[0:1]
        dz_ref[:, d_conv:2 * d_conv] = (duc * ch).astype(BF16)
        dz_ref[:, 2 * d_conv:3 * d_conv] = (duc * cc).astype(BF16)

        q_off = 3 * d_conv
        k_off = q_off + d_k
        v_off = k_off + d_k
        og_off = v_off + GLA_HEADS * dv
        later_and_same, earlier = _chunk_masks(tb)
        alb = al_ref[...].astype(BF16)
        pre = _dot(alb, wgu_ref[...]) + bg_ref[...]
        la = _log_sigmoid(pre) * (1.0 / GATE_NORMALIZER)
        decays = jnp.exp(_mask_dot(later_and_same, la))
        exp_e, dec_all = decays[:tb], decays[tb:]
        kdec = k_ref[...] * exp_e
        kdec_b = kdec.astype(BF16)
        qs = (q_ref[...] * scale).astype(BF16)
        vb = v_ref[...].astype(BF16)
        gg = gg_ref[...]
        rows = [slice(c * CHUNK, (c + 1) * CHUNK) for c in range(cpt)]
        ks = [slice(h * dk, (h + 1) * dk) for h in range(GLA_HEADS)]
        vs = [slice(h * dv, (h + 1) * dv) for h in range(GLA_HEADS)]
        st_b = [[sall_ref[c, h].astype(BF16) for h in range(GLA_HEADS)] for c in range(cpt)]
        do_b = []
        dgg = jnp.zeros_like(gg)
        for h in range(GLA_HEADS):
            o = jnp.concatenate([_dot(qs[rows[c], ks[h]], st_b[c][h], _NT) for c in range(cpt)], axis=0)
            ro = lax.rsqrt(jnp.mean(o * o, axis=-1, keepdims=True) + EPS)
            on = o * ro
            ogs = og_ref[:, vs[h]]
            sg = jax.nn.sigmoid(ogs)
            gate = ogs * sg
            dyg = dy_ref[:, d_conv + h * dv:d_conv + (h + 1) * dv]
            dgg = dgg + jnp.sum(dyg * on * gate, axis=0, keepdims=True)
            dz_ref[:, og_off + h * dv:og_off + (h + 1) * dv] = (
                dyg * on * gg * (sg * (1.0 + ogs * (1.0 - sg)))).astype(BF16)
            don = dyg * gg * gate
            do_b.append((ro * (don - on * jnp.mean(don * on, axis=-1, keepdims=True))).astype(BF16))
        dgg_ref[...] += dgg
        for h in range(GLA_HEADS):
            dq = jnp.concatenate([_dot(do_b[h][rows[c]], st_b[c][h]) for c in range(cpt)], axis=0)
            dz_ref[:, q_off + h * dk:q_off + (h + 1) * dk] = (dq * scale).astype(BF16)
        own = [[_dot(do_b[h][rows[c]], qs[rows[c], ks[h]], _TN) for h in range(GLA_HEADS)] for c in range(cpt)]
        carried = [gd_ref[h] for h in range(GLA_HEADS)]
        gt_b = [None] * cpt
        ddd = [None] * cpt
        for c in reversed(range(cpt)):
            gt = [own[c][h] + carried[h] for h in range(GLA_HEADS)]
            dec = [dec_all[c * CHUNK:c * CHUNK + 1, ks[h]] for h in range(GLA_HEADS)]
            carried = [gt[h] * dec[h] for h in range(GLA_HEADS)]
            if c > 0:
                st_prev = [sall_ref[c - 1, h] for h in range(GLA_HEADS)]
            else:
                st_prev = [jnp.where(first, 0.0, sprev_ref[0, h]) for h in range(GLA_HEADS)]
            ddec = [jnp.sum(gt[h] * st_prev[h], axis=0, keepdims=True) * dec[h] for h in range(GLA_HEADS)]
            ddd[c] = jnp.broadcast_to(jnp.concatenate(ddec, axis=1), (CHUNK, d_k))
            gt_b[c] = [gt[h].astype(BF16) for h in range(GLA_HEADS)]
        for h in range(GLA_HEADS):
            gd_ref[h] = carried[h]
        dkdec_cols = []
        for h in range(GLA_HEADS):
            dvh = jnp.concatenate([_dot(kdec_b[rows[c], ks[h]], gt_b[c][h], _NT) for c in range(cpt)], axis=0)
            dz_ref[:, v_off + h * dv:v_off + (h + 1) * dv] = dvh.astype(BF16)
            dkdec_cols.append(jnp.concatenate([_dot(vb[rows[c], vs[h]], gt_b[c][h]) for c in range(cpt)], axis=0))
        dkdec = jnp.concatenate(dkdec_cols, axis=1)
        dz_ref[:, k_off:k_off + d_k] = (dkdec * exp_e).astype(BF16)
        dla = _mask_dot(earlier, dkdec * kdec) + jnp.concatenate(ddd, axis=0)
        dpre = dla * (1.0 / GATE_NORMALIZER) * jax.nn.sigmoid(-pre)
        dbg_ref[...] += jnp.sum(dpre, axis=0, keepdims=True)
        dpre_b = dpre.astype(BF16)
        dwgu_ref[...] += _dot(alb, dpre_b, _TN)
        dzal_ref[...] = _dot(dpre_b, wgu_ref[...], _NT).astype(BF16)

    full = lambda shape: pl.BlockSpec(shape, lambda i: (0,) * len(shape))
    prev_rows = lambda c: pl.BlockSpec(
        (SUBLANES, d_conv), lambda i, c=c: (jnp.maximum(rev(i) * (tb // SUBLANES) - 1, 0), c))
    n_z = 3 * d_conv + 2 * d_k + 2 * GLA_HEADS * dv
    return pl.pallas_call(
        body, name="mixer_bwd", grid=(nt,),
        in_specs=_z_specs(tb, rev) + [
            prev_rows(1), prev_rows(2),
            pl.BlockSpec((tb, LANES), lambda i: (rev(i), 0)),
            pl.BlockSpec((tb, d_conv + GLA_HEADS * dv), lambda i: (rev(i), 0)),
            pl.BlockSpec((cpt, GLA_HEADS, dv, dk), lambda i: (rev(i), 0, 0, 0)),
            pl.BlockSpec((1, GLA_HEADS, dv, dk), lambda i: (jnp.maximum(rev(i) * cpt - 1, 0), 0, 0, 0)),
            full(wgu.shape), full(b_gate.shape), full(convw.shape), full(conv_g.shape), full(gla_g.shape)]
        + dep_specs,
        out_specs=[pl.BlockSpec((tb, n_z), lambda i: (rev(i), 0)), pl.BlockSpec((tb, LANES), lambda i: (rev(i), 0)),
                   full(convw.shape), full(conv_g.shape), full(gla_g.shape), full(b_gate.shape), full(wgu.shape)],
        out_shape=[jax.ShapeDtypeStruct((t, n_z), BF16), jax.ShapeDtypeStruct((t, LANES), BF16),
                   jax.ShapeDtypeStruct(convw.shape, F32), jax.ShapeDtypeStruct(conv_g.shape, F32),
                   jax.ShapeDtypeStruct(gla_g.shape, F32), jax.ShapeDtypeStruct(b_gate.shape, F32),
                   jax.ShapeDtypeStruct(wgu.shape, F32)],
        scratch_shapes=[pltpu.VMEM((SUBLANES, d_conv), F32), pltpu.VMEM((GLA_HEADS, dv, dk), F32)],
        compiler_params=_params(1),
    )(z, z, z, z, z, z, z, z, z, alow, dy, sall, sall, wgu, b_gate, convw, conv_g, gla_g, *dep_args)


def _adamw_math(g, w, m, v):
    m = ADAM_B1 * m + (1.0 - ADAM_B1) * g
    v = ADAM_B2 * v + (1.0 - ADAM_B2) * (g * g)
    m_hat = m / (1.0 - ADAM_B1 ** ADAM_STEP)
    v_hat = v / (1.0 - ADAM_B2 ** ADAM_STEP)
    delta = -ADAM_LR * (m_hat / (jnp.sqrt(v_hat) + ADAM_EPS) + ADAM_WD * w)
    return delta, m, v


def _adamw(name, parts, w, m, v, tr):
    r, c = w.shape
    n_parts = parts.shape[0]

    def body(p_ref, w_ref, m_ref, v_ref, g_ref, d_ref, nm_ref, nv_ref):
        g = p_ref[0].astype(F32)
        for j in range(1, n_parts):
            g = g + p_ref[j].astype(F32)
        g_ref[...] = g
        d_ref[...], nm_ref[...], nv_ref[...] = _adamw_math(g, w_ref[...], m_ref[...], v_ref[...])

    blk = pl.BlockSpec((tr, c), lambda i: (i, 0))
    return pl.pallas_call(
        body, name=name, grid=(r // tr,),
        in_specs=[pl.BlockSpec((n_parts, tr, c), lambda i: (0, i, 0)), blk, blk, blk],
        out_specs=[blk] * 4, out_shape=[jax.ShapeDtypeStruct((r, c), F32)] * 4,
        compiler_params=_params(1),
    )(parts, w, m, v)


def _adamw_small(grads, ws, ms, vs):
    n = len(grads)

    def body(*refs):
        g, w, m, v = (refs[k * n:(k + 1) * n] for k in range(4))
        d_out, m_out, v_out = (refs[(4 + k) * n:(5 + k) * n] for k in range(3))
        for i in range(n):
            d_out[i][...], m_out[i][...], v_out[i][...] = _adamw_math(g[i][...], w[i][...], m[i][...], v[i][...])

    vmem = pl.BlockSpec(memory_space=pltpu.VMEM)
    outs = pl.pallas_call(
        body, name="adamw_small", out_shape=[jax.ShapeDtypeStruct(w.shape, F32) for w in ws] * 3,
        in_specs=[vmem] * (4 * n), out_specs=[vmem] * (3 * n),
    )(*grads, *ws, *ms, *vs)
    return [outs[:n], outs[n:2 * n], outs[2 * n:]]


def _sum_partials(parts):
    n_parts, rows, lanes = parts.shape

    def body(p_ref, o_ref):
        g = p_ref[0]
        for j in range(1, n_parts):
            g = g + p_ref[j]
        o_ref[...] = g

    return pl.pallas_call(
        body, name="sum_small_partials", out_shape=jax.ShapeDtypeStruct((rows, lanes), F32),
        in_specs=[pl.BlockSpec(memory_space=pltpu.VMEM)], out_specs=pl.BlockSpec(memory_space=pltpu.VMEM),
    )(parts)


def _pack_rows(vectors, rows):
    flat = jnp.concatenate([a.reshape(-1).astype(F32) for a in vectors])
    return jnp.pad(flat, (0, rows * LANES - flat.shape[0])).reshape(rows, LANES)


def _unpack_rows(block, shapes):
    flat = block.reshape(-1)
    out, off = [], 0
    for s in shapes:
        n = 1
        for dim in s:
            n *= dim
        out.append(flat[off:off + n].reshape(s))
        off += n
    return out


def kernel(x, norm1_g, w_in, w_gate_up, b_gate, conv_w, conv_norm_g, gla_norm_g, w_out, norm2_g, w_ff1, w_ff2, norm_f_g, loss_target, m_norm1_g, m_w_in, m_w_gate_up, m_b_gate, m_conv_w, m_conv_norm_g, m_gla_norm_g, m_w_out, m_norm2_g, m_w_ff1, m_w_ff2, m_norm_f_g, v_norm1_g, v_w_in, v_w_gate_up, v_b_gate, v_conv_w, v_conv_norm_g, v_gla_norm_g, v_w_out, v_norm2_g, v_w_ff1, v_w_ff2, v_norm_f_g):
    me = _device_index()
    x2d, tgt = x[0], loss_target[0]
    d = x2d.shape[1]
    d_in_shard = w_in.shape[2]
    n_main = N_DEV * d_in_shard - GATE_RANK
    d_conv = conv_norm_g.shape[1]
    d_k = b_gate.shape[1]
    d_ff = N_DEV * w_ff1.shape[2]
    wmv = dict(
        norm1_g=(norm1_g, m_norm1_g, v_norm1_g), w_in=(w_in, m_w_in, v_w_in),
        w_gate_up=(w_gate_up, m_w_gate_up, v_w_gate_up), b_gate=(b_gate, m_b_gate, v_b_gate),
        conv_w=(conv_w, m_conv_w, v_conv_w), conv_norm_g=(conv_norm_g, m_conv_norm_g, v_conv_norm_g),
        gla_norm_g=(gla_norm_g, m_gla_norm_g, v_gla_norm_g), w_out=(w_out, m_w_out, v_w_out),
        norm2_g=(norm2_g, m_norm2_g, v_norm2_g), w_ff1=(w_ff1, m_w_ff1, v_w_ff1), w_ff2=(w_ff2, m_w_ff2, v_w_ff2),
        norm_f_g=(norm_f_g, m_norm_f_g, v_norm_f_g))

    small_rows = 16
    first_level = (SIBLING,) + SAME_CORE_PEERS
    win_shard = w_in[0].astype(BF16)
    in_send, in_recv, in_src, in_land, token = _exchange_start(
        "all_gather_start_w_in", [win_shard], [_land_zone(win_shard)], scatter=False, masks=[first_level])
    _, wgu_t, cw_t, wout_t, w1_t, w2_t = lax.optimization_barrier((token, w_gate_up, conv_w, w_out, w_ff1, w_ff2))
    small_shard = _pack_rows([wgu_t[0], cw_t[0]], small_rows)
    shards = [small_shard, wout_t[0].astype(BF16), w1_t[0].astype(BF16), w2_t[0].astype(BF16)]
    ag_send, ag_recv, ag_src, ag_land, token = _exchange_start(
        "all_gather_start", shards, [_land_zone(s) for s in shards], scatter=False, behind=token)

    def gathered(k, name, after):
        return _exchange_wait(name, ag_send[k], ag_recv[k], ag_src[k], ag_land[k], after, scatter=False)

    u = _rmsnorm(x2d, norm1_g, behind=token)
    tied = lax.optimization_barrier((token, w_in, m_w_in, v_w_in))
    wmv["w_in"] = tuple(tied[1:])
    small_g = gathered(0, "all_gather_wait_small", [u] + [a[0] for a in wmv["w_in"]])
    win_level1 = _exchange_wait(
        "all_gather_wait_w_in", in_send[0], in_recv[0], in_src[0], in_land[0], small_g, scatter=False,
        masks=first_level)
    win_g = _forward_wait("all_gather_wait_w_in_forwarded", *_forward_start("all_gather_forward_w_in", win_level1))
    w_main, w_alow = _shards_to_columns(win_g, n_main)
    small_flat = small_g.reshape(N_DEV, -1)
    n_wgu = GATE_RANK * (d_k // N_DEV)
    wgu_full = small_flat[:, :n_wgu].reshape(N_DEV, GATE_RANK, d_k // N_DEV).transpose(1, 0, 2).reshape(GATE_RANK, d_k)
    conv_w_full = small_flat[:, n_wgu:n_wgu + (d_conv // N_DEV) * CONV_WIDTH].reshape(d_conv, CONV_WIDTH)
    wgu_pad = jnp.pad(wgu_full, ((0, LANES - GATE_RANK), (0, 0))).astype(BF16)
    convw_taps = jnp.pad(conv_w_full.T, ((0, SUBLANES - CONV_WIDTH), (0, 0)))

    get_w_out = lambda after: gathered(1, "all_gather_wait_w_out", after).reshape(-1, d)
    get_w1 = lambda after: gathered(2, "all_gather_wait_w_ff1", after)
    get_w2 = lambda after: gathered(3, "all_gather_wait_w_ff2", after).reshape(d_ff, d)

    in_flight = {}

    def send_partials(name, parts):
        own = lax.dynamic_index_in_dim(parts, me, axis=0, keepdims=False)
        send, recv, src, land, token = _exchange_start("scatter_start_" + name, [parts], [_land_zone(own)], scatter=True)
        in_flight[name] = (send[0], recv[0], src[0], land[0])
        return token

    def on_grad(name, value):
        if name == "w_in":
            main, alow_part = value
            value = _columns_to_shards(main, alow_part, N_DEV, d_in_shard)
        elif name in ("w_out", "w_ff2"):
            value = value.reshape(N_DEV, -1, d)
        return send_partials(name, value)

    def received(name, after):
        send, recv, src, land = in_flight[name]
        return _exchange_wait("scatter_wait_" + name, send, recv, src, land, after, scatter=True)

    def side_for(name, after):
        return (received(name, after),) + tuple(a[0] for a in wmv[name])

    grads = _local_step(x2d, u, tgt, norm1_g, w_main, w_alow, wgu_pad, b_gate, convw_taps, conv_norm_g, gla_norm_g,
                        norm2_g, norm_f_g, get_w_out, get_w1, get_w2, on_grad, side_for)
    grad_x = grads["x"]

    small_shapes = [(1, d), (1, d_k), (1, d_conv), (1, gla_norm_g.shape[1]), (1, d), (d,),
                    (GATE_RANK, d_k), (d_conv, CONV_WIDTH), (1,)]
    small_grad_rows = 152
    small_part = _pack_rows(
        [grads["norm1_g"], grads["b_gate"], grads["conv_norm_g"], grads["gla_norm_g"], grads["norm2_g"],
         grads["norm_f_g"], grads["w_gate_up"][:GATE_RANK], grads["conv_w"][:CONV_WIDTH].T, grads["loss"][0, 0]],
        small_grad_rows)
    small_token = send_partials("small", jnp.broadcast_to(small_part[None], (N_DEV, small_grad_rows, LANES)))

    gin_r, gout_r = (received(nm, [grad_x, small_token]) for nm in ("w_in", "w_out"))
    get_small = lambda after: received("small", after)
    done = {"w_ff1": grads["adam_w_ff1"], "w_ff2": grads["adam_w_ff2"]}
    return _update(me, gin_r, gout_r, done, get_small, small_shapes, grad_x, wmv)


def _local_step(x2d, u, tgt, norm1_g, w_main, w_alow, wgu_pad, b_gate, convw_taps, conv_norm_g, gla_norm_g,
                norm2_g, norm_f_g, get_w_out, get_w1, get_w2, on_grad, side_for=lambda name, after: None):
    t, d = x2d.shape

    z, alow = _inproj(u, w_main, w_alow)
    y, sall = _mixer_fwd(z, alow, wgu_pad, b_gate, convw_taps, conv_norm_g, gla_norm_g)
    w_out_full = get_w_out(y)
    x1, h = _outproj(y, w_out_full, x2d, norm2_g)
    w1g = get_w1(h)
    a = _ff1(h, w1g)
    w2_full = get_w2(a)
    d_ff = w2_full.shape[0]
    x2 = _ff2(a, w2_full, x1)
    dx2, dx2b, loss_part, d_normf = _loss_head(x2, norm_f_g.reshape(1, d), tgt)

    tk = min(4096, t)
    nk = t // tk
    da = _dff2(dx2b, w2_full, a)
    dw2 = _tn_matmul(
        "dw_ff2", a, dx2b, (d_ff // 1024, d // 1024, nk),
        pl.BlockSpec((tk, 1024), lambda m, j, kk: (kk, m)), pl.BlockSpec((tk, 1024), lambda m, j, kk: (kk, j)),
        jax.ShapeDtypeStruct((d_ff, d), BF16), pl.BlockSpec((1024, 1024), lambda m, j, kk: (m, j)), (1024, 1024),
        a_fn=_relu_sq)
    token = on_grad("w_ff2", dw2)
    f_shard = d_ff // N_DEV
    dw1 = _tn_matmul(
        "dw_ff1", h, da, (N_DEV, d // 1024, nk),
        pl.BlockSpec((tk, 1024), lambda g, m, kk: (kk, m)), pl.BlockSpec((tk, f_shard), lambda g, m, kk: (kk, g)),
        jax.ShapeDtypeStruct((N_DEV, d, f_shard), BF16), pl.BlockSpec((None, 1024, f_shard), lambda g, m, kk: (g, m, 0)),
        (1024, f_shard), behind=token)
    token = on_grad("w_ff1", dw1)
    dh = _dh(da, w1g, behind=token)
    dx1, dx1b, d_norm2, dy = _norm_bwd_dy(dh, x1, norm2_g, dx2, w_out_full)
    dwout = _tn_matmul(
        "dw_out", y, dx1b, (d // 1024, d // 1024, nk),
        pl.BlockSpec((tk, 1024), lambda m, j, kk: (kk, m)), pl.BlockSpec((tk, 1024), lambda m, j, kk: (kk, j)),
        jax.ShapeDtypeStruct((d, d), BF16), pl.BlockSpec((1024, 1024), lambda m, j, kk: (m, j)), (1024, 1024))
    token = on_grad("w_out", dwout)
    dz, dzal, d_convw, d_convg, d_glag, d_bgate, d_wgu = _mixer_bwd(
        z, alow, dy, sall, wgu_pad, b_gate, convw_taps, conv_norm_g, gla_norm_g, behind=token)
    token = on_grad("w_in", _dw_in(u, dz, dzal, tk))
    sides = [s for s in (side_for("w_ff2", token), side_for("w_ff1", token)) if s is not None]
    du, adam = _du(dz, w_main, dzal, w_alow, behind=token, side=sides)
    adam_ff2, adam_ff1 = adam if adam else (None, None)
    grad_x, d_norm1 = _norm_bwd("norm1_bwd", du, x2d, norm1_g, dx1)
    return dict(x=grad_x, loss=loss_part, adam_w_ff2=adam_ff2, adam_w_ff1=adam_ff1,
                norm1_g=d_norm1, w_gate_up=d_wgu, b_gate=d_bgate, conv_w=d_convw,
                conv_norm_g=d_convg, gla_norm_g=d_glag, norm2_g=d_norm2, norm_f_g=d_normf)


_WEIGHT_ORDER = ("norm1_g", "w_in", "w_gate_up", "b_gate", "conv_w", "conv_norm_g", "gla_norm_g", "w_out", "norm2_g",
                 "w_ff1", "w_ff2", "norm_f_g")
_SMALL_ORDER = ("norm1_g", "b_gate", "conv_norm_g", "gla_norm_g", "norm2_g", "norm_f_g", "w_gate_up", "conv_w")
def _update(me, gin_r, gout_r, done, get_small, small_shapes, grad_x, wmv):
    big = dict(done)
    big["w_in"] = _adamw("adamw_w_in", gin_r, *(a[0] for a in wmv["w_in"]), 256)
    big["w_out"] = _adamw("adamw_w_out", gout_r, *(a[0] for a in wmv["w_out"]), 128)

    wgu_cols = wmv["w_gate_up"][0].shape[2]
    cw_rows = wmv["conv_w"][0].shape[1]

    small_r = get_small([big[nm][3] for nm in ("w_in", "w_out")])
    summed = _unpack_rows(_sum_partials(small_r), small_shapes)
    summed[6] = lax.dynamic_slice_in_dim(summed[6], me * wgu_cols, wgu_cols, axis=1)
    summed[7] = lax.dynamic_slice_in_dim(summed[7], me * cw_rows, cw_rows, axis=0)
    as_2d = lambda a: a.reshape((1, -1) if a.ndim == 1 else a.shape[-2:])
    grads_2d = [as_2d(g) for g in summed[:len(_SMALL_ORDER)]]
    small = _adamw_small(grads_2d, *[[as_2d(wmv[nm][k]) for nm in _SMALL_ORDER] for k in range(3)])
    small = [grads_2d] + small

    outs = []
    for k in range(4):
        for nm in _WEIGHT_ORDER:
            if nm in big:
                outs.append(big[nm][k][None])
            else:
                outs.append(small[k][_SMALL_ORDER.index(nm)].reshape(wmv[nm][0].shape))
    loss = summed[8][0]
    return (loss, grad_x[None], *outs)
```
